```python
import math
import jax, jax.numpy as jnp
from jax import lax
import numpy as np

D_MODEL = 1024
BATCH = 8
SEQ = 4096
DEPTH = 1

EPS = 1e-6
Q_BLOCK = 128
H_A = 8
QK_NOPE = 128
QK_ROPE = 64
V_DIM = 128
Q_LORA = 256
KV_LORA = 128
ROPE_THETA = 10000.0
H_B = 16
KV_B = 4
GROUP = H_B // KV_B
HD_B = 64
WINDOW = 128
NUM_BUCKETS = 32
MAX_DISTANCE = 128
D_FF = 2816
CONV_W = 3

W_IN_SIZES = (Q_LORA, KV_LORA + QK_ROPE, H_B * HD_B, KV_B * HD_B, KV_B * HD_B, D_MODEL, D_MODEL)
W_IN_COLS = sum(W_IN_SIZES)
W_IN_SPLITS = tuple(int(s) for s in np.cumsum(W_IN_SIZES)[:-1])

kernel_name = "hybrid_mla_swa_gated_convffn_encoder"


def rms_norm(x, g):
    xf = x.astype(jnp.float32)
    y = xf * lax.rsqrt(jnp.mean(xf * xf, axis=-1, keepdims=True) + EPS)
    return (y * g.astype(jnp.float32)).astype(x.dtype)


def apply_rope(x, positions):
    half = QK_ROPE // 2
    inv_freq = ROPE_THETA ** (-jnp.arange(half, dtype=jnp.float32) / half)
    ang = positions.astype(jnp.float32)[:, None] * inv_freq[None, :]
    cos = jnp.cos(ang)[None, :, None, :]
    sin = jnp.sin(ang)[None, :, None, :]
    xf = x.astype(jnp.float32)
    x1, x2 = xf[..., :half], xf[..., half:]
    out = jnp.concatenate([x1 * cos - x2 * sin, x2 * cos + x1 * sin], axis=-1)
    return out.astype(x.dtype)


def t5_bucket(rel):
    nb = NUM_BUCKETS // 2
    max_exact = nb // 2
    base = (rel > 0).astype(jnp.int32) * nb
    n = jnp.abs(rel)
    nf = jnp.maximum(n, 1).astype(jnp.float32)
    large = max_exact + (jnp.log(nf / max_exact) / math.log(MAX_DISTANCE / max_exact)
                         * (nb - max_exact)).astype(jnp.int32)
    large = jnp.minimum(large, nb - 1)
    return base + jnp.where(n < max_exact, n, large)


def mla_branch(q_lat, kv_lat, positions, q_a_norm_g, w_q_b, kv_a_norm_g, w_kv_b):
    B, S, _ = q_lat.shape
    q = (rms_norm(q_lat, q_a_norm_g) @ w_q_b).reshape(B, S, H_A, QK_NOPE + QK_ROPE)
    q = jnp.concatenate([q[..., :QK_NOPE], apply_rope(q[..., QK_NOPE:], positions)], axis=-1)
    c_kv, k_rope = kv_lat[..., :KV_LORA], kv_lat[..., KV_LORA:]
    kv = (rms_norm(c_kv, kv_a_norm_g) @ w_kv_b).reshape(B, S, H_A, QK_NOPE + V_DIM)
    k_nope, v = kv[..., :QK_NOPE], kv[..., QK_NOPE:]
    k_rope = apply_rope(k_rope[:, :, None, :], positions)
    k = jnp.concatenate([k_nope, jnp.broadcast_to(k_rope, (B, S, H_A, QK_ROPE))], axis=-1)
    scale = 1.0 / math.sqrt(QK_NOPE + QK_ROPE)
    nblk = S // Q_BLOCK
    qb = q.reshape(B, nblk, Q_BLOCK, H_A, QK_NOPE + QK_ROPE).transpose(1, 0, 2, 3, 4)

    def attend(q_blk):
        s = jnp.einsum('bqhd,bkhd->bhqk', q_blk, k, preferred_element_type=jnp.float32) * scale
        p = jax.nn.softmax(s, axis=-1).astype(v.dtype)
        return jnp.einsum('bhqk,bkhd->bqhd', p, v)

    o = lax.map(attend, qb)
    return o.transpose(1, 0, 2, 3, 4).reshape(B, S, H_A * V_DIM)


def window_branch(q, k, v, rel_bias, sinks):
    B, S, _ = q.shape
    q = q.reshape(B, S, KV_B, GROUP, HD_B)
    k = k.reshape(B, S, KV_B, HD_B)
    v = v.reshape(B, S, KV_B, HD_B)
    pad = ((0, 0), (WINDOW, WINDOW), (0, 0), (0, 0))
    kp = jnp.pad(k, pad)
    vp = jnp.pad(v, pad)
    span = Q_BLOCK + 2 * WINDOW
    a = jnp.arange(Q_BLOCK, dtype=jnp.int32)[:, None]
    c = jnp.arange(span, dtype=jnp.int32)[None, :]
    rel = c - WINDOW - a
    in_band = jnp.abs(rel) <= WINDOW
    bias = rel_bias[t5_bucket(rel)].astype(jnp.float32)
    bias = bias.transpose(2, 0, 1).reshape(KV_B, GROUP, Q_BLOCK, span)
    sink = sinks.astype(jnp.float32).reshape(1, KV_B, GROUP, 1, 1)
    scale = 1.0 / math.sqrt(HD_B)
    nblk = S // Q_BLOCK
    qb = q.reshape(B, nblk, Q_BLOCK, KV_B, GROUP, HD_B).transpose(1, 0, 2, 3, 4, 5)

    def attend(args):
        q_blk, n = args
        start = n * Q_BLOCK
        k_blk = lax.dynamic_slice_in_dim(kp, start, span, axis=1)
        v_blk = lax.dynamic_slice_in_dim(vp, start, span, axis=1)
        key_pos = start - WINDOW + c
        valid = in_band & (key_pos >= 0) & (key_pos < S)
        s = jnp.einsum('bqhgd,bkhd->bhgqk', q_blk, k_blk,
                       preferred_element_type=jnp.float32) * scale + bias
        s = jnp.where(valid, s, -1e30)
        sink_col = jnp.broadcast_to(sink, (B, KV_B, GROUP, Q_BLOCK, 1))
        p = jax.nn.softmax(jnp.concatenate([s, sink_col], axis=-1), axis=-1)[..., :span]
        return jnp.einsum('bhgqk,bkhd->bqhgd', p.astype(v_blk.dtype), v_blk)

    o = lax.map(attend, (qb, jnp.arange(nblk, dtype=jnp.int32)))
    return o.transpose(1, 0, 2, 3, 4, 5).reshape(B, S, H_B * HD_B)


def conv_ffn(h, w_up, conv_w, conv_b, w_down):
    u = h @ w_up
    up = jnp.pad(u, ((0, 0), (1, 1), (0, 0)))
    u = up[:, :-2] * conv_w[0] + up[:, 1:-1] * conv_w[1] + up[:, 2:] * conv_w[2] + conv_b
    g, val = u[..., :D_FF], u[..., D_FF:]
    return (jax.nn.silu(g) * val) @ w_down


def _fwd_setup_inputs(seed: int = 0) -> dict:
    key = jax.random.key(seed)
    ks = jax.random.split(key, 20)
    f32 = jnp.float32
    L = DEPTH

    def nrm(k, shape, scale):
        return jax.random.normal(k, shape, f32) * scale

    def gain(k, shape):
        return 1.0 + 0.05 * jax.random.normal(k, shape, f32)

    return {
        "x": jax.random.normal(ks[0], (BATCH, SEQ, D_MODEL), f32),
        "positions": jnp.arange(SEQ, dtype=jnp.int32),
        "norm1_g": gain(ks[1], (L, D_MODEL)),
        "w_in": nrm(ks[2], (L, D_MODEL, W_IN_COLS), D_MODEL ** -0.5),
        "q_a_norm_g": gain(ks[3], (L, Q_LORA)),
        "w_q_b": nrm(ks[4], (L, Q_LORA, H_A * (QK_NOPE + QK_ROPE)), Q_LORA ** -0.5),
        "kv_a_norm_g": gain(ks[5], (L, KV_LORA)),
        "w_kv_b": nrm(ks[6], (L, KV_LORA, H_A * (QK_NOPE + V_DIM)), KV_LORA ** -0.5),
        "rel_bias": nrm(ks[7], (NUM_BUCKETS, H_B), 0.5),
        "sinks": nrm(ks[8], (L, H_B), 0.5),
        "w_out": nrm(ks[9], (L, D_MODEL, D_MODEL), D_MODEL ** -0.5),
        "norm2_g": gain(ks[10], (L, D_MODEL)),
        "w_up": nrm(ks[11], (L, D_MODEL, 2 * D_FF), D_MODEL ** -0.5),
        "conv_w": nrm(ks[12], (L, CONV_W, 2 * D_FF), CONV_W ** -0.5),
        "conv_b": nrm(ks[13], (L, 2 * D_FF), 0.02),
        "w_down": nrm(ks[14], (L, D_FF, D_MODEL), D_FF ** -0.5),
        "final_norm_g": gain(ks[15], (D_MODEL,)),
    }


def _fwd_reference(x, positions, norm1_g, w_in, q_a_norm_g, w_q_b, kv_a_norm_g, w_kv_b, rel_bias,
              sinks, w_out, norm2_g, w_up, conv_w, conv_b, w_down, final_norm_g):
    for l in range(DEPTH):
        h = rms_norm(x, norm1_g[l])
        proj = h @ w_in[l]
        q_lat, kv_lat, q_b, k_b, v_b, gate_a, gate_b = jnp.split(proj, W_IN_SPLITS, axis=-1)
        o_a = mla_branch(q_lat, kv_lat, positions, q_a_norm_g[l], w_q_b[l],
                         kv_a_norm_g[l], w_kv_b[l])
        o_b = window_branch(q_b, k_b, v_b, rel_bias, sinks[l])
        mixed = jax.nn.sigmoid(gate_a) * o_a + jax.nn.sigmoid(gate_b) * o_b
        x = x + mixed @ w_out[l]
        x = x + conv_ffn(rms_norm(x, norm2_g[l]), w_up[l], conv_w[l], conv_b[l], w_down[l])
    return rms_norm(x, final_norm_g)


import jax as _jax
import jax.numpy as _jnp

TWIN_FORMAT = 'train_step'
FWD_PARAMS = ['x', 'positions', 'norm1_g', 'w_in', 'q_a_norm_g', 'w_q_b', 'kv_a_norm_g', 'w_kv_b', 'rel_bias', 'sinks', 'w_out', 'norm2_g', 'w_up', 'conv_w', 'conv_b', 'w_down', 'final_norm_g']
TWIN_WEIGHTS = ['norm1_g', 'w_in', 'q_a_norm_g', 'w_q_b', 'kv_a_norm_g', 'w_kv_b', 'rel_bias', 'sinks', 'w_out', 'norm2_g', 'w_up', 'conv_w', 'conv_b', 'w_down', 'final_norm_g']
TWIN_DIFF_INPUT = 'x'
TWIN_INPUTS = ['x', 'positions', 'norm1_g', 'w_in', 'q_a_norm_g', 'w_q_b', 'kv_a_norm_g', 'w_kv_b', 'rel_bias', 'sinks', 'w_out', 'norm2_g', 'w_up', 'conv_w', 'conv_b', 'w_down', 'final_norm_g', 'loss_target', 'm_norm1_g', 'm_w_in', 'm_q_a_norm_g', 'm_w_q_b', 'm_kv_a_norm_g', 'm_w_kv_b', 'm_rel_bias', 'm_sinks', 'm_w_out', 'm_norm2_g', 'm_w_up', 'm_conv_w', 'm_conv_b', 'm_w_down', 'm_final_norm_g', 'v_norm1_g', 'v_w_in', 'v_q_a_norm_g', 'v_w_q_b', 'v_kv_a_norm_g', 'v_w_kv_b', 'v_rel_bias', 'v_sinks', 'v_w_out', 'v_norm2_g', 'v_w_up', 'v_conv_w', 'v_conv_b', 'v_w_down', 'v_final_norm_g']
TWIN_OUTPUTS = ['loss', 'grad_x', 'grad_norm1_g', 'grad_w_in', 'grad_q_a_norm_g', 'grad_w_q_b', 'grad_kv_a_norm_g', 'grad_w_kv_b', 'grad_rel_bias', 'grad_sinks', 'grad_w_out', 'grad_norm2_g', 'grad_w_up', 'grad_conv_w', 'grad_conv_b', 'grad_w_down', 'grad_final_norm_g', 'delta_norm1_g', 'delta_w_in', 'delta_q_a_norm_g', 'delta_w_q_b', 'delta_kv_a_norm_g', 'delta_w_kv_b', 'delta_rel_bias', 'delta_sinks', 'delta_w_out', 'delta_norm2_g', 'delta_w_up', 'delta_conv_w', 'delta_conv_b', 'delta_w_down', 'delta_final_norm_g', 'new_m_norm1_g', 'new_m_w_in', 'new_m_q_a_norm_g', 'new_m_w_q_b', 'new_m_kv_a_norm_g', 'new_m_w_kv_b', 'new_m_rel_bias', 'new_m_sinks', 'new_m_w_out', 'new_m_norm2_g', 'new_m_w_up', 'new_m_conv_w', 'new_m_conv_b', 'new_m_w_down', 'new_m_final_norm_g', 'new_v_norm1_g', 'new_v_w_in', 'new_v_q_a_norm_g', 'new_v_w_q_b', 'new_v_kv_a_norm_g', 'new_v_w_kv_b', 'new_v_rel_bias', 'new_v_sinks', 'new_v_w_out', 'new_v_norm2_g', 'new_v_w_up', 'new_v_conv_w', 'new_v_conv_b', 'new_v_w_down', 'new_v_final_norm_g']
TWIN_LEAF_KINDS = {'loss': 'loss', 'grad_x': 'grad_x', 'grad_norm1_g': 'grad_w', 'grad_w_in': 'grad_w', 'grad_q_a_norm_g': 'grad_w', 'grad_w_q_b': 'grad_w', 'grad_kv_a_norm_g': 'grad_w', 'grad_w_kv_b': 'grad_w', 'grad_rel_bias': 'grad_w', 'grad_sinks': 'grad_w', 'grad_w_out': 'grad_w', 'grad_norm2_g': 'grad_w', 'grad_w_up': 'grad_w', 'grad_conv_w': 'grad_w', 'grad_conv_b': 'grad_w', 'grad_w_down': 'grad_w', 'grad_final_norm_g': 'grad_w', 'delta_norm1_g': 'delta_w', 'delta_w_in': 'delta_w', 'delta_q_a_norm_g': 'delta_w', 'delta_w_q_b': 'delta_w', 'delta_kv_a_norm_g': 'delta_w', 'delta_w_kv_b': 'delta_w', 'delta_rel_bias': 'delta_w', 'delta_sinks': 'delta_w', 'delta_w_out': 'delta_w', 'delta_norm2_g': 'delta_w', 'delta_w_up': 'delta_w', 'delta_conv_w': 'delta_w', 'delta_conv_b': 'delta_w', 'delta_w_down': 'delta_w', 'delta_final_norm_g': 'delta_w', 'new_m_norm1_g': 'new_m', 'new_m_w_in': 'new_m', 'new_m_q_a_norm_g': 'new_m', 'new_m_w_q_b': 'new_m', 'new_m_kv_a_norm_g': 'new_m', 'new_m_w_kv_b': 'new_m', 'new_m_rel_bias': 'new_m', 'new_m_sinks': 'new_m', 'new_m_w_out': 'new_m', 'new_m_norm2_g': 'new_m', 'new_m_w_up': 'new_m', 'new_m_conv_w': 'new_m', 'new_m_conv_b': 'new_m', 'new_m_w_down': 'new_m', 'new_m_final_norm_g': 'new_m', 'new_v_norm1_g': 'new_v', 'new_v_w_in': 'new_v', 'new_v_q_a_norm_g': 'new_v', 'new_v_w_q_b': 'new_v', 'new_v_kv_a_norm_g': 'new_v', 'new_v_w_kv_b': 'new_v', 'new_v_rel_bias': 'new_v', 'new_v_sinks': 'new_v', 'new_v_w_out': 'new_v', 'new_v_norm2_g': 'new_v', 'new_v_w_up': 'new_v', 'new_v_conv_w': 'new_v', 'new_v_conv_b': 'new_v', 'new_v_w_down': 'new_v', 'new_v_final_norm_g': 'new_v'}


def _forward(args):
    return _fwd_reference(*[args[k] for k in FWD_PARAMS])


def _output_shape():
    out = _jax.eval_shape(lambda: _forward(_fwd_setup_inputs(0)))
    return out.shape, out.dtype

N_MICROBATCH = 1
ADAM_LR = 0.001
ADAM_B1 = 0.9
ADAM_B2 = 0.999
ADAM_EPS = 1e-08
ADAM_WD = 0.01
ADAM_STEP = 10
PER_EXAMPLE_BATCH_AXIS = {'x': 0, 'loss_target': 0}
SHARED_INPUTS = ['positions']
_WEIGHT_DTYPES = {'norm1_g': _jnp.float32, 'w_in': _jnp.float32, 'q_a_norm_g': _jnp.float32, 'w_q_b': _jnp.float32, 'kv_a_norm_g': _jnp.float32, 'w_kv_b': _jnp.float32, 'rel_bias': _jnp.float32, 'sinks': _jnp.float32, 'w_out': _jnp.float32, 'norm2_g': _jnp.float32, 'w_up': _jnp.float32, 'conv_w': _jnp.float32, 'conv_b': _jnp.float32, 'w_down': _jnp.float32, 'final_norm_g': _jnp.float32}
MOMENT_SCALE = {'norm1_g': 3.358238e-02, 'w_in': 1.680411e-02, 'q_a_norm_g': 2.525301e-02, 'w_q_b': 1.109501e-02, 'kv_a_norm_g': 5.847186e-02, 'w_kv_b': 1.296965e-02, 'rel_bias': 1.736799e-02, 'sinks': 3.047354e-04, 'w_out': 1.999219e-02, 'norm2_g': 1.352390e-01, 'w_up': 5.660017e-02, 'conv_w': 5.706658e-02, 'conv_b': 5.864864e-02, 'w_down': 9.385283e-02, 'final_norm_g': 3.197066e+01}


def _to_microbatches(a, axis):
    t = _jnp.moveaxis(a, axis, 0)
    t = t.reshape((N_MICROBATCH, t.shape[0] // N_MICROBATCH) + t.shape[1:])
    return _jnp.moveaxis(t, 1, axis + 1)


def setup_inputs(seed: int = 0) -> dict:
    inp = _fwd_setup_inputs(seed)
    key = _jax.random.fold_in(_jax.random.key(seed), 7919)
    shape, _ = _output_shape()
    out = dict(inp)
    out["loss_target"] = _jax.random.normal(_jax.random.fold_in(key, 0), shape, _jnp.float32)
    for i, name in enumerate(TWIN_WEIGHTS):
        w = inp[name].astype(_jnp.float32)
        if MOMENT_SCALE is None:
            s = _jnp.sqrt(_jnp.mean(_jnp.square(w)) + 1e-30)
        else:
            s = MOMENT_SCALE[name]
        km, kv = _jax.random.split(_jax.random.fold_in(key, i + 1))
        out[name] = w
        out["m_" + name] = s * _jax.random.normal(km, w.shape, _jnp.float32)
        out["v_" + name] = (s * s) * _jax.random.uniform(kv, w.shape, _jnp.float32, 0.5, 1.5)
    if N_MICROBATCH > 1:
        for name, axis in PER_EXAMPLE_BATCH_AXIS.items():
            out[name] = _to_microbatches(out[name], axis)
    return {'x': out['x'], 'positions': out['positions'], 'norm1_g': out['norm1_g'], 'w_in': out['w_in'], 'q_a_norm_g': out['q_a_norm_g'], 'w_q_b': out['w_q_b'], 'kv_a_norm_g': out['kv_a_norm_g'], 'w_kv_b': out['w_kv_b'], 'rel_bias': out['rel_bias'], 'sinks': out['sinks'], 'w_out': out['w_out'], 'norm2_g': out['norm2_g'], 'w_up': out['w_up'], 'conv_w': out['conv_w'], 'conv_b': out['conv_b'], 'w_down': out['w_down'], 'final_norm_g': out['final_norm_g'], 'loss_target': out['loss_target'], 'm_norm1_g': out['m_norm1_g'], 'm_w_in': out['m_w_in'], 'm_q_a_norm_g': out['m_q_a_norm_g'], 'm_w_q_b': out['m_w_q_b'], 'm_kv_a_norm_g': out['m_kv_a_norm_g'], 'm_w_kv_b': out['m_w_kv_b'], 'm_rel_bias': out['m_rel_bias'], 'm_sinks': out['m_sinks'], 'm_w_out': out['m_w_out'], 'm_norm2_g': out['m_norm2_g'], 'm_w_up': out['m_w_up'], 'm_conv_w': out['m_conv_w'], 'm_conv_b': out['m_conv_b'], 'm_w_down': out['m_w_down'], 'm_final_norm_g': out['m_final_norm_g'], 'v_norm1_g': out['v_norm1_g'], 'v_w_in': out['v_w_in'], 'v_q_a_norm_g': out['v_q_a_norm_g'], 'v_w_q_b': out['v_w_q_b'], 'v_kv_a_norm_g': out['v_kv_a_norm_g'], 'v_w_kv_b': out['v_w_kv_b'], 'v_rel_bias': out['v_rel_bias'], 'v_sinks': out['v_sinks'], 'v_w_out': out['v_w_out'], 'v_norm2_g': out['v_norm2_g'], 'v_w_up': out['v_w_up'], 'v_conv_w': out['v_conv_w'], 'v_conv_b': out['v_conv_b'], 'v_w_down': out['v_w_down'], 'v_final_norm_g': out['v_final_norm_g']}


def _loss(weights, diff, rest, loss_target):
    with _jax.named_scope("forward"):
        args = {**rest, TWIN_DIFF_INPUT: diff, **{k: w.astype(_WEIGHT_DTYPES[k]) for k, w in weights.items()}}
        y = _forward(args)
    with _jax.named_scope("loss_head"):
        err = _jnp.square(y.astype(_jnp.float32) - loss_target)
        return 0.5 * _jnp.sum(_jnp.mean(err, axis=-1)) if err.ndim else 0.5 * err


def _adamw(w, g, m, v):
    m = ADAM_B1 * m + (1.0 - ADAM_B1) * g
    v = ADAM_B2 * v + (1.0 - ADAM_B2) * _jnp.square(g)
    m_hat = m / (1.0 - ADAM_B1 ** ADAM_STEP)
    v_hat = v / (1.0 - ADAM_B2 ** ADAM_STEP)
    delta = -ADAM_LR * (m_hat / (_jnp.sqrt(v_hat) + ADAM_EPS) + ADAM_WD * w)
    return delta, m, v


def reference(x, positions, norm1_g, w_in, q_a_norm_g, w_q_b, kv_a_norm_g, w_kv_b, rel_bias, sinks, w_out, norm2_g, w_up, conv_w, conv_b, w_down, final_norm_g, loss_target, m_norm1_g, m_w_in, m_q_a_norm_g, m_w_q_b, m_kv_a_norm_g, m_w_kv_b, m_rel_bias, m_sinks, m_w_out, m_norm2_g, m_w_up, m_conv_w, m_conv_b, m_w_down, m_final_norm_g, v_norm1_g, v_w_in, v_q_a_norm_g, v_w_q_b, v_kv_a_norm_g, v_w_kv_b, v_rel_bias, v_sinks, v_w_out, v_norm2_g, v_w_up, v_conv_w, v_conv_b, v_w_down, v_final_norm_g):
    given = dict(x=x, positions=positions, norm1_g=norm1_g, w_in=w_in, q_a_norm_g=q_a_norm_g, w_q_b=w_q_b, kv_a_norm_g=kv_a_norm_g, w_kv_b=w_kv_b, rel_bias=rel_bias, sinks=sinks, w_out=w_out, norm2_g=norm2_g, w_up=w_up, conv_w=conv_w, conv_b=conv_b, w_down=w_down, final_norm_g=final_norm_g, loss_target=loss_target, m_norm1_g=m_norm1_g, m_w_in=m_w_in, m_q_a_norm_g=m_q_a_norm_g, m_w_q_b=m_w_q_b, m_kv_a_norm_g=m_kv_a_norm_g, m_w_kv_b=m_w_kv_b, m_rel_bias=m_rel_bias, m_sinks=m_sinks, m_w_out=m_w_out, m_norm2_g=m_norm2_g, m_w_up=m_w_up, m_conv_w=m_conv_w, m_conv_b=m_conv_b, m_w_down=m_w_down, m_final_norm_g=m_final_norm_g, v_norm1_g=v_norm1_g, v_w_in=v_w_in, v_q_a_norm_g=v_q_a_norm_g, v_w_q_b=v_w_q_b, v_kv_a_norm_g=v_kv_a_norm_g, v_w_kv_b=v_w_kv_b, v_rel_bias=v_rel_bias, v_sinks=v_sinks, v_w_out=v_w_out, v_norm2_g=v_norm2_g, v_w_up=v_w_up, v_conv_w=v_conv_w, v_conv_b=v_conv_b, v_w_down=v_w_down, v_final_norm_g=v_final_norm_g)
    weights = {n: given[n] for n in TWIN_WEIGHTS}
    shared = {n: given[n] for n in SHARED_INPUTS}
    per_example = {n: given[n] for n in ['x']}
    grad_fn = _jax.value_and_grad(_loss, argnums=(0, 1))

    def one_microbatch(ex, loss_target):
        ex = dict(ex)
        diff = ex.pop(TWIN_DIFF_INPUT)
        return grad_fn(weights, diff, {**shared, **ex}, loss_target)

    if N_MICROBATCH == 1:
        loss, (grad_w, grad_x) = one_microbatch(per_example, given["loss_target"])
    else:
        def body(carry, xs):
            loss_sum, grad_sum = carry
            l_k, (gw_k, gx_k) = one_microbatch(xs[0], xs[1])
            with _jax.named_scope("update"):
                return (loss_sum + l_k, _jax.tree.map(_jnp.add, grad_sum, gw_k)), gx_k

        init = (_jnp.zeros((), _jnp.float32), _jax.tree.map(_jnp.zeros_like, weights))
        (loss, grad_w), grad_x = _jax.lax.scan(body, init, (per_example, given["loss_target"]))
    with _jax.named_scope("update"):
        delta_w, new_m, new_v = {}, {}, {}
        for n in TWIN_WEIGHTS:
            delta_w[n], new_m[n], new_v[n] = _adamw(weights[n], grad_w[n], given["m_" + n], given["v_" + n])
    return (loss, grad_x, *[grad_w[n] for n in TWIN_WEIGHTS], *[delta_w[n] for n in TWIN_WEIGHTS],
            *[new_m[n] for n in TWIN_WEIGHTS], *[new_v[n] for n in TWIN_WEIGHTS])
```

```python
import functools
import math

import jax
import jax.numpy as jnp
from jax import lax
from jax.experimental import pallas as pl
from jax.experimental.pallas import tpu as pltpu

F32 = jnp.float32
BF16 = jnp.bfloat16
MESH = pl.DeviceIdType.MESH

D_MODEL = 1024
EPS = 1e-6
H_A = 8
QK_NOPE = 128
QK_ROPE = 64
V_DIM = 128
Q_LORA = 256
KV_LORA = 128
ROPE_THETA = 10000.0
H_B = 16
KV_B = 4
GROUP = 4
HD_B = 64
WINDOW = 128
Q_BLOCK = 128
NUM_BUCKETS = 32
MAX_DISTANCE = 128
D_FF = 2816
HEAD_PAD = 256

ADAM_LR = 0.001
ADAM_B1 = 0.9
ADAM_B2 = 0.999
ADAM_EPS = 1e-08
ADAM_WD = 0.01
ADAM_STEP = 10

LANES = 128
P_QB, P_GA, P_GB, P_QLAT, P_KB, P_VB, P_CKV, P_KR = 0, 1024, 2048, 3072, 3328, 3584, 3840, 3968
W_IN_PAD = 4096

NT = (((1,), (1,)), ((), ()))
NN = (((1,), (0,)), ((), ()))
TN = (((0,), (0,)), ((), ()))


def _arb(n):
    return pltpu.CompilerParams(dimension_semantics=("arbitrary",) * n)


def _matmul(name, a, b, *, out_shape, out_dtype, grid, a_spec, b_spec, o_spec, contract, add=None):
    nk = grid[2]
    acc_shape = tuple(d for d in o_spec.block_shape if d is not None)

    def body(*refs):
        if add is not None:
            a_ref, b_ref, add_ref, o_ref = refs[:4]
            scratch = refs[4:]
        else:
            a_ref, b_ref, o_ref = refs[:3]
            add_ref = None
            scratch = refs[3:]
        prod = lax.dot_general(a_ref[...].astype(BF16), b_ref[...].astype(BF16), contract,
                               preferred_element_type=F32)

        def finish(val):
            if add_ref is not None:
                val = add_ref[...] + val
            o_ref[...] = val.astype(out_dtype)

        if nk == 1:
            finish(prod)
        else:
            acc_ref = scratch[0]
            k = pl.program_id(2)

            @pl.when(k == 0)
            def _():
                acc_ref[...] = prod

            @pl.when(k > 0)
            def _():
                acc_ref[...] += prod

            @pl.when(k == nk - 1)
            def _():
                finish(acc_ref[...])

    in_specs = [a_spec, b_spec]
    args = [a, b]
    if add is not None:
        in_specs.append(o_spec)
        args.append(add)
    return pl.pallas_call(
        body, name=name, grid=grid, in_specs=in_specs, out_specs=o_spec,
        out_shape=jax.ShapeDtypeStruct(out_shape, out_dtype),
        scratch_shapes=[pltpu.VMEM(acc_shape, F32)] if nk > 1 else [],
        compiler_params=_arb(3),
    )(*args)


def _bs(block, fn):
    return pl.BlockSpec(block, fn)


def _rmsnorm_fwd(name, src, g, d, cb, ts=512):
    s = src.shape[0]

    def body(x_ref, g_ref, h_ref, r_ref):
        x = x_ref[...]
        r = lax.rsqrt(jnp.mean(x * x, axis=-1, keepdims=True) + EPS)
        h_ref[...] = (x * r * g_ref[...]).astype(BF16)
        r_ref[...] = r

    return pl.pallas_call(
        body, name=name, grid=(s // ts,),
        in_specs=[_bs((ts, d), lambda i: (i, cb)), _bs((1, d), lambda i: (0, 0))],
        out_specs=[_bs((ts, d), lambda i: (i, 0)), _bs((ts, 1), lambda i: (i, 0))],
        out_shape=[jax.ShapeDtypeStruct((s, d), BF16), jax.ShapeDtypeStruct((s, 1), F32)],
        compiler_params=_arb(1),
    )(src, g)


def _rmsnorm_bwd(name, dy, src, rstd, g, d, cb, out_dtype, res=None, ts=512):
    s = src.shape[0]

    def body(*refs):
        if res is not None:
            dy_ref, x_ref, r_ref, g_ref, res_ref, dx_ref, dg_ref = refs
        else:
            dy_ref, x_ref, r_ref, g_ref, dx_ref, dg_ref = refs
            res_ref = None
        dyv = dy_ref[...]
        r = r_ref[...]
        xhat = x_ref[...] * r
        dyh = dyv * g_ref[...]
        c = jnp.mean(dyh * xhat, axis=-1, keepdims=True)
        dx = r * (dyh - xhat * c)
        if res_ref is not None:
            dx = res_ref[...] + dx
        dx_ref[...] = dx.astype(out_dtype)
        part = jnp.sum(dyv * xhat, axis=0, keepdims=True)

        @pl.when(pl.program_id(0) == 0)
        def _():
            dg_ref[...] = part

        @pl.when(pl.program_id(0) > 0)
        def _():
            dg_ref[...] += part

    in_specs = [_bs((ts, d), lambda i: (i, 0)), _bs((ts, d), lambda i: (i, cb)),
                _bs((ts, 1), lambda i: (i, 0)), _bs((1, d), lambda i: (0, 0))]
    args = [dy, src, rstd, g]
    if res is not None:
        in_specs.append(_bs((ts, d), lambda i: (i, 0)))
        args.append(res)
    return pl.pallas_call(
        body, name=name, grid=(s // ts,), in_specs=in_specs,
        out_specs=[_bs((ts, d), lambda i: (i, 0)), _bs((1, d), lambda i: (0, 0))],
        out_shape=[jax.ShapeDtypeStruct((s, d), out_dtype), jax.ShapeDtypeStruct((1, d), F32)],
        compiler_params=_arb(1),
    )(*args)


def _final_loss(x2, target, g, ts=512):
    s, d = x2.shape

    def body(x_ref, t_ref, g_ref, loss_ref, dx_ref, dg_ref):
        x = x_ref[...]
        r = lax.rsqrt(jnp.mean(x * x, axis=-1, keepdims=True) + EPS)
        xhat = x * r
        gv = g_ref[...]
        err = xhat * gv - t_ref[...]
        lpart = 0.5 * jnp.sum(jnp.mean(err * err, axis=-1, keepdims=True), axis=0, keepdims=True)
        dyv = err * (1.0 / d)
        dyh = dyv * gv
        c = jnp.mean(dyh * xhat, axis=-1, keepdims=True)
        dx_ref[...] = r * (dyh - xhat * c)
        gpart = jnp.sum(dyv * xhat, axis=0, keepdims=True)

        @pl.when(pl.program_id(0) == 0)
        def _():
            dg_ref[...] = gpart
            loss_ref[...] = lpart

        @pl.when(pl.program_id(0) > 0)
        def _():
            dg_ref[...] += gpart
            loss_ref[...] += lpart

    return pl.pallas_call(
        body, name="final_loss", grid=(s // ts,),
        in_specs=[_bs((ts, d), lambda i: (i, 0)), _bs((ts, d), lambda i: (i, 0)), _bs((1, d), lambda i: (0, 0))],
        out_specs=[_bs((1, 1), lambda i: (0, 0)), _bs((ts, d), lambda i: (i, 0)), _bs((1, d), lambda i: (0, 0))],
        out_shape=[jax.ShapeDtypeStruct((1, 1), F32), jax.ShapeDtypeStruct((s, d), F32),
                   jax.ShapeDtypeStruct((1, d), F32)],
        compiler_params=_arb(1),
    )(x2, target, g)


def _swap_halves(t):
    lane = lax.broadcasted_iota(jnp.int32, t.shape, 1)
    return jnp.where(lane < 32, pltpu.roll(t, 96, 1), pltpu.roll(t, 32, 1))


def _rope_fwd(t, cos_t, sin_t):
    return t * cos_t + _swap_halves(t) * sin_t


def _rope_bwd(dt, cos_t, sin_t):
    return dt * cos_t - _swap_halves(dt) * sin_t


def _lat_norms(proj, gq, gkv, ts=512):
    s = proj.shape[0]

    def body(q_ref, c_ref, gq_ref, gkv_ref, qn_ref, cn_ref, rq_ref, rc_ref):
        q = q_ref[...]
        rq = lax.rsqrt(jnp.mean(q * q, axis=-1, keepdims=True) + EPS)
        qn_ref[...] = (q * rq * gq_ref[...]).astype(BF16)
        rq_ref[...] = rq
        cv = c_ref[...]
        rc = lax.rsqrt(jnp.mean(cv * cv, axis=-1, keepdims=True) + EPS)
        cn_ref[...] = (cv * rc * gkv_ref[...]).astype(BF16)
        rc_ref[...] = rc

    return pl.pallas_call(
        body, name="lat_norms", grid=(s // ts,),
        in_specs=[_bs((ts, Q_LORA), lambda i: (i, P_QLAT // Q_LORA)),
                  _bs((ts, KV_LORA), lambda i: (i, P_CKV // KV_LORA)),
                  _bs((1, Q_LORA), lambda i: (0, 0)), _bs((1, KV_LORA), lambda i: (0, 0))],
        out_specs=[_bs((ts, Q_LORA), lambda i: (i, 0)), _bs((ts, KV_LORA), lambda i: (i, 0)),
                   _bs((ts, 1), lambda i: (i, 0)), _bs((ts, 1), lambda i: (i, 0))],
        out_shape=[jax.ShapeDtypeStruct((s, Q_LORA), BF16), jax.ShapeDtypeStruct((s, KV_LORA), BF16),
                   jax.ShapeDtypeStruct((s, 1), F32), jax.ShapeDtypeStruct((s, 1), F32)],
        compiler_params=_arb(1),
    )(proj, proj, gq, gkv)


def _q_heads(qn, wq, cos_t, sin_t, ts=512):
    s = qn.shape[0]

    def body(qn_ref, w_ref, cos_ref, sin_ref, q_ref):
        o = jnp.dot(qn_ref[...], w_ref[...], preferred_element_type=F32)
        q_ref[:, :LANES] = o[:, :LANES].astype(BF16)
        q_ref[:, LANES:] = _rope_fwd(o[:, LANES:], cos_ref[...], sin_ref[...]).astype(BF16)

    return pl.pallas_call(
        body, name="q_heads", grid=(H_A, s // ts),
        in_specs=[_bs((ts, Q_LORA), lambda h, i: (i, 0)), _bs((None, Q_LORA, HEAD_PAD), lambda h, i: (h, 0, 0)),
                  _bs((ts, LANES), lambda h, i: (i, 0)), _bs((ts, LANES), lambda h, i: (i, 0))],
        out_specs=_bs((None, ts, HEAD_PAD), lambda h, i: (h, i, 0)),
        out_shape=jax.ShapeDtypeStruct((H_A, s, HEAD_PAD), BF16),
        compiler_params=_arb(2),
    )(qn, wq, cos_t, sin_t)


def _kv_heads(cn, wkv, proj, cos_t, sin_t, ts=512):
    s = cn.shape[0]

    def body(cn_ref, w_ref, kr_ref, cos_ref, sin_ref, k_ref, v_ref):
        o = jnp.dot(cn_ref[...], w_ref[...], preferred_element_type=F32)
        k_ref[:, :LANES] = o[:, :LANES].astype(BF16)
        k_ref[:, LANES:] = _rope_fwd(kr_ref[...], cos_ref[...], sin_ref[...]).astype(BF16)
        v_ref[...] = o[:, LANES:].astype(BF16)

    return pl.pallas_call(
        body, name="kv_heads", grid=(H_A, s // ts),
        in_specs=[_bs((ts, KV_LORA), lambda h, i: (i, 0)),
                  _bs((None, KV_LORA, QK_NOPE + V_DIM), lambda h, i: (h, 0, 0)),
                  _bs((ts, LANES), lambda h, i: (i, P_KR // LANES)),
                  _bs((ts, LANES), lambda h, i: (i, 0)), _bs((ts, LANES), lambda h, i: (i, 0))],
        out_specs=[_bs((None, ts, HEAD_PAD), lambda h, i: (h, i, 0)), _bs((None, ts, V_DIM), lambda h, i: (h, i, 0))],
        out_shape=[jax.ShapeDtypeStruct((H_A, s, HEAD_PAD), BF16), jax.ShapeDtypeStruct((H_A, s, V_DIM), BF16)],
        compiler_params=_arb(2),
    )(cn, wkv, proj, cos_t, sin_t)


MLA_SCALE = 1.0 / math.sqrt(QK_NOPE + QK_ROPE)


def _mla_fwd(q, k, v, tq=512, tk=512):
    s = q.shape[1]
    nk = s // tk

    def body(q_ref, k_ref, v_ref, o_ref, lse_ref, m_ref, l_ref, acc_ref):
        m_ref[...] = jnp.full(m_ref.shape, -jnp.inf, F32)
        l_ref[...] = jnp.zeros(l_ref.shape, F32)
        acc_ref[...] = jnp.zeros(acc_ref.shape, F32)
        qv = q_ref[...]

        def step(c, carry):
            rows = pl.ds(pl.multiple_of(c * tk, tk), tk)
            sc = lax.dot_general(qv, k_ref[rows, :], NT, preferred_element_type=F32) * MLA_SCALE
            m_prev = m_ref[...]
            m_new = jnp.maximum(m_prev, jnp.max(sc, axis=-1, keepdims=True))
            alpha = jnp.exp(m_prev - m_new)
            p = jnp.exp(sc - m_new)
            l_ref[...] = alpha * l_ref[...] + jnp.sum(p, axis=-1, keepdims=True)
            acc_ref[...] = alpha * acc_ref[...] + jnp.dot(p.astype(BF16), v_ref[rows, :],
                                                          preferred_element_type=F32)
            m_ref[...] = m_new
            return carry

        lax.fori_loop(0, nk, step, 0)
        l = l_ref[...]
        o_ref[...] = acc_ref[...] / l
        lse_ref[...] = m_ref[...] + jnp.log(l)

    return pl.pallas_call(
        body, name="mla_fwd", grid=(H_A, s // tq),
        in_specs=[_bs((None, tq, HEAD_PAD), lambda h, i: (h, i, 0)),
                  _bs((None, s, HEAD_PAD), lambda h, i: (h, 0, 0)),
                  _bs((None, s, V_DIM), lambda h, i: (h, 0, 0))],
        out_specs=[_bs((tq, V_DIM), lambda h, i: (i, h)), _bs((None, tq, 1), lambda h, i: (h, i, 0))],
        out_shape=[jax.ShapeDtypeStruct((s, H_A * V_DIM), F32), jax.ShapeDtypeStruct((H_A, s, 1), F32)],
        scratch_shapes=[pltpu.VMEM((tq, 1), F32), pltpu.VMEM((tq, 1), F32), pltpu.VMEM((tq, V_DIM), F32)],
        compiler_params=_arb(2),
    )(q, k, v)


def _mla_bwd(q, k, v, do, o, lse, tq=512, tk=512):
    s = q.shape[1]
    nq = s // tq

    def body(q_ref, k_ref, v_ref, do_ref, o_ref, lse_ref, dq_ref, dk_ref, dv_ref, delta_ref):
        @pl.when(pl.program_id(1) == 0)
        def _():
            def init(c, carry):
                rows = pl.ds(pl.multiple_of(c * tq, tq), tq)
                delta_ref[rows, :] = jnp.sum(do_ref[rows, :] * o_ref[rows, :], axis=-1, keepdims=True)
                dq_ref[rows, :] = jnp.zeros((tq, HEAD_PAD), F32)
                return carry

            lax.fori_loop(0, nq, init, 0)

        dk_ref[...] = jnp.zeros(dk_ref.shape, F32)
        dv_ref[...] = jnp.zeros(dv_ref.shape, F32)
        kb = k_ref[...]
        vb = v_ref[...]

        def step(c, carry):
            rows = pl.ds(pl.multiple_of(c * tq, tq), tq)
            qc = q_ref[rows, :]
            doc = do_ref[rows, :].astype(BF16)
            sc = lax.dot_general(qc, kb, NT, preferred_element_type=F32) * MLA_SCALE
            p = jnp.exp(sc - lse_ref[rows, :])
            dv_ref[...] += lax.dot_general(p.astype(BF16), doc, TN, preferred_element_type=F32)
            dp = lax.dot_general(doc, vb, NT, preferred_element_type=F32)
            ds = (p * (dp - delta_ref[rows, :]) * MLA_SCALE).astype(BF16)
            dk_ref[...] += lax.dot_general(ds, qc, TN, preferred_element_type=F32)
            dq_ref[rows, :] += jnp.dot(ds, kb, preferred_element_type=F32)
            return carry

        lax.fori_loop(0, nq, step, 0)

    return pl.pallas_call(
        body, name="mla_bwd", grid=(H_A, s // tk),
        in_specs=[_bs((None, s, HEAD_PAD), lambda h, j: (h, 0, 0)),
                  _bs((None, tk, HEAD_PAD), lambda h, j: (h, j, 0)),
                  _bs((None, tk, V_DIM), lambda h, j: (h, j, 0)),
                  _bs((s, V_DIM), lambda h, j: (0, h)), _bs((s, V_DIM), lambda h, j: (0, h)),
                  _bs((None, s, 1), lambda h, j: (h, 0, 0))],
        out_specs=[_bs((None, s, HEAD_PAD), lambda h, j: (h, 0, 0)),
                   _bs((None, tk, HEAD_PAD), lambda h, j: (h, j, 0)),
                   _bs((None, tk, V_DIM), lambda h, j: (h, j, 0))],
        out_shape=[jax.ShapeDtypeStruct((H_A, s, HEAD_PAD), F32), jax.ShapeDtypeStruct((H_A, s, HEAD_PAD), F32),
                   jax.ShapeDtypeStruct((H_A, s, V_DIM), F32)],
        scratch_shapes=[pltpu.VMEM((s, 1), F32)],
        compiler_params=_arb(2),
    )(q, k, v, do, o, lse)


def _mla_bwd_prep(dq, dk, dv, cos_t, sin_t, ts=256):
    s = dq.shape[1]

    def body(dq_ref, dk_ref, dv_ref, cos_ref, sin_ref, dqp_ref, dkvp_ref, dkr_ref):
        cos_v = cos_ref[...]
        sin_v = sin_ref[...]
        kr = jnp.zeros((ts, LANES), F32)
        for h in range(H_A):
            dqp_ref[h, :, :LANES] = dq_ref[h, :, :LANES].astype(BF16)
            dqp_ref[h, :, LANES:] = _rope_bwd(dq_ref[h, :, LANES:], cos_v, sin_v).astype(BF16)
            dkvp_ref[h, :, :LANES] = dk_ref[h, :, :LANES].astype(BF16)
            dkvp_ref[h, :, LANES:] = dv_ref[h].astype(BF16)
            kr = kr + dk_ref[h, :, LANES:]
        dkr_ref[...] = _rope_bwd(kr, cos_v, sin_v).astype(BF16)

    blk3 = lambda w: _bs((H_A, ts, w), lambda i: (0, i, 0))
    return pl.pallas_call(
        body, name="mla_bwd_prep", grid=(s // ts,),
        in_specs=[blk3(HEAD_PAD), blk3(HEAD_PAD), blk3(V_DIM),
                  _bs((ts, LANES), lambda i: (i, 0)), _bs((ts, LANES), lambda i: (i, 0))],
        out_specs=[blk3(HEAD_PAD), blk3(HEAD_PAD), _bs((ts, LANES), lambda i: (i, 0))],
        out_shape=[jax.ShapeDtypeStruct((H_A, s, HEAD_PAD), BF16), jax.ShapeDtypeStruct((H_A, s, HEAD_PAD), BF16),
                   jax.ShapeDtypeStruct((s, LANES), BF16)],
        compiler_params=_arb(1),
    )(dq, dk, dv, cos_t, sin_t)


WIN_SCALE = 1.0 / math.sqrt(HD_B)
SPAN = Q_BLOCK + 2 * WINDOW


def _t5_bucket_table():
    a = jnp.arange(Q_BLOCK, dtype=jnp.int32)[:, None]
    c = jnp.arange(SPAN, dtype=jnp.int32)[None, :]
    rel = c - WINDOW - a
    nb = NUM_BUCKETS // 2
    max_exact = nb // 2
    base = (rel > 0).astype(jnp.int32) * nb
    n = jnp.abs(rel)
    nf = jnp.maximum(n, 1).astype(F32)
    large = max_exact + (jnp.log(nf / max_exact) / math.log(MAX_DISTANCE / max_exact)
                         * (nb - max_exact)).astype(jnp.int32)
    large = jnp.minimum(large, nb - 1)
    return base + jnp.where(n < max_exact, n, large)


def _win_bias(bucket, rel_bias):
    def body(rb_ref, bk_ref, o_ref):
        h = pl.program_id(0)
        bk = bk_ref[...]
        acc = jnp.zeros((Q_BLOCK, SPAN), F32)
        for b in range(NUM_BUCKETS):
            acc = jnp.where(bk == b, rb_ref[b, h], acc)
        o_ref[...] = acc

    return pl.pallas_call(
        body, name="win_bias", grid=(H_B,),
        in_specs=[pl.BlockSpec(memory_space=pltpu.SMEM), _bs((Q_BLOCK, SPAN), lambda h: (0, 0))],
        out_specs=_bs((None, Q_BLOCK, SPAN), lambda h: (h, 0, 0)),
        out_shape=jax.ShapeDtypeStruct((H_B, Q_BLOCK, SPAN), F32),
        compiler_params=_arb(1),
    )(rel_bias, bucket)


def _win_specs(nblk):
    prev = lambda kv, n: (kv, jnp.maximum(n - 1, 0), 0)
    cur = lambda kv, n: (kv, n, 0)
    nxt = lambda kv, n: (kv, jnp.minimum(n + 1, nblk - 1), 0)
    kvb = lambda fn: _bs((None, Q_BLOCK, HD_B), fn)
    return [kvb(prev), kvb(cur), kvb(nxt)]


def _win_scores(q, k_refs, bias_ref, n, nblk):
    a = lax.broadcasted_iota(jnp.int32, (GROUP, Q_BLOCK, Q_BLOCK), 1)
    cc = lax.broadcasted_iota(jnp.int32, (GROUP, Q_BLOCK, Q_BLOCK), 2)
    valid = [(cc >= a) & (n > 0), None, (cc <= a) & (n < nblk - 1)]
    out = []
    for j in range(3):
        sc = lax.dot_general(q, k_refs[j][...], NT, preferred_element_type=F32)
        sc = sc.reshape(GROUP, Q_BLOCK, Q_BLOCK) * WIN_SCALE + bias_ref[:, :, j * Q_BLOCK:(j + 1) * Q_BLOCK]
        if valid[j] is not None:
            sc = jnp.where(valid[j], sc, -1e30)
        out.append(sc)
    return out


def _win_sink(sink_ref, kv):
    hs = lax.broadcasted_iota(jnp.int32, (GROUP, Q_BLOCK, 1), 0)
    sk = jnp.zeros((GROUP, Q_BLOCK, 1), F32)
    for g in range(GROUP):
        sk = jnp.where(hs == g, sink_ref[kv * GROUP + g], sk)
    return sk


def _win_fwd(qh, kh, vh, bias, sinks):
    s = qh.shape[1]
    nblk = s // Q_BLOCK
    rows = GROUP * Q_BLOCK

    def body(sink_ref, q_ref, k0, k1, k2, v0, v1, v2, bias_ref, o_ref, lse_ref):
        kv = pl.program_id(0)
        n = pl.program_id(1)
        q = q_ref[...].reshape(rows, HD_B)
        ss = _win_scores(q, (k0, k1, k2), bias_ref, n, nblk)
        sk = _win_sink(sink_ref, kv)
        m = jnp.maximum(jnp.maximum(jnp.max(ss[0], axis=2, keepdims=True), jnp.max(ss[1], axis=2, keepdims=True)),
                        jnp.maximum(jnp.max(ss[2], axis=2, keepdims=True), sk))
        es = [jnp.exp(sc - m) for sc in ss]
        l = (jnp.sum(es[0], axis=2, keepdims=True) + jnp.sum(es[1], axis=2, keepdims=True)
             + jnp.sum(es[2], axis=2, keepdims=True) + jnp.exp(sk - m))
        acc = jnp.zeros((rows, HD_B), F32)
        for e, v_ref in zip(es, (v0, v1, v2)):
            p = (e / l).astype(BF16).reshape(rows, Q_BLOCK)
            acc = acc + jnp.dot(p, v_ref[...], preferred_element_type=F32)
        o_ref[...] = acc.reshape(GROUP, Q_BLOCK, HD_B)
        lse_ref[...] = m + jnp.log(l)

    qspec = _bs((GROUP, Q_BLOCK, HD_B), lambda kv, n: (kv, n, 0))
    return pl.pallas_call(
        body, name="win_fwd", grid=(KV_B, nblk),
        in_specs=[pl.BlockSpec(memory_space=pltpu.SMEM), qspec] + _win_specs(nblk) + _win_specs(nblk)
        + [_bs((GROUP, Q_BLOCK, SPAN), lambda kv, n: (kv, 0, 0))],
        out_specs=[qspec, _bs((GROUP, Q_BLOCK, 1), lambda kv, n: (kv, n, 0))],
        out_shape=[jax.ShapeDtypeStruct((H_B, s, HD_B), F32), jax.ShapeDtypeStruct((H_B, s, 1), F32)],
        compiler_params=_arb(2),
    )(sinks, qh, kh, kh, kh, vh, vh, vh, bias)


def _win_bwd(qh, kh, vh, bias, sinks, doh, lse):
    s = qh.shape[1]
    nblk = s // Q_BLOCK
    rows = GROUP * Q_BLOCK
    spad = s + 2 * WINDOW

    def body(sink_ref, q_ref, k0, k1, k2, v0, v1, v2, bias_ref, do_ref, lse_ref,
             dq_ref, dk_ref, dv_ref, db_ref, dsk_ref):
        kv = pl.program_id(0)
        n = pl.program_id(1)

        @pl.when(n == 0)
        def _():
            dk_ref[...] = jnp.zeros(dk_ref.shape, F32)
            dv_ref[...] = jnp.zeros(dv_ref.shape, F32)
            db_ref[...] = jnp.zeros(db_ref.shape, F32)
            dsk_ref[...] = jnp.zeros(dsk_ref.shape, F32)

        q = q_ref[...].reshape(rows, HD_B)
        dob = do_ref[...].reshape(rows, HD_B).astype(BF16)
        lse_v = lse_ref[...]
        ss = _win_scores(q, (k0, k1, k2), bias_ref, n, nblk)
        ps = [jnp.exp(sc - lse_v) for sc in ss]
        dps = [lax.dot_general(dob, v_ref[...], NT, preferred_element_type=F32).reshape(GROUP, Q_BLOCK, Q_BLOCK)
               for v_ref in (v0, v1, v2)]
        delta = (jnp.sum(ps[0] * dps[0], axis=2, keepdims=True) + jnp.sum(ps[1] * dps[1], axis=2, keepdims=True)
                 + jnp.sum(ps[2] * dps[2], axis=2, keepdims=True))
        dq = jnp.zeros((rows, HD_B), F32)
        for j, k_ref in enumerate((k0, k1, k2)):
            ds = ps[j] * (dps[j] - delta)
            db_ref[:, :, j * Q_BLOCK:(j + 1) * Q_BLOCK] += ds
            dsb = (ds * WIN_SCALE).astype(BF16).reshape(rows, Q_BLOCK)
            dq = dq + jnp.dot(dsb, k_ref[...], preferred_element_type=F32)
            krows = pl.ds(pl.multiple_of((n + j) * Q_BLOCK, Q_BLOCK), Q_BLOCK)
            dk_ref[krows, :] += lax.dot_general(dsb, q, TN, preferred_element_type=F32)
            dv_ref[krows, :] += lax.dot_general(ps[j].astype(BF16).reshape(rows, Q_BLOCK), dob, TN,
                                                preferred_element_type=F32)
        sk = _win_sink(sink_ref, kv)
        dsk_ref[...] += -(jnp.exp(sk - lse_v) * delta)
        dq_ref[...] = dq.reshape(GROUP, Q_BLOCK, HD_B)

    qspec = _bs((GROUP, Q_BLOCK, HD_B), lambda kv, n: (kv, n, 0))
    kacc = _bs((None, spad, HD_B), lambda kv, n: (kv, 0, 0))
    return pl.pallas_call(
        body, name="win_bwd", grid=(KV_B, nblk),
        in_specs=[pl.BlockSpec(memory_space=pltpu.SMEM), qspec] + _win_specs(nblk) + _win_specs(nblk)
        + [_bs((GROUP, Q_BLOCK, SPAN), lambda kv, n: (kv, 0, 0)), qspec,
           _bs((GROUP, Q_BLOCK, 1), lambda kv, n: (kv, n, 0))],
        out_specs=[qspec, kacc, kacc, _bs((GROUP, Q_BLOCK, SPAN), lambda kv, n: (kv, 0, 0)),
                   _bs((GROUP, Q_BLOCK, 1), lambda kv, n: (kv, 0, 0))],
        out_shape=[jax.ShapeDtypeStruct((H_B, s, HD_B), F32), jax.ShapeDtypeStruct((KV_B, spad, HD_B), F32),
                   jax.ShapeDtypeStruct((KV_B, spad, HD_B), F32), jax.ShapeDtypeStruct((H_B, Q_BLOCK, SPAN), F32),
                   jax.ShapeDtypeStruct((H_B, Q_BLOCK, 1), F32)],
        compiler_params=_arb(2),
    )(sinks, qh, kh, kh, kh, vh, vh, vh, bias, doh, lse)


def _win_param_grads(bucket, dbias, dsink_rows):
    def body(bk_ref, db_ref, ds_ref, o_ref):
        bk = bk_ref[...]
        dbv = db_ref[...]
        lane = lax.broadcasted_iota(jnp.int32, (1, LANES), 1)
        res = jnp.zeros((1, LANES), F32)
        for b in range(NUM_BUCKETS):
            tot = jnp.sum(jnp.sum(jnp.where(bk == b, dbv, 0.0), axis=1, keepdims=True), axis=0, keepdims=True)
            res = jnp.where(lane == b, tot, res)
        stot = jnp.sum(ds_ref[...], axis=0, keepdims=True)
        o_ref[...] = jnp.where(lane == NUM_BUCKETS, stot, res)

    return pl.pallas_call(
        body, name="win_param_grads", grid=(H_B,),
        in_specs=[_bs((Q_BLOCK, SPAN), lambda h: (0, 0)), _bs((None, Q_BLOCK, SPAN), lambda h: (h, 0, 0)),
                  _bs((None, Q_BLOCK, 1), lambda h: (h, 0, 0))],
        out_specs=_bs((None, 1, LANES), lambda h: (h, 0, 0)),
        out_shape=jax.ShapeDtypeStruct((H_B, 1, LANES), F32),
        compiler_params=_arb(1),
    )(bucket, dbias, dsink_rows)


def _gate_fwd(proj, o_a, o_b, ts=256):
    s = o_a.shape[0]
    wide = lambda cb: _bs((ts, D_MODEL), lambda i: (i, cb))

    def body(ga_ref, gb_ref, oa_ref, ob_ref, m_ref):
        m_ref[...] = (jax.nn.sigmoid(ga_ref[...]) * oa_ref[...]
                      + jax.nn.sigmoid(gb_ref[...]) * ob_ref[...]).astype(BF16)

    return pl.pallas_call(
        body, name="gate_fwd", grid=(s // ts,),
        in_specs=[wide(P_GA // D_MODEL), wide(P_GB // D_MODEL), wide(0), wide(0)],
        out_specs=wide(0), out_shape=jax.ShapeDtypeStruct((s, D_MODEL), BF16),
        compiler_params=_arb(1),
    )(proj, proj, o_a, o_b)


def _gate_bwd(dmixed, proj, o_a, o_b, ts=256):
    s = o_a.shape[0]
    wide = lambda cb: _bs((ts, D_MODEL), lambda i: (i, cb))

    def body(dm_ref, ga_ref, gb_ref, oa_ref, ob_ref, doa_ref, dob_ref, dga_ref, dgb_ref):
        dm = dm_ref[...]
        sa = jax.nn.sigmoid(ga_ref[...])
        sb = jax.nn.sigmoid(gb_ref[...])
        doa_ref[...] = dm * sa
        dob_ref[...] = dm * sb
        dga_ref[...] = (dm * oa_ref[...] * (sa * (1.0 - sa))).astype(BF16)
        dgb_ref[...] = (dm * ob_ref[...] * (sb * (1.0 - sb))).astype(BF16)

    return pl.pallas_call(
        body, name="gate_bwd", grid=(s // ts,),
        in_specs=[wide(0), wide(P_GA // D_MODEL), wide(P_GB // D_MODEL), wide(0), wide(0)],
        out_specs=[wide(0)] * 4,
        out_shape=[jax.ShapeDtypeStruct((s, D_MODEL), F32), jax.ShapeDtypeStruct((s, D_MODEL), F32),
                   jax.ShapeDtypeStruct((s, D_MODEL), BF16), jax.ShapeDtypeStruct((s, D_MODEL), BF16)],
        compiler_params=_arb(1),
    )(dmixed, proj, proj, o_a, o_b)


CONV_CHUNK = 512
N_SLAB = D_FF // LANES


def _shifted(ref, c, nchunks):
    r0 = c * CONV_CHUNK
    cur = ref[r0:r0 + CONV_CHUNK, :]
    row = lax.broadcasted_iota(jnp.int32, cur.shape, 0)
    before = ref[r0 - 8:r0, :][7:8, :] if c > 0 else jnp.zeros((1, LANES), F32)
    after = ref[r0 + CONV_CHUNK:r0 + CONV_CHUNK + 8, :][0:1, :] if c < nchunks - 1 else jnp.zeros((1, LANES), F32)
    prev = jnp.where(row == 0, before, pltpu.roll(cur, 1, 0))
    nxt = jnp.where(row == CONV_CHUNK - 1, after, pltpu.roll(cur, CONV_CHUNK - 1, 0))
    return prev, cur, nxt


def _conv_taps(ref, w_ref, b_ref, c, nchunks):
    prev, cur, nxt = _shifted(ref, c, nchunks)
    conv = prev * w_ref[0:1, :] + cur * w_ref[1:2, :] + nxt * w_ref[2:3, :] + b_ref[...]
    return conv, prev, cur, nxt


def _convffn_fwd(u, conv_w, conv_b):
    s = u.shape[0]
    nchunks = s // CONV_CHUNK

    def body(ug_ref, uv_ref, wg_ref, wv_ref, bg_ref, bv_ref, f_ref):
        for c in range(nchunks):
            cg = _conv_taps(ug_ref, wg_ref, bg_ref, c, nchunks)[0]
            cv = _conv_taps(uv_ref, wv_ref, bv_ref, c, nchunks)[0]
            f_ref[c * CONV_CHUNK:(c + 1) * CONV_CHUNK, :] = (cg * jax.nn.sigmoid(cg) * cv).astype(BF16)

    slab = lambda off: _bs((s, LANES), lambda j: (0, off + j))
    wsl = lambda off: _bs((3, LANES), lambda j: (0, off + j))
    bsl = lambda off: _bs((1, LANES), lambda j: (0, off + j))
    return pl.pallas_call(
        body, name="convffn_fwd", grid=(N_SLAB,),
        in_specs=[slab(0), slab(N_SLAB), wsl(0), wsl(N_SLAB), bsl(0), bsl(N_SLAB)],
        out_specs=slab(0), out_shape=jax.ShapeDtypeStruct((s, D_FF), BF16),
        compiler_params=_arb(1),
    )(u, u, conv_w, conv_w, conv_b, conv_b)


def _convffn_bwd(u, conv_w, conv_b, df):
    s = u.shape[0]
    nchunks = s // CONV_CHUNK

    def body(ug_ref, uv_ref, wg_ref, wv_ref, bg_ref, bv_ref, df_ref, du_ref, dw_ref, db_ref, dcg_ref, dcv_ref):
        dwg = [jnp.zeros((1, LANES), F32) for _ in range(3)]
        dwv = [jnp.zeros((1, LANES), F32) for _ in range(3)]
        dbg = jnp.zeros((1, LANES), F32)
        dbv = jnp.zeros((1, LANES), F32)
        for c in range(nchunks):
            rows = slice(c * CONV_CHUNK, (c + 1) * CONV_CHUNK)
            cg, gp, gc, gn = _conv_taps(ug_ref, wg_ref, bg_ref, c, nchunks)
            cv, vp, vc, vn = _conv_taps(uv_ref, wv_ref, bv_ref, c, nchunks)
            dfv = df_ref[rows, :]
            sg = jax.nn.sigmoid(cg)
            dcg = dfv * cv * (sg * (1.0 + cg * (1.0 - sg)))
            dcv = dfv * (cg * sg)
            dcg_ref[rows, :] = dcg
            dcv_ref[rows, :] = dcv
            for t, (tg, tv) in enumerate(((gp, vp), (gc, vc), (gn, vn))):
                dwg[t] = dwg[t] + jnp.sum(tg * dcg, axis=0, keepdims=True)
                dwv[t] = dwv[t] + jnp.sum(tv * dcv, axis=0, keepdims=True)
            dbg = dbg + jnp.sum(dcg, axis=0, keepdims=True)
            dbv = dbv + jnp.sum(dcv, axis=0, keepdims=True)
        for t in range(3):
            dw_ref[0, t:t + 1, :] = dwg[t]
            dw_ref[1, t:t + 1, :] = dwv[t]
        db_ref[0] = dbg
        db_ref[1] = dbv
        for half, (dc_ref, w_ref) in enumerate(((dcg_ref, wg_ref), (dcv_ref, wv_ref))):
            for c in range(nchunks):
                prev, cur, nxt = _shifted(dc_ref, c, nchunks)
                du = nxt * w_ref[0:1, :] + cur * w_ref[1:2, :] + prev * w_ref[2:3, :]
                du_ref[half, c * CONV_CHUNK:(c + 1) * CONV_CHUNK, :] = du.astype(BF16)

    slab = lambda off: _bs((s, LANES), lambda j: (0, off + j))
    wsl = lambda off: _bs((3, LANES), lambda j: (0, off + j))
    bsl = lambda off: _bs((1, LANES), lambda j: (0, off + j))
    return pl.pallas_call(
        body, name="convffn_bwd", grid=(N_SLAB,),
        in_specs=[slab(0), slab(N_SLAB), wsl(0), wsl(N_SLAB), bsl(0), bsl(N_SLAB), slab(0)],
        out_specs=[_bs((2, s, LANES), lambda j: (0, 0, j)), _bs((2, 3, LANES), lambda j: (0, 0, j)),
                   _bs((2, 1, LANES), lambda j: (0, 0, j))],
        out_shape=[jax.ShapeDtypeStruct((2, s, D_FF), BF16), jax.ShapeDtypeStruct((2, 3, D_FF), F32),
                   jax.ShapeDtypeStruct((2, 1, D_FF), F32)],
        scratch_shapes=[pltpu.VMEM((s, LANES), F32), pltpu.VMEM((s, LANES), F32)],
        compiler_params=_arb(1),
    )(u, u, conv_w, conv_w, conv_b, conv_b, df)


def _row_tile(rows, limit=512):
    best = rows
    for t in range(8, min(rows, limit) + 1, 8):
        if rows % t == 0:
            best = t
    return best if rows % 8 == 0 else rows


def _adamw(name, w, g, m, v):
    rows, cols = w.shape
    tr = _row_tile(rows)
    c1 = 1.0 - ADAM_B1 ** ADAM_STEP
    c2 = 1.0 - ADAM_B2 ** ADAM_STEP

    def body(w_ref, g_ref, m_ref, v_ref, d_ref, nm_ref, nv_ref):
        gv = g_ref[...]
        nm = ADAM_B1 * m_ref[...] + (1.0 - ADAM_B1) * gv
        nv = ADAM_B2 * v_ref[...] + (1.0 - ADAM_B2) * (gv * gv)
        m_hat = nm / c1
        v_hat = nv / c2
        d_ref[...] = -ADAM_LR * (m_hat / (jnp.sqrt(v_hat) + ADAM_EPS) + ADAM_WD * w_ref[...])
        nm_ref[...] = nm
        nv_ref[...] = nv

    spec = _bs((tr, cols), lambda i: (i, 0))
    return pl.pallas_call(
        body, name=name, grid=(rows // tr,), in_specs=[spec] * 4, out_specs=[spec] * 3,
        out_shape=[jax.ShapeDtypeStruct((rows, cols), F32)] * 3, compiler_params=_arb(1),
    )(w, g, m, v)


ANY = pl.BlockSpec(memory_space=pl.ANY)


def _mesh_pos():
    return lax.axis_index("x"), lax.axis_index("y"), lax.axis_index("c")


def _other_chips(x, y):
    return [(1 - x, y), (x, 1 - y), (1 - x, 1 - y)]


def _allgather_weights(shards, split):
    n = len(shards)

    def body(*refs):
        w_refs, o_refs = refs[:n], refs[n:2 * n]
        send_sems, recv_sems, fsend_sems, frecv_sems, local_sems = refs[2 * n:]
        x, y, c = _mesh_pos()
        p = 2 * x + y
        chips = _other_chips(x, y)

        def piece(i, chip_index, core):
            ref = o_refs[i].at[chip_index]
            if split[i]:
                half = shards[i].shape[0] // 2
                return ref.at[pl.ds(core * half, half)]
            return ref

        def remote(src, dst, ssem, rsem, to):
            return pltpu.make_async_remote_copy(src_ref=src, dst_ref=dst, send_sem=ssem, recv_sem=rsem,
                                                device_id=to, device_id_type=MESH)

        started = []
        for i in range(n):
            mine = pltpu.make_async_copy(w_refs[i], o_refs[i].at[p], local_sems.at[i])
            mine.start()
            started.append(mine)
        sends = []
        for i in range(n):
            if split[i]:
                half = shards[i].shape[0] // 2
                src = w_refs[i].at[pl.ds(c * half, half)]
            else:
                src = w_refs[i]
            for k, chip in enumerate(chips):
                cp = remote(src, piece(i, p, c), send_sems.at[3 * i + k], recv_sems.at[3 * i + k], (*chip, c))
                cp.start()
                sends.append(cp)
        for i in range(n):
            for k, chip in enumerate(chips):
                pk = 2 * chip[0] + chip[1]
                landed = piece(i, pk, c)
                remote(landed, landed, send_sems.at[3 * i + k], recv_sems.at[3 * i + k], (*chip, c)).wait_recv()
                if split[i]:
                    fw = remote(landed, landed, fsend_sems.at[3 * i + k], frecv_sems.at[3 * i + k], (x, y, 1 - c))
                    fw.start()
                    sends.append(fw)
        for i in range(n):
            if split[i]:
                for k, chip in enumerate(chips):
                    pk = 2 * chip[0] + chip[1]
                    theirs = piece(i, pk, 1 - c)
                    remote(theirs, theirs, fsend_sems.at[3 * i + k], frecv_sems.at[3 * i + k],
                           (x, y, 1 - c)).wait_recv()
        for cp in sends:
            cp.wait_send()
        for cp in started:
            cp.wait()

    return pl.pallas_call(
        body, name="allgather_weights",
        in_specs=[ANY] * n, out_specs=[ANY] * n,
        out_shape=[jax.ShapeDtypeStruct((4,) + w.shape, w.dtype) for w in shards],
        scratch_shapes=[pltpu.SemaphoreType.DMA((3 * n,)), pltpu.SemaphoreType.DMA((3 * n,)),
                        pltpu.SemaphoreType.DMA((3 * n,)), pltpu.SemaphoreType.DMA((3 * n,)),
                        pltpu.SemaphoreType.DMA((n,))],
    )(*shards)


def _rs_pair_exchange(grads):
    n = len(grads)

    def body(*refs):
        g_refs, o_refs = refs[:n], refs[n:2 * n]
        send_sems, recv_sems = refs[2 * n:]
        x, y, c = _mesh_pos()
        cps = []
        for i in range(n):
            half = grads[i].shape[1] // 2
            cp = pltpu.make_async_remote_copy(
                src_ref=g_refs[i].at[:, pl.ds((1 - c) * half, half), :], dst_ref=o_refs[i],
                send_sem=send_sems.at[i], recv_sem=recv_sems.at[i], device_id=(x, y, 1 - c), device_id_type=MESH)
            cp.start()
            cps.append(cp)
        for cp in cps:
            cp.wait()

    return pl.pallas_call(
        body, name="rs_pair_exchange", in_specs=[ANY] * n, out_specs=[ANY] * n,
        out_shape=[jax.ShapeDtypeStruct((4, g.shape[1] // 2, g.shape[2]), F32) for g in grads],
        scratch_shapes=[pltpu.SemaphoreType.DMA((n,)), pltpu.SemaphoreType.DMA((n,))],
    )(*grads)


def _rs_pair_add(name, core, g, recv):
    _, half, cols = recv.shape
    tr = _row_tile(half)
    nr = half // tr

    def body(core_ref, g_ref, r_ref, o_ref):
        o_ref[...] = (g_ref[...] + r_ref[...]).astype(BF16)

    return pl.pallas_call(
        body, name=name,
        grid_spec=pltpu.PrefetchScalarGridSpec(
            num_scalar_prefetch=1, grid=(4, nr),
            in_specs=[pl.BlockSpec((None, tr, cols), lambda q, r, cr: (q, cr[0] * nr + r, 0)),
                      pl.BlockSpec((None, tr, cols), lambda q, r, cr: (q, r, 0))],
            out_specs=pl.BlockSpec((None, tr, cols), lambda q, r, cr: (q, r, 0))),
        out_shape=jax.ShapeDtypeStruct((4, half, cols), BF16),
        compiler_params=_arb(2),
    )(core, g, recv)


def _rs_ici(pairs):
    n = len(pairs)

    def body(*refs):
        p_refs, o_refs = refs[:n], refs[n:2 * n]
        send_sems, recv_sems = refs[2 * n:]
        x, y, c = _mesh_pos()
        cps = []
        for i in range(n):
            for k, chip in enumerate(_other_chips(x, y)):
                pk = 2 * chip[0] + chip[1]
                cp = pltpu.make_async_remote_copy(
                    src_ref=p_refs[i].at[pk], dst_ref=o_refs[i].at[k],
                    send_sem=send_sems.at[3 * i + k], recv_sem=recv_sems.at[3 * i + k],
                    device_id=(*chip, c), device_id_type=MESH)
                cp.start()
                cps.append(cp)
        for cp in cps:
            cp.wait()

    return pl.pallas_call(
        body, name="rs_ici", in_specs=[ANY] * n, out_specs=[ANY] * n,
        out_shape=[jax.ShapeDtypeStruct((3,) + pr.shape[1:], BF16) for pr in pairs],
        scratch_shapes=[pltpu.SemaphoreType.DMA((3 * n,)), pltpu.SemaphoreType.DMA((3 * n,))],
    )(*pairs)


def _rs_final_add(name, chip, pair, recv):
    _, half, cols = pair.shape
    tr = _row_tile(half)

    def body(chip_ref, p_ref, r_ref, o_ref):
        o_ref[...] = ((p_ref[...].astype(F32) + r_ref[0].astype(F32)) + r_ref[1].astype(F32)) + r_ref[2].astype(F32)

    return pl.pallas_call(
        body, name=name,
        grid_spec=pltpu.PrefetchScalarGridSpec(
            num_scalar_prefetch=1, grid=(half // tr,),
            in_specs=[pl.BlockSpec((None, tr, cols), lambda r, ch: (ch[0], r, 0)),
                      pl.BlockSpec((3, tr, cols), lambda r, ch: (0, r, 0))],
            out_specs=pl.BlockSpec((tr, cols), lambda r, ch: (r, 0))),
        out_shape=jax.ShapeDtypeStruct((half, cols), F32),
        compiler_params=_arb(1),
    )(chip, pair, recv)


def _rs_pair_share(halves):
    n = len(halves)

    def body(*refs):
        h_refs, o_refs = refs[:n], refs[n:2 * n]
        send_sems, recv_sems, local_sems = refs[2 * n:]
        x, y, c = _mesh_pos()
        cps = []
        for i in range(n):
            half = halves[i].shape[0]
            rows = o_refs[i].at[pl.ds(c * half, half)]
            mine = pltpu.make_async_copy(h_refs[i], rows, local_sems.at[i])
            mine.start()
            cp = pltpu.make_async_remote_copy(src_ref=h_refs[i], dst_ref=rows, send_sem=send_sems.at[i],
                                              recv_sem=recv_sems.at[i], device_id=(x, y, 1 - c), device_id_type=MESH)
            cp.start()
            cps.append((mine, cp, half))
        for i, (mine, cp, half) in enumerate(cps):
            mine.wait()
            cp.wait_send()
            theirs = o_refs[i].at[pl.ds((1 - c) * half, half)]
            pltpu.make_async_remote_copy(src_ref=theirs, dst_ref=theirs, send_sem=send_sems.at[i],
                                         recv_sem=recv_sems.at[i], device_id=(x, y, 1 - c),
                                         device_id_type=MESH).wait_recv()

    return pl.pallas_call(
        body, name="rs_pair_share", in_specs=[ANY] * n, out_specs=[ANY] * n,
        out_shape=[jax.ShapeDtypeStruct((2 * h.shape[0], h.shape[1]), F32) for h in halves],
        scratch_shapes=[pltpu.SemaphoreType.DMA((n,)), pltpu.SemaphoreType.DMA((n,)),
                        pltpu.SemaphoreType.DMA((n,))],
    )(*halves)


def _small_allreduce(buf):
    rows = buf.shape[0]

    def body(in_ref, out_ref, gather_ref, send_sems, recv_sems):
        x, y, c = _mesh_pos()
        me = 4 * x + 2 * y + c
        gather_ref[me] = in_ref[...]
        cps = []
        for j in range(1, 8):
            peer = (x ^ (j >> 2), y ^ ((j >> 1) & 1), c ^ (j & 1))
            cp = pltpu.make_async_remote_copy(src_ref=in_ref, dst_ref=gather_ref.at[me], send_sem=send_sems.at[j - 1],
                                              recv_sem=recv_sems.at[j - 1], device_id=peer, device_id_type=MESH)
            cp.start()
            cps.append(cp)
        for j in range(1, 8):
            peer_id = 4 * (x ^ (j >> 2)) + 2 * (y ^ ((j >> 1) & 1)) + (c ^ (j & 1))
            slot = gather_ref.at[peer_id]
            pltpu.make_async_remote_copy(src_ref=slot, dst_ref=slot, send_sem=send_sems.at[j - 1],
                                         recv_sem=recv_sems.at[j - 1], device_id=(x, y, c),
                                         device_id_type=MESH).wait_recv()
        for cp in cps:
            cp.wait_send()
        tot = gather_ref[0]
        for d in range(1, 8):
            tot = tot + gather_ref[d]
        out_ref[...] = tot

    return pl.pallas_call(
        body, name="small_allreduce",
        in_specs=[pl.BlockSpec(memory_space=pltpu.VMEM)], out_specs=pl.BlockSpec(memory_space=pltpu.VMEM),
        out_shape=jax.ShapeDtypeStruct(buf.shape, F32),
        scratch_shapes=[pltpu.VMEM((8, rows, LANES), F32), pltpu.SemaphoreType.DMA((7,)),
                        pltpu.SemaphoreType.DMA((7,))],
    )(buf)


def _pack(parts, rows):
    flat = jnp.concatenate([p.reshape(-1).astype(F32) for p in parts])
    return jnp.pad(flat, (0, rows * LANES - flat.shape[0])).reshape(rows, LANES)


def _pack_rows(parts):
    n = sum(math.prod(p.shape) for p in parts)
    return pl.cdiv(pl.cdiv(n, LANES), 8) * 8


def _unpack(buf, shapes):
    flat = buf.reshape(-1)
    out, off = [], 0
    for shp in shapes:
        size = math.prod(shp)
        out.append(flat[off:off + size].reshape(shp))
        off += size
    return out


def _pad_w_in(w):
    z = jnp.zeros((w.shape[0], 64), w.dtype)
    return jnp.concatenate([w[:, 448:1472], w[:, 1984:3008], w[:, 3008:4032], w[:, 0:256], w[:, 1472:1728],
                            w[:, 1728:1984], w[:, 256:384], w[:, 384:448], z], axis=1)


def _unpad_w_in(p):
    return jnp.concatenate([p[:, P_QLAT:P_QLAT + 256], p[:, P_CKV:P_CKV + 128], p[:, P_KR:P_KR + 64],
                            p[:, P_QB:P_QB + 1024], p[:, P_KB:P_KB + 256], p[:, P_VB:P_VB + 256],
                            p[:, P_GA:P_GA + 1024], p[:, P_GB:P_GB + 1024]], axis=1)


def _col_shards(w):
    r, c4 = w.shape
    return w.reshape(r, 4, c4 // 4).transpose(1, 0, 2)


def _heads_major(a, heads, hd):
    s = a.shape[0]
    return a.reshape(s, heads, hd).transpose(1, 0, 2)


def _heads_minor(a):
    h, s, hd = a.shape
    return a.transpose(1, 0, 2).reshape(s, h * hd)


def _local_step(x, positions, target, norm1_g, w_in_p, q_a_norm_g, wq, kv_a_norm_g, wkv, rel_bias, sinks,
                w_out, norm2_g, w_up, conv_w, conv_b, w_down, final_norm_g):
    s = x.shape[0]
    half = QK_ROPE // 2
    inv_freq = ROPE_THETA ** (-jnp.arange(half, dtype=F32) / half)
    ang = positions.astype(F32)[:, None] * inv_freq[None, :]
    cos, sin = jnp.cos(ang), jnp.sin(ang)
    z64 = jnp.zeros((s, 64), F32)
    cos_t = jnp.concatenate([cos, cos, z64], axis=1)
    sin_t = jnp.concatenate([-sin, sin, z64], axis=1)
    bucket = _t5_bucket_table()
    sinks1 = sinks.reshape(H_B)

    h1, rstd1 = _rmsnorm_fwd("norm1_fwd", x, norm1_g, D_MODEL, 0)
    proj = _matmul("proj", h1, w_in_p, out_shape=(s, W_IN_PAD), out_dtype=F32, grid=(s // 1024, W_IN_PAD // 512, 1),
                   a_spec=_bs((1024, D_MODEL), lambda i, j, k: (i, 0)), b_spec=_bs((D_MODEL, 512), lambda i, j, k: (0, j)),
                   o_spec=_bs((1024, 512), lambda i, j, k: (i, j)), contract=NN)
    qn, cn, rstd_q, rstd_c = _lat_norms(proj, q_a_norm_g, kv_a_norm_g)
    q = _q_heads(qn, wq, cos_t, sin_t)
    k, v = _kv_heads(cn, wkv, proj, cos_t, sin_t)
    o_a, lse_a = _mla_fwd(q, k, v)

    qh = _heads_major(proj[:, P_QB:P_QB + H_B * HD_B].astype(BF16), H_B, HD_B)
    kh = _heads_major(proj[:, P_KB:P_KB + KV_B * HD_B].astype(BF16), KV_B, HD_B)
    vh = _heads_major(proj[:, P_VB:P_VB + KV_B * HD_B].astype(BF16), KV_B, HD_B)
    bias = _win_bias(bucket, rel_bias)
    o_bh, lse_b = _win_fwd(qh, kh, vh, bias, sinks1)
    o_b = _heads_minor(o_bh)

    mixed = _gate_fwd(proj, o_a, o_b)
    row512 = lambda w: _bs((512, w), lambda i, j, k: (i, 0))
    whole = lambda r, c: _bs((r, c), lambda i, j, k: (0, 0))
    x1 = _matmul("attn_out", mixed, w_out, out_shape=(s, D_MODEL), out_dtype=F32, grid=(s // 512, 1, 1),
                 a_spec=row512(D_MODEL), b_spec=whole(D_MODEL, D_MODEL), o_spec=row512(D_MODEL), contract=NN, add=x)
    h2, rstd2 = _rmsnorm_fwd("norm2_fwd", x1, norm2_g, D_MODEL, 0)
    u = _matmul("ffn_up", h2, w_up, out_shape=(s, 2 * D_FF), out_dtype=F32, grid=(s // 1024, 2 * D_FF // 512, 1),
                a_spec=_bs((1024, D_MODEL), lambda i, j, k: (i, 0)), b_spec=_bs((D_MODEL, 512), lambda i, j, k: (0, j)),
                o_spec=_bs((1024, 512), lambda i, j, k: (i, j)), contract=NN)
    f = _convffn_fwd(u, conv_w, conv_b)
    x2 = _matmul("ffn_down", f, w_down, out_shape=(s, D_MODEL), out_dtype=F32, grid=(s // 512, 1, 1),
                 a_spec=row512(D_FF), b_spec=whole(D_FF, D_MODEL), o_spec=row512(D_MODEL), contract=NN, add=x1)
    loss, dx2, d_final_g = _final_loss(x2, target, final_norm_g.reshape(1, D_MODEL))

    df = _matmul("ffn_down_dx", dx2, w_down, out_shape=(s, D_FF), out_dtype=F32, grid=(s // 512, 2, 1),
                 a_spec=row512(D_MODEL), b_spec=_bs((D_FF // 2, D_MODEL), lambda i, j, k: (j, 0)),
                 o_spec=_bs((512, D_FF // 2), lambda i, j, k: (i, j)), contract=NT)
    d_w_down = _matmul("ffn_down_dw", f, dx2, out_shape=(D_FF, D_MODEL), out_dtype=F32, grid=(2, 1, s // 512),
                       a_spec=_bs((512, D_FF // 2), lambda i, j, k: (k, i)), b_spec=_bs((512, D_MODEL), lambda i, j, k: (k, 0)),
                       o_spec=_bs((D_FF // 2, D_MODEL), lambda i, j, k: (i, 0)), contract=TN)
    du, d_conv_w2, d_conv_b2 = _convffn_bwd(u, conv_w, conv_b, df)
    kc = D_FF // 2
    dh2 = _matmul("ffn_up_dx", du, w_up, out_shape=(s, D_MODEL), out_dtype=F32, grid=(s // 1024, 1, 4),
                  a_spec=_bs((None, 1024, kc), lambda i, j, k: (k // 2, i, k % 2)),
                  b_spec=_bs((D_MODEL, kc), lambda i, j, k: (0, k)),
                  o_spec=_bs((1024, D_MODEL), lambda i, j, k: (i, 0)), contract=NT)
    d_w_up = _matmul("ffn_up_dw", h2, du, out_shape=(D_MODEL, 2 * D_FF), out_dtype=F32, grid=(1, 4, s // 512),
                     a_spec=_bs((512, D_MODEL), lambda i, j, k: (k, 0)),
                     b_spec=_bs((None, 512, kc), lambda i, j, k: (j // 2, k, j % 2)),
                     o_spec=_bs((D_MODEL, kc), lambda i, j, k: (0, j)), contract=TN)
    dx1, d_norm2_g = _rmsnorm_bwd("norm2_bwd", dh2, x1, rstd2, norm2_g, D_MODEL, 0, F32, res=dx2)

    dmixed = _matmul("attn_out_dx", dx1, w_out, out_shape=(s, D_MODEL), out_dtype=F32, grid=(s // 512, 1, 1),
                     a_spec=row512(D_MODEL), b_spec=whole(D_MODEL, D_MODEL), o_spec=row512(D_MODEL), contract=NT)
    d_w_out = _matmul("attn_out_dw", mixed, dx1, out_shape=(D_MODEL, D_MODEL), out_dtype=F32, grid=(1, 1, s // 512),
                      a_spec=_bs((512, D_MODEL), lambda i, j, k: (k, 0)), b_spec=_bs((512, D_MODEL), lambda i, j, k: (k, 0)),
                      o_spec=whole(D_MODEL, D_MODEL), contract=TN)
    do_a, do_b, d_ga, d_gb = _gate_bwd(dmixed, proj, o_a, o_b)

    doh = _heads_major(do_b, H_B, HD_B)
    dqh, dkh_pad, dvh_pad, dbias, dsink_rows = _win_bwd(qh, kh, vh, bias, sinks1, doh, lse_b)
    wp = _win_param_grads(bucket, dbias, dsink_rows)[:, 0, :]
    d_rel_bias = wp[:, :NUM_BUCKETS].T
    d_sinks = wp[:, NUM_BUCKETS].reshape(1, H_B)
    d_qb = _heads_minor(dqh).astype(BF16)
    d_kb = _heads_minor(dkh_pad[:, WINDOW:WINDOW + s]).astype(BF16)
    d_vb = _heads_minor(dvh_pad[:, WINDOW:WINDOW + s]).astype(BF16)

    dq, dk, dv = _mla_bwd(q, k, v, do_a, o_a, lse_a)
    dq_pre, dkv_pre, d_kr = _mla_bwd_prep(dq, dk, dv, cos_t, sin_t)
    hgrid = (s // 512, 1, H_A)
    hblock = _bs((None, 512, HEAD_PAD), lambda i, j, k: (k, i, 0))
    dqn = _matmul("q_up_dx", dq_pre, wq, out_shape=(s, Q_LORA), out_dtype=F32, grid=hgrid, a_spec=hblock,
                  b_spec=_bs((None, Q_LORA, HEAD_PAD), lambda i, j, k: (k, 0, 0)), o_spec=row512(Q_LORA), contract=NT)
    dcn = _matmul("kv_up_dx", dkv_pre, wkv, out_shape=(s, KV_LORA), out_dtype=F32, grid=hgrid, a_spec=hblock,
                  b_spec=_bs((None, KV_LORA, HEAD_PAD), lambda i, j, k: (k, 0, 0)), o_spec=row512(KV_LORA), contract=NT)
    wgrid = (H_A, 1, s // 512)
    d_wq = _matmul("q_up_dw", qn, dq_pre, out_shape=(H_A, Q_LORA, HEAD_PAD), out_dtype=F32, grid=wgrid,
                   a_spec=_bs((512, Q_LORA), lambda i, j, k: (k, 0)), b_spec=_bs((None, 512, HEAD_PAD), lambda i, j, k: (i, k, 0)),
                   o_spec=_bs((None, Q_LORA, HEAD_PAD), lambda i, j, k: (i, 0, 0)), contract=TN)
    d_wkv = _matmul("kv_up_dw", cn, dkv_pre, out_shape=(H_A, KV_LORA, HEAD_PAD), out_dtype=F32, grid=wgrid,
                    a_spec=_bs((512, KV_LORA), lambda i, j, k: (k, 0)), b_spec=_bs((None, 512, HEAD_PAD), lambda i, j, k: (i, k, 0)),
                    o_spec=_bs((None, KV_LORA, HEAD_PAD), lambda i, j, k: (i, 0, 0)), contract=TN)
    d_qlat, d_gq = _rmsnorm_bwd("q_norm_bwd", dqn, proj, rstd_q, q_a_norm_g, Q_LORA, P_QLAT // Q_LORA, BF16)
    d_ckv, d_gkv = _rmsnorm_bwd("kv_norm_bwd", dcn, proj, rstd_c, kv_a_norm_g, KV_LORA, P_CKV // KV_LORA, BF16)

    dproj = jnp.concatenate([d_qb, d_ga, d_gb, d_qlat, d_kb, d_vb, d_ckv, d_kr], axis=1)
    dh1 = _matmul("proj_dx", dproj, w_in_p, out_shape=(s, D_MODEL), out_dtype=F32, grid=(s // 1024, 1, W_IN_PAD // 1024),
                  a_spec=_bs((1024, 1024), lambda i, j, k: (i, k)), b_spec=_bs((D_MODEL, 1024), lambda i, j, k: (0, k)),
                  o_spec=_bs((1024, D_MODEL), lambda i, j, k: (i, 0)), contract=NT)
    d_w_in_p = _matmul("proj_dw", h1, dproj, out_shape=(D_MODEL, W_IN_PAD), out_dtype=F32, grid=(1, W_IN_PAD // 1024, s // 512),
                       a_spec=_bs((512, D_MODEL), lambda i, j, k: (k, 0)), b_spec=_bs((512, 1024), lambda i, j, k: (k, j)),
                       o_spec=_bs((D_MODEL, 1024), lambda i, j, k: (0, j)), contract=TN)
    dx, d_norm1_g = _rmsnorm_bwd("norm1_bwd", dh1, x, rstd1, norm1_g, D_MODEL, 0, F32, res=dx1)

    grads = dict(
        norm1_g=d_norm1_g, w_in_p=d_w_in_p, q_a_norm_g=d_gq, wq=d_wq, kv_a_norm_g=d_gkv, wkv=d_wkv,
        rel_bias=d_rel_bias, sinks=d_sinks, w_out=d_w_out, norm2_g=d_norm2_g, w_up=d_w_up,
        conv_w=jnp.concatenate([d_conv_w2[0], d_conv_w2[1]], axis=1),
        conv_b=jnp.concatenate([d_conv_b2[0], d_conv_b2[1]], axis=1),
        w_down=d_w_down, final_norm_g=d_final_g.reshape(D_MODEL))
    return loss, dx, grads


def _wq_heads(w_q_b):
    w = w_q_b.reshape(Q_LORA, H_A, QK_NOPE + QK_ROPE).transpose(1, 0, 2)
    return jnp.pad(w, ((0, 0), (0, 0), (0, HEAD_PAD - QK_NOPE - QK_ROPE)))


def _wq_unheads(d_wq):
    return d_wq[:, :, :QK_NOPE + QK_ROPE].transpose(1, 0, 2).reshape(Q_LORA, H_A * (QK_NOPE + QK_ROPE))


def _wkv_heads(w_kv_b):
    return w_kv_b.reshape(KV_LORA, H_A, QK_NOPE + V_DIM).transpose(1, 0, 2)


def _wkv_unheads(d_wkv):
    return d_wkv.transpose(1, 0, 2).reshape(KV_LORA, H_A * (QK_NOPE + V_DIM))


SMALL = ("norm1_g", "q_a_norm_g", "kv_a_norm_g", "rel_bias", "sinks", "norm2_g", "conv_b", "final_norm_g")
BIG = ("w_in", "w_q_b", "w_kv_b", "w_out", "w_up", "w_down")


def kernel(x, positions, norm1_g, w_in, q_a_norm_g, w_q_b, kv_a_norm_g, w_kv_b, rel_bias, sinks, w_out, norm2_g, w_up, conv_w, conv_b, w_down, final_norm_g, loss_target, m_norm1_g, m_w_in, m_q_a_norm_g, m_w_q_b, m_kv_a_norm_g, m_w_kv_b, m_rel_bias, m_sinks, m_w_out, m_norm2_g, m_w_up, m_conv_w, m_conv_b, m_w_down, m_final_norm_g, v_norm1_g, v_w_in, v_q_a_norm_g, v_w_q_b, v_kv_a_norm_g, v_w_kv_b, v_rel_bias, v_sinks, v_w_out, v_norm2_g, v_w_up, v_conv_w, v_conv_b, v_w_down, v_final_norm_g):
    weights = dict(norm1_g=norm1_g, w_in=w_in, q_a_norm_g=q_a_norm_g, w_q_b=w_q_b, kv_a_norm_g=kv_a_norm_g,
                   w_kv_b=w_kv_b, rel_bias=rel_bias, sinks=sinks, w_out=w_out, norm2_g=norm2_g, w_up=w_up,
                   conv_w=conv_w, conv_b=conv_b, w_down=w_down, final_norm_g=final_norm_g)
    mom_m = dict(norm1_g=m_norm1_g, w_in=m_w_in, q_a_norm_g=m_q_a_norm_g, w_q_b=m_w_q_b, kv_a_norm_g=m_kv_a_norm_g,
                 w_kv_b=m_w_kv_b, rel_bias=m_rel_bias, sinks=m_sinks, w_out=m_w_out, norm2_g=m_norm2_g, w_up=m_w_up,
                 conv_w=m_conv_w, conv_b=m_conv_b, w_down=m_w_down, final_norm_g=m_final_norm_g)
    mom_v = dict(norm1_g=v_norm1_g, w_in=v_w_in, q_a_norm_g=v_q_a_norm_g, w_q_b=v_w_q_b, kv_a_norm_g=v_kv_a_norm_g,
                 w_kv_b=v_w_kv_b, rel_bias=v_rel_bias, sinks=v_sinks, w_out=v_w_out, norm2_g=v_norm2_g, w_up=v_w_up,
                 conv_w=v_conv_w, conv_b=v_conv_b, w_down=v_w_down, final_norm_g=v_final_norm_g)
    shard2d = {n: weights[n][0] for n in BIG}
    conv_w_shard = conv_w[0]
    xi, yi, ci = lax.axis_index("x"), lax.axis_index("y"), lax.axis_index("c")
    chip = (2 * xi + yi).astype(jnp.int32)

    send = [shard2d[n].astype(BF16) for n in BIG] + [conv_w_shard]
    gathered = _allgather_weights(send, split=[True] * len(BIG) + [False])
    g = dict(zip(BIG + ("conv_w",), gathered))
    cat_cols = lambda a: jnp.concatenate([a[0], a[1], a[2], a[3]], axis=1)
    w_in_p = _pad_w_in(cat_cols(g["w_in"]))
    wq = _wq_heads(cat_cols(g["w_q_b"]))
    wkv = _wkv_heads(cat_cols(g["w_kv_b"]))
    w_out_f = g["w_out"].reshape(D_MODEL, D_MODEL)
    w_up_f = cat_cols(g["w_up"])
    conv_w_f = cat_cols(g["conv_w"])
    w_down_f = g["w_down"].reshape(D_FF, D_MODEL)

    loss, dx, gr = _local_step(x[0], positions, loss_target[0], norm1_g, w_in_p, q_a_norm_g, wq, kv_a_norm_g, wkv,
                               rel_bias, sinks, w_out_f, norm2_g, w_up_f, conv_w_f, conv_b, w_down_f, final_norm_g)

    big_grads = [
        _col_shards(_unpad_w_in(gr["w_in_p"])),
        _col_shards(_wq_unheads(gr["wq"])),
        _col_shards(_wkv_unheads(gr["wkv"])),
        gr["w_out"].reshape(4, D_MODEL // 4, D_MODEL),
        _col_shards(gr["w_up"]),
        gr["w_down"].reshape(4, D_FF // 4, D_MODEL),
    ]
    core = ci.astype(jnp.int32).reshape(1)
    chip1 = chip.reshape(1)
    recv1 = _rs_pair_exchange(big_grads)
    pairs = [_rs_pair_add(f"rs_pair_add_{n}", core, gfull, r) for n, gfull, r in zip(BIG, big_grads, recv1)]
    recv2 = _rs_ici(pairs)
    halves = [_rs_final_add(f"rs_final_add_{n}", chip1, pr, r) for n, pr, r in zip(BIG, pairs, recv2)]
    reduced = dict(zip(BIG, _rs_pair_share(halves)))

    small_parts = [gr[n] for n in SMALL] + [gr["conv_w"], loss]
    rows = _pack_rows(small_parts)
    summed = _unpack(_small_allreduce(_pack(small_parts, rows)), [p.shape for p in small_parts])
    small_g = dict(zip(SMALL, summed[:len(SMALL)]))
    conv_w_g = lax.dynamic_slice_in_dim(summed[len(SMALL)], chip * (2 * D_FF // 4), 2 * D_FF // 4, axis=1)
    loss_out = summed[-1].reshape(())

    out_g, out_d, out_m, out_v = {}, {}, {}, {}
    for n in BIG:
        gsh = reduced[n]
        d, nm, nv = _adamw(f"adamw_{n}", shard2d[n], gsh, mom_m[n][0], mom_v[n][0])
        out_g[n], out_d[n], out_m[n], out_v[n] = gsh[None], d[None], nm[None], nv[None]
    names = SMALL + ("conv_w",)
    shapes = [weights[n].shape for n in names]
    sg = [small_g[n].reshape(weights[n].shape) for n in SMALL] + [conv_w_g[None]]
    prow = _pack_rows([weights[n] for n in names])
    d, nm, nv = _adamw("adamw_small", _pack([weights[n] for n in names], prow), _pack(sg, prow),
                       _pack([mom_m[n] for n in names], prow), _pack([mom_v[n] for n in names], prow))
    for n, gg, dd, mm, vv in zip(names, sg, _unpack(d, shapes), _unpack(nm, shapes), _unpack(nv, shapes)):
        out_g[n], out_d[n], out_m[n], out_v[n] = gg, dd, mm, vv

    order = ("norm1_g", "w_in", "q_a_norm_g", "w_q_b", "kv_a_norm_g", "w_kv_b", "rel_bias", "sinks", "w_out",
             "norm2_g", "w_up", "conv_w", "conv_b", "w_down", "final_norm_g")
    return (loss_out, dx[None], *[out_g[n] for n in order], *[out_d[n] for n in order],
            *[out_m[n] for n in order], *[out_v[n] for n in order])
```

```python
import functools
import math

import jax
import jax.numpy as jnp
from jax import lax
from jax.experimental import pallas as pl
from jax.experimental.pallas import tpu as pltpu

F32 = jnp.float32
BF16 = jnp.bfloat16
MESH = pl.DeviceIdType.MESH

D_MODEL = 1024
EPS = 1e-6
H_A = 8
QK_NOPE = 128
QK_ROPE = 64
V_DIM = 128
Q_LORA = 256
KV_LORA = 128
ROPE_THETA = 10000.0
H_B = 16
KV_B = 4
GROUP = 4
HD_B = 64
WINDOW = 128
Q_BLOCK = 128
NUM_BUCKETS = 32
MAX_DISTANCE = 128
D_FF = 2816
HEAD_PAD = 256

ADAM_LR = 0.001
ADAM_B1 = 0.9
ADAM_B2 = 0.999
ADAM_EPS = 1e-08
ADAM_WD = 0.01
ADAM_STEP = 10

LANES = 128
P_QB, P_GA, P_GB, P_QLAT, P_KB, P_VB, P_CKV, P_KR = 0, 1024, 2048, 3072, 3328, 3584, 3840, 3968
W_IN_PAD = 4096

NT = (((1,), (1,)), ((), ()))
NN = (((1,), (0,)), ((), ()))
TN = (((0,), (0,)), ((), ()))


def _arb(n):
    return pltpu.CompilerParams(dimension_semantics=("arbitrary",) * n)


def _matmul(name, a, b, *, out_shape, out_dtype, grid, a_spec, b_spec, o_spec, contract, add=None):
    nk = grid[2]
    acc_shape = tuple(d for d in o_spec.block_shape if d is not None)

    def body(*refs):
        if add is not None:
            a_ref, b_ref, add_ref, o_ref = refs[:4]
            scratch = refs[4:]
        else:
            a_ref, b_ref, o_ref = refs[:3]
            add_ref = None
            scratch = refs[3:]
        prod = lax.dot_general(a_ref[...].astype(BF16), b_ref[...].astype(BF16), contract,
                               preferred_element_type=F32)

        def finish(val):
            if add_ref is not None:
                val = add_ref[...] + val
            o_ref[...] = val.astype(out_dtype)

        if nk == 1:
            finish(prod)
        else:
            acc_ref = scratch[0]
            k = pl.program_id(2)

            @pl.when(k == 0)
            def _():
                acc_ref[...] = prod

            @pl.when(k > 0)
            def _():
                acc_ref[...] += prod

            @pl.when(k == nk - 1)
            def _():
                finish(acc_ref[...])

    in_specs = [a_spec, b_spec]
    args = [a, b]
    if add is not None:
        in_specs.append(o_spec)
        args.append(add)
    return pl.pallas_call(
        body, name=name, grid=grid, in_specs=in_specs, out_specs=o_spec,
        out_shape=jax.ShapeDtypeStruct(out_shape, out_dtype),
        scratch_shapes=[pltpu.VMEM(acc_shape, F32)] if nk > 1 else [],
        compiler_params=_arb(3),
    )(*args)


def _bs(block, fn):
    return pl.BlockSpec(block, fn)


def _rmsnorm_fwd(name, src, g, d, cb, ts=512):
    s = src.shape[0]

    def body(x_ref, g_ref, h_ref, r_ref):
        x = x_ref[...]
        r = lax.rsqrt(jnp.mean(x * x, axis=-1, keepdims=True) + EPS)
        h_ref[...] = (x * r * g_ref[...]).astype(BF16)
        r_ref[...] = r

    return pl.pallas_call(
        body, name=name, grid=(s // ts,),
        in_specs=[_bs((ts, d), lambda i: (i, cb)), _bs((1, d), lambda i: (0, 0))],
        out_specs=[_bs((ts, d), lambda i: (i, 0)), _bs((ts, 1), lambda i: (i, 0))],
        out_shape=[jax.ShapeDtypeStruct((s, d), BF16), jax.ShapeDtypeStruct((s, 1), F32)],
        compiler_params=_arb(1),
    )(src, g)


def _rmsnorm_bwd(name, dy, src, rstd, g, d, cb, out_dtype, res=None, ts=512):
    s = src.shape[0]

    def body(*refs):
        if res is not None:
            dy_ref, x_ref, r_ref, g_ref, res_ref, dx_ref, dg_ref = refs
        else:
            dy_ref, x_ref, r_ref, g_ref, dx_ref, dg_ref = refs
            res_ref = None
        dyv = dy_ref[...]
        r = r_ref[...]
        xhat = x_ref[...] * r
        dyh = dyv * g_ref[...]
        c = jnp.mean(dyh * xhat, axis=-1, keepdims=True)
        dx = r * (dyh - xhat * c)
        if res_ref is not None:
            dx = res_ref[...] + dx
        dx_ref[...] = dx.astype(out_dtype)
        part = jnp.sum(dyv * xhat, axis=0, keepdims=True)

        @pl.when(pl.program_id(0) == 0)
        def _():
            dg_ref[...] = part

        @pl.when(pl.program_id(0) > 0)
        def _():
            dg_ref[...] += part

    in_specs = [_bs((ts, d), lambda i: (i, 0)), _bs((ts, d), lambda i: (i, cb)),
                _bs((ts, 1), lambda i: (i, 0)), _bs((1, d), lambda i: (0, 0))]
    args = [dy, src, rstd, g]
    if res is not None:
        in_specs.append(_bs((ts, d), lambda i: (i, 0)))
        args.append(res)
    return pl.pallas_call(
        body, name=name, grid=(s // ts,), in_specs=in_specs,
        out_specs=[_bs((ts, d), lambda i: (i, 0)), _bs((1, d), lambda i: (0, 0))],
        out_shape=[jax.ShapeDtypeStruct((s, d), out_dtype), jax.ShapeDtypeStruct((1, d), F32)],
        compiler_params=_arb(1),
    )(*args)


def _final_loss(x2, target, g, ts=512):
    s, d = x2.shape

    def body(x_ref, t_ref, g_ref, loss_ref, dx_ref, dg_ref):
        x = x_ref[...]
        r = lax.rsqrt(jnp.mean(x * x, axis=-1, keepdims=True) + EPS)
        xhat = x * r
        gv = g_ref[...]
        err = xhat * gv - t_ref[...]
        lpart = 0.5 * jnp.sum(jnp.mean(err * err, axis=-1, keepdims=True), axis=0, keepdims=True)
        dyv = err * (1.0 / d)
        dyh = dyv * gv
        c = jnp.mean(dyh * xhat, axis=-1, keepdims=True)
        dx_ref[...] = r * (dyh - xhat * c)
        gpart = jnp.sum(dyv * xhat, axis=0, keepdims=True)

        @pl.when(pl.program_id(0) == 0)
        def _():
            dg_ref[...] = gpart
            loss_ref[...] = lpart

        @pl.when(pl.program_id(0) > 0)
        def _():
            dg_ref[...] += gpart
            loss_ref[...] += lpart

    return pl.pallas_call(
        body, name="final_loss", grid=(s // ts,),
        in_specs=[_bs((ts, d), lambda i: (i, 0)), _bs((ts, d), lambda i: (i, 0)), _bs((1, d), lambda i: (0, 0))],
        out_specs=[_bs((1, 1), lambda i: (0, 0)), _bs((ts, d), lambda i: (i, 0)), _bs((1, d), lambda i: (0, 0))],
        out_shape=[jax.ShapeDtypeStruct((1, 1), F32), jax.ShapeDtypeStruct((s, d), F32),
                   jax.ShapeDtypeStruct((1, d), F32)],
        compiler_params=_arb(1),
    )(x2, target, g)


def _swap_halves(t):
    lane = lax.broadcasted_iota(jnp.int32, t.shape, 1)
    return jnp.where(lane < 32, pltpu.roll(t, 96, 1), pltpu.roll(t, 32, 1))


def _rope_fwd(t, cos_t, sin_t):
    return t * cos_t + _swap_halves(t) * sin_t


def _rope_bwd(dt, cos_t, sin_t):
    return dt * cos_t - _swap_halves(dt) * sin_t


def _lat_norms(proj, gq, gkv, ts=512):
    s = proj.shape[0]

    def body(q_ref, c_ref, gq_ref, gkv_ref, qn_ref, cn_ref, rq_ref, rc_ref):
        q = q_ref[...]
        rq = lax.rsqrt(jnp.mean(q * q, axis=-1, keepdims=True) + EPS)
        qn_ref[...] = (q * rq * gq_ref[...]).astype(BF16)
        rq_ref[...] = rq
        cv = c_ref[...]
        rc = lax.rsqrt(jnp.mean(cv * cv, axis=-1, keepdims=True) + EPS)
        cn_ref[...] = (cv * rc * gkv_ref[...]).astype(BF16)
        rc_ref[...] = rc

    return pl.pallas_call(
        body, name="lat_norms", grid=(s // ts,),
        in_specs=[_bs((ts, Q_LORA), lambda i: (i, P_QLAT // Q_LORA)),
                  _bs((ts, KV_LORA), lambda i: (i, P_CKV // KV_LORA)),
                  _bs((1, Q_LORA), lambda i: (0, 0)), _bs((1, KV_LORA), lambda i: (0, 0))],
        out_specs=[_bs((ts, Q_LORA), lambda i: (i, 0)), _bs((ts, KV_LORA), lambda i: (i, 0)),
                   _bs((ts, 1), lambda i: (i, 0)), _bs((ts, 1), lambda i: (i, 0))],
        out_shape=[jax.ShapeDtypeStruct((s, Q_LORA), BF16), jax.ShapeDtypeStruct((s, KV_LORA), BF16),
                   jax.ShapeDtypeStruct((s, 1), F32), jax.ShapeDtypeStruct((s, 1), F32)],
        compiler_params=_arb(1),
    )(proj, proj, gq, gkv)


HEAD_ROWS = 2048


def _q_heads(qn, wq, cos_t, sin_t):
    s = qn.shape[0]
    ts = min(s, HEAD_ROWS)

    def body(qn_ref, w_ref, cos_ref, sin_ref, q_ref):
        o = jnp.dot(qn_ref[...], w_ref[...], preferred_element_type=F32)
        q_ref[:, :LANES] = o[:, :LANES].astype(BF16)
        q_ref[:, LANES:] = _rope_fwd(o[:, LANES:], cos_ref[...], sin_ref[...]).astype(BF16)

    return pl.pallas_call(
        body, name="q_heads", grid=(H_A, s // ts),
        in_specs=[_bs((ts, Q_LORA), lambda h, i: (i, 0)), _bs((None, Q_LORA, HEAD_PAD), lambda h, i: (h, 0, 0)),
                  _bs((ts, LANES), lambda h, i: (i, 0)), _bs((ts, LANES), lambda h, i: (i, 0))],
        out_specs=_bs((None, ts, HEAD_PAD), lambda h, i: (h, i, 0)),
        out_shape=jax.ShapeDtypeStruct((H_A, s, HEAD_PAD), BF16),
        compiler_params=_arb(2),
    )(qn, wq, cos_t, sin_t)


def _kv_heads(cn, wkv, proj, cos_t, sin_t):
    s = cn.shape[0]
    ts = min(s, HEAD_ROWS)

    def body(cn_ref, w_ref, kr_ref, cos_ref, sin_ref, k_ref, v_ref):
        o = jnp.dot(cn_ref[...], w_ref[...], preferred_element_type=F32)
        k_ref[:, :LANES] = o[:, :LANES].astype(BF16)
        k_ref[:, LANES:] = _rope_fwd(kr_ref[...], cos_ref[...], sin_ref[...]).astype(BF16)
        v_ref[...] = o[:, LANES:].astype(BF16)

    return pl.pallas_call(
        body, name="kv_heads", grid=(H_A, s // ts),
        in_specs=[_bs((ts, KV_LORA), lambda h, i: (i, 0)),
                  _bs((None, KV_LORA, QK_NOPE + V_DIM), lambda h, i: (h, 0, 0)),
                  _bs((ts, LANES), lambda h, i: (i, P_KR // LANES)),
                  _bs((ts, LANES), lambda h, i: (i, 0)), _bs((ts, LANES), lambda h, i: (i, 0))],
        out_specs=[_bs((None, ts, HEAD_PAD), lambda h, i: (h, i, 0)), _bs((None, ts, V_DIM), lambda h, i: (h, i, 0))],
        out_shape=[jax.ShapeDtypeStruct((H_A, s, HEAD_PAD), BF16), jax.ShapeDtypeStruct((H_A, s, V_DIM), BF16)],
        compiler_params=_arb(2),
    )(cn, wkv, proj, cos_t, sin_t)


MLA_SCALE = 1.0 / math.sqrt(QK_NOPE + QK_ROPE)
LOG2E = math.log2(math.e)
MLA_EXP2_SCALE = MLA_SCALE * LOG2E


def _lane_tiles(a):
    return [a[:, j * LANES:(j + 1) * LANES] for j in range(a.shape[1] // LANES)]


def _mla_fwd(q, k, v, tq=512, tk=512):
    s = q.shape[1]
    nk = s // tk

    def body(q_ref, k_ref, v_ref, o_ref, lse_ref, m_ref, l_ref, acc_ref):
        m_ref[...] = jnp.full(m_ref.shape, -jnp.inf, F32)
        l_ref[...] = jnp.zeros(l_ref.shape, F32)
        acc_ref[...] = jnp.zeros(acc_ref.shape, F32)
        qv = q_ref[...]

        def step(c, carry):
            rows = pl.ds(pl.multiple_of(c * tk, tk), tk)
            raw = lax.dot_general(qv, k_ref[rows, :], NT, preferred_element_type=F32)
            m_prev = m_ref[...]
            m_new = jnp.maximum(m_prev, jnp.max(raw, axis=-1, keepdims=True))
            alpha = jnp.exp2((m_prev - m_new) * MLA_EXP2_SCALE)
            ps = [jnp.exp2((t - m_new) * MLA_EXP2_SCALE) for t in _lane_tiles(raw)]
            l_ref[...] = alpha * l_ref[...] + functools.reduce(lambda a, b: a + b, ps)
            p = jnp.concatenate(ps, axis=1).astype(BF16)
            acc_ref[...] = alpha * acc_ref[...] + jnp.dot(p, v_ref[rows, :], preferred_element_type=F32)
            m_ref[...] = m_new
            return carry

        lax.fori_loop(0, nk, step, 0, unroll=True)
        l = jnp.sum(l_ref[...], axis=-1, keepdims=True)
        o_ref[...] = acc_ref[...] / l
        lse_ref[...] = m_ref[...] * MLA_SCALE + jnp.log(l)

    return pl.pallas_call(
        body, name="mla_fwd", grid=(H_A, s // tq),
        in_specs=[_bs((None, tq, HEAD_PAD), lambda h, i: (h, i, 0)),
                  _bs((None, s, HEAD_PAD), lambda h, i: (h, 0, 0)),
                  _bs((None, s, V_DIM), lambda h, i: (h, 0, 0))],
        out_specs=[_bs((tq, V_DIM), lambda h, i: (i, h)), _bs((None, tq, LANES), lambda h, i: (h, i, 0))],
        out_shape=[jax.ShapeDtypeStruct((s, H_A * V_DIM), F32), jax.ShapeDtypeStruct((H_A, s, LANES), F32)],
        scratch_shapes=[pltpu.VMEM((tq, LANES), F32), pltpu.VMEM((tq, LANES), F32), pltpu.VMEM((tq, V_DIM), F32)],
        compiler_params=_arb(2),
    )(q, k, v)


def _mla_bwd(q, k, v, do, o, lse, tq=512, tk=512):
    s = q.shape[1]
    nq = s // tq

    def body(q_ref, k_ref, v_ref, do_ref, o_ref, lse_ref, dq_ref, dk_ref, dv_ref, delta_ref):
        @pl.when(pl.program_id(1) == 0)
        def _():
            def init(c, carry):
                rows = pl.ds(pl.multiple_of(c * tq, tq), tq)
                delta = jnp.sum(do_ref[rows, :] * o_ref[rows, :], axis=-1, keepdims=True)
                delta_ref[rows, :] = jnp.broadcast_to(delta, (tq, LANES))
                dq_ref[rows, :] = jnp.zeros((tq, HEAD_PAD), F32)
                return carry

            lax.fori_loop(0, nq, init, 0)

        dk_ref[...] = jnp.zeros(dk_ref.shape, F32)
        dv_ref[...] = jnp.zeros(dv_ref.shape, F32)
        kb = k_ref[...]
        vb = v_ref[...]

        def step(c, carry):
            rows = pl.ds(pl.multiple_of(c * tq, tq), tq)
            qc = q_ref[rows, :]
            doc = do_ref[rows, :].astype(BF16)
            raw = lax.dot_general(qc, kb, NT, preferred_element_type=F32)
            dp = lax.dot_general(doc, vb, NT, preferred_element_type=F32)
            lse2 = lse_ref[rows, :] * LOG2E
            delta = delta_ref[rows, :]
            ps = [jnp.exp2(t * MLA_EXP2_SCALE - lse2) for t in _lane_tiles(raw)]
            dss = [pj * (dj - delta) * MLA_SCALE for pj, dj in zip(ps, _lane_tiles(dp))]
            p = jnp.concatenate(ps, axis=1).astype(BF16)
            ds = jnp.concatenate(dss, axis=1).astype(BF16)
            dv_ref[...] += lax.dot_general(p, doc, TN, preferred_element_type=F32)
            dk_ref[...] += lax.dot_general(ds, qc, TN, preferred_element_type=F32)
            dq_ref[rows, :] += jnp.dot(ds, kb, preferred_element_type=F32)
            return carry

        lax.fori_loop(0, nq, step, 0, unroll=True)

    return pl.pallas_call(
        body, name="mla_bwd", grid=(H_A, s // tk),
        in_specs=[_bs((None, s, HEAD_PAD), lambda h, j: (h, 0, 0)),
                  _bs((None, tk, HEAD_PAD), lambda h, j: (h, j, 0)),
                  _bs((None, tk, V_DIM), lambda h, j: (h, j, 0)),
                  _bs((s, V_DIM), lambda h, j: (0, h)), _bs((s, V_DIM), lambda h, j: (0, h)),
                  _bs((None, s, LANES), lambda h, j: (h, 0, 0))],
        out_specs=[_bs((None, s, HEAD_PAD), lambda h, j: (h, 0, 0)),
                   _bs((None, tk, HEAD_PAD), lambda h, j: (h, j, 0)),
                   _bs((None, tk, V_DIM), lambda h, j: (h, j, 0))],
        out_shape=[jax.ShapeDtypeStruct((H_A, s, HEAD_PAD), F32), jax.ShapeDtypeStruct((H_A, s, HEAD_PAD), F32),
                   jax.ShapeDtypeStruct((H_A, s, V_DIM), F32)],
        scratch_shapes=[pltpu.VMEM((s, LANES), F32)],
        compiler_params=_arb(2),
    )(q, k, v, do, o, lse)


def _mla_bwd_prep(dq, dk, dv, cos_t, sin_t, ts=256):
    s = dq.shape[1]

    def body(dq_ref, dk_ref, dv_ref, cos_ref, sin_ref, dqp_ref, dkvp_ref, dkr_ref):
        cos_v = cos_ref[...]
        sin_v = sin_ref[...]
        kr = jnp.zeros((ts, LANES), F32)
        for h in range(H_A):
            dqp_ref[h, :, :LANES] = dq_ref[h, :, :LANES].astype(BF16)
            dqp_ref[h, :, LANES:] = _rope_bwd(dq_ref[h, :, LANES:], cos_v, sin_v).astype(BF16)
            dkvp_ref[h, :, :LANES] = dk_ref[h, :, :LANES].astype(BF16)
            dkvp_ref[h, :, LANES:] = dv_ref[h].astype(BF16)
            kr = kr + dk_ref[h, :, LANES:]
        dkr_ref[...] = _rope_bwd(kr, cos_v, sin_v).astype(BF16)

    blk3 = lambda w: _bs((H_A, ts, w), lambda i: (0, i, 0))
    return pl.pallas_call(
        body, name="mla_bwd_prep", grid=(s // ts,),
        in_specs=[blk3(HEAD_PAD), blk3(HEAD_PAD), blk3(V_DIM),
                  _bs((ts, LANES), lambda i: (i, 0)), _bs((ts, LANES), lambda i: (i, 0))],
        out_specs=[blk3(HEAD_PAD), blk3(HEAD_PAD), _bs((ts, LANES), lambda i: (i, 0))],
        out_shape=[jax.ShapeDtypeStruct((H_A, s, HEAD_PAD), BF16), jax.ShapeDtypeStruct((H_A, s, HEAD_PAD), BF16),
                   jax.ShapeDtypeStruct((s, LANES), BF16)],
        compiler_params=_arb(1),
    )(dq, dk, dv, cos_t, sin_t)


WIN_SCALE = 1.0 / math.sqrt(HD_B)
SPAN = Q_BLOCK + 2 * WINDOW


def _t5_bucket_table():
    a = jnp.arange(Q_BLOCK, dtype=jnp.int32)[:, None]
    c = jnp.arange(SPAN, dtype=jnp.int32)[None, :]
    rel = c - WINDOW - a
    nb = NUM_BUCKETS // 2
    max_exact = nb // 2
    base = (rel > 0).astype(jnp.int32) * nb
    n = jnp.abs(rel)
    nf = jnp.maximum(n, 1).astype(F32)
    large = max_exact + (jnp.log(nf / max_exact) / math.log(MAX_DISTANCE / max_exact)
                         * (nb - max_exact)).astype(jnp.int32)
    large = jnp.minimum(large, nb - 1)
    return base + jnp.where(n < max_exact, n, large)


def _win_bias(bucket, rel_bias):
    def body(rb_ref, bk_ref, o_ref):
        h = pl.program_id(0)
        bk = bk_ref[...]
        acc = jnp.zeros((Q_BLOCK, SPAN), F32)
        for b in range(NUM_BUCKETS):
            acc = jnp.where(bk == b, rb_ref[b, h], acc)
        o_ref[...] = acc

    return pl.pallas_call(
        body, name="win_bias", grid=(H_B,),
        in_specs=[pl.BlockSpec(memory_space=pltpu.SMEM), _bs((Q_BLOCK, SPAN), lambda h: (0, 0))],
        out_specs=_bs((None, Q_BLOCK, SPAN), lambda h: (h, 0, 0)),
        out_shape=jax.ShapeDtypeStruct((H_B, Q_BLOCK, SPAN), F32),
        compiler_params=_arb(1),
    )(rel_bias, bucket)


def _win_specs(nblk):
    prev = lambda kv, n: (kv, jnp.maximum(n - 1, 0), 0)
    cur = lambda kv, n: (kv, n, 0)
    nxt = lambda kv, n: (kv, jnp.minimum(n + 1, nblk - 1), 0)
    kvb = lambda fn: _bs((None, Q_BLOCK, HD_B), fn)
    return [kvb(prev), kvb(cur), kvb(nxt)]


def _win_scores(q, k_refs, bias_ref, n, nblk):
    a = lax.broadcasted_iota(jnp.int32, (GROUP, Q_BLOCK, Q_BLOCK), 1)
    cc = lax.broadcasted_iota(jnp.int32, (GROUP, Q_BLOCK, Q_BLOCK), 2)
    valid = [(cc >= a) & (n > 0), None, (cc <= a) & (n < nblk - 1)]
    out = []
    for j in range(3):
        sc = lax.dot_general(q, k_refs[j][...], NT, preferred_element_type=F32)
        sc = sc.reshape(GROUP, Q_BLOCK, Q_BLOCK) * WIN_SCALE + bias_ref[:, :, j * Q_BLOCK:(j + 1) * Q_BLOCK]
        if valid[j] is not None:
            sc = jnp.where(valid[j], sc, -1e30)
        out.append(sc)
    return out


def _win_sink(sink_ref, kv):
    hs = lax.broadcasted_iota(jnp.int32, (GROUP, Q_BLOCK, 1), 0)
    sk = jnp.zeros((GROUP, Q_BLOCK, 1), F32)
    for g in range(GROUP):
        sk = jnp.where(hs == g, sink_ref[kv * GROUP + g], sk)
    return sk


def _win_fwd(qh, kh, vh, bias, sinks):
    s = qh.shape[1]
    nblk = s // Q_BLOCK
    rows = GROUP * Q_BLOCK

    def body(sink_ref, q_ref, k0, k1, k2, v0, v1, v2, bias_ref, o_ref, lse_ref):
        kv = pl.program_id(0)
        n = pl.program_id(1)
        q = q_ref[...].reshape(rows, HD_B)
        ss = _win_scores(q, (k0, k1, k2), bias_ref, n, nblk)
        sk = _win_sink(sink_ref, kv)
        m = jnp.maximum(jnp.maximum(jnp.max(ss[0], axis=2, keepdims=True), jnp.max(ss[1], axis=2, keepdims=True)),
                        jnp.maximum(jnp.max(ss[2], axis=2, keepdims=True), sk))
        es = [jnp.exp(sc - m) for sc in ss]
        l = (jnp.sum(es[0], axis=2, keepdims=True) + jnp.sum(es[1], axis=2, keepdims=True)
             + jnp.sum(es[2], axis=2, keepdims=True) + jnp.exp(sk - m))
        acc = jnp.zeros((rows, HD_B), F32)
        for e, v_ref in zip(es, (v0, v1, v2)):
            p = (e / l).astype(BF16).reshape(rows, Q_BLOCK)
            acc = acc + jnp.dot(p, v_ref[...], preferred_element_type=F32)
        o_ref[...] = acc.reshape(GROUP, Q_BLOCK, HD_B)
        lse_ref[...] = m + jnp.log(l)

    qspec = _bs((GROUP, Q_BLOCK, HD_B), lambda kv, n: (kv, n, 0))
    return pl.pallas_call(
        body, name="win_fwd", grid=(KV_B, nblk),
        in_specs=[pl.BlockSpec(memory_space=pltpu.SMEM), qspec] + _win_specs(nblk) + _win_specs(nblk)
        + [_bs((GROUP, Q_BLOCK, SPAN), lambda kv, n: (kv, 0, 0))],
        out_specs=[qspec, _bs((GROUP, Q_BLOCK, 1), lambda kv, n: (kv, n, 0))],
        out_shape=[jax.ShapeDtypeStruct((H_B, s, HD_B), F32), jax.ShapeDtypeStruct((H_B, s, 1), F32)],
        compiler_params=_arb(2),
    )(sinks, qh, kh, kh, kh, vh, vh, vh, bias)


def _win_bwd(qh, kh, vh, bias, sinks, doh, lse):
    s = qh.shape[1]
    nblk = s // Q_BLOCK
    rows = GROUP * Q_BLOCK
    spad = s + 2 * WINDOW

    def body(sink_ref, q_ref, k0, k1, k2, v0, v1, v2, bias_ref, do_ref, lse_ref,
             dq_ref, dk_ref, dv_ref, db_ref, dsk_ref):
        kv = pl.program_id(0)
        n = pl.program_id(1)

        @pl.when(n == 0)
        def _():
            dk_ref[...] = jnp.zeros(dk_ref.shape, F32)
            dv_ref[...] = jnp.zeros(dv_ref.shape, F32)
            db_ref[...] = jnp.zeros(db_ref.shape, F32)
            dsk_ref[...] = jnp.zeros(dsk_ref.shape, F32)

        q = q_ref[...].reshape(rows, HD_B)
        dob = do_ref[...].reshape(rows, HD_B).astype(BF16)
        lse_v = lse_ref[...]
        ss = _win_scores(q, (k0, k1, k2), bias_ref, n, nblk)
        ps = [jnp.exp(sc - lse_v) for sc in ss]
        dps = [lax.dot_general(dob, v_ref[...], NT, preferred_element_type=F32).reshape(GROUP, Q_BLOCK, Q_BLOCK)
               for v_ref in (v0, v1, v2)]
        delta = (jnp.sum(ps[0] * dps[0], axis=2, keepdims=True) + jnp.sum(ps[1] * dps[1], axis=2, keepdims=True)
                 + jnp.sum(ps[2] * dps[2], axis=2, keepdims=True))
        dq = jnp.zeros((rows, HD_B), F32)
        for j, k_ref in enumerate((k0, k1, k2)):
            ds = ps[j] * (dps[j] - delta)
            db_ref[:, :, j * Q_BLOCK:(j + 1) * Q_BLOCK] += ds
            dsb = (ds * WIN_SCALE).astype(BF16).reshape(rows, Q_BLOCK)
            dq = dq + jnp.dot(dsb, k_ref[...], preferred_element_type=F32)
            krows = pl.ds(pl.multiple_of((n + j) * Q_BLOCK, Q_BLOCK), Q_BLOCK)
            dk_ref[krows, :] += lax.dot_general(dsb, q, TN, preferred_element_type=F32)
            dv_ref[krows, :] += lax.dot_general(ps[j].astype(BF16).reshape(rows, Q_BLOCK), dob, TN,
                                                preferred_element_type=F32)
        sk = _win_sink(sink_ref, kv)
        dsk_ref[...] += -(jnp.exp(sk - lse_v) * delta)
        dq_ref[...] = dq.reshape(GROUP, Q_BLOCK, HD_B)

    qspec = _bs((GROUP, Q_BLOCK, HD_B), lambda kv, n: (kv, n, 0))
    kacc = _bs((None, spad, HD_B), lambda kv, n: (kv, 0, 0))
    return pl.pallas_call(
        body, name="win_bwd", grid=(KV_B, nblk),
        in_specs=[pl.BlockSpec(memory_space=pltpu.SMEM), qspec] + _win_specs(nblk) + _win_specs(nblk)
        + [_bs((GROUP, Q_BLOCK, SPAN), lambda kv, n: (kv, 0, 0)), qspec,
           _bs((GROUP, Q_BLOCK, 1), lambda kv, n: (kv, n, 0))],
        out_specs=[qspec, kacc, kacc, _bs((GROUP, Q_BLOCK, SPAN), lambda kv, n: (kv, 0, 0)),
                   _bs((GROUP, Q_BLOCK, 1), lambda kv, n: (kv, 0, 0))],
        out_shape=[jax.ShapeDtypeStruct((H_B, s, HD_B), F32), jax.ShapeDtypeStruct((KV_B, spad, HD_B), F32),
                   jax.ShapeDtypeStruct((KV_B, spad, HD_B), F32), jax.ShapeDtypeStruct((H_B, Q_BLOCK, SPAN), F32),
                   jax.ShapeDtypeStruct((H_B, Q_BLOCK, 1), F32)],
        compiler_params=_arb(2),
    )(sinks, qh, kh, kh, kh, vh, vh, vh, bias, doh, lse)


def _win_param_grads(bucket, dbias, dsink_rows):
    def body(bk_ref, db_ref, ds_ref, o_ref):
        bk = bk_ref[...]
        dbv = db_ref[...]
        lane = lax.broadcasted_iota(jnp.int32, (1, LANES), 1)
        res = jnp.zeros((1, LANES), F32)
        for b in range(NUM_BUCKETS):
            tot = jnp.sum(jnp.sum(jnp.where(bk == b, dbv, 0.0), axis=1, keepdims=True), axis=0, keepdims=True)
            res = jnp.where(lane == b, tot, res)
        stot = jnp.sum(ds_ref[...], axis=0, keepdims=True)
        o_ref[...] = jnp.where(lane == NUM_BUCKETS, stot, res)

    return pl.pallas_call(
        body, name="win_param_grads", grid=(H_B,),
        in_specs=[_bs((Q_BLOCK, SPAN), lambda h: (0, 0)), _bs((None, Q_BLOCK, SPAN), lambda h: (h, 0, 0)),
                  _bs((None, Q_BLOCK, 1), lambda h: (h, 0, 0))],
        out_specs=_bs((None, 1, LANES), lambda h: (h, 0, 0)),
        out_shape=jax.ShapeDtypeStruct((H_B, 1, LANES), F32),
        compiler_params=_arb(1),
    )(bucket, dbias, dsink_rows)


def _gate_fwd(proj, o_a, o_b, ts=256):
    s = o_a.shape[0]
    wide = lambda cb: _bs((ts, D_MODEL), lambda i: (i, cb))

    def body(ga_ref, gb_ref, oa_ref, ob_ref, m_ref):
        m_ref[...] = (jax.nn.sigmoid(ga_ref[...]) * oa_ref[...]
                      + jax.nn.sigmoid(gb_ref[...]) * ob_ref[...]).astype(BF16)

    return pl.pallas_call(
        body, name="gate_fwd", grid=(s // ts,),
        in_specs=[wide(P_GA // D_MODEL), wide(P_GB // D_MODEL), wide(0), wide(0)],
        out_specs=wide(0), out_shape=jax.ShapeDtypeStruct((s, D_MODEL), BF16),
        compiler_params=_arb(1),
    )(proj, proj, o_a, o_b)


def _gate_bwd(dmixed, proj, o_a, o_b, ts=256):
    s = o_a.shape[0]
    wide = lambda cb: _bs((ts, D_MODEL), lambda i: (i, cb))

    def body(dm_ref, ga_ref, gb_ref, oa_ref, ob_ref, doa_ref, dob_ref, dga_ref, dgb_ref):
        dm = dm_ref[...]
        sa = jax.nn.sigmoid(ga_ref[...])
        sb = jax.nn.sigmoid(gb_ref[...])
        doa_ref[...] = dm * sa
        dob_ref[...] = dm * sb
        dga_ref[...] = (dm * oa_ref[...] * (sa * (1.0 - sa))).astype(BF16)
        dgb_ref[...] = (dm * ob_ref[...] * (sb * (1.0 - sb))).astype(BF16)

    return pl.pallas_call(
        body, name="gate_bwd", grid=(s // ts,),
        in_specs=[wide(0), wide(P_GA // D_MODEL), wide(P_GB // D_MODEL), wide(0), wide(0)],
        out_specs=[wide(0)] * 4,
        out_shape=[jax.ShapeDtypeStruct((s, D_MODEL), F32), jax.ShapeDtypeStruct((s, D_MODEL), F32),
                   jax.ShapeDtypeStruct((s, D_MODEL), BF16), jax.ShapeDtypeStruct((s, D_MODEL), BF16)],
        compiler_params=_arb(1),
    )(dmixed, proj, proj, o_a, o_b)


CONV_CHUNK = 512
N_SLAB = D_FF // LANES


def _shifted(ref, c, nchunks):
    r0 = c * CONV_CHUNK
    cur = ref[r0:r0 + CONV_CHUNK, :]
    row = lax.broadcasted_iota(jnp.int32, cur.shape, 0)
    before = ref[r0 - 8:r0, :][7:8, :] if c > 0 else jnp.zeros((1, LANES), F32)
    after = ref[r0 + CONV_CHUNK:r0 + CONV_CHUNK + 8, :][0:1, :] if c < nchunks - 1 else jnp.zeros((1, LANES), F32)
    prev = jnp.where(row == 0, before, pltpu.roll(cur, 1, 0))
    nxt = jnp.where(row == CONV_CHUNK - 1, after, pltpu.roll(cur, CONV_CHUNK - 1, 0))
    return prev, cur, nxt


def _conv_taps(ref, w_ref, b_ref, c, nchunks):
    prev, cur, nxt = _shifted(ref, c, nchunks)
    conv = prev * w_ref[0:1, :] + cur * w_ref[1:2, :] + nxt * w_ref[2:3, :] + b_ref[...]
    return conv, prev, cur, nxt


def _convffn_fwd(u, conv_w, conv_b):
    s = u.shape[0]
    nchunks = s // CONV_CHUNK

    def body(ug_ref, uv_ref, wg_ref, wv_ref, bg_ref, bv_ref, f_ref):
        for c in range(nchunks):
            cg = _conv_taps(ug_ref, wg_ref, bg_ref, c, nchunks)[0]
            cv = _conv_taps(uv_ref, wv_ref, bv_ref, c, nchunks)[0]
            f_ref[c * CONV_CHUNK:(c + 1) * CONV_CHUNK, :] = (cg * jax.nn.sigmoid(cg) * cv).astype(BF16)

    slab = lambda off: _bs((s, LANES), lambda j: (0, off + j))
    wsl = lambda off: _bs((3, LANES), lambda j: (0, off + j))
    bsl = lambda off: _bs((1, LANES), lambda j: (0, off + j))
    return pl.pallas_call(
        body, name="convffn_fwd", grid=(N_SLAB,),
        in_specs=[slab(0), slab(N_SLAB), wsl(0), wsl(N_SLAB), bsl(0), bsl(N_SLAB)],
        out_specs=slab(0), out_shape=jax.ShapeDtypeStruct((s, D_FF), BF16),
        compiler_params=_arb(1),
    )(u, u, conv_w, conv_w, conv_b, conv_b)


def _convffn_bwd(u, conv_w, conv_b, df):
    s = u.shape[0]
    nchunks = s // CONV_CHUNK

    def body(ug_ref, uv_ref, wg_ref, wv_ref, bg_ref, bv_ref, df_ref, du_ref, dw_ref, db_ref, dcg_ref, dcv_ref):
        dwg = [jnp.zeros((1, LANES), F32) for _ in range(3)]
        dwv = [jnp.zeros((1, LANES), F32) for _ in range(3)]
        dbg = jnp.zeros((1, LANES), F32)
        dbv = jnp.zeros((1, LANES), F32)
        for c in range(nchunks):
            rows = slice(c * CONV_CHUNK, (c + 1) * CONV_CHUNK)
            cg, gp, gc, gn = _conv_taps(ug_ref, wg_ref, bg_ref, c, nchunks)
            cv, vp, vc, vn = _conv_taps(uv_ref, wv_ref, bv_ref, c, nchunks)
            dfv = df_ref[rows, :]
            sg = jax.nn.sigmoid(cg)
            dcg = dfv * cv * (sg * (1.0 + cg * (1.0 - sg)))
            dcv = dfv * (cg * sg)
            dcg_ref[rows, :] = dcg
            dcv_ref[rows, :] = dcv
            for t, (tg, tv) in enumerate(((gp, vp), (gc, vc), (gn, vn))):
                dwg[t] = dwg[t] + jnp.sum(tg * dcg, axis=0, keepdims=True)
                dwv[t] = dwv[t] + jnp.sum(tv * dcv, axis=0, keepdims=True)
            dbg = dbg + jnp.sum(dcg, axis=0, keepdims=True)
            dbv = dbv + jnp.sum(dcv, axis=0, keepdims=True)
        for t in range(3):
            dw_ref[0, t:t + 1, :] = dwg[t]
            dw_ref[1, t:t + 1, :] = dwv[t]
        db_ref[0] = dbg
        db_ref[1] = dbv
        for half, (dc_ref, w_ref) in enumerate(((dcg_ref, wg_ref), (dcv_ref, wv_ref))):
            for c in range(nchunks):
                prev, cur, nxt = _shifted(dc_ref, c, nchunks)
                du = nxt * w_ref[0:1, :] + cur * w_ref[1:2, :] + prev * w_ref[2:3, :]
                du_ref[half, c * CONV_CHUNK:(c + 1) * CONV_CHUNK, :] = du.astype(BF16)

    slab = lambda off: _bs((s, LANES), lambda j: (0, off + j))
    wsl = lambda off: _bs((3, LANES), lambda j: (0, off + j))
    bsl = lambda off: _bs((1, LANES), lambda j: (0, off + j))
    return pl.pallas_call(
        body, name="convffn_bwd", grid=(N_SLAB,),
        in_specs=[slab(0), slab(N_SLAB), wsl(0), wsl(N_SLAB), bsl(0), bsl(N_SLAB), slab(0)],
        out_specs=[_bs((2, s, LANES), lambda j: (0, 0, j)), _bs((2, 3, LANES), lambda j: (0, 0, j)),
                   _bs((2, 1, LANES), lambda j: (0, 0, j))],
        out_shape=[jax.ShapeDtypeStruct((2, s, D_FF), BF16), jax.ShapeDtypeStruct((2, 3, D_FF), F32),
                   jax.ShapeDtypeStruct((2, 1, D_FF), F32)],
        scratch_shapes=[pltpu.VMEM((s, LANES), F32), pltpu.VMEM((s, LANES), F32)],
        compiler_params=_arb(1),
    )(u, u, conv_w, conv_w, conv_b, conv_b, df)


def _row_tile(rows, limit=512):
    best = rows
    for t in range(8, min(rows, limit) + 1, 8):
        if rows % t == 0:
            best = t
    return best if rows % 8 == 0 else rows


def _adamw(name, w, g, m, v):
    rows, cols = w.shape
    tr = _row_tile(rows)
    c1 = 1.0 - ADAM_B1 ** ADAM_STEP
    c2 = 1.0 - ADAM_B2 ** ADAM_STEP

    def body(w_ref, g_ref, m_ref, v_ref, d_ref, nm_ref, nv_ref):
        gv = g_ref[...]
        nm = ADAM_B1 * m_ref[...] + (1.0 - ADAM_B1) * gv
        nv = ADAM_B2 * v_ref[...] + (1.0 - ADAM_B2) * (gv * gv)
        m_hat = nm / c1
        v_hat = nv / c2
        d_ref[...] = -ADAM_LR * (m_hat / (jnp.sqrt(v_hat) + ADAM_EPS) + ADAM_WD * w_ref[...])
        nm_ref[...] = nm
        nv_ref[...] = nv

    spec = _bs((tr, cols), lambda i: (i, 0))
    return pl.pallas_call(
        body, name=name, grid=(rows // tr,), in_specs=[spec] * 4, out_specs=[spec] * 3,
        out_shape=[jax.ShapeDtypeStruct((rows, cols), F32)] * 3, compiler_params=_arb(1),
    )(w, g, m, v)


ANY = pl.BlockSpec(memory_space=pl.ANY)


def _mesh_pos():
    return lax.axis_index("x"), lax.axis_index("y"), lax.axis_index("c")


def _other_chips(x, y):
    return [(1 - x, y), (x, 1 - y), (1 - x, 1 - y)]


def _allgather_weights(shards, split):
    n = len(shards)

    def body(*refs):
        w_refs, o_refs = refs[:n], refs[n:2 * n]
        send_sems, recv_sems, fsend_sems, frecv_sems, local_sems = refs[2 * n:]
        x, y, c = _mesh_pos()
        p = 2 * x + y
        chips = _other_chips(x, y)

        def piece(i, chip_index, core):
            return o_refs[i].at[chip_index, core] if split[i] else o_refs[i].at[chip_index]

        def remote(src, dst, ssem, rsem, to):
            return pltpu.make_async_remote_copy(src_ref=src, dst_ref=dst, send_sem=ssem, recv_sem=rsem,
                                                device_id=to, device_id_type=MESH)

        started = []
        for i in range(n):
            mine = pltpu.make_async_copy(w_refs[i], o_refs[i].at[p], local_sems.at[i])
            mine.start()
            started.append(mine)
        sends = []
        for i in range(n):
            src = w_refs[i].at[c] if split[i] else w_refs[i]
            for k, chip in enumerate(chips):
                cp = remote(src, piece(i, p, c), send_sems.at[3 * i + k], recv_sems.at[3 * i + k], (*chip, c))
                cp.start()
                sends.append(cp)
        for i in range(n):
            for k, chip in enumerate(chips):
                pk = 2 * chip[0] + chip[1]
                landed = piece(i, pk, c)
                remote(landed, landed, send_sems.at[3 * i + k], recv_sems.at[3 * i + k], (*chip, c)).wait_recv()
                if split[i]:
                    fw = remote(landed, landed, fsend_sems.at[3 * i + k], frecv_sems.at[3 * i + k], (x, y, 1 - c))
                    fw.start()
                    sends.append(fw)
        for i in range(n):
            if split[i]:
                for k, chip in enumerate(chips):
                    pk = 2 * chip[0] + chip[1]
                    theirs = piece(i, pk, 1 - c)
                    remote(theirs, theirs, fsend_sems.at[3 * i + k], frecv_sems.at[3 * i + k],
                           (x, y, 1 - c)).wait_recv()
        for cp in sends:
            cp.wait_send()
        for cp in started:
            cp.wait()

    return pl.pallas_call(
        body, name="allgather_weights",
        in_specs=[ANY] * n, out_specs=[ANY] * n,
        out_shape=[jax.ShapeDtypeStruct((4,) + w.shape, w.dtype) for w in shards],
        scratch_shapes=[pltpu.SemaphoreType.DMA((3 * n,)), pltpu.SemaphoreType.DMA((3 * n,)),
                        pltpu.SemaphoreType.DMA((3 * n,)), pltpu.SemaphoreType.DMA((3 * n,)),
                        pltpu.SemaphoreType.DMA((n,))],
    )(*shards)


def _rs_pair_exchange(grads):
    n = len(grads)

    def body(*refs):
        g_refs, o_refs = refs[:n], refs[n:2 * n]
        send_sems, recv_sems = refs[2 * n:]
        x, y, c = _mesh_pos()
        cps = []
        for i in range(n):
            cp = pltpu.make_async_remote_copy(
                src_ref=g_refs[i].at[:, 1 - c], dst_ref=o_refs[i],
                send_sem=send_sems.at[i], recv_sem=recv_sems.at[i], device_id=(x, y, 1 - c), device_id_type=MESH)
            cp.start()
            cps.append(cp)
        for cp in cps:
            cp.wait()

    return pl.pallas_call(
        body, name="rs_pair_exchange", in_specs=[ANY] * n, out_specs=[ANY] * n,
        out_shape=[jax.ShapeDtypeStruct((4,) + g.shape[2:], F32) for g in grads],
        scratch_shapes=[pltpu.SemaphoreType.DMA((n,)), pltpu.SemaphoreType.DMA((n,))],
    )(*grads)


def _rs_pair_add(name, core, g, recv):
    _, half, cols = recv.shape
    tr = _row_tile(half)
    nr = half // tr

    def body(core_ref, g_ref, r_ref, o_ref):
        o_ref[...] = (g_ref[...] + r_ref[...]).astype(BF16)

    return pl.pallas_call(
        body, name=name,
        grid_spec=pltpu.PrefetchScalarGridSpec(
            num_scalar_prefetch=1, grid=(4, nr),
            in_specs=[pl.BlockSpec((None, None, tr, cols), lambda q, r, cr: (q, cr[0], r, 0)),
                      pl.BlockSpec((None, tr, cols), lambda q, r, cr: (q, r, 0))],
            out_specs=pl.BlockSpec((None, tr, cols), lambda q, r, cr: (q, r, 0))),
        out_shape=jax.ShapeDtypeStruct((4, half, cols), BF16),
        compiler_params=_arb(2),
    )(core, g, recv)


def _rs_ici(pairs):
    n = len(pairs)

    def body(*refs):
        p_refs, o_refs = refs[:n], refs[n:2 * n]
        send_sems, recv_sems = refs[2 * n:]
        x, y, c = _mesh_pos()
        cps = []
        for i in range(n):
            for k, chip in enumerate(_other_chips(x, y)):
                pk = 2 * chip[0] + chip[1]
                cp = pltpu.make_async_remote_copy(
                    src_ref=p_refs[i].at[pk], dst_ref=o_refs[i].at[k],
                    send_sem=send_sems.at[3 * i + k], recv_sem=recv_sems.at[3 * i + k],
                    device_id=(*chip, c), device_id_type=MESH)
                cp.start()
                cps.append(cp)
        for cp in cps:
            cp.wait()

    return pl.pallas_call(
        body, name="rs_ici", in_specs=[ANY] * n, out_specs=[ANY] * n,
        out_shape=[jax.ShapeDtypeStruct((3,) + pr.shape[1:], BF16) for pr in pairs],
        scratch_shapes=[pltpu.SemaphoreType.DMA((3 * n,)), pltpu.SemaphoreType.DMA((3 * n,))],
    )(*pairs)


def _rs_final_add(name, chip, pair, recv):
    _, half, cols = pair.shape
    tr = _row_tile(half)

    def body(chip_ref, p_ref, r_ref, o_ref):
        o_ref[...] = ((p_ref[...].astype(F32) + r_ref[0].astype(F32)) + r_ref[1].astype(F32)) + r_ref[2].astype(F32)

    return pl.pallas_call(
        body, name=name,
        grid_spec=pltpu.PrefetchScalarGridSpec(
            num_scalar_prefetch=1, grid=(half // tr,),
            in_specs=[pl.BlockSpec((None, tr, cols), lambda r, ch: (ch[0], r, 0)),
                      pl.BlockSpec((3, tr, cols), lambda r, ch: (0, r, 0))],
            out_specs=pl.BlockSpec((tr, cols), lambda r, ch: (r, 0))),
        out_shape=jax.ShapeDtypeStruct((half, cols), F32),
        compiler_params=_arb(1),
    )(chip, pair, recv)


def _rs_pair_share(halves):
    n = len(halves)

    def body(*refs):
        h_refs, o_refs = refs[:n], refs[n:2 * n]
        send_sems, recv_sems, local_sems = refs[2 * n:]
        x, y, c = _mesh_pos()
        cps = []
        for i in range(n):
            rows = o_refs[i].at[c]
            mine = pltpu.make_async_copy(h_refs[i], rows, local_sems.at[i])
            mine.start()
            cp = pltpu.make_async_remote_copy(src_ref=h_refs[i], dst_ref=rows, send_sem=send_sems.at[i],
                                              recv_sem=recv_sems.at[i], device_id=(x, y, 1 - c), device_id_type=MESH)
            cp.start()
            cps.append((mine, cp))
        for i, (mine, cp) in enumerate(cps):
            mine.wait()
            cp.wait_send()
            theirs = o_refs[i].at[1 - c]
            pltpu.make_async_remote_copy(src_ref=theirs, dst_ref=theirs, send_sem=send_sems.at[i],
                                         recv_sem=recv_sems.at[i], device_id=(x, y, 1 - c),
                                         device_id_type=MESH).wait_recv()

    return pl.pallas_call(
        body, name="rs_pair_share", in_specs=[ANY] * n, out_specs=[ANY] * n,
        out_shape=[jax.ShapeDtypeStruct((2,) + h.shape, F32) for h in halves],
        scratch_shapes=[pltpu.SemaphoreType.DMA((n,)), pltpu.SemaphoreType.DMA((n,)),
                        pltpu.SemaphoreType.DMA((n,))],
    )(*halves)


def _small_allreduce(buf):
    rows = buf.shape[0]

    def body(in_ref, out_ref, gather_ref, send_sems, recv_sems):
        x, y, c = _mesh_pos()
        me = 4 * x + 2 * y + c
        gather_ref[me] = in_ref[...]
        cps = []
        for j in range(1, 8):
            peer = (x ^ (j >> 2), y ^ ((j >> 1) & 1), c ^ (j & 1))
            cp = pltpu.make_async_remote_copy(src_ref=in_ref, dst_ref=gather_ref.at[me], send_sem=send_sems.at[j - 1],
                                              recv_sem=recv_sems.at[j - 1], device_id=peer, device_id_type=MESH)
            cp.start()
            cps.append(cp)
        for j in range(1, 8):
            peer_id = 4 * (x ^ (j >> 2)) + 2 * (y ^ ((j >> 1) & 1)) + (c ^ (j & 1))
            slot = gather_ref.at[peer_id]
            pltpu.make_async_remote_copy(src_ref=slot, dst_ref=slot, send_sem=send_sems.at[j - 1],
                                         recv_sem=recv_sems.at[j - 1], device_id=(x, y, c),
                                         device_id_type=MESH).wait_recv()
        for cp in cps:
            cp.wait_send()
        tot = gather_ref[0]
        for d in range(1, 8):
            tot = tot + gather_ref[d]
        out_ref[...] = tot

    return pl.pallas_call(
        body, name="small_allreduce",
        in_specs=[pl.BlockSpec(memory_space=pltpu.VMEM)], out_specs=pl.BlockSpec(memory_space=pltpu.VMEM),
        out_shape=jax.ShapeDtypeStruct(buf.shape, F32),
        scratch_shapes=[pltpu.VMEM((8, rows, LANES), F32), pltpu.SemaphoreType.DMA((7,)),
                        pltpu.SemaphoreType.DMA((7,))],
    )(buf)


def _pack(parts, rows):
    flat = jnp.concatenate([p.reshape(-1).astype(F32) for p in parts])
    return jnp.pad(flat, (0, rows * LANES - flat.shape[0])).reshape(rows, LANES)


def _pack_rows(parts):
    n = sum(math.prod(p.shape) for p in parts)
    return pl.cdiv(pl.cdiv(n, LANES), 8) * 8


def _unpack(buf, shapes):
    flat = buf.reshape(-1)
    out, off = [], 0
    for shp in shapes:
        size = math.prod(shp)
        out.append(flat[off:off + size].reshape(shp))
        off += size
    return out


def _pad_w_in(w):
    z = jnp.zeros((w.shape[0], 64), w.dtype)
    return jnp.concatenate([w[:, 448:1472], w[:, 1984:3008], w[:, 3008:4032], w[:, 0:256], w[:, 1472:1728],
                            w[:, 1728:1984], w[:, 256:384], w[:, 384:448], z], axis=1)


def _unpad_w_in(p):
    return jnp.concatenate([p[:, P_QLAT:P_QLAT + 256], p[:, P_CKV:P_CKV + 128], p[:, P_KR:P_KR + 64],
                            p[:, P_QB:P_QB + 1024], p[:, P_KB:P_KB + 256], p[:, P_VB:P_VB + 256],
                            p[:, P_GA:P_GA + 1024], p[:, P_GB:P_GB + 1024]], axis=1)


def _col_shards(w):
    r, c4 = w.shape
    return w.reshape(r, 4, c4 // 4).transpose(1, 0, 2)


def _heads_major(a, heads, hd):
    s = a.shape[0]
    return a.reshape(s, heads, hd).transpose(1, 0, 2)


def _heads_minor(a):
    h, s, hd = a.shape
    return a.transpose(1, 0, 2).reshape(s, h * hd)


def _local_step(x, positions, target, norm1_g, w_in_p, q_a_norm_g, wq, kv_a_norm_g, wkv, rel_bias, sinks,
                w_out, norm2_g, w_up, conv_w, conv_b, w_down, final_norm_g):
    s = x.shape[0]
    half = QK_ROPE // 2
    inv_freq = ROPE_THETA ** (-jnp.arange(half, dtype=F32) / half)
    ang = positions.astype(F32)[:, None] * inv_freq[None, :]
    cos, sin = jnp.cos(ang), jnp.sin(ang)
    z64 = jnp.zeros((s, 64), F32)
    cos_t = jnp.concatenate([cos, cos, z64], axis=1)
    sin_t = jnp.concatenate([-sin, sin, z64], axis=1)
    bucket = _t5_bucket_table()
    sinks1 = sinks.reshape(H_B)

    h1, rstd1 = _rmsnorm_fwd("norm1_fwd", x, norm1_g, D_MODEL, 0)
    proj = _matmul("proj", h1, w_in_p, out_shape=(s, W_IN_PAD), out_dtype=F32, grid=(s // 1024, W_IN_PAD // 512, 1),
                   a_spec=_bs((1024, D_MODEL), lambda i, j, k: (i, 0)), b_spec=_bs((D_MODEL, 512), lambda i, j, k: (0, j)),
                   o_spec=_bs((1024, 512), lambda i, j, k: (i, j)), contract=NN)
    qn, cn, rstd_q, rstd_c = _lat_norms(proj, q_a_norm_g, kv_a_norm_g)
    q = _q_heads(qn, wq, cos_t, sin_t)
    k, v = _kv_heads(cn, wkv, proj, cos_t, sin_t)
    o_a, lse_a = _mla_fwd(q, k, v)

    qh = _heads_major(proj[:, P_QB:P_QB + H_B * HD_B].astype(BF16), H_B, HD_B)
    kh = _heads_major(proj[:, P_KB:P_KB + KV_B * HD_B].astype(BF16), KV_B, HD_B)
    vh = _heads_major(proj[:, P_VB:P_VB + KV_B * HD_B].astype(BF16), KV_B, HD_B)
    bias = _win_bias(bucket, rel_bias)
    o_bh, lse_b = _win_fwd(qh, kh, vh, bias, sinks1)
    o_b = _heads_minor(o_bh)

    mixed = _gate_fwd(proj, o_a, o_b)
    row512 = lambda w: _bs((512, w), lambda i, j, k: (i, 0))
    whole = lambda r, c: _bs((r, c), lambda i, j, k: (0, 0))
    x1 = _matmul("attn_out", mixed, w_out, out_shape=(s, D_MODEL), out_dtype=F32, grid=(s // 512, 1, 1),
                 a_spec=row512(D_MODEL), b_spec=whole(D_MODEL, D_MODEL), o_spec=row512(D_MODEL), contract=NN, add=x)
    h2, rstd2 = _rmsnorm_fwd("norm2_fwd", x1, norm2_g, D_MODEL, 0)
    u = _matmul("ffn_up", h2, w_up, out_shape=(s, 2 * D_FF), out_dtype=F32, grid=(s // 1024, 2 * D_FF // 512, 1),
                a_spec=_bs((1024, D_MODEL), lambda i, j, k: (i, 0)), b_spec=_bs((D_MODEL, 512), lambda i, j, k: (0, j)),
                o_spec=_bs((1024, 512), lambda i, j, k: (i, j)), contract=NN)
    f = _convffn_fwd(u, conv_w, conv_b)
    x2 = _matmul("ffn_down", f, w_down, out_shape=(s, D_MODEL), out_dtype=F32, grid=(s // 512, 1, 1),
                 a_spec=row512(D_FF), b_spec=whole(D_FF, D_MODEL), o_spec=row512(D_MODEL), contract=NN, add=x1)
    loss, dx2, d_final_g = _final_loss(x2, target, final_norm_g.reshape(1, D_MODEL))

    df = _matmul("ffn_down_dx", dx2, w_down, out_shape=(s, D_FF), out_dtype=F32, grid=(s // 512, 2, 1),
                 a_spec=row512(D_MODEL), b_spec=_bs((D_FF // 2, D_MODEL), lambda i, j, k: (j, 0)),
                 o_spec=_bs((512, D_FF // 2), lambda i, j, k: (i, j)), contract=NT)
    d_w_down = _matmul("ffn_down_dw", f, dx2, out_shape=(D_FF, D_MODEL), out_dtype=F32, grid=(2, 1, s // 512),
                       a_spec=_bs((512, D_FF // 2), lambda i, j, k: (k, i)), b_spec=_bs((512, D_MODEL), lambda i, j, k: (k, 0)),
                       o_spec=_bs((D_FF // 2, D_MODEL), lambda i, j, k: (i, 0)), contract=TN)
    du, d_conv_w2, d_conv_b2 = _convffn_bwd(u, conv_w, conv_b, df)
    kc = D_FF // 2
    dh2 = _matmul("ffn_up_dx", du, w_up, out_shape=(s, D_MODEL), out_dtype=F32, grid=(s // 1024, 1, 4),
                  a_spec=_bs((None, 1024, kc), lambda i, j, k: (k // 2, i, k % 2)),
                  b_spec=_bs((D_MODEL, kc), lambda i, j, k: (0, k)),
                  o_spec=_bs((1024, D_MODEL), lambda i, j, k: (i, 0)), contract=NT)
    d_w_up = _matmul("ffn_up_dw", h2, du, out_shape=(D_MODEL, 2 * D_FF), out_dtype=F32, grid=(1, 4, s // 512),
                     a_spec=_bs((512, D_MODEL), lambda i, j, k: (k, 0)),
                     b_spec=_bs((None, 512, kc), lambda i, j, k: (j // 2, k, j % 2)),
                     o_spec=_bs((D_MODEL, kc), lambda i, j, k: (0, j)), contract=TN)
    dx1, d_norm2_g = _rmsnorm_bwd("norm2_bwd", dh2, x1, rstd2, norm2_g, D_MODEL, 0, F32, res=dx2)

    dmixed = _matmul("attn_out_dx", dx1, w_out, out_shape=(s, D_MODEL), out_dtype=F32, grid=(s // 512, 1, 1),
                     a_spec=row512(D_MODEL), b_spec=whole(D_MODEL, D_MODEL), o_spec=row512(D_MODEL), contract=NT)
    d_w_out = _matmul("attn_out_dw", mixed, dx1, out_shape=(D_MODEL, D_MODEL), out_dtype=F32, grid=(1, 1, s // 512),
                      a_spec=_bs((512, D_MODEL), lambda i, j, k: (k, 0)), b_spec=_bs((512, D_MODEL), lambda i, j, k: (k, 0)),
                      o_spec=whole(D_MODEL, D_MODEL), contract=TN)
    do_a, do_b, d_ga, d_gb = _gate_bwd(dmixed, proj, o_a, o_b)

    doh = _heads_major(do_b, H_B, HD_B)
    dqh, dkh_pad, dvh_pad, dbias, dsink_rows = _win_bwd(qh, kh, vh, bias, sinks1, doh, lse_b)
    wp = _win_param_grads(bucket, dbias, dsink_rows)[:, 0, :]
    d_rel_bias = wp[:, :NUM_BUCKETS].T
    d_sinks = wp[:, NUM_BUCKETS].reshape(1, H_B)
    d_qb = _heads_minor(dqh).astype(BF16)
    d_kb = _heads_minor(dkh_pad[:, WINDOW:WINDOW + s]).astype(BF16)
    d_vb = _heads_minor(dvh_pad[:, WINDOW:WINDOW + s]).astype(BF16)

    dq, dk, dv = _mla_bwd(q, k, v, do_a, o_a, lse_a)
    dq_pre, dkv_pre, d_kr = _mla_bwd_prep(dq, dk, dv, cos_t, sin_t)
    th = min(s, HEAD_ROWS)
    hgrid = (s // th, 1, H_A)
    hblock = _bs((None, th, HEAD_PAD), lambda i, j, k: (k, i, 0))
    hrows = lambda w: _bs((th, w), lambda i, j, k: (i, 0))
    dqn = _matmul("q_up_dx", dq_pre, wq, out_shape=(s, Q_LORA), out_dtype=F32, grid=hgrid, a_spec=hblock,
                  b_spec=_bs((None, Q_LORA, HEAD_PAD), lambda i, j, k: (k, 0, 0)), o_spec=hrows(Q_LORA), contract=NT)
    dcn = _matmul("kv_up_dx", dkv_pre, wkv, out_shape=(s, KV_LORA), out_dtype=F32, grid=hgrid, a_spec=hblock,
                  b_spec=_bs((None, KV_LORA, HEAD_PAD), lambda i, j, k: (k, 0, 0)), o_spec=hrows(KV_LORA), contract=NT)
    wgrid = (H_A, 1, s // th)
    d_wq = _matmul("q_up_dw", qn, dq_pre, out_shape=(H_A, Q_LORA, HEAD_PAD), out_dtype=F32, grid=wgrid,
                   a_spec=_bs((th, Q_LORA), lambda i, j, k: (k, 0)), b_spec=_bs((None, th, HEAD_PAD), lambda i, j, k: (i, k, 0)),
                   o_spec=_bs((None, Q_LORA, HEAD_PAD), lambda i, j, k: (i, 0, 0)), contract=TN)
    d_wkv = _matmul("kv_up_dw", cn, dkv_pre, out_shape=(H_A, KV_LORA, HEAD_PAD), out_dtype=F32, grid=wgrid,
                    a_spec=_bs((th, KV_LORA), lambda i, j, k: (k, 0)), b_spec=_bs((None, th, HEAD_PAD), lambda i, j, k: (i, k, 0)),
                    o_spec=_bs((None, KV_LORA, HEAD_PAD), lambda i, j, k: (i, 0, 0)), contract=TN)
    d_qlat, d_gq = _rmsnorm_bwd("q_norm_bwd", dqn, proj, rstd_q, q_a_norm_g, Q_LORA, P_QLAT // Q_LORA, BF16)
    d_ckv, d_gkv = _rmsnorm_bwd("kv_norm_bwd", dcn, proj, rstd_c, kv_a_norm_g, KV_LORA, P_CKV // KV_LORA, BF16)

    dproj = jnp.concatenate([d_qb, d_ga, d_gb, d_qlat, d_kb, d_vb, d_ckv, d_kr], axis=1)
    dh1 = _matmul("proj_dx", dproj, w_in_p, out_shape=(s, D_MODEL), out_dtype=F32, grid=(s // 1024, 1, W_IN_PAD // 1024),
                  a_spec=_bs((1024, 1024), lambda i, j, k: (i, k)), b_spec=_bs((D_MODEL, 1024), lambda i, j, k: (0, k)),
                  o_spec=_bs((1024, D_MODEL), lambda i, j, k: (i, 0)), contract=NT)
    d_w_in_p = _matmul("proj_dw", h1, dproj, out_shape=(D_MODEL, W_IN_PAD), out_dtype=F32, grid=(1, W_IN_PAD // 1024, s // 512),
                       a_spec=_bs((512, D_MODEL), lambda i, j, k: (k, 0)), b_spec=_bs((512, 1024), lambda i, j, k: (k, j)),
                       o_spec=_bs((D_MODEL, 1024), lambda i, j, k: (0, j)), contract=TN)
    dx, d_norm1_g = _rmsnorm_bwd("norm1_bwd", dh1, x, rstd1, norm1_g, D_MODEL, 0, F32, res=dx1)

    grads = dict(
        norm1_g=d_norm1_g, w_in_p=d_w_in_p, q_a_norm_g=d_gq, wq=d_wq, kv_a_norm_g=d_gkv, wkv=d_wkv,
        rel_bias=d_rel_bias, sinks=d_sinks, w_out=d_w_out, norm2_g=d_norm2_g, w_up=d_w_up,
        conv_w=jnp.concatenate([d_conv_w2[0], d_conv_w2[1]], axis=1),
        conv_b=jnp.concatenate([d_conv_b2[0], d_conv_b2[1]], axis=1),
        w_down=d_w_down, final_norm_g=d_final_g.reshape(D_MODEL))
    return loss, dx, grads


def _wq_heads(w_q_b):
    w = w_q_b.reshape(Q_LORA, H_A, QK_NOPE + QK_ROPE).transpose(1, 0, 2)
    return jnp.pad(w, ((0, 0), (0, 0), (0, HEAD_PAD - QK_NOPE - QK_ROPE)))


def _wq_unheads(d_wq):
    return d_wq[:, :, :QK_NOPE + QK_ROPE].transpose(1, 0, 2).reshape(Q_LORA, H_A * (QK_NOPE + QK_ROPE))


def _wkv_heads(w_kv_b):
    return w_kv_b.reshape(KV_LORA, H_A, QK_NOPE + V_DIM).transpose(1, 0, 2)


def _wkv_unheads(d_wkv):
    return d_wkv.transpose(1, 0, 2).reshape(KV_LORA, H_A * (QK_NOPE + V_DIM))


SMALL = ("norm1_g", "q_a_norm_g", "kv_a_norm_g", "rel_bias", "sinks", "norm2_g", "conv_b", "final_norm_g")
BIG = ("w_in", "w_q_b", "w_kv_b", "w_out", "w_up", "w_down")


def kernel(x, positions, norm1_g, w_in, q_a_norm_g, w_q_b, kv_a_norm_g, w_kv_b, rel_bias, sinks, w_out, norm2_g, w_up, conv_w, conv_b, w_down, final_norm_g, loss_target, m_norm1_g, m_w_in, m_q_a_norm_g, m_w_q_b, m_kv_a_norm_g, m_w_kv_b, m_rel_bias, m_sinks, m_w_out, m_norm2_g, m_w_up, m_conv_w, m_conv_b, m_w_down, m_final_norm_g, v_norm1_g, v_w_in, v_q_a_norm_g, v_w_q_b, v_kv_a_norm_g, v_w_kv_b, v_rel_bias, v_sinks, v_w_out, v_norm2_g, v_w_up, v_conv_w, v_conv_b, v_w_down, v_final_norm_g):
    weights = dict(norm1_g=norm1_g, w_in=w_in, q_a_norm_g=q_a_norm_g, w_q_b=w_q_b, kv_a_norm_g=kv_a_norm_g,
                   w_kv_b=w_kv_b, rel_bias=rel_bias, sinks=sinks, w_out=w_out, norm2_g=norm2_g, w_up=w_up,
                   conv_w=conv_w, conv_b=conv_b, w_down=w_down, final_norm_g=final_norm_g)
    mom_m = dict(norm1_g=m_norm1_g, w_in=m_w_in, q_a_norm_g=m_q_a_norm_g, w_q_b=m_w_q_b, kv_a_norm_g=m_kv_a_norm_g,
                 w_kv_b=m_w_kv_b, rel_bias=m_rel_bias, sinks=m_sinks, w_out=m_w_out, norm2_g=m_norm2_g, w_up=m_w_up,
                 conv_w=m_conv_w, conv_b=m_conv_b, w_down=m_w_down, final_norm_g=m_final_norm_g)
    mom_v = dict(norm1_g=v_norm1_g, w_in=v_w_in, q_a_norm_g=v_q_a_norm_g, w_q_b=v_w_q_b, kv_a_norm_g=v_kv_a_norm_g,
                 w_kv_b=v_w_kv_b, rel_bias=v_rel_bias, sinks=v_sinks, w_out=v_w_out, norm2_g=v_norm2_g, w_up=v_w_up,
                 conv_w=v_conv_w, conv_b=v_conv_b, w_down=v_w_down, final_norm_g=v_final_norm_g)
    shard2d = {n: weights[n][0] for n in BIG}
    conv_w_shard = conv_w[0]
    xi, yi, ci = lax.axis_index("x"), lax.axis_index("y"), lax.axis_index("c")
    chip = (2 * xi + yi).astype(jnp.int32)

    halved = lambda a: a.reshape((2, a.shape[0] // 2) + a.shape[1:])
    send = [halved(shard2d[n].astype(BF16)) for n in BIG] + [conv_w_shard]
    gathered = _allgather_weights(send, split=[True] * len(BIG) + [False])
    g = {n: a.reshape((4,) + shard2d[n].shape) for n, a in zip(BIG, gathered)}
    g["conv_w"] = gathered[-1]
    cat_cols = lambda a: jnp.concatenate([a[0], a[1], a[2], a[3]], axis=1)
    w_in_p = _pad_w_in(cat_cols(g["w_in"]))
    wq = _wq_heads(cat_cols(g["w_q_b"]))
    wkv = _wkv_heads(cat_cols(g["w_kv_b"]))
    w_out_f = g["w_out"].reshape(D_MODEL, D_MODEL)
    w_up_f = cat_cols(g["w_up"])
    conv_w_f = cat_cols(g["conv_w"])
    w_down_f = g["w_down"].reshape(D_FF, D_MODEL)

    loss, dx, gr = _local_step(x[0], positions, loss_target[0], norm1_g, w_in_p, q_a_norm_g, wq, kv_a_norm_g, wkv,
                               rel_bias, sinks, w_out_f, norm2_g, w_up_f, conv_w_f, conv_b, w_down_f, final_norm_g)

    big_grads = [
        _col_shards(_unpad_w_in(gr["w_in_p"])),
        _col_shards(_wq_unheads(gr["wq"])),
        _col_shards(_wkv_unheads(gr["wkv"])),
        gr["w_out"].reshape(4, D_MODEL // 4, D_MODEL),
        _col_shards(gr["w_up"]),
        gr["w_down"].reshape(4, D_FF // 4, D_MODEL),
    ]
    big_grads = [a.reshape(4, 2, a.shape[1] // 2, a.shape[2]) for a in big_grads]
    core = ci.astype(jnp.int32).reshape(1)
    chip1 = chip.reshape(1)
    recv1 = _rs_pair_exchange(big_grads)
    pairs = [_rs_pair_add(f"rs_pair_add_{n}", core, gfull, r) for n, gfull, r in zip(BIG, big_grads, recv1)]
    recv2 = _rs_ici(pairs)
    halves = [_rs_final_add(f"rs_final_add_{n}", chip1, pr, r) for n, pr, r in zip(BIG, pairs, recv2)]
    reduced = {n: a.reshape(shard2d[n].shape) for n, a in zip(BIG, _rs_pair_share(halves))}

    small_parts = [gr[n] for n in SMALL] + [gr["conv_w"], loss]
    rows = _pack_rows(small_parts)
    summed = _unpack(_small_allreduce(_pack(small_parts, rows)), [p.shape for p in small_parts])
    small_g = dict(zip(SMALL, summed[:len(SMALL)]))
    conv_w_g = lax.dynamic_slice_in_dim(summed[len(SMALL)], chip * (2 * D_FF // 4), 2 * D_FF // 4, axis=1)
    loss_out = summed[-1].reshape(())

    out_g, out_d, out_m, out_v = {}, {}, {}, {}
    for n in BIG:
        gsh = reduced[n]
        d, nm, nv = _adamw(f"adamw_{n}", shard2d[n], gsh, mom_m[n][0], mom_v[n][0])
        out_g[n], out_d[n], out_m[n], out_v[n] = gsh[None], d[None], nm[None], nv[None]
    names = SMALL + ("conv_w",)
    shapes = [weights[n].shape for n in names]
    sg = [small_g[n].reshape(weights[n].shape) for n in SMALL] + [conv_w_g[None]]
    prow = _pack_rows([weights[n] for n in names])
    d, nm, nv = _adamw("adamw_small", _pack([weights[n] for n in names], prow), _pack(sg, prow),
                       _pack([mom_m[n] for n in names], prow), _pack([mom_v[n] for n in names], prow))
    for n, gg, dd, mm, vv in zip(names, sg, _unpack(d, shapes), _unpack(nm, shapes), _unpack(nv, shapes)):
        out_g[n], out_d[n], out_m[n], out_v[n] = gg, dd, mm, vv

    order = ("norm1_g", "w_in", "q_a_norm_g", "w_q_b", "kv_a_norm_g", "w_kv_b", "rel_bias", "sinks", "w_out",
             "norm2_g", "w_up", "conv_w", "conv_b", "w_down", "final_norm_g")
    return (loss_out, dx[None], *[out_g[n] for n in order], *[out_d[n] for n in order],
            *[out_m[n] for n in order], *[out_v[n] for n in order])
```

```python
import functools
import math

import jax
import jax.numpy as jnp
from jax import lax
from jax.experimental import pallas as pl
from jax.experimental.pallas import tpu as pltpu

F32 = jnp.float32
BF16 = jnp.bfloat16
MESH = pl.DeviceIdType.MESH

D_MODEL = 1024
EPS = 1e-6
H_A = 8
QK_NOPE = 128
QK_ROPE = 64
V_DIM = 128
Q_LORA = 256
KV_LORA = 128
ROPE_THETA = 10000.0
H_B = 16
KV_B = 4
GROUP = 4
HD_B = 64
WINDOW = 128
Q_BLOCK = 128
NUM_BUCKETS = 32
MAX_DISTANCE = 128
D_FF = 2816
HEAD_PAD = 256

ADAM_LR = 0.001
ADAM_B1 = 0.9
ADAM_B2 = 0.999
ADAM_EPS = 1e-08
ADAM_WD = 0.01
ADAM_STEP = 10

LANES = 128
P_QB, P_GA, P_GB, P_QLAT, P_KB, P_VB, P_CKV, P_KR = 0, 1024, 2048, 3072, 3328, 3584, 3840, 3968
W_IN_PAD = 4096

NT = (((1,), (1,)), ((), ()))
NN = (((1,), (0,)), ((), ()))
TN = (((0,), (0,)), ((), ()))


def _arb(n):
    return pltpu.CompilerParams(dimension_semantics=("arbitrary",) * n)


def _matmul(name, a, b, *, out_shape, out_dtype, grid, a_spec, b_spec, o_spec, contract, add=None, bf16_copy=False):
    nk = grid[2]
    acc_shape = tuple(d for d in o_spec.block_shape if d is not None)
    n_in = 3 if add is not None else 2
    n_out = 2 if bf16_copy else 1

    def body(*refs):
        a_ref, b_ref = refs[:2]
        add_ref = refs[2] if add is not None else None
        o_refs = refs[n_in:n_in + n_out]
        scratch = refs[n_in + n_out:]
        prod = lax.dot_general(a_ref[...].astype(BF16), b_ref[...].astype(BF16), contract,
                               preferred_element_type=F32)

        def finish(val):
            if add_ref is not None:
                val = add_ref[...] + val
            o_refs[0][...] = val.astype(out_dtype)
            if bf16_copy:
                o_refs[1][...] = val.astype(BF16)

        if nk == 1:
            finish(prod)
        else:
            acc_ref = scratch[0]
            k = pl.program_id(2)

            @pl.when(k == 0)
            def _():
                acc_ref[...] = prod

            @pl.when((k > 0) & (k < nk - 1))
            def _():
                acc_ref[...] += prod

            @pl.when(k == nk - 1)
            def _():
                finish(acc_ref[...] + prod)

    in_specs = [a_spec, b_spec]
    args = [a, b]
    if add is not None:
        in_specs.append(o_spec)
        args.append(add)
    out_shapes = [jax.ShapeDtypeStruct(out_shape, out_dtype)]
    if bf16_copy:
        out_shapes.append(jax.ShapeDtypeStruct(out_shape, BF16))
    res = pl.pallas_call(
        body, name=name, grid=grid, in_specs=in_specs, out_specs=[o_spec] * n_out, out_shape=out_shapes,
        scratch_shapes=[pltpu.VMEM(acc_shape, F32)] if nk > 1 else [],
        compiler_params=_arb(3),
    )(*args)
    return res if bf16_copy else res[0]


def _bs(block, fn):
    return pl.BlockSpec(block, fn)


def _rmsnorm_fwd(name, src, g, d, cb, ts=512):
    s = src.shape[0]

    def body(x_ref, g_ref, h_ref, r_ref):
        x = x_ref[...]
        r = lax.rsqrt(jnp.mean(x * x, axis=-1, keepdims=True) + EPS)
        h_ref[...] = (x * r * g_ref[...]).astype(BF16)
        r_ref[...] = r

    return pl.pallas_call(
        body, name=name, grid=(s // ts,),
        in_specs=[_bs((ts, d), lambda i: (i, cb)), _bs((1, d), lambda i: (0, 0))],
        out_specs=[_bs((ts, d), lambda i: (i, 0)), _bs((ts, 1), lambda i: (i, 0))],
        out_shape=[jax.ShapeDtypeStruct((s, d), BF16), jax.ShapeDtypeStruct((s, 1), F32)],
        compiler_params=_arb(1),
    )(src, g)


def _rmsnorm_bwd(name, dy, src, rstd, g, d, cb, out_dtype, res=None, bf16_copy=False, ts=512):
    s = src.shape[0]

    def body(*refs):
        dy_ref, x_ref, r_ref, g_ref = refs[:4]
        res_ref = refs[4] if res is not None else None
        dx_ref, dg_ref = refs[n_in:n_in + 2]
        dyv = dy_ref[...]
        r = r_ref[...]
        xhat = x_ref[...] * r
        dyh = dyv * g_ref[...]
        c = jnp.mean(dyh * xhat, axis=-1, keepdims=True)
        dx = r * (dyh - xhat * c)
        if res_ref is not None:
            dx = res_ref[...] + dx
        dx_ref[...] = dx.astype(out_dtype)
        if bf16_copy:
            refs[n_in + 2][...] = dx.astype(BF16)
        part = jnp.sum(dyv * xhat, axis=0, keepdims=True)

        @pl.when(pl.program_id(0) == 0)
        def _():
            dg_ref[...] = part

        @pl.when(pl.program_id(0) > 0)
        def _():
            dg_ref[...] += part

    in_specs = [_bs((ts, d), lambda i: (i, 0)), _bs((ts, d), lambda i: (i, cb)),
                _bs((ts, 1), lambda i: (i, 0)), _bs((1, d), lambda i: (0, 0))]
    args = [dy, src, rstd, g]
    if res is not None:
        in_specs.append(_bs((ts, d), lambda i: (i, 0)))
        args.append(res)
    n_in = len(args)
    out_specs = [_bs((ts, d), lambda i: (i, 0)), _bs((1, d), lambda i: (0, 0))]
    out_shape = [jax.ShapeDtypeStruct((s, d), out_dtype), jax.ShapeDtypeStruct((1, d), F32)]
    if bf16_copy:
        out_specs.append(_bs((ts, d), lambda i: (i, 0)))
        out_shape.append(jax.ShapeDtypeStruct((s, d), BF16))
    return pl.pallas_call(
        body, name=name, grid=(s // ts,), in_specs=in_specs, out_specs=out_specs, out_shape=out_shape,
        compiler_params=_arb(1),
    )(*args)


def _final_loss(x2, target, g, ts=512):
    s, d = x2.shape

    def body(x_ref, t_ref, g_ref, loss_ref, dx_ref, dg_ref, dxb_ref):
        x = x_ref[...]
        r = lax.rsqrt(jnp.mean(x * x, axis=-1, keepdims=True) + EPS)
        xhat = x * r
        gv = g_ref[...]
        err = xhat * gv - t_ref[...]
        lpart = 0.5 * jnp.sum(jnp.mean(err * err, axis=-1, keepdims=True), axis=0, keepdims=True)
        dyv = err * (1.0 / d)
        dyh = dyv * gv
        c = jnp.mean(dyh * xhat, axis=-1, keepdims=True)
        dx = r * (dyh - xhat * c)
        dx_ref[...] = dx
        dxb_ref[...] = dx.astype(BF16)
        gpart = jnp.sum(dyv * xhat, axis=0, keepdims=True)

        @pl.when(pl.program_id(0) == 0)
        def _():
            dg_ref[...] = gpart
            loss_ref[...] = lpart

        @pl.when(pl.program_id(0) > 0)
        def _():
            dg_ref[...] += gpart
            loss_ref[...] += lpart

    return pl.pallas_call(
        body, name="final_loss", grid=(s // ts,),
        in_specs=[_bs((ts, d), lambda i: (i, 0)), _bs((ts, d), lambda i: (i, 0)), _bs((1, d), lambda i: (0, 0))],
        out_specs=[_bs((1, 1), lambda i: (0, 0)), _bs((ts, d), lambda i: (i, 0)), _bs((1, d), lambda i: (0, 0)),
                   _bs((ts, d), lambda i: (i, 0))],
        out_shape=[jax.ShapeDtypeStruct((1, 1), F32), jax.ShapeDtypeStruct((s, d), F32),
                   jax.ShapeDtypeStruct((1, d), F32), jax.ShapeDtypeStruct((s, d), BF16)],
        compiler_params=_arb(1),
    )(x2, target, g)


def _swap_halves(t):
    lane = lax.broadcasted_iota(jnp.int32, t.shape, 1)
    return jnp.where(lane < 32, pltpu.roll(t, 96, 1), pltpu.roll(t, 32, 1))


def _rope_fwd(t, cos_t, sin_t):
    return t * cos_t + _swap_halves(t) * sin_t


def _rope_bwd(dt, cos_t, sin_t):
    return dt * cos_t - _swap_halves(dt) * sin_t


def _lat_norms(proj, gq, gkv, ts=512):
    s = proj.shape[0]

    def body(q_ref, c_ref, gq_ref, gkv_ref, qn_ref, cn_ref, rq_ref, rc_ref):
        q = q_ref[...]
        rq = lax.rsqrt(jnp.mean(q * q, axis=-1, keepdims=True) + EPS)
        qn_ref[...] = (q * rq * gq_ref[...]).astype(BF16)
        rq_ref[...] = rq
        cv = c_ref[...]
        rc = lax.rsqrt(jnp.mean(cv * cv, axis=-1, keepdims=True) + EPS)
        cn_ref[...] = (cv * rc * gkv_ref[...]).astype(BF16)
        rc_ref[...] = rc

    return pl.pallas_call(
        body, name="lat_norms", grid=(s // ts,),
        in_specs=[_bs((ts, Q_LORA), lambda i: (i, P_QLAT // Q_LORA)),
                  _bs((ts, KV_LORA), lambda i: (i, P_CKV // KV_LORA)),
                  _bs((1, Q_LORA), lambda i: (0, 0)), _bs((1, KV_LORA), lambda i: (0, 0))],
        out_specs=[_bs((ts, Q_LORA), lambda i: (i, 0)), _bs((ts, KV_LORA), lambda i: (i, 0)),
                   _bs((ts, 1), lambda i: (i, 0)), _bs((ts, 1), lambda i: (i, 0))],
        out_shape=[jax.ShapeDtypeStruct((s, Q_LORA), BF16), jax.ShapeDtypeStruct((s, KV_LORA), BF16),
                   jax.ShapeDtypeStruct((s, 1), F32), jax.ShapeDtypeStruct((s, 1), F32)],
        compiler_params=_arb(1),
    )(proj, proj, gq, gkv)


HEAD_ROWS = 2048
DW_ROWS = 2048


def _q_heads(qn, wq, cos_t, sin_t):
    s = qn.shape[0]
    ts = min(s, HEAD_ROWS)

    def body(qn_ref, w_ref, cos_ref, sin_ref, q_ref):
        o = jnp.dot(qn_ref[...], w_ref[...], preferred_element_type=F32)
        q_ref[:, :LANES] = o[:, :LANES].astype(BF16)
        q_ref[:, LANES:] = _rope_fwd(o[:, LANES:], cos_ref[...], sin_ref[...]).astype(BF16)

    return pl.pallas_call(
        body, name="q_heads", grid=(H_A, s // ts),
        in_specs=[_bs((ts, Q_LORA), lambda h, i: (i, 0)), _bs((None, Q_LORA, HEAD_PAD), lambda h, i: (h, 0, 0)),
                  _bs((ts, LANES), lambda h, i: (i, 0)), _bs((ts, LANES), lambda h, i: (i, 0))],
        out_specs=_bs((None, ts, HEAD_PAD), lambda h, i: (h, i, 0)),
        out_shape=jax.ShapeDtypeStruct((H_A, s, HEAD_PAD), BF16),
        compiler_params=_arb(2),
    )(qn, wq, cos_t, sin_t)


def _kv_heads(cn, wkv, proj, cos_t, sin_t):
    s = cn.shape[0]
    ts = min(s, HEAD_ROWS)

    def body(cn_ref, w_ref, kr_ref, cos_ref, sin_ref, k_ref, v_ref):
        o = jnp.dot(cn_ref[...], w_ref[...], preferred_element_type=F32)
        k_ref[:, :LANES] = o[:, :LANES].astype(BF16)
        k_ref[:, LANES:] = _rope_fwd(kr_ref[...], cos_ref[...], sin_ref[...]).astype(BF16)
        v_ref[...] = o[:, LANES:].astype(BF16)

    return pl.pallas_call(
        body, name="kv_heads", grid=(H_A, s // ts),
        in_specs=[_bs((ts, KV_LORA), lambda h, i: (i, 0)),
                  _bs((None, KV_LORA, QK_NOPE + V_DIM), lambda h, i: (h, 0, 0)),
                  _bs((ts, LANES), lambda h, i: (i, P_KR // LANES)),
                  _bs((ts, LANES), lambda h, i: (i, 0)), _bs((ts, LANES), lambda h, i: (i, 0))],
        out_specs=[_bs((None, ts, HEAD_PAD), lambda h, i: (h, i, 0)), _bs((None, ts, V_DIM), lambda h, i: (h, i, 0))],
        out_shape=[jax.ShapeDtypeStruct((H_A, s, HEAD_PAD), BF16), jax.ShapeDtypeStruct((H_A, s, V_DIM), BF16)],
        compiler_params=_arb(2),
    )(cn, wkv, proj, cos_t, sin_t)


MLA_SCALE = 1.0 / math.sqrt(QK_NOPE + QK_ROPE)
LOG2E = math.log2(math.e)
MLA_EXP2_SCALE = MLA_SCALE * LOG2E


def _lane_tiles(a):
    return [a[:, j * LANES:(j + 1) * LANES] for j in range(a.shape[1] // LANES)]


def _mla_fwd(q, k, v, tq=512, tk=512):
    s = q.shape[1]
    nk = s // tk

    def body(q_ref, k_ref, v_ref, o_ref, lse_ref, m_ref, l_ref, acc_ref):
        m_ref[...] = jnp.full(m_ref.shape, -jnp.inf, F32)
        l_ref[...] = jnp.zeros(l_ref.shape, F32)
        acc_ref[...] = jnp.zeros(acc_ref.shape, F32)
        qv = q_ref[...]

        def step(c, carry):
            rows = pl.ds(pl.multiple_of(c * tk, tk), tk)
            raw = lax.dot_general(qv, k_ref[rows, :], NT, preferred_element_type=F32)
            m_prev = m_ref[...]
            m_new = jnp.maximum(m_prev, jnp.max(raw, axis=-1, keepdims=True))
            alpha = jnp.exp2((m_prev - m_new) * MLA_EXP2_SCALE)
            ps = [jnp.exp2((t - m_new) * MLA_EXP2_SCALE) for t in _lane_tiles(raw)]
            l_ref[...] = alpha * l_ref[...] + functools.reduce(lambda a, b: a + b, ps)
            p = jnp.concatenate(ps, axis=1).astype(BF16)
            acc_ref[...] = alpha * acc_ref[...] + jnp.dot(p, v_ref[rows, :], preferred_element_type=F32)
            m_ref[...] = m_new
            return carry

        lax.fori_loop(0, nk, step, 0, unroll=True)
        l = jnp.sum(l_ref[...], axis=-1, keepdims=True)
        o_ref[...] = acc_ref[...] / l
        lse_ref[...] = m_ref[...] * MLA_SCALE + jnp.log(l)

    return pl.pallas_call(
        body, name="mla_fwd", grid=(H_A, s // tq),
        in_specs=[_bs((None, tq, HEAD_PAD), lambda h, i: (h, i, 0)),
                  _bs((None, s, HEAD_PAD), lambda h, i: (h, 0, 0)),
                  _bs((None, s, V_DIM), lambda h, i: (h, 0, 0))],
        out_specs=[_bs((tq, V_DIM), lambda h, i: (i, h)), _bs((None, tq, LANES), lambda h, i: (h, i, 0))],
        out_shape=[jax.ShapeDtypeStruct((s, H_A * V_DIM), F32), jax.ShapeDtypeStruct((H_A, s, LANES), F32)],
        scratch_shapes=[pltpu.VMEM((tq, LANES), F32), pltpu.VMEM((tq, LANES), F32), pltpu.VMEM((tq, V_DIM), F32)],
        compiler_params=_arb(2),
    )(q, k, v)


def _mla_bwd(q, k, v, do, o, lse, tq=512, tk=512):
    s = q.shape[1]
    nq = s // tq

    def body(q_ref, k_ref, v_ref, do_ref, o_ref, lse_ref, dq_ref, dk_ref, dv_ref, delta_ref):
        @pl.when(pl.program_id(1) == 0)
        def _():
            def init(c, carry):
                rows = pl.ds(pl.multiple_of(c * tq, tq), tq)
                delta = jnp.sum(do_ref[rows, :] * o_ref[rows, :], axis=-1, keepdims=True)
                delta_ref[rows, :] = jnp.broadcast_to(delta, (tq, LANES))
                dq_ref[rows, :] = jnp.zeros((tq, HEAD_PAD), F32)
                return carry

            lax.fori_loop(0, nq, init, 0)

        dk_ref[...] = jnp.zeros(dk_ref.shape, F32)
        dv_ref[...] = jnp.zeros(dv_ref.shape, F32)
        kb = k_ref[...]
        vb = v_ref[...]

        def step(c, carry):
            rows = pl.ds(pl.multiple_of(c * tq, tq), tq)
            qc = q_ref[rows, :]
            doc = do_ref[rows, :].astype(BF16)
            raw = lax.dot_general(qc, kb, NT, preferred_element_type=F32)
            dp = lax.dot_general(doc, vb, NT, preferred_element_type=F32)
            lse2 = lse_ref[rows, :] * LOG2E
            delta = delta_ref[rows, :]
            ps = [jnp.exp2(t * MLA_EXP2_SCALE - lse2) for t in _lane_tiles(raw)]
            dss = [pj * (dj - delta) * MLA_SCALE for pj, dj in zip(ps, _lane_tiles(dp))]
            p = jnp.concatenate(ps, axis=1).astype(BF16)
            ds = jnp.concatenate(dss, axis=1).astype(BF16)
            dv_ref[...] += lax.dot_general(p, doc, TN, preferred_element_type=F32)
            dk_ref[...] += lax.dot_general(ds, qc, TN, preferred_element_type=F32)
            dq_ref[rows, :] += jnp.dot(ds, kb, preferred_element_type=F32)
            return carry

        lax.fori_loop(0, nq, step, 0, unroll=True)

    return pl.pallas_call(
        body, name="mla_bwd", grid=(H_A, s // tk),
        in_specs=[_bs((None, s, HEAD_PAD), lambda h, j: (h, 0, 0)),
                  _bs((None, tk, HEAD_PAD), lambda h, j: (h, j, 0)),
                  _bs((None, tk, V_DIM), lambda h, j: (h, j, 0)),
                  _bs((s, V_DIM), lambda h, j: (0, h)), _bs((s, V_DIM), lambda h, j: (0, h)),
                  _bs((None, s, LANES), lambda h, j: (h, 0, 0))],
        out_specs=[_bs((None, s, HEAD_PAD), lambda h, j: (h, 0, 0)),
                   _bs((None, tk, HEAD_PAD), lambda h, j: (h, j, 0)),
                   _bs((None, tk, V_DIM), lambda h, j: (h, j, 0))],
        out_shape=[jax.ShapeDtypeStruct((H_A, s, HEAD_PAD), F32), jax.ShapeDtypeStruct((H_A, s, HEAD_PAD), F32),
                   jax.ShapeDtypeStruct((H_A, s, V_DIM), F32)],
        scratch_shapes=[pltpu.VMEM((s, LANES), F32)],
        compiler_params=_arb(2),
    )(q, k, v, do, o, lse)


def _mla_bwd_prep(dq, dk, dv, cos_t, sin_t, ts=256):
    s = dq.shape[1]

    def body(dq_ref, dk_ref, dv_ref, cos_ref, sin_ref, dqp_ref, dkvp_ref, dkr_ref):
        cos_v = cos_ref[...]
        sin_v = sin_ref[...]
        kr = jnp.zeros((ts, LANES), F32)
        for h in range(H_A):
            dqp_ref[h, :, :LANES] = dq_ref[h, :, :LANES].astype(BF16)
            dqp_ref[h, :, LANES:] = _rope_bwd(dq_ref[h, :, LANES:], cos_v, sin_v).astype(BF16)
            dkvp_ref[h, :, :LANES] = dk_ref[h, :, :LANES].astype(BF16)
            dkvp_ref[h, :, LANES:] = dv_ref[h].astype(BF16)
            kr = kr + dk_ref[h, :, LANES:]
        dkr_ref[...] = _rope_bwd(kr, cos_v, sin_v).astype(BF16)

    blk3 = lambda w: _bs((H_A, ts, w), lambda i: (0, i, 0))
    return pl.pallas_call(
        body, name="mla_bwd_prep", grid=(s // ts,),
        in_specs=[blk3(HEAD_PAD), blk3(HEAD_PAD), blk3(V_DIM),
                  _bs((ts, LANES), lambda i: (i, 0)), _bs((ts, LANES), lambda i: (i, 0))],
        out_specs=[blk3(HEAD_PAD), blk3(HEAD_PAD), _bs((ts, LANES), lambda i: (i, 0))],
        out_shape=[jax.ShapeDtypeStruct((H_A, s, HEAD_PAD), BF16), jax.ShapeDtypeStruct((H_A, s, HEAD_PAD), BF16),
                   jax.ShapeDtypeStruct((s, LANES), BF16)],
        compiler_params=_arb(1),
    )(dq, dk, dv, cos_t, sin_t)


WIN_SCALE = 1.0 / math.sqrt(HD_B)
SPAN = Q_BLOCK + 2 * WINDOW


def _t5_bucket_table():
    a = jnp.arange(Q_BLOCK, dtype=jnp.int32)[:, None]
    c = jnp.arange(SPAN, dtype=jnp.int32)[None, :]
    rel = c - WINDOW - a
    nb = NUM_BUCKETS // 2
    max_exact = nb // 2
    base = (rel > 0).astype(jnp.int32) * nb
    n = jnp.abs(rel)
    nf = jnp.maximum(n, 1).astype(F32)
    large = max_exact + (jnp.log(nf / max_exact) / math.log(MAX_DISTANCE / max_exact)
                         * (nb - max_exact)).astype(jnp.int32)
    large = jnp.minimum(large, nb - 1)
    return base + jnp.where(n < max_exact, n, large)


def _win_bias(bucket, rel_bias):
    def body(rb_ref, bk_ref, o_ref):
        h = pl.program_id(0)
        bk = bk_ref[...]
        acc = jnp.zeros((Q_BLOCK, SPAN), F32)
        for b in range(NUM_BUCKETS):
            acc = jnp.where(bk == b, rb_ref[b, h], acc)
        o_ref[...] = acc

    return pl.pallas_call(
        body, name="win_bias", grid=(H_B,),
        in_specs=[pl.BlockSpec(memory_space=pltpu.SMEM), _bs((Q_BLOCK, SPAN), lambda h: (0, 0))],
        out_specs=_bs((None, Q_BLOCK, SPAN), lambda h: (h, 0, 0)),
        out_shape=jax.ShapeDtypeStruct((H_B, Q_BLOCK, SPAN), F32),
        compiler_params=_arb(1),
    )(rel_bias, bucket)


WIN_NQ = 4


def _win_kv_rows(n, j, nblk):
    blk = jnp.clip(n + j - 1, 0, nblk - 1)
    return pl.ds(pl.multiple_of(blk * Q_BLOCK, Q_BLOCK), Q_BLOCK)


def _win_scores(q, k_ref, bias_ref, n, nblk):
    a = lax.broadcasted_iota(jnp.int32, (GROUP, Q_BLOCK, Q_BLOCK), 1)
    cc = lax.broadcasted_iota(jnp.int32, (GROUP, Q_BLOCK, Q_BLOCK), 2)
    valid = [(cc >= a) & (n > 0), None, (cc <= a) & (n < nblk - 1)]
    out = []
    for j in range(3):
        sc = lax.dot_general(q, k_ref[_win_kv_rows(n, j, nblk), :], NT, preferred_element_type=F32)
        sc = sc.reshape(GROUP, Q_BLOCK, Q_BLOCK) * WIN_SCALE + bias_ref[:, :, j * Q_BLOCK:(j + 1) * Q_BLOCK]
        if valid[j] is not None:
            sc = jnp.where(valid[j], sc, -1e30)
        out.append(sc)
    return out


def _win_sink(sink_ref, kv):
    hs = lax.broadcasted_iota(jnp.int32, (GROUP, Q_BLOCK, 1), 0)
    sk = jnp.zeros((GROUP, Q_BLOCK, 1), F32)
    for g in range(GROUP):
        sk = jnp.where(hs == g, sink_ref[kv * GROUP + g], sk)
    return sk


def _win_fwd(qh, kh, vh, bias, sinks):
    s = qh.shape[1]
    nblk = s // Q_BLOCK
    rows = GROUP * Q_BLOCK

    def body(sink_ref, q_ref, k_ref, v_ref, bias_ref, o_ref, lse_ref):
        kv = pl.program_id(0)
        sk = _win_sink(sink_ref, kv)
        for b in range(WIN_NQ):
            n = pl.program_id(1) * WIN_NQ + b
            qrows = slice(b * Q_BLOCK, (b + 1) * Q_BLOCK)
            q = q_ref[:, qrows, :].reshape(rows, HD_B)
            ss = _win_scores(q, k_ref, bias_ref, n, nblk)
            m = jnp.maximum(jnp.max(jnp.maximum(jnp.maximum(ss[0], ss[1]), ss[2]), axis=2, keepdims=True), sk)
            es = [jnp.exp(sc - m) for sc in ss]
            l = jnp.sum(es[0] + es[1] + es[2], axis=2, keepdims=True) + jnp.exp(sk - m)
            acc = jnp.zeros((rows, HD_B), F32)
            for j, e in enumerate(es):
                p = (e / l).astype(BF16).reshape(rows, Q_BLOCK)
                acc = acc + jnp.dot(p, v_ref[_win_kv_rows(n, j, nblk), :], preferred_element_type=F32)
            o_ref[:, qrows, :] = acc.reshape(GROUP, Q_BLOCK, HD_B)
            lse_ref[:, qrows, :] = m + jnp.log(l)

    qspec = _bs((GROUP, WIN_NQ * Q_BLOCK, HD_B), lambda kv, i: (kv, i, 0))
    head = _bs((None, s, HD_B), lambda kv, i: (kv, 0, 0))
    return pl.pallas_call(
        body, name="win_fwd", grid=(KV_B, nblk // WIN_NQ),
        in_specs=[pl.BlockSpec(memory_space=pltpu.SMEM), qspec, head, head,
                  _bs((GROUP, Q_BLOCK, SPAN), lambda kv, i: (kv, 0, 0))],
        out_specs=[qspec, _bs((GROUP, WIN_NQ * Q_BLOCK, 1), lambda kv, i: (kv, i, 0))],
        out_shape=[jax.ShapeDtypeStruct((H_B, s, HD_B), F32), jax.ShapeDtypeStruct((H_B, s, 1), F32)],
        compiler_params=_arb(2),
    )(sinks, qh, kh, vh, bias)


def _win_bwd(qh, kh, vh, bias, sinks, doh, lse):
    s = qh.shape[1]
    nblk = s // Q_BLOCK
    rows = GROUP * Q_BLOCK
    spad = s + 2 * WINDOW

    def body(sink_ref, q_ref, k_ref, v_ref, bias_ref, do_ref, lse_ref, dq_ref, dk_ref, dv_ref, db_ref, dsk_ref):
        kv = pl.program_id(0)

        @pl.when(pl.program_id(1) == 0)
        def _():
            dk_ref[...] = jnp.zeros(dk_ref.shape, F32)
            dv_ref[...] = jnp.zeros(dv_ref.shape, F32)
            db_ref[...] = jnp.zeros(db_ref.shape, F32)
            dsk_ref[...] = jnp.zeros(dsk_ref.shape, F32)

        sk = _win_sink(sink_ref, kv)
        for b in range(WIN_NQ):
            n = pl.program_id(1) * WIN_NQ + b
            qrows = slice(b * Q_BLOCK, (b + 1) * Q_BLOCK)
            q = q_ref[:, qrows, :].reshape(rows, HD_B)
            dob = do_ref[:, qrows, :].reshape(rows, HD_B).astype(BF16)
            lse_v = lse_ref[:, qrows, :]
            ss = _win_scores(q, k_ref, bias_ref, n, nblk)
            ps = [jnp.exp(sc - lse_v) for sc in ss]
            dps = [lax.dot_general(dob, v_ref[_win_kv_rows(n, j, nblk), :], NT,
                                   preferred_element_type=F32).reshape(GROUP, Q_BLOCK, Q_BLOCK) for j in range(3)]
            delta = jnp.sum(ps[0] * dps[0] + ps[1] * dps[1] + ps[2] * dps[2], axis=2, keepdims=True)
            dq = jnp.zeros((rows, HD_B), F32)
            for j in range(3):
                ds = ps[j] * (dps[j] - delta)
                db_ref[:, :, j * Q_BLOCK:(j + 1) * Q_BLOCK] += ds
                dsb = (ds * WIN_SCALE).astype(BF16).reshape(rows, Q_BLOCK)
                dq = dq + jnp.dot(dsb, k_ref[_win_kv_rows(n, j, nblk), :], preferred_element_type=F32)
                krows = pl.ds(pl.multiple_of((n + j) * Q_BLOCK, Q_BLOCK), Q_BLOCK)
                dk_ref[krows, :] += lax.dot_general(dsb, q, TN, preferred_element_type=F32)
                dv_ref[krows, :] += lax.dot_general(ps[j].astype(BF16).reshape(rows, Q_BLOCK), dob, TN,
                                                    preferred_element_type=F32)
            dsk_ref[...] += -(jnp.exp(sk - lse_v) * delta)
            dq_ref[:, qrows, :] = dq.reshape(GROUP, Q_BLOCK, HD_B)

    qspec = _bs((GROUP, WIN_NQ * Q_BLOCK, HD_B), lambda kv, i: (kv, i, 0))
    head = _bs((None, s, HD_B), lambda kv, i: (kv, 0, 0))
    kacc = _bs((None, spad, HD_B), lambda kv, i: (kv, 0, 0))
    return pl.pallas_call(
        body, name="win_bwd", grid=(KV_B, nblk // WIN_NQ),
        in_specs=[pl.BlockSpec(memory_space=pltpu.SMEM), qspec, head, head,
                  _bs((GROUP, Q_BLOCK, SPAN), lambda kv, i: (kv, 0, 0)), qspec,
                  _bs((GROUP, WIN_NQ * Q_BLOCK, 1), lambda kv, i: (kv, i, 0))],
        out_specs=[qspec, kacc, kacc, _bs((GROUP, Q_BLOCK, SPAN), lambda kv, i: (kv, 0, 0)),
                   _bs((GROUP, Q_BLOCK, 1), lambda kv, i: (kv, 0, 0))],
        out_shape=[jax.ShapeDtypeStruct((H_B, s, HD_B), F32), jax.ShapeDtypeStruct((KV_B, spad, HD_B), F32),
                   jax.ShapeDtypeStruct((KV_B, spad, HD_B), F32), jax.ShapeDtypeStruct((H_B, Q_BLOCK, SPAN), F32),
                   jax.ShapeDtypeStruct((H_B, Q_BLOCK, 1), F32)],
        compiler_params=_arb(2),
    )(sinks, qh, kh, vh, bias, doh, lse)


def _win_param_grads(bucket, dbias, dsink_rows):
    def body(bk_ref, db_ref, ds_ref, o_ref):
        bk = bk_ref[...]
        dbv = db_ref[...]
        lane = lax.broadcasted_iota(jnp.int32, (1, LANES), 1)
        res = jnp.zeros((1, LANES), F32)
        for b in range(NUM_BUCKETS):
            tot = jnp.sum(jnp.sum(jnp.where(bk == b, dbv, 0.0), axis=1, keepdims=True), axis=0, keepdims=True)
            res = jnp.where(lane == b, tot, res)
        stot = jnp.sum(ds_ref[...], axis=0, keepdims=True)
        o_ref[...] = jnp.where(lane == NUM_BUCKETS, stot, res)

    return pl.pallas_call(
        body, name="win_param_grads", grid=(H_B,),
        in_specs=[_bs((Q_BLOCK, SPAN), lambda h: (0, 0)), _bs((None, Q_BLOCK, SPAN), lambda h: (h, 0, 0)),
                  _bs((None, Q_BLOCK, 1), lambda h: (h, 0, 0))],
        out_specs=_bs((None, 1, LANES), lambda h: (h, 0, 0)),
        out_shape=jax.ShapeDtypeStruct((H_B, 1, LANES), F32),
        compiler_params=_arb(1),
    )(bucket, dbias, dsink_rows)


def _gate_fwd(proj, o_a, o_b, ts=256):
    s = o_a.shape[0]
    wide = lambda cb: _bs((ts, D_MODEL), lambda i: (i, cb))

    def body(ga_ref, gb_ref, oa_ref, ob_ref, m_ref):
        m_ref[...] = (jax.nn.sigmoid(ga_ref[...]) * oa_ref[...]
                      + jax.nn.sigmoid(gb_ref[...]) * ob_ref[...]).astype(BF16)

    return pl.pallas_call(
        body, name="gate_fwd", grid=(s // ts,),
        in_specs=[wide(P_GA // D_MODEL), wide(P_GB // D_MODEL), wide(0), wide(0)],
        out_specs=wide(0), out_shape=jax.ShapeDtypeStruct((s, D_MODEL), BF16),
        compiler_params=_arb(1),
    )(proj, proj, o_a, o_b)


def _gate_bwd(dmixed, proj, o_a, o_b, ts=256):
    s = o_a.shape[0]
    wide = lambda cb: _bs((ts, D_MODEL), lambda i: (i, cb))

    def body(dm_ref, ga_ref, gb_ref, oa_ref, ob_ref, doa_ref, dob_ref, dga_ref, dgb_ref):
        dm = dm_ref[...]
        sa = jax.nn.sigmoid(ga_ref[...])
        sb = jax.nn.sigmoid(gb_ref[...])
        doa_ref[...] = dm * sa
        dob_ref[...] = dm * sb
        dga_ref[...] = (dm * oa_ref[...] * (sa * (1.0 - sa))).astype(BF16)
        dgb_ref[...] = (dm * ob_ref[...] * (sb * (1.0 - sb))).astype(BF16)

    return pl.pallas_call(
        body, name="gate_bwd", grid=(s // ts,),
        in_specs=[wide(0), wide(P_GA // D_MODEL), wide(P_GB // D_MODEL), wide(0), wide(0)],
        out_specs=[wide(0)] * 4,
        out_shape=[jax.ShapeDtypeStruct((s, D_MODEL), F32), jax.ShapeDtypeStruct((s, D_MODEL), F32),
                   jax.ShapeDtypeStruct((s, D_MODEL), BF16), jax.ShapeDtypeStruct((s, D_MODEL), BF16)],
        compiler_params=_arb(1),
    )(dmixed, proj, proj, o_a, o_b)


CONV_CHUNK = 512
N_SLAB = D_FF // LANES


def _shifted(ref, c, nchunks):
    r0 = c * CONV_CHUNK
    cur = ref[r0:r0 + CONV_CHUNK, :]
    row = lax.broadcasted_iota(jnp.int32, cur.shape, 0)
    before = ref[r0 - 8:r0, :][7:8, :] if c > 0 else jnp.zeros((1, LANES), F32)
    after = ref[r0 + CONV_CHUNK:r0 + CONV_CHUNK + 8, :][0:1, :] if c < nchunks - 1 else jnp.zeros((1, LANES), F32)
    prev = jnp.where(row == 0, before, pltpu.roll(cur, 1, 0))
    nxt = jnp.where(row == CONV_CHUNK - 1, after, pltpu.roll(cur, CONV_CHUNK - 1, 0))
    return prev, cur, nxt


def _conv_taps(ref, w_ref, b_ref, c, nchunks):
    prev, cur, nxt = _shifted(ref, c, nchunks)
    conv = prev * w_ref[0:1, :] + cur * w_ref[1:2, :] + nxt * w_ref[2:3, :] + b_ref[...]
    return conv, prev, cur, nxt


def _convffn_fwd(u, conv_w, conv_b):
    s = u.shape[0]
    nchunks = s // CONV_CHUNK

    def body(ug_ref, uv_ref, wg_ref, wv_ref, bg_ref, bv_ref, f_ref):
        for c in range(nchunks):
            cg = _conv_taps(ug_ref, wg_ref, bg_ref, c, nchunks)[0]
            cv = _conv_taps(uv_ref, wv_ref, bv_ref, c, nchunks)[0]
            f_ref[c * CONV_CHUNK:(c + 1) * CONV_CHUNK, :] = (cg * jax.nn.sigmoid(cg) * cv).astype(BF16)

    slab = lambda off: _bs((s, LANES), lambda j: (0, off + j))
    wsl = lambda off: _bs((3, LANES), lambda j: (0, off + j))
    bsl = lambda off: _bs((1, LANES), lambda j: (0, off + j))
    return pl.pallas_call(
        body, name="convffn_fwd", grid=(N_SLAB,),
        in_specs=[slab(0), slab(N_SLAB), wsl(0), wsl(N_SLAB), bsl(0), bsl(N_SLAB)],
        out_specs=slab(0), out_shape=jax.ShapeDtypeStruct((s, D_FF), BF16),
        compiler_params=_arb(1),
    )(u, u, conv_w, conv_w, conv_b, conv_b)


def _convffn_bwd(u, conv_w, conv_b, df):
    s = u.shape[0]
    nchunks = s // CONV_CHUNK

    def body(ug_ref, uv_ref, wg_ref, wv_ref, bg_ref, bv_ref, df_ref, du_ref, dw_ref, db_ref, dcg_ref, dcv_ref):
        dwg = [jnp.zeros((1, LANES), F32) for _ in range(3)]
        dwv = [jnp.zeros((1, LANES), F32) for _ in range(3)]
        dbg = jnp.zeros((1, LANES), F32)
        dbv = jnp.zeros((1, LANES), F32)
        for c in range(nchunks):
            rows = slice(c * CONV_CHUNK, (c + 1) * CONV_CHUNK)
            cg, gp, gc, gn = _conv_taps(ug_ref, wg_ref, bg_ref, c, nchunks)
            cv, vp, vc, vn = _conv_taps(uv_ref, wv_ref, bv_ref, c, nchunks)
            dfv = df_ref[rows, :]
            sg = jax.nn.sigmoid(cg)
            dcg = dfv * cv * (sg * (1.0 + cg * (1.0 - sg)))
            dcv = dfv * (cg * sg)
            dcg_ref[rows, :] = dcg
            dcv_ref[rows, :] = dcv
            for t, (tg, tv) in enumerate(((gp, vp), (gc, vc), (gn, vn))):
                dwg[t] = dwg[t] + jnp.sum(tg * dcg, axis=0, keepdims=True)
                dwv[t] = dwv[t] + jnp.sum(tv * dcv, axis=0, keepdims=True)
            dbg = dbg + jnp.sum(dcg, axis=0, keepdims=True)
            dbv = dbv + jnp.sum(dcv, axis=0, keepdims=True)
        for t in range(3):
            dw_ref[0, t:t + 1, :] = dwg[t]
            dw_ref[1, t:t + 1, :] = dwv[t]
        db_ref[0] = dbg
        db_ref[1] = dbv
        for half, (dc_ref, w_ref) in enumerate(((dcg_ref, wg_ref), (dcv_ref, wv_ref))):
            for c in range(nchunks):
                prev, cur, nxt = _shifted(dc_ref, c, nchunks)
                du = nxt * w_ref[0:1, :] + cur * w_ref[1:2, :] + prev * w_ref[2:3, :]
                du_ref[half, c * CONV_CHUNK:(c + 1) * CONV_CHUNK, :] = du.astype(BF16)

    slab = lambda off: _bs((s, LANES), lambda j: (0, off + j))
    wsl = lambda off: _bs((3, LANES), lambda j: (0, off + j))
    bsl = lambda off: _bs((1, LANES), lambda j: (0, off + j))
    return pl.pallas_call(
        body, name="convffn_bwd", grid=(N_SLAB,),
        in_specs=[slab(0), slab(N_SLAB), wsl(0), wsl(N_SLAB), bsl(0), bsl(N_SLAB), slab(0)],
        out_specs=[_bs((2, s, LANES), lambda j: (0, 0, j)), _bs((2, 3, LANES), lambda j: (0, 0, j)),
                   _bs((2, 1, LANES), lambda j: (0, 0, j))],
        out_shape=[jax.ShapeDtypeStruct((2, s, D_FF), BF16), jax.ShapeDtypeStruct((2, 3, D_FF), F32),
                   jax.ShapeDtypeStruct((2, 1, D_FF), F32)],
        scratch_shapes=[pltpu.VMEM((s, LANES), F32), pltpu.VMEM((s, LANES), F32)],
        compiler_params=_arb(1),
    )(u, u, conv_w, conv_w, conv_b, conv_b, df)


def _row_tile(rows, limit=512):
    best = rows
    for t in range(8, min(rows, limit) + 1, 8):
        if rows % t == 0:
            best = t
    return best if rows % 8 == 0 else rows


def _adamw(name, w, g, m, v):
    rows, cols = w.shape
    tr = _row_tile(rows)
    c1 = 1.0 - ADAM_B1 ** ADAM_STEP
    c2 = 1.0 - ADAM_B2 ** ADAM_STEP

    def body(w_ref, g_ref, m_ref, v_ref, d_ref, nm_ref, nv_ref):
        gv = g_ref[...]
        nm = ADAM_B1 * m_ref[...] + (1.0 - ADAM_B1) * gv
        nv = ADAM_B2 * v_ref[...] + (1.0 - ADAM_B2) * (gv * gv)
        m_hat = nm / c1
        v_hat = nv / c2
        d_ref[...] = -ADAM_LR * (m_hat / (jnp.sqrt(v_hat) + ADAM_EPS) + ADAM_WD * w_ref[...])
        nm_ref[...] = nm
        nv_ref[...] = nv

    spec = _bs((tr, cols), lambda i: (i, 0))
    return pl.pallas_call(
        body, name=name, grid=(rows // tr,), in_specs=[spec] * 4, out_specs=[spec] * 3,
        out_shape=[jax.ShapeDtypeStruct((rows, cols), F32)] * 3, compiler_params=_arb(1),
    )(w, g, m, v)


ANY = pl.BlockSpec(memory_space=pl.ANY)


def _mesh_pos():
    return lax.axis_index("x"), lax.axis_index("y"), lax.axis_index("c")


def _other_chips(x, y):
    return [(1 - x, y), (x, 1 - y), (1 - x, 1 - y)]


def _allgather_weights(shards, split):
    n = len(shards)

    def body(*refs):
        w_refs, o_refs = refs[:n], refs[n:2 * n]
        send_sems, recv_sems, fsend_sems, frecv_sems = refs[2 * n:]
        x, y, c = _mesh_pos()
        p = 2 * x + y
        chips = _other_chips(x, y)

        def piece(i, chip_index, core):
            return o_refs[i].at[chip_index, core] if split[i] else o_refs[i].at[chip_index]

        def remote(src, dst, ssem, rsem, to):
            return pltpu.make_async_remote_copy(src_ref=src, dst_ref=dst, send_sem=ssem, recv_sem=rsem,
                                                device_id=to, device_id_type=MESH)

        sends = []
        for i in range(n):
            src = w_refs[i].at[c] if split[i] else w_refs[i]
            for k, chip in enumerate(chips):
                cp = remote(src, piece(i, p, c), send_sems.at[3 * i + k], recv_sems.at[3 * i + k], (*chip, c))
                cp.start()
                sends.append(cp)
        for i in range(n):
            for k, chip in enumerate(chips):
                pk = 2 * chip[0] + chip[1]
                landed = piece(i, pk, c)
                remote(landed, landed, send_sems.at[3 * i + k], recv_sems.at[3 * i + k], (*chip, c)).wait_recv()
                if split[i]:
                    fw = remote(landed, landed, fsend_sems.at[3 * i + k], frecv_sems.at[3 * i + k], (x, y, 1 - c))
                    fw.start()
                    sends.append(fw)
        for i in range(n):
            if split[i]:
                for k, chip in enumerate(chips):
                    pk = 2 * chip[0] + chip[1]
                    theirs = piece(i, pk, 1 - c)
                    remote(theirs, theirs, fsend_sems.at[3 * i + k], frecv_sems.at[3 * i + k],
                           (x, y, 1 - c)).wait_recv()
        for cp in sends:
            cp.wait_send()

    return pl.pallas_call(
        body, name="allgather_weights",
        in_specs=[ANY] * n, out_specs=[ANY] * n,
        out_shape=[jax.ShapeDtypeStruct((4,) + w.shape, w.dtype) for w in shards],
        scratch_shapes=[pltpu.SemaphoreType.DMA((3 * n,)), pltpu.SemaphoreType.DMA((3 * n,)),
                        pltpu.SemaphoreType.DMA((3 * n,)), pltpu.SemaphoreType.DMA((3 * n,))],
    )(*shards)


def _rs_pair_exchange(grads):
    n = len(grads)

    def body(*refs):
        g_refs, o_refs = refs[:n], refs[n:2 * n]
        send_sems, recv_sems = refs[2 * n:]
        x, y, c = _mesh_pos()
        cps = []
        for i in range(n):
            cp = pltpu.make_async_remote_copy(
                src_ref=g_refs[i].at[:, 1 - c], dst_ref=o_refs[i],
                send_sem=send_sems.at[i], recv_sem=recv_sems.at[i], device_id=(x, y, 1 - c), device_id_type=MESH)
            cp.start()
            cps.append(cp)
        for cp in cps:
            cp.wait()

    return pl.pallas_call(
        body, name="rs_pair_exchange", in_specs=[ANY] * n, out_specs=[ANY] * n,
        out_shape=[jax.ShapeDtypeStruct((4,) + g.shape[2:], F32) for g in grads],
        scratch_shapes=[pltpu.SemaphoreType.DMA((n,)), pltpu.SemaphoreType.DMA((n,))],
    )(*grads)


def _rs_pair_add(name, core, g, recv):
    _, half, cols = recv.shape
    tr = _row_tile(half)
    nr = half // tr

    def body(core_ref, g_ref, r_ref, o_ref):
        o_ref[...] = (g_ref[...] + r_ref[...]).astype(BF16)

    return pl.pallas_call(
        body, name=name,
        grid_spec=pltpu.PrefetchScalarGridSpec(
            num_scalar_prefetch=1, grid=(4, nr),
            in_specs=[pl.BlockSpec((None, None, tr, cols), lambda q, r, cr: (q, cr[0], r, 0)),
                      pl.BlockSpec((None, tr, cols), lambda q, r, cr: (q, r, 0))],
            out_specs=pl.BlockSpec((None, tr, cols), lambda q, r, cr: (q, r, 0))),
        out_shape=jax.ShapeDtypeStruct((4, half, cols), BF16),
        compiler_params=_arb(2),
    )(core, g, recv)


def _rs_ici(pairs):
    n = len(pairs)

    def body(*refs):
        p_refs, o_refs = refs[:n], refs[n:2 * n]
        send_sems, recv_sems = refs[2 * n:]
        x, y, c = _mesh_pos()
        cps = []
        for i in range(n):
            for k, chip in enumerate(_other_chips(x, y)):
                pk = 2 * chip[0] + chip[1]
                cp = pltpu.make_async_remote_copy(
                    src_ref=p_refs[i].at[pk], dst_ref=o_refs[i].at[k],
                    send_sem=send_sems.at[3 * i + k], recv_sem=recv_sems.at[3 * i + k],
                    device_id=(*chip, c), device_id_type=MESH)
                cp.start()
                cps.append(cp)
        for cp in cps:
            cp.wait()

    return pl.pallas_call(
        body, name="rs_ici", in_specs=[ANY] * n, out_specs=[ANY] * n,
        out_shape=[jax.ShapeDtypeStruct((3,) + pr.shape[1:], BF16) for pr in pairs],
        scratch_shapes=[pltpu.SemaphoreType.DMA((3 * n,)), pltpu.SemaphoreType.DMA((3 * n,))],
    )(*pairs)


def _rs_final_add(name, chip, pair, recv):
    _, half, cols = pair.shape
    tr = _row_tile(half)

    def body(chip_ref, p_ref, r_ref, o_ref):
        o_ref[...] = ((p_ref[...].astype(F32) + r_ref[0].astype(F32)) + r_ref[1].astype(F32)) + r_ref[2].astype(F32)

    return pl.pallas_call(
        body, name=name,
        grid_spec=pltpu.PrefetchScalarGridSpec(
            num_scalar_prefetch=1, grid=(half // tr,),
            in_specs=[pl.BlockSpec((None, tr, cols), lambda r, ch: (ch[0], r, 0)),
                      pl.BlockSpec((3, tr, cols), lambda r, ch: (0, r, 0))],
            out_specs=pl.BlockSpec((tr, cols), lambda r, ch: (r, 0))),
        out_shape=jax.ShapeDtypeStruct((half, cols), F32),
        compiler_params=_arb(1),
    )(chip, pair, recv)


def _rs_pair_share(halves):
    n = len(halves)

    def body(*refs):
        h_refs, o_refs = refs[:n], refs[n:2 * n]
        send_sems, recv_sems = refs[2 * n:]
        x, y, c = _mesh_pos()
        cps = []
        for i in range(n):
            cp = pltpu.make_async_remote_copy(src_ref=h_refs[i], dst_ref=o_refs[i], send_sem=send_sems.at[i],
                                              recv_sem=recv_sems.at[i], device_id=(x, y, 1 - c), device_id_type=MESH)
            cp.start()
            cps.append(cp)
        for cp in cps:
            cp.wait()

    return pl.pallas_call(
        body, name="rs_pair_share", in_specs=[ANY] * n, out_specs=[ANY] * n,
        out_shape=[jax.ShapeDtypeStruct(h.shape, F32) for h in halves],
        scratch_shapes=[pltpu.SemaphoreType.DMA((n,)), pltpu.SemaphoreType.DMA((n,))],
    )(*halves)


def _small_allreduce(buf):
    rows = buf.shape[0]

    def body(in_ref, out_ref, gather_ref, send_sems, recv_sems):
        x, y, c = _mesh_pos()
        me = 4 * x + 2 * y + c
        gather_ref[me] = in_ref[...]
        cps = []
        for j in range(1, 8):
            peer = (x ^ (j >> 2), y ^ ((j >> 1) & 1), c ^ (j & 1))
            cp = pltpu.make_async_remote_copy(src_ref=in_ref, dst_ref=gather_ref.at[me], send_sem=send_sems.at[j - 1],
                                              recv_sem=recv_sems.at[j - 1], device_id=peer, device_id_type=MESH)
            cp.start()
            cps.append(cp)
        for j in range(1, 8):
            peer_id = 4 * (x ^ (j >> 2)) + 2 * (y ^ ((j >> 1) & 1)) + (c ^ (j & 1))
            slot = gather_ref.at[peer_id]
            pltpu.make_async_remote_copy(src_ref=slot, dst_ref=slot, send_sem=send_sems.at[j - 1],
                                         recv_sem=recv_sems.at[j - 1], device_id=(x, y, c),
                                         device_id_type=MESH).wait_recv()
        for cp in cps:
            cp.wait_send()
        tot = gather_ref[0]
        for d in range(1, 8):
            tot = tot + gather_ref[d]
        out_ref[...] = tot

    return pl.pallas_call(
        body, name="small_allreduce",
        in_specs=[pl.BlockSpec(memory_space=pltpu.VMEM)], out_specs=pl.BlockSpec(memory_space=pltpu.VMEM),
        out_shape=jax.ShapeDtypeStruct(buf.shape, F32),
        scratch_shapes=[pltpu.VMEM((8, rows, LANES), F32), pltpu.SemaphoreType.DMA((7,)),
                        pltpu.SemaphoreType.DMA((7,))],
    )(buf)


def _pack(parts, rows):
    flat = jnp.concatenate([p.reshape(-1).astype(F32) for p in parts])
    return jnp.pad(flat, (0, rows * LANES - flat.shape[0])).reshape(rows, LANES)


def _pack_rows(parts):
    n = sum(math.prod(p.shape) for p in parts)
    return pl.cdiv(pl.cdiv(n, LANES), 8) * 8


def _unpack(buf, shapes):
    flat = buf.reshape(-1)
    out, off = [], 0
    for shp in shapes:
        size = math.prod(shp)
        out.append(flat[off:off + size].reshape(shp))
        off += size
    return out


def _pad_w_in(w):
    z = jnp.zeros((w.shape[0], 64), w.dtype)
    return jnp.concatenate([w[:, 448:1472], w[:, 1984:3008], w[:, 3008:4032], w[:, 0:256], w[:, 1472:1728],
                            w[:, 1728:1984], w[:, 256:384], w[:, 384:448], z], axis=1)


def _unpad_w_in(p):
    return jnp.concatenate([p[:, P_QLAT:P_QLAT + 256], p[:, P_CKV:P_CKV + 128], p[:, P_KR:P_KR + 64],
                            p[:, P_QB:P_QB + 1024], p[:, P_KB:P_KB + 256], p[:, P_VB:P_VB + 256],
                            p[:, P_GA:P_GA + 1024], p[:, P_GB:P_GB + 1024]], axis=1)


def _col_shards(w):
    r, c4 = w.shape
    return w.reshape(r, 4, c4 // 4).transpose(1, 0, 2)


def _heads_major(a, heads, hd):
    s = a.shape[0]
    return a.reshape(s, heads, hd).transpose(1, 0, 2)


def _heads_minor(a):
    h, s, hd = a.shape
    return a.transpose(1, 0, 2).reshape(s, h * hd)


def _local_step(x, positions, target, norm1_g, w_in_p, q_a_norm_g, wq, kv_a_norm_g, wkv, rel_bias, sinks,
                w_out, norm2_g, w_up, conv_w, conv_b, w_down, final_norm_g):
    s = x.shape[0]
    half = QK_ROPE // 2
    inv_freq = ROPE_THETA ** (-jnp.arange(half, dtype=F32) / half)
    ang = positions.astype(F32)[:, None] * inv_freq[None, :]
    cos, sin = jnp.cos(ang), jnp.sin(ang)
    z64 = jnp.zeros((s, 64), F32)
    cos_t = jnp.concatenate([cos, cos, z64], axis=1)
    sin_t = jnp.concatenate([-sin, sin, z64], axis=1)
    bucket = _t5_bucket_table()
    sinks1 = sinks.reshape(H_B)

    h1, rstd1 = _rmsnorm_fwd("norm1_fwd", x, norm1_g, D_MODEL, 0)
    proj, proj_b = _matmul("proj", h1, w_in_p, out_shape=(s, W_IN_PAD), out_dtype=F32, grid=(s // 1024, W_IN_PAD // 512, 1),
                           a_spec=_bs((1024, D_MODEL), lambda i, j, k: (i, 0)), b_spec=_bs((D_MODEL, 512), lambda i, j, k: (0, j)),
                           o_spec=_bs((1024, 512), lambda i, j, k: (i, j)), contract=NN, bf16_copy=True)
    qn, cn, rstd_q, rstd_c = _lat_norms(proj, q_a_norm_g, kv_a_norm_g)
    q = _q_heads(qn, wq, cos_t, sin_t)
    k, v = _kv_heads(cn, wkv, proj, cos_t, sin_t)
    o_a, lse_a = _mla_fwd(q, k, v)

    qh = _heads_major(proj_b[:, P_QB:P_QB + H_B * HD_B], H_B, HD_B)
    kh = _heads_major(proj_b[:, P_KB:P_KB + KV_B * HD_B], KV_B, HD_B)
    vh = _heads_major(proj_b[:, P_VB:P_VB + KV_B * HD_B], KV_B, HD_B)
    bias = _win_bias(bucket, rel_bias)
    o_bh, lse_b = _win_fwd(qh, kh, vh, bias, sinks1)
    o_b = _heads_minor(o_bh)

    mixed = _gate_fwd(proj, o_a, o_b)
    row512 = lambda w: _bs((512, w), lambda i, j, k: (i, 0))
    whole = lambda r, c: _bs((r, c), lambda i, j, k: (0, 0))
    x1 = _matmul("attn_out", mixed, w_out, out_shape=(s, D_MODEL), out_dtype=F32, grid=(s // 512, 1, 1),
                 a_spec=row512(D_MODEL), b_spec=whole(D_MODEL, D_MODEL), o_spec=row512(D_MODEL), contract=NN, add=x)
    h2, rstd2 = _rmsnorm_fwd("norm2_fwd", x1, norm2_g, D_MODEL, 0)
    u = _matmul("ffn_up", h2, w_up, out_shape=(s, 2 * D_FF), out_dtype=F32, grid=(s // 1024, 2 * D_FF // 512, 1),
                a_spec=_bs((1024, D_MODEL), lambda i, j, k: (i, 0)), b_spec=_bs((D_MODEL, 512), lambda i, j, k: (0, j)),
                o_spec=_bs((1024, 512), lambda i, j, k: (i, j)), contract=NN)
    f = _convffn_fwd(u, conv_w, conv_b)
    x2 = _matmul("ffn_down", f, w_down, out_shape=(s, D_MODEL), out_dtype=F32, grid=(s // 512, 1, 1),
                 a_spec=row512(D_FF), b_spec=whole(D_FF, D_MODEL), o_spec=row512(D_MODEL), contract=NN, add=x1)
    loss, dx2, d_final_g, dx2_b = _final_loss(x2, target, final_norm_g.reshape(1, D_MODEL))
    tk = min(s, DW_ROWS)

    df = _matmul("ffn_down_dx", dx2_b, w_down, out_shape=(s, D_FF), out_dtype=F32, grid=(s // 512, 2, 1),
                 a_spec=row512(D_MODEL), b_spec=_bs((D_FF // 2, D_MODEL), lambda i, j, k: (j, 0)),
                 o_spec=_bs((512, D_FF // 2), lambda i, j, k: (i, j)), contract=NT)
    d_w_down = _matmul("ffn_down_dw", f, dx2_b, out_shape=(D_FF, D_MODEL), out_dtype=F32, grid=(2, 1, s // tk),
                       a_spec=_bs((tk, D_FF // 2), lambda i, j, k: (k, i)), b_spec=_bs((tk, D_MODEL), lambda i, j, k: (k, 0)),
                       o_spec=_bs((D_FF // 2, D_MODEL), lambda i, j, k: (i, 0)), contract=TN)
    du, d_conv_w2, d_conv_b2 = _convffn_bwd(u, conv_w, conv_b, df)
    kc = D_FF // 2
    dh2 = _matmul("ffn_up_dx", du, w_up, out_shape=(s, D_MODEL), out_dtype=F32, grid=(s // 1024, 1, 4),
                  a_spec=_bs((None, 1024, kc), lambda i, j, k: (k // 2, i, k % 2)),
                  b_spec=_bs((D_MODEL, kc), lambda i, j, k: (0, k)),
                  o_spec=_bs((1024, D_MODEL), lambda i, j, k: (i, 0)), contract=NT)
    d_w_up = _matmul("ffn_up_dw", h2, du, out_shape=(D_MODEL, 2 * D_FF), out_dtype=F32, grid=(1, 4, s // tk),
                     a_spec=_bs((tk, D_MODEL), lambda i, j, k: (k, 0)),
                     b_spec=_bs((None, tk, kc), lambda i, j, k: (j // 2, k, j % 2)),
                     o_spec=_bs((D_MODEL, kc), lambda i, j, k: (0, j)), contract=TN)
    dx1, d_norm2_g, dx1_b = _rmsnorm_bwd("norm2_bwd", dh2, x1, rstd2, norm2_g, D_MODEL, 0, F32, res=dx2, bf16_copy=True)

    dmixed = _matmul("attn_out_dx", dx1_b, w_out, out_shape=(s, D_MODEL), out_dtype=F32, grid=(s // 512, 1, 1),
                     a_spec=row512(D_MODEL), b_spec=whole(D_MODEL, D_MODEL), o_spec=row512(D_MODEL), contract=NT)
    d_w_out = _matmul("attn_out_dw", mixed, dx1_b, out_shape=(D_MODEL, D_MODEL), out_dtype=F32, grid=(1, 1, s // tk),
                      a_spec=_bs((tk, D_MODEL), lambda i, j, k: (k, 0)), b_spec=_bs((tk, D_MODEL), lambda i, j, k: (k, 0)),
                      o_spec=whole(D_MODEL, D_MODEL), contract=TN)
    do_a, do_b, d_ga, d_gb = _gate_bwd(dmixed, proj, o_a, o_b)

    doh = _heads_major(do_b, H_B, HD_B)
    dqh, dkh_pad, dvh_pad, dbias, dsink_rows = _win_bwd(qh, kh, vh, bias, sinks1, doh, lse_b)
    wp = _win_param_grads(bucket, dbias, dsink_rows)[:, 0, :]
    d_rel_bias = wp[:, :NUM_BUCKETS].T
    d_sinks = wp[:, NUM_BUCKETS].reshape(1, H_B)
    d_qb = _heads_minor(dqh).astype(BF16)
    d_kb = _heads_minor(dkh_pad[:, WINDOW:WINDOW + s]).astype(BF16)
    d_vb = _heads_minor(dvh_pad[:, WINDOW:WINDOW + s]).astype(BF16)

    dq, dk, dv = _mla_bwd(q, k, v, do_a, o_a, lse_a)
    dq_pre, dkv_pre, d_kr = _mla_bwd_prep(dq, dk, dv, cos_t, sin_t)
    th = min(s, HEAD_ROWS)
    hgrid = (s // th, 1, H_A)
    hblock = _bs((None, th, HEAD_PAD), lambda i, j, k: (k, i, 0))
    hrows = lambda w: _bs((th, w), lambda i, j, k: (i, 0))
    dqn = _matmul("q_up_dx", dq_pre, wq, out_shape=(s, Q_LORA), out_dtype=F32, grid=hgrid, a_spec=hblock,
                  b_spec=_bs((None, Q_LORA, HEAD_PAD), lambda i, j, k: (k, 0, 0)), o_spec=hrows(Q_LORA), contract=NT)
    dcn = _matmul("kv_up_dx", dkv_pre, wkv, out_shape=(s, KV_LORA), out_dtype=F32, grid=hgrid, a_spec=hblock,
                  b_spec=_bs((None, KV_LORA, HEAD_PAD), lambda i, j, k: (k, 0, 0)), o_spec=hrows(KV_LORA), contract=NT)
    wgrid = (H_A, 1, s // th)
    d_wq = _matmul("q_up_dw", qn, dq_pre, out_shape=(H_A, Q_LORA, HEAD_PAD), out_dtype=F32, grid=wgrid,
                   a_spec=_bs((th, Q_LORA), lambda i, j, k: (k, 0)), b_spec=_bs((None, th, HEAD_PAD), lambda i, j, k: (i, k, 0)),
                   o_spec=_bs((None, Q_LORA, HEAD_PAD), lambda i, j, k: (i, 0, 0)), contract=TN)
    d_wkv = _matmul("kv_up_dw", cn, dkv_pre, out_shape=(H_A, KV_LORA, HEAD_PAD), out_dtype=F32, grid=wgrid,
                    a_spec=_bs((th, KV_LORA), lambda i, j, k: (k, 0)), b_spec=_bs((None, th, HEAD_PAD), lambda i, j, k: (i, k, 0)),
                    o_spec=_bs((None, KV_LORA, HEAD_PAD), lambda i, j, k: (i, 0, 0)), contract=TN)
    d_qlat, d_gq = _rmsnorm_bwd("q_norm_bwd", dqn, proj, rstd_q, q_a_norm_g, Q_LORA, P_QLAT // Q_LORA, BF16)
    d_ckv, d_gkv = _rmsnorm_bwd("kv_norm_bwd", dcn, proj, rstd_c, kv_a_norm_g, KV_LORA, P_CKV // KV_LORA, BF16)

    dproj = jnp.concatenate([d_qb, d_ga, d_gb, d_qlat, d_kb, d_vb, d_ckv, d_kr], axis=1)
    dh1 = _matmul("proj_dx", dproj, w_in_p, out_shape=(s, D_MODEL), out_dtype=F32, grid=(s // 1024, 1, W_IN_PAD // 1024),
                  a_spec=_bs((1024, 1024), lambda i, j, k: (i, k)), b_spec=_bs((D_MODEL, 1024), lambda i, j, k: (0, k)),
                  o_spec=_bs((1024, D_MODEL), lambda i, j, k: (i, 0)), contract=NT)
    d_w_in_p = _matmul("proj_dw", h1, dproj, out_shape=(D_MODEL, W_IN_PAD), out_dtype=F32, grid=(1, W_IN_PAD // 1024, s // tk),
                       a_spec=_bs((tk, D_MODEL), lambda i, j, k: (k, 0)), b_spec=_bs((tk, 1024), lambda i, j, k: (k, j)),
                       o_spec=_bs((D_MODEL, 1024), lambda i, j, k: (0, j)), contract=TN)
    dx, d_norm1_g = _rmsnorm_bwd("norm1_bwd", dh1, x, rstd1, norm1_g, D_MODEL, 0, F32, res=dx1)

    grads = dict(
        norm1_g=d_norm1_g, w_in_p=d_w_in_p, q_a_norm_g=d_gq, wq=d_wq, kv_a_norm_g=d_gkv, wkv=d_wkv,
        rel_bias=d_rel_bias, sinks=d_sinks, w_out=d_w_out, norm2_g=d_norm2_g, w_up=d_w_up,
        conv_w=jnp.concatenate([d_conv_w2[0], d_conv_w2[1]], axis=1),
        conv_b=jnp.concatenate([d_conv_b2[0], d_conv_b2[1]], axis=1),
        w_down=d_w_down, final_norm_g=d_final_g.reshape(D_MODEL))
    return loss, dx, grads


def _wq_heads(w_q_b):
    w = w_q_b.reshape(Q_LORA, H_A, QK_NOPE + QK_ROPE).transpose(1, 0, 2)
    return jnp.pad(w, ((0, 0), (0, 0), (0, HEAD_PAD - QK_NOPE - QK_ROPE)))


def _wq_unheads(d_wq):
    return d_wq[:, :, :QK_NOPE + QK_ROPE].transpose(1, 0, 2).reshape(Q_LORA, H_A * (QK_NOPE + QK_ROPE))


def _wkv_heads(w_kv_b):
    return w_kv_b.reshape(KV_LORA, H_A, QK_NOPE + V_DIM).transpose(1, 0, 2)


def _wkv_unheads(d_wkv):
    return d_wkv.transpose(1, 0, 2).reshape(KV_LORA, H_A * (QK_NOPE + V_DIM))


SMALL = ("norm1_g", "q_a_norm_g", "kv_a_norm_g", "rel_bias", "sinks", "norm2_g", "conv_b", "final_norm_g")
BIG = ("w_in", "w_q_b", "w_kv_b", "w_out", "w_up", "w_down")


def kernel(x, positions, norm1_g, w_in, q_a_norm_g, w_q_b, kv_a_norm_g, w_kv_b, rel_bias, sinks, w_out, norm2_g, w_up, conv_w, conv_b, w_down, final_norm_g, loss_target, m_norm1_g, m_w_in, m_q_a_norm_g, m_w_q_b, m_kv_a_norm_g, m_w_kv_b, m_rel_bias, m_sinks, m_w_out, m_norm2_g, m_w_up, m_conv_w, m_conv_b, m_w_down, m_final_norm_g, v_norm1_g, v_w_in, v_q_a_norm_g, v_w_q_b, v_kv_a_norm_g, v_w_kv_b, v_rel_bias, v_sinks, v_w_out, v_norm2_g, v_w_up, v_conv_w, v_conv_b, v_w_down, v_final_norm_g):
    weights = dict(norm1_g=norm1_g, w_in=w_in, q_a_norm_g=q_a_norm_g, w_q_b=w_q_b, kv_a_norm_g=kv_a_norm_g,
                   w_kv_b=w_kv_b, rel_bias=rel_bias, sinks=sinks, w_out=w_out, norm2_g=norm2_g, w_up=w_up,
                   conv_w=conv_w, conv_b=conv_b, w_down=w_down, final_norm_g=final_norm_g)
    mom_m = dict(norm1_g=m_norm1_g, w_in=m_w_in, q_a_norm_g=m_q_a_norm_g, w_q_b=m_w_q_b, kv_a_norm_g=m_kv_a_norm_g,
                 w_kv_b=m_w_kv_b, rel_bias=m_rel_bias, sinks=m_sinks, w_out=m_w_out, norm2_g=m_norm2_g, w_up=m_w_up,
                 conv_w=m_conv_w, conv_b=m_conv_b, w_down=m_w_down, final_norm_g=m_final_norm_g)
    mom_v = dict(norm1_g=v_norm1_g, w_in=v_w_in, q_a_norm_g=v_q_a_norm_g, w_q_b=v_w_q_b, kv_a_norm_g=v_kv_a_norm_g,
                 w_kv_b=v_w_kv_b, rel_bias=v_rel_bias, sinks=v_sinks, w_out=v_w_out, norm2_g=v_norm2_g, w_up=v_w_up,
                 conv_w=v_conv_w, conv_b=v_conv_b, w_down=v_w_down, final_norm_g=v_final_norm_g)
    shard2d = {n: weights[n][0] for n in BIG}
    conv_w_shard = conv_w[0]
    xi, yi, ci = lax.axis_index("x"), lax.axis_index("y"), lax.axis_index("c")
    chip = (2 * xi + yi).astype(jnp.int32)

    halved = lambda a: a.reshape((2, a.shape[0] // 2) + a.shape[1:])
    send = [halved(shard2d[n].astype(BF16)) for n in BIG] + [conv_w_shard]
    gathered = _allgather_weights(send, split=[True] * len(BIG) + [False])
    gathered = [lax.dynamic_update_index_in_dim(a, own, chip, 0) for a, own in zip(gathered, send)]
    g = {n: a.reshape((4,) + shard2d[n].shape) for n, a in zip(BIG, gathered)}
    g["conv_w"] = gathered[-1]
    cat_cols = lambda a: jnp.concatenate([a[0], a[1], a[2], a[3]], axis=1)
    w_in_p = _pad_w_in(cat_cols(g["w_in"]))
    wq = _wq_heads(cat_cols(g["w_q_b"]))
    wkv = _wkv_heads(cat_cols(g["w_kv_b"]))
    w_out_f = g["w_out"].reshape(D_MODEL, D_MODEL)
    w_up_f = cat_cols(g["w_up"])
    conv_w_f = cat_cols(g["conv_w"])
    w_down_f = g["w_down"].reshape(D_FF, D_MODEL)

    loss, dx, gr = _local_step(x[0], positions, loss_target[0], norm1_g, w_in_p, q_a_norm_g, wq, kv_a_norm_g, wkv,
                               rel_bias, sinks, w_out_f, norm2_g, w_up_f, conv_w_f, conv_b, w_down_f, final_norm_g)

    big_grads = [
        _col_shards(_unpad_w_in(gr["w_in_p"])),
        _col_shards(_wq_unheads(gr["wq"])),
        _col_shards(_wkv_unheads(gr["wkv"])),
        gr["w_out"].reshape(4, D_MODEL // 4, D_MODEL),
        _col_shards(gr["w_up"]),
        gr["w_down"].reshape(4, D_FF // 4, D_MODEL),
    ]
    big_grads = [a.reshape(4, 2, a.shape[1] // 2, a.shape[2]) for a in big_grads]
    core = ci.astype(jnp.int32).reshape(1)
    chip1 = chip.reshape(1)
    recv1 = _rs_pair_exchange(big_grads)
    pairs = [_rs_pair_add(f"rs_pair_add_{n}", core, gfull, r) for n, gfull, r in zip(BIG, big_grads, recv1)]
    recv2 = _rs_ici(pairs)
    halves = [_rs_final_add(f"rs_final_add_{n}", chip1, pr, r) for n, pr, r in zip(BIG, pairs, recv2)]
    both = lambda mine, theirs: jnp.where(ci == 0, jnp.concatenate([mine, theirs]), jnp.concatenate([theirs, mine]))
    reduced = {n: both(h, t) for n, h, t in zip(BIG, halves, _rs_pair_share(halves))}

    small_parts = [gr[n] for n in SMALL] + [gr["conv_w"], loss]
    rows = _pack_rows(small_parts)
    summed = _unpack(_small_allreduce(_pack(small_parts, rows)), [p.shape for p in small_parts])
    small_g = dict(zip(SMALL, summed[:len(SMALL)]))
    conv_w_g = lax.dynamic_slice_in_dim(summed[len(SMALL)], chip * (2 * D_FF // 4), 2 * D_FF // 4, axis=1)
    loss_out = summed[-1].reshape(())

    out_g, out_d, out_m, out_v = {}, {}, {}, {}
    for n in BIG:
        gsh = reduced[n]
        d, nm, nv = _adamw(f"adamw_{n}", shard2d[n], gsh, mom_m[n][0], mom_v[n][0])
        out_g[n], out_d[n], out_m[n], out_v[n] = gsh[None], d[None], nm[None], nv[None]
    names = SMALL + ("conv_w",)
    shapes = [weights[n].shape for n in names]
    sg = [small_g[n].reshape(weights[n].shape) for n in SMALL] + [conv_w_g[None]]
    prow = _pack_rows([weights[n] for n in names])
    d, nm, nv = _adamw("adamw_small", _pack([weights[n] for n in names], prow), _pack(sg, prow),
                       _pack([mom_m[n] for n in names], prow), _pack([mom_v[n] for n in names], prow))
    for n, gg, dd, mm, vv in zip(names, sg, _unpack(d, shapes), _unpack(nm, shapes), _unpack(nv, shapes)):
        out_g[n], out_d[n], out_m[n], out_v[n] = gg, dd, mm, vv

    order = ("norm1_g", "w_in", "q_a_norm_g", "w_q_b", "kv_a_norm_g", "w_kv_b", "rel_bias", "sinks", "w_out",
             "norm2_g", "w_up", "conv_w", "conv_b", "w_down", "final_norm_g")
    return (loss_out, dx[None], *[out_g[n] for n in order], *[out_d[n] for n in order],
            *[out_m[n] for n in order], *[out_v[n] for n in order])
```

```python
import functools
import math

import jax
import jax.numpy as jnp
import numpy as np
from jax import lax
from jax.experimental import pallas as pl
from jax.experimental.pallas import tpu as pltpu

F32 = jnp.float32
BF16 = jnp.bfloat16
MESH = pl.DeviceIdType.MESH

D_MODEL = 1024
EPS = 1e-6
H_A = 8
QK_NOPE = 128
QK_ROPE = 64
V_DIM = 128
Q_LORA = 256
KV_LORA = 128
ROPE_THETA = 10000.0
H_B = 16
KV_B = 4
GROUP = 4
HD_B = 64
WINDOW = 128
Q_BLOCK = 128
NUM_BUCKETS = 32
MAX_DISTANCE = 128
D_FF = 2816
HEAD_PAD = 256

ADAM_LR = 0.001
ADAM_B1 = 0.9
ADAM_B2 = 0.999
ADAM_EPS = 1e-08
ADAM_WD = 0.01
ADAM_STEP = 10

LANES = 128
P_QB, P_GA, P_GB, P_QLAT, P_KB, P_VB, P_CKV, P_KR = 0, 1024, 2048, 3072, 3328, 3584, 3840, 3968
W_IN_PAD = 4096

NT = (((1,), (1,)), ((), ()))
NN = (((1,), (0,)), ((), ()))
TN = (((0,), (0,)), ((), ()))


def _arb(n):
    return pltpu.CompilerParams(dimension_semantics=("arbitrary",) * n)


def _matmul(name, a, b, *, out_shape, out_dtype, grid, a_spec, b_spec, o_spec, contract, add=None, bf16_copy=False):
    nk = grid[2]
    acc_shape = tuple(d for d in o_spec.block_shape if d is not None)
    n_in = 3 if add is not None else 2
    n_out = 2 if bf16_copy else 1

    def body(*refs):
        a_ref, b_ref = refs[:2]
        add_ref = refs[2] if add is not None else None
        o_refs = refs[n_in:n_in + n_out]
        scratch = refs[n_in + n_out:]
        prod = lax.dot_general(a_ref[...].astype(BF16), b_ref[...].astype(BF16), contract,
                               preferred_element_type=F32)

        def finish(val):
            if add_ref is not None:
                val = add_ref[...] + val
            o_refs[0][...] = val.astype(out_dtype)
            if bf16_copy:
                o_refs[1][...] = val.astype(BF16)

        if nk == 1:
            finish(prod)
        else:
            acc_ref = scratch[0]
            k = pl.program_id(2)

            @pl.when(k == 0)
            def _():
                acc_ref[...] = prod

            @pl.when((k > 0) & (k < nk - 1))
            def _():
                acc_ref[...] += prod

            @pl.when(k == nk - 1)
            def _():
                finish(acc_ref[...] + prod)

    in_specs = [a_spec, b_spec]
    args = [a, b]
    if add is not None:
        in_specs.append(o_spec)
        args.append(add)
    out_shapes = [jax.ShapeDtypeStruct(out_shape, out_dtype)]
    if bf16_copy:
        out_shapes.append(jax.ShapeDtypeStruct(out_shape, BF16))
    res = pl.pallas_call(
        body, name=name, grid=grid, in_specs=in_specs, out_specs=[o_spec] * n_out, out_shape=out_shapes,
        scratch_shapes=[pltpu.VMEM(acc_shape, F32)] if nk > 1 else [],
        compiler_params=_arb(3),
    )(*args)
    return res if bf16_copy else res[0]


def _bs(block, fn):
    return pl.BlockSpec(block, fn)


def _rmsnorm_fwd(name, src, g, d, cb, ts=512):
    s = src.shape[0]

    def body(x_ref, g_ref, h_ref, r_ref):
        x = x_ref[...]
        r = lax.rsqrt(jnp.mean(x * x, axis=-1, keepdims=True) + EPS)
        h_ref[...] = (x * r * g_ref[...]).astype(BF16)
        r_ref[...] = r

    return pl.pallas_call(
        body, name=name, grid=(s // ts,),
        in_specs=[_bs((ts, d), lambda i: (i, cb)), _bs((1, d), lambda i: (0, 0))],
        out_specs=[_bs((ts, d), lambda i: (i, 0)), _bs((ts, 1), lambda i: (i, 0))],
        out_shape=[jax.ShapeDtypeStruct((s, d), BF16), jax.ShapeDtypeStruct((s, 1), F32)],
        compiler_params=_arb(1),
    )(src, g)


def _rmsnorm_bwd(name, dy, src, rstd, g, d, cb, out_dtype, res=None, bf16_copy=False, ts=512):
    s = src.shape[0]

    def body(*refs):
        dy_ref, x_ref, r_ref, g_ref = refs[:4]
        res_ref = refs[4] if res is not None else None
        dx_ref, dg_ref = refs[n_in:n_in + 2]
        dyv = dy_ref[...]
        r = r_ref[...]
        xhat = x_ref[...] * r
        dyh = dyv * g_ref[...]
        c = jnp.mean(dyh * xhat, axis=-1, keepdims=True)
        dx = r * (dyh - xhat * c)
        if res_ref is not None:
            dx = res_ref[...] + dx
        dx_ref[...] = dx.astype(out_dtype)
        if bf16_copy:
            refs[n_in + 2][...] = dx.astype(BF16)
        part = jnp.sum(dyv * xhat, axis=0, keepdims=True)

        @pl.when(pl.program_id(0) == 0)
        def _():
            dg_ref[...] = part

        @pl.when(pl.program_id(0) > 0)
        def _():
            dg_ref[...] += part

    in_specs = [_bs((ts, d), lambda i: (i, 0)), _bs((ts, d), lambda i: (i, cb)),
                _bs((ts, 1), lambda i: (i, 0)), _bs((1, d), lambda i: (0, 0))]
    args = [dy, src, rstd, g]
    if res is not None:
        in_specs.append(_bs((ts, d), lambda i: (i, 0)))
        args.append(res)
    n_in = len(args)
    out_specs = [_bs((ts, d), lambda i: (i, 0)), _bs((1, d), lambda i: (0, 0))]
    out_shape = [jax.ShapeDtypeStruct((s, d), out_dtype), jax.ShapeDtypeStruct((1, d), F32)]
    if bf16_copy:
        out_specs.append(_bs((ts, d), lambda i: (i, 0)))
        out_shape.append(jax.ShapeDtypeStruct((s, d), BF16))
    return pl.pallas_call(
        body, name=name, grid=(s // ts,), in_specs=in_specs, out_specs=out_specs, out_shape=out_shape,
        compiler_params=_arb(1),
    )(*args)


def _final_loss(x2, target, g, ts=512):
    s, d = x2.shape

    def body(x_ref, t_ref, g_ref, loss_ref, dx_ref, dg_ref, dxb_ref):
        x = x_ref[...]
        r = lax.rsqrt(jnp.mean(x * x, axis=-1, keepdims=True) + EPS)
        xhat = x * r
        gv = g_ref[...]
        err = xhat * gv - t_ref[...]
        lpart = 0.5 * jnp.sum(jnp.mean(err * err, axis=-1, keepdims=True), axis=0, keepdims=True)
        dyv = err * (1.0 / d)
        dyh = dyv * gv
        c = jnp.mean(dyh * xhat, axis=-1, keepdims=True)
        dx = r * (dyh - xhat * c)
        dx_ref[...] = dx
        dxb_ref[...] = dx.astype(BF16)
        gpart = jnp.sum(dyv * xhat, axis=0, keepdims=True)

        @pl.when(pl.program_id(0) == 0)
        def _():
            dg_ref[...] = gpart
            loss_ref[...] = lpart

        @pl.when(pl.program_id(0) > 0)
        def _():
            dg_ref[...] += gpart
            loss_ref[...] += lpart

    return pl.pallas_call(
        body, name="final_loss", grid=(s // ts,),
        in_specs=[_bs((ts, d), lambda i: (i, 0)), _bs((ts, d), lambda i: (i, 0)), _bs((1, d), lambda i: (0, 0))],
        out_specs=[_bs((1, 1), lambda i: (0, 0)), _bs((ts, d), lambda i: (i, 0)), _bs((1, d), lambda i: (0, 0)),
                   _bs((ts, d), lambda i: (i, 0))],
        out_shape=[jax.ShapeDtypeStruct((1, 1), F32), jax.ShapeDtypeStruct((s, d), F32),
                   jax.ShapeDtypeStruct((1, d), F32), jax.ShapeDtypeStruct((s, d), BF16)],
        compiler_params=_arb(1),
    )(x2, target, g)


def _swap_halves(t):
    lane = lax.broadcasted_iota(jnp.int32, t.shape, 1)
    return jnp.where(lane < 32, pltpu.roll(t, 96, 1), pltpu.roll(t, 32, 1))


def _rope_fwd(t, cos_t, sin_t):
    return t * cos_t + _swap_halves(t) * sin_t


def _rope_bwd(dt, cos_t, sin_t):
    return dt * cos_t - _swap_halves(dt) * sin_t


def _lat_norms(proj, gq, gkv, ts=512):
    s = proj.shape[0]

    def body(q_ref, c_ref, gq_ref, gkv_ref, qn_ref, cn_ref, rq_ref, rc_ref):
        q = q_ref[...]
        rq = lax.rsqrt(jnp.mean(q * q, axis=-1, keepdims=True) + EPS)
        qn_ref[...] = (q * rq * gq_ref[...]).astype(BF16)
        rq_ref[...] = rq
        cv = c_ref[...]
        rc = lax.rsqrt(jnp.mean(cv * cv, axis=-1, keepdims=True) + EPS)
        cn_ref[...] = (cv * rc * gkv_ref[...]).astype(BF16)
        rc_ref[...] = rc

    return pl.pallas_call(
        body, name="lat_norms", grid=(s // ts,),
        in_specs=[_bs((ts, Q_LORA), lambda i: (i, P_QLAT // Q_LORA)),
                  _bs((ts, KV_LORA), lambda i: (i, P_CKV // KV_LORA)),
                  _bs((1, Q_LORA), lambda i: (0, 0)), _bs((1, KV_LORA), lambda i: (0, 0))],
        out_specs=[_bs((ts, Q_LORA), lambda i: (i, 0)), _bs((ts, KV_LORA), lambda i: (i, 0)),
                   _bs((ts, 1), lambda i: (i, 0)), _bs((ts, 1), lambda i: (i, 0))],
        out_shape=[jax.ShapeDtypeStruct((s, Q_LORA), BF16), jax.ShapeDtypeStruct((s, KV_LORA), BF16),
                   jax.ShapeDtypeStruct((s, 1), F32), jax.ShapeDtypeStruct((s, 1), F32)],
        compiler_params=_arb(1),
    )(proj, proj, gq, gkv)


HEAD_ROWS = 2048
DW_ROWS = 2048
MM_ROWS = 1024


def _q_heads(qn, wq, cos_t, sin_t):
    s = qn.shape[0]
    ts = min(s, HEAD_ROWS)

    def body(qn_ref, w_ref, cos_ref, sin_ref, q_ref):
        o = jnp.dot(qn_ref[...], w_ref[...], preferred_element_type=F32)
        q_ref[:, :LANES] = o[:, :LANES].astype(BF16)
        q_ref[:, LANES:] = _rope_fwd(o[:, LANES:], cos_ref[...], sin_ref[...]).astype(BF16)

    return pl.pallas_call(
        body, name="q_heads", grid=(H_A, s // ts),
        in_specs=[_bs((ts, Q_LORA), lambda h, i: (i, 0)), _bs((None, Q_LORA, HEAD_PAD), lambda h, i: (h, 0, 0)),
                  _bs((ts, LANES), lambda h, i: (i, 0)), _bs((ts, LANES), lambda h, i: (i, 0))],
        out_specs=_bs((None, ts, HEAD_PAD), lambda h, i: (h, i, 0)),
        out_shape=jax.ShapeDtypeStruct((H_A, s, HEAD_PAD), BF16),
        compiler_params=_arb(2),
    )(qn, wq, cos_t, sin_t)


def _kv_heads(cn, wkv, proj, cos_t, sin_t):
    s = cn.shape[0]
    ts = min(s, HEAD_ROWS)

    def body(cn_ref, w_ref, kr_ref, cos_ref, sin_ref, k_ref, v_ref):
        o = jnp.dot(cn_ref[...], w_ref[...], preferred_element_type=F32)
        k_ref[:, :LANES] = o[:, :LANES].astype(BF16)
        k_ref[:, LANES:] = _rope_fwd(kr_ref[...], cos_ref[...], sin_ref[...]).astype(BF16)
        v_ref[...] = o[:, LANES:].astype(BF16)

    return pl.pallas_call(
        body, name="kv_heads", grid=(H_A, s // ts),
        in_specs=[_bs((ts, KV_LORA), lambda h, i: (i, 0)),
                  _bs((None, KV_LORA, QK_NOPE + V_DIM), lambda h, i: (h, 0, 0)),
                  _bs((ts, LANES), lambda h, i: (i, P_KR // LANES)),
                  _bs((ts, LANES), lambda h, i: (i, 0)), _bs((ts, LANES), lambda h, i: (i, 0))],
        out_specs=[_bs((None, ts, HEAD_PAD), lambda h, i: (h, i, 0)), _bs((None, ts, V_DIM), lambda h, i: (h, i, 0))],
        out_shape=[jax.ShapeDtypeStruct((H_A, s, HEAD_PAD), BF16), jax.ShapeDtypeStruct((H_A, s, V_DIM), BF16)],
        compiler_params=_arb(2),
    )(cn, wkv, proj, cos_t, sin_t)


MLA_SCALE = 1.0 / math.sqrt(QK_NOPE + QK_ROPE)
LOG2E = math.log2(math.e)
MLA_EXP2_SCALE = MLA_SCALE * LOG2E


def _lane_tiles(a):
    return [a[:, j * LANES:(j + 1) * LANES] for j in range(a.shape[1] // LANES)]


def _mla_fwd(q, k, v, tq=512, tk=512):
    s = q.shape[1]
    nk = s // tk

    def body(q_ref, k_ref, v_ref, o_ref, lse_ref, m_ref, l_ref, acc_ref):
        m_ref[...] = jnp.full(m_ref.shape, -jnp.inf, F32)
        l_ref[...] = jnp.zeros(l_ref.shape, F32)
        acc_ref[...] = jnp.zeros(acc_ref.shape, F32)
        qv = q_ref[...]

        def step(c, carry):
            rows = pl.ds(pl.multiple_of(c * tk, tk), tk)
            raw = lax.dot_general(qv, k_ref[rows, :], NT, preferred_element_type=F32)
            m_prev = m_ref[...]
            m_new = jnp.maximum(m_prev, jnp.max(raw, axis=-1, keepdims=True))
            alpha = jnp.exp2((m_prev - m_new) * MLA_EXP2_SCALE)
            ps = [jnp.exp2((t - m_new) * MLA_EXP2_SCALE) for t in _lane_tiles(raw)]
            l_ref[...] = alpha * l_ref[...] + functools.reduce(lambda a, b: a + b, ps)
            p = jnp.concatenate(ps, axis=1).astype(BF16)
            acc_ref[...] = alpha * acc_ref[...] + jnp.dot(p, v_ref[rows, :], preferred_element_type=F32)
            m_ref[...] = m_new
            return carry

        lax.fori_loop(0, nk, step, 0, unroll=True)
        l = jnp.sum(l_ref[...], axis=-1, keepdims=True)
        o_ref[...] = acc_ref[...] / l
        lse_ref[...] = m_ref[...] * MLA_SCALE + jnp.log(l)

    return pl.pallas_call(
        body, name="mla_fwd", grid=(H_A, s // tq),
        in_specs=[_bs((None, tq, HEAD_PAD), lambda h, i: (h, i, 0)),
                  _bs((None, s, HEAD_PAD), lambda h, i: (h, 0, 0)),
                  _bs((None, s, V_DIM), lambda h, i: (h, 0, 0))],
        out_specs=[_bs((tq, V_DIM), lambda h, i: (i, h)), _bs((None, tq, LANES), lambda h, i: (h, i, 0))],
        out_shape=[jax.ShapeDtypeStruct((s, H_A * V_DIM), F32), jax.ShapeDtypeStruct((H_A, s, LANES), F32)],
        scratch_shapes=[pltpu.VMEM((tq, LANES), F32), pltpu.VMEM((tq, LANES), F32), pltpu.VMEM((tq, V_DIM), F32)],
        compiler_params=_arb(2),
    )(q, k, v)


def _mla_bwd(q, k, v, do, o, lse, tq=512, tk=512):
    s = q.shape[1]
    nq = s // tq

    def body(q_ref, k_ref, v_ref, do_ref, o_ref, lse_ref, dq_ref, dk_ref, dv_ref, delta_ref):
        @pl.when(pl.program_id(1) == 0)
        def _():
            def init(c, carry):
                rows = pl.ds(pl.multiple_of(c * tq, tq), tq)
                delta = jnp.sum(do_ref[rows, :] * o_ref[rows, :], axis=-1, keepdims=True)
                delta_ref[rows, :] = jnp.broadcast_to(delta, (tq, LANES))
                dq_ref[rows, :] = jnp.zeros((tq, HEAD_PAD), F32)
                return carry

            lax.fori_loop(0, nq, init, 0)

        dk_ref[...] = jnp.zeros(dk_ref.shape, F32)
        dv_ref[...] = jnp.zeros(dv_ref.shape, F32)
        kb = k_ref[...]
        vb = v_ref[...]

        def step(c, carry):
            rows = pl.ds(pl.multiple_of(c * tq, tq), tq)
            qc = q_ref[rows, :]
            doc = do_ref[rows, :].astype(BF16)
            raw = lax.dot_general(qc, kb, NT, preferred_element_type=F32)
            dp = lax.dot_general(doc, vb, NT, preferred_element_type=F32)
            lse2 = lse_ref[rows, :] * LOG2E
            delta = delta_ref[rows, :]
            ps = [jnp.exp2(t * MLA_EXP2_SCALE - lse2) for t in _lane_tiles(raw)]
            dss = [pj * (dj - delta) * MLA_SCALE for pj, dj in zip(ps, _lane_tiles(dp))]
            p = jnp.concatenate(ps, axis=1).astype(BF16)
            ds = jnp.concatenate(dss, axis=1).astype(BF16)
            dv_ref[...] += lax.dot_general(p, doc, TN, preferred_element_type=F32)
            dk_ref[...] += lax.dot_general(ds, qc, TN, preferred_element_type=F32)
            dq_ref[rows, :] += jnp.dot(ds, kb, preferred_element_type=F32)
            return carry

        lax.fori_loop(0, nq, step, 0, unroll=True)

    return pl.pallas_call(
        body, name="mla_bwd", grid=(H_A, s // tk),
        in_specs=[_bs((None, s, HEAD_PAD), lambda h, j: (h, 0, 0)),
                  _bs((None, tk, HEAD_PAD), lambda h, j: (h, j, 0)),
                  _bs((None, tk, V_DIM), lambda h, j: (h, j, 0)),
                  _bs((s, V_DIM), lambda h, j: (0, h)), _bs((s, V_DIM), lambda h, j: (0, h)),
                  _bs((None, s, LANES), lambda h, j: (h, 0, 0))],
        out_specs=[_bs((None, s, HEAD_PAD), lambda h, j: (h, 0, 0)),
                   _bs((None, tk, HEAD_PAD), lambda h, j: (h, j, 0)),
                   _bs((None, tk, V_DIM), lambda h, j: (h, j, 0))],
        out_shape=[jax.ShapeDtypeStruct((H_A, s, HEAD_PAD), F32), jax.ShapeDtypeStruct((H_A, s, HEAD_PAD), F32),
                   jax.ShapeDtypeStruct((H_A, s, V_DIM), F32)],
        scratch_shapes=[pltpu.VMEM((s, LANES), F32)],
        compiler_params=_arb(2),
    )(q, k, v, do, o, lse)


def _mla_bwd_prep(dq, dk, dv, cos_t, sin_t, ts=256):
    s = dq.shape[1]

    def body(dq_ref, dk_ref, dv_ref, cos_ref, sin_ref, dqp_ref, dkvp_ref, dkr_ref):
        cos_v = cos_ref[...]
        sin_v = sin_ref[...]
        kr = jnp.zeros((ts, LANES), F32)
        for h in range(H_A):
            dqp_ref[h, :, :LANES] = dq_ref[h, :, :LANES].astype(BF16)
            dqp_ref[h, :, LANES:] = _rope_bwd(dq_ref[h, :, LANES:], cos_v, sin_v).astype(BF16)
            dkvp_ref[h, :, :LANES] = dk_ref[h, :, :LANES].astype(BF16)
            dkvp_ref[h, :, LANES:] = dv_ref[h].astype(BF16)
            kr = kr + dk_ref[h, :, LANES:]
        dkr_ref[...] = _rope_bwd(kr, cos_v, sin_v).astype(BF16)

    blk3 = lambda w: _bs((H_A, ts, w), lambda i: (0, i, 0))
    return pl.pallas_call(
        body, name="mla_bwd_prep", grid=(s // ts,),
        in_specs=[blk3(HEAD_PAD), blk3(HEAD_PAD), blk3(V_DIM),
                  _bs((ts, LANES), lambda i: (i, 0)), _bs((ts, LANES), lambda i: (i, 0))],
        out_specs=[blk3(HEAD_PAD), blk3(HEAD_PAD), _bs((ts, LANES), lambda i: (i, 0))],
        out_shape=[jax.ShapeDtypeStruct((H_A, s, HEAD_PAD), BF16), jax.ShapeDtypeStruct((H_A, s, HEAD_PAD), BF16),
                   jax.ShapeDtypeStruct((s, LANES), BF16)],
        compiler_params=_arb(1),
    )(dq, dk, dv, cos_t, sin_t)


WIN_SCALE = 1.0 / math.sqrt(HD_B)
SPAN = Q_BLOCK + 2 * WINDOW


def _t5_bucket_table():
    a = jnp.arange(Q_BLOCK, dtype=jnp.int32)[:, None]
    c = jnp.arange(SPAN, dtype=jnp.int32)[None, :]
    rel = c - WINDOW - a
    nb = NUM_BUCKETS // 2
    max_exact = nb // 2
    base = (rel > 0).astype(jnp.int32) * nb
    n = jnp.abs(rel)
    nf = jnp.maximum(n, 1).astype(F32)
    large = max_exact + (jnp.log(nf / max_exact) / math.log(MAX_DISTANCE / max_exact)
                         * (nb - max_exact)).astype(jnp.int32)
    large = jnp.minimum(large, nb - 1)
    return base + jnp.where(n < max_exact, n, large)


def _win_bias(bucket, rel_bias):
    def body(rb_ref, bk_ref, o_ref):
        h = pl.program_id(0)
        bk = bk_ref[...]
        acc = jnp.zeros((Q_BLOCK, SPAN), F32)
        for b in range(NUM_BUCKETS):
            acc = jnp.where(bk == b, rb_ref[b, h], acc)
        o_ref[...] = acc

    return pl.pallas_call(
        body, name="win_bias", grid=(H_B,),
        in_specs=[pl.BlockSpec(memory_space=pltpu.SMEM), _bs((Q_BLOCK, SPAN), lambda h: (0, 0))],
        out_specs=_bs((None, Q_BLOCK, SPAN), lambda h: (h, 0, 0)),
        out_shape=jax.ShapeDtypeStruct((H_B, Q_BLOCK, SPAN), F32),
        compiler_params=_arb(1),
    )(rel_bias, bucket)


WIN_NQ = 4


def _win_kv_rows(n, j, nblk):
    blk = jnp.clip(n + j - 1, 0, nblk - 1)
    return pl.ds(pl.multiple_of(blk * Q_BLOCK, Q_BLOCK), Q_BLOCK)


def _win_scores(q, k_ref, bias_ref, n, nblk):
    a = lax.broadcasted_iota(jnp.int32, (GROUP, Q_BLOCK, Q_BLOCK), 1)
    cc = lax.broadcasted_iota(jnp.int32, (GROUP, Q_BLOCK, Q_BLOCK), 2)
    valid = [(cc >= a) & (n > 0), None, (cc <= a) & (n < nblk - 1)]
    out = []
    for j in range(3):
        sc = lax.dot_general(q, k_ref[_win_kv_rows(n, j, nblk), :], NT, preferred_element_type=F32)
        sc = sc.reshape(GROUP, Q_BLOCK, Q_BLOCK) * WIN_SCALE + bias_ref[:, :, j * Q_BLOCK:(j + 1) * Q_BLOCK]
        if valid[j] is not None:
            sc = jnp.where(valid[j], sc, -1e30)
        out.append(sc)
    return out


def _win_sink(sink_ref, kv):
    hs = lax.broadcasted_iota(jnp.int32, (GROUP, Q_BLOCK, 1), 0)
    sk = jnp.zeros((GROUP, Q_BLOCK, 1), F32)
    for g in range(GROUP):
        sk = jnp.where(hs == g, sink_ref[kv * GROUP + g], sk)
    return sk


def _win_fwd(qh, kh, vh, bias, sinks):
    s = qh.shape[1]
    nblk = s // Q_BLOCK
    rows = GROUP * Q_BLOCK

    def body(sink_ref, q_ref, k_ref, v_ref, bias_ref, o_ref, lse_ref):
        kv = pl.program_id(0)
        sk = _win_sink(sink_ref, kv)
        for b in range(WIN_NQ):
            n = pl.program_id(1) * WIN_NQ + b
            qrows = slice(b * Q_BLOCK, (b + 1) * Q_BLOCK)
            q = q_ref[:, qrows, :].reshape(rows, HD_B)
            ss = _win_scores(q, k_ref, bias_ref, n, nblk)
            m = jnp.maximum(jnp.max(jnp.maximum(jnp.maximum(ss[0], ss[1]), ss[2]), axis=2, keepdims=True), sk)
            es = [jnp.exp(sc - m) for sc in ss]
            l = jnp.sum(es[0] + es[1] + es[2], axis=2, keepdims=True) + jnp.exp(sk - m)
            acc = jnp.zeros((rows, HD_B), F32)
            for j, e in enumerate(es):
                p = (e / l).astype(BF16).reshape(rows, Q_BLOCK)
                acc = acc + jnp.dot(p, v_ref[_win_kv_rows(n, j, nblk), :], preferred_element_type=F32)
            o_ref[:, qrows, :] = acc.reshape(GROUP, Q_BLOCK, HD_B)
            lse_ref[:, qrows, :] = m + jnp.log(l)

    qspec = _bs((GROUP, WIN_NQ * Q_BLOCK, HD_B), lambda kv, i: (kv, i, 0))
    head = _bs((None, s, HD_B), lambda kv, i: (kv, 0, 0))
    return pl.pallas_call(
        body, name="win_fwd", grid=(KV_B, nblk // WIN_NQ),
        in_specs=[pl.BlockSpec(memory_space=pltpu.SMEM), qspec, head, head,
                  _bs((GROUP, Q_BLOCK, SPAN), lambda kv, i: (kv, 0, 0))],
        out_specs=[qspec, _bs((GROUP, WIN_NQ * Q_BLOCK, 1), lambda kv, i: (kv, i, 0))],
        out_shape=[jax.ShapeDtypeStruct((H_B, s, HD_B), F32), jax.ShapeDtypeStruct((H_B, s, 1), F32)],
        compiler_params=_arb(2),
    )(sinks, qh, kh, vh, bias)


def _win_bwd(qh, kh, vh, bias, sinks, doh, lse):
    s = qh.shape[1]
    nblk = s // Q_BLOCK
    rows = GROUP * Q_BLOCK
    spad = s + 2 * WINDOW

    def body(sink_ref, q_ref, k_ref, v_ref, bias_ref, do_ref, lse_ref, dq_ref, dk_ref, dv_ref, db_ref, dsk_ref):
        kv = pl.program_id(0)

        @pl.when(pl.program_id(1) == 0)
        def _():
            dk_ref[...] = jnp.zeros(dk_ref.shape, F32)
            dv_ref[...] = jnp.zeros(dv_ref.shape, F32)
            db_ref[...] = jnp.zeros(db_ref.shape, F32)
            dsk_ref[...] = jnp.zeros(dsk_ref.shape, F32)

        sk = _win_sink(sink_ref, kv)
        for b in range(WIN_NQ):
            n = pl.program_id(1) * WIN_NQ + b
            qrows = slice(b * Q_BLOCK, (b + 1) * Q_BLOCK)
            q = q_ref[:, qrows, :].reshape(rows, HD_B)
            dob = do_ref[:, qrows, :].reshape(rows, HD_B).astype(BF16)
            lse_v = lse_ref[:, qrows, :]
            ss = _win_scores(q, k_ref, bias_ref, n, nblk)
            ps = [jnp.exp(sc - lse_v) for sc in ss]
            dps = [lax.dot_general(dob, v_ref[_win_kv_rows(n, j, nblk), :], NT,
                                   preferred_element_type=F32).reshape(GROUP, Q_BLOCK, Q_BLOCK) for j in range(3)]
            delta = jnp.sum(ps[0] * dps[0] + ps[1] * dps[1] + ps[2] * dps[2], axis=2, keepdims=True)
            dq = jnp.zeros((rows, HD_B), F32)
            for j in range(3):
                ds = ps[j] * (dps[j] - delta)
                db_ref[:, :, j * Q_BLOCK:(j + 1) * Q_BLOCK] += ds
                dsb = (ds * WIN_SCALE).astype(BF16).reshape(rows, Q_BLOCK)
                dq = dq + jnp.dot(dsb, k_ref[_win_kv_rows(n, j, nblk), :], preferred_element_type=F32)
                krows = pl.ds(pl.multiple_of((n + j) * Q_BLOCK, Q_BLOCK), Q_BLOCK)
                dk_ref[krows, :] += lax.dot_general(dsb, q, TN, preferred_element_type=F32)
                dv_ref[krows, :] += lax.dot_general(ps[j].astype(BF16).reshape(rows, Q_BLOCK), dob, TN,
                                                    preferred_element_type=F32)
            dsk_ref[...] += -(jnp.exp(sk - lse_v) * delta)
            dq_ref[:, qrows, :] = dq.reshape(GROUP, Q_BLOCK, HD_B)

    qspec = _bs((GROUP, WIN_NQ * Q_BLOCK, HD_B), lambda kv, i: (kv, i, 0))
    head = _bs((None, s, HD_B), lambda kv, i: (kv, 0, 0))
    kacc = _bs((None, spad, HD_B), lambda kv, i: (kv, 0, 0))
    return pl.pallas_call(
        body, name="win_bwd", grid=(KV_B, nblk // WIN_NQ),
        in_specs=[pl.BlockSpec(memory_space=pltpu.SMEM), qspec, head, head,
                  _bs((GROUP, Q_BLOCK, SPAN), lambda kv, i: (kv, 0, 0)), qspec,
                  _bs((GROUP, WIN_NQ * Q_BLOCK, 1), lambda kv, i: (kv, i, 0))],
        out_specs=[qspec, kacc, kacc, _bs((GROUP, Q_BLOCK, SPAN), lambda kv, i: (kv, 0, 0)),
                   _bs((GROUP, Q_BLOCK, 1), lambda kv, i: (kv, 0, 0))],
        out_shape=[jax.ShapeDtypeStruct((H_B, s, HD_B), F32), jax.ShapeDtypeStruct((KV_B, spad, HD_B), F32),
                   jax.ShapeDtypeStruct((KV_B, spad, HD_B), F32), jax.ShapeDtypeStruct((H_B, Q_BLOCK, SPAN), F32),
                   jax.ShapeDtypeStruct((H_B, Q_BLOCK, 1), F32)],
        compiler_params=_arb(2),
    )(sinks, qh, kh, vh, bias, doh, lse)


def _win_param_grads(bucket, dbias, dsink_rows):
    def body(bk_ref, db_ref, ds_ref, o_ref):
        bk = bk_ref[...]
        dbv = db_ref[...]
        lane = lax.broadcasted_iota(jnp.int32, (1, LANES), 1)
        res = jnp.zeros((1, LANES), F32)
        for b in range(NUM_BUCKETS):
            tot = jnp.sum(jnp.sum(jnp.where(bk == b, dbv, 0.0), axis=1, keepdims=True), axis=0, keepdims=True)
            res = jnp.where(lane == b, tot, res)
        stot = jnp.sum(ds_ref[...], axis=0, keepdims=True)
        o_ref[...] = jnp.where(lane == NUM_BUCKETS, stot, res)

    return pl.pallas_call(
        body, name="win_param_grads", grid=(H_B,),
        in_specs=[_bs((Q_BLOCK, SPAN), lambda h: (0, 0)), _bs((None, Q_BLOCK, SPAN), lambda h: (h, 0, 0)),
                  _bs((None, Q_BLOCK, 1), lambda h: (h, 0, 0))],
        out_specs=_bs((None, 1, LANES), lambda h: (h, 0, 0)),
        out_shape=jax.ShapeDtypeStruct((H_B, 1, LANES), F32),
        compiler_params=_arb(1),
    )(bucket, dbias, dsink_rows)


def _gate_fwd(proj, o_a, o_b, ts=256):
    s = o_a.shape[0]
    wide = lambda cb: _bs((ts, D_MODEL), lambda i: (i, cb))

    def body(ga_ref, gb_ref, oa_ref, ob_ref, m_ref):
        m_ref[...] = (jax.nn.sigmoid(ga_ref[...]) * oa_ref[...]
                      + jax.nn.sigmoid(gb_ref[...]) * ob_ref[...]).astype(BF16)

    return pl.pallas_call(
        body, name="gate_fwd", grid=(s // ts,),
        in_specs=[wide(P_GA // D_MODEL), wide(P_GB // D_MODEL), wide(0), wide(0)],
        out_specs=wide(0), out_shape=jax.ShapeDtypeStruct((s, D_MODEL), BF16),
        compiler_params=_arb(1),
    )(proj, proj, o_a, o_b)


def _gate_bwd(dmixed, proj, o_a, o_b, ts=256):
    s = o_a.shape[0]
    wide = lambda cb: _bs((ts, D_MODEL), lambda i: (i, cb))

    def body(dm_ref, ga_ref, gb_ref, oa_ref, ob_ref, doa_ref, dob_ref, dga_ref, dgb_ref):
        dm = dm_ref[...]
        sa = jax.nn.sigmoid(ga_ref[...])
        sb = jax.nn.sigmoid(gb_ref[...])
        doa_ref[...] = dm * sa
        dob_ref[...] = dm * sb
        dga_ref[...] = (dm * oa_ref[...] * (sa * (1.0 - sa))).astype(BF16)
        dgb_ref[...] = (dm * ob_ref[...] * (sb * (1.0 - sb))).astype(BF16)

    return pl.pallas_call(
        body, name="gate_bwd", grid=(s // ts,),
        in_specs=[wide(0), wide(P_GA // D_MODEL), wide(P_GB // D_MODEL), wide(0), wide(0)],
        out_specs=[wide(0)] * 4,
        out_shape=[jax.ShapeDtypeStruct((s, D_MODEL), F32), jax.ShapeDtypeStruct((s, D_MODEL), F32),
                   jax.ShapeDtypeStruct((s, D_MODEL), BF16), jax.ShapeDtypeStruct((s, D_MODEL), BF16)],
        compiler_params=_arb(1),
    )(dmixed, proj, proj, o_a, o_b)


CONV_CHUNK = 512
N_SLAB = D_FF // LANES


def _shifted(ref, c, nchunks):
    r0 = c * CONV_CHUNK
    cur = ref[r0:r0 + CONV_CHUNK, :]
    row = lax.broadcasted_iota(jnp.int32, cur.shape, 0)
    before = ref[r0 - 8:r0, :][7:8, :] if c > 0 else jnp.zeros((1, LANES), F32)
    after = ref[r0 + CONV_CHUNK:r0 + CONV_CHUNK + 8, :][0:1, :] if c < nchunks - 1 else jnp.zeros((1, LANES), F32)
    prev = jnp.where(row == 0, before, pltpu.roll(cur, 1, 0))
    nxt = jnp.where(row == CONV_CHUNK - 1, after, pltpu.roll(cur, CONV_CHUNK - 1, 0))
    return prev, cur, nxt


def _conv_taps(ref, w_ref, b_ref, c, nchunks):
    prev, cur, nxt = _shifted(ref, c, nchunks)
    conv = prev * w_ref[0:1, :] + cur * w_ref[1:2, :] + nxt * w_ref[2:3, :] + b_ref[...]
    return conv, prev, cur, nxt


def _convffn_fwd(u, conv_w, conv_b):
    s = u.shape[0]
    nchunks = s // CONV_CHUNK

    def body(ug_ref, uv_ref, wg_ref, wv_ref, bg_ref, bv_ref, f_ref):
        for c in range(nchunks):
            cg = _conv_taps(ug_ref, wg_ref, bg_ref, c, nchunks)[0]
            cv = _conv_taps(uv_ref, wv_ref, bv_ref, c, nchunks)[0]
            f_ref[c * CONV_CHUNK:(c + 1) * CONV_CHUNK, :] = (cg * jax.nn.sigmoid(cg) * cv).astype(BF16)

    slab = lambda off: _bs((s, LANES), lambda j: (0, off + j))
    wsl = lambda off: _bs((3, LANES), lambda j: (0, off + j))
    bsl = lambda off: _bs((1, LANES), lambda j: (0, off + j))
    return pl.pallas_call(
        body, name="convffn_fwd", grid=(N_SLAB,),
        in_specs=[slab(0), slab(N_SLAB), wsl(0), wsl(N_SLAB), bsl(0), bsl(N_SLAB)],
        out_specs=slab(0), out_shape=jax.ShapeDtypeStruct((s, D_FF), BF16),
        compiler_params=_arb(1),
    )(u, u, conv_w, conv_w, conv_b, conv_b)


def _convffn_bwd(u, conv_w, conv_b, df):
    s = u.shape[0]
    nchunks = s // CONV_CHUNK

    def body(ug_ref, uv_ref, wg_ref, wv_ref, bg_ref, bv_ref, df_ref, du_ref, dw_ref, db_ref, dcg_ref, dcv_ref):
        dwg = [jnp.zeros((1, LANES), F32) for _ in range(3)]
        dwv = [jnp.zeros((1, LANES), F32) for _ in range(3)]
        dbg = jnp.zeros((1, LANES), F32)
        dbv = jnp.zeros((1, LANES), F32)
        for c in range(nchunks):
            rows = slice(c * CONV_CHUNK, (c + 1) * CONV_CHUNK)
            cg, gp, gc, gn = _conv_taps(ug_ref, wg_ref, bg_ref, c, nchunks)
            cv, vp, vc, vn = _conv_taps(uv_ref, wv_ref, bv_ref, c, nchunks)
            dfv = df_ref[rows, :]
            sg = jax.nn.sigmoid(cg)
            dcg = dfv * cv * (sg * (1.0 + cg * (1.0 - sg)))
            dcv = dfv * (cg * sg)
            dcg_ref[rows, :] = dcg
            dcv_ref[rows, :] = dcv
            for t, (tg, tv) in enumerate(((gp, vp), (gc, vc), (gn, vn))):
                dwg[t] = dwg[t] + jnp.sum(tg * dcg, axis=0, keepdims=True)
                dwv[t] = dwv[t] + jnp.sum(tv * dcv, axis=0, keepdims=True)
            dbg = dbg + jnp.sum(dcg, axis=0, keepdims=True)
            dbv = dbv + jnp.sum(dcv, axis=0, keepdims=True)
        for t in range(3):
            dw_ref[0, t:t + 1, :] = dwg[t]
            dw_ref[1, t:t + 1, :] = dwv[t]
        db_ref[0] = dbg
        db_ref[1] = dbv
        for half, (dc_ref, w_ref) in enumerate(((dcg_ref, wg_ref), (dcv_ref, wv_ref))):
            for c in range(nchunks):
                prev, cur, nxt = _shifted(dc_ref, c, nchunks)
                du = nxt * w_ref[0:1, :] + cur * w_ref[1:2, :] + prev * w_ref[2:3, :]
                du_ref[half, c * CONV_CHUNK:(c + 1) * CONV_CHUNK, :] = du.astype(BF16)

    slab = lambda off: _bs((s, LANES), lambda j: (0, off + j))
    wsl = lambda off: _bs((3, LANES), lambda j: (0, off + j))
    bsl = lambda off: _bs((1, LANES), lambda j: (0, off + j))
    return pl.pallas_call(
        body, name="convffn_bwd", grid=(N_SLAB,),
        in_specs=[slab(0), slab(N_SLAB), wsl(0), wsl(N_SLAB), bsl(0), bsl(N_SLAB), slab(0)],
        out_specs=[_bs((2, s, LANES), lambda j: (0, 0, j)), _bs((2, 3, LANES), lambda j: (0, 0, j)),
                   _bs((2, 1, LANES), lambda j: (0, 0, j))],
        out_shape=[jax.ShapeDtypeStruct((2, s, D_FF), BF16), jax.ShapeDtypeStruct((2, 3, D_FF), F32),
                   jax.ShapeDtypeStruct((2, 1, D_FF), F32)],
        scratch_shapes=[pltpu.VMEM((s, LANES), F32), pltpu.VMEM((s, LANES), F32)],
        compiler_params=_arb(1),
    )(u, u, conv_w, conv_w, conv_b, conv_b, df)


def _row_tile(rows, limit=512):
    best = rows
    for t in range(8, min(rows, limit) + 1, 8):
        if rows % t == 0:
            best = t
    return best if rows % 8 == 0 else rows


def _adamw(name, w, g, m, v):
    rows, cols = w.shape
    tr = _row_tile(rows)
    c1 = 1.0 - ADAM_B1 ** ADAM_STEP
    c2 = 1.0 - ADAM_B2 ** ADAM_STEP

    def body(w_ref, g_ref, m_ref, v_ref, d_ref, nm_ref, nv_ref):
        gv = g_ref[...]
        nm = ADAM_B1 * m_ref[...] + (1.0 - ADAM_B1) * gv
        nv = ADAM_B2 * v_ref[...] + (1.0 - ADAM_B2) * (gv * gv)
        m_hat = nm / c1
        v_hat = nv / c2
        d_ref[...] = -ADAM_LR * (m_hat / (jnp.sqrt(v_hat) + ADAM_EPS) + ADAM_WD * w_ref[...])
        nm_ref[...] = nm
        nv_ref[...] = nv

    spec = _bs((tr, cols), lambda i: (i, 0))
    return pl.pallas_call(
        body, name=name, grid=(rows // tr,), in_specs=[spec] * 4, out_specs=[spec] * 3,
        out_shape=[jax.ShapeDtypeStruct((rows, cols), F32)] * 3, compiler_params=_arb(1),
    )(w, g, m, v)


ANY = pl.BlockSpec(memory_space=pl.ANY)


def _mesh_pos():
    return lax.axis_index("x"), lax.axis_index("y"), lax.axis_index("c")


def _other_chips(x, y):
    return [(1 - x, y), (x, 1 - y), (1 - x, 1 - y)]


def _allgather_weights(shards, split):
    n = len(shards)

    def body(*refs):
        w_refs, o_refs = refs[:n], refs[n:2 * n]
        send_sems, recv_sems, fsend_sems, frecv_sems = refs[2 * n:]
        x, y, c = _mesh_pos()
        p = 2 * x + y
        chips = _other_chips(x, y)

        def piece(i, chip_index, core):
            return o_refs[i].at[chip_index, core] if split[i] else o_refs[i].at[chip_index]

        def remote(src, dst, ssem, rsem, to):
            return pltpu.make_async_remote_copy(src_ref=src, dst_ref=dst, send_sem=ssem, recv_sem=rsem,
                                                device_id=to, device_id_type=MESH)

        sends = []
        for i in range(n):
            src = w_refs[i].at[c] if split[i] else w_refs[i]
            for k, chip in enumerate(chips):
                cp = remote(src, piece(i, p, c), send_sems.at[3 * i + k], recv_sems.at[3 * i + k], (*chip, c))
                cp.start()
                sends.append(cp)
        for i in range(n):
            for k, chip in enumerate(chips):
                pk = 2 * chip[0] + chip[1]
                landed = piece(i, pk, c)
                remote(landed, landed, send_sems.at[3 * i + k], recv_sems.at[3 * i + k], (*chip, c)).wait_recv()
                if split[i]:
                    fw = remote(landed, landed, fsend_sems.at[3 * i + k], frecv_sems.at[3 * i + k], (x, y, 1 - c))
                    fw.start()
                    sends.append(fw)
        for i in range(n):
            if split[i]:
                for k, chip in enumerate(chips):
                    pk = 2 * chip[0] + chip[1]
                    theirs = piece(i, pk, 1 - c)
                    remote(theirs, theirs, fsend_sems.at[3 * i + k], frecv_sems.at[3 * i + k],
                           (x, y, 1 - c)).wait_recv()
        for cp in sends:
            cp.wait_send()

    return pl.pallas_call(
        body, name="allgather_weights",
        in_specs=[ANY] * n, out_specs=[ANY] * n,
        out_shape=[jax.ShapeDtypeStruct((4,) + w.shape, w.dtype) for w in shards],
        scratch_shapes=[pltpu.SemaphoreType.DMA((3 * n,)), pltpu.SemaphoreType.DMA((3 * n,)),
                        pltpu.SemaphoreType.DMA((3 * n,)), pltpu.SemaphoreType.DMA((3 * n,))],
    )(*shards)


def _rs_pair_exchange(name, grads):
    n = len(grads)

    def body(*refs):
        g_refs, o_refs = refs[:n], refs[n:2 * n]
        send_sems, recv_sems = refs[2 * n:]
        x, y, c = _mesh_pos()
        cps = []
        for i in range(n):
            cp = pltpu.make_async_remote_copy(
                src_ref=g_refs[i].at[:, 1 - c], dst_ref=o_refs[i],
                send_sem=send_sems.at[i], recv_sem=recv_sems.at[i], device_id=(x, y, 1 - c), device_id_type=MESH)
            cp.start()
            cps.append(cp)
        for cp in cps:
            cp.wait()

    return pl.pallas_call(
        body, name=name, in_specs=[ANY] * n, out_specs=[ANY] * n,
        out_shape=[jax.ShapeDtypeStruct((4,) + g.shape[2:], F32) for g in grads],
        scratch_shapes=[pltpu.SemaphoreType.DMA((n,)), pltpu.SemaphoreType.DMA((n,))],
    )(*grads)


def _rs_pair_add(name, core, g, recv):
    _, half, cols = recv.shape
    tr = _row_tile(half)
    nr = half // tr

    def body(core_ref, g_ref, r_ref, o_ref):
        o_ref[...] = (g_ref[...] + r_ref[...]).astype(BF16)

    return pl.pallas_call(
        body, name=name,
        grid_spec=pltpu.PrefetchScalarGridSpec(
            num_scalar_prefetch=1, grid=(4, nr),
            in_specs=[pl.BlockSpec((None, None, tr, cols), lambda q, r, cr: (q, cr[0], r, 0)),
                      pl.BlockSpec((None, tr, cols), lambda q, r, cr: (q, r, 0))],
            out_specs=pl.BlockSpec((None, tr, cols), lambda q, r, cr: (q, r, 0))),
        out_shape=jax.ShapeDtypeStruct((4, half, cols), BF16),
        compiler_params=_arb(2),
    )(core, g, recv)


def _rs_ici(pairs):
    n = len(pairs)

    def body(*refs):
        p_refs, o_refs = refs[:n], refs[n:2 * n]
        send_sems, recv_sems = refs[2 * n:]
        x, y, c = _mesh_pos()
        cps = []
        for i in range(n):
            for k, chip in enumerate(_other_chips(x, y)):
                pk = 2 * chip[0] + chip[1]
                cp = pltpu.make_async_remote_copy(
                    src_ref=p_refs[i].at[pk], dst_ref=o_refs[i].at[k],
                    send_sem=send_sems.at[3 * i + k], recv_sem=recv_sems.at[3 * i + k],
                    device_id=(*chip, c), device_id_type=MESH)
                cp.start()
                cps.append(cp)
        for cp in cps:
            cp.wait()

    return pl.pallas_call(
        body, name="rs_ici", in_specs=[ANY] * n, out_specs=[ANY] * n,
        out_shape=[jax.ShapeDtypeStruct((3,) + pr.shape[1:], BF16) for pr in pairs],
        scratch_shapes=[pltpu.SemaphoreType.DMA((3 * n,)), pltpu.SemaphoreType.DMA((3 * n,))],
    )(*pairs)


def _rs_final_add(name, chip, pair, recv):
    _, half, cols = pair.shape
    tr = _row_tile(half)

    def body(chip_ref, p_ref, r_ref, o_ref):
        o_ref[...] = ((p_ref[...].astype(F32) + r_ref[0].astype(F32)) + r_ref[1].astype(F32)) + r_ref[2].astype(F32)

    return pl.pallas_call(
        body, name=name,
        grid_spec=pltpu.PrefetchScalarGridSpec(
            num_scalar_prefetch=1, grid=(half // tr,),
            in_specs=[pl.BlockSpec((None, tr, cols), lambda r, ch: (ch[0], r, 0)),
                      pl.BlockSpec((3, tr, cols), lambda r, ch: (0, r, 0))],
            out_specs=pl.BlockSpec((tr, cols), lambda r, ch: (r, 0))),
        out_shape=jax.ShapeDtypeStruct((half, cols), F32),
        compiler_params=_arb(1),
    )(chip, pair, recv)


def _rs_pair_share(halves):
    n = len(halves)

    def body(*refs):
        h_refs, o_refs = refs[:n], refs[n:2 * n]
        send_sems, recv_sems = refs[2 * n:]
        x, y, c = _mesh_pos()
        cps = []
        for i in range(n):
            cp = pltpu.make_async_remote_copy(src_ref=h_refs[i], dst_ref=o_refs[i], send_sem=send_sems.at[i],
                                              recv_sem=recv_sems.at[i], device_id=(x, y, 1 - c), device_id_type=MESH)
            cp.start()
            cps.append(cp)
        for cp in cps:
            cp.wait()

    return pl.pallas_call(
        body, name="rs_pair_share", in_specs=[ANY] * n, out_specs=[ANY] * n,
        out_shape=[jax.ShapeDtypeStruct(h.shape, F32) for h in halves],
        scratch_shapes=[pltpu.SemaphoreType.DMA((n,)), pltpu.SemaphoreType.DMA((n,))],
    )(*halves)


HBM = pl.BlockSpec(memory_space=pltpu.HBM)
SEM = pl.BlockSpec(memory_space=pltpu.SEMAPHORE)


class _SplitExchange:
    def __init__(self, name, srcs, land_shapes, src_of, dst_of, arrive_of):
        self.name, self.srcs, self.land_shapes = name, list(srcs), list(land_shapes)
        self.src_of, self.dst_of, self.arrive_of = src_of, dst_of, arrive_of

    def _copies(self, src_refs, land_refs, send_sems, recv_sems):
        x, y, c = _mesh_pos()
        p = 2 * x + y
        out = []
        for i, (src, land) in enumerate(zip(src_refs, land_refs)):
            for k, chip in enumerate(_other_chips(x, y)):
                pk = 2 * chip[0] + chip[1]
                sems = dict(send_sem=send_sems.at[3 * i + k], recv_sem=recv_sems.at[3 * i + k],
                            device_id=(*chip, c), device_id_type=MESH)
                sent = pltpu.make_async_remote_copy(src_ref=self.src_of(src, k, p, pk),
                                                    dst_ref=self.dst_of(land, k, p, pk), **sems)
                here = self.arrive_of(land, k, p, pk)
                out.append((sent, pltpu.make_async_remote_copy(src_ref=here, dst_ref=here, **sems)))
        return out

    def start(self):
        n = len(self.srcs)

        def body(*refs):
            for sent, _ in self._copies(refs[:n], refs[n:2 * n], refs[2 * n], refs[2 * n + 1]):
                sent.start()
            refs[-1][...] = jnp.zeros((8, LANES), F32)

        lands = [lax.empty(shape, src.dtype) for shape, src in zip(self.land_shapes, self.srcs)]
        operands = [pltpu.with_memory_space_constraint(a, pltpu.HBM) for a in self.srcs + lands]
        outs = pl.pallas_call(
            body, name=self.name + "_start",
            out_shape=(pltpu.SemaphoreType.DMA((3 * n,)), pltpu.SemaphoreType.DMA((3 * n,)),
                       *[pltpu.HBM(a.shape, a.dtype) for a in operands], jax.ShapeDtypeStruct((8, LANES), F32)),
            in_specs=[HBM] * (2 * n), out_specs=(SEM, SEM, *[HBM] * (2 * n), pl.BlockSpec(memory_space=pltpu.VMEM)),
            input_output_aliases={j: 2 + j for j in range(2 * n)},
            compiler_params=pltpu.CompilerParams(has_side_effects=pltpu.SideEffectType.DATAFLOW_SIDE_EFFECTING),
        )(*operands)
        self._sems, self._thru = outs[:2], list(outs[2:2 + 2 * n])
        return outs[-1]

    def wait(self, after):
        n = len(self.srcs)

        def body(*refs):
            for sent, arrived in self._copies(refs[:n], refs[n:2 * n], refs[2 * n], refs[2 * n + 1]):
                sent.wait_send()
                arrived.wait_recv()

        outs = pl.pallas_call(
            body, name=self.name + "_wait",
            out_shape=tuple(pltpu.HBM(a.shape, a.dtype) for a in self._thru),
            in_specs=[HBM] * (2 * n) + [SEM, SEM, ANY], out_specs=tuple([HBM] * (2 * n)),
            input_output_aliases={j: j for j in range(2 * n)},
            compiler_params=pltpu.CompilerParams(has_side_effects=pltpu.SideEffectType.DATAFLOW_SIDE_EFFECTING),
        )(*self._thru, *self._sems, after)
        return list(outs[n:])


def _small_allreduce(buf):
    rows = buf.shape[0]

    def body(in_ref, out_ref, gather_ref, send_sems, recv_sems):
        x, y, c = _mesh_pos()
        me = 4 * x + 2 * y + c
        gather_ref[me] = in_ref[...]
        cps = []
        for j in range(1, 8):
            peer = (x ^ (j >> 2), y ^ ((j >> 1) & 1), c ^ (j & 1))
            cp = pltpu.make_async_remote_copy(src_ref=in_ref, dst_ref=gather_ref.at[me], send_sem=send_sems.at[j - 1],
                                              recv_sem=recv_sems.at[j - 1], device_id=peer, device_id_type=MESH)
            cp.start()
            cps.append(cp)
        for j in range(1, 8):
            peer_id = 4 * (x ^ (j >> 2)) + 2 * (y ^ ((j >> 1) & 1)) + (c ^ (j & 1))
            slot = gather_ref.at[peer_id]
            pltpu.make_async_remote_copy(src_ref=slot, dst_ref=slot, send_sem=send_sems.at[j - 1],
                                         recv_sem=recv_sems.at[j - 1], device_id=(x, y, c),
                                         device_id_type=MESH).wait_recv()
        for cp in cps:
            cp.wait_send()
        tot = gather_ref[0]
        for d in range(1, 8):
            tot = tot + gather_ref[d]
        out_ref[...] = tot

    return pl.pallas_call(
        body, name="small_allreduce",
        in_specs=[pl.BlockSpec(memory_space=pltpu.VMEM)], out_specs=pl.BlockSpec(memory_space=pltpu.VMEM),
        out_shape=jax.ShapeDtypeStruct(buf.shape, F32),
        scratch_shapes=[pltpu.VMEM((8, rows, LANES), F32), pltpu.SemaphoreType.DMA((7,)),
                        pltpu.SemaphoreType.DMA((7,))],
    )(buf)


def _pack(parts, rows):
    flat = jnp.concatenate([p.reshape(-1).astype(F32) for p in parts])
    return jnp.pad(flat, (0, rows * LANES - flat.shape[0])).reshape(rows, LANES)


def _pack_rows(parts):
    n = sum(math.prod(p.shape) for p in parts)
    return pl.cdiv(pl.cdiv(n, LANES), 8) * 8


def _unpack(buf, shapes):
    flat = buf.reshape(-1)
    out, off = [], 0
    for shp in shapes:
        size = math.prod(shp)
        out.append(flat[off:off + size].reshape(shp))
        off += size
    return out


def _pad_w_in(w):
    z = jnp.zeros((w.shape[0], 64), w.dtype)
    return jnp.concatenate([w[:, 448:1472], w[:, 1984:3008], w[:, 3008:4032], w[:, 0:256], w[:, 1472:1728],
                            w[:, 1728:1984], w[:, 256:384], w[:, 384:448], z], axis=1)


def _unpad_w_in(p):
    return jnp.concatenate([p[:, P_QLAT:P_QLAT + 256], p[:, P_CKV:P_CKV + 128], p[:, P_KR:P_KR + 64],
                            p[:, P_QB:P_QB + 1024], p[:, P_KB:P_KB + 256], p[:, P_VB:P_VB + 256],
                            p[:, P_GA:P_GA + 1024], p[:, P_GB:P_GB + 1024]], axis=1)


def _col_shards(w):
    r, c4 = w.shape
    return w.reshape(r, 4, c4 // 4).transpose(1, 0, 2)


def _heads_major(a, heads, hd):
    s = a.shape[0]
    return a.reshape(s, heads, hd).transpose(1, 0, 2)


def _heads_minor(a):
    h, s, hd = a.shape
    return a.transpose(1, 0, 2).reshape(s, h * hd)


def _local_step(x, positions, target, norm1_g, w_in_p, q_a_norm_g, wq, kv_a_norm_g, wkv, rel_bias, sinks,
                late_weights, norm2_g, conv_w, conv_b, final_norm_g, early_grads=None):
    s = x.shape[0]
    half = QK_ROPE // 2
    inv_freq = jnp.asarray(np.float32(ROPE_THETA) ** (-np.arange(half, dtype=np.float32) / np.float32(half)))
    ang = positions.astype(F32)[:, None] * inv_freq[None, :]
    cos, sin = jnp.cos(ang), jnp.sin(ang)
    z64 = jnp.zeros((s, 64), F32)
    cos_t = jnp.concatenate([cos, cos, z64], axis=1)
    sin_t = jnp.concatenate([-sin, sin, z64], axis=1)
    bucket = _t5_bucket_table()
    sinks1 = sinks.reshape(H_B)

    h1, rstd1 = _rmsnorm_fwd("norm1_fwd", x, norm1_g, D_MODEL, 0)
    proj, proj_b = _matmul("proj", h1, w_in_p, out_shape=(s, W_IN_PAD), out_dtype=F32, grid=(s // MM_ROWS, W_IN_PAD // 1024, 1),
                           a_spec=_bs((MM_ROWS, D_MODEL), lambda i, j, k: (i, 0)), b_spec=_bs((D_MODEL, 1024), lambda i, j, k: (0, j)),
                           o_spec=_bs((MM_ROWS, 1024), lambda i, j, k: (i, j)), contract=NN, bf16_copy=True)
    qn, cn, rstd_q, rstd_c = _lat_norms(proj, q_a_norm_g, kv_a_norm_g)
    q = _q_heads(qn, wq, cos_t, sin_t)
    k, v = _kv_heads(cn, wkv, proj, cos_t, sin_t)
    o_a, lse_a = _mla_fwd(q, k, v)

    qh = _heads_major(proj_b[:, P_QB:P_QB + H_B * HD_B], H_B, HD_B)
    kh = _heads_major(proj_b[:, P_KB:P_KB + KV_B * HD_B], KV_B, HD_B)
    vh = _heads_major(proj_b[:, P_VB:P_VB + KV_B * HD_B], KV_B, HD_B)
    bias = _win_bias(bucket, rel_bias)
    o_bh, lse_b = _win_fwd(qh, kh, vh, bias, sinks1)
    o_b = _heads_minor(o_bh)

    mixed = _gate_fwd(proj, o_a, o_b)
    w_out, w_up, w_down = late_weights(mixed)
    row512 = lambda w: _bs((MM_ROWS, w), lambda i, j, k: (i, 0))
    whole = lambda r, c: _bs((r, c), lambda i, j, k: (0, 0))
    x1 = _matmul("attn_out", mixed, w_out, out_shape=(s, D_MODEL), out_dtype=F32, grid=(s // MM_ROWS, 1, 1),
                 a_spec=row512(D_MODEL), b_spec=whole(D_MODEL, D_MODEL), o_spec=row512(D_MODEL), contract=NN, add=x)
    h2, rstd2 = _rmsnorm_fwd("norm2_fwd", x1, norm2_g, D_MODEL, 0)
    u = _matmul("ffn_up", h2, w_up, out_shape=(s, 2 * D_FF), out_dtype=F32, grid=(s // MM_ROWS, 4, 1),
                a_spec=_bs((MM_ROWS, D_MODEL), lambda i, j, k: (i, 0)), b_spec=_bs((D_MODEL, D_FF // 2), lambda i, j, k: (0, j)),
                o_spec=_bs((MM_ROWS, D_FF // 2), lambda i, j, k: (i, j)), contract=NN)
    f = _convffn_fwd(u, conv_w, conv_b)
    x2 = _matmul("ffn_down", f, w_down, out_shape=(s, D_MODEL), out_dtype=F32, grid=(s // MM_ROWS, 1, 1),
                 a_spec=row512(D_FF), b_spec=whole(D_FF, D_MODEL), o_spec=row512(D_MODEL), contract=NN, add=x1)
    loss, dx2, d_final_g, dx2_b = _final_loss(x2, target, final_norm_g.reshape(1, D_MODEL))
    tk = min(s, DW_ROWS)

    df = _matmul("ffn_down_dx", dx2_b, w_down, out_shape=(s, D_FF), out_dtype=F32, grid=(s // MM_ROWS, 2, 1),
                 a_spec=row512(D_MODEL), b_spec=_bs((D_FF // 2, D_MODEL), lambda i, j, k: (j, 0)),
                 o_spec=_bs((MM_ROWS, D_FF // 2), lambda i, j, k: (i, j)), contract=NT)
    d_w_down = _matmul("ffn_down_dw", f, dx2_b, out_shape=(D_FF, D_MODEL), out_dtype=F32, grid=(2, 1, s // tk),
                       a_spec=_bs((tk, D_FF // 2), lambda i, j, k: (k, i)), b_spec=_bs((tk, D_MODEL), lambda i, j, k: (k, 0)),
                       o_spec=_bs((D_FF // 2, D_MODEL), lambda i, j, k: (i, 0)), contract=TN)
    du, d_conv_w2, d_conv_b2 = _convffn_bwd(u, conv_w, conv_b, df)
    kc = D_FF // 2
    dh2 = _matmul("ffn_up_dx", du, w_up, out_shape=(s, D_MODEL), out_dtype=F32, grid=(s // 1024, 1, 4),
                  a_spec=_bs((None, 1024, kc), lambda i, j, k: (k // 2, i, k % 2)),
                  b_spec=_bs((D_MODEL, kc), lambda i, j, k: (0, k)),
                  o_spec=_bs((1024, D_MODEL), lambda i, j, k: (i, 0)), contract=NT)
    d_w_up = _matmul("ffn_up_dw", h2, du, out_shape=(D_MODEL, 2 * D_FF), out_dtype=F32, grid=(1, 4, s // tk),
                     a_spec=_bs((tk, D_MODEL), lambda i, j, k: (k, 0)),
                     b_spec=_bs((None, tk, kc), lambda i, j, k: (j // 2, k, j % 2)),
                     o_spec=_bs((D_MODEL, kc), lambda i, j, k: (0, j)), contract=TN)
    dx1, d_norm2_g, dx1_b = _rmsnorm_bwd("norm2_bwd", dh2, x1, rstd2, norm2_g, D_MODEL, 0, F32, res=dx2, bf16_copy=True)

    d_w_out = _matmul("attn_out_dw", mixed, dx1_b, out_shape=(D_MODEL, D_MODEL), out_dtype=F32, grid=(1, 1, s // tk),
                      a_spec=_bs((tk, D_MODEL), lambda i, j, k: (k, 0)), b_spec=_bs((tk, D_MODEL), lambda i, j, k: (k, 0)),
                      o_spec=whole(D_MODEL, D_MODEL), contract=TN)
    if early_grads is not None:
        token = early_grads(d_w_out, d_w_up, d_w_down)
        if token is not None:
            sinks1 = sinks1 + token[0, :H_B]
    dmixed = _matmul("attn_out_dx", dx1_b, w_out, out_shape=(s, D_MODEL), out_dtype=F32, grid=(s // MM_ROWS, 1, 1),
                     a_spec=row512(D_MODEL), b_spec=whole(D_MODEL, D_MODEL), o_spec=row512(D_MODEL), contract=NT)
    do_a, do_b, d_ga, d_gb = _gate_bwd(dmixed, proj, o_a, o_b)

    doh = _heads_major(do_b, H_B, HD_B)
    dqh, dkh_pad, dvh_pad, dbias, dsink_rows = _win_bwd(qh, kh, vh, bias, sinks1, doh, lse_b)
    wp = _win_param_grads(bucket, dbias, dsink_rows)[:, 0, :]
    d_rel_bias = wp[:, :NUM_BUCKETS].T
    d_sinks = wp[:, NUM_BUCKETS].reshape(1, H_B)
    d_qb = _heads_minor(dqh).astype(BF16)
    d_kb = _heads_minor(dkh_pad[:, WINDOW:WINDOW + s]).astype(BF16)
    d_vb = _heads_minor(dvh_pad[:, WINDOW:WINDOW + s]).astype(BF16)

    dq, dk, dv = _mla_bwd(q, k, v, do_a, o_a, lse_a)
    dq_pre, dkv_pre, d_kr = _mla_bwd_prep(dq, dk, dv, cos_t, sin_t)
    th = min(s, HEAD_ROWS)
    hgrid = (s // th, 1, H_A)
    hblock = _bs((None, th, HEAD_PAD), lambda i, j, k: (k, i, 0))
    hrows = lambda w: _bs((th, w), lambda i, j, k: (i, 0))
    dqn = _matmul("q_up_dx", dq_pre, wq, out_shape=(s, Q_LORA), out_dtype=F32, grid=hgrid, a_spec=hblock,
                  b_spec=_bs((None, Q_LORA, HEAD_PAD), lambda i, j, k: (k, 0, 0)), o_spec=hrows(Q_LORA), contract=NT)
    dcn = _matmul("kv_up_dx", dkv_pre, wkv, out_shape=(s, KV_LORA), out_dtype=F32, grid=hgrid, a_spec=hblock,
                  b_spec=_bs((None, KV_LORA, HEAD_PAD), lambda i, j, k: (k, 0, 0)), o_spec=hrows(KV_LORA), contract=NT)
    wgrid = (H_A, 1, s // th)
    d_wq = _matmul("q_up_dw", qn, dq_pre, out_shape=(H_A, Q_LORA, HEAD_PAD), out_dtype=F32, grid=wgrid,
                   a_spec=_bs((th, Q_LORA), lambda i, j, k: (k, 0)), b_spec=_bs((None, th, HEAD_PAD), lambda i, j, k: (i, k, 0)),
                   o_spec=_bs((None, Q_LORA, HEAD_PAD), lambda i, j, k: (i, 0, 0)), contract=TN)
    d_wkv = _matmul("kv_up_dw", cn, dkv_pre, out_shape=(H_A, KV_LORA, HEAD_PAD), out_dtype=F32, grid=wgrid,
                    a_spec=_bs((th, KV_LORA), lambda i, j, k: (k, 0)), b_spec=_bs((None, th, HEAD_PAD), lambda i, j, k: (i, k, 0)),
                    o_spec=_bs((None, KV_LORA, HEAD_PAD), lambda i, j, k: (i, 0, 0)), contract=TN)
    d_qlat, d_gq = _rmsnorm_bwd("q_norm_bwd", dqn, proj, rstd_q, q_a_norm_g, Q_LORA, P_QLAT // Q_LORA, BF16)
    d_ckv, d_gkv = _rmsnorm_bwd("kv_norm_bwd", dcn, proj, rstd_c, kv_a_norm_g, KV_LORA, P_CKV // KV_LORA, BF16)

    dproj = jnp.concatenate([d_qb, d_ga, d_gb, d_qlat, d_kb, d_vb, d_ckv, d_kr], axis=1)
    dh1 = _matmul("proj_dx", dproj, w_in_p, out_shape=(s, D_MODEL), out_dtype=F32, grid=(s // 1024, 1, W_IN_PAD // 1024),
                  a_spec=_bs((1024, 1024), lambda i, j, k: (i, k)), b_spec=_bs((D_MODEL, 1024), lambda i, j, k: (0, k)),
                  o_spec=_bs((1024, D_MODEL), lambda i, j, k: (i, 0)), contract=NT)
    d_w_in_p = _matmul("proj_dw", h1, dproj, out_shape=(D_MODEL, W_IN_PAD), out_dtype=F32, grid=(1, W_IN_PAD // 1024, s // tk),
                       a_spec=_bs((tk, D_MODEL), lambda i, j, k: (k, 0)), b_spec=_bs((tk, 1024), lambda i, j, k: (k, j)),
                       o_spec=_bs((D_MODEL, 1024), lambda i, j, k: (0, j)), contract=TN)
    dx, d_norm1_g = _rmsnorm_bwd("norm1_bwd", dh1, x, rstd1, norm1_g, D_MODEL, 0, F32, res=dx1)

    grads = dict(
        norm1_g=d_norm1_g, w_in_p=d_w_in_p, q_a_norm_g=d_gq, wq=d_wq, kv_a_norm_g=d_gkv, wkv=d_wkv,
        rel_bias=d_rel_bias, sinks=d_sinks, w_out=d_w_out, norm2_g=d_norm2_g, w_up=d_w_up,
        conv_w=jnp.concatenate([d_conv_w2[0], d_conv_w2[1]], axis=1),
        conv_b=jnp.concatenate([d_conv_b2[0], d_conv_b2[1]], axis=1),
        w_down=d_w_down, final_norm_g=d_final_g.reshape(D_MODEL))
    return loss, dx, grads


def _wq_heads(w_q_b):
    w = w_q_b.reshape(Q_LORA, H_A, QK_NOPE + QK_ROPE).transpose(1, 0, 2)
    return jnp.pad(w, ((0, 0), (0, 0), (0, HEAD_PAD - QK_NOPE - QK_ROPE)))


def _wq_unheads(d_wq):
    return d_wq[:, :, :QK_NOPE + QK_ROPE].transpose(1, 0, 2).reshape(Q_LORA, H_A * (QK_NOPE + QK_ROPE))


def _wkv_heads(w_kv_b):
    return w_kv_b.reshape(KV_LORA, H_A, QK_NOPE + V_DIM).transpose(1, 0, 2)


def _wkv_unheads(d_wkv):
    return d_wkv.transpose(1, 0, 2).reshape(KV_LORA, H_A * (QK_NOPE + V_DIM))


SMALL = ("norm1_g", "q_a_norm_g", "kv_a_norm_g", "rel_bias", "sinks", "norm2_g", "conv_b", "final_norm_g")
FIRST = ("w_in", "w_q_b", "w_kv_b")
LATER = ("w_out", "w_up", "w_down")
BIG = FIRST + LATER


def kernel(x, positions, norm1_g, w_in, q_a_norm_g, w_q_b, kv_a_norm_g, w_kv_b, rel_bias, sinks, w_out, norm2_g, w_up, conv_w, conv_b, w_down, final_norm_g, loss_target, m_norm1_g, m_w_in, m_q_a_norm_g, m_w_q_b, m_kv_a_norm_g, m_w_kv_b, m_rel_bias, m_sinks, m_w_out, m_norm2_g, m_w_up, m_conv_w, m_conv_b, m_w_down, m_final_norm_g, v_norm1_g, v_w_in, v_q_a_norm_g, v_w_q_b, v_kv_a_norm_g, v_w_kv_b, v_rel_bias, v_sinks, v_w_out, v_norm2_g, v_w_up, v_conv_w, v_conv_b, v_w_down, v_final_norm_g):
    weights = dict(norm1_g=norm1_g, w_in=w_in, q_a_norm_g=q_a_norm_g, w_q_b=w_q_b, kv_a_norm_g=kv_a_norm_g,
                   w_kv_b=w_kv_b, rel_bias=rel_bias, sinks=sinks, w_out=w_out, norm2_g=norm2_g, w_up=w_up,
                   conv_w=conv_w, conv_b=conv_b, w_down=w_down, final_norm_g=final_norm_g)
    mom_m = dict(norm1_g=m_norm1_g, w_in=m_w_in, q_a_norm_g=m_q_a_norm_g, w_q_b=m_w_q_b, kv_a_norm_g=m_kv_a_norm_g,
                 w_kv_b=m_w_kv_b, rel_bias=m_rel_bias, sinks=m_sinks, w_out=m_w_out, norm2_g=m_norm2_g, w_up=m_w_up,
                 conv_w=m_conv_w, conv_b=m_conv_b, w_down=m_w_down, final_norm_g=m_final_norm_g)
    mom_v = dict(norm1_g=v_norm1_g, w_in=v_w_in, q_a_norm_g=v_q_a_norm_g, w_q_b=v_w_q_b, kv_a_norm_g=v_kv_a_norm_g,
                 w_kv_b=v_w_kv_b, rel_bias=v_rel_bias, sinks=v_sinks, w_out=v_w_out, norm2_g=v_norm2_g, w_up=v_w_up,
                 conv_w=v_conv_w, conv_b=v_conv_b, w_down=v_w_down, final_norm_g=v_final_norm_g)
    shard2d = {n: weights[n][0] for n in BIG}
    conv_w_shard = conv_w[0]
    xi, yi, ci = lax.axis_index("x"), lax.axis_index("y"), lax.axis_index("c")
    chip = (2 * xi + yi).astype(jnp.int32)

    core = ci.astype(jnp.int32).reshape(1)
    chip1 = chip.reshape(1)
    cat_cols = lambda a: jnp.concatenate([a[0], a[1], a[2], a[3]], axis=1)
    own_slot = lambda a, own: lax.dynamic_update_index_in_dim(a, own, chip, 0)
    halved = lambda a: a.reshape((2, a.shape[0] // 2) + a.shape[1:])
    quartered = lambda a: a.reshape(4, 2, a.shape[1] // 2, a.shape[2])

    send = [halved(shard2d[n].astype(BF16)) for n in FIRST] + [conv_w_shard]
    gathered = [own_slot(a, own) for a, own in zip(_allgather_weights(send, split=[True] * len(FIRST) + [False]), send)]
    g = {n: a.reshape((4,) + shard2d[n].shape) for n, a in zip(FIRST, gathered)}
    w_in_p = _pad_w_in(cat_cols(g["w_in"]))
    wq = _wq_heads(cat_cols(g["w_q_b"]))
    wkv = _wkv_heads(cat_cols(g["w_kv_b"]))
    conv_w_f = cat_cols(gathered[-1])

    later = [shard2d[n].astype(BF16) for n in LATER]
    gather2 = _SplitExchange("gather_later", later, [(4,) + a.shape for a in later],
                             src_of=lambda ref, k, p, pk: ref, dst_of=lambda ref, k, p, pk: ref.at[p],
                             arrive_of=lambda ref, k, p, pk: ref.at[pk])
    norm1_g_in = norm1_g + gather2.start()[:1, :1]

    def late_weights(after):
        w_out_g, w_up_g, w_down_g = [own_slot(a, own) for a, own in zip(gather2.wait(after), later)]
        return w_out_g.reshape(D_MODEL, D_MODEL), cat_cols(w_up_g), w_down_g.reshape(D_FF, D_MODEL)

    early = {}

    def early_grads(d_w_out, d_w_up, d_w_down):
        grads = [quartered(d_w_out.reshape(4, D_MODEL // 4, D_MODEL)), quartered(_col_shards(d_w_up)),
                 quartered(d_w_down.reshape(4, D_FF // 4, D_MODEL))]
        recv = _rs_pair_exchange("rs_pair_exchange_early", grads)
        early["pairs"] = [_rs_pair_add(f"rs_pair_add_{n}", core, gfull, r) for n, gfull, r in zip(LATER, grads, recv)]
        early["ici"] = _SplitExchange("rs_ici_early", early["pairs"], [(3,) + a.shape[1:] for a in early["pairs"]],
                                      src_of=lambda ref, k, p, pk: ref.at[pk], dst_of=lambda ref, k, p, pk: ref.at[k],
                                      arrive_of=lambda ref, k, p, pk: ref.at[k])
        return early["ici"].start()

    loss, dx, gr = _local_step(x[0], positions, loss_target[0], norm1_g_in, w_in_p, q_a_norm_g, wq, kv_a_norm_g, wkv,
                               rel_bias, sinks, late_weights, norm2_g, conv_w_f, conv_b, final_norm_g, early_grads)

    grads = [quartered(_col_shards(_unpad_w_in(gr["w_in_p"]))), quartered(_col_shards(_wq_unheads(gr["wq"]))),
             quartered(_col_shards(_wkv_unheads(gr["wkv"])))]
    recv1 = _rs_pair_exchange("rs_pair_exchange", grads)
    pairs = [_rs_pair_add(f"rs_pair_add_{n}", core, gfull, r) for n, gfull, r in zip(FIRST, grads, recv1)]
    recv2 = list(_rs_ici(pairs)) + early["ici"].wait(dx)
    pairs = pairs + early["pairs"]
    halves = [_rs_final_add(f"rs_final_add_{n}", chip1, pr, r) for n, pr, r in zip(FIRST + LATER, pairs, recv2)]
    both = lambda mine, theirs: jnp.where(ci == 0, jnp.concatenate([mine, theirs]), jnp.concatenate([theirs, mine]))
    reduced = {n: both(h, t) for n, h, t in zip(FIRST + LATER, halves, _rs_pair_share(halves))}

    small_parts = [gr[n] for n in SMALL] + [gr["conv_w"], loss]
    rows = _pack_rows(small_parts)
    summed = _unpack(_small_allreduce(_pack(small_parts, rows)), [p.shape for p in small_parts])
    small_g = dict(zip(SMALL, summed[:len(SMALL)]))
    conv_w_g = lax.dynamic_slice_in_dim(summed[len(SMALL)], chip * (2 * D_FF // 4), 2 * D_FF // 4, axis=1)
    loss_out = summed[-1].reshape(())

    out_g, out_d, out_m, out_v = {}, {}, {}, {}
    for n in BIG:
        gsh = reduced[n]
        d, nm, nv = _adamw(f"adamw_{n}", shard2d[n], gsh, mom_m[n][0], mom_v[n][0])
        out_g[n], out_d[n], out_m[n], out_v[n] = gsh[None], d[None], nm[None], nv[None]
    names = SMALL + ("conv_w",)
    shapes = [weights[n].shape for n in names]
    sg = [small_g[n].reshape(weights[n].shape) for n in SMALL] + [conv_w_g[None]]
    prow = _pack_rows([weights[n] for n in names])
    d, nm, nv = _adamw("adamw_small", _pack([weights[n] for n in names], prow), _pack(sg, prow),
                       _pack([mom_m[n] for n in names], prow), _pack([mom_v[n] for n in names], prow))
    for n, gg, dd, mm, vv in zip(names, sg, _unpack(d, shapes), _unpack(nm, shapes), _unpack(nv, shapes)):
        out_g[n], out_d[n], out_m[n], out_v[n] = gg, dd, mm, vv

    order = ("norm1_g", "w_in", "q_a_norm_g", "w_q_b", "kv_a_norm_g", "w_kv_b", "rel_bias", "sinks", "w_out",
             "norm2_g", "w_up", "conv_w", "conv_b", "w_down", "final_norm_g")
    return (loss_out, dx[None], *[out_g[n] for n in order], *[out_d[n] for n in order],
            *[out_m[n] for n in order], *[out_v[n] for n in order])
```

```python
import functools
import math

import jax
import jax.numpy as jnp
import numpy as np
from jax import lax
from jax.experimental import pallas as pl
from jax.experimental.pallas import tpu as pltpu

F32 = jnp.float32
BF16 = jnp.bfloat16
MESH = pl.DeviceIdType.MESH

D_MODEL = 1024
EPS = 1e-6
H_A = 8
QK_NOPE = 128
QK_ROPE = 64
V_DIM = 128
Q_LORA = 256
KV_LORA = 128
ROPE_THETA = 10000.0
H_B = 16
KV_B = 4
GROUP = 4
HD_B = 64
WINDOW = 128
Q_BLOCK = 128
NUM_BUCKETS = 32
MAX_DISTANCE = 128
D_FF = 2816
HEAD_PAD = 256

ADAM_LR = 0.001
ADAM_B1 = 0.9
ADAM_B2 = 0.999
ADAM_EPS = 1e-08
ADAM_WD = 0.01
ADAM_STEP = 10

LANES = 128
P_QB, P_GA, P_GB, P_QLAT, P_KB, P_VB, P_CKV, P_KR = 0, 1024, 2048, 3072, 3328, 3584, 3840, 3968
W_IN_PAD = 4096

NT = (((1,), (1,)), ((), ()))
NN = (((1,), (0,)), ((), ()))
TN = (((0,), (0,)), ((), ()))


def _arb(n):
    return pltpu.CompilerParams(dimension_semantics=("arbitrary",) * n)


def _matmul(name, a, b, *, out_shape, out_dtype, grid, a_spec, b_spec, o_spec, contract, add=None, bf16_copy=False):
    nk = grid[2]
    acc_shape = tuple(d for d in o_spec.block_shape if d is not None)
    n_in = 3 if add is not None else 2
    n_out = 2 if bf16_copy else 1

    def body(*refs):
        a_ref, b_ref = refs[:2]
        add_ref = refs[2] if add is not None else None
        o_refs = refs[n_in:n_in + n_out]
        scratch = refs[n_in + n_out:]
        prod = lax.dot_general(a_ref[...].astype(BF16), b_ref[...].astype(BF16), contract,
                               preferred_element_type=F32)

        def finish(val):
            if add_ref is not None:
                val = add_ref[...] + val
            o_refs[0][...] = val.astype(out_dtype)
            if bf16_copy:
                o_refs[1][...] = val.astype(BF16)

        if nk == 1:
            finish(prod)
        else:
            acc_ref = scratch[0]
            k = pl.program_id(2)

            @pl.when(k == 0)
            def _():
                acc_ref[...] = prod

            @pl.when((k > 0) & (k < nk - 1))
            def _():
                acc_ref[...] += prod

            @pl.when(k == nk - 1)
            def _():
                finish(acc_ref[...] + prod)

    in_specs = [a_spec, b_spec]
    args = [a, b]
    if add is not None:
        in_specs.append(o_spec)
        args.append(add)
    out_shapes = [jax.ShapeDtypeStruct(out_shape, out_dtype)]
    if bf16_copy:
        out_shapes.append(jax.ShapeDtypeStruct(out_shape, BF16))
    res = pl.pallas_call(
        body, name=name, grid=grid, in_specs=in_specs, out_specs=[o_spec] * n_out, out_shape=out_shapes,
        scratch_shapes=[pltpu.VMEM(acc_shape, F32)] if nk > 1 else [],
        compiler_params=_arb(3),
    )(*args)
    return res if bf16_copy else res[0]


def _bs(block, fn):
    return pl.BlockSpec(block, fn)


def _rmsnorm_fwd(name, src, g, d, cb, ts=512):
    s = src.shape[0]

    def body(x_ref, g_ref, h_ref, r_ref):
        x = x_ref[...]
        r = lax.rsqrt(jnp.mean(x * x, axis=-1, keepdims=True) + EPS)
        h_ref[...] = (x * r * g_ref[...]).astype(BF16)
        r_ref[...] = r

    return pl.pallas_call(
        body, name=name, grid=(s // ts,),
        in_specs=[_bs((ts, d), lambda i: (i, cb)), _bs((1, d), lambda i: (0, 0))],
        out_specs=[_bs((ts, d), lambda i: (i, 0)), _bs((ts, 1), lambda i: (i, 0))],
        out_shape=[jax.ShapeDtypeStruct((s, d), BF16), jax.ShapeDtypeStruct((s, 1), F32)],
        compiler_params=_arb(1),
    )(src, g)


def _rmsnorm_bwd(name, dy, src, rstd, g, d, cb, out_dtype, res=None, bf16_copy=False, ts=512):
    s = src.shape[0]

    def body(*refs):
        dy_ref, x_ref, r_ref, g_ref = refs[:4]
        res_ref = refs[4] if res is not None else None
        dx_ref, dg_ref = refs[n_in:n_in + 2]
        dyv = dy_ref[...]
        r = r_ref[...]
        xhat = x_ref[...] * r
        dyh = dyv * g_ref[...]
        c = jnp.mean(dyh * xhat, axis=-1, keepdims=True)
        dx = r * (dyh - xhat * c)
        if res_ref is not None:
            dx = res_ref[...] + dx
        dx_ref[...] = dx.astype(out_dtype)
        if bf16_copy:
            refs[n_in + 2][...] = dx.astype(BF16)
        part = jnp.sum(dyv * xhat, axis=0, keepdims=True)

        @pl.when(pl.program_id(0) == 0)
        def _():
            dg_ref[...] = part

        @pl.when(pl.program_id(0) > 0)
        def _():
            dg_ref[...] += part

    in_specs = [_bs((ts, d), lambda i: (i, 0)), _bs((ts, d), lambda i: (i, cb)),
                _bs((ts, 1), lambda i: (i, 0)), _bs((1, d), lambda i: (0, 0))]
    args = [dy, src, rstd, g]
    if res is not None:
        in_specs.append(_bs((ts, d), lambda i: (i, 0)))
        args.append(res)
    n_in = len(args)
    out_specs = [_bs((ts, d), lambda i: (i, 0)), _bs((1, d), lambda i: (0, 0))]
    out_shape = [jax.ShapeDtypeStruct((s, d), out_dtype), jax.ShapeDtypeStruct((1, d), F32)]
    if bf16_copy:
        out_specs.append(_bs((ts, d), lambda i: (i, 0)))
        out_shape.append(jax.ShapeDtypeStruct((s, d), BF16))
    return pl.pallas_call(
        body, name=name, grid=(s // ts,), in_specs=in_specs, out_specs=out_specs, out_shape=out_shape,
        compiler_params=_arb(1),
    )(*args)


def _final_loss(x2, target, g, ts=512):
    s, d = x2.shape

    def body(x_ref, t_ref, g_ref, loss_ref, dx_ref, dg_ref, dxb_ref):
        x = x_ref[...]
        r = lax.rsqrt(jnp.mean(x * x, axis=-1, keepdims=True) + EPS)
        xhat = x * r
        gv = g_ref[...]
        err = xhat * gv - t_ref[...]
        lpart = 0.5 * jnp.sum(jnp.mean(err * err, axis=-1, keepdims=True), axis=0, keepdims=True)
        dyv = err * (1.0 / d)
        dyh = dyv * gv
        c = jnp.mean(dyh * xhat, axis=-1, keepdims=True)
        dx = r * (dyh - xhat * c)
        dx_ref[...] = dx
        dxb_ref[...] = dx.astype(BF16)
        gpart = jnp.sum(dyv * xhat, axis=0, keepdims=True)

        @pl.when(pl.program_id(0) == 0)
        def _():
            dg_ref[...] = gpart
            loss_ref[...] = lpart

        @pl.when(pl.program_id(0) > 0)
        def _():
            dg_ref[...] += gpart
            loss_ref[...] += lpart

    return pl.pallas_call(
        body, name="final_loss", grid=(s // ts,),
        in_specs=[_bs((ts, d), lambda i: (i, 0)), _bs((ts, d), lambda i: (i, 0)), _bs((1, d), lambda i: (0, 0))],
        out_specs=[_bs((1, 1), lambda i: (0, 0)), _bs((ts, d), lambda i: (i, 0)), _bs((1, d), lambda i: (0, 0)),
                   _bs((ts, d), lambda i: (i, 0))],
        out_shape=[jax.ShapeDtypeStruct((1, 1), F32), jax.ShapeDtypeStruct((s, d), F32),
                   jax.ShapeDtypeStruct((1, d), F32), jax.ShapeDtypeStruct((s, d), BF16)],
        compiler_params=_arb(1),
    )(x2, target, g)


def _swap_halves(t):
    lane = lax.broadcasted_iota(jnp.int32, t.shape, 1)
    return jnp.where(lane < 32, pltpu.roll(t, 96, 1), pltpu.roll(t, 32, 1))


def _rope_fwd(t, cos_t, sin_t):
    return t * cos_t + _swap_halves(t) * sin_t


def _rope_bwd(dt, cos_t, sin_t):
    return dt * cos_t - _swap_halves(dt) * sin_t


def _lat_norms(proj, gq, gkv, ts=512):
    s = proj.shape[0]

    def body(q_ref, c_ref, gq_ref, gkv_ref, qn_ref, cn_ref, rq_ref, rc_ref):
        q = q_ref[...]
        rq = lax.rsqrt(jnp.mean(q * q, axis=-1, keepdims=True) + EPS)
        qn_ref[...] = (q * rq * gq_ref[...]).astype(BF16)
        rq_ref[...] = rq
        cv = c_ref[...]
        rc = lax.rsqrt(jnp.mean(cv * cv, axis=-1, keepdims=True) + EPS)
        cn_ref[...] = (cv * rc * gkv_ref[...]).astype(BF16)
        rc_ref[...] = rc

    return pl.pallas_call(
        body, name="lat_norms", grid=(s // ts,),
        in_specs=[_bs((ts, Q_LORA), lambda i: (i, P_QLAT // Q_LORA)),
                  _bs((ts, KV_LORA), lambda i: (i, P_CKV // KV_LORA)),
                  _bs((1, Q_LORA), lambda i: (0, 0)), _bs((1, KV_LORA), lambda i: (0, 0))],
        out_specs=[_bs((ts, Q_LORA), lambda i: (i, 0)), _bs((ts, KV_LORA), lambda i: (i, 0)),
                   _bs((ts, 1), lambda i: (i, 0)), _bs((ts, 1), lambda i: (i, 0))],
        out_shape=[jax.ShapeDtypeStruct((s, Q_LORA), BF16), jax.ShapeDtypeStruct((s, KV_LORA), BF16),
                   jax.ShapeDtypeStruct((s, 1), F32), jax.ShapeDtypeStruct((s, 1), F32)],
        compiler_params=_arb(1),
    )(proj, proj, gq, gkv)


HEAD_ROWS = 2048
DW_ROWS = 2048
MM_ROWS = 1024


def _q_heads(qn, wq, cos_t, sin_t):
    s = qn.shape[0]
    ts = min(s, HEAD_ROWS)

    def body(qn_ref, w_ref, cos_ref, sin_ref, q_ref):
        o = jnp.dot(qn_ref[...], w_ref[...], preferred_element_type=F32)
        q_ref[:, :LANES] = o[:, :LANES].astype(BF16)
        q_ref[:, LANES:] = _rope_fwd(o[:, LANES:], cos_ref[...], sin_ref[...]).astype(BF16)

    return pl.pallas_call(
        body, name="q_heads", grid=(H_A, s // ts),
        in_specs=[_bs((ts, Q_LORA), lambda h, i: (i, 0)), _bs((None, Q_LORA, HEAD_PAD), lambda h, i: (h, 0, 0)),
                  _bs((ts, LANES), lambda h, i: (i, 0)), _bs((ts, LANES), lambda h, i: (i, 0))],
        out_specs=_bs((None, ts, HEAD_PAD), lambda h, i: (h, i, 0)),
        out_shape=jax.ShapeDtypeStruct((H_A, s, HEAD_PAD), BF16),
        compiler_params=_arb(2),
    )(qn, wq, cos_t, sin_t)


def _kv_heads(cn, wkv, proj, cos_t, sin_t):
    s = cn.shape[0]
    ts = min(s, HEAD_ROWS)

    def body(cn_ref, w_ref, kr_ref, cos_ref, sin_ref, k_ref, v_ref):
        o = jnp.dot(cn_ref[...], w_ref[...], preferred_element_type=F32)
        k_ref[:, :LANES] = o[:, :LANES].astype(BF16)
        k_ref[:, LANES:] = _rope_fwd(kr_ref[...], cos_ref[...], sin_ref[...]).astype(BF16)
        v_ref[...] = o[:, LANES:].astype(BF16)

    return pl.pallas_call(
        body, name="kv_heads", grid=(H_A, s // ts),
        in_specs=[_bs((ts, KV_LORA), lambda h, i: (i, 0)),
                  _bs((None, KV_LORA, QK_NOPE + V_DIM), lambda h, i: (h, 0, 0)),
                  _bs((ts, LANES), lambda h, i: (i, P_KR // LANES)),
                  _bs((ts, LANES), lambda h, i: (i, 0)), _bs((ts, LANES), lambda h, i: (i, 0))],
        out_specs=[_bs((None, ts, HEAD_PAD), lambda h, i: (h, i, 0)), _bs((None, ts, V_DIM), lambda h, i: (h, i, 0))],
        out_shape=[jax.ShapeDtypeStruct((H_A, s, HEAD_PAD), BF16), jax.ShapeDtypeStruct((H_A, s, V_DIM), BF16)],
        compiler_params=_arb(2),
    )(cn, wkv, proj, cos_t, sin_t)


MLA_SCALE = 1.0 / math.sqrt(QK_NOPE + QK_ROPE)
LOG2E = math.log2(math.e)
MLA_EXP2_SCALE = MLA_SCALE * LOG2E


def _lane_tiles(a):
    return [a[:, j * LANES:(j + 1) * LANES] for j in range(a.shape[1] // LANES)]


def _mla_fwd(q, k, v, tq=512, tk=512):
    s = q.shape[1]
    nk = s // tk

    def body(q_ref, k_ref, v_ref, o_ref, lse_ref, m_ref, l_ref, acc_ref):
        m_ref[...] = jnp.full(m_ref.shape, -jnp.inf, F32)
        l_ref[...] = jnp.zeros(l_ref.shape, F32)
        acc_ref[...] = jnp.zeros(acc_ref.shape, F32)
        qv = q_ref[...]

        def step(c, carry):
            rows = pl.ds(pl.multiple_of(c * tk, tk), tk)
            raw = lax.dot_general(qv, k_ref[rows, :], NT, preferred_element_type=F32)
            m_prev = m_ref[...]
            m_new = jnp.maximum(m_prev, jnp.max(raw, axis=-1, keepdims=True))
            alpha = jnp.exp2((m_prev - m_new) * MLA_EXP2_SCALE)
            ps = [jnp.exp2((t - m_new) * MLA_EXP2_SCALE) for t in _lane_tiles(raw)]
            l_ref[...] = alpha * l_ref[...] + functools.reduce(lambda a, b: a + b, ps)
            p = jnp.concatenate(ps, axis=1).astype(BF16)
            acc_ref[...] = alpha * acc_ref[...] + jnp.dot(p, v_ref[rows, :], preferred_element_type=F32)
            m_ref[...] = m_new
            return carry

        lax.fori_loop(0, nk, step, 0, unroll=True)
        l = jnp.sum(l_ref[...], axis=-1, keepdims=True)
        o_ref[...] = acc_ref[...] / l
        lse_ref[...] = m_ref[...] * MLA_SCALE + jnp.log(l)

    return pl.pallas_call(
        body, name="mla_fwd", grid=(H_A, s // tq),
        in_specs=[_bs((None, tq, HEAD_PAD), lambda h, i: (h, i, 0)),
                  _bs((None, s, HEAD_PAD), lambda h, i: (h, 0, 0)),
                  _bs((None, s, V_DIM), lambda h, i: (h, 0, 0))],
        out_specs=[_bs((tq, V_DIM), lambda h, i: (i, h)), _bs((None, tq, LANES), lambda h, i: (h, i, 0))],
        out_shape=[jax.ShapeDtypeStruct((s, H_A * V_DIM), F32), jax.ShapeDtypeStruct((H_A, s, LANES), F32)],
        scratch_shapes=[pltpu.VMEM((tq, LANES), F32), pltpu.VMEM((tq, LANES), F32), pltpu.VMEM((tq, V_DIM), F32)],
        compiler_params=_arb(2),
    )(q, k, v)


def _mla_bwd(q, k, v, do, o, lse, tq=512, tk=512):
    s = q.shape[1]
    nq = s // tq

    def body(q_ref, k_ref, v_ref, do_ref, o_ref, lse_ref, dq_ref, dk_ref, dv_ref, delta_ref):
        @pl.when(pl.program_id(1) == 0)
        def _():
            def init(c, carry):
                rows = pl.ds(pl.multiple_of(c * tq, tq), tq)
                delta = jnp.sum(do_ref[rows, :] * o_ref[rows, :], axis=-1, keepdims=True)
                delta_ref[rows, :] = jnp.broadcast_to(delta, (tq, LANES))
                dq_ref[rows, :] = jnp.zeros((tq, HEAD_PAD), F32)
                return carry

            lax.fori_loop(0, nq, init, 0)

        dk_ref[...] = jnp.zeros(dk_ref.shape, F32)
        dv_ref[...] = jnp.zeros(dv_ref.shape, F32)
        kb = k_ref[...]
        vb = v_ref[...]

        def step(c, carry):
            rows = pl.ds(pl.multiple_of(c * tq, tq), tq)
            qc = q_ref[rows, :]
            doc = do_ref[rows, :].astype(BF16)
            raw = lax.dot_general(qc, kb, NT, preferred_element_type=F32)
            dp = lax.dot_general(doc, vb, NT, preferred_element_type=F32)
            lse2 = lse_ref[rows, :] * LOG2E
            delta = delta_ref[rows, :]
            ps = [jnp.exp2(t * MLA_EXP2_SCALE - lse2) for t in _lane_tiles(raw)]
            dss = [pj * (dj - delta) * MLA_SCALE for pj, dj in zip(ps, _lane_tiles(dp))]
            p = jnp.concatenate(ps, axis=1).astype(BF16)
            ds = jnp.concatenate(dss, axis=1).astype(BF16)
            dv_ref[...] += lax.dot_general(p, doc, TN, preferred_element_type=F32)
            dk_ref[...] += lax.dot_general(ds, qc, TN, preferred_element_type=F32)
            dq_ref[rows, :] += jnp.dot(ds, kb, preferred_element_type=F32)
            return carry

        lax.fori_loop(0, nq, step, 0, unroll=True)

    return pl.pallas_call(
        body, name="mla_bwd", grid=(H_A, s // tk),
        in_specs=[_bs((None, s, HEAD_PAD), lambda h, j: (h, 0, 0)),
                  _bs((None, tk, HEAD_PAD), lambda h, j: (h, j, 0)),
                  _bs((None, tk, V_DIM), lambda h, j: (h, j, 0)),
                  _bs((s, V_DIM), lambda h, j: (0, h)), _bs((s, V_DIM), lambda h, j: (0, h)),
                  _bs((None, s, LANES), lambda h, j: (h, 0, 0))],
        out_specs=[_bs((None, s, HEAD_PAD), lambda h, j: (h, 0, 0)),
                   _bs((None, tk, HEAD_PAD), lambda h, j: (h, j, 0)),
                   _bs((None, tk, V_DIM), lambda h, j: (h, j, 0))],
        out_shape=[jax.ShapeDtypeStruct((H_A, s, HEAD_PAD), F32), jax.ShapeDtypeStruct((H_A, s, HEAD_PAD), F32),
                   jax.ShapeDtypeStruct((H_A, s, V_DIM), F32)],
        scratch_shapes=[pltpu.VMEM((s, LANES), F32)],
        compiler_params=_arb(2),
    )(q, k, v, do, o, lse)


def _mla_bwd_prep(dq, dk, dv, cos_t, sin_t, ts=256):
    s = dq.shape[1]

    def body(dq_ref, dk_ref, dv_ref, cos_ref, sin_ref, dqp_ref, dkvp_ref, dkr_ref):
        cos_v = cos_ref[...]
        sin_v = sin_ref[...]
        kr = jnp.zeros((ts, LANES), F32)
        for h in range(H_A):
            dqp_ref[h, :, :LANES] = dq_ref[h, :, :LANES].astype(BF16)
            dqp_ref[h, :, LANES:] = _rope_bwd(dq_ref[h, :, LANES:], cos_v, sin_v).astype(BF16)
            dkvp_ref[h, :, :LANES] = dk_ref[h, :, :LANES].astype(BF16)
            dkvp_ref[h, :, LANES:] = dv_ref[h].astype(BF16)
            kr = kr + dk_ref[h, :, LANES:]
        dkr_ref[...] = _rope_bwd(kr, cos_v, sin_v).astype(BF16)

    blk3 = lambda w: _bs((H_A, ts, w), lambda i: (0, i, 0))
    return pl.pallas_call(
        body, name="mla_bwd_prep", grid=(s // ts,),
        in_specs=[blk3(HEAD_PAD), blk3(HEAD_PAD), blk3(V_DIM),
                  _bs((ts, LANES), lambda i: (i, 0)), _bs((ts, LANES), lambda i: (i, 0))],
        out_specs=[blk3(HEAD_PAD), blk3(HEAD_PAD), _bs((ts, LANES), lambda i: (i, 0))],
        out_shape=[jax.ShapeDtypeStruct((H_A, s, HEAD_PAD), BF16), jax.ShapeDtypeStruct((H_A, s, HEAD_PAD), BF16),
                   jax.ShapeDtypeStruct((s, LANES), BF16)],
        compiler_params=_arb(1),
    )(dq, dk, dv, cos_t, sin_t)


WIN_SCALE = 1.0 / math.sqrt(HD_B)
SPAN = Q_BLOCK + 2 * WINDOW


def _t5_bucket_table():
    a = jnp.arange(Q_BLOCK, dtype=jnp.int32)[:, None]
    c = jnp.arange(SPAN, dtype=jnp.int32)[None, :]
    rel = c - WINDOW - a
    nb = NUM_BUCKETS // 2
    max_exact = nb // 2
    base = (rel > 0).astype(jnp.int32) * nb
    n = jnp.abs(rel)
    nf = jnp.maximum(n, 1).astype(F32)
    large = max_exact + (jnp.log(nf / max_exact) / math.log(MAX_DISTANCE / max_exact)
                         * (nb - max_exact)).astype(jnp.int32)
    large = jnp.minimum(large, nb - 1)
    return base + jnp.where(n < max_exact, n, large)


def _win_bias(bucket, rel_bias):
    def body(rb_ref, bk_ref, o_ref):
        h = pl.program_id(0)
        bk = bk_ref[...]
        acc = jnp.zeros((Q_BLOCK, SPAN), F32)
        for b in range(NUM_BUCKETS):
            acc = jnp.where(bk == b, rb_ref[b, h], acc)
        o_ref[...] = acc

    return pl.pallas_call(
        body, name="win_bias", grid=(H_B,),
        in_specs=[pl.BlockSpec(memory_space=pltpu.SMEM), _bs((Q_BLOCK, SPAN), lambda h: (0, 0))],
        out_specs=_bs((None, Q_BLOCK, SPAN), lambda h: (h, 0, 0)),
        out_shape=jax.ShapeDtypeStruct((H_B, Q_BLOCK, SPAN), F32),
        compiler_params=_arb(1),
    )(rel_bias, bucket)


WIN_NQ = 4


def _win_kv_rows(n, j, nblk):
    blk = jnp.clip(n + j - 1, 0, nblk - 1)
    return pl.ds(pl.multiple_of(blk * Q_BLOCK, Q_BLOCK), Q_BLOCK)


def _win_scores(q, k_ref, bias_ref, n, nblk):
    a = lax.broadcasted_iota(jnp.int32, (GROUP, Q_BLOCK, Q_BLOCK), 1)
    cc = lax.broadcasted_iota(jnp.int32, (GROUP, Q_BLOCK, Q_BLOCK), 2)
    valid = [(cc >= a) & (n > 0), None, (cc <= a) & (n < nblk - 1)]
    out = []
    for j in range(3):
        sc = lax.dot_general(q, k_ref[_win_kv_rows(n, j, nblk), :], NT, preferred_element_type=F32)
        sc = sc.reshape(GROUP, Q_BLOCK, Q_BLOCK) * WIN_SCALE + bias_ref[:, :, j * Q_BLOCK:(j + 1) * Q_BLOCK]
        if valid[j] is not None:
            sc = jnp.where(valid[j], sc, -1e30)
        out.append(sc)
    return out


def _win_sink(sink_ref, kv):
    hs = lax.broadcasted_iota(jnp.int32, (GROUP, Q_BLOCK, 1), 0)
    sk = jnp.zeros((GROUP, Q_BLOCK, 1), F32)
    for g in range(GROUP):
        sk = jnp.where(hs == g, sink_ref[kv * GROUP + g], sk)
    return sk


def _win_fwd(qh, kh, vh, bias, sinks):
    s = qh.shape[1]
    nblk = s // Q_BLOCK
    rows = GROUP * Q_BLOCK

    def body(sink_ref, q_ref, k_ref, v_ref, bias_ref, o_ref, lse_ref):
        kv = pl.program_id(0)
        sk = _win_sink(sink_ref, kv)
        for b in range(WIN_NQ):
            n = pl.program_id(1) * WIN_NQ + b
            qrows = slice(b * Q_BLOCK, (b + 1) * Q_BLOCK)
            q = q_ref[:, qrows, :].reshape(rows, HD_B)
            ss = _win_scores(q, k_ref, bias_ref, n, nblk)
            m = jnp.maximum(jnp.max(jnp.maximum(jnp.maximum(ss[0], ss[1]), ss[2]), axis=2, keepdims=True), sk)
            es = [jnp.exp(sc - m) for sc in ss]
            l = jnp.sum(es[0] + es[1] + es[2], axis=2, keepdims=True) + jnp.exp(sk - m)
            acc = jnp.zeros((rows, HD_B), F32)
            for j, e in enumerate(es):
                p = (e / l).astype(BF16).reshape(rows, Q_BLOCK)
                acc = acc + jnp.dot(p, v_ref[_win_kv_rows(n, j, nblk), :], preferred_element_type=F32)
            o_ref[:, qrows, :] = acc.reshape(GROUP, Q_BLOCK, HD_B)
            lse_ref[:, qrows, :] = m + jnp.log(l)

    qspec = _bs((GROUP, WIN_NQ * Q_BLOCK, HD_B), lambda kv, i: (kv, i, 0))
    head = _bs((None, s, HD_B), lambda kv, i: (kv, 0, 0))
    return pl.pallas_call(
        body, name="win_fwd", grid=(KV_B, nblk // WIN_NQ),
        in_specs=[pl.BlockSpec(memory_space=pltpu.SMEM), qspec, head, head,
                  _bs((GROUP, Q_BLOCK, SPAN), lambda kv, i: (kv, 0, 0))],
        out_specs=[qspec, _bs((GROUP, WIN_NQ * Q_BLOCK, 1), lambda kv, i: (kv, i, 0))],
        out_shape=[jax.ShapeDtypeStruct((H_B, s, HD_B), F32), jax.ShapeDtypeStruct((H_B, s, 1), F32)],
        compiler_params=_arb(2),
    )(sinks, qh, kh, vh, bias)


def _win_bwd(qh, kh, vh, bias, sinks, doh, lse):
    s = qh.shape[1]
    nblk = s // Q_BLOCK
    rows = GROUP * Q_BLOCK
    spad = s + 2 * WINDOW

    def body(sink_ref, q_ref, k_ref, v_ref, bias_ref, do_ref, lse_ref, dq_ref, dk_ref, dv_ref, db_ref, dsk_ref):
        kv = pl.program_id(0)

        @pl.when(pl.program_id(1) == 0)
        def _():
            dk_ref[...] = jnp.zeros(dk_ref.shape, F32)
            dv_ref[...] = jnp.zeros(dv_ref.shape, F32)
            db_ref[...] = jnp.zeros(db_ref.shape, F32)
            dsk_ref[...] = jnp.zeros(dsk_ref.shape, F32)

        sk = _win_sink(sink_ref, kv)
        for b in range(WIN_NQ):
            n = pl.program_id(1) * WIN_NQ + b
            qrows = slice(b * Q_BLOCK, (b + 1) * Q_BLOCK)
            q = q_ref[:, qrows, :].reshape(rows, HD_B)
            dob = do_ref[:, qrows, :].reshape(rows, HD_B).astype(BF16)
            lse_v = lse_ref[:, qrows, :]
            ss = _win_scores(q, k_ref, bias_ref, n, nblk)
            ps = [jnp.exp(sc - lse_v) for sc in ss]
            dps = [lax.dot_general(dob, v_ref[_win_kv_rows(n, j, nblk), :], NT,
                                   preferred_element_type=F32).reshape(GROUP, Q_BLOCK, Q_BLOCK) for j in range(3)]
            delta = jnp.sum(ps[0] * dps[0] + ps[1] * dps[1] + ps[2] * dps[2], axis=2, keepdims=True)
            dq = jnp.zeros((rows, HD_B), F32)
            for j in range(3):
                ds = ps[j] * (dps[j] - delta)
                db_ref[:, :, j * Q_BLOCK:(j + 1) * Q_BLOCK] += ds
                dsb = (ds * WIN_SCALE).astype(BF16).reshape(rows, Q_BLOCK)
                dq = dq + jnp.dot(dsb, k_ref[_win_kv_rows(n, j, nblk), :], preferred_element_type=F32)
                krows = pl.ds(pl.multiple_of((n + j) * Q_BLOCK, Q_BLOCK), Q_BLOCK)
                dk_ref[krows, :] += lax.dot_general(dsb, q, TN, preferred_element_type=F32)
                dv_ref[krows, :] += lax.dot_general(ps[j].astype(BF16).reshape(rows, Q_BLOCK), dob, TN,
                                                    preferred_element_type=F32)
            dsk_ref[...] += -(jnp.exp(sk - lse_v) * delta)
            dq_ref[:, qrows, :] = dq.reshape(GROUP, Q_BLOCK, HD_B)

    qspec = _bs((GROUP, WIN_NQ * Q_BLOCK, HD_B), lambda kv, i: (kv, i, 0))
    head = _bs((None, s, HD_B), lambda kv, i: (kv, 0, 0))
    kacc = _bs((None, spad, HD_B), lambda kv, i: (kv, 0, 0))
    return pl.pallas_call(
        body, name="win_bwd", grid=(KV_B, nblk // WIN_NQ),
        in_specs=[pl.BlockSpec(memory_space=pltpu.SMEM), qspec, head, head,
                  _bs((GROUP, Q_BLOCK, SPAN), lambda kv, i: (kv, 0, 0)), qspec,
                  _bs((GROUP, WIN_NQ * Q_BLOCK, 1), lambda kv, i: (kv, i, 0))],
        out_specs=[qspec, kacc, kacc, _bs((GROUP, Q_BLOCK, SPAN), lambda kv, i: (kv, 0, 0)),
                   _bs((GROUP, Q_BLOCK, 1), lambda kv, i: (kv, 0, 0))],
        out_shape=[jax.ShapeDtypeStruct((H_B, s, HD_B), F32), jax.ShapeDtypeStruct((KV_B, spad, HD_B), F32),
                   jax.ShapeDtypeStruct((KV_B, spad, HD_B), F32), jax.ShapeDtypeStruct((H_B, Q_BLOCK, SPAN), F32),
                   jax.ShapeDtypeStruct((H_B, Q_BLOCK, 1), F32)],
        compiler_params=_arb(2),
    )(sinks, qh, kh, vh, bias, doh, lse)


def _win_param_grads(bucket, dbias, dsink_rows):
    def body(bk_ref, db_ref, ds_ref, o_ref):
        bk = bk_ref[...]
        dbv = db_ref[...]
        lane = lax.broadcasted_iota(jnp.int32, (1, LANES), 1)
        res = jnp.zeros((1, LANES), F32)
        for b in range(NUM_BUCKETS):
            tot = jnp.sum(jnp.sum(jnp.where(bk == b, dbv, 0.0), axis=1, keepdims=True), axis=0, keepdims=True)
            res = jnp.where(lane == b, tot, res)
        stot = jnp.sum(ds_ref[...], axis=0, keepdims=True)
        o_ref[...] = jnp.where(lane == NUM_BUCKETS, stot, res)

    return pl.pallas_call(
        body, name="win_param_grads", grid=(H_B,),
        in_specs=[_bs((Q_BLOCK, SPAN), lambda h: (0, 0)), _bs((None, Q_BLOCK, SPAN), lambda h: (h, 0, 0)),
                  _bs((None, Q_BLOCK, 1), lambda h: (h, 0, 0))],
        out_specs=_bs((None, 1, LANES), lambda h: (h, 0, 0)),
        out_shape=jax.ShapeDtypeStruct((H_B, 1, LANES), F32),
        compiler_params=_arb(1),
    )(bucket, dbias, dsink_rows)


def _gate_fwd(proj, o_a, o_b, ts=256):
    s = o_a.shape[0]
    wide = lambda cb: _bs((ts, D_MODEL), lambda i: (i, cb))

    def body(ga_ref, gb_ref, oa_ref, ob_ref, m_ref):
        m_ref[...] = (jax.nn.sigmoid(ga_ref[...]) * oa_ref[...]
                      + jax.nn.sigmoid(gb_ref[...]) * ob_ref[...]).astype(BF16)

    return pl.pallas_call(
        body, name="gate_fwd", grid=(s // ts,),
        in_specs=[wide(P_GA // D_MODEL), wide(P_GB // D_MODEL), wide(0), wide(0)],
        out_specs=wide(0), out_shape=jax.ShapeDtypeStruct((s, D_MODEL), BF16),
        compiler_params=_arb(1),
    )(proj, proj, o_a, o_b)


def _gate_bwd(dmixed, proj, o_a, o_b, ts=256):
    s = o_a.shape[0]
    wide = lambda cb: _bs((ts, D_MODEL), lambda i: (i, cb))

    def body(dm_ref, ga_ref, gb_ref, oa_ref, ob_ref, doa_ref, dob_ref, dga_ref, dgb_ref):
        dm = dm_ref[...]
        sa = jax.nn.sigmoid(ga_ref[...])
        sb = jax.nn.sigmoid(gb_ref[...])
        doa_ref[...] = dm * sa
        dob_ref[...] = (dm * sb).astype(BF16)
        dga_ref[...] = (dm * oa_ref[...] * (sa * (1.0 - sa))).astype(BF16)
        dgb_ref[...] = (dm * ob_ref[...] * (sb * (1.0 - sb))).astype(BF16)

    return pl.pallas_call(
        body, name="gate_bwd", grid=(s // ts,),
        in_specs=[wide(0), wide(P_GA // D_MODEL), wide(P_GB // D_MODEL), wide(0), wide(0)],
        out_specs=[wide(0)] * 4,
        out_shape=[jax.ShapeDtypeStruct((s, D_MODEL), F32), jax.ShapeDtypeStruct((s, D_MODEL), BF16),
                   jax.ShapeDtypeStruct((s, D_MODEL), BF16), jax.ShapeDtypeStruct((s, D_MODEL), BF16)],
        compiler_params=_arb(1),
    )(dmixed, proj, proj, o_a, o_b)


CONV_CHUNK = 512
N_SLAB = D_FF // LANES


def _shifted(ref, c, nchunks):
    r0 = c * CONV_CHUNK
    cur = ref[r0:r0 + CONV_CHUNK, :]
    row = lax.broadcasted_iota(jnp.int32, (8, LANES), 0)
    before = ref[r0 - 8:r0, :][7:8, :] if c > 0 else jnp.zeros((1, LANES), F32)
    after = ref[r0 + CONV_CHUNK:r0 + CONV_CHUNK + 8, :][0:1, :] if c < nchunks - 1 else jnp.zeros((1, LANES), F32)
    down = pltpu.roll(cur, 1, 0)
    up = pltpu.roll(cur, CONV_CHUNK - 1, 0)
    prev = jnp.concatenate([jnp.where(row == 0, before, down[:8]), down[8:]], axis=0)
    nxt = jnp.concatenate([up[:-8], jnp.where(row == 7, after, up[-8:])], axis=0)
    return prev, cur, nxt


def _conv_taps(ref, w_ref, b_ref, c, nchunks):
    prev, cur, nxt = _shifted(ref, c, nchunks)
    conv = prev * w_ref[0:1, :] + cur * w_ref[1:2, :] + nxt * w_ref[2:3, :] + b_ref[...]
    return conv, prev, cur, nxt


def _convffn_fwd(u, conv_w, conv_b):
    s = u.shape[0]
    nchunks = s // CONV_CHUNK

    def body(ug_ref, uv_ref, wg_ref, wv_ref, bg_ref, bv_ref, f_ref):
        for c in range(nchunks):
            cg = _conv_taps(ug_ref, wg_ref, bg_ref, c, nchunks)[0]
            cv = _conv_taps(uv_ref, wv_ref, bv_ref, c, nchunks)[0]
            f_ref[c * CONV_CHUNK:(c + 1) * CONV_CHUNK, :] = (cg * jax.nn.sigmoid(cg) * cv).astype(BF16)

    slab = lambda off: _bs((s, LANES), lambda j: (0, off + j))
    wsl = lambda off: _bs((3, LANES), lambda j: (0, off + j))
    bsl = lambda off: _bs((1, LANES), lambda j: (0, off + j))
    return pl.pallas_call(
        body, name="convffn_fwd", grid=(N_SLAB,),
        in_specs=[slab(0), slab(N_SLAB), wsl(0), wsl(N_SLAB), bsl(0), bsl(N_SLAB)],
        out_specs=slab(0), out_shape=jax.ShapeDtypeStruct((s, D_FF), BF16),
        compiler_params=_arb(1),
    )(u, u, conv_w, conv_w, conv_b, conv_b)


def _convffn_bwd(u, conv_w, conv_b, df):
    s = u.shape[0]
    nchunks = s // CONV_CHUNK

    def body(ug_ref, uv_ref, wg_ref, wv_ref, bg_ref, bv_ref, df_ref, du_ref, dw_ref, db_ref, dcg_ref, dcv_ref):
        dwg = [jnp.zeros((1, LANES), F32) for _ in range(3)]
        dwv = [jnp.zeros((1, LANES), F32) for _ in range(3)]
        dbg = jnp.zeros((1, LANES), F32)
        dbv = jnp.zeros((1, LANES), F32)
        for c in range(nchunks):
            rows = slice(c * CONV_CHUNK, (c + 1) * CONV_CHUNK)
            cg, gp, gc, gn = _conv_taps(ug_ref, wg_ref, bg_ref, c, nchunks)
            cv, vp, vc, vn = _conv_taps(uv_ref, wv_ref, bv_ref, c, nchunks)
            dfv = df_ref[rows, :]
            sg = jax.nn.sigmoid(cg)
            dcg = dfv * cv * (sg * (1.0 + cg * (1.0 - sg)))
            dcv = dfv * (cg * sg)
            dcg_ref[rows, :] = dcg
            dcv_ref[rows, :] = dcv
            for t, (tg, tv) in enumerate(((gp, vp), (gc, vc), (gn, vn))):
                dwg[t] = dwg[t] + jnp.sum(tg * dcg, axis=0, keepdims=True)
                dwv[t] = dwv[t] + jnp.sum(tv * dcv, axis=0, keepdims=True)
            dbg = dbg + jnp.sum(dcg, axis=0, keepdims=True)
            dbv = dbv + jnp.sum(dcv, axis=0, keepdims=True)
        for t in range(3):
            dw_ref[0, t:t + 1, :] = dwg[t]
            dw_ref[1, t:t + 1, :] = dwv[t]
        db_ref[0] = dbg
        db_ref[1] = dbv
        for half, (dc_ref, w_ref) in enumerate(((dcg_ref, wg_ref), (dcv_ref, wv_ref))):
            for c in range(nchunks):
                prev, cur, nxt = _shifted(dc_ref, c, nchunks)
                du = nxt * w_ref[0:1, :] + cur * w_ref[1:2, :] + prev * w_ref[2:3, :]
                du_ref[half, c * CONV_CHUNK:(c + 1) * CONV_CHUNK, :] = du.astype(BF16)

    slab = lambda off: _bs((s, LANES), lambda j: (0, off + j))
    wsl = lambda off: _bs((3, LANES), lambda j: (0, off + j))
    bsl = lambda off: _bs((1, LANES), lambda j: (0, off + j))
    return pl.pallas_call(
        body, name="convffn_bwd", grid=(N_SLAB,),
        in_specs=[slab(0), slab(N_SLAB), wsl(0), wsl(N_SLAB), bsl(0), bsl(N_SLAB), slab(0)],
        out_specs=[_bs((2, s, LANES), lambda j: (0, 0, j)), _bs((2, 3, LANES), lambda j: (0, 0, j)),
                   _bs((2, 1, LANES), lambda j: (0, 0, j))],
        out_shape=[jax.ShapeDtypeStruct((2, s, D_FF), BF16), jax.ShapeDtypeStruct((2, 3, D_FF), F32),
                   jax.ShapeDtypeStruct((2, 1, D_FF), F32)],
        scratch_shapes=[pltpu.VMEM((s, LANES), F32), pltpu.VMEM((s, LANES), F32)],
        compiler_params=_arb(1),
    )(u, u, conv_w, conv_w, conv_b, conv_b, df)


def _row_tile(rows, limit=512):
    best = rows
    for t in range(8, min(rows, limit) + 1, 8):
        if rows % t == 0:
            best = t
    return best if rows % 8 == 0 else rows


def _adamw(name, w, g, m, v):
    rows, cols = w.shape
    tr = _row_tile(rows)
    c1 = 1.0 - ADAM_B1 ** ADAM_STEP
    c2 = 1.0 - ADAM_B2 ** ADAM_STEP

    def body(w_ref, g_ref, m_ref, v_ref, d_ref, nm_ref, nv_ref):
        gv = g_ref[...]
        nm = ADAM_B1 * m_ref[...] + (1.0 - ADAM_B1) * gv
        nv = ADAM_B2 * v_ref[...] + (1.0 - ADAM_B2) * (gv * gv)
        m_hat = nm / c1
        v_hat = nv / c2
        d_ref[...] = -ADAM_LR * (m_hat / (jnp.sqrt(v_hat) + ADAM_EPS) + ADAM_WD * w_ref[...])
        nm_ref[...] = nm
        nv_ref[...] = nv

    spec = _bs((tr, cols), lambda i: (i, 0))
    return pl.pallas_call(
        body, name=name, grid=(rows // tr,), in_specs=[spec] * 4, out_specs=[spec] * 3,
        out_shape=[jax.ShapeDtypeStruct((rows, cols), F32)] * 3, compiler_params=_arb(1),
    )(w, g, m, v)


ANY = pl.BlockSpec(memory_space=pl.ANY)


def _mesh_pos():
    return lax.axis_index("x"), lax.axis_index("y"), lax.axis_index("c")


def _other_chips(x, y):
    return [(1 - x, y), (x, 1 - y), (1 - x, 1 - y)]


def _allgather_weights(shards, split):
    n = len(shards)

    def body(*refs):
        w_refs, o_refs = refs[:n], refs[n:2 * n]
        send_sems, recv_sems, fsend_sems, frecv_sems = refs[2 * n:]
        x, y, c = _mesh_pos()
        p = 2 * x + y
        chips = _other_chips(x, y)

        def piece(i, chip_index, core):
            return o_refs[i].at[chip_index, core] if split[i] else o_refs[i].at[chip_index]

        def remote(src, dst, ssem, rsem, to):
            return pltpu.make_async_remote_copy(src_ref=src, dst_ref=dst, send_sem=ssem, recv_sem=rsem,
                                                device_id=to, device_id_type=MESH)

        sends = []
        for i in range(n):
            src = w_refs[i].at[c] if split[i] else w_refs[i]
            for k, chip in enumerate(chips):
                cp = remote(src, piece(i, p, c), send_sems.at[3 * i + k], recv_sems.at[3 * i + k], (*chip, c))
                cp.start()
                sends.append(cp)
        for i in range(n):
            for k, chip in enumerate(chips):
                pk = 2 * chip[0] + chip[1]
                landed = piece(i, pk, c)
                remote(landed, landed, send_sems.at[3 * i + k], recv_sems.at[3 * i + k], (*chip, c)).wait_recv()
                if split[i]:
                    fw = remote(landed, landed, fsend_sems.at[3 * i + k], frecv_sems.at[3 * i + k], (x, y, 1 - c))
                    fw.start()
                    sends.append(fw)
        for i in range(n):
            if split[i]:
                for k, chip in enumerate(chips):
                    pk = 2 * chip[0] + chip[1]
                    theirs = piece(i, pk, 1 - c)
                    remote(theirs, theirs, fsend_sems.at[3 * i + k], frecv_sems.at[3 * i + k],
                           (x, y, 1 - c)).wait_recv()
        for cp in sends:
            cp.wait_send()

    return pl.pallas_call(
        body, name="allgather_weights",
        in_specs=[ANY] * n, out_specs=[ANY] * n,
        out_shape=[jax.ShapeDtypeStruct((4,) + w.shape, w.dtype) for w in shards],
        scratch_shapes=[pltpu.SemaphoreType.DMA((3 * n,)), pltpu.SemaphoreType.DMA((3 * n,)),
                        pltpu.SemaphoreType.DMA((3 * n,)), pltpu.SemaphoreType.DMA((3 * n,))],
    )(*shards)


def _rs_pair_exchange(name, grads):
    n = len(grads)

    def body(*refs):
        g_refs, o_refs = refs[:n], refs[n:2 * n]
        send_sems, recv_sems = refs[2 * n:]
        x, y, c = _mesh_pos()
        cps = []
        for i in range(n):
            cp = pltpu.make_async_remote_copy(
                src_ref=g_refs[i].at[:, 1 - c], dst_ref=o_refs[i],
                send_sem=send_sems.at[i], recv_sem=recv_sems.at[i], device_id=(x, y, 1 - c), device_id_type=MESH)
            cp.start()
            cps.append(cp)
        for cp in cps:
            cp.wait()

    return pl.pallas_call(
        body, name=name, in_specs=[ANY] * n, out_specs=[ANY] * n,
        out_shape=[jax.ShapeDtypeStruct((4,) + g.shape[2:], F32) for g in grads],
        scratch_shapes=[pltpu.SemaphoreType.DMA((n,)), pltpu.SemaphoreType.DMA((n,))],
    )(*grads)


def _rs_pair_add(name, core, g, recv):
    _, half, cols = recv.shape
    tr = _row_tile(half)
    nr = half // tr

    def body(core_ref, g_ref, r_ref, o_ref):
        o_ref[...] = (g_ref[...] + r_ref[...]).astype(BF16)

    return pl.pallas_call(
        body, name=name,
        grid_spec=pltpu.PrefetchScalarGridSpec(
            num_scalar_prefetch=1, grid=(4, nr),
            in_specs=[pl.BlockSpec((None, None, tr, cols), lambda q, r, cr: (q, cr[0], r, 0)),
                      pl.BlockSpec((None, tr, cols), lambda q, r, cr: (q, r, 0))],
            out_specs=pl.BlockSpec((None, tr, cols), lambda q, r, cr: (q, r, 0))),
        out_shape=jax.ShapeDtypeStruct((4, half, cols), BF16),
        compiler_params=_arb(2),
    )(core, g, recv)


def _rs_ici(pairs):
    n = len(pairs)

    def body(*refs):
        p_refs, o_refs = refs[:n], refs[n:2 * n]
        send_sems, recv_sems = refs[2 * n:]
        x, y, c = _mesh_pos()
        cps = []
        for i in range(n):
            for k, chip in enumerate(_other_chips(x, y)):
                pk = 2 * chip[0] + chip[1]
                cp = pltpu.make_async_remote_copy(
                    src_ref=p_refs[i].at[pk], dst_ref=o_refs[i].at[k],
                    send_sem=send_sems.at[3 * i + k], recv_sem=recv_sems.at[3 * i + k],
                    device_id=(*chip, c), device_id_type=MESH)
                cp.start()
                cps.append(cp)
        for cp in cps:
            cp.wait()

    return pl.pallas_call(
        body, name="rs_ici", in_specs=[ANY] * n, out_specs=[ANY] * n,
        out_shape=[jax.ShapeDtypeStruct((3,) + pr.shape[1:], BF16) for pr in pairs],
        scratch_shapes=[pltpu.SemaphoreType.DMA((3 * n,)), pltpu.SemaphoreType.DMA((3 * n,))],
    )(*pairs)


def _rs_final_add(name, chip, pair, recv):
    _, half, cols = pair.shape
    tr = _row_tile(half)

    def body(chip_ref, p_ref, r_ref, o_ref):
        o_ref[...] = ((p_ref[...].astype(F32) + r_ref[0].astype(F32)) + r_ref[1].astype(F32)) + r_ref[2].astype(F32)

    return pl.pallas_call(
        body, name=name,
        grid_spec=pltpu.PrefetchScalarGridSpec(
            num_scalar_prefetch=1, grid=(half // tr,),
            in_specs=[pl.BlockSpec((None, tr, cols), lambda r, ch: (ch[0], r, 0)),
                      pl.BlockSpec((3, tr, cols), lambda r, ch: (0, r, 0))],
            out_specs=pl.BlockSpec((tr, cols), lambda r, ch: (r, 0))),
        out_shape=jax.ShapeDtypeStruct((half, cols), F32),
        compiler_params=_arb(1),
    )(chip, pair, recv)


def _rs_pair_share(halves):
    n = len(halves)

    def body(*refs):
        h_refs, o_refs = refs[:n], refs[n:2 * n]
        send_sems, recv_sems = refs[2 * n:]
        x, y, c = _mesh_pos()
        cps = []
        for i in range(n):
            cp = pltpu.make_async_remote_copy(src_ref=h_refs[i], dst_ref=o_refs[i], send_sem=send_sems.at[i],
                                              recv_sem=recv_sems.at[i], device_id=(x, y, 1 - c), device_id_type=MESH)
            cp.start()
            cps.append(cp)
        for cp in cps:
            cp.wait()

    return pl.pallas_call(
        body, name="rs_pair_share", in_specs=[ANY] * n, out_specs=[ANY] * n,
        out_shape=[jax.ShapeDtypeStruct(h.shape, F32) for h in halves],
        scratch_shapes=[pltpu.SemaphoreType.DMA((n,)), pltpu.SemaphoreType.DMA((n,))],
    )(*halves)


HBM = pl.BlockSpec(memory_space=pltpu.HBM)
SEM = pl.BlockSpec(memory_space=pltpu.SEMAPHORE)


class _SplitExchange:
    def __init__(self, name, srcs, land_shapes, src_of, dst_of, arrive_of):
        self.name, self.srcs, self.land_shapes = name, list(srcs), list(land_shapes)
        self.src_of, self.dst_of, self.arrive_of = src_of, dst_of, arrive_of

    def _copies(self, src_refs, land_refs, send_sems, recv_sems):
        x, y, c = _mesh_pos()
        p = 2 * x + y
        out = []
        for i, (src, land) in enumerate(zip(src_refs, land_refs)):
            for k, chip in enumerate(_other_chips(x, y)):
                pk = 2 * chip[0] + chip[1]
                sems = dict(send_sem=send_sems.at[3 * i + k], recv_sem=recv_sems.at[3 * i + k],
                            device_id=(*chip, c), device_id_type=MESH)
                sent = pltpu.make_async_remote_copy(src_ref=self.src_of(src, k, p, pk),
                                                    dst_ref=self.dst_of(land, k, p, pk), **sems)
                here = self.arrive_of(land, k, p, pk)
                out.append((sent, pltpu.make_async_remote_copy(src_ref=here, dst_ref=here, **sems)))
        return out

    def start(self):
        n = len(self.srcs)

        def body(*refs):
            for sent, _ in self._copies(refs[:n], refs[n:2 * n], refs[2 * n], refs[2 * n + 1]):
                sent.start()
            refs[-1][...] = jnp.zeros((8, LANES), F32)

        lands = [lax.empty(shape, src.dtype) for shape, src in zip(self.land_shapes, self.srcs)]
        operands = [pltpu.with_memory_space_constraint(a, pltpu.HBM) for a in self.srcs + lands]
        outs = pl.pallas_call(
            body, name=self.name + "_start",
            out_shape=(pltpu.SemaphoreType.DMA((3 * n,)), pltpu.SemaphoreType.DMA((3 * n,)),
                       *[pltpu.HBM(a.shape, a.dtype) for a in operands], jax.ShapeDtypeStruct((8, LANES), F32)),
            in_specs=[HBM] * (2 * n), out_specs=(SEM, SEM, *[HBM] * (2 * n), pl.BlockSpec(memory_space=pltpu.VMEM)),
            input_output_aliases={j: 2 + j for j in range(2 * n)},
            compiler_params=pltpu.CompilerParams(has_side_effects=pltpu.SideEffectType.DATAFLOW_SIDE_EFFECTING),
        )(*operands)
        self._sems, self._thru = outs[:2], list(outs[2:2 + 2 * n])
        return outs[-1]

    def wait(self, after):
        n = len(self.srcs)

        def body(*refs):
            for sent, arrived in self._copies(refs[:n], refs[n:2 * n], refs[2 * n], refs[2 * n + 1]):
                sent.wait_send()
                arrived.wait_recv()

        outs = pl.pallas_call(
            body, name=self.name + "_wait",
            out_shape=tuple(pltpu.HBM(a.shape, a.dtype) for a in self._thru),
            in_specs=[HBM] * (2 * n) + [SEM, SEM, ANY], out_specs=tuple([HBM] * (2 * n)),
            input_output_aliases={j: j for j in range(2 * n)},
            compiler_params=pltpu.CompilerParams(has_side_effects=pltpu.SideEffectType.DATAFLOW_SIDE_EFFECTING),
        )(*self._thru, *self._sems, after)
        return list(outs[n:])


def _small_allreduce(buf):
    rows = buf.shape[0]

    def body(in_ref, out_ref, gather_ref, send_sems, recv_sems):
        x, y, c = _mesh_pos()
        me = 4 * x + 2 * y + c
        gather_ref[me] = in_ref[...]
        cps = []
        for j in range(1, 8):
            peer = (x ^ (j >> 2), y ^ ((j >> 1) & 1), c ^ (j & 1))
            cp = pltpu.make_async_remote_copy(src_ref=in_ref, dst_ref=gather_ref.at[me], send_sem=send_sems.at[j - 1],
                                              recv_sem=recv_sems.at[j - 1], device_id=peer, device_id_type=MESH)
            cp.start()
            cps.append(cp)
        for j in range(1, 8):
            peer_id = 4 * (x ^ (j >> 2)) + 2 * (y ^ ((j >> 1) & 1)) + (c ^ (j & 1))
            slot = gather_ref.at[peer_id]
            pltpu.make_async_remote_copy(src_ref=slot, dst_ref=slot, send_sem=send_sems.at[j - 1],
                                         recv_sem=recv_sems.at[j - 1], device_id=(x, y, c),
                                         device_id_type=MESH).wait_recv()
        for cp in cps:
            cp.wait_send()
        tot = gather_ref[0]
        for d in range(1, 8):
            tot = tot + gather_ref[d]
        out_ref[...] = tot

    return pl.pallas_call(
        body, name="small_allreduce",
        in_specs=[pl.BlockSpec(memory_space=pltpu.VMEM)], out_specs=pl.BlockSpec(memory_space=pltpu.VMEM),
        out_shape=jax.ShapeDtypeStruct(buf.shape, F32),
        scratch_shapes=[pltpu.VMEM((8, rows, LANES), F32), pltpu.SemaphoreType.DMA((7,)),
                        pltpu.SemaphoreType.DMA((7,))],
    )(buf)


def _pack(parts, rows):
    flat = jnp.concatenate([p.reshape(-1).astype(F32) for p in parts])
    return jnp.pad(flat, (0, rows * LANES - flat.shape[0])).reshape(rows, LANES)


def _pack_rows(parts):
    n = sum(math.prod(p.shape) for p in parts)
    return pl.cdiv(pl.cdiv(n, LANES), 8) * 8


def _unpack(buf, shapes):
    flat = buf.reshape(-1)
    out, off = [], 0
    for shp in shapes:
        size = math.prod(shp)
        out.append(flat[off:off + size].reshape(shp))
        off += size
    return out


def _pad_w_in(w):
    z = jnp.zeros((w.shape[0], 64), w.dtype)
    return jnp.concatenate([w[:, 448:1472], w[:, 1984:3008], w[:, 3008:4032], w[:, 0:256], w[:, 1472:1728],
                            w[:, 1728:1984], w[:, 256:384], w[:, 384:448], z], axis=1)


def _unpad_w_in(p):
    return jnp.concatenate([p[:, P_QLAT:P_QLAT + 256], p[:, P_CKV:P_CKV + 128], p[:, P_KR:P_KR + 64],
                            p[:, P_QB:P_QB + 1024], p[:, P_KB:P_KB + 256], p[:, P_VB:P_VB + 256],
                            p[:, P_GA:P_GA + 1024], p[:, P_GB:P_GB + 1024]], axis=1)


def _col_shards(w):
    r, c4 = w.shape
    return w.reshape(r, 4, c4 // 4).transpose(1, 0, 2)


def _to_heads(name, src, col0, heads, ts=512):
    s = src.shape[0]
    width = heads * HD_B

    def body(x_ref, o_ref):
        for h in range(heads):
            o_ref[h] = x_ref[:, h * HD_B:(h + 1) * HD_B]

    return pl.pallas_call(
        body, name=name, grid=(s // ts,),
        in_specs=[_bs((ts, width), lambda i: (i, col0 // width))],
        out_specs=_bs((heads, ts, HD_B), lambda i: (0, i, 0)),
        out_shape=jax.ShapeDtypeStruct((heads, s, HD_B), src.dtype),
        compiler_params=_arb(1),
    )(src)


def _from_heads(name, src, s, row0, out_dtype, ts=128):
    heads = src.shape[0]

    def body(x_ref, o_ref):
        for h in range(heads):
            o_ref[:, h * HD_B:(h + 1) * HD_B] = x_ref[h].astype(out_dtype)

    return pl.pallas_call(
        body, name=name, grid=(s // ts,),
        in_specs=[_bs((heads, ts, HD_B), lambda i: (0, i + row0 // ts, 0))],
        out_specs=_bs((ts, heads * HD_B), lambda i: (i, 0)),
        out_shape=jax.ShapeDtypeStruct((s, heads * HD_B), out_dtype),
        compiler_params=_arb(1),
    )(src)


def _local_step(x, positions, target, norm1_g, w_in_p, q_a_norm_g, wq, kv_a_norm_g, wkv, rel_bias, sinks,
                late_weights, norm2_g, conv_w, conv_b, final_norm_g, early_grads=None):
    s = x.shape[0]
    half = QK_ROPE // 2
    inv_freq = jnp.asarray(np.float32(ROPE_THETA) ** (-np.arange(half, dtype=np.float32) / np.float32(half)))
    ang = positions.astype(F32)[:, None] * inv_freq[None, :]
    cos, sin = jnp.cos(ang), jnp.sin(ang)
    z64 = jnp.zeros((s, 64), F32)
    cos_t = jnp.concatenate([cos, cos, z64], axis=1)
    sin_t = jnp.concatenate([-sin, sin, z64], axis=1)
    bucket = _t5_bucket_table()
    sinks1 = sinks.reshape(H_B)

    h1, rstd1 = _rmsnorm_fwd("norm1_fwd", x, norm1_g, D_MODEL, 0)
    proj, proj_b = _matmul("proj", h1, w_in_p, out_shape=(s, W_IN_PAD), out_dtype=F32, grid=(s // MM_ROWS, W_IN_PAD // 1024, 1),
                           a_spec=_bs((MM_ROWS, D_MODEL), lambda i, j, k: (i, 0)), b_spec=_bs((D_MODEL, 1024), lambda i, j, k: (0, j)),
                           o_spec=_bs((MM_ROWS, 1024), lambda i, j, k: (i, j)), contract=NN, bf16_copy=True)
    qn, cn, rstd_q, rstd_c = _lat_norms(proj, q_a_norm_g, kv_a_norm_g)
    q = _q_heads(qn, wq, cos_t, sin_t)
    k, v = _kv_heads(cn, wkv, proj, cos_t, sin_t)
    o_a, lse_a = _mla_fwd(q, k, v)

    qh = _to_heads("win_q_heads", proj_b, P_QB, H_B)
    kh = _to_heads("win_k_heads", proj_b, P_KB, KV_B)
    vh = _to_heads("win_v_heads", proj_b, P_VB, KV_B)
    bias = _win_bias(bucket, rel_bias)
    o_bh, lse_b = _win_fwd(qh, kh, vh, bias, sinks1)
    o_b = _from_heads("win_o_rows", o_bh, s, 0, F32, ts=512)

    mixed = _gate_fwd(proj, o_a, o_b)
    w_out, w_up, w_down = late_weights(mixed)
    row512 = lambda w: _bs((MM_ROWS, w), lambda i, j, k: (i, 0))
    whole = lambda r, c: _bs((r, c), lambda i, j, k: (0, 0))
    x1 = _matmul("attn_out", mixed, w_out, out_shape=(s, D_MODEL), out_dtype=F32, grid=(s // MM_ROWS, 1, 1),
                 a_spec=row512(D_MODEL), b_spec=whole(D_MODEL, D_MODEL), o_spec=row512(D_MODEL), contract=NN, add=x)
    h2, rstd2 = _rmsnorm_fwd("norm2_fwd", x1, norm2_g, D_MODEL, 0)
    u = _matmul("ffn_up", h2, w_up, out_shape=(s, 2 * D_FF), out_dtype=F32, grid=(s // MM_ROWS, 4, 1),
                a_spec=_bs((MM_ROWS, D_MODEL), lambda i, j, k: (i, 0)), b_spec=_bs((D_MODEL, D_FF // 2), lambda i, j, k: (0, j)),
                o_spec=_bs((MM_ROWS, D_FF // 2), lambda i, j, k: (i, j)), contract=NN)
    f = _convffn_fwd(u, conv_w, conv_b)
    x2 = _matmul("ffn_down", f, w_down, out_shape=(s, D_MODEL), out_dtype=F32, grid=(s // MM_ROWS, 1, 1),
                 a_spec=row512(D_FF), b_spec=whole(D_FF, D_MODEL), o_spec=row512(D_MODEL), contract=NN, add=x1)
    loss, dx2, d_final_g, dx2_b = _final_loss(x2, target, final_norm_g.reshape(1, D_MODEL))
    tk = min(s, DW_ROWS)

    df = _matmul("ffn_down_dx", dx2_b, w_down, out_shape=(s, D_FF), out_dtype=F32, grid=(s // MM_ROWS, 2, 1),
                 a_spec=row512(D_MODEL), b_spec=_bs((D_FF // 2, D_MODEL), lambda i, j, k: (j, 0)),
                 o_spec=_bs((MM_ROWS, D_FF // 2), lambda i, j, k: (i, j)), contract=NT)
    d_w_down = _matmul("ffn_down_dw", f, dx2_b, out_shape=(D_FF, D_MODEL), out_dtype=F32, grid=(2, 1, s // tk),
                       a_spec=_bs((tk, D_FF // 2), lambda i, j, k: (k, i)), b_spec=_bs((tk, D_MODEL), lambda i, j, k: (k, 0)),
                       o_spec=_bs((D_FF // 2, D_MODEL), lambda i, j, k: (i, 0)), contract=TN)
    du, d_conv_w2, d_conv_b2 = _convffn_bwd(u, conv_w, conv_b, df)
    kc = D_FF // 2
    dh2 = _matmul("ffn_up_dx", du, w_up, out_shape=(s, D_MODEL), out_dtype=F32, grid=(s // 1024, 1, 4),
                  a_spec=_bs((None, 1024, kc), lambda i, j, k: (k // 2, i, k % 2)),
                  b_spec=_bs((D_MODEL, kc), lambda i, j, k: (0, k)),
                  o_spec=_bs((1024, D_MODEL), lambda i, j, k: (i, 0)), contract=NT)
    d_w_up = _matmul("ffn_up_dw", h2, du, out_shape=(D_MODEL, 2 * D_FF), out_dtype=F32, grid=(1, 4, s // tk),
                     a_spec=_bs((tk, D_MODEL), lambda i, j, k: (k, 0)),
                     b_spec=_bs((None, tk, kc), lambda i, j, k: (j // 2, k, j % 2)),
                     o_spec=_bs((D_MODEL, kc), lambda i, j, k: (0, j)), contract=TN)
    dx1, d_norm2_g, dx1_b = _rmsnorm_bwd("norm2_bwd", dh2, x1, rstd2, norm2_g, D_MODEL, 0, F32, res=dx2, bf16_copy=True)

    d_w_out = _matmul("attn_out_dw", mixed, dx1_b, out_shape=(D_MODEL, D_MODEL), out_dtype=F32, grid=(1, 1, s // tk),
                      a_spec=_bs((tk, D_MODEL), lambda i, j, k: (k, 0)), b_spec=_bs((tk, D_MODEL), lambda i, j, k: (k, 0)),
                      o_spec=whole(D_MODEL, D_MODEL), contract=TN)
    if early_grads is not None:
        token = early_grads(d_w_out, d_w_up, d_w_down)
        if token is not None:
            sinks1 = sinks1 + token[0, :H_B]
    dmixed = _matmul("attn_out_dx", dx1_b, w_out, out_shape=(s, D_MODEL), out_dtype=F32, grid=(s // MM_ROWS, 1, 1),
                     a_spec=row512(D_MODEL), b_spec=whole(D_MODEL, D_MODEL), o_spec=row512(D_MODEL), contract=NT)
    do_a, do_b, d_ga, d_gb = _gate_bwd(dmixed, proj, o_a, o_b)

    doh = _to_heads("win_do_heads", do_b, 0, H_B)
    dqh, dkh_pad, dvh_pad, dbias, dsink_rows = _win_bwd(qh, kh, vh, bias, sinks1, doh, lse_b)
    wp = _win_param_grads(bucket, dbias, dsink_rows)[:, 0, :]
    d_rel_bias = wp[:, :NUM_BUCKETS].T
    d_sinks = wp[:, NUM_BUCKETS].reshape(1, H_B)
    d_qb = _from_heads("win_dq_rows", dqh, s, 0, BF16, ts=512)
    d_kb = _from_heads("win_dk_rows", dkh_pad, s, WINDOW, BF16)
    d_vb = _from_heads("win_dv_rows", dvh_pad, s, WINDOW, BF16)

    dq, dk, dv = _mla_bwd(q, k, v, do_a, o_a, lse_a)
    dq_pre, dkv_pre, d_kr = _mla_bwd_prep(dq, dk, dv, cos_t, sin_t)
    th = min(s, HEAD_ROWS)
    hgrid = (s // th, 1, H_A)
    hblock = _bs((None, th, HEAD_PAD), lambda i, j, k: (k, i, 0))
    hrows = lambda w: _bs((th, w), lambda i, j, k: (i, 0))
    dqn = _matmul("q_up_dx", dq_pre, wq, out_shape=(s, Q_LORA), out_dtype=F32, grid=hgrid, a_spec=hblock,
                  b_spec=_bs((None, Q_LORA, HEAD_PAD), lambda i, j, k: (k, 0, 0)), o_spec=hrows(Q_LORA), contract=NT)
    dcn = _matmul("kv_up_dx", dkv_pre, wkv, out_shape=(s, KV_LORA), out_dtype=F32, grid=hgrid, a_spec=hblock,
                  b_spec=_bs((None, KV_LORA, HEAD_PAD), lambda i, j, k: (k, 0, 0)), o_spec=hrows(KV_LORA), contract=NT)
    wgrid = (H_A, 1, s // th)
    d_wq = _matmul("q_up_dw", qn, dq_pre, out_shape=(H_A, Q_LORA, HEAD_PAD), out_dtype=F32, grid=wgrid,
                   a_spec=_bs((th, Q_LORA), lambda i, j, k: (k, 0)), b_spec=_bs((None, th, HEAD_PAD), lambda i, j, k: (i, k, 0)),
                   o_spec=_bs((None, Q_LORA, HEAD_PAD), lambda i, j, k: (i, 0, 0)), contract=TN)
    d_wkv = _matmul("kv_up_dw", cn, dkv_pre, out_shape=(H_A, KV_LORA, HEAD_PAD), out_dtype=F32, grid=wgrid,
                    a_spec=_bs((th, KV_LORA), lambda i, j, k: (k, 0)), b_spec=_bs((None, th, HEAD_PAD), lambda i, j, k: (i, k, 0)),
                    o_spec=_bs((None, KV_LORA, HEAD_PAD), lambda i, j, k: (i, 0, 0)), contract=TN)
    d_qlat, d_gq = _rmsnorm_bwd("q_norm_bwd", dqn, proj, rstd_q, q_a_norm_g, Q_LORA, P_QLAT // Q_LORA, BF16)
    d_ckv, d_gkv = _rmsnorm_bwd("kv_norm_bwd", dcn, proj, rstd_c, kv_a_norm_g, KV_LORA, P_CKV // KV_LORA, BF16)

    dproj = jnp.concatenate([d_qb, d_ga, d_gb, d_qlat, d_kb, d_vb, d_ckv, d_kr], axis=1)
    dh1 = _matmul("proj_dx", dproj, w_in_p, out_shape=(s, D_MODEL), out_dtype=F32, grid=(s // 1024, 1, W_IN_PAD // 1024),
                  a_spec=_bs((1024, 1024), lambda i, j, k: (i, k)), b_spec=_bs((D_MODEL, 1024), lambda i, j, k: (0, k)),
                  o_spec=_bs((1024, D_MODEL), lambda i, j, k: (i, 0)), contract=NT)
    d_w_in_p = _matmul("proj_dw", h1, dproj, out_shape=(D_MODEL, W_IN_PAD), out_dtype=F32, grid=(1, W_IN_PAD // 1024, s // tk),
                       a_spec=_bs((tk, D_MODEL), lambda i, j, k: (k, 0)), b_spec=_bs((tk, 1024), lambda i, j, k: (k, j)),
                       o_spec=_bs((D_MODEL, 1024), lambda i, j, k: (0, j)), contract=TN)
    dx, d_norm1_g = _rmsnorm_bwd("norm1_bwd", dh1, x, rstd1, norm1_g, D_MODEL, 0, F32, res=dx1)

    grads = dict(
        norm1_g=d_norm1_g, w_in_p=d_w_in_p, q_a_norm_g=d_gq, wq=d_wq, kv_a_norm_g=d_gkv, wkv=d_wkv,
        rel_bias=d_rel_bias, sinks=d_sinks, w_out=d_w_out, norm2_g=d_norm2_g, w_up=d_w_up,
        conv_w=jnp.concatenate([d_conv_w2[0], d_conv_w2[1]], axis=1),
        conv_b=jnp.concatenate([d_conv_b2[0], d_conv_b2[1]], axis=1),
        w_down=d_w_down, final_norm_g=d_final_g.reshape(D_MODEL))
    return loss, dx, grads


def _wq_heads(w_q_b):
    w = w_q_b.reshape(Q_LORA, H_A, QK_NOPE + QK_ROPE).transpose(1, 0, 2)
    return jnp.pad(w, ((0, 0), (0, 0), (0, HEAD_PAD - QK_NOPE - QK_ROPE)))


def _wq_unheads(d_wq):
    return d_wq[:, :, :QK_NOPE + QK_ROPE].transpose(1, 0, 2).reshape(Q_LORA, H_A * (QK_NOPE + QK_ROPE))


def _wkv_heads(w_kv_b):
    return w_kv_b.reshape(KV_LORA, H_A, QK_NOPE + V_DIM).transpose(1, 0, 2)


def _wkv_unheads(d_wkv):
    return d_wkv.transpose(1, 0, 2).reshape(KV_LORA, H_A * (QK_NOPE + V_DIM))


SMALL = ("norm1_g", "q_a_norm_g", "kv_a_norm_g", "rel_bias", "sinks", "norm2_g", "conv_b", "final_norm_g")
FIRST = ("w_in", "w_q_b", "w_kv_b")
LATER = ("w_out", "w_up", "w_down")
BIG = FIRST + LATER


def kernel(x, positions, norm1_g, w_in, q_a_norm_g, w_q_b, kv_a_norm_g, w_kv_b, rel_bias, sinks, w_out, norm2_g, w_up, conv_w, conv_b, w_down, final_norm_g, loss_target, m_norm1_g, m_w_in, m_q_a_norm_g, m_w_q_b, m_kv_a_norm_g, m_w_kv_b, m_rel_bias, m_sinks, m_w_out, m_norm2_g, m_w_up, m_conv_w, m_conv_b, m_w_down, m_final_norm_g, v_norm1_g, v_w_in, v_q_a_norm_g, v_w_q_b, v_kv_a_norm_g, v_w_kv_b, v_rel_bias, v_sinks, v_w_out, v_norm2_g, v_w_up, v_conv_w, v_conv_b, v_w_down, v_final_norm_g):
    weights = dict(norm1_g=norm1_g, w_in=w_in, q_a_norm_g=q_a_norm_g, w_q_b=w_q_b, kv_a_norm_g=kv_a_norm_g,
                   w_kv_b=w_kv_b, rel_bias=rel_bias, sinks=sinks, w_out=w_out, norm2_g=norm2_g, w_up=w_up,
                   conv_w=conv_w, conv_b=conv_b, w_down=w_down, final_norm_g=final_norm_g)
    mom_m = dict(norm1_g=m_norm1_g, w_in=m_w_in, q_a_norm_g=m_q_a_norm_g, w_q_b=m_w_q_b, kv_a_norm_g=m_kv_a_norm_g,
                 w_kv_b=m_w_kv_b, rel_bias=m_rel_bias, sinks=m_sinks, w_out=m_w_out, norm2_g=m_norm2_g, w_up=m_w_up,
                 conv_w=m_conv_w, conv_b=m_conv_b, w_down=m_w_down, final_norm_g=m_final_norm_g)
    mom_v = dict(norm1_g=v_norm1_g, w_in=v_w_in, q_a_norm_g=v_q_a_norm_g, w_q_b=v_w_q_b, kv_a_norm_g=v_kv_a_norm_g,
                 w_kv_b=v_w_kv_b, rel_bias=v_rel_bias, sinks=v_sinks, w_out=v_w_out, norm2_g=v_norm2_g, w_up=v_w_up,
                 conv_w=v_conv_w, conv_b=v_conv_b, w_down=v_w_down, final_norm_g=v_final_norm_g)
    shard2d = {n: weights[n][0] for n in BIG}
    conv_w_shard = conv_w[0]
    xi, yi, ci = lax.axis_index("x"), lax.axis_index("y"), lax.axis_index("c")
    chip = (2 * xi + yi).astype(jnp.int32)

    core = ci.astype(jnp.int32).reshape(1)
    chip1 = chip.reshape(1)
    cat_cols = lambda a: jnp.concatenate([a[0], a[1], a[2], a[3]], axis=1)
    own_slot = lambda a, own: lax.dynamic_update_index_in_dim(a, own, chip, 0)
    halved = lambda a: a.reshape((2, a.shape[0] // 2) + a.shape[1:])
    quartered = lambda a: a.reshape(4, 2, a.shape[1] // 2, a.shape[2])

    send = [halved(shard2d[n].astype(BF16)) for n in FIRST] + [conv_w_shard]
    gathered = [own_slot(a, own) for a, own in zip(_allgather_weights(send, split=[True] * len(FIRST) + [False]), send)]
    g = {n: a.reshape((4,) + shard2d[n].shape) for n, a in zip(FIRST, gathered)}
    w_in_p = _pad_w_in(cat_cols(g["w_in"]))
    wq = _wq_heads(cat_cols(g["w_q_b"]))
    wkv = _wkv_heads(cat_cols(g["w_kv_b"]))
    conv_w_f = cat_cols(gathered[-1])

    later = [shard2d[n].astype(BF16) for n in LATER]
    gather2 = _SplitExchange("gather_later", later, [(4,) + a.shape for a in later],
                             src_of=lambda ref, k, p, pk: ref, dst_of=lambda ref, k, p, pk: ref.at[p],
                             arrive_of=lambda ref, k, p, pk: ref.at[pk])
    norm1_g_in = norm1_g + gather2.start()[:1, :1]

    def late_weights(after):
        w_out_g, w_up_g, w_down_g = [own_slot(a, own) for a, own in zip(gather2.wait(after), later)]
        return w_out_g.reshape(D_MODEL, D_MODEL), cat_cols(w_up_g), w_down_g.reshape(D_FF, D_MODEL)

    early = {}

    def early_grads(d_w_out, d_w_up, d_w_down):
        grads = [quartered(d_w_out.reshape(4, D_MODEL // 4, D_MODEL)), quartered(_col_shards(d_w_up)),
                 quartered(d_w_down.reshape(4, D_FF // 4, D_MODEL))]
        recv = _rs_pair_exchange("rs_pair_exchange_early", grads)
        early["pairs"] = [_rs_pair_add(f"rs_pair_add_{n}", core, gfull, r) for n, gfull, r in zip(LATER, grads, recv)]
        early["ici"] = _SplitExchange("rs_ici_early", early["pairs"], [(3,) + a.shape[1:] for a in early["pairs"]],
                                      src_of=lambda ref, k, p, pk: ref.at[pk], dst_of=lambda ref, k, p, pk: ref.at[k],
                                      arrive_of=lambda ref, k, p, pk: ref.at[k])
        return early["ici"].start()

    loss, dx, gr = _local_step(x[0], positions, loss_target[0], norm1_g_in, w_in_p, q_a_norm_g, wq, kv_a_norm_g, wkv,
                               rel_bias, sinks, late_weights, norm2_g, conv_w_f, conv_b, final_norm_g, early_grads)

    grads = [quartered(_col_shards(_unpad_w_in(gr["w_in_p"]))), quartered(_col_shards(_wq_unheads(gr["wq"]))),
             quartered(_col_shards(_wkv_unheads(gr["wkv"])))]
    recv1 = _rs_pair_exchange("rs_pair_exchange", grads)
    pairs = [_rs_pair_add(f"rs_pair_add_{n}", core, gfull, r) for n, gfull, r in zip(FIRST, grads, recv1)]
    recv2 = list(_rs_ici(pairs)) + early["ici"].wait(dx)
    pairs = pairs + early["pairs"]
    halves = [_rs_final_add(f"rs_final_add_{n}", chip1, pr, r) for n, pr, r in zip(FIRST + LATER, pairs, recv2)]
    both = lambda mine, theirs: jnp.where(ci == 0, jnp.concatenate([mine, theirs]), jnp.concatenate([theirs, mine]))
    reduced = {n: both(h, t) for n, h, t in zip(FIRST + LATER, halves, _rs_pair_share(halves))}

    small_parts = [gr[n] for n in SMALL] + [gr["conv_w"], loss]
    rows = _pack_rows(small_parts)
    summed = _unpack(_small_allreduce(_pack(small_parts, rows)), [p.shape for p in small_parts])
    small_g = dict(zip(SMALL, summed[:len(SMALL)]))
    conv_w_g = lax.dynamic_slice_in_dim(summed[len(SMALL)], chip * (2 * D_FF // 4), 2 * D_FF // 4, axis=1)
    loss_out = summed[-1].reshape(())

    out_g, out_d, out_m, out_v = {}, {}, {}, {}
    for n in BIG:
        gsh = reduced[n]
        d, nm, nv = _adamw(f"adamw_{n}", shard2d[n], gsh, mom_m[n][0], mom_v[n][0])
        out_g[n], out_d[n], out_m[n], out_v[n] = gsh[None], d[None], nm[None], nv[None]
    names = SMALL + ("conv_w",)
    shapes = [weights[n].shape for n in names]
    sg = [small_g[n].reshape(weights[n].shape) for n in SMALL] + [conv_w_g[None]]
    prow = _pack_rows([weights[n] for n in names])
    d, nm, nv = _adamw("adamw_small", _pack([weights[n] for n in names], prow), _pack(sg, prow),
                       _pack([mom_m[n] for n in names], prow), _pack([mom_v[n] for n in names], prow))
    for n, gg, dd, mm, vv in zip(names, sg, _unpack(d, shapes), _unpack(nm, shapes), _unpack(nv, shapes)):
        out_g[n], out_d[n], out_m[n], out_v[n] = gg, dd, mm, vv

    order = ("norm1_g", "w_in", "q_a_norm_g", "w_q_b", "kv_a_norm_g", "w_kv_b", "rel_bias", "sinks", "w_out",
             "norm2_g", "w_up", "conv_w", "conv_b", "w_down", "final_norm_g")
    return (loss_out, dx[None], *[out_g[n] for n in order], *[out_d[n] for n in order],
            *[out_m[n] for n in order], *[out_v[n] for n in order])
```

```python
import functools
import math

import jax
import jax.numpy as jnp
import numpy as np
from jax import lax
from jax.experimental import pallas as pl
from jax.experimental.pallas import tpu as pltpu

F32 = jnp.float32
BF16 = jnp.bfloat16
MESH = pl.DeviceIdType.MESH

D_MODEL = 1024
EPS = 1e-6
H_A = 8
QK_NOPE = 128
QK_ROPE = 64
V_DIM = 128
Q_LORA = 256
KV_LORA = 128
ROPE_THETA = 10000.0
H_B = 16
KV_B = 4
GROUP = 4
HD_B = 64
WINDOW = 128
Q_BLOCK = 128
NUM_BUCKETS = 32
MAX_DISTANCE = 128
D_FF = 2816
HEAD_PAD = 256

ADAM_LR = 0.001
ADAM_B1 = 0.9
ADAM_B2 = 0.999
ADAM_EPS = 1e-08
ADAM_WD = 0.01
ADAM_STEP = 10

LANES = 128
P_QB, P_GA, P_GB, P_QLAT, P_KB, P_VB, P_CKV, P_KR = 0, 1024, 2048, 3072, 3328, 3584, 3840, 3968
W_IN_PAD = 4096

NT = (((1,), (1,)), ((), ()))
NN = (((1,), (0,)), ((), ()))
TN = (((0,), (0,)), ((), ()))


def _arb(n):
    return pltpu.CompilerParams(dimension_semantics=("arbitrary",) * n)


def _matmul(name, a, b, *, out_shape, out_dtype, grid, a_spec, b_spec, o_spec, contract, add=None, bf16_copy=False):
    nk = grid[2]
    acc_shape = tuple(d for d in o_spec.block_shape if d is not None)
    n_in = 3 if add is not None else 2
    n_out = 2 if bf16_copy else 1

    def body(*refs):
        a_ref, b_ref = refs[:2]
        add_ref = refs[2] if add is not None else None
        o_refs = refs[n_in:n_in + n_out]
        scratch = refs[n_in + n_out:]
        prod = lax.dot_general(a_ref[...].astype(BF16), b_ref[...].astype(BF16), contract,
                               preferred_element_type=F32)

        def finish(val):
            if add_ref is not None:
                val = add_ref[...] + val
            o_refs[0][...] = val.astype(out_dtype)
            if bf16_copy:
                o_refs[1][...] = val.astype(BF16)

        if nk == 1:
            finish(prod)
        else:
            acc_ref = scratch[0]
            k = pl.program_id(2)

            @pl.when(k == 0)
            def _():
                acc_ref[...] = prod

            @pl.when((k > 0) & (k < nk - 1))
            def _():
                acc_ref[...] += prod

            @pl.when(k == nk - 1)
            def _():
                finish(acc_ref[...] + prod)

    in_specs = [a_spec, b_spec]
    args = [a, b]
    if add is not None:
        in_specs.append(o_spec)
        args.append(add)
    out_shapes = [jax.ShapeDtypeStruct(out_shape, out_dtype)]
    if bf16_copy:
        out_shapes.append(jax.ShapeDtypeStruct(out_shape, BF16))
    res = pl.pallas_call(
        body, name=name, grid=grid, in_specs=in_specs, out_specs=[o_spec] * n_out, out_shape=out_shapes,
        scratch_shapes=[pltpu.VMEM(acc_shape, F32)] if nk > 1 else [],
        compiler_params=_arb(3),
    )(*args)
    return res if bf16_copy else res[0]


def _bs(block, fn):
    return pl.BlockSpec(block, fn)


def _rmsnorm_fwd(name, src, g, d, cb, ts=512):
    s = src.shape[0]

    def body(x_ref, g_ref, h_ref, r_ref):
        x = x_ref[...]
        r = lax.rsqrt(jnp.mean(x * x, axis=-1, keepdims=True) + EPS)
        h_ref[...] = (x * r * g_ref[...]).astype(BF16)
        r_ref[...] = r

    return pl.pallas_call(
        body, name=name, grid=(s // ts,),
        in_specs=[_bs((ts, d), lambda i: (i, cb)), _bs((1, d), lambda i: (0, 0))],
        out_specs=[_bs((ts, d), lambda i: (i, 0)), _bs((ts, 1), lambda i: (i, 0))],
        out_shape=[jax.ShapeDtypeStruct((s, d), BF16), jax.ShapeDtypeStruct((s, 1), F32)],
        compiler_params=_arb(1),
    )(src, g)


def _rmsnorm_bwd(name, dy, src, rstd, g, d, cb, out_dtype, res=None, bf16_copy=False, ts=512):
    s = src.shape[0]

    def body(*refs):
        dy_ref, x_ref, r_ref, g_ref = refs[:4]
        res_ref = refs[4] if res is not None else None
        dx_ref, dg_ref = refs[n_in:n_in + 2]
        dyv = dy_ref[...]
        r = r_ref[...]
        xhat = x_ref[...] * r
        dyh = dyv * g_ref[...]
        c = jnp.mean(dyh * xhat, axis=-1, keepdims=True)
        dx = r * (dyh - xhat * c)
        if res_ref is not None:
            dx = res_ref[...] + dx
        dx_ref[...] = dx.astype(out_dtype)
        if bf16_copy:
            refs[n_in + 2][...] = dx.astype(BF16)
        part = jnp.sum(dyv * xhat, axis=0, keepdims=True)

        @pl.when(pl.program_id(0) == 0)
        def _():
            dg_ref[...] = part

        @pl.when(pl.program_id(0) > 0)
        def _():
            dg_ref[...] += part

    in_specs = [_bs((ts, d), lambda i: (i, 0)), _bs((ts, d), lambda i: (i, cb)),
                _bs((ts, 1), lambda i: (i, 0)), _bs((1, d), lambda i: (0, 0))]
    args = [dy, src, rstd, g]
    if res is not None:
        in_specs.append(_bs((ts, d), lambda i: (i, 0)))
        args.append(res)
    n_in = len(args)
    out_specs = [_bs((ts, d), lambda i: (i, 0)), _bs((1, d), lambda i: (0, 0))]
    out_shape = [jax.ShapeDtypeStruct((s, d), out_dtype), jax.ShapeDtypeStruct((1, d), F32)]
    if bf16_copy:
        out_specs.append(_bs((ts, d), lambda i: (i, 0)))
        out_shape.append(jax.ShapeDtypeStruct((s, d), BF16))
    return pl.pallas_call(
        body, name=name, grid=(s // ts,), in_specs=in_specs, out_specs=out_specs, out_shape=out_shape,
        compiler_params=_arb(1),
    )(*args)


def _final_loss(x2, target, g, ts=512):
    s, d = x2.shape

    def body(x_ref, t_ref, g_ref, loss_ref, dx_ref, dg_ref, dxb_ref):
        x = x_ref[...]
        r = lax.rsqrt(jnp.mean(x * x, axis=-1, keepdims=True) + EPS)
        xhat = x * r
        gv = g_ref[...]
        err = xhat * gv - t_ref[...]
        lpart = 0.5 * jnp.sum(jnp.mean(err * err, axis=-1, keepdims=True), axis=0, keepdims=True)
        dyv = err * (1.0 / d)
        dyh = dyv * gv
        c = jnp.mean(dyh * xhat, axis=-1, keepdims=True)
        dx = r * (dyh - xhat * c)
        dx_ref[...] = dx
        dxb_ref[...] = dx.astype(BF16)
        gpart = jnp.sum(dyv * xhat, axis=0, keepdims=True)

        @pl.when(pl.program_id(0) == 0)
        def _():
            dg_ref[...] = gpart
            loss_ref[...] = lpart

        @pl.when(pl.program_id(0) > 0)
        def _():
            dg_ref[...] += gpart
            loss_ref[...] += lpart

    return pl.pallas_call(
        body, name="final_loss", grid=(s // ts,),
        in_specs=[_bs((ts, d), lambda i: (i, 0)), _bs((ts, d), lambda i: (i, 0)), _bs((1, d), lambda i: (0, 0))],
        out_specs=[_bs((1, 1), lambda i: (0, 0)), _bs((ts, d), lambda i: (i, 0)), _bs((1, d), lambda i: (0, 0)),
                   _bs((ts, d), lambda i: (i, 0))],
        out_shape=[jax.ShapeDtypeStruct((1, 1), F32), jax.ShapeDtypeStruct((s, d), F32),
                   jax.ShapeDtypeStruct((1, d), F32), jax.ShapeDtypeStruct((s, d), BF16)],
        compiler_params=_arb(1),
    )(x2, target, g)


def _swap_halves(t):
    lane = lax.broadcasted_iota(jnp.int32, t.shape, 1)
    return jnp.where(lane < 32, pltpu.roll(t, 96, 1), pltpu.roll(t, 32, 1))


def _rope_fwd(t, cos_t, sin_t):
    return t * cos_t + _swap_halves(t) * sin_t


def _rope_bwd(dt, cos_t, sin_t):
    return dt * cos_t - _swap_halves(dt) * sin_t


def _lat_norms(proj, gq, gkv, ts=512):
    s = proj.shape[0]

    def body(q_ref, c_ref, gq_ref, gkv_ref, qn_ref, cn_ref, rq_ref, rc_ref):
        q = q_ref[...]
        rq = lax.rsqrt(jnp.mean(q * q, axis=-1, keepdims=True) + EPS)
        qn_ref[...] = (q * rq * gq_ref[...]).astype(BF16)
        rq_ref[...] = rq
        cv = c_ref[...]
        rc = lax.rsqrt(jnp.mean(cv * cv, axis=-1, keepdims=True) + EPS)
        cn_ref[...] = (cv * rc * gkv_ref[...]).astype(BF16)
        rc_ref[...] = rc

    return pl.pallas_call(
        body, name="lat_norms", grid=(s // ts,),
        in_specs=[_bs((ts, Q_LORA), lambda i: (i, P_QLAT // Q_LORA)),
                  _bs((ts, KV_LORA), lambda i: (i, P_CKV // KV_LORA)),
                  _bs((1, Q_LORA), lambda i: (0, 0)), _bs((1, KV_LORA), lambda i: (0, 0))],
        out_specs=[_bs((ts, Q_LORA), lambda i: (i, 0)), _bs((ts, KV_LORA), lambda i: (i, 0)),
                   _bs((ts, 1), lambda i: (i, 0)), _bs((ts, 1), lambda i: (i, 0))],
        out_shape=[jax.ShapeDtypeStruct((s, Q_LORA), BF16), jax.ShapeDtypeStruct((s, KV_LORA), BF16),
                   jax.ShapeDtypeStruct((s, 1), F32), jax.ShapeDtypeStruct((s, 1), F32)],
        compiler_params=_arb(1),
    )(proj, proj, gq, gkv)


HEAD_ROWS = 2048
DW_ROWS = 2048
MM_ROWS = 1024


def _q_heads(qn, wq, cos_t, sin_t):
    s = qn.shape[0]
    ts = min(s, HEAD_ROWS)

    def body(qn_ref, w_ref, cos_ref, sin_ref, q_ref):
        o = jnp.dot(qn_ref[...], w_ref[...], preferred_element_type=F32)
        q_ref[:, :LANES] = o[:, :LANES].astype(BF16)
        q_ref[:, LANES:] = _rope_fwd(o[:, LANES:], cos_ref[...], sin_ref[...]).astype(BF16)

    return pl.pallas_call(
        body, name="q_heads", grid=(H_A, s // ts),
        in_specs=[_bs((ts, Q_LORA), lambda h, i: (i, 0)), _bs((None, Q_LORA, HEAD_PAD), lambda h, i: (h, 0, 0)),
                  _bs((ts, LANES), lambda h, i: (i, 0)), _bs((ts, LANES), lambda h, i: (i, 0))],
        out_specs=_bs((None, ts, HEAD_PAD), lambda h, i: (h, i, 0)),
        out_shape=jax.ShapeDtypeStruct((H_A, s, HEAD_PAD), BF16),
        compiler_params=_arb(2),
    )(qn, wq, cos_t, sin_t)


def _kv_heads(cn, wkv, proj, cos_t, sin_t):
    s = cn.shape[0]
    ts = min(s, HEAD_ROWS)

    def body(cn_ref, w_ref, kr_ref, cos_ref, sin_ref, k_ref, v_ref):
        o = jnp.dot(cn_ref[...], w_ref[...], preferred_element_type=F32)
        k_ref[:, :LANES] = o[:, :LANES].astype(BF16)
        k_ref[:, LANES:] = _rope_fwd(kr_ref[...], cos_ref[...], sin_ref[...]).astype(BF16)
        v_ref[...] = o[:, LANES:].astype(BF16)

    return pl.pallas_call(
        body, name="kv_heads", grid=(H_A, s // ts),
        in_specs=[_bs((ts, KV_LORA), lambda h, i: (i, 0)),
                  _bs((None, KV_LORA, QK_NOPE + V_DIM), lambda h, i: (h, 0, 0)),
                  _bs((ts, LANES), lambda h, i: (i, P_KR // LANES)),
                  _bs((ts, LANES), lambda h, i: (i, 0)), _bs((ts, LANES), lambda h, i: (i, 0))],
        out_specs=[_bs((None, ts, HEAD_PAD), lambda h, i: (h, i, 0)), _bs((None, ts, V_DIM), lambda h, i: (h, i, 0))],
        out_shape=[jax.ShapeDtypeStruct((H_A, s, HEAD_PAD), BF16), jax.ShapeDtypeStruct((H_A, s, V_DIM), BF16)],
        compiler_params=_arb(2),
    )(cn, wkv, proj, cos_t, sin_t)


MLA_SCALE = 1.0 / math.sqrt(QK_NOPE + QK_ROPE)
LOG2E = math.log2(math.e)
MLA_EXP2_SCALE = MLA_SCALE * LOG2E


def _lane_tiles(a):
    return [a[:, j * LANES:(j + 1) * LANES] for j in range(a.shape[1] // LANES)]


def _mla_fwd(q, k, v, tq=512, tk=512):
    s = q.shape[1]
    nk = s // tk

    def body(q_ref, k_ref, v_ref, o_ref, lse_ref, m_ref, l_ref, acc_ref):
        m_ref[...] = jnp.full(m_ref.shape, -jnp.inf, F32)
        l_ref[...] = jnp.zeros(l_ref.shape, F32)
        acc_ref[...] = jnp.zeros(acc_ref.shape, F32)
        qv = q_ref[...]

        def step(c, carry):
            rows = pl.ds(pl.multiple_of(c * tk, tk), tk)
            raw = lax.dot_general(qv, k_ref[rows, :], NT, preferred_element_type=F32)
            m_prev = m_ref[...]
            m_new = jnp.maximum(m_prev, jnp.max(raw, axis=-1, keepdims=True))
            alpha = jnp.exp2((m_prev - m_new) * MLA_EXP2_SCALE)
            ps = [jnp.exp2((t - m_new) * MLA_EXP2_SCALE) for t in _lane_tiles(raw)]
            l_ref[...] = alpha * l_ref[...] + functools.reduce(lambda a, b: a + b, ps)
            p = jnp.concatenate(ps, axis=1).astype(BF16)
            acc_ref[...] = alpha * acc_ref[...] + jnp.dot(p, v_ref[rows, :], preferred_element_type=F32)
            m_ref[...] = m_new
            return carry

        lax.fori_loop(0, nk, step, 0, unroll=True)
        l = jnp.sum(l_ref[...], axis=-1, keepdims=True)
        o_ref[...] = acc_ref[...] / l
        lse_ref[...] = m_ref[...] * MLA_SCALE + jnp.log(l)

    return pl.pallas_call(
        body, name="mla_fwd", grid=(H_A, s // tq),
        in_specs=[_bs((None, tq, HEAD_PAD), lambda h, i: (h, i, 0)),
                  _bs((None, s, HEAD_PAD), lambda h, i: (h, 0, 0)),
                  _bs((None, s, V_DIM), lambda h, i: (h, 0, 0))],
        out_specs=[_bs((tq, V_DIM), lambda h, i: (i, h)), _bs((None, tq, LANES), lambda h, i: (h, i, 0))],
        out_shape=[jax.ShapeDtypeStruct((s, H_A * V_DIM), F32), jax.ShapeDtypeStruct((H_A, s, LANES), F32)],
        scratch_shapes=[pltpu.VMEM((tq, LANES), F32), pltpu.VMEM((tq, LANES), F32), pltpu.VMEM((tq, V_DIM), F32)],
        compiler_params=_arb(2),
    )(q, k, v)


def _mla_bwd(q, k, v, do, o, lse, tq=512, tk=512):
    s = q.shape[1]
    nq = s // tq

    def body(q_ref, k_ref, v_ref, do_ref, o_ref, lse_ref, dq_ref, dk_ref, dv_ref, delta_ref):
        @pl.when(pl.program_id(1) == 0)
        def _():
            def init(c, carry):
                rows = pl.ds(pl.multiple_of(c * tq, tq), tq)
                delta = jnp.sum(do_ref[rows, :] * o_ref[rows, :], axis=-1, keepdims=True)
                delta_ref[rows, :] = jnp.broadcast_to(delta, (tq, LANES))
                dq_ref[rows, :] = jnp.zeros((tq, HEAD_PAD), F32)
                return carry

            lax.fori_loop(0, nq, init, 0)

        dk_ref[...] = jnp.zeros(dk_ref.shape, F32)
        dv_ref[...] = jnp.zeros(dv_ref.shape, F32)
        kb = k_ref[...]
        vb = v_ref[...]

        def step(c, carry):
            rows = pl.ds(pl.multiple_of(c * tq, tq), tq)
            qc = q_ref[rows, :]
            doc = do_ref[rows, :].astype(BF16)
            raw = lax.dot_general(qc, kb, NT, preferred_element_type=F32)
            dp = lax.dot_general(doc, vb, NT, preferred_element_type=F32)
            lse2 = lse_ref[rows, :] * LOG2E
            delta = delta_ref[rows, :]
            ps = [jnp.exp2(t * MLA_EXP2_SCALE - lse2) for t in _lane_tiles(raw)]
            dss = [pj * (dj - delta) * MLA_SCALE for pj, dj in zip(ps, _lane_tiles(dp))]
            p = jnp.concatenate(ps, axis=1).astype(BF16)
            ds = jnp.concatenate(dss, axis=1).astype(BF16)
            dv_ref[...] += lax.dot_general(p, doc, TN, preferred_element_type=F32)
            dk_ref[...] += lax.dot_general(ds, qc, TN, preferred_element_type=F32)
            dq_ref[rows, :] += jnp.dot(ds, kb, preferred_element_type=F32)
            return carry

        lax.fori_loop(0, nq, step, 0, unroll=True)

    return pl.pallas_call(
        body, name="mla_bwd", grid=(H_A, s // tk),
        in_specs=[_bs((None, s, HEAD_PAD), lambda h, j: (h, 0, 0)),
                  _bs((None, tk, HEAD_PAD), lambda h, j: (h, j, 0)),
                  _bs((None, tk, V_DIM), lambda h, j: (h, j, 0)),
                  _bs((s, V_DIM), lambda h, j: (0, h)), _bs((s, V_DIM), lambda h, j: (0, h)),
                  _bs((None, s, LANES), lambda h, j: (h, 0, 0))],
        out_specs=[_bs((None, s, HEAD_PAD), lambda h, j: (h, 0, 0)),
                   _bs((None, tk, HEAD_PAD), lambda h, j: (h, j, 0)),
                   _bs((None, tk, V_DIM), lambda h, j: (h, j, 0))],
        out_shape=[jax.ShapeDtypeStruct((H_A, s, HEAD_PAD), F32), jax.ShapeDtypeStruct((H_A, s, HEAD_PAD), F32),
                   jax.ShapeDtypeStruct((H_A, s, V_DIM), F32)],
        scratch_shapes=[pltpu.VMEM((s, LANES), F32)],
        compiler_params=_arb(2),
    )(q, k, v, do, o, lse)


def _mla_bwd_prep(dq, dk, dv, cos_t, sin_t, ts=256):
    s = dq.shape[1]

    def body(dq_ref, dk_ref, dv_ref, cos_ref, sin_ref, dqp_ref, dkvp_ref, dkr_ref):
        cos_v = cos_ref[...]
        sin_v = sin_ref[...]
        kr = jnp.zeros((ts, LANES), F32)
        for h in range(H_A):
            dqp_ref[h, :, :LANES] = dq_ref[h, :, :LANES].astype(BF16)
            dqp_ref[h, :, LANES:] = _rope_bwd(dq_ref[h, :, LANES:], cos_v, sin_v).astype(BF16)
            dkvp_ref[h, :, :LANES] = dk_ref[h, :, :LANES].astype(BF16)
            dkvp_ref[h, :, LANES:] = dv_ref[h].astype(BF16)
            kr = kr + dk_ref[h, :, LANES:]
        dkr_ref[...] = _rope_bwd(kr, cos_v, sin_v).astype(BF16)

    blk3 = lambda w: _bs((H_A, ts, w), lambda i: (0, i, 0))
    return pl.pallas_call(
        body, name="mla_bwd_prep", grid=(s // ts,),
        in_specs=[blk3(HEAD_PAD), blk3(HEAD_PAD), blk3(V_DIM),
                  _bs((ts, LANES), lambda i: (i, 0)), _bs((ts, LANES), lambda i: (i, 0))],
        out_specs=[blk3(HEAD_PAD), blk3(HEAD_PAD), _bs((ts, LANES), lambda i: (i, 0))],
        out_shape=[jax.ShapeDtypeStruct((H_A, s, HEAD_PAD), BF16), jax.ShapeDtypeStruct((H_A, s, HEAD_PAD), BF16),
                   jax.ShapeDtypeStruct((s, LANES), BF16)],
        compiler_params=_arb(1),
    )(dq, dk, dv, cos_t, sin_t)


WIN_SCALE = 1.0 / math.sqrt(HD_B)
SPAN = Q_BLOCK + 2 * WINDOW


def _t5_bucket_table():
    a = jnp.arange(Q_BLOCK, dtype=jnp.int32)[:, None]
    c = jnp.arange(SPAN, dtype=jnp.int32)[None, :]
    rel = c - WINDOW - a
    nb = NUM_BUCKETS // 2
    max_exact = nb // 2
    base = (rel > 0).astype(jnp.int32) * nb
    n = jnp.abs(rel)
    nf = jnp.maximum(n, 1).astype(F32)
    large = max_exact + (jnp.log(nf / max_exact) / math.log(MAX_DISTANCE / max_exact)
                         * (nb - max_exact)).astype(jnp.int32)
    large = jnp.minimum(large, nb - 1)
    return base + jnp.where(n < max_exact, n, large)


def _win_bias(bucket, rel_bias):
    def body(rb_ref, bk_ref, o_ref):
        h = pl.program_id(0)
        bk = bk_ref[...]
        acc = jnp.zeros((Q_BLOCK, SPAN), F32)
        for b in range(NUM_BUCKETS):
            acc = jnp.where(bk == b, rb_ref[b, h], acc)
        o_ref[...] = acc

    return pl.pallas_call(
        body, name="win_bias", grid=(H_B,),
        in_specs=[pl.BlockSpec(memory_space=pltpu.SMEM), _bs((Q_BLOCK, SPAN), lambda h: (0, 0))],
        out_specs=_bs((None, Q_BLOCK, SPAN), lambda h: (h, 0, 0)),
        out_shape=jax.ShapeDtypeStruct((H_B, Q_BLOCK, SPAN), F32),
        compiler_params=_arb(1),
    )(rel_bias, bucket)


GROUP_W = GROUP * HD_B


def _win_kv_rows(n, j, nblk):
    blk = jnp.clip(n + j - 1, 0, nblk - 1)
    return pl.ds(pl.multiple_of(blk * Q_BLOCK, Q_BLOCK), Q_BLOCK)


def _win_head_cols(kv):
    return slice(kv * HD_B, (kv + 1) * HD_B)


def _win_stack(ref, kv):
    return jnp.concatenate([ref[:, kv * GROUP_W + g * HD_B:kv * GROUP_W + (g + 1) * HD_B] for g in range(GROUP)], axis=0)


def _win_unstack(ref, kv, val):
    for g in range(GROUP):
        ref[:, kv * GROUP_W + g * HD_B:kv * GROUP_W + (g + 1) * HD_B] = val[g * Q_BLOCK:(g + 1) * Q_BLOCK].astype(ref.dtype)


def _win_scores(q, k_ref, kv, bias_ref, n, nblk):
    a = lax.broadcasted_iota(jnp.int32, (GROUP, Q_BLOCK, Q_BLOCK), 1)
    cc = lax.broadcasted_iota(jnp.int32, (GROUP, Q_BLOCK, Q_BLOCK), 2)
    valid = [(cc >= a) & (n > 0), None, (cc <= a) & (n < nblk - 1)]
    out = []
    for j in range(3):
        sc = lax.dot_general(q, k_ref[_win_kv_rows(n, j, nblk), _win_head_cols(kv)], NT, preferred_element_type=F32)
        sc = (sc.reshape(GROUP, Q_BLOCK, Q_BLOCK) * WIN_SCALE
              + bias_ref[kv * GROUP:(kv + 1) * GROUP, :, j * Q_BLOCK:(j + 1) * Q_BLOCK])
        if valid[j] is not None:
            sc = jnp.where(valid[j], sc, -1e30)
        out.append(sc)
    return out


def _win_sink(sink_ref, kv):
    hs = lax.broadcasted_iota(jnp.int32, (GROUP, Q_BLOCK, 1), 0)
    sk = jnp.zeros((GROUP, Q_BLOCK, 1), F32)
    for g in range(GROUP):
        sk = jnp.where(hs == g, sink_ref[kv * GROUP + g], sk)
    return sk


def _win_fwd(proj_b, bias, sinks):
    s = proj_b.shape[0]
    nblk = s // Q_BLOCK
    rows = GROUP * Q_BLOCK

    def body(sink_ref, q_ref, k_ref, v_ref, bias_ref, o_ref, lse_ref):
        n = pl.program_id(0)
        for kv in range(KV_B):
            sk = _win_sink(sink_ref, kv)
            q = _win_stack(q_ref, kv)
            ss = _win_scores(q, k_ref, kv, bias_ref, n, nblk)
            m = jnp.maximum(jnp.max(jnp.maximum(jnp.maximum(ss[0], ss[1]), ss[2]), axis=2, keepdims=True), sk)
            es = [jnp.exp(sc - m) for sc in ss]
            l = jnp.sum(es[0] + es[1] + es[2], axis=2, keepdims=True) + jnp.exp(sk - m)
            acc = jnp.zeros((rows, HD_B), F32)
            for j, e in enumerate(es):
                p = (e / l).astype(BF16).reshape(rows, Q_BLOCK)
                acc = acc + jnp.dot(p, v_ref[_win_kv_rows(n, j, nblk), _win_head_cols(kv)],
                                    preferred_element_type=F32)
            _win_unstack(o_ref, kv, acc)
            lse_ref[kv * GROUP:(kv + 1) * GROUP] = m + jnp.log(l)

    kv_w = KV_B * HD_B
    return pl.pallas_call(
        body, name="win_fwd", grid=(nblk,),
        in_specs=[pl.BlockSpec(memory_space=pltpu.SMEM), _bs((Q_BLOCK, H_B * HD_B), lambda n: (n, P_QB // (H_B * HD_B))),
                  _bs((s, kv_w), lambda n: (0, P_KB // kv_w)), _bs((s, kv_w), lambda n: (0, P_VB // kv_w)),
                  _bs((H_B, Q_BLOCK, SPAN), lambda n: (0, 0, 0))],
        out_specs=[_bs((Q_BLOCK, H_B * HD_B), lambda n: (n, 0)), _bs((H_B, Q_BLOCK, 1), lambda n: (0, n, 0))],
        out_shape=[jax.ShapeDtypeStruct((s, H_B * HD_B), F32), jax.ShapeDtypeStruct((H_B, s, 1), F32)],
        compiler_params=_arb(1),
    )(sinks, proj_b, proj_b, proj_b, bias)


def _win_bwd(proj_b, bias, sinks, do_b, lse):
    s = proj_b.shape[0]
    nblk = s // Q_BLOCK
    rows = GROUP * Q_BLOCK
    spad = s + 2 * WINDOW

    def body(sink_ref, q_ref, k_ref, v_ref, bias_ref, do_ref, lse_ref, dq_ref, dk_ref, dv_ref, db_ref, dsk_ref):
        n = pl.program_id(0)

        @pl.when(n == 0)
        def _():
            dk_ref[...] = jnp.zeros(dk_ref.shape, F32)
            dv_ref[...] = jnp.zeros(dv_ref.shape, F32)
            db_ref[...] = jnp.zeros(db_ref.shape, F32)
            dsk_ref[...] = jnp.zeros(dsk_ref.shape, F32)

        for kv in range(KV_B):
            heads = slice(kv * GROUP, (kv + 1) * GROUP)
            sk = _win_sink(sink_ref, kv)
            q = _win_stack(q_ref, kv)
            dob = _win_stack(do_ref, kv)
            lse_v = lse_ref[heads]
            ss = _win_scores(q, k_ref, kv, bias_ref, n, nblk)
            ps = [jnp.exp(sc - lse_v) for sc in ss]
            dps = [lax.dot_general(dob, v_ref[_win_kv_rows(n, j, nblk), _win_head_cols(kv)], NT,
                                   preferred_element_type=F32).reshape(GROUP, Q_BLOCK, Q_BLOCK) for j in range(3)]
            delta = jnp.sum(ps[0] * dps[0] + ps[1] * dps[1] + ps[2] * dps[2], axis=2, keepdims=True)
            dq = jnp.zeros((rows, HD_B), F32)
            for j in range(3):
                ds = ps[j] * (dps[j] - delta)
                db_ref[heads, :, j * Q_BLOCK:(j + 1) * Q_BLOCK] += ds
                dsb = (ds * WIN_SCALE).astype(BF16).reshape(rows, Q_BLOCK)
                dq = dq + jnp.dot(dsb, k_ref[_win_kv_rows(n, j, nblk), _win_head_cols(kv)],
                                  preferred_element_type=F32)
                krows = pl.ds(pl.multiple_of((n + j) * Q_BLOCK, Q_BLOCK), Q_BLOCK)
                dk_ref[krows, _win_head_cols(kv)] += lax.dot_general(dsb, q, TN, preferred_element_type=F32)
                dv_ref[krows, _win_head_cols(kv)] += lax.dot_general(
                    ps[j].astype(BF16).reshape(rows, Q_BLOCK), dob, TN, preferred_element_type=F32)
            dsk_ref[heads] += -(jnp.exp(sk - lse_v) * delta)
            _win_unstack(dq_ref, kv, dq)

    kv_w = KV_B * HD_B
    qspec = _bs((Q_BLOCK, H_B * HD_B), lambda n: (n, 0))
    kacc = _bs((spad, kv_w), lambda n: (0, 0))
    return pl.pallas_call(
        body, name="win_bwd", grid=(nblk,),
        in_specs=[pl.BlockSpec(memory_space=pltpu.SMEM), _bs((Q_BLOCK, H_B * HD_B), lambda n: (n, P_QB // (H_B * HD_B))),
                  _bs((s, kv_w), lambda n: (0, P_KB // kv_w)), _bs((s, kv_w), lambda n: (0, P_VB // kv_w)),
                  _bs((H_B, Q_BLOCK, SPAN), lambda n: (0, 0, 0)), qspec, _bs((H_B, Q_BLOCK, 1), lambda n: (0, n, 0))],
        out_specs=[qspec, kacc, kacc, _bs((H_B, Q_BLOCK, SPAN), lambda n: (0, 0, 0)),
                   _bs((H_B, Q_BLOCK, 1), lambda n: (0, 0, 0))],
        out_shape=[jax.ShapeDtypeStruct((s, H_B * HD_B), BF16), jax.ShapeDtypeStruct((spad, kv_w), F32),
                   jax.ShapeDtypeStruct((spad, kv_w), F32), jax.ShapeDtypeStruct((H_B, Q_BLOCK, SPAN), F32),
                   jax.ShapeDtypeStruct((H_B, Q_BLOCK, 1), F32)],
        compiler_params=_arb(1),
    )(sinks, proj_b, proj_b, proj_b, bias, do_b, lse)


def _win_param_grads(bucket, dbias, dsink_rows):
    def body(bk_ref, db_ref, ds_ref, o_ref):
        bk = bk_ref[...]
        dbv = db_ref[...]
        lane = lax.broadcasted_iota(jnp.int32, (1, LANES), 1)
        res = jnp.zeros((1, LANES), F32)
        for b in range(NUM_BUCKETS):
            tot = jnp.sum(jnp.sum(jnp.where(bk == b, dbv, 0.0), axis=1, keepdims=True), axis=0, keepdims=True)
            res = jnp.where(lane == b, tot, res)
        stot = jnp.sum(ds_ref[...], axis=0, keepdims=True)
        o_ref[...] = jnp.where(lane == NUM_BUCKETS, stot, res)

    return pl.pallas_call(
        body, name="win_param_grads", grid=(H_B,),
        in_specs=[_bs((Q_BLOCK, SPAN), lambda h: (0, 0)), _bs((None, Q_BLOCK, SPAN), lambda h: (h, 0, 0)),
                  _bs((None, Q_BLOCK, 1), lambda h: (h, 0, 0))],
        out_specs=_bs((None, 1, LANES), lambda h: (h, 0, 0)),
        out_shape=jax.ShapeDtypeStruct((H_B, 1, LANES), F32),
        compiler_params=_arb(1),
    )(bucket, dbias, dsink_rows)


def _gate_fwd(proj, o_a, o_b, ts=256):
    s = o_a.shape[0]
    wide = lambda cb: _bs((ts, D_MODEL), lambda i: (i, cb))

    def body(ga_ref, gb_ref, oa_ref, ob_ref, m_ref):
        m_ref[...] = (jax.nn.sigmoid(ga_ref[...]) * oa_ref[...]
                      + jax.nn.sigmoid(gb_ref[...]) * ob_ref[...]).astype(BF16)

    return pl.pallas_call(
        body, name="gate_fwd", grid=(s // ts,),
        in_specs=[wide(P_GA // D_MODEL), wide(P_GB // D_MODEL), wide(0), wide(0)],
        out_specs=wide(0), out_shape=jax.ShapeDtypeStruct((s, D_MODEL), BF16),
        compiler_params=_arb(1),
    )(proj, proj, o_a, o_b)


def _gate_bwd(dmixed, proj, o_a, o_b, ts=256):
    s = o_a.shape[0]
    wide = lambda cb: _bs((ts, D_MODEL), lambda i: (i, cb))

    def body(dm_ref, ga_ref, gb_ref, oa_ref, ob_ref, doa_ref, dob_ref, dga_ref, dgb_ref):
        dm = dm_ref[...]
        sa = jax.nn.sigmoid(ga_ref[...])
        sb = jax.nn.sigmoid(gb_ref[...])
        doa_ref[...] = dm * sa
        dob_ref[...] = (dm * sb).astype(BF16)
        dga_ref[...] = (dm * oa_ref[...] * (sa * (1.0 - sa))).astype(BF16)
        dgb_ref[...] = (dm * ob_ref[...] * (sb * (1.0 - sb))).astype(BF16)

    return pl.pallas_call(
        body, name="gate_bwd", grid=(s // ts,),
        in_specs=[wide(0), wide(P_GA // D_MODEL), wide(P_GB // D_MODEL), wide(0), wide(0)],
        out_specs=[wide(0)] * 4,
        out_shape=[jax.ShapeDtypeStruct((s, D_MODEL), F32), jax.ShapeDtypeStruct((s, D_MODEL), BF16),
                   jax.ShapeDtypeStruct((s, D_MODEL), BF16), jax.ShapeDtypeStruct((s, D_MODEL), BF16)],
        compiler_params=_arb(1),
    )(dmixed, proj, proj, o_a, o_b)


CONV_CHUNK = 512
N_SLAB = D_FF // LANES


def _shifted(ref, c, nchunks):
    r0 = c * CONV_CHUNK
    cur = ref[r0:r0 + CONV_CHUNK, :]
    row = lax.broadcasted_iota(jnp.int32, (8, LANES), 0)
    before = ref[r0 - 8:r0, :][7:8, :] if c > 0 else jnp.zeros((1, LANES), F32)
    after = ref[r0 + CONV_CHUNK:r0 + CONV_CHUNK + 8, :][0:1, :] if c < nchunks - 1 else jnp.zeros((1, LANES), F32)
    down = pltpu.roll(cur, 1, 0)
    up = pltpu.roll(cur, CONV_CHUNK - 1, 0)
    prev = jnp.concatenate([jnp.where(row == 0, before, down[:8]), down[8:]], axis=0)
    nxt = jnp.concatenate([up[:-8], jnp.where(row == 7, after, up[-8:])], axis=0)
    return prev, cur, nxt


def _conv_taps(ref, w_ref, b_ref, c, nchunks):
    prev, cur, nxt = _shifted(ref, c, nchunks)
    conv = prev * w_ref[0:1, :] + cur * w_ref[1:2, :] + nxt * w_ref[2:3, :] + b_ref[...]
    return conv, prev, cur, nxt


def _convffn_fwd(u, conv_w, conv_b):
    s = u.shape[0]
    nchunks = s // CONV_CHUNK

    def body(ug_ref, uv_ref, wg_ref, wv_ref, bg_ref, bv_ref, f_ref):
        for c in range(nchunks):
            cg = _conv_taps(ug_ref, wg_ref, bg_ref, c, nchunks)[0]
            cv = _conv_taps(uv_ref, wv_ref, bv_ref, c, nchunks)[0]
            f_ref[c * CONV_CHUNK:(c + 1) * CONV_CHUNK, :] = (cg * jax.nn.sigmoid(cg) * cv).astype(BF16)

    slab = lambda off: _bs((s, LANES), lambda j: (0, off + j))
    wsl = lambda off: _bs((3, LANES), lambda j: (0, off + j))
    bsl = lambda off: _bs((1, LANES), lambda j: (0, off + j))
    return pl.pallas_call(
        body, name="convffn_fwd", grid=(N_SLAB,),
        in_specs=[slab(0), slab(N_SLAB), wsl(0), wsl(N_SLAB), bsl(0), bsl(N_SLAB)],
        out_specs=slab(0), out_shape=jax.ShapeDtypeStruct((s, D_FF), BF16),
        compiler_params=_arb(1),
    )(u, u, conv_w, conv_w, conv_b, conv_b)


def _convffn_bwd(u, conv_w, conv_b, df):
    s = u.shape[0]
    nchunks = s // CONV_CHUNK

    def body(ug_ref, uv_ref, wg_ref, wv_ref, bg_ref, bv_ref, df_ref, du_ref, dw_ref, db_ref, dcg_ref, dcv_ref):
        dwg = [jnp.zeros((1, LANES), F32) for _ in range(3)]
        dwv = [jnp.zeros((1, LANES), F32) for _ in range(3)]
        dbg = jnp.zeros((1, LANES), F32)
        dbv = jnp.zeros((1, LANES), F32)
        for c in range(nchunks):
            rows = slice(c * CONV_CHUNK, (c + 1) * CONV_CHUNK)
            cg, gp, gc, gn = _conv_taps(ug_ref, wg_ref, bg_ref, c, nchunks)
            cv, vp, vc, vn = _conv_taps(uv_ref, wv_ref, bv_ref, c, nchunks)
            dfv = df_ref[rows, :]
            sg = jax.nn.sigmoid(cg)
            dcg = dfv * cv * (sg * (1.0 + cg * (1.0 - sg)))
            dcv = dfv * (cg * sg)
            dcg_ref[rows, :] = dcg
            dcv_ref[rows, :] = dcv
            for t, (tg, tv) in enumerate(((gp, vp), (gc, vc), (gn, vn))):
                dwg[t] = dwg[t] + jnp.sum(tg * dcg, axis=0, keepdims=True)
                dwv[t] = dwv[t] + jnp.sum(tv * dcv, axis=0, keepdims=True)
            dbg = dbg + jnp.sum(dcg, axis=0, keepdims=True)
            dbv = dbv + jnp.sum(dcv, axis=0, keepdims=True)
        for t in range(3):
            dw_ref[0, t:t + 1, :] = dwg[t]
            dw_ref[1, t:t + 1, :] = dwv[t]
        db_ref[0] = dbg
        db_ref[1] = dbv
        for half, (dc_ref, w_ref) in enumerate(((dcg_ref, wg_ref), (dcv_ref, wv_ref))):
            for c in range(nchunks):
                prev, cur, nxt = _shifted(dc_ref, c, nchunks)
                du = nxt * w_ref[0:1, :] + cur * w_ref[1:2, :] + prev * w_ref[2:3, :]
                du_ref[half, c * CONV_CHUNK:(c + 1) * CONV_CHUNK, :] = du.astype(BF16)

    slab = lambda off: _bs((s, LANES), lambda j: (0, off + j))
    wsl = lambda off: _bs((3, LANES), lambda j: (0, off + j))
    bsl = lambda off: _bs((1, LANES), lambda j: (0, off + j))
    return pl.pallas_call(
        body, name="convffn_bwd", grid=(N_SLAB,),
        in_specs=[slab(0), slab(N_SLAB), wsl(0), wsl(N_SLAB), bsl(0), bsl(N_SLAB), slab(0)],
        out_specs=[_bs((2, s, LANES), lambda j: (0, 0, j)), _bs((2, 3, LANES), lambda j: (0, 0, j)),
                   _bs((2, 1, LANES), lambda j: (0, 0, j))],
        out_shape=[jax.ShapeDtypeStruct((2, s, D_FF), BF16), jax.ShapeDtypeStruct((2, 3, D_FF), F32),
                   jax.ShapeDtypeStruct((2, 1, D_FF), F32)],
        scratch_shapes=[pltpu.VMEM((s, LANES), F32), pltpu.VMEM((s, LANES), F32)],
        compiler_params=_arb(1),
    )(u, u, conv_w, conv_w, conv_b, conv_b, df)


def _row_tile(rows, limit=512):
    best = rows
    for t in range(8, min(rows, limit) + 1, 8):
        if rows % t == 0:
            best = t
    return best if rows % 8 == 0 else rows


def _adamw(name, w, g, m, v):
    rows, cols = w.shape
    tr = _row_tile(rows)
    c1 = 1.0 - ADAM_B1 ** ADAM_STEP
    c2 = 1.0 - ADAM_B2 ** ADAM_STEP

    def body(w_ref, g_ref, m_ref, v_ref, d_ref, nm_ref, nv_ref):
        gv = g_ref[...]
        nm = ADAM_B1 * m_ref[...] + (1.0 - ADAM_B1) * gv
        nv = ADAM_B2 * v_ref[...] + (1.0 - ADAM_B2) * (gv * gv)
        m_hat = nm / c1
        v_hat = nv / c2
        d_ref[...] = -ADAM_LR * (m_hat / (jnp.sqrt(v_hat) + ADAM_EPS) + ADAM_WD * w_ref[...])
        nm_ref[...] = nm
        nv_ref[...] = nv

    spec = _bs((tr, cols), lambda i: (i, 0))
    return pl.pallas_call(
        body, name=name, grid=(rows // tr,), in_specs=[spec] * 4, out_specs=[spec] * 3,
        out_shape=[jax.ShapeDtypeStruct((rows, cols), F32)] * 3, compiler_params=_arb(1),
    )(w, g, m, v)


ANY = pl.BlockSpec(memory_space=pl.ANY)


def _mesh_pos():
    return lax.axis_index("x"), lax.axis_index("y"), lax.axis_index("c")


def _other_chips(x, y):
    return [(1 - x, y), (x, 1 - y), (1 - x, 1 - y)]


def _allgather_weights(shards, split):
    n = len(shards)

    def body(*refs):
        w_refs, o_refs = refs[:n], refs[n:2 * n]
        send_sems, recv_sems, fsend_sems, frecv_sems = refs[2 * n:]
        x, y, c = _mesh_pos()
        p = 2 * x + y
        chips = _other_chips(x, y)

        def piece(i, chip_index, core):
            return o_refs[i].at[chip_index, core] if split[i] else o_refs[i].at[chip_index]

        def remote(src, dst, ssem, rsem, to):
            return pltpu.make_async_remote_copy(src_ref=src, dst_ref=dst, send_sem=ssem, recv_sem=rsem,
                                                device_id=to, device_id_type=MESH)

        sends = []
        for i in range(n):
            src = w_refs[i].at[c] if split[i] else w_refs[i]
            for k, chip in enumerate(chips):
                cp = remote(src, piece(i, p, c), send_sems.at[3 * i + k], recv_sems.at[3 * i + k], (*chip, c))
                cp.start()
                sends.append(cp)
        for i in range(n):
            for k, chip in enumerate(chips):
                pk = 2 * chip[0] + chip[1]
                landed = piece(i, pk, c)
                remote(landed, landed, send_sems.at[3 * i + k], recv_sems.at[3 * i + k], (*chip, c)).wait_recv()
                if split[i]:
                    fw = remote(landed, landed, fsend_sems.at[3 * i + k], frecv_sems.at[3 * i + k], (x, y, 1 - c))
                    fw.start()
                    sends.append(fw)
        for i in range(n):
            if split[i]:
                for k, chip in enumerate(chips):
                    pk = 2 * chip[0] + chip[1]
                    theirs = piece(i, pk, 1 - c)
                    remote(theirs, theirs, fsend_sems.at[3 * i + k], frecv_sems.at[3 * i + k],
                           (x, y, 1 - c)).wait_recv()
        for cp in sends:
            cp.wait_send()

    return pl.pallas_call(
        body, name="allgather_weights",
        in_specs=[ANY] * n, out_specs=[ANY] * n,
        out_shape=[jax.ShapeDtypeStruct((4,) + w.shape, w.dtype) for w in shards],
        scratch_shapes=[pltpu.SemaphoreType.DMA((3 * n,)), pltpu.SemaphoreType.DMA((3 * n,)),
                        pltpu.SemaphoreType.DMA((3 * n,)), pltpu.SemaphoreType.DMA((3 * n,))],
    )(*shards)


def _rs_pair_exchange(name, grads):
    n = len(grads)

    def body(*refs):
        g_refs, o_refs = refs[:n], refs[n:2 * n]
        send_sems, recv_sems = refs[2 * n:]
        x, y, c = _mesh_pos()
        cps = []
        for i in range(n):
            cp = pltpu.make_async_remote_copy(
                src_ref=g_refs[i].at[:, 1 - c], dst_ref=o_refs[i],
                send_sem=send_sems.at[i], recv_sem=recv_sems.at[i], device_id=(x, y, 1 - c), device_id_type=MESH)
            cp.start()
            cps.append(cp)
        for cp in cps:
            cp.wait()

    return pl.pallas_call(
        body, name=name, in_specs=[ANY] * n, out_specs=[ANY] * n,
        out_shape=[jax.ShapeDtypeStruct((4,) + g.shape[2:], F32) for g in grads],
        scratch_shapes=[pltpu.SemaphoreType.DMA((n,)), pltpu.SemaphoreType.DMA((n,))],
    )(*grads)


def _rs_pair_add(name, core, g, recv):
    _, half, cols = recv.shape
    tr = _row_tile(half)
    nr = half // tr

    def body(core_ref, g_ref, r_ref, o_ref):
        o_ref[...] = (g_ref[...] + r_ref[...]).astype(BF16)

    return pl.pallas_call(
        body, name=name,
        grid_spec=pltpu.PrefetchScalarGridSpec(
            num_scalar_prefetch=1, grid=(4, nr),
            in_specs=[pl.BlockSpec((None, None, tr, cols), lambda q, r, cr: (q, cr[0], r, 0)),
                      pl.BlockSpec((None, tr, cols), lambda q, r, cr: (q, r, 0))],
            out_specs=pl.BlockSpec((None, tr, cols), lambda q, r, cr: (q, r, 0))),
        out_shape=jax.ShapeDtypeStruct((4, half, cols), BF16),
        compiler_params=_arb(2),
    )(core, g, recv)


def _rs_ici(pairs):
    n = len(pairs)

    def body(*refs):
        p_refs, o_refs = refs[:n], refs[n:2 * n]
        send_sems, recv_sems = refs[2 * n:]
        x, y, c = _mesh_pos()
        cps = []
        for i in range(n):
            for k, chip in enumerate(_other_chips(x, y)):
                pk = 2 * chip[0] + chip[1]
                cp = pltpu.make_async_remote_copy(
                    src_ref=p_refs[i].at[pk], dst_ref=o_refs[i].at[k],
                    send_sem=send_sems.at[3 * i + k], recv_sem=recv_sems.at[3 * i + k],
                    device_id=(*chip, c), device_id_type=MESH)
                cp.start()
                cps.append(cp)
        for cp in cps:
            cp.wait()

    return pl.pallas_call(
        body, name="rs_ici", in_specs=[ANY] * n, out_specs=[ANY] * n,
        out_shape=[jax.ShapeDtypeStruct((3,) + pr.shape[1:], BF16) for pr in pairs],
        scratch_shapes=[pltpu.SemaphoreType.DMA((3 * n,)), pltpu.SemaphoreType.DMA((3 * n,))],
    )(*pairs)


def _rs_final_add(name, chip, pair, recv):
    _, half, cols = pair.shape
    tr = _row_tile(half)

    def body(chip_ref, p_ref, r_ref, o_ref):
        o_ref[...] = ((p_ref[...].astype(F32) + r_ref[0].astype(F32)) + r_ref[1].astype(F32)) + r_ref[2].astype(F32)

    return pl.pallas_call(
        body, name=name,
        grid_spec=pltpu.PrefetchScalarGridSpec(
            num_scalar_prefetch=1, grid=(half // tr,),
            in_specs=[pl.BlockSpec((None, tr, cols), lambda r, ch: (ch[0], r, 0)),
                      pl.BlockSpec((3, tr, cols), lambda r, ch: (0, r, 0))],
            out_specs=pl.BlockSpec((tr, cols), lambda r, ch: (r, 0))),
        out_shape=jax.ShapeDtypeStruct((half, cols), F32),
        compiler_params=_arb(1),
    )(chip, pair, recv)


def _rs_pair_share(halves):
    n = len(halves)

    def body(*refs):
        h_refs, o_refs = refs[:n], refs[n:2 * n]
        send_sems, recv_sems = refs[2 * n:]
        x, y, c = _mesh_pos()
        cps = []
        for i in range(n):
            cp = pltpu.make_async_remote_copy(src_ref=h_refs[i], dst_ref=o_refs[i], send_sem=send_sems.at[i],
                                              recv_sem=recv_sems.at[i], device_id=(x, y, 1 - c), device_id_type=MESH)
            cp.start()
            cps.append(cp)
        for cp in cps:
            cp.wait()

    return pl.pallas_call(
        body, name="rs_pair_share", in_specs=[ANY] * n, out_specs=[ANY] * n,
        out_shape=[jax.ShapeDtypeStruct(h.shape, F32) for h in halves],
        scratch_shapes=[pltpu.SemaphoreType.DMA((n,)), pltpu.SemaphoreType.DMA((n,))],
    )(*halves)


HBM = pl.BlockSpec(memory_space=pltpu.HBM)
SEM = pl.BlockSpec(memory_space=pltpu.SEMAPHORE)


class _SplitExchange:
    def __init__(self, name, srcs, land_shapes, src_of, dst_of, arrive_of):
        self.name, self.srcs, self.land_shapes = name, list(srcs), list(land_shapes)
        self.src_of, self.dst_of, self.arrive_of = src_of, dst_of, arrive_of

    def _copies(self, src_refs, land_refs, send_sems, recv_sems):
        x, y, c = _mesh_pos()
        p = 2 * x + y
        out = []
        for i, (src, land) in enumerate(zip(src_refs, land_refs)):
            for k, chip in enumerate(_other_chips(x, y)):
                pk = 2 * chip[0] + chip[1]
                sems = dict(send_sem=send_sems.at[3 * i + k], recv_sem=recv_sems.at[3 * i + k],
                            device_id=(*chip, c), device_id_type=MESH)
                sent = pltpu.make_async_remote_copy(src_ref=self.src_of(src, k, p, pk),
                                                    dst_ref=self.dst_of(land, k, p, pk), **sems)
                here = self.arrive_of(land, k, p, pk)
                out.append((sent, pltpu.make_async_remote_copy(src_ref=here, dst_ref=here, **sems)))
        return out

    def start(self):
        n = len(self.srcs)

        def body(*refs):
            for sent, _ in self._copies(refs[:n], refs[n:2 * n], refs[2 * n], refs[2 * n + 1]):
                sent.start()
            refs[-1][...] = jnp.zeros((8, LANES), F32)

        lands = [lax.empty(shape, src.dtype) for shape, src in zip(self.land_shapes, self.srcs)]
        operands = [pltpu.with_memory_space_constraint(a, pltpu.HBM) for a in self.srcs + lands]
        outs = pl.pallas_call(
            body, name=self.name + "_start",
            out_shape=(pltpu.SemaphoreType.DMA((3 * n,)), pltpu.SemaphoreType.DMA((3 * n,)),
                       *[pltpu.HBM(a.shape, a.dtype) for a in operands], jax.ShapeDtypeStruct((8, LANES), F32)),
            in_specs=[HBM] * (2 * n), out_specs=(SEM, SEM, *[HBM] * (2 * n), pl.BlockSpec(memory_space=pltpu.VMEM)),
            input_output_aliases={j: 2 + j for j in range(2 * n)},
            compiler_params=pltpu.CompilerParams(has_side_effects=pltpu.SideEffectType.DATAFLOW_SIDE_EFFECTING),
        )(*operands)
        self._sems, self._thru = outs[:2], list(outs[2:2 + 2 * n])
        return outs[-1]

    def wait(self, after):
        n = len(self.srcs)

        def body(*refs):
            for sent, arrived in self._copies(refs[:n], refs[n:2 * n], refs[2 * n], refs[2 * n + 1]):
                sent.wait_send()
                arrived.wait_recv()

        outs = pl.pallas_call(
            body, name=self.name + "_wait",
            out_shape=tuple(pltpu.HBM(a.shape, a.dtype) for a in self._thru),
            in_specs=[HBM] * (2 * n) + [SEM, SEM, ANY], out_specs=tuple([HBM] * (2 * n)),
            input_output_aliases={j: j for j in range(2 * n)},
            compiler_params=pltpu.CompilerParams(has_side_effects=pltpu.SideEffectType.DATAFLOW_SIDE_EFFECTING),
        )(*self._thru, *self._sems, after)
        return list(outs[n:])


def _small_allreduce(buf):
    rows = buf.shape[0]

    def body(in_ref, out_ref, gather_ref, send_sems, recv_sems):
        x, y, c = _mesh_pos()
        me = 4 * x + 2 * y + c
        gather_ref[me] = in_ref[...]
        cps = []
        for j in range(1, 8):
            peer = (x ^ (j >> 2), y ^ ((j >> 1) & 1), c ^ (j & 1))
            cp = pltpu.make_async_remote_copy(src_ref=in_ref, dst_ref=gather_ref.at[me], send_sem=send_sems.at[j - 1],
                                              recv_sem=recv_sems.at[j - 1], device_id=peer, device_id_type=MESH)
            cp.start()
            cps.append(cp)
        for j in range(1, 8):
            peer_id = 4 * (x ^ (j >> 2)) + 2 * (y ^ ((j >> 1) & 1)) + (c ^ (j & 1))
            slot = gather_ref.at[peer_id]
            pltpu.make_async_remote_copy(src_ref=slot, dst_ref=slot, send_sem=send_sems.at[j - 1],
                                         recv_sem=recv_sems.at[j - 1], device_id=(x, y, c),
                                         device_id_type=MESH).wait_recv()
        for cp in cps:
            cp.wait_send()
        tot = gather_ref[0]
        for d in range(1, 8):
            tot = tot + gather_ref[d]
        out_ref[...] = tot

    return pl.pallas_call(
        body, name="small_allreduce",
        in_specs=[pl.BlockSpec(memory_space=pltpu.VMEM)], out_specs=pl.BlockSpec(memory_space=pltpu.VMEM),
        out_shape=jax.ShapeDtypeStruct(buf.shape, F32),
        scratch_shapes=[pltpu.VMEM((8, rows, LANES), F32), pltpu.SemaphoreType.DMA((7,)),
                        pltpu.SemaphoreType.DMA((7,))],
    )(buf)


def _pack(parts, rows):
    flat = jnp.concatenate([p.reshape(-1).astype(F32) for p in parts])
    return jnp.pad(flat, (0, rows * LANES - flat.shape[0])).reshape(rows, LANES)


def _pack_rows(parts):
    n = sum(math.prod(p.shape) for p in parts)
    return pl.cdiv(pl.cdiv(n, LANES), 8) * 8


def _unpack(buf, shapes):
    flat = buf.reshape(-1)
    out, off = [], 0
    for shp in shapes:
        size = math.prod(shp)
        out.append(flat[off:off + size].reshape(shp))
        off += size
    return out


def _pad_w_in(w):
    z = jnp.zeros((w.shape[0], 64), w.dtype)
    return jnp.concatenate([w[:, 448:1472], w[:, 1984:3008], w[:, 3008:4032], w[:, 0:256], w[:, 1472:1728],
                            w[:, 1728:1984], w[:, 256:384], w[:, 384:448], z], axis=1)


def _unpad_w_in(p):
    return jnp.concatenate([p[:, P_QLAT:P_QLAT + 256], p[:, P_CKV:P_CKV + 128], p[:, P_KR:P_KR + 64],
                            p[:, P_QB:P_QB + 1024], p[:, P_KB:P_KB + 256], p[:, P_VB:P_VB + 256],
                            p[:, P_GA:P_GA + 1024], p[:, P_GB:P_GB + 1024]], axis=1)


def _col_shards(w):
    r, c4 = w.shape
    return w.reshape(r, 4, c4 // 4).transpose(1, 0, 2)


def _local_step(x, positions, target, norm1_g, w_in_p, q_a_norm_g, wq, kv_a_norm_g, wkv, rel_bias, sinks,
                late_weights, norm2_g, conv_w, conv_b, final_norm_g, early_grads=None):
    s = x.shape[0]
    half = QK_ROPE // 2
    inv_freq = jnp.asarray(np.float32(ROPE_THETA) ** (-np.arange(half, dtype=np.float32) / np.float32(half)))
    ang = positions.astype(F32)[:, None] * inv_freq[None, :]
    cos, sin = jnp.cos(ang), jnp.sin(ang)
    z64 = jnp.zeros((s, 64), F32)
    cos_t = jnp.concatenate([cos, cos, z64], axis=1)
    sin_t = jnp.concatenate([-sin, sin, z64], axis=1)
    bucket = _t5_bucket_table()
    sinks1 = sinks.reshape(H_B)

    h1, rstd1 = _rmsnorm_fwd("norm1_fwd", x, norm1_g, D_MODEL, 0)
    proj, proj_b = _matmul("proj", h1, w_in_p, out_shape=(s, W_IN_PAD), out_dtype=F32, grid=(s // MM_ROWS, W_IN_PAD // 1024, 1),
                           a_spec=_bs((MM_ROWS, D_MODEL), lambda i, j, k: (i, 0)), b_spec=_bs((D_MODEL, 1024), lambda i, j, k: (0, j)),
                           o_spec=_bs((MM_ROWS, 1024), lambda i, j, k: (i, j)), contract=NN, bf16_copy=True)
    qn, cn, rstd_q, rstd_c = _lat_norms(proj, q_a_norm_g, kv_a_norm_g)
    q = _q_heads(qn, wq, cos_t, sin_t)
    k, v = _kv_heads(cn, wkv, proj, cos_t, sin_t)
    o_a, lse_a = _mla_fwd(q, k, v)

    bias = _win_bias(bucket, rel_bias)
    o_b, lse_b = _win_fwd(proj_b, bias, sinks1)

    mixed = _gate_fwd(proj, o_a, o_b)
    w_out, w_up, w_down = late_weights(mixed)
    row512 = lambda w: _bs((MM_ROWS, w), lambda i, j, k: (i, 0))
    whole = lambda r, c: _bs((r, c), lambda i, j, k: (0, 0))
    x1 = _matmul("attn_out", mixed, w_out, out_shape=(s, D_MODEL), out_dtype=F32, grid=(s // MM_ROWS, 1, 1),
                 a_spec=row512(D_MODEL), b_spec=whole(D_MODEL, D_MODEL), o_spec=row512(D_MODEL), contract=NN, add=x)
    h2, rstd2 = _rmsnorm_fwd("norm2_fwd", x1, norm2_g, D_MODEL, 0)
    u = _matmul("ffn_up", h2, w_up, out_shape=(s, 2 * D_FF), out_dtype=F32, grid=(s // MM_ROWS, 4, 1),
                a_spec=_bs((MM_ROWS, D_MODEL), lambda i, j, k: (i, 0)), b_spec=_bs((D_MODEL, D_FF // 2), lambda i, j, k: (0, j)),
                o_spec=_bs((MM_ROWS, D_FF // 2), lambda i, j, k: (i, j)), contract=NN)
    f = _convffn_fwd(u, conv_w, conv_b)
    x2 = _matmul("ffn_down", f, w_down, out_shape=(s, D_MODEL), out_dtype=F32, grid=(s // MM_ROWS, 1, 1),
                 a_spec=row512(D_FF), b_spec=whole(D_FF, D_MODEL), o_spec=row512(D_MODEL), contract=NN, add=x1)
    loss, dx2, d_final_g, dx2_b = _final_loss(x2, target, final_norm_g.reshape(1, D_MODEL))
    tk = min(s, DW_ROWS)

    df = _matmul("ffn_down_dx", dx2_b, w_down, out_shape=(s, D_FF), out_dtype=F32, grid=(s // MM_ROWS, 2, 1),
                 a_spec=row512(D_MODEL), b_spec=_bs((D_FF // 2, D_MODEL), lambda i, j, k: (j, 0)),
                 o_spec=_bs((MM_ROWS, D_FF // 2), lambda i, j, k: (i, j)), contract=NT)
    d_w_down = _matmul("ffn_down_dw", f, dx2_b, out_shape=(D_FF, D_MODEL), out_dtype=F32, grid=(2, 1, s // tk),
                       a_spec=_bs((tk, D_FF // 2), lambda i, j, k: (k, i)), b_spec=_bs((tk, D_MODEL), lambda i, j, k: (k, 0)),
                       o_spec=_bs((D_FF // 2, D_MODEL), lambda i, j, k: (i, 0)), contract=TN)
    du, d_conv_w2, d_conv_b2 = _convffn_bwd(u, conv_w, conv_b, df)
    kc = D_FF // 2
    dh2 = _matmul("ffn_up_dx", du, w_up, out_shape=(s, D_MODEL), out_dtype=F32, grid=(s // 1024, 1, 4),
                  a_spec=_bs((None, 1024, kc), lambda i, j, k: (k // 2, i, k % 2)),
                  b_spec=_bs((D_MODEL, kc), lambda i, j, k: (0, k)),
                  o_spec=_bs((1024, D_MODEL), lambda i, j, k: (i, 0)), contract=NT)
    d_w_up = _matmul("ffn_up_dw", h2, du, out_shape=(D_MODEL, 2 * D_FF), out_dtype=F32, grid=(1, 4, s // tk),
                     a_spec=_bs((tk, D_MODEL), lambda i, j, k: (k, 0)),
                     b_spec=_bs((None, tk, kc), lambda i, j, k: (j // 2, k, j % 2)),
                     o_spec=_bs((D_MODEL, kc), lambda i, j, k: (0, j)), contract=TN)
    dx1, d_norm2_g, dx1_b = _rmsnorm_bwd("norm2_bwd", dh2, x1, rstd2, norm2_g, D_MODEL, 0, F32, res=dx2, bf16_copy=True)

    d_w_out = _matmul("attn_out_dw", mixed, dx1_b, out_shape=(D_MODEL, D_MODEL), out_dtype=F32, grid=(1, 1, s // tk),
                      a_spec=_bs((tk, D_MODEL), lambda i, j, k: (k, 0)), b_spec=_bs((tk, D_MODEL), lambda i, j, k: (k, 0)),
                      o_spec=whole(D_MODEL, D_MODEL), contract=TN)
    if early_grads is not None:
        token = early_grads(d_w_out, d_w_up, d_w_down)
        if token is not None:
            sinks1 = sinks1 + token[0, :H_B]
    dmixed = _matmul("attn_out_dx", dx1_b, w_out, out_shape=(s, D_MODEL), out_dtype=F32, grid=(s // MM_ROWS, 1, 1),
                     a_spec=row512(D_MODEL), b_spec=whole(D_MODEL, D_MODEL), o_spec=row512(D_MODEL), contract=NT)
    do_a, do_b, d_ga, d_gb = _gate_bwd(dmixed, proj, o_a, o_b)

    d_qb, dk_pad, dv_pad, dbias, dsink_rows = _win_bwd(proj_b, bias, sinks1, do_b, lse_b)
    wp = _win_param_grads(bucket, dbias, dsink_rows)[:, 0, :]
    d_rel_bias = wp[:, :NUM_BUCKETS].T
    d_sinks = wp[:, NUM_BUCKETS].reshape(1, H_B)
    d_kb = dk_pad[WINDOW:WINDOW + s].astype(BF16)
    d_vb = dv_pad[WINDOW:WINDOW + s].astype(BF16)

    dq, dk, dv = _mla_bwd(q, k, v, do_a, o_a, lse_a)
    dq_pre, dkv_pre, d_kr = _mla_bwd_prep(dq, dk, dv, cos_t, sin_t)
    th = min(s, HEAD_ROWS)
    hgrid = (s // th, 1, H_A)
    hblock = _bs((None, th, HEAD_PAD), lambda i, j, k: (k, i, 0))
    hrows = lambda w: _bs((th, w), lambda i, j, k: (i, 0))
    dqn = _matmul("q_up_dx", dq_pre, wq, out_shape=(s, Q_LORA), out_dtype=F32, grid=hgrid, a_spec=hblock,
                  b_spec=_bs((None, Q_LORA, HEAD_PAD), lambda i, j, k: (k, 0, 0)), o_spec=hrows(Q_LORA), contract=NT)
    dcn = _matmul("kv_up_dx", dkv_pre, wkv, out_shape=(s, KV_LORA), out_dtype=F32, grid=hgrid, a_spec=hblock,
                  b_spec=_bs((None, KV_LORA, HEAD_PAD), lambda i, j, k: (k, 0, 0)), o_spec=hrows(KV_LORA), contract=NT)
    wgrid = (H_A, 1, s // th)
    d_wq = _matmul("q_up_dw", qn, dq_pre, out_shape=(H_A, Q_LORA, HEAD_PAD), out_dtype=F32, grid=wgrid,
                   a_spec=_bs((th, Q_LORA), lambda i, j, k: (k, 0)), b_spec=_bs((None, th, HEAD_PAD), lambda i, j, k: (i, k, 0)),
                   o_spec=_bs((None, Q_LORA, HEAD_PAD), lambda i, j, k: (i, 0, 0)), contract=TN)
    d_wkv = _matmul("kv_up_dw", cn, dkv_pre, out_shape=(H_A, KV_LORA, HEAD_PAD), out_dtype=F32, grid=wgrid,
                    a_spec=_bs((th, KV_LORA), lambda i, j, k: (k, 0)), b_spec=_bs((None, th, HEAD_PAD), lambda i, j, k: (i, k, 0)),
                    o_spec=_bs((None, KV_LORA, HEAD_PAD), lambda i, j, k: (i, 0, 0)), contract=TN)
    d_qlat, d_gq = _rmsnorm_bwd("q_norm_bwd", dqn, proj, rstd_q, q_a_norm_g, Q_LORA, P_QLAT // Q_LORA, BF16)
    d_ckv, d_gkv = _rmsnorm_bwd("kv_norm_bwd", dcn, proj, rstd_c, kv_a_norm_g, KV_LORA, P_CKV // KV_LORA, BF16)

    dproj = jnp.concatenate([d_qb, d_ga, d_gb, d_qlat, d_kb, d_vb, d_ckv, d_kr], axis=1)
    dh1 = _matmul("proj_dx", dproj, w_in_p, out_shape=(s, D_MODEL), out_dtype=F32, grid=(s // 1024, 1, W_IN_PAD // 1024),
                  a_spec=_bs((1024, 1024), lambda i, j, k: (i, k)), b_spec=_bs((D_MODEL, 1024), lambda i, j, k: (0, k)),
                  o_spec=_bs((1024, D_MODEL), lambda i, j, k: (i, 0)), contract=NT)
    d_w_in_p = _matmul("proj_dw", h1, dproj, out_shape=(D_MODEL, W_IN_PAD), out_dtype=F32, grid=(1, W_IN_PAD // 1024, s // tk),
                       a_spec=_bs((tk, D_MODEL), lambda i, j, k: (k, 0)), b_spec=_bs((tk, 1024), lambda i, j, k: (k, j)),
                       o_spec=_bs((D_MODEL, 1024), lambda i, j, k: (0, j)), contract=TN)
    dx, d_norm1_g = _rmsnorm_bwd("norm1_bwd", dh1, x, rstd1, norm1_g, D_MODEL, 0, F32, res=dx1)

    grads = dict(
        norm1_g=d_norm1_g, w_in_p=d_w_in_p, q_a_norm_g=d_gq, wq=d_wq, kv_a_norm_g=d_gkv, wkv=d_wkv,
        rel_bias=d_rel_bias, sinks=d_sinks, w_out=d_w_out, norm2_g=d_norm2_g, w_up=d_w_up,
        conv_w=jnp.concatenate([d_conv_w2[0], d_conv_w2[1]], axis=1),
        conv_b=jnp.concatenate([d_conv_b2[0], d_conv_b2[1]], axis=1),
        w_down=d_w_down, final_norm_g=d_final_g.reshape(D_MODEL))
    return loss, dx, grads


def _wq_heads(w_q_b):
    w = w_q_b.reshape(Q_LORA, H_A, QK_NOPE + QK_ROPE).transpose(1, 0, 2)
    return jnp.pad(w, ((0, 0), (0, 0), (0, HEAD_PAD - QK_NOPE - QK_ROPE)))


def _wq_unheads(d_wq):
    return d_wq[:, :, :QK_NOPE + QK_ROPE].transpose(1, 0, 2).reshape(Q_LORA, H_A * (QK_NOPE + QK_ROPE))


def _wkv_heads(w_kv_b):
    return w_kv_b.reshape(KV_LORA, H_A, QK_NOPE + V_DIM).transpose(1, 0, 2)


def _wkv_unheads(d_wkv):
    return d_wkv.transpose(1, 0, 2).reshape(KV_LORA, H_A * (QK_NOPE + V_DIM))


SMALL = ("norm1_g", "q_a_norm_g", "kv_a_norm_g", "rel_bias", "sinks", "norm2_g", "conv_b", "final_norm_g")
FIRST = ("w_in", "w_q_b", "w_kv_b")
LATER = ("w_out", "w_up", "w_down")
BIG = FIRST + LATER


def kernel(x, positions, norm1_g, w_in, q_a_norm_g, w_q_b, kv_a_norm_g, w_kv_b, rel_bias, sinks, w_out, norm2_g, w_up, conv_w, conv_b, w_down, final_norm_g, loss_target, m_norm1_g, m_w_in, m_q_a_norm_g, m_w_q_b, m_kv_a_norm_g, m_w_kv_b, m_rel_bias, m_sinks, m_w_out, m_norm2_g, m_w_up, m_conv_w, m_conv_b, m_w_down, m_final_norm_g, v_norm1_g, v_w_in, v_q_a_norm_g, v_w_q_b, v_kv_a_norm_g, v_w_kv_b, v_rel_bias, v_sinks, v_w_out, v_norm2_g, v_w_up, v_conv_w, v_conv_b, v_w_down, v_final_norm_g):
    weights = dict(norm1_g=norm1_g, w_in=w_in, q_a_norm_g=q_a_norm_g, w_q_b=w_q_b, kv_a_norm_g=kv_a_norm_g,
                   w_kv_b=w_kv_b, rel_bias=rel_bias, sinks=sinks, w_out=w_out, norm2_g=norm2_g, w_up=w_up,
                   conv_w=conv_w, conv_b=conv_b, w_down=w_down, final_norm_g=final_norm_g)
    mom_m = dict(norm1_g=m_norm1_g, w_in=m_w_in, q_a_norm_g=m_q_a_norm_g, w_q_b=m_w_q_b, kv_a_norm_g=m_kv_a_norm_g,
                 w_kv_b=m_w_kv_b, rel_bias=m_rel_bias, sinks=m_sinks, w_out=m_w_out, norm2_g=m_norm2_g, w_up=m_w_up,
                 conv_w=m_conv_w, conv_b=m_conv_b, w_down=m_w_down, final_norm_g=m_final_norm_g)
    mom_v = dict(norm1_g=v_norm1_g, w_in=v_w_in, q_a_norm_g=v_q_a_norm_g, w_q_b=v_w_q_b, kv_a_norm_g=v_kv_a_norm_g,
                 w_kv_b=v_w_kv_b, rel_bias=v_rel_bias, sinks=v_sinks, w_out=v_w_out, norm2_g=v_norm2_g, w_up=v_w_up,
                 conv_w=v_conv_w, conv_b=v_conv_b, w_down=v_w_down, final_norm_g=v_final_norm_g)
    shard2d = {n: weights[n][0] for n in BIG}
    conv_w_shard = conv_w[0]
    xi, yi, ci = lax.axis_index("x"), lax.axis_index("y"), lax.axis_index("c")
    chip = (2 * xi + yi).astype(jnp.int32)

    core = ci.astype(jnp.int32).reshape(1)
    chip1 = chip.reshape(1)
    cat_cols = lambda a: jnp.concatenate([a[0], a[1], a[2], a[3]], axis=1)
    own_slot = lambda a, own: lax.dynamic_update_index_in_dim(a, own, chip, 0)
    halved = lambda a: a.reshape((2, a.shape[0] // 2) + a.shape[1:])
    quartered = lambda a: a.reshape(4, 2, a.shape[1] // 2, a.shape[2])

    send = [halved(shard2d[n].astype(BF16)) for n in FIRST] + [conv_w_shard]
    gathered = [own_slot(a, own) for a, own in zip(_allgather_weights(send, split=[True] * len(FIRST) + [False]), send)]
    g = {n: a.reshape((4,) + shard2d[n].shape) for n, a in zip(FIRST, gathered)}
    w_in_p = _pad_w_in(cat_cols(g["w_in"]))
    wq = _wq_heads(cat_cols(g["w_q_b"]))
    wkv = _wkv_heads(cat_cols(g["w_kv_b"]))
    conv_w_f = cat_cols(gathered[-1])

    later = [shard2d[n].astype(BF16) for n in LATER]
    gather2 = _SplitExchange("gather_later", later, [(4,) + a.shape for a in later],
                             src_of=lambda ref, k, p, pk: ref, dst_of=lambda ref, k, p, pk: ref.at[p],
                             arrive_of=lambda ref, k, p, pk: ref.at[pk])
    norm1_g_in = norm1_g + gather2.start()[:1, :1]

    def late_weights(after):
        w_out_g, w_up_g, w_down_g = [own_slot(a, own) for a, own in zip(gather2.wait(after), later)]
        return w_out_g.reshape(D_MODEL, D_MODEL), cat_cols(w_up_g), w_down_g.reshape(D_FF, D_MODEL)

    early = {}

    def early_grads(d_w_out, d_w_up, d_w_down):
        grads = [quartered(d_w_out.reshape(4, D_MODEL // 4, D_MODEL)), quartered(_col_shards(d_w_up)),
                 quartered(d_w_down.reshape(4, D_FF // 4, D_MODEL))]
        recv = _rs_pair_exchange("rs_pair_exchange_early", grads)
        early["pairs"] = [_rs_pair_add(f"rs_pair_add_{n}", core, gfull, r) for n, gfull, r in zip(LATER, grads, recv)]
        early["ici"] = _SplitExchange("rs_ici_early", early["pairs"], [(3,) + a.shape[1:] for a in early["pairs"]],
                                      src_of=lambda ref, k, p, pk: ref.at[pk], dst_of=lambda ref, k, p, pk: ref.at[k],
                                      arrive_of=lambda ref, k, p, pk: ref.at[k])
        return early["ici"].start()

    loss, dx, gr = _local_step(x[0], positions, loss_target[0], norm1_g_in, w_in_p, q_a_norm_g, wq, kv_a_norm_g, wkv,
                               rel_bias, sinks, late_weights, norm2_g, conv_w_f, conv_b, final_norm_g, early_grads)

    grads = [quartered(_col_shards(_unpad_w_in(gr["w_in_p"]))), quartered(_col_shards(_wq_unheads(gr["wq"]))),
             quartered(_col_shards(_wkv_unheads(gr["wkv"])))]
    recv1 = _rs_pair_exchange("rs_pair_exchange", grads)
    pairs = [_rs_pair_add(f"rs_pair_add_{n}", core, gfull, r) for n, gfull, r in zip(FIRST, grads, recv1)]
    recv2 = list(_rs_ici(pairs)) + early["ici"].wait(dx)
    pairs = pairs + early["pairs"]
    halves = [_rs_final_add(f"rs_final_add_{n}", chip1, pr, r) for n, pr, r in zip(FIRST + LATER, pairs, recv2)]
    both = lambda mine, theirs: jnp.where(ci == 0, jnp.concatenate([mine, theirs]), jnp.concatenate([theirs, mine]))
    reduced = {n: both(h, t) for n, h, t in zip(FIRST + LATER, halves, _rs_pair_share(halves))}

    small_parts = [gr[n] for n in SMALL] + [gr["conv_w"], loss]
    rows = _pack_rows(small_parts)
    summed = _unpack(_small_allreduce(_pack(small_parts, rows)), [p.shape for p in small_parts])
    small_g = dict(zip(SMALL, summed[:len(SMALL)]))
    conv_w_g = lax.dynamic_slice_in_dim(summed[len(SMALL)], chip * (2 * D_FF // 4), 2 * D_FF // 4, axis=1)
    loss_out = summed[-1].reshape(())

    out_g, out_d, out_m, out_v = {}, {}, {}, {}
    for n in BIG:
        gsh = reduced[n]
        d, nm, nv = _adamw(f"adamw_{n}", shard2d[n], gsh, mom_m[n][0], mom_v[n][0])
        out_g[n], out_d[n], out_m[n], out_v[n] = gsh[None], d[None], nm[None], nv[None]
    names = SMALL + ("conv_w",)
    shapes = [weights[n].shape for n in names]
    sg = [small_g[n].reshape(weights[n].shape) for n in SMALL] + [conv_w_g[None]]
    prow = _pack_rows([weights[n] for n in names])
    d, nm, nv = _adamw("adamw_small", _pack([weights[n] for n in names], prow), _pack(sg, prow),
                       _pack([mom_m[n] for n in names], prow), _pack([mom_v[n] for n in names], prow))
    for n, gg, dd, mm, vv in zip(names, sg, _unpack(d, shapes), _unpack(nm, shapes), _unpack(nv, shapes)):
        out_g[n], out_d[n], out_m[n], out_v[n] = gg, dd, mm, vv

    order = ("norm1_g", "w_in", "q_a_norm_g", "w_q_b", "kv_a_norm_g", "w_kv_b", "rel_bias", "sinks", "w_out",
             "norm2_g", "w_up", "conv_w", "conv_b", "w_down", "final_norm_g")
    return (loss_out, dx[None], *[out_g[n] for n in order], *[out_d[n] for n in order],
            *[out_m[n] for n in order], *[out_v[n] for n in order])
```

```python
import functools
import math

import jax
import jax.numpy as jnp
import numpy as np
from jax import lax
from jax.experimental import pallas as pl
from jax.experimental.pallas import tpu as pltpu

F32 = jnp.float32
BF16 = jnp.bfloat16
MESH = pl.DeviceIdType.MESH

D_MODEL = 1024
EPS = 1e-6
H_A = 8
QK_NOPE = 128
QK_ROPE = 64
V_DIM = 128
Q_LORA = 256
KV_LORA = 128
ROPE_THETA = 10000.0
H_B = 16
KV_B = 4
GROUP = 4
HD_B = 64
WINDOW = 128
Q_BLOCK = 128
NUM_BUCKETS = 32
MAX_DISTANCE = 128
D_FF = 2816
HEAD_PAD = 256

ADAM_LR = 0.001
ADAM_B1 = 0.9
ADAM_B2 = 0.999
ADAM_EPS = 1e-08
ADAM_WD = 0.01
ADAM_STEP = 10

LANES = 128
P_QB, P_GA, P_GB, P_QLAT, P_KB, P_VB, P_CKV, P_KR = 0, 1024, 2048, 3072, 3328, 3584, 3840, 3968
W_IN_PAD = 4096

NT = (((1,), (1,)), ((), ()))
NN = (((1,), (0,)), ((), ()))
TN = (((0,), (0,)), ((), ()))


def _arb(n):
    return pltpu.CompilerParams(dimension_semantics=("arbitrary",) * n)


def _matmul(name, a, b, *, out_shape, out_dtype, grid, a_spec, b_spec, o_spec, contract, add=None, bf16_copy=False,
            after=None):
    nk = grid[2]
    acc_shape = tuple(d for d in o_spec.block_shape if d is not None)
    n_in = 2 + (add is not None) + (after is not None)
    n_out = 2 if bf16_copy else 1

    def body(*refs):
        a_ref, b_ref = refs[:2]
        add_ref = refs[2] if add is not None else None
        o_refs = refs[n_in:n_in + n_out]
        scratch = refs[n_in + n_out:]
        prod = lax.dot_general(a_ref[...].astype(BF16), b_ref[...].astype(BF16), contract,
                               preferred_element_type=F32)

        def finish(val):
            if add_ref is not None:
                val = add_ref[...] + val
            o_refs[0][...] = val.astype(out_dtype)
            if bf16_copy:
                o_refs[1][...] = val.astype(BF16)

        if nk == 1:
            finish(prod)
        else:
            acc_ref = scratch[0]
            k = pl.program_id(2)

            @pl.when(k == 0)
            def _():
                acc_ref[...] = prod

            @pl.when((k > 0) & (k < nk - 1))
            def _():
                acc_ref[...] += prod

            @pl.when(k == nk - 1)
            def _():
                finish(acc_ref[...] + prod)

    in_specs = [a_spec, b_spec]
    args = [a, b]
    if add is not None:
        in_specs.append(o_spec)
        args.append(add)
    if after is not None:
        in_specs.append(pl.BlockSpec(memory_space=pl.ANY))
        args.append(after)
    out_shapes = [jax.ShapeDtypeStruct(out_shape, out_dtype)]
    if bf16_copy:
        out_shapes.append(jax.ShapeDtypeStruct(out_shape, BF16))
    res = pl.pallas_call(
        body, name=name, grid=grid, in_specs=in_specs, out_specs=[o_spec] * n_out, out_shape=out_shapes,
        scratch_shapes=[pltpu.VMEM(acc_shape, F32)] if nk > 1 else [],
        compiler_params=_arb(3),
    )(*args)
    return res if bf16_copy else res[0]


def _bs(block, fn):
    return pl.BlockSpec(block, fn)


def _rmsnorm_fwd(name, src, g, d, cb, ts=512):
    s = src.shape[0]

    def body(x_ref, g_ref, h_ref, r_ref):
        x = x_ref[...]
        r = lax.rsqrt(jnp.mean(x * x, axis=-1, keepdims=True) + EPS)
        h_ref[...] = (x * r * g_ref[...]).astype(BF16)
        r_ref[...] = r

    return pl.pallas_call(
        body, name=name, grid=(s // ts,),
        in_specs=[_bs((ts, d), lambda i: (i, cb)), _bs((1, d), lambda i: (0, 0))],
        out_specs=[_bs((ts, d), lambda i: (i, 0)), _bs((ts, 1), lambda i: (i, 0))],
        out_shape=[jax.ShapeDtypeStruct((s, d), BF16), jax.ShapeDtypeStruct((s, 1), F32)],
        compiler_params=_arb(1),
    )(src, g)


def _rmsnorm_bwd(name, dy, src, rstd, g, d, cb, out_dtype, res=None, bf16_copy=False, ts=512):
    s = src.shape[0]

    def body(*refs):
        dy_ref, x_ref, r_ref, g_ref = refs[:4]
        res_ref = refs[4] if res is not None else None
        dx_ref, dg_ref = refs[n_in:n_in + 2]
        dyv = dy_ref[...]
        r = r_ref[...]
        xhat = x_ref[...] * r
        dyh = dyv * g_ref[...]
        c = jnp.mean(dyh * xhat, axis=-1, keepdims=True)
        dx = r * (dyh - xhat * c)
        if res_ref is not None:
            dx = res_ref[...] + dx
        dx_ref[...] = dx.astype(out_dtype)
        if bf16_copy:
            refs[n_in + 2][...] = dx.astype(BF16)
        part = jnp.sum(dyv * xhat, axis=0, keepdims=True)

        @pl.when(pl.program_id(0) == 0)
        def _():
            dg_ref[...] = part

        @pl.when(pl.program_id(0) > 0)
        def _():
            dg_ref[...] += part

    in_specs = [_bs((ts, d), lambda i: (i, 0)), _bs((ts, d), lambda i: (i, cb)),
                _bs((ts, 1), lambda i: (i, 0)), _bs((1, d), lambda i: (0, 0))]
    args = [dy, src, rstd, g]
    if res is not None:
        in_specs.append(_bs((ts, d), lambda i: (i, 0)))
        args.append(res)
    n_in = len(args)
    out_specs = [_bs((ts, d), lambda i: (i, 0)), _bs((1, d), lambda i: (0, 0))]
    out_shape = [jax.ShapeDtypeStruct((s, d), out_dtype), jax.ShapeDtypeStruct((1, d), F32)]
    if bf16_copy:
        out_specs.append(_bs((ts, d), lambda i: (i, 0)))
        out_shape.append(jax.ShapeDtypeStruct((s, d), BF16))
    return pl.pallas_call(
        body, name=name, grid=(s // ts,), in_specs=in_specs, out_specs=out_specs, out_shape=out_shape,
        compiler_params=_arb(1),
    )(*args)


def _final_loss(x2, target, g, ts=512):
    s, d = x2.shape

    def body(x_ref, t_ref, g_ref, loss_ref, dx_ref, dg_ref, dxb_ref):
        x = x_ref[...]
        r = lax.rsqrt(jnp.mean(x * x, axis=-1, keepdims=True) + EPS)
        xhat = x * r
        gv = g_ref[...]
        err = xhat * gv - t_ref[...]
        lpart = 0.5 * jnp.sum(jnp.mean(err * err, axis=-1, keepdims=True), axis=0, keepdims=True)
        dyv = err * (1.0 / d)
        dyh = dyv * gv
        c = jnp.mean(dyh * xhat, axis=-1, keepdims=True)
        dx = r * (dyh - xhat * c)
        dx_ref[...] = dx
        dxb_ref[...] = dx.astype(BF16)
        gpart = jnp.sum(dyv * xhat, axis=0, keepdims=True)

        @pl.when(pl.program_id(0) == 0)
        def _():
            dg_ref[...] = gpart
            loss_ref[...] = lpart

        @pl.when(pl.program_id(0) > 0)
        def _():
            dg_ref[...] += gpart
            loss_ref[...] += lpart

    return pl.pallas_call(
        body, name="final_loss", grid=(s // ts,),
        in_specs=[_bs((ts, d), lambda i: (i, 0)), _bs((ts, d), lambda i: (i, 0)), _bs((1, d), lambda i: (0, 0))],
        out_specs=[_bs((1, 1), lambda i: (0, 0)), _bs((ts, d), lambda i: (i, 0)), _bs((1, d), lambda i: (0, 0)),
                   _bs((ts, d), lambda i: (i, 0))],
        out_shape=[jax.ShapeDtypeStruct((1, 1), F32), jax.ShapeDtypeStruct((s, d), F32),
                   jax.ShapeDtypeStruct((1, d), F32), jax.ShapeDtypeStruct((s, d), BF16)],
        compiler_params=_arb(1),
    )(x2, target, g)


def _swap_halves(t):
    lane = lax.broadcasted_iota(jnp.int32, t.shape, 1)
    return jnp.where(lane < 32, pltpu.roll(t, 96, 1), pltpu.roll(t, 32, 1))


def _rope_fwd(t, cos_t, sin_t):
    return t * cos_t + _swap_halves(t) * sin_t


def _rope_bwd(dt, cos_t, sin_t):
    return dt * cos_t - _swap_halves(dt) * sin_t


def _lat_norms(proj, gq, gkv, ts=512):
    s = proj.shape[0]

    def body(q_ref, c_ref, gq_ref, gkv_ref, qn_ref, cn_ref, rq_ref, rc_ref):
        q = q_ref[...]
        rq = lax.rsqrt(jnp.mean(q * q, axis=-1, keepdims=True) + EPS)
        qn_ref[...] = (q * rq * gq_ref[...]).astype(BF16)
        rq_ref[...] = rq
        cv = c_ref[...]
        rc = lax.rsqrt(jnp.mean(cv * cv, axis=-1, keepdims=True) + EPS)
        cn_ref[...] = (cv * rc * gkv_ref[...]).astype(BF16)
        rc_ref[...] = rc

    return pl.pallas_call(
        body, name="lat_norms", grid=(s // ts,),
        in_specs=[_bs((ts, Q_LORA), lambda i: (i, P_QLAT // Q_LORA)),
                  _bs((ts, KV_LORA), lambda i: (i, P_CKV // KV_LORA)),
                  _bs((1, Q_LORA), lambda i: (0, 0)), _bs((1, KV_LORA), lambda i: (0, 0))],
        out_specs=[_bs((ts, Q_LORA), lambda i: (i, 0)), _bs((ts, KV_LORA), lambda i: (i, 0)),
                   _bs((ts, 1), lambda i: (i, 0)), _bs((ts, 1), lambda i: (i, 0))],
        out_shape=[jax.ShapeDtypeStruct((s, Q_LORA), BF16), jax.ShapeDtypeStruct((s, KV_LORA), BF16),
                   jax.ShapeDtypeStruct((s, 1), F32), jax.ShapeDtypeStruct((s, 1), F32)],
        compiler_params=_arb(1),
    )(proj, proj, gq, gkv)


HEAD_ROWS = 2048
DW_ROWS = 2048
MM_ROWS = 1024


def _q_heads(qn, wq, cos_t, sin_t):
    s = qn.shape[0]
    ts = min(s, HEAD_ROWS)

    def body(qn_ref, w_ref, cos_ref, sin_ref, q_ref):
        o = jnp.dot(qn_ref[...], w_ref[...], preferred_element_type=F32)
        q_ref[:, :LANES] = o[:, :LANES].astype(BF16)
        q_ref[:, LANES:] = _rope_fwd(o[:, LANES:], cos_ref[...], sin_ref[...]).astype(BF16)

    return pl.pallas_call(
        body, name="q_heads", grid=(H_A, s // ts),
        in_specs=[_bs((ts, Q_LORA), lambda h, i: (i, 0)), _bs((None, Q_LORA, HEAD_PAD), lambda h, i: (h, 0, 0)),
                  _bs((ts, LANES), lambda h, i: (i, 0)), _bs((ts, LANES), lambda h, i: (i, 0))],
        out_specs=_bs((None, ts, HEAD_PAD), lambda h, i: (h, i, 0)),
        out_shape=jax.ShapeDtypeStruct((H_A, s, HEAD_PAD), BF16),
        compiler_params=_arb(2),
    )(qn, wq, cos_t, sin_t)


def _kv_heads(cn, wkv, proj, cos_t, sin_t):
    s = cn.shape[0]
    ts = min(s, HEAD_ROWS)

    def body(cn_ref, w_ref, kr_ref, cos_ref, sin_ref, k_ref, v_ref):
        o = jnp.dot(cn_ref[...], w_ref[...], preferred_element_type=F32)
        k_ref[:, :LANES] = o[:, :LANES].astype(BF16)
        k_ref[:, LANES:] = _rope_fwd(kr_ref[...], cos_ref[...], sin_ref[...]).astype(BF16)
        v_ref[...] = o[:, LANES:].astype(BF16)

    return pl.pallas_call(
        body, name="kv_heads", grid=(H_A, s // ts),
        in_specs=[_bs((ts, KV_LORA), lambda h, i: (i, 0)),
                  _bs((None, KV_LORA, QK_NOPE + V_DIM), lambda h, i: (h, 0, 0)),
                  _bs((ts, LANES), lambda h, i: (i, P_KR // LANES)),
                  _bs((ts, LANES), lambda h, i: (i, 0)), _bs((ts, LANES), lambda h, i: (i, 0))],
        out_specs=[_bs((None, ts, HEAD_PAD), lambda h, i: (h, i, 0)), _bs((None, ts, V_DIM), lambda h, i: (h, i, 0))],
        out_shape=[jax.ShapeDtypeStruct((H_A, s, HEAD_PAD), BF16), jax.ShapeDtypeStruct((H_A, s, V_DIM), BF16)],
        compiler_params=_arb(2),
    )(cn, wkv, proj, cos_t, sin_t)


MLA_SCALE = 1.0 / math.sqrt(QK_NOPE + QK_ROPE)
LOG2E = math.log2(math.e)
MLA_EXP2_SCALE = MLA_SCALE * LOG2E


def _lane_tiles(a):
    return [a[:, j * LANES:(j + 1) * LANES] for j in range(a.shape[1] // LANES)]


def _mla_fwd(q, k, v, tq=512, tk=512):
    s = q.shape[1]
    nk = s // tk

    def body(q_ref, k_ref, v_ref, o_ref, lse_ref, m_ref, l_ref, acc_ref):
        m_ref[...] = jnp.full(m_ref.shape, -jnp.inf, F32)
        l_ref[...] = jnp.zeros(l_ref.shape, F32)
        acc_ref[...] = jnp.zeros(acc_ref.shape, F32)
        qv = q_ref[...]

        def step(c, carry):
            rows = pl.ds(pl.multiple_of(c * tk, tk), tk)
            raw = lax.dot_general(qv, k_ref[rows, :], NT, preferred_element_type=F32)
            m_prev = m_ref[...]
            m_new = jnp.maximum(m_prev, jnp.max(raw, axis=-1, keepdims=True))
            alpha = jnp.exp2((m_prev - m_new) * MLA_EXP2_SCALE)
            ps = [jnp.exp2((t - m_new) * MLA_EXP2_SCALE) for t in _lane_tiles(raw)]
            l_ref[...] = alpha * l_ref[...] + functools.reduce(lambda a, b: a + b, ps)
            p = jnp.concatenate(ps, axis=1).astype(BF16)
            acc_ref[...] = alpha * acc_ref[...] + jnp.dot(p, v_ref[rows, :], preferred_element_type=F32)
            m_ref[...] = m_new
            return carry

        lax.fori_loop(0, nk, step, 0, unroll=True)
        l = jnp.sum(l_ref[...], axis=-1, keepdims=True)
        o_ref[...] = acc_ref[...] / l
        lse_ref[...] = m_ref[...] * MLA_SCALE + jnp.log(l)

    return pl.pallas_call(
        body, name="mla_fwd", grid=(H_A, s // tq),
        in_specs=[_bs((None, tq, HEAD_PAD), lambda h, i: (h, i, 0)),
                  _bs((None, s, HEAD_PAD), lambda h, i: (h, 0, 0)),
                  _bs((None, s, V_DIM), lambda h, i: (h, 0, 0))],
        out_specs=[_bs((tq, V_DIM), lambda h, i: (i, h)), _bs((None, tq, LANES), lambda h, i: (h, i, 0))],
        out_shape=[jax.ShapeDtypeStruct((s, H_A * V_DIM), F32), jax.ShapeDtypeStruct((H_A, s, LANES), F32)],
        scratch_shapes=[pltpu.VMEM((tq, LANES), F32), pltpu.VMEM((tq, LANES), F32), pltpu.VMEM((tq, V_DIM), F32)],
        compiler_params=_arb(2),
    )(q, k, v)


def _mla_bwd(q, k, v, do, o, lse, tq=512, tk=512):
    s = q.shape[1]
    nq = s // tq

    def body(q_ref, k_ref, v_ref, do_ref, o_ref, lse_ref, dq_ref, dk_ref, dv_ref, delta_ref):
        @pl.when(pl.program_id(1) == 0)
        def _():
            def init(c, carry):
                rows = pl.ds(pl.multiple_of(c * tq, tq), tq)
                delta = jnp.sum(do_ref[rows, :] * o_ref[rows, :], axis=-1, keepdims=True)
                delta_ref[rows, :] = jnp.broadcast_to(delta, (tq, LANES))
                dq_ref[rows, :] = jnp.zeros((tq, HEAD_PAD), F32)
                return carry

            lax.fori_loop(0, nq, init, 0)

        dk_ref[...] = jnp.zeros(dk_ref.shape, F32)
        dv_ref[...] = jnp.zeros(dv_ref.shape, F32)
        kb = k_ref[...]
        vb = v_ref[...]

        def step(c, carry):
            rows = pl.ds(pl.multiple_of(c * tq, tq), tq)
            qc = q_ref[rows, :]
            doc = do_ref[rows, :].astype(BF16)
            raw = lax.dot_general(qc, kb, NT, preferred_element_type=F32)
            dp = lax.dot_general(doc, vb, NT, preferred_element_type=F32)
            lse2 = lse_ref[rows, :] * LOG2E
            delta = delta_ref[rows, :]
            ps = [jnp.exp2(t * MLA_EXP2_SCALE - lse2) for t in _lane_tiles(raw)]
            dss = [pj * (dj - delta) * MLA_SCALE for pj, dj in zip(ps, _lane_tiles(dp))]
            p = jnp.concatenate(ps, axis=1).astype(BF16)
            ds = jnp.concatenate(dss, axis=1).astype(BF16)
            dv_ref[...] += lax.dot_general(p, doc, TN, preferred_element_type=F32)
            dk_ref[...] += lax.dot_general(ds, qc, TN, preferred_element_type=F32)
            dq_ref[rows, :] += jnp.dot(ds, kb, preferred_element_type=F32)
            return carry

        lax.fori_loop(0, nq, step, 0, unroll=True)

    return pl.pallas_call(
        body, name="mla_bwd", grid=(H_A, s // tk),
        in_specs=[_bs((None, s, HEAD_PAD), lambda h, j: (h, 0, 0)),
                  _bs((None, tk, HEAD_PAD), lambda h, j: (h, j, 0)),
                  _bs((None, tk, V_DIM), lambda h, j: (h, j, 0)),
                  _bs((s, V_DIM), lambda h, j: (0, h)), _bs((s, V_DIM), lambda h, j: (0, h)),
                  _bs((None, s, LANES), lambda h, j: (h, 0, 0))],
        out_specs=[_bs((None, s, HEAD_PAD), lambda h, j: (h, 0, 0)),
                   _bs((None, tk, HEAD_PAD), lambda h, j: (h, j, 0)),
                   _bs((None, tk, V_DIM), lambda h, j: (h, j, 0))],
        out_shape=[jax.ShapeDtypeStruct((H_A, s, HEAD_PAD), F32), jax.ShapeDtypeStruct((H_A, s, HEAD_PAD), F32),
                   jax.ShapeDtypeStruct((H_A, s, V_DIM), F32)],
        scratch_shapes=[pltpu.VMEM((s, LANES), F32)],
        compiler_params=_arb(2),
    )(q, k, v, do, o, lse)


def _mla_bwd_prep(dq, dk, dv, cos_t, sin_t, ts=256):
    s = dq.shape[1]

    def body(dq_ref, dk_ref, dv_ref, cos_ref, sin_ref, dqp_ref, dkvp_ref, dkr_ref):
        cos_v = cos_ref[...]
        sin_v = sin_ref[...]
        kr = jnp.zeros((ts, LANES), F32)
        for h in range(H_A):
            dqp_ref[h, :, :LANES] = dq_ref[h, :, :LANES].astype(BF16)
            dqp_ref[h, :, LANES:] = _rope_bwd(dq_ref[h, :, LANES:], cos_v, sin_v).astype(BF16)
            dkvp_ref[h, :, :LANES] = dk_ref[h, :, :LANES].astype(BF16)
            dkvp_ref[h, :, LANES:] = dv_ref[h].astype(BF16)
            kr = kr + dk_ref[h, :, LANES:]
        dkr_ref[...] = _rope_bwd(kr, cos_v, sin_v).astype(BF16)

    blk3 = lambda w: _bs((H_A, ts, w), lambda i: (0, i, 0))
    return pl.pallas_call(
        body, name="mla_bwd_prep", grid=(s // ts,),
        in_specs=[blk3(HEAD_PAD), blk3(HEAD_PAD), blk3(V_DIM),
                  _bs((ts, LANES), lambda i: (i, 0)), _bs((ts, LANES), lambda i: (i, 0))],
        out_specs=[blk3(HEAD_PAD), blk3(HEAD_PAD), _bs((ts, LANES), lambda i: (i, 0))],
        out_shape=[jax.ShapeDtypeStruct((H_A, s, HEAD_PAD), BF16), jax.ShapeDtypeStruct((H_A, s, HEAD_PAD), BF16),
                   jax.ShapeDtypeStruct((s, LANES), BF16)],
        compiler_params=_arb(1),
    )(dq, dk, dv, cos_t, sin_t)


WIN_SCALE = 1.0 / math.sqrt(HD_B)
SPAN = Q_BLOCK + 2 * WINDOW


def _t5_bucket_table():
    a = jnp.arange(Q_BLOCK, dtype=jnp.int32)[:, None]
    c = jnp.arange(SPAN, dtype=jnp.int32)[None, :]
    rel = c - WINDOW - a
    nb = NUM_BUCKETS // 2
    max_exact = nb // 2
    base = (rel > 0).astype(jnp.int32) * nb
    n = jnp.abs(rel)
    nf = jnp.maximum(n, 1).astype(F32)
    large = max_exact + (jnp.log(nf / max_exact) / math.log(MAX_DISTANCE / max_exact)
                         * (nb - max_exact)).astype(jnp.int32)
    large = jnp.minimum(large, nb - 1)
    return base + jnp.where(n < max_exact, n, large)


def _win_bias(bucket, rel_bias):
    def body(rb_ref, bk_ref, o_ref):
        h = pl.program_id(0)
        bk = bk_ref[...]
        acc = jnp.zeros((Q_BLOCK, SPAN), F32)
        for b in range(NUM_BUCKETS):
            acc = jnp.where(bk == b, rb_ref[b, h], acc)
        o_ref[...] = acc

    return pl.pallas_call(
        body, name="win_bias", grid=(H_B,),
        in_specs=[pl.BlockSpec(memory_space=pltpu.SMEM), _bs((Q_BLOCK, SPAN), lambda h: (0, 0))],
        out_specs=_bs((None, Q_BLOCK, SPAN), lambda h: (h, 0, 0)),
        out_shape=jax.ShapeDtypeStruct((H_B, Q_BLOCK, SPAN), F32),
        compiler_params=_arb(1),
    )(rel_bias, bucket)


GROUP_W = GROUP * HD_B


def _win_kv_rows(n, j, nblk):
    blk = jnp.clip(n + j - 1, 0, nblk - 1)
    return pl.ds(pl.multiple_of(blk * Q_BLOCK, Q_BLOCK), Q_BLOCK)


def _win_head_cols(kv):
    return slice(kv * HD_B, (kv + 1) * HD_B)


def _win_stack(ref, kv):
    return jnp.concatenate([ref[:, kv * GROUP_W + g * HD_B:kv * GROUP_W + (g + 1) * HD_B] for g in range(GROUP)], axis=0)


def _win_unstack(ref, kv, val):
    for g in range(GROUP):
        ref[:, kv * GROUP_W + g * HD_B:kv * GROUP_W + (g + 1) * HD_B] = val[g * Q_BLOCK:(g + 1) * Q_BLOCK].astype(ref.dtype)


def _win_scores(q, k_ref, kv, bias_ref, n, nblk):
    a = lax.broadcasted_iota(jnp.int32, (GROUP, Q_BLOCK, Q_BLOCK), 1)
    cc = lax.broadcasted_iota(jnp.int32, (GROUP, Q_BLOCK, Q_BLOCK), 2)
    valid = [(cc >= a) & (n > 0), None, (cc <= a) & (n < nblk - 1)]
    out = []
    for j in range(3):
        sc = lax.dot_general(q, k_ref[_win_kv_rows(n, j, nblk), _win_head_cols(kv)], NT, preferred_element_type=F32)
        sc = (sc.reshape(GROUP, Q_BLOCK, Q_BLOCK) * WIN_SCALE
              + bias_ref[kv * GROUP:(kv + 1) * GROUP, :, j * Q_BLOCK:(j + 1) * Q_BLOCK])
        if valid[j] is not None:
            sc = jnp.where(valid[j], sc, -1e30)
        out.append(sc)
    return out


def _win_sink(sink_ref, kv):
    hs = lax.broadcasted_iota(jnp.int32, (GROUP, Q_BLOCK, 1), 0)
    sk = jnp.zeros((GROUP, Q_BLOCK, 1), F32)
    for g in range(GROUP):
        sk = jnp.where(hs == g, sink_ref[kv * GROUP + g], sk)
    return sk


def _win_fwd(proj_b, bias, sinks):
    s = proj_b.shape[0]
    nblk = s // Q_BLOCK
    rows = GROUP * Q_BLOCK

    def body(sink_ref, q_ref, k_ref, v_ref, bias_ref, o_ref, lse_ref):
        n = pl.program_id(0)
        for kv in range(KV_B):
            sk = _win_sink(sink_ref, kv)
            q = _win_stack(q_ref, kv)
            ss = _win_scores(q, k_ref, kv, bias_ref, n, nblk)
            m = jnp.maximum(jnp.max(jnp.maximum(jnp.maximum(ss[0], ss[1]), ss[2]), axis=2, keepdims=True), sk)
            es = [jnp.exp(sc - m) for sc in ss]
            l = jnp.sum(es[0] + es[1] + es[2], axis=2, keepdims=True) + jnp.exp(sk - m)
            acc = jnp.zeros((rows, HD_B), F32)
            for j, e in enumerate(es):
                p = (e / l).astype(BF16).reshape(rows, Q_BLOCK)
                acc = acc + jnp.dot(p, v_ref[_win_kv_rows(n, j, nblk), _win_head_cols(kv)],
                                    preferred_element_type=F32)
            _win_unstack(o_ref, kv, acc)
            lse_ref[kv * GROUP:(kv + 1) * GROUP] = m + jnp.log(l)

    kv_w = KV_B * HD_B
    return pl.pallas_call(
        body, name="win_fwd", grid=(nblk,),
        in_specs=[pl.BlockSpec(memory_space=pltpu.SMEM), _bs((Q_BLOCK, H_B * HD_B), lambda n: (n, P_QB // (H_B * HD_B))),
                  _bs((s, kv_w), lambda n: (0, P_KB // kv_w)), _bs((s, kv_w), lambda n: (0, P_VB // kv_w)),
                  _bs((H_B, Q_BLOCK, SPAN), lambda n: (0, 0, 0))],
        out_specs=[_bs((Q_BLOCK, H_B * HD_B), lambda n: (n, 0)), _bs((H_B, Q_BLOCK, 1), lambda n: (0, n, 0))],
        out_shape=[jax.ShapeDtypeStruct((s, H_B * HD_B), F32), jax.ShapeDtypeStruct((H_B, s, 1), F32)],
        compiler_params=_arb(1),
    )(sinks, proj_b, proj_b, proj_b, bias)


def _win_bwd(proj_b, bias, sinks, do_b, lse):
    s = proj_b.shape[0]
    nblk = s // Q_BLOCK
    rows = GROUP * Q_BLOCK
    spad = s + 2 * WINDOW

    def body(sink_ref, q_ref, k_ref, v_ref, bias_ref, do_ref, lse_ref, dq_ref, dk_ref, dv_ref, db_ref, dsk_ref):
        n = pl.program_id(0)

        @pl.when(n == 0)
        def _():
            dk_ref[...] = jnp.zeros(dk_ref.shape, F32)
            dv_ref[...] = jnp.zeros(dv_ref.shape, F32)
            db_ref[...] = jnp.zeros(db_ref.shape, F32)
            dsk_ref[...] = jnp.zeros(dsk_ref.shape, F32)

        for kv in range(KV_B):
            heads = slice(kv * GROUP, (kv + 1) * GROUP)
            sk = _win_sink(sink_ref, kv)
            q = _win_stack(q_ref, kv)
            dob = _win_stack(do_ref, kv)
            lse_v = lse_ref[heads]
            ss = _win_scores(q, k_ref, kv, bias_ref, n, nblk)
            ps = [jnp.exp(sc - lse_v) for sc in ss]
            dps = [lax.dot_general(dob, v_ref[_win_kv_rows(n, j, nblk), _win_head_cols(kv)], NT,
                                   preferred_element_type=F32).reshape(GROUP, Q_BLOCK, Q_BLOCK) for j in range(3)]
            delta = jnp.sum(ps[0] * dps[0] + ps[1] * dps[1] + ps[2] * dps[2], axis=2, keepdims=True)
            dq = jnp.zeros((rows, HD_B), F32)
            for j in range(3):
                ds = ps[j] * (dps[j] - delta)
                db_ref[heads, :, j * Q_BLOCK:(j + 1) * Q_BLOCK] += ds
                dsb = (ds * WIN_SCALE).astype(BF16).reshape(rows, Q_BLOCK)
                dq = dq + jnp.dot(dsb, k_ref[_win_kv_rows(n, j, nblk), _win_head_cols(kv)],
                                  preferred_element_type=F32)
                krows = pl.ds(pl.multiple_of((n + j) * Q_BLOCK, Q_BLOCK), Q_BLOCK)
                dk_ref[krows, _win_head_cols(kv)] += lax.dot_general(dsb, q, TN, preferred_element_type=F32)
                dv_ref[krows, _win_head_cols(kv)] += lax.dot_general(
                    ps[j].astype(BF16).reshape(rows, Q_BLOCK), dob, TN, preferred_element_type=F32)
            dsk_ref[heads] += -(jnp.exp(sk - lse_v) * delta)
            _win_unstack(dq_ref, kv, dq)

    kv_w = KV_B * HD_B
    qspec = _bs((Q_BLOCK, H_B * HD_B), lambda n: (n, 0))
    kacc = _bs((spad, kv_w), lambda n: (0, 0))
    return pl.pallas_call(
        body, name="win_bwd", grid=(nblk,),
        in_specs=[pl.BlockSpec(memory_space=pltpu.SMEM), _bs((Q_BLOCK, H_B * HD_B), lambda n: (n, P_QB // (H_B * HD_B))),
                  _bs((s, kv_w), lambda n: (0, P_KB // kv_w)), _bs((s, kv_w), lambda n: (0, P_VB // kv_w)),
                  _bs((H_B, Q_BLOCK, SPAN), lambda n: (0, 0, 0)), qspec, _bs((H_B, Q_BLOCK, 1), lambda n: (0, n, 0))],
        out_specs=[qspec, kacc, kacc, _bs((H_B, Q_BLOCK, SPAN), lambda n: (0, 0, 0)),
                   _bs((H_B, Q_BLOCK, 1), lambda n: (0, 0, 0))],
        out_shape=[jax.ShapeDtypeStruct((s, H_B * HD_B), BF16), jax.ShapeDtypeStruct((spad, kv_w), F32),
                   jax.ShapeDtypeStruct((spad, kv_w), F32), jax.ShapeDtypeStruct((H_B, Q_BLOCK, SPAN), F32),
                   jax.ShapeDtypeStruct((H_B, Q_BLOCK, 1), F32)],
        compiler_params=_arb(1),
    )(sinks, proj_b, proj_b, proj_b, bias, do_b, lse)


def _win_param_grads(bucket, dbias, dsink_rows):
    def body(bk_ref, db_ref, ds_ref, o_ref):
        bk = bk_ref[...]
        dbv = db_ref[...]
        lane = lax.broadcasted_iota(jnp.int32, (1, LANES), 1)
        res = jnp.zeros((1, LANES), F32)
        for b in range(NUM_BUCKETS):
            tot = jnp.sum(jnp.sum(jnp.where(bk == b, dbv, 0.0), axis=1, keepdims=True), axis=0, keepdims=True)
            res = jnp.where(lane == b, tot, res)
        stot = jnp.sum(ds_ref[...], axis=0, keepdims=True)
        o_ref[...] = jnp.where(lane == NUM_BUCKETS, stot, res)

    return pl.pallas_call(
        body, name="win_param_grads", grid=(H_B,),
        in_specs=[_bs((Q_BLOCK, SPAN), lambda h: (0, 0)), _bs((None, Q_BLOCK, SPAN), lambda h: (h, 0, 0)),
                  _bs((None, Q_BLOCK, 1), lambda h: (h, 0, 0))],
        out_specs=_bs((None, 1, LANES), lambda h: (h, 0, 0)),
        out_shape=jax.ShapeDtypeStruct((H_B, 1, LANES), F32),
        compiler_params=_arb(1),
    )(bucket, dbias, dsink_rows)


def _gate_fwd(proj, o_a, o_b, ts=256):
    s = o_a.shape[0]
    wide = lambda cb: _bs((ts, D_MODEL), lambda i: (i, cb))

    def body(ga_ref, gb_ref, oa_ref, ob_ref, m_ref):
        m_ref[...] = (jax.nn.sigmoid(ga_ref[...]) * oa_ref[...]
                      + jax.nn.sigmoid(gb_ref[...]) * ob_ref[...]).astype(BF16)

    return pl.pallas_call(
        body, name="gate_fwd", grid=(s // ts,),
        in_specs=[wide(P_GA // D_MODEL), wide(P_GB // D_MODEL), wide(0), wide(0)],
        out_specs=wide(0), out_shape=jax.ShapeDtypeStruct((s, D_MODEL), BF16),
        compiler_params=_arb(1),
    )(proj, proj, o_a, o_b)


def _gate_bwd(dmixed, proj, o_a, o_b, ts=256):
    s = o_a.shape[0]
    wide = lambda cb: _bs((ts, D_MODEL), lambda i: (i, cb))

    def body(dm_ref, ga_ref, gb_ref, oa_ref, ob_ref, doa_ref, dob_ref, dga_ref, dgb_ref):
        dm = dm_ref[...]
        sa = jax.nn.sigmoid(ga_ref[...])
        sb = jax.nn.sigmoid(gb_ref[...])
        doa_ref[...] = dm * sa
        dob_ref[...] = (dm * sb).astype(BF16)
        dga_ref[...] = (dm * oa_ref[...] * (sa * (1.0 - sa))).astype(BF16)
        dgb_ref[...] = (dm * ob_ref[...] * (sb * (1.0 - sb))).astype(BF16)

    return pl.pallas_call(
        body, name="gate_bwd", grid=(s // ts,),
        in_specs=[wide(0), wide(P_GA // D_MODEL), wide(P_GB // D_MODEL), wide(0), wide(0)],
        out_specs=[wide(0)] * 4,
        out_shape=[jax.ShapeDtypeStruct((s, D_MODEL), F32), jax.ShapeDtypeStruct((s, D_MODEL), BF16),
                   jax.ShapeDtypeStruct((s, D_MODEL), BF16), jax.ShapeDtypeStruct((s, D_MODEL), BF16)],
        compiler_params=_arb(1),
    )(dmixed, proj, proj, o_a, o_b)


CONV_CHUNK = 512
N_SLAB = D_FF // LANES


def _shifted(ref, c, nchunks):
    r0 = c * CONV_CHUNK
    cur = ref[r0:r0 + CONV_CHUNK, :]
    row = lax.broadcasted_iota(jnp.int32, (8, LANES), 0)
    before = ref[r0 - 8:r0, :][7:8, :] if c > 0 else jnp.zeros((1, LANES), F32)
    after = ref[r0 + CONV_CHUNK:r0 + CONV_CHUNK + 8, :][0:1, :] if c < nchunks - 1 else jnp.zeros((1, LANES), F32)
    down = pltpu.roll(cur, 1, 0)
    up = pltpu.roll(cur, CONV_CHUNK - 1, 0)
    prev = jnp.concatenate([jnp.where(row == 0, before, down[:8]), down[8:]], axis=0)
    nxt = jnp.concatenate([up[:-8], jnp.where(row == 7, after, up[-8:])], axis=0)
    return prev, cur, nxt


def _conv_taps(ref, w_ref, b_ref, c, nchunks):
    prev, cur, nxt = _shifted(ref, c, nchunks)
    conv = prev * w_ref[0:1, :] + cur * w_ref[1:2, :] + nxt * w_ref[2:3, :] + b_ref[...]
    return conv, prev, cur, nxt


def _convffn_fwd(u, conv_w, conv_b):
    s = u.shape[0]
    nchunks = s // CONV_CHUNK

    def body(ug_ref, uv_ref, wg_ref, wv_ref, bg_ref, bv_ref, f_ref):
        for c in range(nchunks):
            cg = _conv_taps(ug_ref, wg_ref, bg_ref, c, nchunks)[0]
            cv = _conv_taps(uv_ref, wv_ref, bv_ref, c, nchunks)[0]
            f_ref[c * CONV_CHUNK:(c + 1) * CONV_CHUNK, :] = (cg * jax.nn.sigmoid(cg) * cv).astype(BF16)

    slab = lambda off: _bs((s, LANES), lambda j: (0, off + j))
    wsl = lambda off: _bs((3, LANES), lambda j: (0, off + j))
    bsl = lambda off: _bs((1, LANES), lambda j: (0, off + j))
    return pl.pallas_call(
        body, name="convffn_fwd", grid=(N_SLAB,),
        in_specs=[slab(0), slab(N_SLAB), wsl(0), wsl(N_SLAB), bsl(0), bsl(N_SLAB)],
        out_specs=slab(0), out_shape=jax.ShapeDtypeStruct((s, D_FF), BF16),
        compiler_params=_arb(1),
    )(u, u, conv_w, conv_w, conv_b, conv_b)


def _convffn_bwd(u, conv_w, conv_b, df):
    s = u.shape[0]
    nchunks = s // CONV_CHUNK

    def body(ug_ref, uv_ref, wg_ref, wv_ref, bg_ref, bv_ref, df_ref, du_ref, dw_ref, db_ref, dcg_ref, dcv_ref):
        dwg = [jnp.zeros((1, LANES), F32) for _ in range(3)]
        dwv = [jnp.zeros((1, LANES), F32) for _ in range(3)]
        dbg = jnp.zeros((1, LANES), F32)
        dbv = jnp.zeros((1, LANES), F32)
        for c in range(nchunks):
            rows = slice(c * CONV_CHUNK, (c + 1) * CONV_CHUNK)
            cg, gp, gc, gn = _conv_taps(ug_ref, wg_ref, bg_ref, c, nchunks)
            cv, vp, vc, vn = _conv_taps(uv_ref, wv_ref, bv_ref, c, nchunks)
            dfv = df_ref[rows, :]
            sg = jax.nn.sigmoid(cg)
            dcg = dfv * cv * (sg * (1.0 + cg * (1.0 - sg)))
            dcv = dfv * (cg * sg)
            dcg_ref[rows, :] = dcg
            dcv_ref[rows, :] = dcv
            for t, (tg, tv) in enumerate(((gp, vp), (gc, vc), (gn, vn))):
                dwg[t] = dwg[t] + jnp.sum(tg * dcg, axis=0, keepdims=True)
                dwv[t] = dwv[t] + jnp.sum(tv * dcv, axis=0, keepdims=True)
            dbg = dbg + jnp.sum(dcg, axis=0, keepdims=True)
            dbv = dbv + jnp.sum(dcv, axis=0, keepdims=True)
        for t in range(3):
            dw_ref[0, t:t + 1, :] = dwg[t]
            dw_ref[1, t:t + 1, :] = dwv[t]
        db_ref[0] = dbg
        db_ref[1] = dbv
        for half, (dc_ref, w_ref) in enumerate(((dcg_ref, wg_ref), (dcv_ref, wv_ref))):
            for c in range(nchunks):
                prev, cur, nxt = _shifted(dc_ref, c, nchunks)
                du = nxt * w_ref[0:1, :] + cur * w_ref[1:2, :] + prev * w_ref[2:3, :]
                du_ref[half, c * CONV_CHUNK:(c + 1) * CONV_CHUNK, :] = du.astype(BF16)

    slab = lambda off: _bs((s, LANES), lambda j: (0, off + j))
    wsl = lambda off: _bs((3, LANES), lambda j: (0, off + j))
    bsl = lambda off: _bs((1, LANES), lambda j: (0, off + j))
    return pl.pallas_call(
        body, name="convffn_bwd", grid=(N_SLAB,),
        in_specs=[slab(0), slab(N_SLAB), wsl(0), wsl(N_SLAB), bsl(0), bsl(N_SLAB), slab(0)],
        out_specs=[_bs((2, s, LANES), lambda j: (0, 0, j)), _bs((2, 3, LANES), lambda j: (0, 0, j)),
                   _bs((2, 1, LANES), lambda j: (0, 0, j))],
        out_shape=[jax.ShapeDtypeStruct((2, s, D_FF), BF16), jax.ShapeDtypeStruct((2, 3, D_FF), F32),
                   jax.ShapeDtypeStruct((2, 1, D_FF), F32)],
        scratch_shapes=[pltpu.VMEM((s, LANES), F32), pltpu.VMEM((s, LANES), F32)],
        compiler_params=_arb(1),
    )(u, u, conv_w, conv_w, conv_b, conv_b, df)


def _row_tile(rows, limit=512):
    best = rows
    for t in range(8, min(rows, limit) + 1, 8):
        if rows % t == 0:
            best = t
    return best if rows % 8 == 0 else rows


ADAM_C1 = 1.0 - ADAM_B1 ** ADAM_STEP
ADAM_C2 = 1.0 - ADAM_B2 ** ADAM_STEP


def _adamw_math(w, gv, m, v):
    nm = ADAM_B1 * m + (1.0 - ADAM_B1) * gv
    nv = ADAM_B2 * v + (1.0 - ADAM_B2) * (gv * gv)
    m_hat = nm / ADAM_C1
    v_hat = nv / ADAM_C2
    return -ADAM_LR * (m_hat / (jnp.sqrt(v_hat) + ADAM_EPS) + ADAM_WD * w), nm, nv


def _adamw_halves(name, core, w, mine, theirs, m, v):
    half, cols = mine.shape
    tr = _row_tile(half)
    nr = half // tr

    def body(core_ref, w_ref, mine_ref, theirs_ref, m_ref, v_ref, g_ref, d_ref, nm_ref, nv_ref):
        gv = jnp.where(pl.program_id(0) == core_ref[0], mine_ref[...], theirs_ref[...])
        g_ref[...] = gv
        d_ref[...], nm_ref[...], nv_ref[...] = _adamw_math(w_ref[...], gv, m_ref[...], v_ref[...])

    full = pl.BlockSpec((tr, cols), lambda hf, r, cr: (hf * nr + r, 0))
    part = pl.BlockSpec((tr, cols), lambda hf, r, cr: (r, 0))
    return pl.pallas_call(
        body, name=name,
        grid_spec=pltpu.PrefetchScalarGridSpec(num_scalar_prefetch=1, grid=(2, nr),
                                               in_specs=[full, part, part, full, full], out_specs=[full] * 4),
        out_shape=[jax.ShapeDtypeStruct((2 * half, cols), F32)] * 4, compiler_params=_arb(2),
    )(core, w, mine, theirs, m, v)


def _adamw(name, w, g, m, v):
    rows, cols = w.shape
    tr = _row_tile(rows)

    def body(w_ref, g_ref, m_ref, v_ref, d_ref, nm_ref, nv_ref):
        d_ref[...], nm_ref[...], nv_ref[...] = _adamw_math(w_ref[...], g_ref[...], m_ref[...], v_ref[...])

    spec = _bs((tr, cols), lambda i: (i, 0))
    return pl.pallas_call(
        body, name=name, grid=(rows // tr,), in_specs=[spec] * 4, out_specs=[spec] * 3,
        out_shape=[jax.ShapeDtypeStruct((rows, cols), F32)] * 3, compiler_params=_arb(1),
    )(w, g, m, v)


ANY = pl.BlockSpec(memory_space=pl.ANY)


def _mesh_pos():
    return lax.axis_index("x"), lax.axis_index("y"), lax.axis_index("c")


def _other_chips(x, y):
    return [(1 - x, y), (x, 1 - y), (1 - x, 1 - y)]


def _allgather_weights(shards, split):
    n = len(shards)

    def body(*refs):
        w_refs, o_refs = refs[:n], refs[n:2 * n]
        send_sems, recv_sems, fsend_sems, frecv_sems = refs[2 * n:]
        x, y, c = _mesh_pos()
        p = 2 * x + y
        chips = _other_chips(x, y)

        def piece(i, chip_index, core):
            return o_refs[i].at[chip_index, core] if split[i] else o_refs[i].at[chip_index]

        def remote(src, dst, ssem, rsem, to):
            return pltpu.make_async_remote_copy(src_ref=src, dst_ref=dst, send_sem=ssem, recv_sem=rsem,
                                                device_id=to, device_id_type=MESH)

        sends = []
        for i in range(n):
            src = w_refs[i].at[c] if split[i] else w_refs[i]
            for k, chip in enumerate(chips):
                cp = remote(src, piece(i, p, c), send_sems.at[3 * i + k], recv_sems.at[3 * i + k], (*chip, c))
                cp.start()
                sends.append(cp)
        for i in range(n):
            for k, chip in enumerate(chips):
                pk = 2 * chip[0] + chip[1]
                landed = piece(i, pk, c)
                remote(landed, landed, send_sems.at[3 * i + k], recv_sems.at[3 * i + k], (*chip, c)).wait_recv()
                if split[i]:
                    fw = remote(landed, landed, fsend_sems.at[3 * i + k], frecv_sems.at[3 * i + k], (x, y, 1 - c))
                    fw.start()
                    sends.append(fw)
        for i in range(n):
            if split[i]:
                for k, chip in enumerate(chips):
                    pk = 2 * chip[0] + chip[1]
                    theirs = piece(i, pk, 1 - c)
                    remote(theirs, theirs, fsend_sems.at[3 * i + k], frecv_sems.at[3 * i + k],
                           (x, y, 1 - c)).wait_recv()
        for cp in sends:
            cp.wait_send()

    return pl.pallas_call(
        body, name="allgather_weights",
        in_specs=[ANY] * n, out_specs=[ANY] * n,
        out_shape=[jax.ShapeDtypeStruct((4,) + w.shape, w.dtype) for w in shards],
        scratch_shapes=[pltpu.SemaphoreType.DMA((3 * n,)), pltpu.SemaphoreType.DMA((3 * n,)),
                        pltpu.SemaphoreType.DMA((3 * n,)), pltpu.SemaphoreType.DMA((3 * n,))],
    )(*shards)


def _rs_pair_exchange(name, grads):
    n = len(grads)

    def body(*refs):
        g_refs, o_refs = refs[:n], refs[n:2 * n]
        send_sems, recv_sems = refs[2 * n:]
        x, y, c = _mesh_pos()
        cps = []
        for i in range(n):
            cp = pltpu.make_async_remote_copy(
                src_ref=g_refs[i].at[:, 1 - c], dst_ref=o_refs[i],
                send_sem=send_sems.at[i], recv_sem=recv_sems.at[i], device_id=(x, y, 1 - c), device_id_type=MESH)
            cp.start()
            cps.append(cp)
        for cp in cps:
            cp.wait()

    return pl.pallas_call(
        body, name=name, in_specs=[ANY] * n, out_specs=[ANY] * n,
        out_shape=[jax.ShapeDtypeStruct((4,) + g.shape[2:], F32) for g in grads],
        scratch_shapes=[pltpu.SemaphoreType.DMA((n,)), pltpu.SemaphoreType.DMA((n,))],
    )(*grads)


def _rs_pair_add(name, core, g, recv):
    _, half, cols = recv.shape
    tr = _row_tile(half)
    nr = half // tr

    def body(core_ref, g_ref, r_ref, o_ref):
        o_ref[...] = (g_ref[...] + r_ref[...]).astype(BF16)

    return pl.pallas_call(
        body, name=name,
        grid_spec=pltpu.PrefetchScalarGridSpec(
            num_scalar_prefetch=1, grid=(4, nr),
            in_specs=[pl.BlockSpec((None, None, tr, cols), lambda q, r, cr: (q, cr[0], r, 0)),
                      pl.BlockSpec((None, tr, cols), lambda q, r, cr: (q, r, 0))],
            out_specs=pl.BlockSpec((None, tr, cols), lambda q, r, cr: (q, r, 0))),
        out_shape=jax.ShapeDtypeStruct((4, half, cols), BF16),
        compiler_params=_arb(2),
    )(core, g, recv)


def _rs_final_add(name, chip, pair, recv):
    _, half, cols = pair.shape
    tr = _row_tile(half)

    def body(chip_ref, p_ref, r_ref, o_ref):
        o_ref[...] = ((p_ref[...].astype(F32) + r_ref[0].astype(F32)) + r_ref[1].astype(F32)) + r_ref[2].astype(F32)

    return pl.pallas_call(
        body, name=name,
        grid_spec=pltpu.PrefetchScalarGridSpec(
            num_scalar_prefetch=1, grid=(half // tr,),
            in_specs=[pl.BlockSpec((None, tr, cols), lambda r, ch: (ch[0], r, 0)),
                      pl.BlockSpec((3, tr, cols), lambda r, ch: (0, r, 0))],
            out_specs=pl.BlockSpec((tr, cols), lambda r, ch: (r, 0))),
        out_shape=jax.ShapeDtypeStruct((half, cols), F32),
        compiler_params=_arb(1),
    )(chip, pair, recv)


def _rs_pair_share(halves):
    n = len(halves)

    def body(*refs):
        h_refs, o_refs = refs[:n], refs[n:2 * n]
        send_sems, recv_sems = refs[2 * n:]
        x, y, c = _mesh_pos()
        cps = []
        for i in range(n):
            cp = pltpu.make_async_remote_copy(src_ref=h_refs[i], dst_ref=o_refs[i], send_sem=send_sems.at[i],
                                              recv_sem=recv_sems.at[i], device_id=(x, y, 1 - c), device_id_type=MESH)
            cp.start()
            cps.append(cp)
        for cp in cps:
            cp.wait()

    return pl.pallas_call(
        body, name="rs_pair_share", in_specs=[ANY] * n, out_specs=[ANY] * n,
        out_shape=[jax.ShapeDtypeStruct(h.shape, F32) for h in halves],
        scratch_shapes=[pltpu.SemaphoreType.DMA((n,)), pltpu.SemaphoreType.DMA((n,))],
    )(*halves)


HBM = pl.BlockSpec(memory_space=pltpu.HBM)
SEM = pl.BlockSpec(memory_space=pltpu.SEMAPHORE)


class _SplitExchange:
    def __init__(self, name, srcs, land_shapes, src_of, dst_of, arrive_of):
        self.name, self.srcs, self.land_shapes = name, list(srcs), list(land_shapes)
        self.src_of, self.dst_of, self.arrive_of = src_of, dst_of, arrive_of

    def _copies(self, src_refs, land_refs, send_sems, recv_sems):
        x, y, c = _mesh_pos()
        p = 2 * x + y
        out = []
        for i, (src, land) in enumerate(zip(src_refs, land_refs)):
            for k, chip in enumerate(_other_chips(x, y)):
                pk = 2 * chip[0] + chip[1]
                sems = dict(send_sem=send_sems.at[3 * i + k], recv_sem=recv_sems.at[3 * i + k],
                            device_id=(*chip, c), device_id_type=MESH)
                sent = pltpu.make_async_remote_copy(src_ref=self.src_of(src, k, p, pk),
                                                    dst_ref=self.dst_of(land, k, p, pk), **sems)
                here = self.arrive_of(land, k, p, pk)
                out.append((sent, pltpu.make_async_remote_copy(src_ref=here, dst_ref=here, **sems)))
        return out

    def start(self):
        n = len(self.srcs)

        def body(*refs):
            for sent, _ in self._copies(refs[:n], refs[n:2 * n], refs[2 * n], refs[2 * n + 1]):
                sent.start()
            refs[-1][...] = jnp.zeros((8, LANES), F32)

        lands = [lax.empty(shape, src.dtype) for shape, src in zip(self.land_shapes, self.srcs)]
        operands = [pltpu.with_memory_space_constraint(a, pltpu.HBM) for a in self.srcs + lands]
        outs = pl.pallas_call(
            body, name=self.name + "_start",
            out_shape=(pltpu.SemaphoreType.DMA((3 * n,)), pltpu.SemaphoreType.DMA((3 * n,)),
                       *[pltpu.HBM(a.shape, a.dtype) for a in operands], jax.ShapeDtypeStruct((8, LANES), F32)),
            in_specs=[HBM] * (2 * n), out_specs=(SEM, SEM, *[HBM] * (2 * n), pl.BlockSpec(memory_space=pltpu.VMEM)),
            input_output_aliases={j: 2 + j for j in range(2 * n)},
            compiler_params=pltpu.CompilerParams(has_side_effects=pltpu.SideEffectType.DATAFLOW_SIDE_EFFECTING),
        )(*operands)
        self._sems, self._thru = outs[:2], list(outs[2:2 + 2 * n])
        return outs[-1]

    def wait(self, after):
        n = len(self.srcs)

        def body(*refs):
            for sent, arrived in self._copies(refs[:n], refs[n:2 * n], refs[2 * n], refs[2 * n + 1]):
                sent.wait_send()
                arrived.wait_recv()

        outs = pl.pallas_call(
            body, name=self.name + "_wait",
            out_shape=tuple(pltpu.HBM(a.shape, a.dtype) for a in self._thru),
            in_specs=[HBM] * (2 * n) + [SEM, SEM, ANY], out_specs=tuple([HBM] * (2 * n)),
            input_output_aliases={j: j for j in range(2 * n)},
            compiler_params=pltpu.CompilerParams(has_side_effects=pltpu.SideEffectType.DATAFLOW_SIDE_EFFECTING),
        )(*self._thru, *self._sems, after)
        return list(outs[n:])


def _small_allreduce(buf):
    rows = buf.shape[0]

    def body(in_ref, out_ref, gather_ref, send_sems, recv_sems):
        x, y, c = _mesh_pos()
        me = 4 * x + 2 * y + c
        gather_ref[me] = in_ref[...]
        cps = []
        for j in range(1, 8):
            peer = (x ^ (j >> 2), y ^ ((j >> 1) & 1), c ^ (j & 1))
            cp = pltpu.make_async_remote_copy(src_ref=in_ref, dst_ref=gather_ref.at[me], send_sem=send_sems.at[j - 1],
                                              recv_sem=recv_sems.at[j - 1], device_id=peer, device_id_type=MESH)
            cp.start()
            cps.append(cp)
        for j in range(1, 8):
            peer_id = 4 * (x ^ (j >> 2)) + 2 * (y ^ ((j >> 1) & 1)) + (c ^ (j & 1))
            slot = gather_ref.at[peer_id]
            pltpu.make_async_remote_copy(src_ref=slot, dst_ref=slot, send_sem=send_sems.at[j - 1],
                                         recv_sem=recv_sems.at[j - 1], device_id=(x, y, c),
                                         device_id_type=MESH).wait_recv()
        for cp in cps:
            cp.wait_send()
        tot = gather_ref[0]
        for d in range(1, 8):
            tot = tot + gather_ref[d]
        out_ref[...] = tot

    return pl.pallas_call(
        body, name="small_allreduce",
        in_specs=[pl.BlockSpec(memory_space=pltpu.VMEM)], out_specs=pl.BlockSpec(memory_space=pltpu.VMEM),
        out_shape=jax.ShapeDtypeStruct(buf.shape, F32),
        scratch_shapes=[pltpu.VMEM((8, rows, LANES), F32), pltpu.SemaphoreType.DMA((7,)),
                        pltpu.SemaphoreType.DMA((7,))],
    )(buf)


def _pack(parts, rows):
    flat = jnp.concatenate([p.reshape(-1).astype(F32) for p in parts])
    return jnp.pad(flat, (0, rows * LANES - flat.shape[0])).reshape(rows, LANES)


def _pack_rows(parts):
    n = sum(math.prod(p.shape) for p in parts)
    return pl.cdiv(pl.cdiv(n, LANES), 8) * 8


def _unpack(buf, shapes):
    flat = buf.reshape(-1)
    out, off = [], 0
    for shp in shapes:
        size = math.prod(shp)
        out.append(flat[off:off + size].reshape(shp))
        off += size
    return out


def _pad_w_in(w):
    z = jnp.zeros((w.shape[0], 64), w.dtype)
    return jnp.concatenate([w[:, 448:1472], w[:, 1984:3008], w[:, 3008:4032], w[:, 0:256], w[:, 1472:1728],
                            w[:, 1728:1984], w[:, 256:384], w[:, 384:448], z], axis=1)


def _unpad_w_in(p):
    return jnp.concatenate([p[:, P_QLAT:P_QLAT + 256], p[:, P_CKV:P_CKV + 128], p[:, P_KR:P_KR + 64],
                            p[:, P_QB:P_QB + 1024], p[:, P_KB:P_KB + 256], p[:, P_VB:P_VB + 256],
                            p[:, P_GA:P_GA + 1024], p[:, P_GB:P_GB + 1024]], axis=1)


def _col_shards(w):
    r, c4 = w.shape
    return w.reshape(r, 4, c4 // 4).transpose(1, 0, 2)


def _local_step(x, positions, target, norm1_g, w_in_p, q_a_norm_g, wq, kv_a_norm_g, wkv, rel_bias, sinks,
                late_weights, norm2_g, conv_w, conv_b, final_norm_g, early_grads=None, last_grads=None):
    s = x.shape[0]
    half = QK_ROPE // 2
    inv_freq = jnp.asarray(np.float32(ROPE_THETA) ** (-np.arange(half, dtype=np.float32) / np.float32(half)))
    ang = positions.astype(F32)[:, None] * inv_freq[None, :]
    cos, sin = jnp.cos(ang), jnp.sin(ang)
    z64 = jnp.zeros((s, 64), F32)
    cos_t = jnp.concatenate([cos, cos, z64], axis=1)
    sin_t = jnp.concatenate([-sin, sin, z64], axis=1)
    bucket = _t5_bucket_table()
    sinks1 = sinks.reshape(H_B)

    h1, rstd1 = _rmsnorm_fwd("norm1_fwd", x, norm1_g, D_MODEL, 0)
    proj, proj_b = _matmul("proj", h1, w_in_p, out_shape=(s, W_IN_PAD), out_dtype=F32, grid=(s // MM_ROWS, W_IN_PAD // 1024, 1),
                           a_spec=_bs((MM_ROWS, D_MODEL), lambda i, j, k: (i, 0)), b_spec=_bs((D_MODEL, 1024), lambda i, j, k: (0, j)),
                           o_spec=_bs((MM_ROWS, 1024), lambda i, j, k: (i, j)), contract=NN, bf16_copy=True)
    qn, cn, rstd_q, rstd_c = _lat_norms(proj, q_a_norm_g, kv_a_norm_g)
    q = _q_heads(qn, wq, cos_t, sin_t)
    k, v = _kv_heads(cn, wkv, proj, cos_t, sin_t)
    o_a, lse_a = _mla_fwd(q, k, v)

    bias = _win_bias(bucket, rel_bias)
    o_b, lse_b = _win_fwd(proj_b, bias, sinks1)

    mixed = _gate_fwd(proj, o_a, o_b)
    w_out, w_up, w_down = late_weights(mixed)
    row512 = lambda w: _bs((MM_ROWS, w), lambda i, j, k: (i, 0))
    whole = lambda r, c: _bs((r, c), lambda i, j, k: (0, 0))
    x1 = _matmul("attn_out", mixed, w_out, out_shape=(s, D_MODEL), out_dtype=F32, grid=(s // MM_ROWS, 1, 1),
                 a_spec=row512(D_MODEL), b_spec=whole(D_MODEL, D_MODEL), o_spec=row512(D_MODEL), contract=NN, add=x)
    h2, rstd2 = _rmsnorm_fwd("norm2_fwd", x1, norm2_g, D_MODEL, 0)
    u = _matmul("ffn_up", h2, w_up, out_shape=(s, 2 * D_FF), out_dtype=F32, grid=(s // MM_ROWS, 4, 1),
                a_spec=_bs((MM_ROWS, D_MODEL), lambda i, j, k: (i, 0)), b_spec=_bs((D_MODEL, D_FF // 2), lambda i, j, k: (0, j)),
                o_spec=_bs((MM_ROWS, D_FF // 2), lambda i, j, k: (i, j)), contract=NN)
    f = _convffn_fwd(u, conv_w, conv_b)
    x2 = _matmul("ffn_down", f, w_down, out_shape=(s, D_MODEL), out_dtype=F32, grid=(s // MM_ROWS, 1, 1),
                 a_spec=row512(D_FF), b_spec=whole(D_FF, D_MODEL), o_spec=row512(D_MODEL), contract=NN, add=x1)
    loss, dx2, d_final_g, dx2_b = _final_loss(x2, target, final_norm_g.reshape(1, D_MODEL))
    tk = min(s, DW_ROWS)

    df = _matmul("ffn_down_dx", dx2_b, w_down, out_shape=(s, D_FF), out_dtype=F32, grid=(s // MM_ROWS, 2, 1),
                 a_spec=row512(D_MODEL), b_spec=_bs((D_FF // 2, D_MODEL), lambda i, j, k: (j, 0)),
                 o_spec=_bs((MM_ROWS, D_FF // 2), lambda i, j, k: (i, j)), contract=NT)
    d_w_down = _matmul("ffn_down_dw", f, dx2_b, out_shape=(D_FF, D_MODEL), out_dtype=F32, grid=(2, 1, s // tk),
                       a_spec=_bs((tk, D_FF // 2), lambda i, j, k: (k, i)), b_spec=_bs((tk, D_MODEL), lambda i, j, k: (k, 0)),
                       o_spec=_bs((D_FF // 2, D_MODEL), lambda i, j, k: (i, 0)), contract=TN)
    du, d_conv_w2, d_conv_b2 = _convffn_bwd(u, conv_w, conv_b, df)
    kc = D_FF // 2
    dh2 = _matmul("ffn_up_dx", du, w_up, out_shape=(s, D_MODEL), out_dtype=F32, grid=(s // 1024, 1, 4),
                  a_spec=_bs((None, 1024, kc), lambda i, j, k: (k // 2, i, k % 2)),
                  b_spec=_bs((D_MODEL, kc), lambda i, j, k: (0, k)),
                  o_spec=_bs((1024, D_MODEL), lambda i, j, k: (i, 0)), contract=NT)
    d_w_up = _matmul("ffn_up_dw", h2, du, out_shape=(D_MODEL, 2 * D_FF), out_dtype=F32, grid=(1, 4, s // tk),
                     a_spec=_bs((tk, D_MODEL), lambda i, j, k: (k, 0)),
                     b_spec=_bs((None, tk, kc), lambda i, j, k: (j // 2, k, j % 2)),
                     o_spec=_bs((D_MODEL, kc), lambda i, j, k: (0, j)), contract=TN)
    dx1, d_norm2_g, dx1_b = _rmsnorm_bwd("norm2_bwd", dh2, x1, rstd2, norm2_g, D_MODEL, 0, F32, res=dx2, bf16_copy=True)

    d_w_out = _matmul("attn_out_dw", mixed, dx1_b, out_shape=(D_MODEL, D_MODEL), out_dtype=F32, grid=(1, 1, s // tk),
                      a_spec=_bs((tk, D_MODEL), lambda i, j, k: (k, 0)), b_spec=_bs((tk, D_MODEL), lambda i, j, k: (k, 0)),
                      o_spec=whole(D_MODEL, D_MODEL), contract=TN)
    if early_grads is not None:
        token = early_grads(d_w_out, d_w_up, d_w_down)
        if token is not None:
            sinks1 = sinks1 + token[0, :H_B]
    dmixed = _matmul("attn_out_dx", dx1_b, w_out, out_shape=(s, D_MODEL), out_dtype=F32, grid=(s // MM_ROWS, 1, 1),
                     a_spec=row512(D_MODEL), b_spec=whole(D_MODEL, D_MODEL), o_spec=row512(D_MODEL), contract=NT)
    do_a, do_b, d_ga, d_gb = _gate_bwd(dmixed, proj, o_a, o_b)

    d_qb, dk_pad, dv_pad, dbias, dsink_rows = _win_bwd(proj_b, bias, sinks1, do_b, lse_b)
    wp = _win_param_grads(bucket, dbias, dsink_rows)[:, 0, :]
    d_rel_bias = wp[:, :NUM_BUCKETS].T
    d_sinks = wp[:, NUM_BUCKETS].reshape(1, H_B)
    d_kb = dk_pad[WINDOW:WINDOW + s].astype(BF16)
    d_vb = dv_pad[WINDOW:WINDOW + s].astype(BF16)

    dq, dk, dv = _mla_bwd(q, k, v, do_a, o_a, lse_a)
    dq_pre, dkv_pre, d_kr = _mla_bwd_prep(dq, dk, dv, cos_t, sin_t)
    th = min(s, HEAD_ROWS)
    hgrid = (s // th, 1, H_A)
    hblock = _bs((None, th, HEAD_PAD), lambda i, j, k: (k, i, 0))
    hrows = lambda w: _bs((th, w), lambda i, j, k: (i, 0))
    dqn = _matmul("q_up_dx", dq_pre, wq, out_shape=(s, Q_LORA), out_dtype=F32, grid=hgrid, a_spec=hblock,
                  b_spec=_bs((None, Q_LORA, HEAD_PAD), lambda i, j, k: (k, 0, 0)), o_spec=hrows(Q_LORA), contract=NT)
    dcn = _matmul("kv_up_dx", dkv_pre, wkv, out_shape=(s, KV_LORA), out_dtype=F32, grid=hgrid, a_spec=hblock,
                  b_spec=_bs((None, KV_LORA, HEAD_PAD), lambda i, j, k: (k, 0, 0)), o_spec=hrows(KV_LORA), contract=NT)
    wgrid = (H_A, 1, s // th)
    d_wq = _matmul("q_up_dw", qn, dq_pre, out_shape=(H_A, Q_LORA, HEAD_PAD), out_dtype=F32, grid=wgrid,
                   a_spec=_bs((th, Q_LORA), lambda i, j, k: (k, 0)), b_spec=_bs((None, th, HEAD_PAD), lambda i, j, k: (i, k, 0)),
                   o_spec=_bs((None, Q_LORA, HEAD_PAD), lambda i, j, k: (i, 0, 0)), contract=TN)
    d_wkv = _matmul("kv_up_dw", cn, dkv_pre, out_shape=(H_A, KV_LORA, HEAD_PAD), out_dtype=F32, grid=wgrid,
                    a_spec=_bs((th, KV_LORA), lambda i, j, k: (k, 0)), b_spec=_bs((None, th, HEAD_PAD), lambda i, j, k: (i, k, 0)),
                    o_spec=_bs((None, KV_LORA, HEAD_PAD), lambda i, j, k: (i, 0, 0)), contract=TN)
    d_qlat, d_gq = _rmsnorm_bwd("q_norm_bwd", dqn, proj, rstd_q, q_a_norm_g, Q_LORA, P_QLAT // Q_LORA, BF16)
    d_ckv, d_gkv = _rmsnorm_bwd("kv_norm_bwd", dcn, proj, rstd_c, kv_a_norm_g, KV_LORA, P_CKV // KV_LORA, BF16)

    dproj = jnp.concatenate([d_qb, d_ga, d_gb, d_qlat, d_kb, d_vb, d_ckv, d_kr], axis=1)
    d_w_in_p = _matmul("proj_dw", h1, dproj, out_shape=(D_MODEL, W_IN_PAD), out_dtype=F32, grid=(1, W_IN_PAD // 1024, s // tk),
                       a_spec=_bs((tk, D_MODEL), lambda i, j, k: (k, 0)), b_spec=_bs((tk, 1024), lambda i, j, k: (k, j)),
                       o_spec=_bs((D_MODEL, 1024), lambda i, j, k: (0, j)), contract=TN)
    token = last_grads(d_w_in_p, d_wq, d_wkv) if last_grads is not None else None
    dh1 = _matmul("proj_dx", dproj, w_in_p, out_shape=(s, D_MODEL), out_dtype=F32, grid=(s // 1024, 1, W_IN_PAD // 1024),
                  a_spec=_bs((1024, 1024), lambda i, j, k: (i, k)), b_spec=_bs((D_MODEL, 1024), lambda i, j, k: (0, k)),
                  o_spec=_bs((1024, D_MODEL), lambda i, j, k: (i, 0)), contract=NT, after=token)
    dx, d_norm1_g = _rmsnorm_bwd("norm1_bwd", dh1, x, rstd1, norm1_g, D_MODEL, 0, F32, res=dx1)

    grads = dict(
        norm1_g=d_norm1_g, w_in_p=d_w_in_p, q_a_norm_g=d_gq, wq=d_wq, kv_a_norm_g=d_gkv, wkv=d_wkv,
        rel_bias=d_rel_bias, sinks=d_sinks, w_out=d_w_out, norm2_g=d_norm2_g, w_up=d_w_up,
        conv_w=jnp.concatenate([d_conv_w2[0], d_conv_w2[1]], axis=1),
        conv_b=jnp.concatenate([d_conv_b2[0], d_conv_b2[1]], axis=1),
        w_down=d_w_down, final_norm_g=d_final_g.reshape(D_MODEL))
    return loss, dx, grads


def _wq_heads(w_q_b):
    w = w_q_b.reshape(Q_LORA, H_A, QK_NOPE + QK_ROPE).transpose(1, 0, 2)
    return jnp.pad(w, ((0, 0), (0, 0), (0, HEAD_PAD - QK_NOPE - QK_ROPE)))


def _wq_unheads(d_wq):
    return d_wq[:, :, :QK_NOPE + QK_ROPE].transpose(1, 0, 2).reshape(Q_LORA, H_A * (QK_NOPE + QK_ROPE))


def _wkv_heads(w_kv_b):
    return w_kv_b.reshape(KV_LORA, H_A, QK_NOPE + V_DIM).transpose(1, 0, 2)


def _wkv_unheads(d_wkv):
    return d_wkv.transpose(1, 0, 2).reshape(KV_LORA, H_A * (QK_NOPE + V_DIM))


SMALL = ("norm1_g", "q_a_norm_g", "kv_a_norm_g", "rel_bias", "sinks", "norm2_g", "conv_b", "final_norm_g")
FIRST = ("w_in", "w_q_b", "w_kv_b")
LATER = ("w_out", "w_up", "w_down")
BIG = FIRST + LATER


def kernel(x, positions, norm1_g, w_in, q_a_norm_g, w_q_b, kv_a_norm_g, w_kv_b, rel_bias, sinks, w_out, norm2_g, w_up, conv_w, conv_b, w_down, final_norm_g, loss_target, m_norm1_g, m_w_in, m_q_a_norm_g, m_w_q_b, m_kv_a_norm_g, m_w_kv_b, m_rel_bias, m_sinks, m_w_out, m_norm2_g, m_w_up, m_conv_w, m_conv_b, m_w_down, m_final_norm_g, v_norm1_g, v_w_in, v_q_a_norm_g, v_w_q_b, v_kv_a_norm_g, v_w_kv_b, v_rel_bias, v_sinks, v_w_out, v_norm2_g, v_w_up, v_conv_w, v_conv_b, v_w_down, v_final_norm_g):
    weights = dict(norm1_g=norm1_g, w_in=w_in, q_a_norm_g=q_a_norm_g, w_q_b=w_q_b, kv_a_norm_g=kv_a_norm_g,
                   w_kv_b=w_kv_b, rel_bias=rel_bias, sinks=sinks, w_out=w_out, norm2_g=norm2_g, w_up=w_up,
                   conv_w=conv_w, conv_b=conv_b, w_down=w_down, final_norm_g=final_norm_g)
    mom_m = dict(norm1_g=m_norm1_g, w_in=m_w_in, q_a_norm_g=m_q_a_norm_g, w_q_b=m_w_q_b, kv_a_norm_g=m_kv_a_norm_g,
                 w_kv_b=m_w_kv_b, rel_bias=m_rel_bias, sinks=m_sinks, w_out=m_w_out, norm2_g=m_norm2_g, w_up=m_w_up,
                 conv_w=m_conv_w, conv_b=m_conv_b, w_down=m_w_down, final_norm_g=m_final_norm_g)
    mom_v = dict(norm1_g=v_norm1_g, w_in=v_w_in, q_a_norm_g=v_q_a_norm_g, w_q_b=v_w_q_b, kv_a_norm_g=v_kv_a_norm_g,
                 w_kv_b=v_w_kv_b, rel_bias=v_rel_bias, sinks=v_sinks, w_out=v_w_out, norm2_g=v_norm2_g, w_up=v_w_up,
                 conv_w=v_conv_w, conv_b=v_conv_b, w_down=v_w_down, final_norm_g=v_final_norm_g)
    shard2d = {n: weights[n][0] for n in BIG}
    conv_w_shard = conv_w[0]
    xi, yi, ci = lax.axis_index("x"), lax.axis_index("y"), lax.axis_index("c")
    chip = (2 * xi + yi).astype(jnp.int32)

    core = ci.astype(jnp.int32).reshape(1)
    chip1 = chip.reshape(1)
    cat_cols = lambda a: jnp.concatenate([a[0], a[1], a[2], a[3]], axis=1)
    own_slot = lambda a, own: lax.dynamic_update_index_in_dim(a, own, chip, 0)
    halved = lambda a: a.reshape((2, a.shape[0] // 2) + a.shape[1:])
    quartered = lambda a: a.reshape(4, 2, a.shape[1] // 2, a.shape[2])

    send = [halved(shard2d[n].astype(BF16)) for n in FIRST] + [conv_w_shard]
    gathered = [own_slot(a, own) for a, own in zip(_allgather_weights(send, split=[True] * len(FIRST) + [False]), send)]
    g = {n: a.reshape((4,) + shard2d[n].shape) for n, a in zip(FIRST, gathered)}
    w_in_p = _pad_w_in(cat_cols(g["w_in"]))
    wq = _wq_heads(cat_cols(g["w_q_b"]))
    wkv = _wkv_heads(cat_cols(g["w_kv_b"]))
    conv_w_f = cat_cols(gathered[-1])

    later = [shard2d[n].astype(BF16) for n in LATER]
    gather2 = _SplitExchange("gather_later", later, [(4,) + a.shape for a in later],
                             src_of=lambda ref, k, p, pk: ref, dst_of=lambda ref, k, p, pk: ref.at[p],
                             arrive_of=lambda ref, k, p, pk: ref.at[pk])
    norm1_g_in = norm1_g + gather2.start()[:1, :1]

    def late_weights(after):
        w_out_g, w_up_g, w_down_g = [own_slot(a, own) for a, own in zip(gather2.wait(after), later)]
        return w_out_g.reshape(D_MODEL, D_MODEL), cat_cols(w_up_g), w_down_g.reshape(D_FF, D_MODEL)

    early = {}

    def early_grads(d_w_out, d_w_up, d_w_down):
        grads = [quartered(d_w_out.reshape(4, D_MODEL // 4, D_MODEL)), quartered(_col_shards(d_w_up)),
                 quartered(d_w_down.reshape(4, D_FF // 4, D_MODEL))]
        recv = _rs_pair_exchange("rs_pair_exchange_early", grads)
        early["pairs"] = [_rs_pair_add(f"rs_pair_add_{n}", core, gfull, r) for n, gfull, r in zip(LATER, grads, recv)]
        early["ici"] = _SplitExchange("rs_ici_early", early["pairs"], [(3,) + a.shape[1:] for a in early["pairs"]],
                                      src_of=lambda ref, k, p, pk: ref.at[pk], dst_of=lambda ref, k, p, pk: ref.at[k],
                                      arrive_of=lambda ref, k, p, pk: ref.at[k])
        return early["ici"].start()

    last = {}

    def last_grads(d_w_in_p, d_wq, d_wkv):
        grads = [quartered(_col_shards(_unpad_w_in(d_w_in_p))), quartered(_col_shards(_wq_unheads(d_wq))),
                 quartered(_col_shards(_wkv_unheads(d_wkv)))]
        recv = _rs_pair_exchange("rs_pair_exchange_last", grads)
        last["pairs"] = [_rs_pair_add(f"rs_pair_add_{n}", core, gfull, r) for n, gfull, r in zip(FIRST, grads, recv)]
        last["ici"] = _SplitExchange("rs_ici_last", last["pairs"], [(3,) + a.shape[1:] for a in last["pairs"]],
                                     src_of=lambda ref, k, p, pk: ref.at[pk], dst_of=lambda ref, k, p, pk: ref.at[k],
                                     arrive_of=lambda ref, k, p, pk: ref.at[k])
        return last["ici"].start()

    loss, dx, gr = _local_step(x[0], positions, loss_target[0], norm1_g_in, w_in_p, q_a_norm_g, wq, kv_a_norm_g, wkv,
                               rel_bias, sinks, late_weights, norm2_g, conv_w_f, conv_b, final_norm_g, early_grads,
                               last_grads)

    recv2 = last["ici"].wait(dx) + early["ici"].wait(dx)
    pairs = last["pairs"] + early["pairs"]
    halves = [_rs_final_add(f"rs_final_add_{n}", chip1, pr, r) for n, pr, r in zip(FIRST + LATER, pairs, recv2)]
    sibling_halves = _rs_pair_share(halves)

    small_parts = [gr[n] for n in SMALL] + [gr["conv_w"], loss]
    rows = _pack_rows(small_parts)
    summed = _unpack(_small_allreduce(_pack(small_parts, rows)), [p.shape for p in small_parts])
    small_g = dict(zip(SMALL, summed[:len(SMALL)]))
    conv_w_g = lax.dynamic_slice_in_dim(summed[len(SMALL)], chip * (2 * D_FF // 4), 2 * D_FF // 4, axis=1)
    loss_out = summed[-1].reshape(())

    out_g, out_d, out_m, out_v = {}, {}, {}, {}
    for n, mine, theirs in zip(FIRST + LATER, halves, sibling_halves):
        gsh, d, nm, nv = _adamw_halves(f"adamw_{n}", core, shard2d[n], mine, theirs, mom_m[n][0], mom_v[n][0])
        out_g[n], out_d[n], out_m[n], out_v[n] = gsh[None], d[None], nm[None], nv[None]
    names = SMALL + ("conv_w",)
    shapes = [weights[n].shape for n in names]
    sg = [small_g[n].reshape(weights[n].shape) for n in SMALL] + [conv_w_g[None]]
    prow = _pack_rows([weights[n] for n in names])
    d, nm, nv = _adamw("adamw_small", _pack([weights[n] for n in names], prow), _pack(sg, prow),
                       _pack([mom_m[n] for n in names], prow), _pack([mom_v[n] for n in names], prow))
    for n, gg, dd, mm, vv in zip(names, sg, _unpack(d, shapes), _unpack(nm, shapes), _unpack(nv, shapes)):
        out_g[n], out_d[n], out_m[n], out_v[n] = gg, dd, mm, vv

    order = ("norm1_g", "w_in", "q_a_norm_g", "w_q_b", "kv_a_norm_g", "w_kv_b", "rel_bias", "sinks", "w_out",
             "norm2_g", "w_up", "conv_w", "conv_b", "w_down", "final_norm_g")
    return (loss_out, dx[None], *[out_g[n] for n in order], *[out_d[n] for n in order],
            *[out_m[n] for n in order], *[out_v[n] for n in order])
```

```python
import functools
import math

import jax
import jax.numpy as jnp
import numpy as np
from jax import lax
from jax.experimental import pallas as pl
from jax.experimental.pallas import tpu as pltpu

F32 = jnp.float32
BF16 = jnp.bfloat16
MESH = pl.DeviceIdType.MESH

D_MODEL = 1024
EPS = 1e-6
H_A = 8
QK_NOPE = 128
QK_ROPE = 64
V_DIM = 128
Q_LORA = 256
KV_LORA = 128
ROPE_THETA = 10000.0
H_B = 16
KV_B = 4
GROUP = 4
HD_B = 64
WINDOW = 128
Q_BLOCK = 128
NUM_BUCKETS = 32
MAX_DISTANCE = 128
D_FF = 2816
HEAD_PAD = 256

ADAM_LR = 0.001
ADAM_B1 = 0.9
ADAM_B2 = 0.999
ADAM_EPS = 1e-08
ADAM_WD = 0.01
ADAM_STEP = 10

LANES = 128
P_QB, P_GA, P_GB, P_QLAT, P_KB, P_VB, P_CKV, P_KR = 0, 1024, 2048, 3072, 3328, 3584, 3840, 3968
W_IN_PAD = 4096

NT = (((1,), (1,)), ((), ()))
NN = (((1,), (0,)), ((), ()))
TN = (((0,), (0,)), ((), ()))


def _arb(n):
    return pltpu.CompilerParams(dimension_semantics=("arbitrary",) * n)


def _matmul(name, a, b, *, out_shape, out_dtype, grid, a_spec, b_spec, o_spec, contract, add=None, bf16_copy=False,
            after=None):
    nk = grid[2]
    acc_shape = tuple(d for d in o_spec.block_shape if d is not None)
    n_in = 2 + (add is not None) + (after is not None)
    n_out = 2 if bf16_copy else 1

    def body(*refs):
        a_ref, b_ref = refs[:2]
        add_ref = refs[2] if add is not None else None
        o_refs = refs[n_in:n_in + n_out]
        scratch = refs[n_in + n_out:]
        prod = lax.dot_general(a_ref[...].astype(BF16), b_ref[...].astype(BF16), contract,
                               preferred_element_type=F32)

        def finish(val):
            if add_ref is not None:
                val = add_ref[...] + val
            o_refs[0][...] = val.astype(out_dtype)
            if bf16_copy:
                o_refs[1][...] = val.astype(BF16)

        if nk == 1:
            finish(prod)
        else:
            acc_ref = scratch[0]
            k = pl.program_id(2)

            @pl.when(k == 0)
            def _():
                acc_ref[...] = prod

            @pl.when((k > 0) & (k < nk - 1))
            def _():
                acc_ref[...] += prod

            @pl.when(k == nk - 1)
            def _():
                finish(acc_ref[...] + prod)

    in_specs = [a_spec, b_spec]
    args = [a, b]
    if add is not None:
        in_specs.append(o_spec)
        args.append(add)
    if after is not None:
        in_specs.append(pl.BlockSpec(memory_space=pl.ANY))
        args.append(after)
    out_shapes = [jax.ShapeDtypeStruct(out_shape, out_dtype)]
    if bf16_copy:
        out_shapes.append(jax.ShapeDtypeStruct(out_shape, BF16))
    res = pl.pallas_call(
        body, name=name, grid=grid, in_specs=in_specs, out_specs=[o_spec] * n_out, out_shape=out_shapes,
        scratch_shapes=[pltpu.VMEM(acc_shape, F32)] if nk > 1 else [],
        compiler_params=_arb(3),
    )(*args)
    return res if bf16_copy else res[0]


def _bs(block, fn):
    return pl.BlockSpec(block, fn)


def _rmsnorm_fwd(name, src, g, d, cb, ts=512):
    s = src.shape[0]

    def body(x_ref, g_ref, h_ref, r_ref):
        x = x_ref[...]
        r = lax.rsqrt(jnp.mean(x * x, axis=-1, keepdims=True) + EPS)
        h_ref[...] = (x * r * g_ref[...]).astype(BF16)
        r_ref[...] = r

    return pl.pallas_call(
        body, name=name, grid=(s // ts,),
        in_specs=[_bs((ts, d), lambda i: (i, cb)), _bs((1, d), lambda i: (0, 0))],
        out_specs=[_bs((ts, d), lambda i: (i, 0)), _bs((ts, 1), lambda i: (i, 0))],
        out_shape=[jax.ShapeDtypeStruct((s, d), BF16), jax.ShapeDtypeStruct((s, 1), F32)],
        compiler_params=_arb(1),
    )(src, g)


def _rmsnorm_bwd(name, dy, src, rstd, g, d, cb, out_dtype, res=None, bf16_copy=False, ts=512):
    s = src.shape[0]

    def body(*refs):
        dy_ref, x_ref, r_ref, g_ref = refs[:4]
        res_ref = refs[4] if res is not None else None
        dx_ref, dg_ref = refs[n_in:n_in + 2]
        dyv = dy_ref[...]
        r = r_ref[...]
        xhat = x_ref[...] * r
        dyh = dyv * g_ref[...]
        c = jnp.mean(dyh * xhat, axis=-1, keepdims=True)
        dx = r * (dyh - xhat * c)
        if res_ref is not None:
            dx = res_ref[...] + dx
        dx_ref[...] = dx.astype(out_dtype)
        if bf16_copy:
            refs[n_in + 2][...] = dx.astype(BF16)
        part = jnp.sum(dyv * xhat, axis=0, keepdims=True)

        @pl.when(pl.program_id(0) == 0)
        def _():
            dg_ref[...] = part

        @pl.when(pl.program_id(0) > 0)
        def _():
            dg_ref[...] += part

    in_specs = [_bs((ts, d), lambda i: (i, 0)), _bs((ts, d), lambda i: (i, cb)),
                _bs((ts, 1), lambda i: (i, 0)), _bs((1, d), lambda i: (0, 0))]
    args = [dy, src, rstd, g]
    if res is not None:
        in_specs.append(_bs((ts, d), lambda i: (i, 0)))
        args.append(res)
    n_in = len(args)
    out_specs = [_bs((ts, d), lambda i: (i, 0)), _bs((1, d), lambda i: (0, 0))]
    out_shape = [jax.ShapeDtypeStruct((s, d), out_dtype), jax.ShapeDtypeStruct((1, d), F32)]
    if bf16_copy:
        out_specs.append(_bs((ts, d), lambda i: (i, 0)))
        out_shape.append(jax.ShapeDtypeStruct((s, d), BF16))
    return pl.pallas_call(
        body, name=name, grid=(s // ts,), in_specs=in_specs, out_specs=out_specs, out_shape=out_shape,
        compiler_params=_arb(1),
    )(*args)


def _final_loss(x2, target, g, ts=512):
    s, d = x2.shape

    def body(x_ref, t_ref, g_ref, loss_ref, dx_ref, dg_ref, dxb_ref):
        x = x_ref[...]
        r = lax.rsqrt(jnp.mean(x * x, axis=-1, keepdims=True) + EPS)
        xhat = x * r
        gv = g_ref[...]
        err = xhat * gv - t_ref[...]
        lpart = 0.5 * jnp.sum(jnp.mean(err * err, axis=-1, keepdims=True), axis=0, keepdims=True)
        dyv = err * (1.0 / d)
        dyh = dyv * gv
        c = jnp.mean(dyh * xhat, axis=-1, keepdims=True)
        dx = r * (dyh - xhat * c)
        dx_ref[...] = dx
        dxb_ref[...] = dx.astype(BF16)
        gpart = jnp.sum(dyv * xhat, axis=0, keepdims=True)

        @pl.when(pl.program_id(0) == 0)
        def _():
            dg_ref[...] = gpart
            loss_ref[...] = lpart

        @pl.when(pl.program_id(0) > 0)
        def _():
            dg_ref[...] += gpart
            loss_ref[...] += lpart

    return pl.pallas_call(
        body, name="final_loss", grid=(s // ts,),
        in_specs=[_bs((ts, d), lambda i: (i, 0)), _bs((ts, d), lambda i: (i, 0)), _bs((1, d), lambda i: (0, 0))],
        out_specs=[_bs((1, 1), lambda i: (0, 0)), _bs((ts, d), lambda i: (i, 0)), _bs((1, d), lambda i: (0, 0)),
                   _bs((ts, d), lambda i: (i, 0))],
        out_shape=[jax.ShapeDtypeStruct((1, 1), F32), jax.ShapeDtypeStruct((s, d), F32),
                   jax.ShapeDtypeStruct((1, d), F32), jax.ShapeDtypeStruct((s, d), BF16)],
        compiler_params=_arb(1),
    )(x2, target, g)


def _swap_halves(t):
    lane = lax.broadcasted_iota(jnp.int32, t.shape, 1)
    return jnp.where(lane < 32, pltpu.roll(t, 96, 1), pltpu.roll(t, 32, 1))


def _rope_fwd(t, cos_t, sin_t):
    return t * cos_t + _swap_halves(t) * sin_t


def _rope_bwd(dt, cos_t, sin_t):
    return dt * cos_t - _swap_halves(dt) * sin_t


def _lat_norms(proj, gq, gkv, ts=512):
    s = proj.shape[0]

    def body(q_ref, c_ref, gq_ref, gkv_ref, qn_ref, cn_ref, rq_ref, rc_ref):
        q = q_ref[...]
        rq = lax.rsqrt(jnp.mean(q * q, axis=-1, keepdims=True) + EPS)
        qn_ref[...] = (q * rq * gq_ref[...]).astype(BF16)
        rq_ref[...] = rq
        cv = c_ref[...]
        rc = lax.rsqrt(jnp.mean(cv * cv, axis=-1, keepdims=True) + EPS)
        cn_ref[...] = (cv * rc * gkv_ref[...]).astype(BF16)
        rc_ref[...] = rc

    return pl.pallas_call(
        body, name="lat_norms", grid=(s // ts,),
        in_specs=[_bs((ts, Q_LORA), lambda i: (i, P_QLAT // Q_LORA)),
                  _bs((ts, KV_LORA), lambda i: (i, P_CKV // KV_LORA)),
                  _bs((1, Q_LORA), lambda i: (0, 0)), _bs((1, KV_LORA), lambda i: (0, 0))],
        out_specs=[_bs((ts, Q_LORA), lambda i: (i, 0)), _bs((ts, KV_LORA), lambda i: (i, 0)),
                   _bs((ts, 1), lambda i: (i, 0)), _bs((ts, 1), lambda i: (i, 0))],
        out_shape=[jax.ShapeDtypeStruct((s, Q_LORA), BF16), jax.ShapeDtypeStruct((s, KV_LORA), BF16),
                   jax.ShapeDtypeStruct((s, 1), F32), jax.ShapeDtypeStruct((s, 1), F32)],
        compiler_params=_arb(1),
    )(proj, proj, gq, gkv)


HEAD_ROWS = 2048
DW_ROWS = 2048
MM_ROWS = 1024


def _q_heads(qn, wq, cos_t, sin_t):
    s = qn.shape[0]
    ts = min(s, HEAD_ROWS)

    def body(qn_ref, w_ref, cos_ref, sin_ref, q_ref):
        o = jnp.dot(qn_ref[...], w_ref[...], preferred_element_type=F32)
        q_ref[:, :LANES] = o[:, :LANES].astype(BF16)
        q_ref[:, LANES:] = _rope_fwd(o[:, LANES:], cos_ref[...], sin_ref[...]).astype(BF16)

    return pl.pallas_call(
        body, name="q_heads", grid=(H_A, s // ts),
        in_specs=[_bs((ts, Q_LORA), lambda h, i: (i, 0)), _bs((None, Q_LORA, HEAD_PAD), lambda h, i: (h, 0, 0)),
                  _bs((ts, LANES), lambda h, i: (i, 0)), _bs((ts, LANES), lambda h, i: (i, 0))],
        out_specs=_bs((None, ts, HEAD_PAD), lambda h, i: (h, i, 0)),
        out_shape=jax.ShapeDtypeStruct((H_A, s, HEAD_PAD), BF16),
        compiler_params=_arb(2),
    )(qn, wq, cos_t, sin_t)


def _kv_heads(cn, wkv, proj, cos_t, sin_t):
    s = cn.shape[0]
    ts = min(s, HEAD_ROWS)

    def body(cn_ref, w_ref, kr_ref, cos_ref, sin_ref, k_ref, v_ref):
        o = jnp.dot(cn_ref[...], w_ref[...], preferred_element_type=F32)
        k_ref[:, :LANES] = o[:, :LANES].astype(BF16)
        k_ref[:, LANES:] = _rope_fwd(kr_ref[...], cos_ref[...], sin_ref[...]).astype(BF16)
        v_ref[...] = o[:, LANES:].astype(BF16)

    return pl.pallas_call(
        body, name="kv_heads", grid=(H_A, s // ts),
        in_specs=[_bs((ts, KV_LORA), lambda h, i: (i, 0)),
                  _bs((None, KV_LORA, QK_NOPE + V_DIM), lambda h, i: (h, 0, 0)),
                  _bs((ts, LANES), lambda h, i: (i, P_KR // LANES)),
                  _bs((ts, LANES), lambda h, i: (i, 0)), _bs((ts, LANES), lambda h, i: (i, 0))],
        out_specs=[_bs((None, ts, HEAD_PAD), lambda h, i: (h, i, 0)), _bs((None, ts, V_DIM), lambda h, i: (h, i, 0))],
        out_shape=[jax.ShapeDtypeStruct((H_A, s, HEAD_PAD), BF16), jax.ShapeDtypeStruct((H_A, s, V_DIM), BF16)],
        compiler_params=_arb(2),
    )(cn, wkv, proj, cos_t, sin_t)


MLA_SCALE = 1.0 / math.sqrt(QK_NOPE + QK_ROPE)
LOG2E = math.log2(math.e)
MLA_EXP2_SCALE = MLA_SCALE * LOG2E


def _lane_tiles(a):
    return [a[:, j * LANES:(j + 1) * LANES] for j in range(a.shape[1] // LANES)]


MLA_SUB = 512


def _mla_fwd(q, k, v, tq=512, tk=1024):
    s = q.shape[1]
    tq = min(tq, s)
    nk = s // tk

    def body(q_ref, k_ref, v_ref, o_ref, lse_ref, m_ref, l_ref, acc_ref):
        m_ref[...] = jnp.full(m_ref.shape, -jnp.inf, F32)
        l_ref[...] = jnp.zeros(l_ref.shape, F32)
        acc_ref[...] = jnp.zeros(acc_ref.shape, F32)

        def step(c, carry):
            rows = pl.ds(pl.multiple_of(c * tk, tk), tk)
            for sub in range(tq // MLA_SUB):
                qr = slice(sub * MLA_SUB, (sub + 1) * MLA_SUB)
                raw = lax.dot_general(q_ref[qr, :], k_ref[rows, :], NT, preferred_element_type=F32)
                m_prev = m_ref[qr, :]
                m_new = jnp.maximum(m_prev, jnp.max(raw, axis=-1, keepdims=True))
                alpha = jnp.exp2((m_prev - m_new) * MLA_EXP2_SCALE)
                ps = [jnp.exp2((t - m_new) * MLA_EXP2_SCALE) for t in _lane_tiles(raw)]
                l_ref[qr, :] = alpha * l_ref[qr, :] + functools.reduce(lambda a, b: a + b, ps)
                p = jnp.concatenate(ps, axis=1).astype(BF16)
                acc_ref[qr, :] = alpha * acc_ref[qr, :] + jnp.dot(p, v_ref[rows, :], preferred_element_type=F32)
                m_ref[qr, :] = m_new
            return carry

        lax.fori_loop(0, nk, step, 0, unroll=True)
        l = jnp.sum(l_ref[...], axis=-1, keepdims=True)
        o_ref[...] = acc_ref[...] / l
        lse_ref[...] = m_ref[...] * MLA_SCALE + jnp.log(l)

    return pl.pallas_call(
        body, name="mla_fwd", grid=(H_A, s // tq),
        in_specs=[_bs((None, tq, HEAD_PAD), lambda h, i: (h, i, 0)),
                  _bs((None, s, HEAD_PAD), lambda h, i: (h, 0, 0)),
                  _bs((None, s, V_DIM), lambda h, i: (h, 0, 0))],
        out_specs=[_bs((tq, V_DIM), lambda h, i: (i, h)), _bs((None, tq, LANES), lambda h, i: (h, i, 0))],
        out_shape=[jax.ShapeDtypeStruct((s, H_A * V_DIM), F32), jax.ShapeDtypeStruct((H_A, s, LANES), F32)],
        scratch_shapes=[pltpu.VMEM((tq, LANES), F32), pltpu.VMEM((tq, LANES), F32), pltpu.VMEM((tq, V_DIM), F32)],
        compiler_params=_arb(2),
    )(q, k, v)


def _mla_bwd(q, k, v, do, o, lse, tq=512, tk=512):
    s = q.shape[1]
    nq = s // tq

    def body(q_ref, k_ref, v_ref, do_ref, o_ref, lse_ref, dq_ref, dk_ref, dv_ref, delta_ref):
        @pl.when(pl.program_id(1) == 0)
        def _():
            def init(c, carry):
                rows = pl.ds(pl.multiple_of(c * tq, tq), tq)
                delta = jnp.sum(do_ref[rows, :] * o_ref[rows, :], axis=-1, keepdims=True)
                delta_ref[rows, :] = jnp.broadcast_to(delta, (tq, LANES))
                dq_ref[rows, :] = jnp.zeros((tq, HEAD_PAD), F32)
                return carry

            lax.fori_loop(0, nq, init, 0)

        dk_ref[...] = jnp.zeros(dk_ref.shape, F32)
        dv_ref[...] = jnp.zeros(dv_ref.shape, F32)
        kb = k_ref[...]
        vb = v_ref[...]

        def step(c, carry):
            rows = pl.ds(pl.multiple_of(c * tq, tq), tq)
            qc = q_ref[rows, :]
            doc = do_ref[rows, :].astype(BF16)
            raw = lax.dot_general(qc, kb, NT, preferred_element_type=F32)
            dp = lax.dot_general(doc, vb, NT, preferred_element_type=F32)
            lse2 = lse_ref[rows, :] * LOG2E
            delta = delta_ref[rows, :]
            ps = [jnp.exp2(t * MLA_EXP2_SCALE - lse2) for t in _lane_tiles(raw)]
            dss = [pj * (dj - delta) * MLA_SCALE for pj, dj in zip(ps, _lane_tiles(dp))]
            p = jnp.concatenate(ps, axis=1).astype(BF16)
            ds = jnp.concatenate(dss, axis=1).astype(BF16)
            dv_ref[...] += lax.dot_general(p, doc, TN, preferred_element_type=F32)
            dk_ref[...] += lax.dot_general(ds, qc, TN, preferred_element_type=F32)
            dq_ref[rows, :] += jnp.dot(ds, kb, preferred_element_type=F32)
            return carry

        lax.fori_loop(0, nq, step, 0, unroll=True)

    return pl.pallas_call(
        body, name="mla_bwd", grid=(H_A, s // tk),
        in_specs=[_bs((None, s, HEAD_PAD), lambda h, j: (h, 0, 0)),
                  _bs((None, tk, HEAD_PAD), lambda h, j: (h, j, 0)),
                  _bs((None, tk, V_DIM), lambda h, j: (h, j, 0)),
                  _bs((s, V_DIM), lambda h, j: (0, h)), _bs((s, V_DIM), lambda h, j: (0, h)),
                  _bs((None, s, LANES), lambda h, j: (h, 0, 0))],
        out_specs=[_bs((None, s, HEAD_PAD), lambda h, j: (h, 0, 0)),
                   _bs((None, tk, HEAD_PAD), lambda h, j: (h, j, 0)),
                   _bs((None, tk, V_DIM), lambda h, j: (h, j, 0))],
        out_shape=[jax.ShapeDtypeStruct((H_A, s, HEAD_PAD), F32), jax.ShapeDtypeStruct((H_A, s, HEAD_PAD), F32),
                   jax.ShapeDtypeStruct((H_A, s, V_DIM), F32)],
        scratch_shapes=[pltpu.VMEM((s, LANES), F32)],
        compiler_params=_arb(2),
    )(q, k, v, do, o, lse)


def _mla_bwd_prep(dq, dk, dv, cos_t, sin_t, ts=256):
    s = dq.shape[1]

    def body(dq_ref, dk_ref, dv_ref, cos_ref, sin_ref, dqp_ref, dkvp_ref, dkr_ref):
        cos_v = cos_ref[...]
        sin_v = sin_ref[...]
        kr = jnp.zeros((ts, LANES), F32)
        for h in range(H_A):
            dqp_ref[h, :, :LANES] = dq_ref[h, :, :LANES].astype(BF16)
            dqp_ref[h, :, LANES:] = _rope_bwd(dq_ref[h, :, LANES:], cos_v, sin_v).astype(BF16)
            dkvp_ref[h, :, :LANES] = dk_ref[h, :, :LANES].astype(BF16)
            dkvp_ref[h, :, LANES:] = dv_ref[h].astype(BF16)
            kr = kr + dk_ref[h, :, LANES:]
        dkr_ref[...] = _rope_bwd(kr, cos_v, sin_v).astype(BF16)

    blk3 = lambda w: _bs((H_A, ts, w), lambda i: (0, i, 0))
    return pl.pallas_call(
        body, name="mla_bwd_prep", grid=(s // ts,),
        in_specs=[blk3(HEAD_PAD), blk3(HEAD_PAD), blk3(V_DIM),
                  _bs((ts, LANES), lambda i: (i, 0)), _bs((ts, LANES), lambda i: (i, 0))],
        out_specs=[blk3(HEAD_PAD), blk3(HEAD_PAD), _bs((ts, LANES), lambda i: (i, 0))],
        out_shape=[jax.ShapeDtypeStruct((H_A, s, HEAD_PAD), BF16), jax.ShapeDtypeStruct((H_A, s, HEAD_PAD), BF16),
                   jax.ShapeDtypeStruct((s, LANES), BF16)],
        compiler_params=_arb(1),
    )(dq, dk, dv, cos_t, sin_t)


WIN_SCALE = 1.0 / math.sqrt(HD_B)
SPAN = Q_BLOCK + 2 * WINDOW


def _t5_bucket_table():
    a = jnp.arange(Q_BLOCK, dtype=jnp.int32)[:, None]
    c = jnp.arange(SPAN, dtype=jnp.int32)[None, :]
    rel = c - WINDOW - a
    nb = NUM_BUCKETS // 2
    max_exact = nb // 2
    base = (rel > 0).astype(jnp.int32) * nb
    n = jnp.abs(rel)
    nf = jnp.maximum(n, 1).astype(F32)
    large = max_exact + (jnp.log(nf / max_exact) / math.log(MAX_DISTANCE / max_exact)
                         * (nb - max_exact)).astype(jnp.int32)
    large = jnp.minimum(large, nb - 1)
    return base + jnp.where(n < max_exact, n, large)


def _win_bias(bucket, rel_bias):
    def body(rb_ref, bk_ref, o_ref):
        h = pl.program_id(0)
        bk = bk_ref[...]
        acc = jnp.zeros((Q_BLOCK, SPAN), F32)
        for b in range(NUM_BUCKETS):
            acc = jnp.where(bk == b, rb_ref[b, h], acc)
        o_ref[...] = acc

    return pl.pallas_call(
        body, name="win_bias", grid=(H_B,),
        in_specs=[pl.BlockSpec(memory_space=pltpu.SMEM), _bs((Q_BLOCK, SPAN), lambda h: (0, 0))],
        out_specs=_bs((None, Q_BLOCK, SPAN), lambda h: (h, 0, 0)),
        out_shape=jax.ShapeDtypeStruct((H_B, Q_BLOCK, SPAN), F32),
        compiler_params=_arb(1),
    )(rel_bias, bucket)


GROUP_W = GROUP * HD_B


def _win_kv_rows(n, j, nblk):
    blk = jnp.clip(n + j - 1, 0, nblk - 1)
    return pl.ds(pl.multiple_of(blk * Q_BLOCK, Q_BLOCK), Q_BLOCK)


def _win_head_cols(kv):
    return slice(kv * HD_B, (kv + 1) * HD_B)


def _win_stack(ref, kv):
    return jnp.concatenate([ref[:, kv * GROUP_W + g * HD_B:kv * GROUP_W + (g + 1) * HD_B] for g in range(GROUP)], axis=0)


def _win_unstack(ref, kv, val):
    for g in range(GROUP):
        ref[:, kv * GROUP_W + g * HD_B:kv * GROUP_W + (g + 1) * HD_B] = val[g * Q_BLOCK:(g + 1) * Q_BLOCK].astype(ref.dtype)


def _win_scores(q, k_ref, kv, bias_ref, n, nblk):
    a = lax.broadcasted_iota(jnp.int32, (GROUP, Q_BLOCK, Q_BLOCK), 1)
    cc = lax.broadcasted_iota(jnp.int32, (GROUP, Q_BLOCK, Q_BLOCK), 2)
    valid = [(cc >= a) & (n > 0), None, (cc <= a) & (n < nblk - 1)]
    out = []
    for j in range(3):
        sc = lax.dot_general(q, k_ref[_win_kv_rows(n, j, nblk), _win_head_cols(kv)], NT, preferred_element_type=F32)
        sc = (sc.reshape(GROUP, Q_BLOCK, Q_BLOCK) * WIN_SCALE
              + bias_ref[kv * GROUP:(kv + 1) * GROUP, :, j * Q_BLOCK:(j + 1) * Q_BLOCK])
        if valid[j] is not None:
            sc = jnp.where(valid[j], sc, -1e30)
        out.append(sc)
    return out


def _win_sink(sink_ref, kv):
    hs = lax.broadcasted_iota(jnp.int32, (GROUP, Q_BLOCK, 1), 0)
    sk = jnp.zeros((GROUP, Q_BLOCK, 1), F32)
    for g in range(GROUP):
        sk = jnp.where(hs == g, sink_ref[kv * GROUP + g], sk)
    return sk


def _win_fwd(proj_b, bias, sinks):
    s = proj_b.shape[0]
    nblk = s // Q_BLOCK
    rows = GROUP * Q_BLOCK

    def body(sink_ref, q_ref, k_ref, v_ref, bias_ref, o_ref, lse_ref):
        n = pl.program_id(0)
        for kv in range(KV_B):
            sk = _win_sink(sink_ref, kv)
            q = _win_stack(q_ref, kv)
            ss = _win_scores(q, k_ref, kv, bias_ref, n, nblk)
            m = jnp.maximum(jnp.max(jnp.maximum(jnp.maximum(ss[0], ss[1]), ss[2]), axis=2, keepdims=True), sk)
            es = [jnp.exp(sc - m) for sc in ss]
            l = jnp.sum(es[0] + es[1] + es[2], axis=2, keepdims=True) + jnp.exp(sk - m)
            acc = jnp.zeros((rows, HD_B), F32)
            for j, e in enumerate(es):
                p = (e / l).astype(BF16).reshape(rows, Q_BLOCK)
                acc = acc + jnp.dot(p, v_ref[_win_kv_rows(n, j, nblk), _win_head_cols(kv)],
                                    preferred_element_type=F32)
            _win_unstack(o_ref, kv, acc)
            lse_ref[kv * GROUP:(kv + 1) * GROUP] = m + jnp.log(l)

    kv_w = KV_B * HD_B
    return pl.pallas_call(
        body, name="win_fwd", grid=(nblk,),
        in_specs=[pl.BlockSpec(memory_space=pltpu.SMEM), _bs((Q_BLOCK, H_B * HD_B), lambda n: (n, P_QB // (H_B * HD_B))),
                  _bs((s, kv_w), lambda n: (0, P_KB // kv_w)), _bs((s, kv_w), lambda n: (0, P_VB // kv_w)),
                  _bs((H_B, Q_BLOCK, SPAN), lambda n: (0, 0, 0))],
        out_specs=[_bs((Q_BLOCK, H_B * HD_B), lambda n: (n, 0)), _bs((H_B, Q_BLOCK, 1), lambda n: (0, n, 0))],
        out_shape=[jax.ShapeDtypeStruct((s, H_B * HD_B), F32), jax.ShapeDtypeStruct((H_B, s, 1), F32)],
        compiler_params=_arb(1),
    )(sinks, proj_b, proj_b, proj_b, bias)


def _win_bwd(proj_b, bias, sinks, do_b, lse):
    s = proj_b.shape[0]
    nblk = s // Q_BLOCK
    rows = GROUP * Q_BLOCK
    spad = s + 2 * WINDOW

    def body(sink_ref, q_ref, k_ref, v_ref, bias_ref, do_ref, lse_ref, dq_ref, dk_ref, dv_ref, db_ref, dsk_ref):
        n = pl.program_id(0)

        @pl.when(n == 0)
        def _():
            dk_ref[...] = jnp.zeros(dk_ref.shape, F32)
            dv_ref[...] = jnp.zeros(dv_ref.shape, F32)
            db_ref[...] = jnp.zeros(db_ref.shape, F32)
            dsk_ref[...] = jnp.zeros(dsk_ref.shape, F32)

        for kv in range(KV_B):
            heads = slice(kv * GROUP, (kv + 1) * GROUP)
            sk = _win_sink(sink_ref, kv)
            q = _win_stack(q_ref, kv)
            dob = _win_stack(do_ref, kv)
            lse_v = lse_ref[heads]
            ss = _win_scores(q, k_ref, kv, bias_ref, n, nblk)
            ps = [jnp.exp(sc - lse_v) for sc in ss]
            dps = [lax.dot_general(dob, v_ref[_win_kv_rows(n, j, nblk), _win_head_cols(kv)], NT,
                                   preferred_element_type=F32).reshape(GROUP, Q_BLOCK, Q_BLOCK) for j in range(3)]
            delta = jnp.sum(ps[0] * dps[0] + ps[1] * dps[1] + ps[2] * dps[2], axis=2, keepdims=True)
            dq = jnp.zeros((rows, HD_B), F32)
            for j in range(3):
                ds = ps[j] * (dps[j] - delta)
                db_ref[heads, :, j * Q_BLOCK:(j + 1) * Q_BLOCK] += ds
                dsb = (ds * WIN_SCALE).astype(BF16).reshape(rows, Q_BLOCK)
                dq = dq + jnp.dot(dsb, k_ref[_win_kv_rows(n, j, nblk), _win_head_cols(kv)],
                                  preferred_element_type=F32)
                krows = pl.ds(pl.multiple_of((n + j) * Q_BLOCK, Q_BLOCK), Q_BLOCK)
                dk_ref[krows, _win_head_cols(kv)] += lax.dot_general(dsb, q, TN, preferred_element_type=F32)
                dv_ref[krows, _win_head_cols(kv)] += lax.dot_general(
                    ps[j].astype(BF16).reshape(rows, Q_BLOCK), dob, TN, preferred_element_type=F32)
            dsk_ref[heads] += -(jnp.exp(sk - lse_v) * delta)
            _win_unstack(dq_ref, kv, dq)

    kv_w = KV_B * HD_B
    qspec = _bs((Q_BLOCK, H_B * HD_B), lambda n: (n, 0))
    kacc = _bs((spad, kv_w), lambda n: (0, 0))
    return pl.pallas_call(
        body, name="win_bwd", grid=(nblk,),
        in_specs=[pl.BlockSpec(memory_space=pltpu.SMEM), _bs((Q_BLOCK, H_B * HD_B), lambda n: (n, P_QB // (H_B * HD_B))),
                  _bs((s, kv_w), lambda n: (0, P_KB // kv_w)), _bs((s, kv_w), lambda n: (0, P_VB // kv_w)),
                  _bs((H_B, Q_BLOCK, SPAN), lambda n: (0, 0, 0)), qspec, _bs((H_B, Q_BLOCK, 1), lambda n: (0, n, 0))],
        out_specs=[qspec, kacc, kacc, _bs((H_B, Q_BLOCK, SPAN), lambda n: (0, 0, 0)),
                   _bs((H_B, Q_BLOCK, 1), lambda n: (0, 0, 0))],
        out_shape=[jax.ShapeDtypeStruct((s, H_B * HD_B), BF16), jax.ShapeDtypeStruct((spad, kv_w), F32),
                   jax.ShapeDtypeStruct((spad, kv_w), F32), jax.ShapeDtypeStruct((H_B, Q_BLOCK, SPAN), F32),
                   jax.ShapeDtypeStruct((H_B, Q_BLOCK, 1), F32)],
        compiler_params=_arb(1),
    )(sinks, proj_b, proj_b, proj_b, bias, do_b, lse)


def _win_param_grads(bucket, dbias, dsink_rows):
    def body(bk_ref, db_ref, ds_ref, o_ref):
        bk = bk_ref[...]
        dbv = db_ref[...]
        lane = lax.broadcasted_iota(jnp.int32, (1, LANES), 1)
        res = jnp.zeros((1, LANES), F32)
        for b in range(NUM_BUCKETS):
            tot = jnp.sum(jnp.sum(jnp.where(bk == b, dbv, 0.0), axis=1, keepdims=True), axis=0, keepdims=True)
            res = jnp.where(lane == b, tot, res)
        stot = jnp.sum(ds_ref[...], axis=0, keepdims=True)
        o_ref[...] = jnp.where(lane == NUM_BUCKETS, stot, res)

    return pl.pallas_call(
        body, name="win_param_grads", grid=(H_B,),
        in_specs=[_bs((Q_BLOCK, SPAN), lambda h: (0, 0)), _bs((None, Q_BLOCK, SPAN), lambda h: (h, 0, 0)),
                  _bs((None, Q_BLOCK, 1), lambda h: (h, 0, 0))],
        out_specs=_bs((None, 1, LANES), lambda h: (h, 0, 0)),
        out_shape=jax.ShapeDtypeStruct((H_B, 1, LANES), F32),
        compiler_params=_arb(1),
    )(bucket, dbias, dsink_rows)


def _gate_fwd(proj, o_a, o_b, ts=256):
    s = o_a.shape[0]
    wide = lambda cb: _bs((ts, D_MODEL), lambda i: (i, cb))

    def body(ga_ref, gb_ref, oa_ref, ob_ref, m_ref):
        m_ref[...] = (jax.nn.sigmoid(ga_ref[...]) * oa_ref[...]
                      + jax.nn.sigmoid(gb_ref[...]) * ob_ref[...]).astype(BF16)

    return pl.pallas_call(
        body, name="gate_fwd", grid=(s // ts,),
        in_specs=[wide(P_GA // D_MODEL), wide(P_GB // D_MODEL), wide(0), wide(0)],
        out_specs=wide(0), out_shape=jax.ShapeDtypeStruct((s, D_MODEL), BF16),
        compiler_params=_arb(1),
    )(proj, proj, o_a, o_b)


def _gate_bwd(dmixed, proj, o_a, o_b, ts=256):
    s = o_a.shape[0]
    wide = lambda cb: _bs((ts, D_MODEL), lambda i: (i, cb))

    def body(dm_ref, ga_ref, gb_ref, oa_ref, ob_ref, doa_ref, dob_ref, dga_ref, dgb_ref):
        dm = dm_ref[...]
        sa = jax.nn.sigmoid(ga_ref[...])
        sb = jax.nn.sigmoid(gb_ref[...])
        doa_ref[...] = dm * sa
        dob_ref[...] = (dm * sb).astype(BF16)
        dga_ref[...] = (dm * oa_ref[...] * (sa * (1.0 - sa))).astype(BF16)
        dgb_ref[...] = (dm * ob_ref[...] * (sb * (1.0 - sb))).astype(BF16)

    return pl.pallas_call(
        body, name="gate_bwd", grid=(s // ts,),
        in_specs=[wide(0), wide(P_GA // D_MODEL), wide(P_GB // D_MODEL), wide(0), wide(0)],
        out_specs=[wide(0)] * 4,
        out_shape=[jax.ShapeDtypeStruct((s, D_MODEL), F32), jax.ShapeDtypeStruct((s, D_MODEL), BF16),
                   jax.ShapeDtypeStruct((s, D_MODEL), BF16), jax.ShapeDtypeStruct((s, D_MODEL), BF16)],
        compiler_params=_arb(1),
    )(dmixed, proj, proj, o_a, o_b)


CONV_CHUNK = 512
N_SLAB = D_FF // LANES


def _shifted(ref, c, nchunks):
    r0 = c * CONV_CHUNK
    cur = ref[r0:r0 + CONV_CHUNK, :]
    row = lax.broadcasted_iota(jnp.int32, (8, LANES), 0)
    before = ref[r0 - 8:r0, :][7:8, :] if c > 0 else jnp.zeros((1, LANES), F32)
    after = ref[r0 + CONV_CHUNK:r0 + CONV_CHUNK + 8, :][0:1, :] if c < nchunks - 1 else jnp.zeros((1, LANES), F32)
    down = pltpu.roll(cur, 1, 0)
    up = pltpu.roll(cur, CONV_CHUNK - 1, 0)
    prev = jnp.concatenate([jnp.where(row == 0, before, down[:8]), down[8:]], axis=0)
    nxt = jnp.concatenate([up[:-8], jnp.where(row == 7, after, up[-8:])], axis=0)
    return prev, cur, nxt


def _conv_taps(ref, w_ref, b_ref, c, nchunks):
    prev, cur, nxt = _shifted(ref, c, nchunks)
    conv = prev * w_ref[0:1, :] + cur * w_ref[1:2, :] + nxt * w_ref[2:3, :] + b_ref[...]
    return conv, prev, cur, nxt


def _convffn_fwd(u, conv_w, conv_b):
    s = u.shape[0]
    nchunks = s // CONV_CHUNK

    def body(ug_ref, uv_ref, wg_ref, wv_ref, bg_ref, bv_ref, f_ref):
        for c in range(nchunks):
            cg = _conv_taps(ug_ref, wg_ref, bg_ref, c, nchunks)[0]
            cv = _conv_taps(uv_ref, wv_ref, bv_ref, c, nchunks)[0]
            f_ref[c * CONV_CHUNK:(c + 1) * CONV_CHUNK, :] = (cg * jax.nn.sigmoid(cg) * cv).astype(BF16)

    slab = lambda off: _bs((s, LANES), lambda j: (0, off + j))
    wsl = lambda off: _bs((3, LANES), lambda j: (0, off + j))
    bsl = lambda off: _bs((1, LANES), lambda j: (0, off + j))
    return pl.pallas_call(
        body, name="convffn_fwd", grid=(N_SLAB,),
        in_specs=[slab(0), slab(N_SLAB), wsl(0), wsl(N_SLAB), bsl(0), bsl(N_SLAB)],
        out_specs=slab(0), out_shape=jax.ShapeDtypeStruct((s, D_FF), BF16),
        compiler_params=_arb(1),
    )(u, u, conv_w, conv_w, conv_b, conv_b)


def _convffn_bwd(u, conv_w, conv_b, df):
    s = u.shape[0]
    nchunks = s // CONV_CHUNK

    def body(ug_ref, uv_ref, wg_ref, wv_ref, bg_ref, bv_ref, df_ref, du_ref, dw_ref, db_ref, dcg_ref, dcv_ref):
        dwg = [jnp.zeros((1, LANES), F32) for _ in range(3)]
        dwv = [jnp.zeros((1, LANES), F32) for _ in range(3)]
        dbg = jnp.zeros((1, LANES), F32)
        dbv = jnp.zeros((1, LANES), F32)
        for c in range(nchunks):
            rows = slice(c * CONV_CHUNK, (c + 1) * CONV_CHUNK)
            cg, gp, gc, gn = _conv_taps(ug_ref, wg_ref, bg_ref, c, nchunks)
            cv, vp, vc, vn = _conv_taps(uv_ref, wv_ref, bv_ref, c, nchunks)
            dfv = df_ref[rows, :]
            sg = jax.nn.sigmoid(cg)
            dcg = dfv * cv * (sg * (1.0 + cg * (1.0 - sg)))
            dcv = dfv * (cg * sg)
            dcg_ref[rows, :] = dcg
            dcv_ref[rows, :] = dcv
            for t, (tg, tv) in enumerate(((gp, vp), (gc, vc), (gn, vn))):
                dwg[t] = dwg[t] + jnp.sum(tg * dcg, axis=0, keepdims=True)
                dwv[t] = dwv[t] + jnp.sum(tv * dcv, axis=0, keepdims=True)
            dbg = dbg + jnp.sum(dcg, axis=0, keepdims=True)
            dbv = dbv + jnp.sum(dcv, axis=0, keepdims=True)
        for t in range(3):
            dw_ref[0, t:t + 1, :] = dwg[t]
            dw_ref[1, t:t + 1, :] = dwv[t]
        db_ref[0] = dbg
        db_ref[1] = dbv
        for half, (dc_ref, w_ref) in enumerate(((dcg_ref, wg_ref), (dcv_ref, wv_ref))):
            for c in range(nchunks):
                prev, cur, nxt = _shifted(dc_ref, c, nchunks)
                du = nxt * w_ref[0:1, :] + cur * w_ref[1:2, :] + prev * w_ref[2:3, :]
                du_ref[half, c * CONV_CHUNK:(c + 1) * CONV_CHUNK, :] = du.astype(BF16)

    slab = lambda off: _bs((s, LANES), lambda j: (0, off + j))
    wsl = lambda off: _bs((3, LANES), lambda j: (0, off + j))
    bsl = lambda off: _bs((1, LANES), lambda j: (0, off + j))
    return pl.pallas_call(
        body, name="convffn_bwd", grid=(N_SLAB,),
        in_specs=[slab(0), slab(N_SLAB), wsl(0), wsl(N_SLAB), bsl(0), bsl(N_SLAB), slab(0)],
        out_specs=[_bs((2, s, LANES), lambda j: (0, 0, j)), _bs((2, 3, LANES), lambda j: (0, 0, j)),
                   _bs((2, 1, LANES), lambda j: (0, 0, j))],
        out_shape=[jax.ShapeDtypeStruct((2, s, D_FF), BF16), jax.ShapeDtypeStruct((2, 3, D_FF), F32),
                   jax.ShapeDtypeStruct((2, 1, D_FF), F32)],
        scratch_shapes=[pltpu.VMEM((s, LANES), F32), pltpu.VMEM((s, LANES), F32)],
        compiler_params=_arb(1),
    )(u, u, conv_w, conv_w, conv_b, conv_b, df)


def _row_tile(rows, limit=512):
    best = rows
    for t in range(8, min(rows, limit) + 1, 8):
        if rows % t == 0:
            best = t
    return best if rows % 8 == 0 else rows


ADAM_C1 = 1.0 - ADAM_B1 ** ADAM_STEP
ADAM_C2 = 1.0 - ADAM_B2 ** ADAM_STEP


def _adamw_math(w, gv, m, v):
    nm = ADAM_B1 * m + (1.0 - ADAM_B1) * gv
    nv = ADAM_B2 * v + (1.0 - ADAM_B2) * (gv * gv)
    m_hat = nm / ADAM_C1
    v_hat = nv / ADAM_C2
    return -ADAM_LR * (m_hat / (jnp.sqrt(v_hat) + ADAM_EPS) + ADAM_WD * w), nm, nv


def _adamw_halves(name, core, w, mine, theirs, m, v):
    half, cols = mine.shape
    tr = _row_tile(half)
    nr = half // tr

    def body(core_ref, w_ref, mine_ref, theirs_ref, m_ref, v_ref, g_ref, d_ref, nm_ref, nv_ref):
        gv = jnp.where(pl.program_id(0) == core_ref[0], mine_ref[...], theirs_ref[...])
        g_ref[...] = gv
        d_ref[...], nm_ref[...], nv_ref[...] = _adamw_math(w_ref[...], gv, m_ref[...], v_ref[...])

    full = pl.BlockSpec((tr, cols), lambda hf, r, cr: (hf * nr + r, 0))
    part = pl.BlockSpec((tr, cols), lambda hf, r, cr: (r, 0))
    return pl.pallas_call(
        body, name=name,
        grid_spec=pltpu.PrefetchScalarGridSpec(num_scalar_prefetch=1, grid=(2, nr),
                                               in_specs=[full, part, part, full, full], out_specs=[full] * 4),
        out_shape=[jax.ShapeDtypeStruct((2 * half, cols), F32)] * 4, compiler_params=_arb(2),
    )(core, w, mine, theirs, m, v)


ANY = pl.BlockSpec(memory_space=pl.ANY)


def _mesh_pos():
    return lax.axis_index("x"), lax.axis_index("y"), lax.axis_index("c")


def _other_chips(x, y):
    return [(1 - x, y), (x, 1 - y), (1 - x, 1 - y)]


def _allgather_weights(shards, split):
    n = len(shards)

    def body(*refs):
        w_refs, o_refs = refs[:n], refs[n:2 * n]
        send_sems, recv_sems, fsend_sems, frecv_sems = refs[2 * n:]
        x, y, c = _mesh_pos()
        p = 2 * x + y
        chips = _other_chips(x, y)

        def piece(i, chip_index, core):
            return o_refs[i].at[chip_index, core] if split[i] else o_refs[i].at[chip_index]

        def remote(src, dst, ssem, rsem, to):
            return pltpu.make_async_remote_copy(src_ref=src, dst_ref=dst, send_sem=ssem, recv_sem=rsem,
                                                device_id=to, device_id_type=MESH)

        sends = []
        for i in range(n):
            src = w_refs[i].at[c] if split[i] else w_refs[i]
            for k, chip in enumerate(chips):
                cp = remote(src, piece(i, p, c), send_sems.at[3 * i + k], recv_sems.at[3 * i + k], (*chip, c))
                cp.start()
                sends.append(cp)
        for i in range(n):
            for k, chip in enumerate(chips):
                pk = 2 * chip[0] + chip[1]
                landed = piece(i, pk, c)
                remote(landed, landed, send_sems.at[3 * i + k], recv_sems.at[3 * i + k], (*chip, c)).wait_recv()
                if split[i]:
                    fw = remote(landed, landed, fsend_sems.at[3 * i + k], frecv_sems.at[3 * i + k], (x, y, 1 - c))
                    fw.start()
                    sends.append(fw)
        for i in range(n):
            if split[i]:
                for k, chip in enumerate(chips):
                    pk = 2 * chip[0] + chip[1]
                    theirs = piece(i, pk, 1 - c)
                    remote(theirs, theirs, fsend_sems.at[3 * i + k], frecv_sems.at[3 * i + k],
                           (x, y, 1 - c)).wait_recv()
        for cp in sends:
            cp.wait_send()

    return pl.pallas_call(
        body, name="allgather_weights",
        in_specs=[ANY] * n, out_specs=[ANY] * n,
        out_shape=[jax.ShapeDtypeStruct((4,) + w.shape, w.dtype) for w in shards],
        scratch_shapes=[pltpu.SemaphoreType.DMA((3 * n,)), pltpu.SemaphoreType.DMA((3 * n,)),
                        pltpu.SemaphoreType.DMA((3 * n,)), pltpu.SemaphoreType.DMA((3 * n,))],
    )(*shards)


def _rs_pair_exchange(name, grads):
    n = len(grads)

    def body(*refs):
        g_refs, o_refs = refs[:n], refs[n:2 * n]
        send_sems, recv_sems = refs[2 * n:]
        x, y, c = _mesh_pos()
        cps = []
        for i in range(n):
            cp = pltpu.make_async_remote_copy(
                src_ref=g_refs[i].at[:, 1 - c], dst_ref=o_refs[i],
                send_sem=send_sems.at[i], recv_sem=recv_sems.at[i], device_id=(x, y, 1 - c), device_id_type=MESH)
            cp.start()
            cps.append(cp)
        for cp in cps:
            cp.wait()

    return pl.pallas_call(
        body, name=name, in_specs=[ANY] * n, out_specs=[ANY] * n,
        out_shape=[jax.ShapeDtypeStruct((4,) + g.shape[2:], F32) for g in grads],
        scratch_shapes=[pltpu.SemaphoreType.DMA((n,)), pltpu.SemaphoreType.DMA((n,))],
    )(*grads)


def _rs_pair_add(name, core, g, recv):
    _, half, cols = recv.shape
    tr = _row_tile(half)
    nr = half // tr

    def body(core_ref, g_ref, r_ref, o_ref):
        o_ref[...] = (g_ref[...] + r_ref[...]).astype(BF16)

    return pl.pallas_call(
        body, name=name,
        grid_spec=pltpu.PrefetchScalarGridSpec(
            num_scalar_prefetch=1, grid=(4, nr),
            in_specs=[pl.BlockSpec((None, None, tr, cols), lambda q, r, cr: (q, cr[0], r, 0)),
                      pl.BlockSpec((None, tr, cols), lambda q, r, cr: (q, r, 0))],
            out_specs=pl.BlockSpec((None, tr, cols), lambda q, r, cr: (q, r, 0))),
        out_shape=jax.ShapeDtypeStruct((4, half, cols), BF16),
        compiler_params=_arb(2),
    )(core, g, recv)


def _rs_final_add(name, chip, pair, recv):
    _, half, cols = pair.shape
    tr = _row_tile(half)

    def body(chip_ref, p_ref, r_ref, o_ref):
        o_ref[...] = ((p_ref[...].astype(F32) + r_ref[0].astype(F32)) + r_ref[1].astype(F32)) + r_ref[2].astype(F32)

    return pl.pallas_call(
        body, name=name,
        grid_spec=pltpu.PrefetchScalarGridSpec(
            num_scalar_prefetch=1, grid=(half // tr,),
            in_specs=[pl.BlockSpec((None, tr, cols), lambda r, ch: (ch[0], r, 0)),
                      pl.BlockSpec((3, tr, cols), lambda r, ch: (0, r, 0))],
            out_specs=pl.BlockSpec((tr, cols), lambda r, ch: (r, 0))),
        out_shape=jax.ShapeDtypeStruct((half, cols), F32),
        compiler_params=_arb(1),
    )(chip, pair, recv)


def _rs_pair_share(halves):
    n = len(halves)

    def body(*refs):
        h_refs, o_refs = refs[:n], refs[n:2 * n]
        send_sems, recv_sems = refs[2 * n:]
        x, y, c = _mesh_pos()
        cps = []
        for i in range(n):
            cp = pltpu.make_async_remote_copy(src_ref=h_refs[i], dst_ref=o_refs[i], send_sem=send_sems.at[i],
                                              recv_sem=recv_sems.at[i], device_id=(x, y, 1 - c), device_id_type=MESH)
            cp.start()
            cps.append(cp)
        for cp in cps:
            cp.wait()

    return pl.pallas_call(
        body, name="rs_pair_share", in_specs=[ANY] * n, out_specs=[ANY] * n,
        out_shape=[jax.ShapeDtypeStruct(h.shape, F32) for h in halves],
        scratch_shapes=[pltpu.SemaphoreType.DMA((n,)), pltpu.SemaphoreType.DMA((n,))],
    )(*halves)


HBM = pl.BlockSpec(memory_space=pltpu.HBM)
SEM = pl.BlockSpec(memory_space=pltpu.SEMAPHORE)


class _SplitExchange:
    def __init__(self, name, srcs, land_shapes, src_of, dst_of, arrive_of):
        self.name, self.srcs, self.land_shapes = name, list(srcs), list(land_shapes)
        self.src_of, self.dst_of, self.arrive_of = src_of, dst_of, arrive_of

    def _copies(self, src_refs, land_refs, send_sems, recv_sems):
        x, y, c = _mesh_pos()
        p = 2 * x + y
        out = []
        for i, (src, land) in enumerate(zip(src_refs, land_refs)):
            for k, chip in enumerate(_other_chips(x, y)):
                pk = 2 * chip[0] + chip[1]
                sems = dict(send_sem=send_sems.at[3 * i + k], recv_sem=recv_sems.at[3 * i + k],
                            device_id=(*chip, c), device_id_type=MESH)
                sent = pltpu.make_async_remote_copy(src_ref=self.src_of(src, k, p, pk),
                                                    dst_ref=self.dst_of(land, k, p, pk), **sems)
                here = self.arrive_of(land, k, p, pk)
                out.append((sent, pltpu.make_async_remote_copy(src_ref=here, dst_ref=here, **sems)))
        return out

    def start(self):
        n = len(self.srcs)

        def body(*refs):
            for sent, _ in self._copies(refs[:n], refs[n:2 * n], refs[2 * n], refs[2 * n + 1]):
                sent.start()
            refs[-1][...] = jnp.zeros((8, LANES), F32)

        lands = [lax.empty(shape, src.dtype) for shape, src in zip(self.land_shapes, self.srcs)]
        operands = [pltpu.with_memory_space_constraint(a, pltpu.HBM) for a in self.srcs + lands]
        outs = pl.pallas_call(
            body, name=self.name + "_start",
            out_shape=(pltpu.SemaphoreType.DMA((3 * n,)), pltpu.SemaphoreType.DMA((3 * n,)),
                       *[pltpu.HBM(a.shape, a.dtype) for a in operands], jax.ShapeDtypeStruct((8, LANES), F32)),
            in_specs=[HBM] * (2 * n), out_specs=(SEM, SEM, *[HBM] * (2 * n), pl.BlockSpec(memory_space=pltpu.VMEM)),
            input_output_aliases={j: 2 + j for j in range(2 * n)},
            compiler_params=pltpu.CompilerParams(has_side_effects=pltpu.SideEffectType.DATAFLOW_SIDE_EFFECTING),
        )(*operands)
        self._sems, self._thru = outs[:2], list(outs[2:2 + 2 * n])
        return outs[-1]

    def wait(self, after):
        n = len(self.srcs)

        def body(*refs):
            for sent, arrived in self._copies(refs[:n], refs[n:2 * n], refs[2 * n], refs[2 * n + 1]):
                sent.wait_send()
                arrived.wait_recv()

        outs = pl.pallas_call(
            body, name=self.name + "_wait",
            out_shape=tuple(pltpu.HBM(a.shape, a.dtype) for a in self._thru),
            in_specs=[HBM] * (2 * n) + [SEM, SEM, ANY], out_specs=tuple([HBM] * (2 * n)),
            input_output_aliases={j: j for j in range(2 * n)},
            compiler_params=pltpu.CompilerParams(has_side_effects=pltpu.SideEffectType.DATAFLOW_SIDE_EFFECTING),
        )(*self._thru, *self._sems, after)
        return list(outs[n:])


def _small_allreduce(parts):
    n = len(parts)

    def body(*refs):
        in_refs, out_refs, gather_refs = refs[:n], refs[n:2 * n], refs[2 * n:3 * n]
        send_sems, recv_sems = refs[3 * n:]
        x, y, c = _mesh_pos()
        me = 4 * x + 2 * y + c
        cps = []
        for i in range(n):
            gather_refs[i][me] = in_refs[i][...]
            for j in range(1, 8):
                peer = (x ^ (j >> 2), y ^ ((j >> 1) & 1), c ^ (j & 1))
                cp = pltpu.make_async_remote_copy(
                    src_ref=in_refs[i], dst_ref=gather_refs[i].at[me], send_sem=send_sems.at[7 * i + j - 1],
                    recv_sem=recv_sems.at[7 * i + j - 1], device_id=peer, device_id_type=MESH)
                cp.start()
                cps.append(cp)
        for i in range(n):
            for j in range(1, 8):
                peer_id = 4 * (x ^ (j >> 2)) + 2 * (y ^ ((j >> 1) & 1)) + (c ^ (j & 1))
                slot = gather_refs[i].at[peer_id]
                pltpu.make_async_remote_copy(src_ref=slot, dst_ref=slot, send_sem=send_sems.at[7 * i + j - 1],
                                             recv_sem=recv_sems.at[7 * i + j - 1], device_id=(x, y, c),
                                             device_id_type=MESH).wait_recv()
        for cp in cps:
            cp.wait_send()
        for i in range(n):
            tot = gather_refs[i][0]
            for d in range(1, 8):
                tot = tot + gather_refs[i][d]
            out_refs[i][...] = tot

    vmem = pl.BlockSpec(memory_space=pltpu.VMEM)
    return pl.pallas_call(
        body, name="small_allreduce", in_specs=[vmem] * n, out_specs=[vmem] * n,
        out_shape=[jax.ShapeDtypeStruct(p.shape, F32) for p in parts],
        scratch_shapes=[pltpu.VMEM((8,) + p.shape, F32) for p in parts]
        + [pltpu.SemaphoreType.DMA((7 * n,)), pltpu.SemaphoreType.DMA((7 * n,))],
    )(*parts)


def _adamw_small(ws, gs, ms, vs):
    n = len(ws)

    def body(*refs):
        for i in range(n):
            w_ref, g_ref, m_ref, v_ref = refs[i], refs[n + i], refs[2 * n + i], refs[3 * n + i]
            d_ref, nm_ref, nv_ref = refs[4 * n + i], refs[5 * n + i], refs[6 * n + i]
            d_ref[...], nm_ref[...], nv_ref[...] = _adamw_math(w_ref[...], g_ref[...], m_ref[...], v_ref[...])

    vmem = pl.BlockSpec(memory_space=pltpu.VMEM)
    shapes = [jax.ShapeDtypeStruct(w.shape, F32) for w in ws]
    outs = pl.pallas_call(body, name="adamw_small", in_specs=[vmem] * (4 * n), out_specs=[vmem] * (3 * n),
                          out_shape=shapes * 3)(*ws, *gs, *ms, *vs)
    return outs[:n], outs[n:2 * n], outs[2 * n:]


def _pad_w_in(w):
    z = jnp.zeros((w.shape[0], 64), w.dtype)
    return jnp.concatenate([w[:, 448:1472], w[:, 1984:3008], w[:, 3008:4032], w[:, 0:256], w[:, 1472:1728],
                            w[:, 1728:1984], w[:, 256:384], w[:, 384:448], z], axis=1)


def _unpad_w_in(p):
    return jnp.concatenate([p[:, P_QLAT:P_QLAT + 256], p[:, P_CKV:P_CKV + 128], p[:, P_KR:P_KR + 64],
                            p[:, P_QB:P_QB + 1024], p[:, P_KB:P_KB + 256], p[:, P_VB:P_VB + 256],
                            p[:, P_GA:P_GA + 1024], p[:, P_GB:P_GB + 1024]], axis=1)


def _col_shards(w):
    r, c4 = w.shape
    return w.reshape(r, 4, c4 // 4).transpose(1, 0, 2)


def _local_step(x, positions, target, norm1_g, w_in_p, q_a_norm_g, wq, kv_a_norm_g, wkv, rel_bias, sinks,
                late_weights, norm2_g, conv_w, conv_b, final_norm_g, early_grads=None, last_grads=None):
    s = x.shape[0]
    half = QK_ROPE // 2
    inv_freq = jnp.asarray(np.float32(ROPE_THETA) ** (-np.arange(half, dtype=np.float32) / np.float32(half)))
    ang = positions.astype(F32)[:, None] * inv_freq[None, :]
    cos, sin = jnp.cos(ang), jnp.sin(ang)
    z64 = jnp.zeros((s, 64), F32)
    cos_t = jnp.concatenate([cos, cos, z64], axis=1)
    sin_t = jnp.concatenate([-sin, sin, z64], axis=1)
    bucket = _t5_bucket_table()
    sinks1 = sinks.reshape(H_B)

    h1, rstd1 = _rmsnorm_fwd("norm1_fwd", x, norm1_g, D_MODEL, 0)
    proj, proj_b = _matmul("proj", h1, w_in_p, out_shape=(s, W_IN_PAD), out_dtype=F32, grid=(s // MM_ROWS, W_IN_PAD // 1024, 1),
                           a_spec=_bs((MM_ROWS, D_MODEL), lambda i, j, k: (i, 0)), b_spec=_bs((D_MODEL, 1024), lambda i, j, k: (0, j)),
                           o_spec=_bs((MM_ROWS, 1024), lambda i, j, k: (i, j)), contract=NN, bf16_copy=True)
    qn, cn, rstd_q, rstd_c = _lat_norms(proj, q_a_norm_g, kv_a_norm_g)
    q = _q_heads(qn, wq, cos_t, sin_t)
    k, v = _kv_heads(cn, wkv, proj, cos_t, sin_t)
    o_a, lse_a = _mla_fwd(q, k, v)

    bias = _win_bias(bucket, rel_bias)
    o_b, lse_b = _win_fwd(proj_b, bias, sinks1)

    mixed = _gate_fwd(proj, o_a, o_b)
    w_out, w_up, w_down = late_weights(mixed)
    row512 = lambda w: _bs((MM_ROWS, w), lambda i, j, k: (i, 0))
    whole = lambda r, c: _bs((r, c), lambda i, j, k: (0, 0))
    x1 = _matmul("attn_out", mixed, w_out, out_shape=(s, D_MODEL), out_dtype=F32, grid=(s // MM_ROWS, 1, 1),
                 a_spec=row512(D_MODEL), b_spec=whole(D_MODEL, D_MODEL), o_spec=row512(D_MODEL), contract=NN, add=x)
    h2, rstd2 = _rmsnorm_fwd("norm2_fwd", x1, norm2_g, D_MODEL, 0)
    u = _matmul("ffn_up", h2, w_up, out_shape=(s, 2 * D_FF), out_dtype=F32, grid=(s // MM_ROWS, 4, 1),
                a_spec=_bs((MM_ROWS, D_MODEL), lambda i, j, k: (i, 0)), b_spec=_bs((D_MODEL, D_FF // 2), lambda i, j, k: (0, j)),
                o_spec=_bs((MM_ROWS, D_FF // 2), lambda i, j, k: (i, j)), contract=NN)
    f = _convffn_fwd(u, conv_w, conv_b)
    x2 = _matmul("ffn_down", f, w_down, out_shape=(s, D_MODEL), out_dtype=F32, grid=(s // MM_ROWS, 1, 1),
                 a_spec=row512(D_FF), b_spec=whole(D_FF, D_MODEL), o_spec=row512(D_MODEL), contract=NN, add=x1)
    loss, dx2, d_final_g, dx2_b = _final_loss(x2, target, final_norm_g.reshape(1, D_MODEL))
    tk = min(s, DW_ROWS)

    df = _matmul("ffn_down_dx", dx2_b, w_down, out_shape=(s, D_FF), out_dtype=F32, grid=(s // MM_ROWS, 2, 1),
                 a_spec=row512(D_MODEL), b_spec=_bs((D_FF // 2, D_MODEL), lambda i, j, k: (j, 0)),
                 o_spec=_bs((MM_ROWS, D_FF // 2), lambda i, j, k: (i, j)), contract=NT)
    d_w_down = _matmul("ffn_down_dw", f, dx2_b, out_shape=(D_FF, D_MODEL), out_dtype=F32, grid=(2, 1, s // tk),
                       a_spec=_bs((tk, D_FF // 2), lambda i, j, k: (k, i)), b_spec=_bs((tk, D_MODEL), lambda i, j, k: (k, 0)),
                       o_spec=_bs((D_FF // 2, D_MODEL), lambda i, j, k: (i, 0)), contract=TN)
    du, d_conv_w2, d_conv_b2 = _convffn_bwd(u, conv_w, conv_b, df)
    kc = D_FF // 2
    dh2 = _matmul("ffn_up_dx", du, w_up, out_shape=(s, D_MODEL), out_dtype=F32, grid=(s // 1024, 1, 4),
                  a_spec=_bs((None, 1024, kc), lambda i, j, k: (k // 2, i, k % 2)),
                  b_spec=_bs((D_MODEL, kc), lambda i, j, k: (0, k)),
                  o_spec=_bs((1024, D_MODEL), lambda i, j, k: (i, 0)), contract=NT)
    d_w_up = _matmul("ffn_up_dw", h2, du, out_shape=(D_MODEL, 2 * D_FF), out_dtype=F32, grid=(1, 4, s // tk),
                     a_spec=_bs((tk, D_MODEL), lambda i, j, k: (k, 0)),
                     b_spec=_bs((None, tk, kc), lambda i, j, k: (j // 2, k, j % 2)),
                     o_spec=_bs((D_MODEL, kc), lambda i, j, k: (0, j)), contract=TN)
    dx1, d_norm2_g, dx1_b = _rmsnorm_bwd("norm2_bwd", dh2, x1, rstd2, norm2_g, D_MODEL, 0, F32, res=dx2, bf16_copy=True)

    d_w_out = _matmul("attn_out_dw", mixed, dx1_b, out_shape=(D_MODEL, D_MODEL), out_dtype=F32, grid=(1, 1, s // tk),
                      a_spec=_bs((tk, D_MODEL), lambda i, j, k: (k, 0)), b_spec=_bs((tk, D_MODEL), lambda i, j, k: (k, 0)),
                      o_spec=whole(D_MODEL, D_MODEL), contract=TN)
    if early_grads is not None:
        token = early_grads(d_w_out, d_w_up, d_w_down)
        if token is not None:
            sinks1 = sinks1 + token[0, :H_B]
    dmixed = _matmul("attn_out_dx", dx1_b, w_out, out_shape=(s, D_MODEL), out_dtype=F32, grid=(s // MM_ROWS, 1, 1),
                     a_spec=row512(D_MODEL), b_spec=whole(D_MODEL, D_MODEL), o_spec=row512(D_MODEL), contract=NT)
    do_a, do_b, d_ga, d_gb = _gate_bwd(dmixed, proj, o_a, o_b)

    d_qb, dk_pad, dv_pad, dbias, dsink_rows = _win_bwd(proj_b, bias, sinks1, do_b, lse_b)
    wp = _win_param_grads(bucket, dbias, dsink_rows)[:, 0, :]
    d_rel_bias = wp[:, :NUM_BUCKETS].T
    d_sinks = wp[:, NUM_BUCKETS].reshape(1, H_B)
    d_kb = dk_pad[WINDOW:WINDOW + s].astype(BF16)
    d_vb = dv_pad[WINDOW:WINDOW + s].astype(BF16)

    dq, dk, dv = _mla_bwd(q, k, v, do_a, o_a, lse_a)
    dq_pre, dkv_pre, d_kr = _mla_bwd_prep(dq, dk, dv, cos_t, sin_t)
    th = min(s, HEAD_ROWS)
    hgrid = (s // th, 1, H_A)
    hblock = _bs((None, th, HEAD_PAD), lambda i, j, k: (k, i, 0))
    hrows = lambda w: _bs((th, w), lambda i, j, k: (i, 0))
    dqn = _matmul("q_up_dx", dq_pre, wq, out_shape=(s, Q_LORA), out_dtype=F32, grid=hgrid, a_spec=hblock,
                  b_spec=_bs((None, Q_LORA, HEAD_PAD), lambda i, j, k: (k, 0, 0)), o_spec=hrows(Q_LORA), contract=NT)
    dcn = _matmul("kv_up_dx", dkv_pre, wkv, out_shape=(s, KV_LORA), out_dtype=F32, grid=hgrid, a_spec=hblock,
                  b_spec=_bs((None, KV_LORA, HEAD_PAD), lambda i, j, k: (k, 0, 0)), o_spec=hrows(KV_LORA), contract=NT)
    wgrid = (H_A, 1, s // th)
    d_wq = _matmul("q_up_dw", qn, dq_pre, out_shape=(H_A, Q_LORA, HEAD_PAD), out_dtype=F32, grid=wgrid,
                   a_spec=_bs((th, Q_LORA), lambda i, j, k: (k, 0)), b_spec=_bs((None, th, HEAD_PAD), lambda i, j, k: (i, k, 0)),
                   o_spec=_bs((None, Q_LORA, HEAD_PAD), lambda i, j, k: (i, 0, 0)), contract=TN)
    d_wkv = _matmul("kv_up_dw", cn, dkv_pre, out_shape=(H_A, KV_LORA, HEAD_PAD), out_dtype=F32, grid=wgrid,
                    a_spec=_bs((th, KV_LORA), lambda i, j, k: (k, 0)), b_spec=_bs((None, th, HEAD_PAD), lambda i, j, k: (i, k, 0)),
                    o_spec=_bs((None, KV_LORA, HEAD_PAD), lambda i, j, k: (i, 0, 0)), contract=TN)
    d_qlat, d_gq = _rmsnorm_bwd("q_norm_bwd", dqn, proj, rstd_q, q_a_norm_g, Q_LORA, P_QLAT // Q_LORA, BF16)
    d_ckv, d_gkv = _rmsnorm_bwd("kv_norm_bwd", dcn, proj, rstd_c, kv_a_norm_g, KV_LORA, P_CKV // KV_LORA, BF16)

    dproj = jnp.concatenate([d_qb, d_ga, d_gb, d_qlat, d_kb, d_vb, d_ckv, d_kr], axis=1)
    d_w_in_p = _matmul("proj_dw", h1, dproj, out_shape=(D_MODEL, W_IN_PAD), out_dtype=F32, grid=(1, W_IN_PAD // 1024, s // tk),
                       a_spec=_bs((tk, D_MODEL), lambda i, j, k: (k, 0)), b_spec=_bs((tk, 1024), lambda i, j, k: (k, j)),
                       o_spec=_bs((D_MODEL, 1024), lambda i, j, k: (0, j)), contract=TN)
    token = last_grads(d_w_in_p, d_wq, d_wkv) if last_grads is not None else None
    dh1 = _matmul("proj_dx", dproj, w_in_p, out_shape=(s, D_MODEL), out_dtype=F32, grid=(s // 1024, 1, W_IN_PAD // 1024),
                  a_spec=_bs((1024, 1024), lambda i, j, k: (i, k)), b_spec=_bs((D_MODEL, 1024), lambda i, j, k: (0, k)),
                  o_spec=_bs((1024, D_MODEL), lambda i, j, k: (i, 0)), contract=NT, after=token)
    dx, d_norm1_g = _rmsnorm_bwd("norm1_bwd", dh1, x, rstd1, norm1_g, D_MODEL, 0, F32, res=dx1)

    grads = dict(
        norm1_g=d_norm1_g, w_in_p=d_w_in_p, q_a_norm_g=d_gq, wq=d_wq, kv_a_norm_g=d_gkv, wkv=d_wkv,
        rel_bias=d_rel_bias, sinks=d_sinks, w_out=d_w_out, norm2_g=d_norm2_g, w_up=d_w_up,
        conv_w=jnp.concatenate([d_conv_w2[0], d_conv_w2[1]], axis=1),
        conv_b=jnp.concatenate([d_conv_b2[0], d_conv_b2[1]], axis=1),
        w_down=d_w_down, final_norm_g=d_final_g.reshape(D_MODEL))
    return loss, dx, grads


def _wq_heads(w_q_b):
    w = w_q_b.reshape(Q_LORA, H_A, QK_NOPE + QK_ROPE).transpose(1, 0, 2)
    return jnp.pad(w, ((0, 0), (0, 0), (0, HEAD_PAD - QK_NOPE - QK_ROPE)))


def _wq_unheads(d_wq):
    return d_wq[:, :, :QK_NOPE + QK_ROPE].transpose(1, 0, 2).reshape(Q_LORA, H_A * (QK_NOPE + QK_ROPE))


def _wkv_heads(w_kv_b):
    return w_kv_b.reshape(KV_LORA, H_A, QK_NOPE + V_DIM).transpose(1, 0, 2)


def _wkv_unheads(d_wkv):
    return d_wkv.transpose(1, 0, 2).reshape(KV_LORA, H_A * (QK_NOPE + V_DIM))


SMALL = ("norm1_g", "q_a_norm_g", "kv_a_norm_g", "rel_bias", "sinks", "norm2_g", "conv_b", "final_norm_g")
FIRST = ("w_in", "w_q_b", "w_kv_b")
LATER = ("w_out", "w_up", "w_down")
BIG = FIRST + LATER


def kernel(x, positions, norm1_g, w_in, q_a_norm_g, w_q_b, kv_a_norm_g, w_kv_b, rel_bias, sinks, w_out, norm2_g, w_up, conv_w, conv_b, w_down, final_norm_g, loss_target, m_norm1_g, m_w_in, m_q_a_norm_g, m_w_q_b, m_kv_a_norm_g, m_w_kv_b, m_rel_bias, m_sinks, m_w_out, m_norm2_g, m_w_up, m_conv_w, m_conv_b, m_w_down, m_final_norm_g, v_norm1_g, v_w_in, v_q_a_norm_g, v_w_q_b, v_kv_a_norm_g, v_w_kv_b, v_rel_bias, v_sinks, v_w_out, v_norm2_g, v_w_up, v_conv_w, v_conv_b, v_w_down, v_final_norm_g):
    weights = dict(norm1_g=norm1_g, w_in=w_in, q_a_norm_g=q_a_norm_g, w_q_b=w_q_b, kv_a_norm_g=kv_a_norm_g,
                   w_kv_b=w_kv_b, rel_bias=rel_bias, sinks=sinks, w_out=w_out, norm2_g=norm2_g, w_up=w_up,
                   conv_w=conv_w, conv_b=conv_b, w_down=w_down, final_norm_g=final_norm_g)
    mom_m = dict(norm1_g=m_norm1_g, w_in=m_w_in, q_a_norm_g=m_q_a_norm_g, w_q_b=m_w_q_b, kv_a_norm_g=m_kv_a_norm_g,
                 w_kv_b=m_w_kv_b, rel_bias=m_rel_bias, sinks=m_sinks, w_out=m_w_out, norm2_g=m_norm2_g, w_up=m_w_up,
                 conv_w=m_conv_w, conv_b=m_conv_b, w_down=m_w_down, final_norm_g=m_final_norm_g)
    mom_v = dict(norm1_g=v_norm1_g, w_in=v_w_in, q_a_norm_g=v_q_a_norm_g, w_q_b=v_w_q_b, kv_a_norm_g=v_kv_a_norm_g,
                 w_kv_b=v_w_kv_b, rel_bias=v_rel_bias, sinks=v_sinks, w_out=v_w_out, norm2_g=v_norm2_g, w_up=v_w_up,
                 conv_w=v_conv_w, conv_b=v_conv_b, w_down=v_w_down, final_norm_g=v_final_norm_g)
    shard2d = {n: weights[n][0] for n in BIG}
    conv_w_shard = conv_w[0]
    xi, yi, ci = lax.axis_index("x"), lax.axis_index("y"), lax.axis_index("c")
    chip = (2 * xi + yi).astype(jnp.int32)

    core = ci.astype(jnp.int32).reshape(1)
    chip1 = chip.reshape(1)
    cat_cols = lambda a: jnp.concatenate([a[0], a[1], a[2], a[3]], axis=1)
    own_slot = lambda a, own: lax.dynamic_update_index_in_dim(a, own, chip, 0)
    halved = lambda a: a.reshape((2, a.shape[0] // 2) + a.shape[1:])
    quartered = lambda a: a.reshape(4, 2, a.shape[1] // 2, a.shape[2])

    send = [halved(shard2d[n].astype(BF16)) for n in FIRST] + [conv_w_shard]
    gathered = [own_slot(a, own) for a, own in zip(_allgather_weights(send, split=[True] * len(FIRST) + [False]), send)]
    g = {n: a.reshape((4,) + shard2d[n].shape) for n, a in zip(FIRST, gathered)}
    w_in_p = _pad_w_in(cat_cols(g["w_in"]))
    wq = _wq_heads(cat_cols(g["w_q_b"]))
    wkv = _wkv_heads(cat_cols(g["w_kv_b"]))
    conv_w_f = cat_cols(gathered[-1])

    later = [shard2d[n].astype(BF16) for n in LATER]
    gather2 = _SplitExchange("gather_later", later, [(4,) + a.shape for a in later],
                             src_of=lambda ref, k, p, pk: ref, dst_of=lambda ref, k, p, pk: ref.at[p],
                             arrive_of=lambda ref, k, p, pk: ref.at[pk])
    norm1_g_in = norm1_g + gather2.start()[:1, :1]

    def late_weights(after):
        w_out_g, w_up_g, w_down_g = [own_slot(a, own) for a, own in zip(gather2.wait(after), later)]
        return w_out_g.reshape(D_MODEL, D_MODEL), cat_cols(w_up_g), w_down_g.reshape(D_FF, D_MODEL)

    early = {}

    def early_grads(d_w_out, d_w_up, d_w_down):
        grads = [quartered(d_w_out.reshape(4, D_MODEL // 4, D_MODEL)), quartered(_col_shards(d_w_up)),
                 quartered(d_w_down.reshape(4, D_FF // 4, D_MODEL))]
        recv = _rs_pair_exchange("rs_pair_exchange_early", grads)
        early["pairs"] = [_rs_pair_add(f"rs_pair_add_{n}", core, gfull, r) for n, gfull, r in zip(LATER, grads, recv)]
        early["ici"] = _SplitExchange("rs_ici_early", early["pairs"], [(3,) + a.shape[1:] for a in early["pairs"]],
                                      src_of=lambda ref, k, p, pk: ref.at[pk], dst_of=lambda ref, k, p, pk: ref.at[k],
                                      arrive_of=lambda ref, k, p, pk: ref.at[k])
        return early["ici"].start()

    last = {}

    def last_grads(d_w_in_p, d_wq, d_wkv):
        grads = [quartered(_col_shards(_unpad_w_in(d_w_in_p))), quartered(_col_shards(_wq_unheads(d_wq))),
                 quartered(_col_shards(_wkv_unheads(d_wkv)))]
        recv = _rs_pair_exchange("rs_pair_exchange_last", grads)
        last["pairs"] = [_rs_pair_add(f"rs_pair_add_{n}", core, gfull, r) for n, gfull, r in zip(FIRST, grads, recv)]
        last["ici"] = _SplitExchange("rs_ici_last", last["pairs"], [(3,) + a.shape[1:] for a in last["pairs"]],
                                     src_of=lambda ref, k, p, pk: ref.at[pk], dst_of=lambda ref, k, p, pk: ref.at[k],
                                     arrive_of=lambda ref, k, p, pk: ref.at[k])
        return last["ici"].start()

    loss, dx, gr = _local_step(x[0], positions, loss_target[0], norm1_g_in, w_in_p, q_a_norm_g, wq, kv_a_norm_g, wkv,
                               rel_bias, sinks, late_weights, norm2_g, conv_w_f, conv_b, final_norm_g, early_grads,
                               last_grads)

    recv2 = last["ici"].wait(dx) + early["ici"].wait(dx)
    pairs = last["pairs"] + early["pairs"]
    halves = [_rs_final_add(f"rs_final_add_{n}", chip1, pr, r) for n, pr, r in zip(FIRST + LATER, pairs, recv2)]
    sibling_halves = _rs_pair_share(halves)

    as_rows = lambda a: a.reshape((-1, a.shape[-1]))
    summed = _small_allreduce([as_rows(gr[n]) for n in SMALL] + [gr["conv_w"], loss])
    small_g = dict(zip(SMALL, summed[:len(SMALL)]))
    conv_w_g = lax.dynamic_slice_in_dim(summed[len(SMALL)], chip * (2 * D_FF // 4), 2 * D_FF // 4, axis=1)
    loss_out = summed[-1].reshape(())

    out_g, out_d, out_m, out_v = {}, {}, {}, {}
    for n, mine, theirs in zip(FIRST + LATER, halves, sibling_halves):
        gsh, d, nm, nv = _adamw_halves(f"adamw_{n}", core, shard2d[n], mine, theirs, mom_m[n][0], mom_v[n][0])
        out_g[n], out_d[n], out_m[n], out_v[n] = gsh[None], d[None], nm[None], nv[None]
    names = SMALL + ("conv_w",)
    sg = [small_g[n] for n in SMALL] + [conv_w_g]
    ds, nms, nvs = _adamw_small([as_rows(weights[n]) for n in names], sg, [as_rows(mom_m[n]) for n in names],
                                [as_rows(mom_v[n]) for n in names])
    for n, gg, dd, mm, vv in zip(names, sg, ds, nms, nvs):
        shp = weights[n].shape
        out_g[n], out_d[n], out_m[n], out_v[n] = gg.reshape(shp), dd.reshape(shp), mm.reshape(shp), vv.reshape(shp)

    order = ("norm1_g", "w_in", "q_a_norm_g", "w_q_b", "kv_a_norm_g", "w_kv_b", "rel_bias", "sinks", "w_out",
             "norm2_g", "w_up", "conv_w", "conv_b", "w_down", "final_norm_g")
    return (loss_out, dx[None], *[out_g[n] for n in order], *[out_d[n] for n in order],
            *[out_m[n] for n in order], *[out_v[n] for n in order])
```

```python
import functools
import math

import jax
import jax.numpy as jnp
import numpy as np
from jax import lax
from jax.experimental import pallas as pl
from jax.experimental.pallas import tpu as pltpu

F32 = jnp.float32
BF16 = jnp.bfloat16
MESH = pl.DeviceIdType.MESH

D_MODEL = 1024
EPS = 1e-6
H_A = 8
QK_NOPE = 128
QK_ROPE = 64
V_DIM = 128
Q_LORA = 256
KV_LORA = 128
ROPE_THETA = 10000.0
H_B = 16
KV_B = 4
GROUP = 4
HD_B = 64
WINDOW = 128
Q_BLOCK = 128
NUM_BUCKETS = 32
MAX_DISTANCE = 128
D_FF = 2816
HEAD_PAD = 256

ADAM_LR = 0.001
ADAM_B1 = 0.9
ADAM_B2 = 0.999
ADAM_EPS = 1e-08
ADAM_WD = 0.01
ADAM_STEP = 10

LANES = 128
P_QB, P_GA, P_GB, P_QLAT, P_KB, P_VB, P_CKV, P_KR = 0, 1024, 2048, 3072, 3328, 3584, 3840, 3968
W_IN_PAD = 4096

NT = (((1,), (1,)), ((), ()))
NN = (((1,), (0,)), ((), ()))
TN = (((0,), (0,)), ((), ()))


def _arb(n):
    return pltpu.CompilerParams(dimension_semantics=("arbitrary",) * n)


def _matmul(name, a, b, *, out_shape, out_dtype, grid, a_spec, b_spec, o_spec, contract, add=None, bf16_copy=False,
            after=None):
    nk = grid[2]
    acc_shape = tuple(d for d in o_spec.block_shape if d is not None)
    n_in = 2 + (add is not None) + (after is not None)
    n_out = 2 if bf16_copy else 1

    def body(*refs):
        a_ref, b_ref = refs[:2]
        add_ref = refs[2] if add is not None else None
        o_refs = refs[n_in:n_in + n_out]
        scratch = refs[n_in + n_out:]
        prod = lax.dot_general(a_ref[...].astype(BF16), b_ref[...].astype(BF16), contract,
                               preferred_element_type=F32)

        def finish(val):
            if add_ref is not None:
                val = add_ref[...] + val
            o_refs[0][...] = val.astype(out_dtype)
            if bf16_copy:
                o_refs[1][...] = val.astype(BF16)

        if nk == 1:
            finish(prod)
        else:
            acc_ref = scratch[0]
            k = pl.program_id(2)

            @pl.when(k == 0)
            def _():
                acc_ref[...] = prod

            @pl.when((k > 0) & (k < nk - 1))
            def _():
                acc_ref[...] += prod

            @pl.when(k == nk - 1)
            def _():
                finish(acc_ref[...] + prod)

    in_specs = [a_spec, b_spec]
    args = [a, b]
    if add is not None:
        in_specs.append(o_spec)
        args.append(add)
    if after is not None:
        in_specs.append(pl.BlockSpec(memory_space=pl.ANY))
        args.append(after)
    out_shapes = [jax.ShapeDtypeStruct(out_shape, out_dtype)]
    if bf16_copy:
        out_shapes.append(jax.ShapeDtypeStruct(out_shape, BF16))
    res = pl.pallas_call(
        body, name=name, grid=grid, in_specs=in_specs, out_specs=[o_spec] * n_out, out_shape=out_shapes,
        scratch_shapes=[pltpu.VMEM(acc_shape, F32)] if nk > 1 else [],
        compiler_params=_arb(3),
    )(*args)
    return res if bf16_copy else res[0]


def _bs(block, fn):
    return pl.BlockSpec(block, fn)


def _rmsnorm_fwd(name, src, g, d, cb, ts=512):
    s = src.shape[0]

    def body(x_ref, g_ref, h_ref, r_ref):
        x = x_ref[...]
        r = lax.rsqrt(jnp.mean(x * x, axis=-1, keepdims=True) + EPS)
        h_ref[...] = (x * r * g_ref[...]).astype(BF16)
        r_ref[...] = r

    return pl.pallas_call(
        body, name=name, grid=(s // ts,),
        in_specs=[_bs((ts, d), lambda i: (i, cb)), _bs((1, d), lambda i: (0, 0))],
        out_specs=[_bs((ts, d), lambda i: (i, 0)), _bs((ts, 1), lambda i: (i, 0))],
        out_shape=[jax.ShapeDtypeStruct((s, d), BF16), jax.ShapeDtypeStruct((s, 1), F32)],
        compiler_params=_arb(1),
    )(src, g)


def _rmsnorm_bwd(name, dy, src, rstd, g, d, cb, out_dtype, res=None, bf16_copy=False, ts=512):
    s = src.shape[0]

    def body(*refs):
        dy_ref, x_ref, r_ref, g_ref = refs[:4]
        res_ref = refs[4] if res is not None else None
        dx_ref, dg_ref = refs[n_in:n_in + 2]
        dyv = dy_ref[...]
        r = r_ref[...]
        xhat = x_ref[...] * r
        dyh = dyv * g_ref[...]
        c = jnp.mean(dyh * xhat, axis=-1, keepdims=True)
        dx = r * (dyh - xhat * c)
        if res_ref is not None:
            dx = res_ref[...] + dx
        dx_ref[...] = dx.astype(out_dtype)
        if bf16_copy:
            refs[n_in + 2][...] = dx.astype(BF16)
        part = jnp.sum(dyv * xhat, axis=0, keepdims=True)

        @pl.when(pl.program_id(0) == 0)
        def _():
            dg_ref[...] = part

        @pl.when(pl.program_id(0) > 0)
        def _():
            dg_ref[...] += part

    in_specs = [_bs((ts, d), lambda i: (i, 0)), _bs((ts, d), lambda i: (i, cb)),
                _bs((ts, 1), lambda i: (i, 0)), _bs((1, d), lambda i: (0, 0))]
    args = [dy, src, rstd, g]
    if res is not None:
        in_specs.append(_bs((ts, d), lambda i: (i, 0)))
        args.append(res)
    n_in = len(args)
    out_specs = [_bs((ts, d), lambda i: (i, 0)), _bs((1, d), lambda i: (0, 0))]
    out_shape = [jax.ShapeDtypeStruct((s, d), out_dtype), jax.ShapeDtypeStruct((1, d), F32)]
    if bf16_copy:
        out_specs.append(_bs((ts, d), lambda i: (i, 0)))
        out_shape.append(jax.ShapeDtypeStruct((s, d), BF16))
    return pl.pallas_call(
        body, name=name, grid=(s // ts,), in_specs=in_specs, out_specs=out_specs, out_shape=out_shape,
        compiler_params=_arb(1),
    )(*args)


def _final_loss(x2, target, g, ts=512):
    s, d = x2.shape

    def body(x_ref, t_ref, g_ref, loss_ref, dx_ref, dg_ref, dxb_ref):
        x = x_ref[...]
        r = lax.rsqrt(jnp.mean(x * x, axis=-1, keepdims=True) + EPS)
        xhat = x * r
        gv = g_ref[...]
        err = xhat * gv - t_ref[...]
        lpart = 0.5 * jnp.sum(jnp.mean(err * err, axis=-1, keepdims=True), axis=0, keepdims=True)
        dyv = err * (1.0 / d)
        dyh = dyv * gv
        c = jnp.mean(dyh * xhat, axis=-1, keepdims=True)
        dx = r * (dyh - xhat * c)
        dx_ref[...] = dx
        dxb_ref[...] = dx.astype(BF16)
        gpart = jnp.sum(dyv * xhat, axis=0, keepdims=True)

        @pl.when(pl.program_id(0) == 0)
        def _():
            dg_ref[...] = gpart
            loss_ref[...] = lpart

        @pl.when(pl.program_id(0) > 0)
        def _():
            dg_ref[...] += gpart
            loss_ref[...] += lpart

    return pl.pallas_call(
        body, name="final_loss", grid=(s // ts,),
        in_specs=[_bs((ts, d), lambda i: (i, 0)), _bs((ts, d), lambda i: (i, 0)), _bs((1, d), lambda i: (0, 0))],
        out_specs=[_bs((1, 1), lambda i: (0, 0)), _bs((ts, d), lambda i: (i, 0)), _bs((1, d), lambda i: (0, 0)),
                   _bs((ts, d), lambda i: (i, 0))],
        out_shape=[jax.ShapeDtypeStruct((1, 1), F32), jax.ShapeDtypeStruct((s, d), F32),
                   jax.ShapeDtypeStruct((1, d), F32), jax.ShapeDtypeStruct((s, d), BF16)],
        compiler_params=_arb(1),
    )(x2, target, g)


def _swap_halves(t):
    lane = lax.broadcasted_iota(jnp.int32, t.shape, 1)
    return jnp.where(lane < 32, pltpu.roll(t, 96, 1), pltpu.roll(t, 32, 1))


def _rope_fwd(t, cos_t, sin_t):
    return t * cos_t + _swap_halves(t) * sin_t


def _rope_bwd(dt, cos_t, sin_t):
    return dt * cos_t - _swap_halves(dt) * sin_t


def _lat_norms(proj, gq, gkv, ts=512):
    s = proj.shape[0]

    def body(q_ref, c_ref, gq_ref, gkv_ref, qn_ref, cn_ref, rq_ref, rc_ref):
        q = q_ref[...]
        rq = lax.rsqrt(jnp.mean(q * q, axis=-1, keepdims=True) + EPS)
        qn_ref[...] = (q * rq * gq_ref[...]).astype(BF16)
        rq_ref[...] = rq
        cv = c_ref[...]
        rc = lax.rsqrt(jnp.mean(cv * cv, axis=-1, keepdims=True) + EPS)
        cn_ref[...] = (cv * rc * gkv_ref[...]).astype(BF16)
        rc_ref[...] = rc

    return pl.pallas_call(
        body, name="lat_norms", grid=(s // ts,),
        in_specs=[_bs((ts, Q_LORA), lambda i: (i, P_QLAT // Q_LORA)),
                  _bs((ts, KV_LORA), lambda i: (i, P_CKV // KV_LORA)),
                  _bs((1, Q_LORA), lambda i: (0, 0)), _bs((1, KV_LORA), lambda i: (0, 0))],
        out_specs=[_bs((ts, Q_LORA), lambda i: (i, 0)), _bs((ts, KV_LORA), lambda i: (i, 0)),
                   _bs((ts, 1), lambda i: (i, 0)), _bs((ts, 1), lambda i: (i, 0))],
        out_shape=[jax.ShapeDtypeStruct((s, Q_LORA), BF16), jax.ShapeDtypeStruct((s, KV_LORA), BF16),
                   jax.ShapeDtypeStruct((s, 1), F32), jax.ShapeDtypeStruct((s, 1), F32)],
        compiler_params=_arb(1),
    )(proj, proj, gq, gkv)


HEAD_ROWS = 2048
DW_ROWS = 2048
MM_ROWS = 1024


def _q_heads(qn, wq, cos_t, sin_t):
    s = qn.shape[0]
    ts = min(s, HEAD_ROWS)

    def body(qn_ref, w_ref, cos_ref, sin_ref, q_ref):
        o = jnp.dot(qn_ref[...], w_ref[...], preferred_element_type=F32)
        q_ref[:, :LANES] = o[:, :LANES].astype(BF16)
        q_ref[:, LANES:] = _rope_fwd(o[:, LANES:], cos_ref[...], sin_ref[...]).astype(BF16)

    return pl.pallas_call(
        body, name="q_heads", grid=(H_A, s // ts),
        in_specs=[_bs((ts, Q_LORA), lambda h, i: (i, 0)), _bs((None, Q_LORA, HEAD_PAD), lambda h, i: (h, 0, 0)),
                  _bs((ts, LANES), lambda h, i: (i, 0)), _bs((ts, LANES), lambda h, i: (i, 0))],
        out_specs=_bs((None, ts, HEAD_PAD), lambda h, i: (h, i, 0)),
        out_shape=jax.ShapeDtypeStruct((H_A, s, HEAD_PAD), BF16),
        compiler_params=_arb(2),
    )(qn, wq, cos_t, sin_t)


def _kv_heads(cn, wkv, proj, cos_t, sin_t):
    s = cn.shape[0]
    ts = min(s, HEAD_ROWS)

    def body(cn_ref, w_ref, kr_ref, cos_ref, sin_ref, k_ref, v_ref):
        o = jnp.dot(cn_ref[...], w_ref[...], preferred_element_type=F32)
        k_ref[:, :LANES] = o[:, :LANES].astype(BF16)
        k_ref[:, LANES:] = _rope_fwd(kr_ref[...], cos_ref[...], sin_ref[...]).astype(BF16)
        v_ref[...] = o[:, LANES:].astype(BF16)

    return pl.pallas_call(
        body, name="kv_heads", grid=(H_A, s // ts),
        in_specs=[_bs((ts, KV_LORA), lambda h, i: (i, 0)),
                  _bs((None, KV_LORA, QK_NOPE + V_DIM), lambda h, i: (h, 0, 0)),
                  _bs((ts, LANES), lambda h, i: (i, P_KR // LANES)),
                  _bs((ts, LANES), lambda h, i: (i, 0)), _bs((ts, LANES), lambda h, i: (i, 0))],
        out_specs=[_bs((None, ts, HEAD_PAD), lambda h, i: (h, i, 0)), _bs((None, ts, V_DIM), lambda h, i: (h, i, 0))],
        out_shape=[jax.ShapeDtypeStruct((H_A, s, HEAD_PAD), BF16), jax.ShapeDtypeStruct((H_A, s, V_DIM), BF16)],
        compiler_params=_arb(2),
    )(cn, wkv, proj, cos_t, sin_t)


MLA_SCALE = 1.0 / math.sqrt(QK_NOPE + QK_ROPE)
LOG2E = math.log2(math.e)
MLA_EXP2_SCALE = MLA_SCALE * LOG2E


def _lane_tiles(a):
    return [a[:, j * LANES:(j + 1) * LANES] for j in range(a.shape[1] // LANES)]


MLA_SUB = 512


def _mla_fwd(q, k, v, tq=512, tk=1024):
    s = q.shape[1]
    tq = min(tq, s)
    nk = s // tk

    def body(q_ref, k_ref, v_ref, o_ref, lse_ref, m_ref, l_ref, acc_ref):
        m_ref[...] = jnp.full(m_ref.shape, -jnp.inf, F32)
        l_ref[...] = jnp.zeros(l_ref.shape, F32)
        acc_ref[...] = jnp.zeros(acc_ref.shape, F32)

        def step(c, carry):
            rows = pl.ds(pl.multiple_of(c * tk, tk), tk)
            for sub in range(tq // MLA_SUB):
                qr = slice(sub * MLA_SUB, (sub + 1) * MLA_SUB)
                raw = lax.dot_general(q_ref[qr, :], k_ref[rows, :], NT, preferred_element_type=F32)
                m_prev = m_ref[qr, :]
                m_new = jnp.maximum(m_prev, jnp.max(raw, axis=-1, keepdims=True))
                alpha = jnp.exp2((m_prev - m_new) * MLA_EXP2_SCALE)
                ps = [jnp.exp2((t - m_new) * MLA_EXP2_SCALE) for t in _lane_tiles(raw)]
                l_ref[qr, :] = alpha * l_ref[qr, :] + functools.reduce(lambda a, b: a + b, ps)
                p = jnp.concatenate(ps, axis=1).astype(BF16)
                acc_ref[qr, :] = alpha * acc_ref[qr, :] + jnp.dot(p, v_ref[rows, :], preferred_element_type=F32)
                m_ref[qr, :] = m_new
            return carry

        lax.fori_loop(0, nk, step, 0, unroll=True)
        l = jnp.sum(l_ref[...], axis=-1, keepdims=True)
        o_ref[...] = acc_ref[...] / l
        lse_ref[...] = m_ref[...] * MLA_SCALE + jnp.log(l)

    return pl.pallas_call(
        body, name="mla_fwd", grid=(H_A, s // tq),
        in_specs=[_bs((None, tq, HEAD_PAD), lambda h, i: (h, i, 0)),
                  _bs((None, s, HEAD_PAD), lambda h, i: (h, 0, 0)),
                  _bs((None, s, V_DIM), lambda h, i: (h, 0, 0))],
        out_specs=[_bs((tq, V_DIM), lambda h, i: (i, h)), _bs((None, tq, LANES), lambda h, i: (h, i, 0))],
        out_shape=[jax.ShapeDtypeStruct((s, H_A * V_DIM), F32), jax.ShapeDtypeStruct((H_A, s, LANES), F32)],
        scratch_shapes=[pltpu.VMEM((tq, LANES), F32), pltpu.VMEM((tq, LANES), F32), pltpu.VMEM((tq, V_DIM), F32)],
        compiler_params=_arb(2),
    )(q, k, v)


def _mla_bwd(q, k, v, do, o, lse, tq=512, tk=512):
    s = q.shape[1]
    nq = s // tq

    def body(q_ref, k_ref, v_ref, do_ref, o_ref, lse_ref, dq_ref, dk_ref, dv_ref, delta_ref):
        @pl.when(pl.program_id(1) == 0)
        def _():
            def init(c, carry):
                rows = pl.ds(pl.multiple_of(c * tq, tq), tq)
                delta = jnp.sum(do_ref[rows, :] * o_ref[rows, :], axis=-1, keepdims=True)
                delta_ref[rows, :] = jnp.broadcast_to(delta, (tq, LANES))
                dq_ref[rows, :] = jnp.zeros((tq, HEAD_PAD), F32)
                return carry

            lax.fori_loop(0, nq, init, 0)

        dk_ref[...] = jnp.zeros(dk_ref.shape, F32)
        dv_ref[...] = jnp.zeros(dv_ref.shape, F32)
        kb = k_ref[...]
        vb = v_ref[...]

        def step(c, carry):
            rows = pl.ds(pl.multiple_of(c * tq, tq), tq)
            qc = q_ref[rows, :]
            doc = do_ref[rows, :].astype(BF16)
            raw = lax.dot_general(qc, kb, NT, preferred_element_type=F32)
            dp = lax.dot_general(doc, vb, NT, preferred_element_type=F32)
            lse2 = lse_ref[rows, :] * LOG2E
            delta = delta_ref[rows, :]
            ps = [jnp.exp2(t * MLA_EXP2_SCALE - lse2) for t in _lane_tiles(raw)]
            dss = [pj * (dj - delta) * MLA_SCALE for pj, dj in zip(ps, _lane_tiles(dp))]
            p = jnp.concatenate(ps, axis=1).astype(BF16)
            ds = jnp.concatenate(dss, axis=1).astype(BF16)
            dv_ref[...] += lax.dot_general(p, doc, TN, preferred_element_type=F32)
            dk_ref[...] += lax.dot_general(ds, qc, TN, preferred_element_type=F32)
            dq_ref[rows, :] += jnp.dot(ds, kb, preferred_element_type=F32)
            return carry

        lax.fori_loop(0, nq, step, 0, unroll=True)

    return pl.pallas_call(
        body, name="mla_bwd", grid=(H_A, s // tk),
        in_specs=[_bs((None, s, HEAD_PAD), lambda h, j: (h, 0, 0)),
                  _bs((None, tk, HEAD_PAD), lambda h, j: (h, j, 0)),
                  _bs((None, tk, V_DIM), lambda h, j: (h, j, 0)),
                  _bs((s, V_DIM), lambda h, j: (0, h)), _bs((s, V_DIM), lambda h, j: (0, h)),
                  _bs((None, s, LANES), lambda h, j: (h, 0, 0))],
        out_specs=[_bs((None, s, HEAD_PAD), lambda h, j: (h, 0, 0)),
                   _bs((None, tk, HEAD_PAD), lambda h, j: (h, j, 0)),
                   _bs((None, tk, V_DIM), lambda h, j: (h, j, 0))],
        out_shape=[jax.ShapeDtypeStruct((H_A, s, HEAD_PAD), F32), jax.ShapeDtypeStruct((H_A, s, HEAD_PAD), F32),
                   jax.ShapeDtypeStruct((H_A, s, V_DIM), F32)],
        scratch_shapes=[pltpu.VMEM((s, LANES), F32)],
        compiler_params=_arb(2),
    )(q, k, v, do, o, lse)


def _mla_bwd_prep(dq, dk, dv, cos_t, sin_t, ts=256):
    s = dq.shape[1]

    def body(dq_ref, dk_ref, dv_ref, cos_ref, sin_ref, dqp_ref, dkvp_ref, dkr_ref):
        cos_v = cos_ref[...]
        sin_v = sin_ref[...]
        kr = jnp.zeros((ts, LANES), F32)
        for h in range(H_A):
            dqp_ref[h, :, :LANES] = dq_ref[h, :, :LANES].astype(BF16)
            dqp_ref[h, :, LANES:] = _rope_bwd(dq_ref[h, :, LANES:], cos_v, sin_v).astype(BF16)
            dkvp_ref[h, :, :LANES] = dk_ref[h, :, :LANES].astype(BF16)
            dkvp_ref[h, :, LANES:] = dv_ref[h].astype(BF16)
            kr = kr + dk_ref[h, :, LANES:]
        dkr_ref[...] = _rope_bwd(kr, cos_v, sin_v).astype(BF16)

    blk3 = lambda w: _bs((H_A, ts, w), lambda i: (0, i, 0))
    return pl.pallas_call(
        body, name="mla_bwd_prep", grid=(s // ts,),
        in_specs=[blk3(HEAD_PAD), blk3(HEAD_PAD), blk3(V_DIM),
                  _bs((ts, LANES), lambda i: (i, 0)), _bs((ts, LANES), lambda i: (i, 0))],
        out_specs=[blk3(HEAD_PAD), blk3(HEAD_PAD), _bs((ts, LANES), lambda i: (i, 0))],
        out_shape=[jax.ShapeDtypeStruct((H_A, s, HEAD_PAD), BF16), jax.ShapeDtypeStruct((H_A, s, HEAD_PAD), BF16),
                   jax.ShapeDtypeStruct((s, LANES), BF16)],
        compiler_params=_arb(1),
    )(dq, dk, dv, cos_t, sin_t)


WIN_SCALE = 1.0 / math.sqrt(HD_B)
SPAN = Q_BLOCK + 2 * WINDOW


def _t5_bucket_table():
    a = jnp.arange(Q_BLOCK, dtype=jnp.int32)[:, None]
    c = jnp.arange(SPAN, dtype=jnp.int32)[None, :]
    rel = c - WINDOW - a
    nb = NUM_BUCKETS // 2
    max_exact = nb // 2
    base = (rel > 0).astype(jnp.int32) * nb
    n = jnp.abs(rel)
    nf = jnp.maximum(n, 1).astype(F32)
    large = max_exact + (jnp.log(nf / max_exact) / math.log(MAX_DISTANCE / max_exact)
                         * (nb - max_exact)).astype(jnp.int32)
    large = jnp.minimum(large, nb - 1)
    return base + jnp.where(n < max_exact, n, large)


def _win_bias(bucket, rel_bias):
    def body(rb_ref, bk_ref, o_ref):
        h = pl.program_id(0)
        bk = bk_ref[...]
        acc = jnp.zeros((Q_BLOCK, SPAN), F32)
        for b in range(NUM_BUCKETS):
            acc = jnp.where(bk == b, rb_ref[b, h], acc)
        o_ref[...] = acc

    return pl.pallas_call(
        body, name="win_bias", grid=(H_B,),
        in_specs=[pl.BlockSpec(memory_space=pltpu.SMEM), _bs((Q_BLOCK, SPAN), lambda h: (0, 0))],
        out_specs=_bs((None, Q_BLOCK, SPAN), lambda h: (h, 0, 0)),
        out_shape=jax.ShapeDtypeStruct((H_B, Q_BLOCK, SPAN), F32),
        compiler_params=_arb(1),
    )(rel_bias, bucket)


GROUP_W = GROUP * HD_B


def _win_kv_rows(n, j, nblk):
    blk = jnp.clip(n + j - 1, 0, nblk - 1)
    return pl.ds(pl.multiple_of(blk * Q_BLOCK, Q_BLOCK), Q_BLOCK)


def _win_head_cols(kv):
    return slice(kv * HD_B, (kv + 1) * HD_B)


def _win_stack(ref, kv):
    return jnp.concatenate([ref[:, kv * GROUP_W + g * HD_B:kv * GROUP_W + (g + 1) * HD_B] for g in range(GROUP)], axis=0)


def _win_unstack(ref, kv, val):
    for g in range(GROUP):
        ref[:, kv * GROUP_W + g * HD_B:kv * GROUP_W + (g + 1) * HD_B] = val[g * Q_BLOCK:(g + 1) * Q_BLOCK].astype(ref.dtype)


def _win_scores(q, k_ref, kv, bias_ref, n, nblk):
    a = lax.broadcasted_iota(jnp.int32, (GROUP, Q_BLOCK, Q_BLOCK), 1)
    cc = lax.broadcasted_iota(jnp.int32, (GROUP, Q_BLOCK, Q_BLOCK), 2)
    valid = [(cc >= a) & (n > 0), None, (cc <= a) & (n < nblk - 1)]
    out = []
    for j in range(3):
        sc = lax.dot_general(q, k_ref[_win_kv_rows(n, j, nblk), _win_head_cols(kv)], NT, preferred_element_type=F32)
        sc = (sc.reshape(GROUP, Q_BLOCK, Q_BLOCK) * WIN_SCALE
              + bias_ref[kv * GROUP:(kv + 1) * GROUP, :, j * Q_BLOCK:(j + 1) * Q_BLOCK])
        if valid[j] is not None:
            sc = jnp.where(valid[j], sc, -1e30)
        out.append(sc)
    return out


def _win_sink(sink_ref, kv):
    hs = lax.broadcasted_iota(jnp.int32, (GROUP, Q_BLOCK, 1), 0)
    sk = jnp.zeros((GROUP, Q_BLOCK, 1), F32)
    for g in range(GROUP):
        sk = jnp.where(hs == g, sink_ref[kv * GROUP + g], sk)
    return sk


def _win_fwd(proj_b, bias, sinks):
    s = proj_b.shape[0]
    nblk = s // Q_BLOCK
    rows = GROUP * Q_BLOCK

    def body(sink_ref, q_ref, k_ref, v_ref, bias_ref, o_ref, lse_ref):
        n = pl.program_id(0)
        for kv in range(KV_B):
            sk = _win_sink(sink_ref, kv)
            q = _win_stack(q_ref, kv)
            ss = _win_scores(q, k_ref, kv, bias_ref, n, nblk)
            m = jnp.maximum(jnp.max(jnp.maximum(jnp.maximum(ss[0], ss[1]), ss[2]), axis=2, keepdims=True), sk)
            es = [jnp.exp(sc - m) for sc in ss]
            l = jnp.sum(es[0] + es[1] + es[2], axis=2, keepdims=True) + jnp.exp(sk - m)
            acc = jnp.zeros((rows, HD_B), F32)
            for j, e in enumerate(es):
                p = (e / l).astype(BF16).reshape(rows, Q_BLOCK)
                acc = acc + jnp.dot(p, v_ref[_win_kv_rows(n, j, nblk), _win_head_cols(kv)],
                                    preferred_element_type=F32)
            _win_unstack(o_ref, kv, acc)
            lse_ref[kv * GROUP:(kv + 1) * GROUP] = m + jnp.log(l)

    kv_w = KV_B * HD_B
    return pl.pallas_call(
        body, name="win_fwd", grid=(nblk,),
        in_specs=[pl.BlockSpec(memory_space=pltpu.SMEM), _bs((Q_BLOCK, H_B * HD_B), lambda n: (n, P_QB // (H_B * HD_B))),
                  _bs((s, kv_w), lambda n: (0, P_KB // kv_w)), _bs((s, kv_w), lambda n: (0, P_VB // kv_w)),
                  _bs((H_B, Q_BLOCK, SPAN), lambda n: (0, 0, 0))],
        out_specs=[_bs((Q_BLOCK, H_B * HD_B), lambda n: (n, 0)), _bs((H_B, Q_BLOCK, 1), lambda n: (0, n, 0))],
        out_shape=[jax.ShapeDtypeStruct((s, H_B * HD_B), F32), jax.ShapeDtypeStruct((H_B, s, 1), F32)],
        compiler_params=_arb(1),
    )(sinks, proj_b, proj_b, proj_b, bias)


def _win_bwd(proj_b, bias, sinks, do_b, lse):
    s = proj_b.shape[0]
    nblk = s // Q_BLOCK
    rows = GROUP * Q_BLOCK
    spad = s + 2 * WINDOW

    def body(sink_ref, q_ref, k_ref, v_ref, bias_ref, do_ref, lse_ref, dq_ref, dk_ref, dv_ref, db_ref, dsk_ref):
        n = pl.program_id(0)

        @pl.when(n == 0)
        def _():
            dk_ref[...] = jnp.zeros(dk_ref.shape, F32)
            dv_ref[...] = jnp.zeros(dv_ref.shape, F32)
            db_ref[...] = jnp.zeros(db_ref.shape, F32)
            dsk_ref[...] = jnp.zeros(dsk_ref.shape, F32)

        for kv in range(KV_B):
            heads = slice(kv * GROUP, (kv + 1) * GROUP)
            sk = _win_sink(sink_ref, kv)
            q = _win_stack(q_ref, kv)
            dob = _win_stack(do_ref, kv)
            lse_v = lse_ref[heads]
            ss = _win_scores(q, k_ref, kv, bias_ref, n, nblk)
            ps = [jnp.exp(sc - lse_v) for sc in ss]
            dps = [lax.dot_general(dob, v_ref[_win_kv_rows(n, j, nblk), _win_head_cols(kv)], NT,
                                   preferred_element_type=F32).reshape(GROUP, Q_BLOCK, Q_BLOCK) for j in range(3)]
            delta = jnp.sum(ps[0] * dps[0] + ps[1] * dps[1] + ps[2] * dps[2], axis=2, keepdims=True)
            dq = jnp.zeros((rows, HD_B), F32)
            for j in range(3):
                ds = ps[j] * (dps[j] - delta)
                db_ref[heads, :, j * Q_BLOCK:(j + 1) * Q_BLOCK] += ds
                dsb = (ds * WIN_SCALE).astype(BF16).reshape(rows, Q_BLOCK)
                dq = dq + jnp.dot(dsb, k_ref[_win_kv_rows(n, j, nblk), _win_head_cols(kv)],
                                  preferred_element_type=F32)
                krows = pl.ds(pl.multiple_of((n + j) * Q_BLOCK, Q_BLOCK), Q_BLOCK)
                dk_ref[krows, _win_head_cols(kv)] += lax.dot_general(dsb, q, TN, preferred_element_type=F32)
                dv_ref[krows, _win_head_cols(kv)] += lax.dot_general(
                    ps[j].astype(BF16).reshape(rows, Q_BLOCK), dob, TN, preferred_element_type=F32)
            dsk_ref[heads] += -(jnp.exp(sk - lse_v) * delta)
            _win_unstack(dq_ref, kv, dq)

    kv_w = KV_B * HD_B
    qspec = _bs((Q_BLOCK, H_B * HD_B), lambda n: (n, 0))
    kacc = _bs((spad, kv_w), lambda n: (0, 0))
    return pl.pallas_call(
        body, name="win_bwd", grid=(nblk,),
        in_specs=[pl.BlockSpec(memory_space=pltpu.SMEM), _bs((Q_BLOCK, H_B * HD_B), lambda n: (n, P_QB // (H_B * HD_B))),
                  _bs((s, kv_w), lambda n: (0, P_KB // kv_w)), _bs((s, kv_w), lambda n: (0, P_VB // kv_w)),
                  _bs((H_B, Q_BLOCK, SPAN), lambda n: (0, 0, 0)), qspec, _bs((H_B, Q_BLOCK, 1), lambda n: (0, n, 0))],
        out_specs=[qspec, kacc, kacc, _bs((H_B, Q_BLOCK, SPAN), lambda n: (0, 0, 0)),
                   _bs((H_B, Q_BLOCK, 1), lambda n: (0, 0, 0))],
        out_shape=[jax.ShapeDtypeStruct((s, H_B * HD_B), BF16), jax.ShapeDtypeStruct((spad, kv_w), F32),
                   jax.ShapeDtypeStruct((spad, kv_w), F32), jax.ShapeDtypeStruct((H_B, Q_BLOCK, SPAN), F32),
                   jax.ShapeDtypeStruct((H_B, Q_BLOCK, 1), F32)],
        compiler_params=_arb(1),
    )(sinks, proj_b, proj_b, proj_b, bias, do_b, lse)


def _win_param_grads(bucket, dbias, dsink_rows):
    def body(bk_ref, db_ref, ds_ref, o_ref):
        bk = bk_ref[...]
        dbv = db_ref[...]
        lane = lax.broadcasted_iota(jnp.int32, (1, LANES), 1)
        res = jnp.zeros((1, LANES), F32)
        for b in range(NUM_BUCKETS):
            tot = jnp.sum(jnp.sum(jnp.where(bk == b, dbv, 0.0), axis=1, keepdims=True), axis=0, keepdims=True)
            res = jnp.where(lane == b, tot, res)
        stot = jnp.sum(ds_ref[...], axis=0, keepdims=True)
        o_ref[...] = jnp.where(lane == NUM_BUCKETS, stot, res)

    return pl.pallas_call(
        body, name="win_param_grads", grid=(H_B,),
        in_specs=[_bs((Q_BLOCK, SPAN), lambda h: (0, 0)), _bs((None, Q_BLOCK, SPAN), lambda h: (h, 0, 0)),
                  _bs((None, Q_BLOCK, 1), lambda h: (h, 0, 0))],
        out_specs=_bs((None, 1, LANES), lambda h: (h, 0, 0)),
        out_shape=jax.ShapeDtypeStruct((H_B, 1, LANES), F32),
        compiler_params=_arb(1),
    )(bucket, dbias, dsink_rows)


def _gate_fwd(proj, o_a, o_b, ts=256):
    s = o_a.shape[0]
    wide = lambda cb: _bs((ts, D_MODEL), lambda i: (i, cb))

    def body(ga_ref, gb_ref, oa_ref, ob_ref, m_ref):
        m_ref[...] = (jax.nn.sigmoid(ga_ref[...]) * oa_ref[...]
                      + jax.nn.sigmoid(gb_ref[...]) * ob_ref[...]).astype(BF16)

    return pl.pallas_call(
        body, name="gate_fwd", grid=(s // ts,),
        in_specs=[wide(P_GA // D_MODEL), wide(P_GB // D_MODEL), wide(0), wide(0)],
        out_specs=wide(0), out_shape=jax.ShapeDtypeStruct((s, D_MODEL), BF16),
        compiler_params=_arb(1),
    )(proj, proj, o_a, o_b)


def _gate_bwd(dmixed, proj, o_a, o_b, ts=256):
    s = o_a.shape[0]
    wide = lambda cb: _bs((ts, D_MODEL), lambda i: (i, cb))

    def body(dm_ref, ga_ref, gb_ref, oa_ref, ob_ref, doa_ref, dob_ref, dga_ref, dgb_ref):
        dm = dm_ref[...]
        sa = jax.nn.sigmoid(ga_ref[...])
        sb = jax.nn.sigmoid(gb_ref[...])
        doa_ref[...] = dm * sa
        dob_ref[...] = (dm * sb).astype(BF16)
        dga_ref[...] = (dm * oa_ref[...] * (sa * (1.0 - sa))).astype(BF16)
        dgb_ref[...] = (dm * ob_ref[...] * (sb * (1.0 - sb))).astype(BF16)

    return pl.pallas_call(
        body, name="gate_bwd", grid=(s // ts,),
        in_specs=[wide(0), wide(P_GA // D_MODEL), wide(P_GB // D_MODEL), wide(0), wide(0)],
        out_specs=[wide(0)] * 4,
        out_shape=[jax.ShapeDtypeStruct((s, D_MODEL), F32), jax.ShapeDtypeStruct((s, D_MODEL), BF16),
                   jax.ShapeDtypeStruct((s, D_MODEL), BF16), jax.ShapeDtypeStruct((s, D_MODEL), BF16)],
        compiler_params=_arb(1),
    )(dmixed, proj, proj, o_a, o_b)


CONV_CHUNK = 512
N_SLAB = D_FF // LANES


def _shifted(ref, c, nchunks):
    r0 = c * CONV_CHUNK
    cur = ref[r0:r0 + CONV_CHUNK, :]
    row = lax.broadcasted_iota(jnp.int32, (8, LANES), 0)
    before = ref[r0 - 8:r0, :][7:8, :] if c > 0 else jnp.zeros((1, LANES), F32)
    after = ref[r0 + CONV_CHUNK:r0 + CONV_CHUNK + 8, :][0:1, :] if c < nchunks - 1 else jnp.zeros((1, LANES), F32)
    down = pltpu.roll(cur, 1, 0)
    up = pltpu.roll(cur, CONV_CHUNK - 1, 0)
    prev = jnp.concatenate([jnp.where(row == 0, before, down[:8]), down[8:]], axis=0)
    nxt = jnp.concatenate([up[:-8], jnp.where(row == 7, after, up[-8:])], axis=0)
    return prev, cur, nxt


def _conv_taps(ref, w_ref, b_ref, c, nchunks):
    prev, cur, nxt = _shifted(ref, c, nchunks)
    conv = prev * w_ref[0:1, :] + cur * w_ref[1:2, :] + nxt * w_ref[2:3, :] + b_ref[...]
    return conv, prev, cur, nxt


def _convffn_fwd(u, conv_w, conv_b):
    s = u.shape[0]
    nchunks = s // CONV_CHUNK

    def body(ug_ref, uv_ref, wg_ref, wv_ref, bg_ref, bv_ref, f_ref):
        for c in range(nchunks):
            cg = _conv_taps(ug_ref, wg_ref, bg_ref, c, nchunks)[0]
            cv = _conv_taps(uv_ref, wv_ref, bv_ref, c, nchunks)[0]
            f_ref[c * CONV_CHUNK:(c + 1) * CONV_CHUNK, :] = (cg * jax.nn.sigmoid(cg) * cv).astype(BF16)

    slab = lambda off: _bs((s, LANES), lambda j: (0, off + j))
    wsl = lambda off: _bs((3, LANES), lambda j: (0, off + j))
    bsl = lambda off: _bs((1, LANES), lambda j: (0, off + j))
    return pl.pallas_call(
        body, name="convffn_fwd", grid=(N_SLAB,),
        in_specs=[slab(0), slab(N_SLAB), wsl(0), wsl(N_SLAB), bsl(0), bsl(N_SLAB)],
        out_specs=slab(0), out_shape=jax.ShapeDtypeStruct((s, D_FF), BF16),
        compiler_params=_arb(1),
    )(u, u, conv_w, conv_w, conv_b, conv_b)


def _convffn_bwd(u, conv_w, conv_b, df):
    s = u.shape[0]
    nchunks = s // CONV_CHUNK

    def body(ug_ref, uv_ref, wg_ref, wv_ref, bg_ref, bv_ref, df_ref, du_ref, dw_ref, db_ref, dcg_ref, dcv_ref):
        dwg = [jnp.zeros((1, LANES), F32) for _ in range(3)]
        dwv = [jnp.zeros((1, LANES), F32) for _ in range(3)]
        dbg = jnp.zeros((1, LANES), F32)
        dbv = jnp.zeros((1, LANES), F32)
        for c in range(nchunks):
            rows = slice(c * CONV_CHUNK, (c + 1) * CONV_CHUNK)
            cg, gp, gc, gn = _conv_taps(ug_ref, wg_ref, bg_ref, c, nchunks)
            cv, vp, vc, vn = _conv_taps(uv_ref, wv_ref, bv_ref, c, nchunks)
            dfv = df_ref[rows, :]
            sg = jax.nn.sigmoid(cg)
            dcg = dfv * cv * (sg * (1.0 + cg * (1.0 - sg)))
            dcv = dfv * (cg * sg)
            dcg_ref[rows, :] = dcg
            dcv_ref[rows, :] = dcv
            for t, (tg, tv) in enumerate(((gp, vp), (gc, vc), (gn, vn))):
                dwg[t] = dwg[t] + jnp.sum(tg * dcg, axis=0, keepdims=True)
                dwv[t] = dwv[t] + jnp.sum(tv * dcv, axis=0, keepdims=True)
            dbg = dbg + jnp.sum(dcg, axis=0, keepdims=True)
            dbv = dbv + jnp.sum(dcv, axis=0, keepdims=True)
        for t in range(3):
            dw_ref[0, t:t + 1, :] = dwg[t]
            dw_ref[1, t:t + 1, :] = dwv[t]
        db_ref[0] = dbg
        db_ref[1] = dbv
        for half, (dc_ref, w_ref) in enumerate(((dcg_ref, wg_ref), (dcv_ref, wv_ref))):
            for c in range(nchunks):
                prev, cur, nxt = _shifted(dc_ref, c, nchunks)
                du = nxt * w_ref[0:1, :] + cur * w_ref[1:2, :] + prev * w_ref[2:3, :]
                du_ref[half, c * CONV_CHUNK:(c + 1) * CONV_CHUNK, :] = du.astype(BF16)

    slab = lambda off: _bs((s, LANES), lambda j: (0, off + j))
    wsl = lambda off: _bs((3, LANES), lambda j: (0, off + j))
    bsl = lambda off: _bs((1, LANES), lambda j: (0, off + j))
    return pl.pallas_call(
        body, name="convffn_bwd", grid=(N_SLAB,),
        in_specs=[slab(0), slab(N_SLAB), wsl(0), wsl(N_SLAB), bsl(0), bsl(N_SLAB), slab(0)],
        out_specs=[_bs((2, s, LANES), lambda j: (0, 0, j)), _bs((2, 3, LANES), lambda j: (0, 0, j)),
                   _bs((2, 1, LANES), lambda j: (0, 0, j))],
        out_shape=[jax.ShapeDtypeStruct((2, s, D_FF), BF16), jax.ShapeDtypeStruct((2, 3, D_FF), F32),
                   jax.ShapeDtypeStruct((2, 1, D_FF), F32)],
        scratch_shapes=[pltpu.VMEM((s, LANES), F32), pltpu.VMEM((s, LANES), F32)],
        compiler_params=_arb(1),
    )(u, u, conv_w, conv_w, conv_b, conv_b, df)


def _row_tile(rows, limit=512):
    best = rows
    for t in range(8, min(rows, limit) + 1, 8):
        if rows % t == 0:
            best = t
    return best if rows % 8 == 0 else rows


ADAM_C1 = 1.0 - ADAM_B1 ** ADAM_STEP
ADAM_C2 = 1.0 - ADAM_B2 ** ADAM_STEP


def _adamw_math(w, gv, m, v):
    nm = ADAM_B1 * m + (1.0 - ADAM_B1) * gv
    nv = ADAM_B2 * v + (1.0 - ADAM_B2) * (gv * gv)
    m_hat = nm / ADAM_C1
    v_hat = nv / ADAM_C2
    return -ADAM_LR * (m_hat / (jnp.sqrt(v_hat) + ADAM_EPS) + ADAM_WD * w), nm, nv


def _adamw_halves(name, core, w, mine, theirs, m, v):
    half, cols = mine.shape
    tr = _row_tile(half)
    nr = half // tr

    def body(core_ref, w_ref, mine_ref, theirs_ref, m_ref, v_ref, g_ref, d_ref, nm_ref, nv_ref):
        gv = jnp.where(pl.program_id(0) == core_ref[0], mine_ref[...], theirs_ref[...])
        g_ref[...] = gv
        d_ref[...], nm_ref[...], nv_ref[...] = _adamw_math(w_ref[...], gv, m_ref[...], v_ref[...])

    full = pl.BlockSpec((tr, cols), lambda hf, r, cr: (hf * nr + r, 0))
    part = pl.BlockSpec((tr, cols), lambda hf, r, cr: (r, 0))
    return pl.pallas_call(
        body, name=name,
        grid_spec=pltpu.PrefetchScalarGridSpec(num_scalar_prefetch=1, grid=(2, nr),
                                               in_specs=[full, part, part, full, full], out_specs=[full] * 4),
        out_shape=[jax.ShapeDtypeStruct((2 * half, cols), F32)] * 4, compiler_params=_arb(2),
    )(core, w, mine, theirs, m, v)


ANY = pl.BlockSpec(memory_space=pl.ANY)


def _mesh_pos():
    return lax.axis_index("x"), lax.axis_index("y"), lax.axis_index("c")


def _other_chips(x, y):
    return [(1 - x, y), (x, 1 - y), (1 - x, 1 - y)]


def _allgather_weights(shards, split):
    n = len(shards)

    def body(*refs):
        w_refs, o_refs = refs[:n], refs[n:2 * n]
        send_sems, recv_sems, fsend_sems, frecv_sems = refs[2 * n:]
        x, y, c = _mesh_pos()
        p = 2 * x + y
        chips = _other_chips(x, y)

        def piece(i, chip_index, core):
            return o_refs[i].at[chip_index, core] if split[i] else o_refs[i].at[chip_index]

        def remote(src, dst, ssem, rsem, to):
            return pltpu.make_async_remote_copy(src_ref=src, dst_ref=dst, send_sem=ssem, recv_sem=rsem,
                                                device_id=to, device_id_type=MESH)

        sends = []
        for i in range(n):
            src = w_refs[i].at[c] if split[i] else w_refs[i]
            for k, chip in enumerate(chips):
                cp = remote(src, piece(i, p, c), send_sems.at[3 * i + k], recv_sems.at[3 * i + k], (*chip, c))
                cp.start()
                sends.append(cp)
        for i in range(n):
            for k, chip in enumerate(chips):
                pk = 2 * chip[0] + chip[1]
                landed = piece(i, pk, c)
                remote(landed, landed, send_sems.at[3 * i + k], recv_sems.at[3 * i + k], (*chip, c)).wait_recv()
                if split[i]:
                    fw = remote(landed, landed, fsend_sems.at[3 * i + k], frecv_sems.at[3 * i + k], (x, y, 1 - c))
                    fw.start()
                    sends.append(fw)
        for i in range(n):
            if split[i]:
                for k, chip in enumerate(chips):
                    pk = 2 * chip[0] + chip[1]
                    theirs = piece(i, pk, 1 - c)
                    remote(theirs, theirs, fsend_sems.at[3 * i + k], frecv_sems.at[3 * i + k],
                           (x, y, 1 - c)).wait_recv()
        for cp in sends:
            cp.wait_send()

    return pl.pallas_call(
        body, name="allgather_weights",
        in_specs=[ANY] * n, out_specs=[ANY] * n,
        out_shape=[jax.ShapeDtypeStruct((4,) + w.shape, w.dtype) for w in shards],
        scratch_shapes=[pltpu.SemaphoreType.DMA((3 * n,)), pltpu.SemaphoreType.DMA((3 * n,)),
                        pltpu.SemaphoreType.DMA((3 * n,)), pltpu.SemaphoreType.DMA((3 * n,))],
    )(*shards)


def _rs_pair_exchange(name, grads):
    n = len(grads)

    def body(*refs):
        g_refs, o_refs = refs[:n], refs[n:2 * n]
        send_sems, recv_sems = refs[2 * n:]
        x, y, c = _mesh_pos()
        cps = []
        for i in range(n):
            cp = pltpu.make_async_remote_copy(
                src_ref=g_refs[i].at[:, 1 - c], dst_ref=o_refs[i],
                send_sem=send_sems.at[i], recv_sem=recv_sems.at[i], device_id=(x, y, 1 - c), device_id_type=MESH)
            cp.start()
            cps.append(cp)
        for cp in cps:
            cp.wait()

    return pl.pallas_call(
        body, name=name, in_specs=[ANY] * n, out_specs=[ANY] * n,
        out_shape=[jax.ShapeDtypeStruct((4,) + g.shape[2:], F32) for g in grads],
        scratch_shapes=[pltpu.SemaphoreType.DMA((n,)), pltpu.SemaphoreType.DMA((n,))],
    )(*grads)


def _rs_pair_add(name, core, g, recv):
    _, half, cols = recv.shape
    tr = _row_tile(half)
    nr = half // tr

    def body(core_ref, g_ref, r_ref, o_ref):
        o_ref[...] = (g_ref[...] + r_ref[...]).astype(BF16)

    return pl.pallas_call(
        body, name=name,
        grid_spec=pltpu.PrefetchScalarGridSpec(
            num_scalar_prefetch=1, grid=(4, nr),
            in_specs=[pl.BlockSpec((None, None, tr, cols), lambda q, r, cr: (q, cr[0], r, 0)),
                      pl.BlockSpec((None, tr, cols), lambda q, r, cr: (q, r, 0))],
            out_specs=pl.BlockSpec((None, tr, cols), lambda q, r, cr: (q, r, 0))),
        out_shape=jax.ShapeDtypeStruct((4, half, cols), BF16),
        compiler_params=_arb(2),
    )(core, g, recv)


def _rs_final_add(name, chip, pair, recv):
    _, half, cols = pair.shape
    tr = _row_tile(half)

    def body(chip_ref, p_ref, r_ref, o_ref):
        o_ref[...] = ((p_ref[...].astype(F32) + r_ref[0].astype(F32)) + r_ref[1].astype(F32)) + r_ref[2].astype(F32)

    return pl.pallas_call(
        body, name=name,
        grid_spec=pltpu.PrefetchScalarGridSpec(
            num_scalar_prefetch=1, grid=(half // tr,),
            in_specs=[pl.BlockSpec((None, tr, cols), lambda r, ch: (ch[0], r, 0)),
                      pl.BlockSpec((3, tr, cols), lambda r, ch: (0, r, 0))],
            out_specs=pl.BlockSpec((tr, cols), lambda r, ch: (r, 0))),
        out_shape=jax.ShapeDtypeStruct((half, cols), F32),
        compiler_params=_arb(1),
    )(chip, pair, recv)


def _rs_pair_share(halves):
    n = len(halves)

    def body(*refs):
        h_refs, o_refs = refs[:n], refs[n:2 * n]
        send_sems, recv_sems = refs[2 * n:]
        x, y, c = _mesh_pos()
        cps = []
        for i in range(n):
            cp = pltpu.make_async_remote_copy(src_ref=h_refs[i], dst_ref=o_refs[i], send_sem=send_sems.at[i],
                                              recv_sem=recv_sems.at[i], device_id=(x, y, 1 - c), device_id_type=MESH)
            cp.start()
            cps.append(cp)
        for cp in cps:
            cp.wait()

    return pl.pallas_call(
        body, name="rs_pair_share", in_specs=[ANY] * n, out_specs=[ANY] * n,
        out_shape=[jax.ShapeDtypeStruct(h.shape, F32) for h in halves],
        scratch_shapes=[pltpu.SemaphoreType.DMA((n,)), pltpu.SemaphoreType.DMA((n,))],
    )(*halves)


HBM = pl.BlockSpec(memory_space=pltpu.HBM)
SEM = pl.BlockSpec(memory_space=pltpu.SEMAPHORE)


class _SplitExchange:
    def __init__(self, name, srcs, land_shapes, src_of, dst_of, arrive_of, to_sibling=False):
        self.name, self.srcs, self.land_shapes = name, list(srcs), list(land_shapes)
        self.src_of, self.dst_of, self.arrive_of = src_of, dst_of, arrive_of
        self.to_sibling = to_sibling
        self.fan = 1 if to_sibling else 3

    def _copies(self, src_refs, land_refs, send_sems, recv_sems):
        x, y, c = _mesh_pos()
        p = 2 * x + y
        if self.to_sibling:
            peers = [((x, y, 1 - c), 1 - c)]
        else:
            peers = [((*chip, c), 2 * chip[0] + chip[1]) for chip in _other_chips(x, y)]
        out = []
        for i, (src, land) in enumerate(zip(src_refs, land_refs)):
            for k, (peer, pk) in enumerate(peers):
                sems = dict(send_sem=send_sems.at[self.fan * i + k], recv_sem=recv_sems.at[self.fan * i + k],
                            device_id=peer, device_id_type=MESH)
                sent = pltpu.make_async_remote_copy(src_ref=self.src_of(src, k, p, pk),
                                                    dst_ref=self.dst_of(land, k, p, pk), **sems)
                here = self.arrive_of(land, k, p, pk)
                out.append((sent, pltpu.make_async_remote_copy(src_ref=here, dst_ref=here, **sems)))
        return out

    def start(self):
        n = len(self.srcs)

        def body(*refs):
            for sent, _ in self._copies(refs[:n], refs[n:2 * n], refs[2 * n], refs[2 * n + 1]):
                sent.start()
            refs[-1][...] = jnp.zeros((8, LANES), F32)

        lands = [lax.empty(shape, src.dtype) for shape, src in zip(self.land_shapes, self.srcs)]
        operands = [pltpu.with_memory_space_constraint(a, pltpu.HBM) for a in self.srcs + lands]
        outs = pl.pallas_call(
            body, name=self.name + "_start",
            out_shape=(pltpu.SemaphoreType.DMA((self.fan * n,)), pltpu.SemaphoreType.DMA((self.fan * n,)),
                       *[pltpu.HBM(a.shape, a.dtype) for a in operands], jax.ShapeDtypeStruct((8, LANES), F32)),
            in_specs=[HBM] * (2 * n), out_specs=(SEM, SEM, *[HBM] * (2 * n), pl.BlockSpec(memory_space=pltpu.VMEM)),
            input_output_aliases={j: 2 + j for j in range(2 * n)},
            compiler_params=pltpu.CompilerParams(has_side_effects=pltpu.SideEffectType.DATAFLOW_SIDE_EFFECTING),
        )(*operands)
        self._sems, self._thru = outs[:2], list(outs[2:2 + 2 * n])
        return outs[-1]

    def wait(self, after):
        n = len(self.srcs)

        def body(*refs):
            for sent, arrived in self._copies(refs[:n], refs[n:2 * n], refs[2 * n], refs[2 * n + 1]):
                sent.wait_send()
                arrived.wait_recv()

        outs = pl.pallas_call(
            body, name=self.name + "_wait",
            out_shape=tuple(pltpu.HBM(a.shape, a.dtype) for a in self._thru),
            in_specs=[HBM] * (2 * n) + [SEM, SEM, ANY], out_specs=tuple([HBM] * (2 * n)),
            input_output_aliases={j: j for j in range(2 * n)},
            compiler_params=pltpu.CompilerParams(has_side_effects=pltpu.SideEffectType.DATAFLOW_SIDE_EFFECTING),
        )(*self._thru, *self._sems, after)
        return list(outs[:n]), list(outs[n:])


def _small_allreduce(parts):
    n = len(parts)

    def body(*refs):
        in_refs, out_refs, gather_refs = refs[:n], refs[n:2 * n], refs[2 * n:3 * n]
        send_sems, recv_sems = refs[3 * n:]
        x, y, c = _mesh_pos()
        me = 4 * x + 2 * y + c
        cps = []
        for i in range(n):
            gather_refs[i][me] = in_refs[i][...]
            for j in range(1, 8):
                peer = (x ^ (j >> 2), y ^ ((j >> 1) & 1), c ^ (j & 1))
                cp = pltpu.make_async_remote_copy(
                    src_ref=in_refs[i], dst_ref=gather_refs[i].at[me], send_sem=send_sems.at[7 * i + j - 1],
                    recv_sem=recv_sems.at[7 * i + j - 1], device_id=peer, device_id_type=MESH)
                cp.start()
                cps.append(cp)
        for i in range(n):
            for j in range(1, 8):
                peer_id = 4 * (x ^ (j >> 2)) + 2 * (y ^ ((j >> 1) & 1)) + (c ^ (j & 1))
                slot = gather_refs[i].at[peer_id]
                pltpu.make_async_remote_copy(src_ref=slot, dst_ref=slot, send_sem=send_sems.at[7 * i + j - 1],
                                             recv_sem=recv_sems.at[7 * i + j - 1], device_id=(x, y, c),
                                             device_id_type=MESH).wait_recv()
        for cp in cps:
            cp.wait_send()
        for i in range(n):
            tot = gather_refs[i][0]
            for d in range(1, 8):
                tot = tot + gather_refs[i][d]
            out_refs[i][...] = tot

    vmem = pl.BlockSpec(memory_space=pltpu.VMEM)
    return pl.pallas_call(
        body, name="small_allreduce", in_specs=[vmem] * n, out_specs=[vmem] * n,
        out_shape=[jax.ShapeDtypeStruct(p.shape, F32) for p in parts],
        scratch_shapes=[pltpu.VMEM((8,) + p.shape, F32) for p in parts]
        + [pltpu.SemaphoreType.DMA((7 * n,)), pltpu.SemaphoreType.DMA((7 * n,))],
    )(*parts)


def _adamw_small(ws, gs, ms, vs):
    n = len(ws)

    def body(*refs):
        for i in range(n):
            w_ref, g_ref, m_ref, v_ref = refs[i], refs[n + i], refs[2 * n + i], refs[3 * n + i]
            d_ref, nm_ref, nv_ref = refs[4 * n + i], refs[5 * n + i], refs[6 * n + i]
            d_ref[...], nm_ref[...], nv_ref[...] = _adamw_math(w_ref[...], g_ref[...], m_ref[...], v_ref[...])

    vmem = pl.BlockSpec(memory_space=pltpu.VMEM)
    shapes = [jax.ShapeDtypeStruct(w.shape, F32) for w in ws]
    outs = pl.pallas_call(body, name="adamw_small", in_specs=[vmem] * (4 * n), out_specs=[vmem] * (3 * n),
                          out_shape=shapes * 3)(*ws, *gs, *ms, *vs)
    return outs[:n], outs[n:2 * n], outs[2 * n:]


W_IN_PIECES = ((0, 256, P_QLAT), (256, 384, P_CKV), (384, 448, P_KR), (448, 1472, P_QB), (1472, 1728, P_KB),
               (1728, 1984, P_VB), (1984, 3008, P_GA), (3008, 4032, P_GB))
W_IN_SHARD = 1008


def _w_in_from_shards(shards):
    cols = []
    for lo, hi, _ in sorted(W_IN_PIECES, key=lambda piece: piece[2]):
        for q in range(4):
            a, b = max(lo, q * W_IN_SHARD), min(hi, (q + 1) * W_IN_SHARD)
            if a < b:
                cols.append(shards[q][:, a - q * W_IN_SHARD:b - q * W_IN_SHARD])
    cols.append(jnp.zeros((shards.shape[1], W_IN_PAD - 4 * W_IN_SHARD), shards.dtype))
    return jnp.concatenate(cols, axis=1)


def _w_in_to_shards(p):
    shards = []
    for q in range(4):
        cols = []
        for lo, hi, at in W_IN_PIECES:
            a, b = max(lo, q * W_IN_SHARD), min(hi, (q + 1) * W_IN_SHARD)
            if a < b:
                cols.append(p[:, at + a - lo:at + b - lo])
        shards.append(jnp.concatenate(cols, axis=1))
    return jnp.stack(shards)


def _col_shards(w):
    r, c4 = w.shape
    return w.reshape(r, 4, c4 // 4).transpose(1, 0, 2)


def _local_step(x, positions, target, norm1_g, w_in_p, q_a_norm_g, wq, kv_a_norm_g, wkv, rel_bias, sinks,
                late_weights, norm2_g, conv_w, conv_b, final_norm_g, early_grads=None, last_grads=None):
    s = x.shape[0]
    half = QK_ROPE // 2
    inv_freq = jnp.asarray(np.float32(ROPE_THETA) ** (-np.arange(half, dtype=np.float32) / np.float32(half)))
    ang = positions.astype(F32)[:, None] * inv_freq[None, :]
    cos, sin = jnp.cos(ang), jnp.sin(ang)
    z64 = jnp.zeros((s, 64), F32)
    cos_t = jnp.concatenate([cos, cos, z64], axis=1)
    sin_t = jnp.concatenate([-sin, sin, z64], axis=1)
    bucket = _t5_bucket_table()
    sinks1 = sinks.reshape(H_B)

    h1, rstd1 = _rmsnorm_fwd("norm1_fwd", x, norm1_g, D_MODEL, 0)
    proj, proj_b = _matmul("proj", h1, w_in_p, out_shape=(s, W_IN_PAD), out_dtype=F32, grid=(s // MM_ROWS, W_IN_PAD // 1024, 1),
                           a_spec=_bs((MM_ROWS, D_MODEL), lambda i, j, k: (i, 0)), b_spec=_bs((D_MODEL, 1024), lambda i, j, k: (0, j)),
                           o_spec=_bs((MM_ROWS, 1024), lambda i, j, k: (i, j)), contract=NN, bf16_copy=True)
    qn, cn, rstd_q, rstd_c = _lat_norms(proj, q_a_norm_g, kv_a_norm_g)
    q = _q_heads(qn, wq, cos_t, sin_t)
    k, v = _kv_heads(cn, wkv, proj, cos_t, sin_t)
    o_a, lse_a = _mla_fwd(q, k, v)

    bias = _win_bias(bucket, rel_bias)
    o_b, lse_b = _win_fwd(proj_b, bias, sinks1)

    mixed = _gate_fwd(proj, o_a, o_b)
    w_out, w_up, w_down = late_weights(mixed)
    row512 = lambda w: _bs((MM_ROWS, w), lambda i, j, k: (i, 0))
    whole = lambda r, c: _bs((r, c), lambda i, j, k: (0, 0))
    x1 = _matmul("attn_out", mixed, w_out, out_shape=(s, D_MODEL), out_dtype=F32, grid=(s // MM_ROWS, 1, 1),
                 a_spec=row512(D_MODEL), b_spec=whole(D_MODEL, D_MODEL), o_spec=row512(D_MODEL), contract=NN, add=x)
    h2, rstd2 = _rmsnorm_fwd("norm2_fwd", x1, norm2_g, D_MODEL, 0)
    u = _matmul("ffn_up", h2, w_up, out_shape=(s, 2 * D_FF), out_dtype=F32, grid=(s // MM_ROWS, 4, 1),
                a_spec=_bs((MM_ROWS, D_MODEL), lambda i, j, k: (i, 0)), b_spec=_bs((D_MODEL, D_FF // 2), lambda i, j, k: (0, j)),
                o_spec=_bs((MM_ROWS, D_FF // 2), lambda i, j, k: (i, j)), contract=NN)
    f = _convffn_fwd(u, conv_w, conv_b)
    x2 = _matmul("ffn_down", f, w_down, out_shape=(s, D_MODEL), out_dtype=F32, grid=(s // MM_ROWS, 1, 1),
                 a_spec=row512(D_FF), b_spec=whole(D_FF, D_MODEL), o_spec=row512(D_MODEL), contract=NN, add=x1)
    loss, dx2, d_final_g, dx2_b = _final_loss(x2, target, final_norm_g.reshape(1, D_MODEL))
    tk = min(s, DW_ROWS)

    df = _matmul("ffn_down_dx", dx2_b, w_down, out_shape=(s, D_FF), out_dtype=F32, grid=(s // MM_ROWS, 2, 1),
                 a_spec=row512(D_MODEL), b_spec=_bs((D_FF // 2, D_MODEL), lambda i, j, k: (j, 0)),
                 o_spec=_bs((MM_ROWS, D_FF // 2), lambda i, j, k: (i, j)), contract=NT)
    d_w_down = _matmul("ffn_down_dw", f, dx2_b, out_shape=(D_FF, D_MODEL), out_dtype=F32, grid=(2, 1, s // tk),
                       a_spec=_bs((tk, D_FF // 2), lambda i, j, k: (k, i)), b_spec=_bs((tk, D_MODEL), lambda i, j, k: (k, 0)),
                       o_spec=_bs((D_FF // 2, D_MODEL), lambda i, j, k: (i, 0)), contract=TN)
    du, d_conv_w2, d_conv_b2 = _convffn_bwd(u, conv_w, conv_b, df)
    kc = D_FF // 2
    dh2 = _matmul("ffn_up_dx", du, w_up, out_shape=(s, D_MODEL), out_dtype=F32, grid=(s // 1024, 1, 4),
                  a_spec=_bs((None, 1024, kc), lambda i, j, k: (k // 2, i, k % 2)),
                  b_spec=_bs((D_MODEL, kc), lambda i, j, k: (0, k)),
                  o_spec=_bs((1024, D_MODEL), lambda i, j, k: (i, 0)), contract=NT)
    d_w_up = _matmul("ffn_up_dw", h2, du, out_shape=(D_MODEL, 2 * D_FF), out_dtype=F32, grid=(1, 4, s // tk),
                     a_spec=_bs((tk, D_MODEL), lambda i, j, k: (k, 0)),
                     b_spec=_bs((None, tk, kc), lambda i, j, k: (j // 2, k, j % 2)),
                     o_spec=_bs((D_MODEL, kc), lambda i, j, k: (0, j)), contract=TN)
    dx1, d_norm2_g, dx1_b = _rmsnorm_bwd("norm2_bwd", dh2, x1, rstd2, norm2_g, D_MODEL, 0, F32, res=dx2, bf16_copy=True)

    d_w_out = _matmul("attn_out_dw", mixed, dx1_b, out_shape=(D_MODEL, D_MODEL), out_dtype=F32, grid=(1, 1, s // tk),
                      a_spec=_bs((tk, D_MODEL), lambda i, j, k: (k, 0)), b_spec=_bs((tk, D_MODEL), lambda i, j, k: (k, 0)),
                      o_spec=whole(D_MODEL, D_MODEL), contract=TN)
    token, early_grads_on = early_grads(d_w_out, d_w_up, d_w_down) if early_grads is not None else (None, None)
    dmixed = _matmul("attn_out_dx", dx1_b, w_out, out_shape=(s, D_MODEL), out_dtype=F32, grid=(s // MM_ROWS, 1, 1),
                     a_spec=row512(D_MODEL), b_spec=whole(D_MODEL, D_MODEL), o_spec=row512(D_MODEL), contract=NT,
                     after=token)
    do_a, do_b, d_ga, d_gb = _gate_bwd(dmixed, proj, o_a, o_b)
    if early_grads_on is not None:
        sinks1 = sinks1 + early_grads_on(d_ga)[0, :H_B]

    d_qb, dk_pad, dv_pad, dbias, dsink_rows = _win_bwd(proj_b, bias, sinks1, do_b, lse_b)
    wp = _win_param_grads(bucket, dbias, dsink_rows)[:, 0, :]
    d_rel_bias = wp[:, :NUM_BUCKETS].T
    d_sinks = wp[:, NUM_BUCKETS].reshape(1, H_B)
    d_kb = dk_pad[WINDOW:WINDOW + s].astype(BF16)
    d_vb = dv_pad[WINDOW:WINDOW + s].astype(BF16)

    dq, dk, dv = _mla_bwd(q, k, v, do_a, o_a, lse_a)
    dq_pre, dkv_pre, d_kr = _mla_bwd_prep(dq, dk, dv, cos_t, sin_t)
    th = min(s, HEAD_ROWS)
    hgrid = (s // th, 1, H_A)
    hblock = _bs((None, th, HEAD_PAD), lambda i, j, k: (k, i, 0))
    hrows = lambda w: _bs((th, w), lambda i, j, k: (i, 0))
    dqn = _matmul("q_up_dx", dq_pre, wq, out_shape=(s, Q_LORA), out_dtype=F32, grid=hgrid, a_spec=hblock,
                  b_spec=_bs((None, Q_LORA, HEAD_PAD), lambda i, j, k: (k, 0, 0)), o_spec=hrows(Q_LORA), contract=NT)
    dcn = _matmul("kv_up_dx", dkv_pre, wkv, out_shape=(s, KV_LORA), out_dtype=F32, grid=hgrid, a_spec=hblock,
                  b_spec=_bs((None, KV_LORA, HEAD_PAD), lambda i, j, k: (k, 0, 0)), o_spec=hrows(KV_LORA), contract=NT)
    wgrid = (H_A, 1, s // th)
    d_wq = _matmul("q_up_dw", qn, dq_pre, out_shape=(H_A, Q_LORA, HEAD_PAD), out_dtype=F32, grid=wgrid,
                   a_spec=_bs((th, Q_LORA), lambda i, j, k: (k, 0)), b_spec=_bs((None, th, HEAD_PAD), lambda i, j, k: (i, k, 0)),
                   o_spec=_bs((None, Q_LORA, HEAD_PAD), lambda i, j, k: (i, 0, 0)), contract=TN)
    d_wkv = _matmul("kv_up_dw", cn, dkv_pre, out_shape=(H_A, KV_LORA, HEAD_PAD), out_dtype=F32, grid=wgrid,
                    a_spec=_bs((th, KV_LORA), lambda i, j, k: (k, 0)), b_spec=_bs((None, th, HEAD_PAD), lambda i, j, k: (i, k, 0)),
                    o_spec=_bs((None, KV_LORA, HEAD_PAD), lambda i, j, k: (i, 0, 0)), contract=TN)
    d_qlat, d_gq = _rmsnorm_bwd("q_norm_bwd", dqn, proj, rstd_q, q_a_norm_g, Q_LORA, P_QLAT // Q_LORA, BF16)
    d_ckv, d_gkv = _rmsnorm_bwd("kv_norm_bwd", dcn, proj, rstd_c, kv_a_norm_g, KV_LORA, P_CKV // KV_LORA, BF16)

    dproj = jnp.concatenate([d_qb, d_ga, d_gb, d_qlat, d_kb, d_vb, d_ckv, d_kr], axis=1)
    d_w_in_p = _matmul("proj_dw", h1, dproj, out_shape=(D_MODEL, W_IN_PAD), out_dtype=F32, grid=(1, W_IN_PAD // 1024, s // tk),
                       a_spec=_bs((tk, D_MODEL), lambda i, j, k: (k, 0)), b_spec=_bs((tk, 1024), lambda i, j, k: (k, j)),
                       o_spec=_bs((D_MODEL, 1024), lambda i, j, k: (0, j)), contract=TN)
    token = last_grads(d_w_in_p, d_wq, d_wkv) if last_grads is not None else None
    dh1 = _matmul("proj_dx", dproj, w_in_p, out_shape=(s, D_MODEL), out_dtype=F32, grid=(s // 1024, 1, W_IN_PAD // 1024),
                  a_spec=_bs((1024, 1024), lambda i, j, k: (i, k)), b_spec=_bs((D_MODEL, 1024), lambda i, j, k: (0, k)),
                  o_spec=_bs((1024, D_MODEL), lambda i, j, k: (i, 0)), contract=NT, after=token)
    dx, d_norm1_g = _rmsnorm_bwd("norm1_bwd", dh1, x, rstd1, norm1_g, D_MODEL, 0, F32, res=dx1)

    grads = dict(
        norm1_g=d_norm1_g, w_in_p=d_w_in_p, q_a_norm_g=d_gq, wq=d_wq, kv_a_norm_g=d_gkv, wkv=d_wkv,
        rel_bias=d_rel_bias, sinks=d_sinks, w_out=d_w_out, norm2_g=d_norm2_g, w_up=d_w_up,
        conv_w=jnp.concatenate([d_conv_w2[0], d_conv_w2[1]], axis=1),
        conv_b=jnp.concatenate([d_conv_b2[0], d_conv_b2[1]], axis=1),
        w_down=d_w_down, final_norm_g=d_final_g.reshape(D_MODEL))
    return loss, dx, grads


def _wq_heads(w_q_b):
    w = w_q_b.reshape(Q_LORA, H_A, QK_NOPE + QK_ROPE).transpose(1, 0, 2)
    return jnp.pad(w, ((0, 0), (0, 0), (0, HEAD_PAD - QK_NOPE - QK_ROPE)))


def _wq_unheads(d_wq):
    return d_wq[:, :, :QK_NOPE + QK_ROPE].transpose(1, 0, 2).reshape(Q_LORA, H_A * (QK_NOPE + QK_ROPE))


def _wkv_heads(w_kv_b):
    return w_kv_b.reshape(KV_LORA, H_A, QK_NOPE + V_DIM).transpose(1, 0, 2)


def _wkv_unheads(d_wkv):
    return d_wkv.transpose(1, 0, 2).reshape(KV_LORA, H_A * (QK_NOPE + V_DIM))


SMALL = ("norm1_g", "q_a_norm_g", "kv_a_norm_g", "rel_bias", "sinks", "norm2_g", "conv_b", "final_norm_g")
FIRST = ("w_in", "w_q_b", "w_kv_b")
LATER = ("w_out", "w_up", "w_down")
BIG = FIRST + LATER


def kernel(x, positions, norm1_g, w_in, q_a_norm_g, w_q_b, kv_a_norm_g, w_kv_b, rel_bias, sinks, w_out, norm2_g, w_up, conv_w, conv_b, w_down, final_norm_g, loss_target, m_norm1_g, m_w_in, m_q_a_norm_g, m_w_q_b, m_kv_a_norm_g, m_w_kv_b, m_rel_bias, m_sinks, m_w_out, m_norm2_g, m_w_up, m_conv_w, m_conv_b, m_w_down, m_final_norm_g, v_norm1_g, v_w_in, v_q_a_norm_g, v_w_q_b, v_kv_a_norm_g, v_w_kv_b, v_rel_bias, v_sinks, v_w_out, v_norm2_g, v_w_up, v_conv_w, v_conv_b, v_w_down, v_final_norm_g):
    weights = dict(norm1_g=norm1_g, w_in=w_in, q_a_norm_g=q_a_norm_g, w_q_b=w_q_b, kv_a_norm_g=kv_a_norm_g,
                   w_kv_b=w_kv_b, rel_bias=rel_bias, sinks=sinks, w_out=w_out, norm2_g=norm2_g, w_up=w_up,
                   conv_w=conv_w, conv_b=conv_b, w_down=w_down, final_norm_g=final_norm_g)
    mom_m = dict(norm1_g=m_norm1_g, w_in=m_w_in, q_a_norm_g=m_q_a_norm_g, w_q_b=m_w_q_b, kv_a_norm_g=m_kv_a_norm_g,
                 w_kv_b=m_w_kv_b, rel_bias=m_rel_bias, sinks=m_sinks, w_out=m_w_out, norm2_g=m_norm2_g, w_up=m_w_up,
                 conv_w=m_conv_w, conv_b=m_conv_b, w_down=m_w_down, final_norm_g=m_final_norm_g)
    mom_v = dict(norm1_g=v_norm1_g, w_in=v_w_in, q_a_norm_g=v_q_a_norm_g, w_q_b=v_w_q_b, kv_a_norm_g=v_kv_a_norm_g,
                 w_kv_b=v_w_kv_b, rel_bias=v_rel_bias, sinks=v_sinks, w_out=v_w_out, norm2_g=v_norm2_g, w_up=v_w_up,
                 conv_w=v_conv_w, conv_b=v_conv_b, w_down=v_w_down, final_norm_g=v_final_norm_g)
    shard2d = {n: weights[n][0] for n in BIG}
    conv_w_shard = conv_w[0]
    xi, yi, ci = lax.axis_index("x"), lax.axis_index("y"), lax.axis_index("c")
    chip = (2 * xi + yi).astype(jnp.int32)

    core = ci.astype(jnp.int32).reshape(1)
    chip1 = chip.reshape(1)
    cat_cols = lambda a: jnp.concatenate([a[0], a[1], a[2], a[3]], axis=1)
    own_slot = lambda a, own: lax.dynamic_update_index_in_dim(a, own, chip, 0)
    halved = lambda a: a.reshape((2, a.shape[0] // 2) + a.shape[1:])
    quartered = lambda a: a.reshape(4, 2, a.shape[1] // 2, a.shape[2])

    send = [halved(shard2d[n].astype(BF16)) for n in FIRST] + [conv_w_shard]
    gathered = [own_slot(a, own) for a, own in zip(_allgather_weights(send, split=[True] * len(FIRST) + [False]), send)]
    g = {n: a.reshape((4,) + shard2d[n].shape) for n, a in zip(FIRST, gathered)}
    w_in_p = _w_in_from_shards(g["w_in"])
    wq = _wq_heads(cat_cols(g["w_q_b"]))
    wkv = _wkv_heads(cat_cols(g["w_kv_b"]))
    conv_w_f = cat_cols(gathered[-1])

    later = [shard2d[n].astype(BF16) for n in LATER]
    gather2 = _SplitExchange("gather_later", later, [(4,) + a.shape for a in later],
                             src_of=lambda ref, k, p, pk: ref, dst_of=lambda ref, k, p, pk: ref.at[p],
                             arrive_of=lambda ref, k, p, pk: ref.at[pk])
    norm1_g_in = norm1_g + gather2.start()[:1, :1]

    def late_weights(after):
        w_out_g, w_up_g, w_down_g = [own_slot(a, own) for own, a in zip(*gather2.wait(after))]
        return w_out_g.reshape(D_MODEL, D_MODEL), cat_cols(w_up_g), w_down_g.reshape(D_FF, D_MODEL)

    early = {}

    def early_grads(d_w_out, d_w_up, d_w_down):
        grads = [quartered(d_w_out.reshape(4, D_MODEL // 4, D_MODEL)), quartered(_col_shards(d_w_up)),
                 quartered(d_w_down.reshape(4, D_FF // 4, D_MODEL))]
        swap = _SplitExchange("rs_pair_exchange_early", grads, [(4,) + a.shape[2:] for a in grads], to_sibling=True,
                              src_of=lambda ref, k, p, pk: ref.at[:, pk], dst_of=lambda ref, k, p, pk: ref,
                              arrive_of=lambda ref, k, p, pk: ref)

        def on(after):
            kept, recv = swap.wait(after)
            early["pairs"] = [_rs_pair_add(f"rs_pair_add_{n}", core, gfull, r) for n, gfull, r in zip(LATER, kept, recv)]
            early["ici"] = _SplitExchange("rs_ici_early", early["pairs"], [(3,) + a.shape[1:] for a in early["pairs"]],
                                          src_of=lambda ref, k, p, pk: ref.at[pk], dst_of=lambda ref, k, p, pk: ref.at[k],
                                          arrive_of=lambda ref, k, p, pk: ref.at[k])
            return early["ici"].start()

        return swap.start(), on

    last = {}

    def last_grads(d_w_in_p, d_wq, d_wkv):
        grads = [quartered(_w_in_to_shards(d_w_in_p)), quartered(_col_shards(_wq_unheads(d_wq))),
                 quartered(_col_shards(_wkv_unheads(d_wkv)))]
        recv = _rs_pair_exchange("rs_pair_exchange_last", grads)
        last["pairs"] = [_rs_pair_add(f"rs_pair_add_{n}", core, gfull, r) for n, gfull, r in zip(FIRST, grads, recv)]
        last["ici"] = _SplitExchange("rs_ici_last", last["pairs"], [(3,) + a.shape[1:] for a in last["pairs"]],
                                     src_of=lambda ref, k, p, pk: ref.at[pk], dst_of=lambda ref, k, p, pk: ref.at[k],
                                     arrive_of=lambda ref, k, p, pk: ref.at[k])
        return last["ici"].start()

    loss, dx, gr = _local_step(x[0], positions, loss_target[0], norm1_g_in, w_in_p, q_a_norm_g, wq, kv_a_norm_g, wkv,
                               rel_bias, sinks, late_weights, norm2_g, conv_w_f, conv_b, final_norm_g, early_grads,
                               last_grads)

    last_pairs, last_recv = last["ici"].wait(dx)
    early_pairs, early_recv = early["ici"].wait(dx)
    pairs, recv2 = last_pairs + early_pairs, last_recv + early_recv
    halves = [_rs_final_add(f"rs_final_add_{n}", chip1, pr, r) for n, pr, r in zip(FIRST + LATER, pairs, recv2)]
    sibling_halves = _rs_pair_share(halves)

    as_rows = lambda a: a.reshape((-1, a.shape[-1]))
    summed = _small_allreduce([as_rows(gr[n]) for n in SMALL] + [gr["conv_w"], loss])
    small_g = dict(zip(SMALL, summed[:len(SMALL)]))
    conv_w_g = lax.dynamic_slice_in_dim(summed[len(SMALL)], chip * (2 * D_FF // 4), 2 * D_FF // 4, axis=1)
    loss_out = summed[-1].reshape(())

    out_g, out_d, out_m, out_v = {}, {}, {}, {}
    for n, mine, theirs in zip(FIRST + LATER, halves, sibling_halves):
        gsh, d, nm, nv = _adamw_halves(f"adamw_{n}", core, shard2d[n], mine, theirs, mom_m[n][0], mom_v[n][0])
        out_g[n], out_d[n], out_m[n], out_v[n] = gsh[None], d[None], nm[None], nv[None]
    names = SMALL + ("conv_w",)
    sg = [small_g[n] for n in SMALL] + [conv_w_g]
    ds, nms, nvs = _adamw_small([as_rows(weights[n]) for n in names], sg, [as_rows(mom_m[n]) for n in names],
                                [as_rows(mom_v[n]) for n in names])
    for n, gg, dd, mm, vv in zip(names, sg, ds, nms, nvs):
        shp = weights[n].shape
        out_g[n], out_d[n], out_m[n], out_v[n] = gg.reshape(shp), dd.reshape(shp), mm.reshape(shp), vv.reshape(shp)

    order = ("norm1_g", "w_in", "q_a_norm_g", "w_q_b", "kv_a_norm_g", "w_kv_b", "rel_bias", "sinks", "w_out",
             "norm2_g", "w_up", "conv_w", "conv_b", "w_down", "final_norm_g")
    return (loss_out, dx[None], *[out_g[n] for n in order], *[out_d[n] for n in order],
            *[out_m[n] for n in order], *[out_v[n] for n in order])
```

```python
import functools
import math

import jax
import jax.numpy as jnp
import numpy as np
from jax import lax
from jax.experimental import pallas as pl
from jax.experimental.pallas import tpu as pltpu

F32 = jnp.float32
BF16 = jnp.bfloat16
MESH = pl.DeviceIdType.MESH

D_MODEL = 1024
EPS = 1e-6
H_A = 8
QK_NOPE = 128
QK_ROPE = 64
V_DIM = 128
Q_LORA = 256
KV_LORA = 128
ROPE_THETA = 10000.0
H_B = 16
KV_B = 4
GROUP = 4
HD_B = 64
WINDOW = 128
Q_BLOCK = 128
NUM_BUCKETS = 32
MAX_DISTANCE = 128
D_FF = 2816
HEAD_PAD = 256

ADAM_LR = 0.001
ADAM_B1 = 0.9
ADAM_B2 = 0.999
ADAM_EPS = 1e-08
ADAM_WD = 0.01
ADAM_STEP = 10

LANES = 128
P_QB, P_GA, P_GB, P_QLAT, P_KB, P_VB, P_CKV, P_KR = 0, 1024, 2048, 3072, 3328, 3584, 3840, 3968
W_IN_PAD = 4096

NT = (((1,), (1,)), ((), ()))
NN = (((1,), (0,)), ((), ()))
TN = (((0,), (0,)), ((), ()))


def _arb(n):
    return pltpu.CompilerParams(dimension_semantics=("arbitrary",) * n)


def _matmul(name, a, b, *, out_shape, out_dtype, grid, a_spec, b_spec, o_spec, contract, add=None, bf16_copy=False,
            after=None):
    nk = grid[2]
    acc_shape = tuple(d for d in o_spec.block_shape if d is not None)
    n_in = 2 + (add is not None) + (after is not None)
    n_out = 2 if bf16_copy else 1

    def body(*refs):
        a_ref, b_ref = refs[:2]
        add_ref = refs[2] if add is not None else None
        o_refs = refs[n_in:n_in + n_out]
        scratch = refs[n_in + n_out:]
        prod = lax.dot_general(a_ref[...].astype(BF16), b_ref[...].astype(BF16), contract,
                               preferred_element_type=F32)

        def finish(val):
            if add_ref is not None:
                val = add_ref[...] + val
            o_refs[0][...] = val.astype(out_dtype)
            if bf16_copy:
                o_refs[1][...] = val.astype(BF16)

        if nk == 1:
            finish(prod)
        else:
            acc_ref = scratch[0]
            k = pl.program_id(2)

            @pl.when(k == 0)
            def _():
                acc_ref[...] = prod

            @pl.when((k > 0) & (k < nk - 1))
            def _():
                acc_ref[...] += prod

            @pl.when(k == nk - 1)
            def _():
                finish(acc_ref[...] + prod)

    in_specs = [a_spec, b_spec]
    args = [a, b]
    if add is not None:
        in_specs.append(o_spec)
        args.append(add)
    if after is not None:
        in_specs.append(pl.BlockSpec(memory_space=pl.ANY))
        args.append(after)
    out_shapes = [jax.ShapeDtypeStruct(out_shape, out_dtype)]
    if bf16_copy:
        out_shapes.append(jax.ShapeDtypeStruct(out_shape, BF16))
    res = pl.pallas_call(
        body, name=name, grid=grid, in_specs=in_specs, out_specs=[o_spec] * n_out, out_shape=out_shapes,
        scratch_shapes=[pltpu.VMEM(acc_shape, F32)] if nk > 1 else [],
        compiler_params=_arb(3),
    )(*args)
    return res if bf16_copy else res[0]


def _bs(block, fn):
    return pl.BlockSpec(block, fn)


def _rmsnorm_fwd(name, src, g, d, cb, ts=512):
    s = src.shape[0]

    def body(x_ref, g_ref, h_ref, r_ref):
        x = x_ref[...]
        r = lax.rsqrt(jnp.mean(x * x, axis=-1, keepdims=True) + EPS)
        h_ref[...] = (x * r * g_ref[...]).astype(BF16)
        r_ref[...] = r

    return pl.pallas_call(
        body, name=name, grid=(s // ts,),
        in_specs=[_bs((ts, d), lambda i: (i, cb)), _bs((1, d), lambda i: (0, 0))],
        out_specs=[_bs((ts, d), lambda i: (i, 0)), _bs((ts, 1), lambda i: (i, 0))],
        out_shape=[jax.ShapeDtypeStruct((s, d), BF16), jax.ShapeDtypeStruct((s, 1), F32)],
        compiler_params=_arb(1),
    )(src, g)


def _rmsnorm_bwd(name, dy, src, rstd, g, d, cb, out_dtype, res=None, bf16_copy=False, ts=512):
    s = src.shape[0]

    def body(*refs):
        dy_ref, x_ref, r_ref, g_ref = refs[:4]
        res_ref = refs[4] if res is not None else None
        dx_ref, dg_ref = refs[n_in:n_in + 2]
        dyv = dy_ref[...]
        r = r_ref[...]
        xhat = x_ref[...] * r
        dyh = dyv * g_ref[...]
        c = jnp.mean(dyh * xhat, axis=-1, keepdims=True)
        dx = r * (dyh - xhat * c)
        if res_ref is not None:
            dx = res_ref[...] + dx
        dx_ref[...] = dx.astype(out_dtype)
        if bf16_copy:
            refs[n_in + 2][...] = dx.astype(BF16)
        part = jnp.sum(dyv * xhat, axis=0, keepdims=True)

        @pl.when(pl.program_id(0) == 0)
        def _():
            dg_ref[...] = part

        @pl.when(pl.program_id(0) > 0)
        def _():
            dg_ref[...] += part

    in_specs = [_bs((ts, d), lambda i: (i, 0)), _bs((ts, d), lambda i: (i, cb)),
                _bs((ts, 1), lambda i: (i, 0)), _bs((1, d), lambda i: (0, 0))]
    args = [dy, src, rstd, g]
    if res is not None:
        in_specs.append(_bs((ts, d), lambda i: (i, 0)))
        args.append(res)
    n_in = len(args)
    out_specs = [_bs((ts, d), lambda i: (i, 0)), _bs((1, d), lambda i: (0, 0))]
    out_shape = [jax.ShapeDtypeStruct((s, d), out_dtype), jax.ShapeDtypeStruct((1, d), F32)]
    if bf16_copy:
        out_specs.append(_bs((ts, d), lambda i: (i, 0)))
        out_shape.append(jax.ShapeDtypeStruct((s, d), BF16))
    return pl.pallas_call(
        body, name=name, grid=(s // ts,), in_specs=in_specs, out_specs=out_specs, out_shape=out_shape,
        compiler_params=_arb(1),
    )(*args)


def _final_loss(x2, target, g, ts=512):
    s, d = x2.shape

    def body(x_ref, t_ref, g_ref, loss_ref, dx_ref, dg_ref, dxb_ref):
        x = x_ref[...]
        r = lax.rsqrt(jnp.mean(x * x, axis=-1, keepdims=True) + EPS)
        xhat = x * r
        gv = g_ref[...]
        err = xhat * gv - t_ref[...]
        lpart = 0.5 * jnp.sum(jnp.mean(err * err, axis=-1, keepdims=True), axis=0, keepdims=True)
        dyv = err * (1.0 / d)
        dyh = dyv * gv
        c = jnp.mean(dyh * xhat, axis=-1, keepdims=True)
        dx = r * (dyh - xhat * c)
        dx_ref[...] = dx
        dxb_ref[...] = dx.astype(BF16)
        gpart = jnp.sum(dyv * xhat, axis=0, keepdims=True)

        @pl.when(pl.program_id(0) == 0)
        def _():
            dg_ref[...] = gpart
            loss_ref[...] = lpart

        @pl.when(pl.program_id(0) > 0)
        def _():
            dg_ref[...] += gpart
            loss_ref[...] += lpart

    return pl.pallas_call(
        body, name="final_loss", grid=(s // ts,),
        in_specs=[_bs((ts, d), lambda i: (i, 0)), _bs((ts, d), lambda i: (i, 0)), _bs((1, d), lambda i: (0, 0))],
        out_specs=[_bs((1, 1), lambda i: (0, 0)), _bs((ts, d), lambda i: (i, 0)), _bs((1, d), lambda i: (0, 0)),
                   _bs((ts, d), lambda i: (i, 0))],
        out_shape=[jax.ShapeDtypeStruct((1, 1), F32), jax.ShapeDtypeStruct((s, d), F32),
                   jax.ShapeDtypeStruct((1, d), F32), jax.ShapeDtypeStruct((s, d), BF16)],
        compiler_params=_arb(1),
    )(x2, target, g)


def _swap_halves(t):
    lane = lax.broadcasted_iota(jnp.int32, t.shape, 1)
    return jnp.where(lane < 32, pltpu.roll(t, 96, 1), pltpu.roll(t, 32, 1))


def _rope_fwd(t, cos_t, sin_t):
    return t * cos_t + _swap_halves(t) * sin_t


def _rope_bwd(dt, cos_t, sin_t):
    return dt * cos_t - _swap_halves(dt) * sin_t


def _lat_norms(proj, gq, gkv, ts=512):
    s = proj.shape[0]

    def body(q_ref, c_ref, gq_ref, gkv_ref, qn_ref, cn_ref, rq_ref, rc_ref):
        q = q_ref[...]
        rq = lax.rsqrt(jnp.mean(q * q, axis=-1, keepdims=True) + EPS)
        qn_ref[...] = (q * rq * gq_ref[...]).astype(BF16)
        rq_ref[...] = rq
        cv = c_ref[...]
        rc = lax.rsqrt(jnp.mean(cv * cv, axis=-1, keepdims=True) + EPS)
        cn_ref[...] = (cv * rc * gkv_ref[...]).astype(BF16)
        rc_ref[...] = rc

    return pl.pallas_call(
        body, name="lat_norms", grid=(s // ts,),
        in_specs=[_bs((ts, Q_LORA), lambda i: (i, P_QLAT // Q_LORA)),
                  _bs((ts, KV_LORA), lambda i: (i, P_CKV // KV_LORA)),
                  _bs((1, Q_LORA), lambda i: (0, 0)), _bs((1, KV_LORA), lambda i: (0, 0))],
        out_specs=[_bs((ts, Q_LORA), lambda i: (i, 0)), _bs((ts, KV_LORA), lambda i: (i, 0)),
                   _bs((ts, 1), lambda i: (i, 0)), _bs((ts, 1), lambda i: (i, 0))],
        out_shape=[jax.ShapeDtypeStruct((s, Q_LORA), BF16), jax.ShapeDtypeStruct((s, KV_LORA), BF16),
                   jax.ShapeDtypeStruct((s, 1), F32), jax.ShapeDtypeStruct((s, 1), F32)],
        compiler_params=_arb(1),
    )(proj, proj, gq, gkv)


HEAD_ROWS = 2048
DW_ROWS = 2048
MM_ROWS = 1024


def _q_heads(qn, wq, cos_t, sin_t):
    s = qn.shape[0]
    ts = min(s, HEAD_ROWS)

    def body(qn_ref, w_ref, cos_ref, sin_ref, q_ref):
        o = jnp.dot(qn_ref[...], w_ref[...], preferred_element_type=F32)
        q_ref[:, :LANES] = o[:, :LANES].astype(BF16)
        q_ref[:, LANES:] = _rope_fwd(o[:, LANES:], cos_ref[...], sin_ref[...]).astype(BF16)

    return pl.pallas_call(
        body, name="q_heads", grid=(H_A, s // ts),
        in_specs=[_bs((ts, Q_LORA), lambda h, i: (i, 0)), _bs((None, Q_LORA, HEAD_PAD), lambda h, i: (h, 0, 0)),
                  _bs((ts, LANES), lambda h, i: (i, 0)), _bs((ts, LANES), lambda h, i: (i, 0))],
        out_specs=_bs((None, ts, HEAD_PAD), lambda h, i: (h, i, 0)),
        out_shape=jax.ShapeDtypeStruct((H_A, s, HEAD_PAD), BF16),
        compiler_params=_arb(2),
    )(qn, wq, cos_t, sin_t)


def _kv_heads(cn, wkv, proj, cos_t, sin_t):
    s = cn.shape[0]
    ts = min(s, HEAD_ROWS)

    def body(cn_ref, w_ref, kr_ref, cos_ref, sin_ref, k_ref, v_ref):
        o = jnp.dot(cn_ref[...], w_ref[...], preferred_element_type=F32)
        k_ref[:, :LANES] = o[:, :LANES].astype(BF16)
        k_ref[:, LANES:] = _rope_fwd(kr_ref[...], cos_ref[...], sin_ref[...]).astype(BF16)
        v_ref[...] = o[:, LANES:].astype(BF16)

    return pl.pallas_call(
        body, name="kv_heads", grid=(H_A, s // ts),
        in_specs=[_bs((ts, KV_LORA), lambda h, i: (i, 0)),
                  _bs((None, KV_LORA, QK_NOPE + V_DIM), lambda h, i: (h, 0, 0)),
                  _bs((ts, LANES), lambda h, i: (i, P_KR // LANES)),
                  _bs((ts, LANES), lambda h, i: (i, 0)), _bs((ts, LANES), lambda h, i: (i, 0))],
        out_specs=[_bs((None, ts, HEAD_PAD), lambda h, i: (h, i, 0)), _bs((None, ts, V_DIM), lambda h, i: (h, i, 0))],
        out_shape=[jax.ShapeDtypeStruct((H_A, s, HEAD_PAD), BF16), jax.ShapeDtypeStruct((H_A, s, V_DIM), BF16)],
        compiler_params=_arb(2),
    )(cn, wkv, proj, cos_t, sin_t)


MLA_SCALE = 1.0 / math.sqrt(QK_NOPE + QK_ROPE)
LOG2E = math.log2(math.e)
MLA_EXP2_SCALE = MLA_SCALE * LOG2E


def _lane_tiles(a):
    return [a[:, j * LANES:(j + 1) * LANES] for j in range(a.shape[1] // LANES)]


MLA_SUB = 512


def _mla_fwd(q, k, v, tq=512, tk=1024):
    s = q.shape[1]
    tq = min(tq, s)
    nk = s // tk

    def body(q_ref, k_ref, v_ref, o_ref, lse_ref, m_ref, l_ref, acc_ref):
        m_ref[...] = jnp.full(m_ref.shape, -jnp.inf, F32)
        l_ref[...] = jnp.zeros(l_ref.shape, F32)
        acc_ref[...] = jnp.zeros(acc_ref.shape, F32)

        def step(c, carry):
            rows = pl.ds(pl.multiple_of(c * tk, tk), tk)
            for sub in range(tq // MLA_SUB):
                qr = slice(sub * MLA_SUB, (sub + 1) * MLA_SUB)
                raw = lax.dot_general(q_ref[qr, :], k_ref[rows, :], NT, preferred_element_type=F32)
                m_prev = m_ref[qr, :]
                m_new = jnp.maximum(m_prev, jnp.max(raw, axis=-1, keepdims=True))
                alpha = jnp.exp2((m_prev - m_new) * MLA_EXP2_SCALE)
                ps = [jnp.exp2((t - m_new) * MLA_EXP2_SCALE) for t in _lane_tiles(raw)]
                l_ref[qr, :] = alpha * l_ref[qr, :] + functools.reduce(lambda a, b: a + b, ps)
                p = jnp.concatenate(ps, axis=1).astype(BF16)
                acc_ref[qr, :] = alpha * acc_ref[qr, :] + jnp.dot(p, v_ref[rows, :], preferred_element_type=F32)
                m_ref[qr, :] = m_new
            return carry

        lax.fori_loop(0, nk, step, 0, unroll=True)
        l = jnp.sum(l_ref[...], axis=-1, keepdims=True)
        o_ref[...] = acc_ref[...] / l
        lse_ref[...] = m_ref[...] * MLA_SCALE + jnp.log(l)

    return pl.pallas_call(
        body, name="mla_fwd", grid=(H_A, s // tq),
        in_specs=[_bs((None, tq, HEAD_PAD), lambda h, i: (h, i, 0)),
                  _bs((None, s, HEAD_PAD), lambda h, i: (h, 0, 0)),
                  _bs((None, s, V_DIM), lambda h, i: (h, 0, 0))],
        out_specs=[_bs((tq, V_DIM), lambda h, i: (i, h)), _bs((None, tq, LANES), lambda h, i: (h, i, 0))],
        out_shape=[jax.ShapeDtypeStruct((s, H_A * V_DIM), F32), jax.ShapeDtypeStruct((H_A, s, LANES), F32)],
        scratch_shapes=[pltpu.VMEM((tq, LANES), F32), pltpu.VMEM((tq, LANES), F32), pltpu.VMEM((tq, V_DIM), F32)],
        compiler_params=_arb(2),
    )(q, k, v)


def _mla_bwd(q, k, v, do, o, lse, tq=512, tk=512):
    s = q.shape[1]
    nq = s // tq

    def body(q_ref, k_ref, v_ref, do_ref, o_ref, lse_ref, dq_ref, dk_ref, dv_ref, delta_ref):
        @pl.when(pl.program_id(1) == 0)
        def _():
            def init(c, carry):
                rows = pl.ds(pl.multiple_of(c * tq, tq), tq)
                delta = jnp.sum(do_ref[rows, :] * o_ref[rows, :], axis=-1, keepdims=True)
                delta_ref[rows, :] = jnp.broadcast_to(delta, (tq, LANES))
                dq_ref[rows, :] = jnp.zeros((tq, HEAD_PAD), F32)
                return carry

            lax.fori_loop(0, nq, init, 0)

        dk_ref[...] = jnp.zeros(dk_ref.shape, F32)
        dv_ref[...] = jnp.zeros(dv_ref.shape, F32)
        kb = k_ref[...]
        vb = v_ref[...]

        def step(c, carry):
            rows = pl.ds(pl.multiple_of(c * tq, tq), tq)
            qc = q_ref[rows, :]
            doc = do_ref[rows, :].astype(BF16)
            raw = lax.dot_general(qc, kb, NT, preferred_element_type=F32)
            dp = lax.dot_general(doc, vb, NT, preferred_element_type=F32)
            lse2 = lse_ref[rows, :] * LOG2E
            delta = delta_ref[rows, :]
            ps = [jnp.exp2(t * MLA_EXP2_SCALE - lse2) for t in _lane_tiles(raw)]
            dss = [pj * (dj - delta) * MLA_SCALE for pj, dj in zip(ps, _lane_tiles(dp))]
            p = jnp.concatenate(ps, axis=1).astype(BF16)
            ds = jnp.concatenate(dss, axis=1).astype(BF16)
            dv_ref[...] += lax.dot_general(p, doc, TN, preferred_element_type=F32)
            dk_ref[...] += lax.dot_general(ds, qc, TN, preferred_element_type=F32)
            dq_ref[rows, :] += jnp.dot(ds, kb, preferred_element_type=F32)
            return carry

        lax.fori_loop(0, nq, step, 0, unroll=True)

    return pl.pallas_call(
        body, name="mla_bwd", grid=(H_A, s // tk),
        in_specs=[_bs((None, s, HEAD_PAD), lambda h, j: (h, 0, 0)),
                  _bs((None, tk, HEAD_PAD), lambda h, j: (h, j, 0)),
                  _bs((None, tk, V_DIM), lambda h, j: (h, j, 0)),
                  _bs((s, V_DIM), lambda h, j: (0, h)), _bs((s, V_DIM), lambda h, j: (0, h)),
                  _bs((None, s, LANES), lambda h, j: (h, 0, 0))],
        out_specs=[_bs((None, s, HEAD_PAD), lambda h, j: (h, 0, 0)),
                   _bs((None, tk, HEAD_PAD), lambda h, j: (h, j, 0)),
                   _bs((None, tk, V_DIM), lambda h, j: (h, j, 0))],
        out_shape=[jax.ShapeDtypeStruct((H_A, s, HEAD_PAD), F32), jax.ShapeDtypeStruct((H_A, s, HEAD_PAD), F32),
                   jax.ShapeDtypeStruct((H_A, s, V_DIM), F32)],
        scratch_shapes=[pltpu.VMEM((s, LANES), F32)],
        compiler_params=_arb(2),
    )(q, k, v, do, o, lse)


def _mla_bwd_prep(dq, dk, dv, cos_t, sin_t, ts=256):
    s = dq.shape[1]

    def body(dq_ref, dk_ref, dv_ref, cos_ref, sin_ref, dqp_ref, dkvp_ref, dkr_ref):
        cos_v = cos_ref[...]
        sin_v = sin_ref[...]
        kr = jnp.zeros((ts, LANES), F32)
        for h in range(H_A):
            dqp_ref[h, :, :LANES] = dq_ref[h, :, :LANES].astype(BF16)
            dqp_ref[h, :, LANES:] = _rope_bwd(dq_ref[h, :, LANES:], cos_v, sin_v).astype(BF16)
            dkvp_ref[h, :, :LANES] = dk_ref[h, :, :LANES].astype(BF16)
            dkvp_ref[h, :, LANES:] = dv_ref[h].astype(BF16)
            kr = kr + dk_ref[h, :, LANES:]
        dkr_ref[...] = _rope_bwd(kr, cos_v, sin_v).astype(BF16)

    blk3 = lambda w: _bs((H_A, ts, w), lambda i: (0, i, 0))
    return pl.pallas_call(
        body, name="mla_bwd_prep", grid=(s // ts,),
        in_specs=[blk3(HEAD_PAD), blk3(HEAD_PAD), blk3(V_DIM),
                  _bs((ts, LANES), lambda i: (i, 0)), _bs((ts, LANES), lambda i: (i, 0))],
        out_specs=[blk3(HEAD_PAD), blk3(HEAD_PAD), _bs((ts, LANES), lambda i: (i, 0))],
        out_shape=[jax.ShapeDtypeStruct((H_A, s, HEAD_PAD), BF16), jax.ShapeDtypeStruct((H_A, s, HEAD_PAD), BF16),
                   jax.ShapeDtypeStruct((s, LANES), BF16)],
        compiler_params=_arb(1),
    )(dq, dk, dv, cos_t, sin_t)


WIN_SCALE = 1.0 / math.sqrt(HD_B)
SPAN = Q_BLOCK + 2 * WINDOW


def _t5_bucket_table():
    a = jnp.arange(Q_BLOCK, dtype=jnp.int32)[:, None]
    c = jnp.arange(SPAN, dtype=jnp.int32)[None, :]
    rel = c - WINDOW - a
    nb = NUM_BUCKETS // 2
    max_exact = nb // 2
    base = (rel > 0).astype(jnp.int32) * nb
    n = jnp.abs(rel)
    nf = jnp.maximum(n, 1).astype(F32)
    large = max_exact + (jnp.log(nf / max_exact) / math.log(MAX_DISTANCE / max_exact)
                         * (nb - max_exact)).astype(jnp.int32)
    large = jnp.minimum(large, nb - 1)
    return base + jnp.where(n < max_exact, n, large)


def _win_bias(bucket, rel_bias):
    def body(rb_ref, bk_ref, o_ref):
        h = pl.program_id(0)
        bk = bk_ref[...]
        acc = jnp.zeros((Q_BLOCK, SPAN), F32)
        for b in range(NUM_BUCKETS):
            acc = jnp.where(bk == b, rb_ref[b, h], acc)
        o_ref[...] = acc

    return pl.pallas_call(
        body, name="win_bias", grid=(H_B,),
        in_specs=[pl.BlockSpec(memory_space=pltpu.SMEM), _bs((Q_BLOCK, SPAN), lambda h: (0, 0))],
        out_specs=_bs((None, Q_BLOCK, SPAN), lambda h: (h, 0, 0)),
        out_shape=jax.ShapeDtypeStruct((H_B, Q_BLOCK, SPAN), F32),
        compiler_params=_arb(1),
    )(rel_bias, bucket)


GROUP_W = GROUP * HD_B


def _win_kv_rows(n, j, nblk):
    blk = jnp.clip(n + j - 1, 0, nblk - 1)
    return pl.ds(pl.multiple_of(blk * Q_BLOCK, Q_BLOCK), Q_BLOCK)


def _win_head_cols(kv):
    return slice(kv * HD_B, (kv + 1) * HD_B)


def _win_stack(ref, kv):
    return jnp.concatenate([ref[:, kv * GROUP_W + g * HD_B:kv * GROUP_W + (g + 1) * HD_B] for g in range(GROUP)], axis=0)


def _win_unstack(ref, kv, val):
    for g in range(GROUP):
        ref[:, kv * GROUP_W + g * HD_B:kv * GROUP_W + (g + 1) * HD_B] = val[g * Q_BLOCK:(g + 1) * Q_BLOCK].astype(ref.dtype)


def _win_scores(q, k_ref, kv, bias_ref, n, nblk):
    a = lax.broadcasted_iota(jnp.int32, (GROUP, Q_BLOCK, Q_BLOCK), 1)
    cc = lax.broadcasted_iota(jnp.int32, (GROUP, Q_BLOCK, Q_BLOCK), 2)
    valid = [(cc >= a) & (n > 0), None, (cc <= a) & (n < nblk - 1)]
    out = []
    for j in range(3):
        sc = lax.dot_general(q, k_ref[_win_kv_rows(n, j, nblk), _win_head_cols(kv)], NT, preferred_element_type=F32)
        sc = (sc.reshape(GROUP, Q_BLOCK, Q_BLOCK) * WIN_SCALE
              + bias_ref[kv * GROUP:(kv + 1) * GROUP, :, j * Q_BLOCK:(j + 1) * Q_BLOCK])
        if valid[j] is not None:
            sc = jnp.where(valid[j], sc, -1e30)
        out.append(sc)
    return out


def _win_sink(sink_ref, kv):
    hs = lax.broadcasted_iota(jnp.int32, (GROUP, Q_BLOCK, 1), 0)
    sk = jnp.zeros((GROUP, Q_BLOCK, 1), F32)
    for g in range(GROUP):
        sk = jnp.where(hs == g, sink_ref[kv * GROUP + g], sk)
    return sk


def _win_fwd(proj_b, bias, sinks):
    s = proj_b.shape[0]
    nblk = s // Q_BLOCK
    rows = GROUP * Q_BLOCK

    def body(sink_ref, q_ref, k_ref, v_ref, bias_ref, o_ref, lse_ref):
        n = pl.program_id(0)
        for kv in range(KV_B):
            sk = _win_sink(sink_ref, kv)
            q = _win_stack(q_ref, kv)
            ss = _win_scores(q, k_ref, kv, bias_ref, n, nblk)
            m = jnp.maximum(jnp.max(jnp.maximum(jnp.maximum(ss[0], ss[1]), ss[2]), axis=2, keepdims=True), sk)
            es = [jnp.exp(sc - m) for sc in ss]
            l = jnp.sum(es[0] + es[1] + es[2], axis=2, keepdims=True) + jnp.exp(sk - m)
            acc = jnp.zeros((rows, HD_B), F32)
            for j, e in enumerate(es):
                p = (e / l).astype(BF16).reshape(rows, Q_BLOCK)
                acc = acc + jnp.dot(p, v_ref[_win_kv_rows(n, j, nblk), _win_head_cols(kv)],
                                    preferred_element_type=F32)
            _win_unstack(o_ref, kv, acc)
            lse_ref[kv * GROUP:(kv + 1) * GROUP] = m + jnp.log(l)

    kv_w = KV_B * HD_B
    return pl.pallas_call(
        body, name="win_fwd", grid=(nblk,),
        in_specs=[pl.BlockSpec(memory_space=pltpu.SMEM), _bs((Q_BLOCK, H_B * HD_B), lambda n: (n, P_QB // (H_B * HD_B))),
                  _bs((s, kv_w), lambda n: (0, P_KB // kv_w)), _bs((s, kv_w), lambda n: (0, P_VB // kv_w)),
                  _bs((H_B, Q_BLOCK, SPAN), lambda n: (0, 0, 0))],
        out_specs=[_bs((Q_BLOCK, H_B * HD_B), lambda n: (n, 0)), _bs((H_B, Q_BLOCK, 1), lambda n: (0, n, 0))],
        out_shape=[jax.ShapeDtypeStruct((s, H_B * HD_B), F32), jax.ShapeDtypeStruct((H_B, s, 1), F32)],
        compiler_params=_arb(1),
    )(sinks, proj_b, proj_b, proj_b, bias)


def _win_bwd(proj_b, bias, sinks, do_b, lse):
    s = proj_b.shape[0]
    nblk = s // Q_BLOCK
    rows = GROUP * Q_BLOCK
    spad = s + 2 * WINDOW

    def body(sink_ref, q_ref, k_ref, v_ref, bias_ref, do_ref, lse_ref, dq_ref, dk_ref, dv_ref, db_ref, dsk_ref):
        n = pl.program_id(0)

        @pl.when(n == 0)
        def _():
            dk_ref[...] = jnp.zeros(dk_ref.shape, F32)
            dv_ref[...] = jnp.zeros(dv_ref.shape, F32)
            db_ref[...] = jnp.zeros(db_ref.shape, F32)
            dsk_ref[...] = jnp.zeros(dsk_ref.shape, F32)

        for kv in range(KV_B):
            heads = slice(kv * GROUP, (kv + 1) * GROUP)
            sk = _win_sink(sink_ref, kv)
            q = _win_stack(q_ref, kv)
            dob = _win_stack(do_ref, kv)
            lse_v = lse_ref[heads]
            ss = _win_scores(q, k_ref, kv, bias_ref, n, nblk)
            ps = [jnp.exp(sc - lse_v) for sc in ss]
            dps = [lax.dot_general(dob, v_ref[_win_kv_rows(n, j, nblk), _win_head_cols(kv)], NT,
                                   preferred_element_type=F32).reshape(GROUP, Q_BLOCK, Q_BLOCK) for j in range(3)]
            delta = jnp.sum(ps[0] * dps[0] + ps[1] * dps[1] + ps[2] * dps[2], axis=2, keepdims=True)
            dq = jnp.zeros((rows, HD_B), F32)
            for j in range(3):
                ds = ps[j] * (dps[j] - delta)
                db_ref[heads, :, j * Q_BLOCK:(j + 1) * Q_BLOCK] += ds
                dsb = (ds * WIN_SCALE).astype(BF16).reshape(rows, Q_BLOCK)
                dq = dq + jnp.dot(dsb, k_ref[_win_kv_rows(n, j, nblk), _win_head_cols(kv)],
                                  preferred_element_type=F32)
                krows = pl.ds(pl.multiple_of((n + j) * Q_BLOCK, Q_BLOCK), Q_BLOCK)
                dk_ref[krows, _win_head_cols(kv)] += lax.dot_general(dsb, q, TN, preferred_element_type=F32)
                dv_ref[krows, _win_head_cols(kv)] += lax.dot_general(
                    ps[j].astype(BF16).reshape(rows, Q_BLOCK), dob, TN, preferred_element_type=F32)
            dsk_ref[heads] += -(jnp.exp(sk - lse_v) * delta)
            _win_unstack(dq_ref, kv, dq)

    kv_w = KV_B * HD_B
    qspec = _bs((Q_BLOCK, H_B * HD_B), lambda n: (n, 0))
    kacc = _bs((spad, kv_w), lambda n: (0, 0))
    return pl.pallas_call(
        body, name="win_bwd", grid=(nblk,),
        in_specs=[pl.BlockSpec(memory_space=pltpu.SMEM), _bs((Q_BLOCK, H_B * HD_B), lambda n: (n, P_QB // (H_B * HD_B))),
                  _bs((s, kv_w), lambda n: (0, P_KB // kv_w)), _bs((s, kv_w), lambda n: (0, P_VB // kv_w)),
                  _bs((H_B, Q_BLOCK, SPAN), lambda n: (0, 0, 0)), qspec, _bs((H_B, Q_BLOCK, 1), lambda n: (0, n, 0))],
        out_specs=[qspec, kacc, kacc, _bs((H_B, Q_BLOCK, SPAN), lambda n: (0, 0, 0)),
                   _bs((H_B, Q_BLOCK, 1), lambda n: (0, 0, 0))],
        out_shape=[jax.ShapeDtypeStruct((s, H_B * HD_B), BF16), jax.ShapeDtypeStruct((spad, kv_w), F32),
                   jax.ShapeDtypeStruct((spad, kv_w), F32), jax.ShapeDtypeStruct((H_B, Q_BLOCK, SPAN), F32),
                   jax.ShapeDtypeStruct((H_B, Q_BLOCK, 1), F32)],
        compiler_params=_arb(1),
    )(sinks, proj_b, proj_b, proj_b, bias, do_b, lse)


def _win_param_grads(bucket, dbias, dsink_rows):
    def body(bk_ref, db_ref, ds_ref, o_ref):
        bk = bk_ref[...]
        dbv = db_ref[...]
        lane = lax.broadcasted_iota(jnp.int32, (1, LANES), 1)
        res = jnp.zeros((1, LANES), F32)
        for b in range(NUM_BUCKETS):
            tot = jnp.sum(jnp.sum(jnp.where(bk == b, dbv, 0.0), axis=1, keepdims=True), axis=0, keepdims=True)
            res = jnp.where(lane == b, tot, res)
        stot = jnp.sum(ds_ref[...], axis=0, keepdims=True)
        o_ref[...] = jnp.where(lane == NUM_BUCKETS, stot, res)

    return pl.pallas_call(
        body, name="win_param_grads", grid=(H_B,),
        in_specs=[_bs((Q_BLOCK, SPAN), lambda h: (0, 0)), _bs((None, Q_BLOCK, SPAN), lambda h: (h, 0, 0)),
                  _bs((None, Q_BLOCK, 1), lambda h: (h, 0, 0))],
        out_specs=_bs((None, 1, LANES), lambda h: (h, 0, 0)),
        out_shape=jax.ShapeDtypeStruct((H_B, 1, LANES), F32),
        compiler_params=_arb(1),
    )(bucket, dbias, dsink_rows)


def _gate_fwd(proj, o_a, o_b, ts=256):
    s = o_a.shape[0]
    wide = lambda cb: _bs((ts, D_MODEL), lambda i: (i, cb))

    def body(ga_ref, gb_ref, oa_ref, ob_ref, m_ref):
        m_ref[...] = (jax.nn.sigmoid(ga_ref[...]) * oa_ref[...]
                      + jax.nn.sigmoid(gb_ref[...]) * ob_ref[...]).astype(BF16)

    return pl.pallas_call(
        body, name="gate_fwd", grid=(s // ts,),
        in_specs=[wide(P_GA // D_MODEL), wide(P_GB // D_MODEL), wide(0), wide(0)],
        out_specs=wide(0), out_shape=jax.ShapeDtypeStruct((s, D_MODEL), BF16),
        compiler_params=_arb(1),
    )(proj, proj, o_a, o_b)


def _gate_bwd(dmixed, proj, o_a, o_b, ts=256):
    s = o_a.shape[0]
    wide = lambda cb: _bs((ts, D_MODEL), lambda i: (i, cb))

    def body(dm_ref, ga_ref, gb_ref, oa_ref, ob_ref, doa_ref, dob_ref, dga_ref, dgb_ref):
        dm = dm_ref[...]
        sa = jax.nn.sigmoid(ga_ref[...])
        sb = jax.nn.sigmoid(gb_ref[...])
        doa_ref[...] = dm * sa
        dob_ref[...] = (dm * sb).astype(BF16)
        dga_ref[...] = (dm * oa_ref[...] * (sa * (1.0 - sa))).astype(BF16)
        dgb_ref[...] = (dm * ob_ref[...] * (sb * (1.0 - sb))).astype(BF16)

    return pl.pallas_call(
        body, name="gate_bwd", grid=(s // ts,),
        in_specs=[wide(0), wide(P_GA // D_MODEL), wide(P_GB // D_MODEL), wide(0), wide(0)],
        out_specs=[wide(0)] * 4,
        out_shape=[jax.ShapeDtypeStruct((s, D_MODEL), F32), jax.ShapeDtypeStruct((s, D_MODEL), BF16),
                   jax.ShapeDtypeStruct((s, D_MODEL), BF16), jax.ShapeDtypeStruct((s, D_MODEL), BF16)],
        compiler_params=_arb(1),
    )(dmixed, proj, proj, o_a, o_b)


CONV_CHUNK = 128
N_SLAB = D_FF // LANES


def _shifted(ref, c, nchunks):
    r0 = c * CONV_CHUNK
    cur = ref[r0:r0 + CONV_CHUNK, :]
    row = lax.broadcasted_iota(jnp.int32, (8, LANES), 0)
    if c > 0:
        prev = ref[r0 - 1:r0 - 1 + CONV_CHUNK, :]
    else:
        down = pltpu.roll(cur, 1, 0)
        prev = jnp.concatenate([jnp.where(row == 0, 0.0, down[:8]), down[8:]], axis=0)
    if c < nchunks - 1:
        nxt = ref[r0 + 1:r0 + 1 + CONV_CHUNK, :]
    else:
        up = pltpu.roll(cur, CONV_CHUNK - 1, 0)
        nxt = jnp.concatenate([up[:-8], jnp.where(row == 7, 0.0, up[-8:])], axis=0)
    return prev, cur, nxt


def _conv_taps(ref, w_ref, b_ref, c, nchunks):
    prev, cur, nxt = _shifted(ref, c, nchunks)
    conv = prev * w_ref[0:1, :] + cur * w_ref[1:2, :] + nxt * w_ref[2:3, :] + b_ref[...]
    return conv, prev, cur, nxt


def _convffn_fwd(u, conv_w, conv_b):
    s = u.shape[0]
    nchunks = s // CONV_CHUNK

    def body(ug_ref, uv_ref, wg_ref, wv_ref, bg_ref, bv_ref, f_ref):
        for c in range(nchunks):
            cg = _conv_taps(ug_ref, wg_ref, bg_ref, c, nchunks)[0]
            cv = _conv_taps(uv_ref, wv_ref, bv_ref, c, nchunks)[0]
            f_ref[c * CONV_CHUNK:(c + 1) * CONV_CHUNK, :] = (cg * jax.nn.sigmoid(cg) * cv).astype(BF16)

    slab = lambda off: _bs((s, LANES), lambda j: (0, off + j))
    wsl = lambda off: _bs((3, LANES), lambda j: (0, off + j))
    bsl = lambda off: _bs((1, LANES), lambda j: (0, off + j))
    return pl.pallas_call(
        body, name="convffn_fwd", grid=(N_SLAB,),
        in_specs=[slab(0), slab(N_SLAB), wsl(0), wsl(N_SLAB), bsl(0), bsl(N_SLAB)],
        out_specs=slab(0), out_shape=jax.ShapeDtypeStruct((s, D_FF), BF16),
        compiler_params=_arb(1),
    )(u, u, conv_w, conv_w, conv_b, conv_b)


def _convffn_bwd(u, conv_w, conv_b, df):
    s = u.shape[0]
    nchunks = s // CONV_CHUNK

    def body(ug_ref, uv_ref, wg_ref, wv_ref, bg_ref, bv_ref, df_ref, du_ref, dw_ref, db_ref, dcg_ref, dcv_ref):
        dwg = [jnp.zeros((1, LANES), F32) for _ in range(3)]
        dwv = [jnp.zeros((1, LANES), F32) for _ in range(3)]
        dbg = jnp.zeros((1, LANES), F32)
        dbv = jnp.zeros((1, LANES), F32)
        for c in range(nchunks):
            rows = slice(c * CONV_CHUNK, (c + 1) * CONV_CHUNK)
            cg, gp, gc, gn = _conv_taps(ug_ref, wg_ref, bg_ref, c, nchunks)
            cv, vp, vc, vn = _conv_taps(uv_ref, wv_ref, bv_ref, c, nchunks)
            dfv = df_ref[rows, :]
            sg = jax.nn.sigmoid(cg)
            dcg = dfv * cv * (sg * (1.0 + cg * (1.0 - sg)))
            dcv = dfv * (cg * sg)
            dcg_ref[rows, :] = dcg
            dcv_ref[rows, :] = dcv
            for t, (tg, tv) in enumerate(((gp, vp), (gc, vc), (gn, vn))):
                dwg[t] = dwg[t] + jnp.sum(tg * dcg, axis=0, keepdims=True)
                dwv[t] = dwv[t] + jnp.sum(tv * dcv, axis=0, keepdims=True)
            dbg = dbg + jnp.sum(dcg, axis=0, keepdims=True)
            dbv = dbv + jnp.sum(dcv, axis=0, keepdims=True)
        for t in range(3):
            dw_ref[0, t:t + 1, :] = dwg[t]
            dw_ref[1, t:t + 1, :] = dwv[t]
        db_ref[0] = dbg
        db_ref[1] = dbv
        for half, (dc_ref, w_ref) in enumerate(((dcg_ref, wg_ref), (dcv_ref, wv_ref))):
            for c in range(nchunks):
                prev, cur, nxt = _shifted(dc_ref, c, nchunks)
                du = nxt * w_ref[0:1, :] + cur * w_ref[1:2, :] + prev * w_ref[2:3, :]
                du_ref[half, c * CONV_CHUNK:(c + 1) * CONV_CHUNK, :] = du.astype(BF16)

    slab = lambda off: _bs((s, LANES), lambda j: (0, off + j))
    wsl = lambda off: _bs((3, LANES), lambda j: (0, off + j))
    bsl = lambda off: _bs((1, LANES), lambda j: (0, off + j))
    return pl.pallas_call(
        body, name="convffn_bwd", grid=(N_SLAB,),
        in_specs=[slab(0), slab(N_SLAB), wsl(0), wsl(N_SLAB), bsl(0), bsl(N_SLAB), slab(0)],
        out_specs=[_bs((2, s, LANES), lambda j: (0, 0, j)), _bs((2, 3, LANES), lambda j: (0, 0, j)),
                   _bs((2, 1, LANES), lambda j: (0, 0, j))],
        out_shape=[jax.ShapeDtypeStruct((2, s, D_FF), BF16), jax.ShapeDtypeStruct((2, 3, D_FF), F32),
                   jax.ShapeDtypeStruct((2, 1, D_FF), F32)],
        scratch_shapes=[pltpu.VMEM((s, LANES), F32), pltpu.VMEM((s, LANES), F32)],
        compiler_params=_arb(1),
    )(u, u, conv_w, conv_w, conv_b, conv_b, df)


def _row_tile(rows, limit=512):
    best = rows
    for t in range(8, min(rows, limit) + 1, 8):
        if rows % t == 0:
            best = t
    return best if rows % 8 == 0 else rows


ADAM_C1 = 1.0 - ADAM_B1 ** ADAM_STEP
ADAM_C2 = 1.0 - ADAM_B2 ** ADAM_STEP


def _adamw_math(w, gv, m, v):
    nm = ADAM_B1 * m + (1.0 - ADAM_B1) * gv
    nv = ADAM_B2 * v + (1.0 - ADAM_B2) * (gv * gv)
    m_hat = nm / ADAM_C1
    v_hat = nv / ADAM_C2
    return -ADAM_LR * (m_hat / (jnp.sqrt(v_hat) + ADAM_EPS) + ADAM_WD * w), nm, nv


def _adamw_halves(name, core, w, mine, theirs, m, v):
    half, cols = mine.shape
    tr = _row_tile(half)
    nr = half // tr

    def body(core_ref, w_ref, mine_ref, theirs_ref, m_ref, v_ref, g_ref, d_ref, nm_ref, nv_ref):
        gv = jnp.where(pl.program_id(0) == core_ref[0], mine_ref[...], theirs_ref[...])
        g_ref[...] = gv
        d_ref[...], nm_ref[...], nv_ref[...] = _adamw_math(w_ref[...], gv, m_ref[...], v_ref[...])

    full = pl.BlockSpec((tr, cols), lambda hf, r, cr: (hf * nr + r, 0))
    part = pl.BlockSpec((tr, cols), lambda hf, r, cr: (r, 0))
    return pl.pallas_call(
        body, name=name,
        grid_spec=pltpu.PrefetchScalarGridSpec(num_scalar_prefetch=1, grid=(2, nr),
                                               in_specs=[full, part, part, full, full], out_specs=[full] * 4),
        out_shape=[jax.ShapeDtypeStruct((2 * half, cols), F32)] * 4, compiler_params=_arb(2),
    )(core, w, mine, theirs, m, v)


ANY = pl.BlockSpec(memory_space=pl.ANY)


def _mesh_pos():
    return lax.axis_index("x"), lax.axis_index("y"), lax.axis_index("c")


def _other_chips(x, y):
    return [(1 - x, y), (x, 1 - y), (1 - x, 1 - y)]


def _allgather_weights(shards, split):
    n = len(shards)

    def body(*refs):
        w_refs, o_refs = refs[:n], refs[n:2 * n]
        send_sems, recv_sems, fsend_sems, frecv_sems = refs[2 * n:]
        x, y, c = _mesh_pos()
        p = 2 * x + y
        chips = _other_chips(x, y)

        def piece(i, chip_index, core):
            return o_refs[i].at[chip_index, core] if split[i] else o_refs[i].at[chip_index]

        def remote(src, dst, ssem, rsem, to):
            return pltpu.make_async_remote_copy(src_ref=src, dst_ref=dst, send_sem=ssem, recv_sem=rsem,
                                                device_id=to, device_id_type=MESH)

        sends = []
        for i in range(n):
            src = w_refs[i].at[c] if split[i] else w_refs[i]
            for k, chip in enumerate(chips):
                cp = remote(src, piece(i, p, c), send_sems.at[3 * i + k], recv_sems.at[3 * i + k], (*chip, c))
                cp.start()
                sends.append(cp)
        for i in range(n):
            for k, chip in enumerate(chips):
                pk = 2 * chip[0] + chip[1]
                landed = piece(i, pk, c)
                remote(landed, landed, send_sems.at[3 * i + k], recv_sems.at[3 * i + k], (*chip, c)).wait_recv()
                if split[i]:
                    fw = remote(landed, landed, fsend_sems.at[3 * i + k], frecv_sems.at[3 * i + k], (x, y, 1 - c))
                    fw.start()
                    sends.append(fw)
        for i in range(n):
            if split[i]:
                for k, chip in enumerate(chips):
                    pk = 2 * chip[0] + chip[1]
                    theirs = piece(i, pk, 1 - c)
                    remote(theirs, theirs, fsend_sems.at[3 * i + k], frecv_sems.at[3 * i + k],
                           (x, y, 1 - c)).wait_recv()
        for cp in sends:
            cp.wait_send()

    return pl.pallas_call(
        body, name="allgather_weights",
        in_specs=[ANY] * n, out_specs=[ANY] * n,
        out_shape=[jax.ShapeDtypeStruct((4,) + w.shape, w.dtype) for w in shards],
        scratch_shapes=[pltpu.SemaphoreType.DMA((3 * n,)), pltpu.SemaphoreType.DMA((3 * n,)),
                        pltpu.SemaphoreType.DMA((3 * n,)), pltpu.SemaphoreType.DMA((3 * n,))],
    )(*shards)


def _rs_pair_exchange(name, grads):
    n = len(grads)

    def body(*refs):
        g_refs, o_refs = refs[:n], refs[n:2 * n]
        send_sems, recv_sems = refs[2 * n:]
        x, y, c = _mesh_pos()
        cps = []
        for i in range(n):
            cp = pltpu.make_async_remote_copy(
                src_ref=g_refs[i].at[:, 1 - c], dst_ref=o_refs[i],
                send_sem=send_sems.at[i], recv_sem=recv_sems.at[i], device_id=(x, y, 1 - c), device_id_type=MESH)
            cp.start()
            cps.append(cp)
        for cp in cps:
            cp.wait()

    return pl.pallas_call(
        body, name=name, in_specs=[ANY] * n, out_specs=[ANY] * n,
        out_shape=[jax.ShapeDtypeStruct((4,) + g.shape[2:], F32) for g in grads],
        scratch_shapes=[pltpu.SemaphoreType.DMA((n,)), pltpu.SemaphoreType.DMA((n,))],
    )(*grads)


def _rs_pair_add(name, core, g, recv):
    _, half, cols = recv.shape
    tr = _row_tile(half)
    nr = half // tr

    def body(core_ref, g_ref, r_ref, o_ref):
        o_ref[...] = (g_ref[...] + r_ref[...]).astype(BF16)

    return pl.pallas_call(
        body, name=name,
        grid_spec=pltpu.PrefetchScalarGridSpec(
            num_scalar_prefetch=1, grid=(4, nr),
            in_specs=[pl.BlockSpec((None, None, tr, cols), lambda q, r, cr: (q, cr[0], r, 0)),
                      pl.BlockSpec((None, tr, cols), lambda q, r, cr: (q, r, 0))],
            out_specs=pl.BlockSpec((None, tr, cols), lambda q, r, cr: (q, r, 0))),
        out_shape=jax.ShapeDtypeStruct((4, half, cols), BF16),
        compiler_params=_arb(2),
    )(core, g, recv)


def _rs_final_add(name, chip, pair, recv):
    _, half, cols = pair.shape
    tr = _row_tile(half)

    def body(chip_ref, p_ref, r_ref, o_ref):
        o_ref[...] = ((p_ref[...].astype(F32) + r_ref[0].astype(F32)) + r_ref[1].astype(F32)) + r_ref[2].astype(F32)

    return pl.pallas_call(
        body, name=name,
        grid_spec=pltpu.PrefetchScalarGridSpec(
            num_scalar_prefetch=1, grid=(half // tr,),
            in_specs=[pl.BlockSpec((None, tr, cols), lambda r, ch: (ch[0], r, 0)),
                      pl.BlockSpec((3, tr, cols), lambda r, ch: (0, r, 0))],
            out_specs=pl.BlockSpec((tr, cols), lambda r, ch: (r, 0))),
        out_shape=jax.ShapeDtypeStruct((half, cols), F32),
        compiler_params=_arb(1),
    )(chip, pair, recv)


def _rs_pair_share(halves):
    n = len(halves)

    def body(*refs):
        h_refs, o_refs = refs[:n], refs[n:2 * n]
        send_sems, recv_sems = refs[2 * n:]
        x, y, c = _mesh_pos()
        cps = []
        for i in range(n):
            cp = pltpu.make_async_remote_copy(src_ref=h_refs[i], dst_ref=o_refs[i], send_sem=send_sems.at[i],
                                              recv_sem=recv_sems.at[i], device_id=(x, y, 1 - c), device_id_type=MESH)
            cp.start()
            cps.append(cp)
        for cp in cps:
            cp.wait()

    return pl.pallas_call(
        body, name="rs_pair_share", in_specs=[ANY] * n, out_specs=[ANY] * n,
        out_shape=[jax.ShapeDtypeStruct(h.shape, F32) for h in halves],
        scratch_shapes=[pltpu.SemaphoreType.DMA((n,)), pltpu.SemaphoreType.DMA((n,))],
    )(*halves)


HBM = pl.BlockSpec(memory_space=pltpu.HBM)
SEM = pl.BlockSpec(memory_space=pltpu.SEMAPHORE)


class _SplitExchange:
    def __init__(self, name, srcs, land_shapes, src_of, dst_of, arrive_of, to_sibling=False):
        self.name, self.srcs, self.land_shapes = name, list(srcs), list(land_shapes)
        self.src_of, self.dst_of, self.arrive_of = src_of, dst_of, arrive_of
        self.to_sibling = to_sibling
        self.fan = 1 if to_sibling else 3

    def _copies(self, src_refs, land_refs, send_sems, recv_sems):
        x, y, c = _mesh_pos()
        p = 2 * x + y
        if self.to_sibling:
            peers = [((x, y, 1 - c), 1 - c)]
        else:
            peers = [((*chip, c), 2 * chip[0] + chip[1]) for chip in _other_chips(x, y)]
        out = []
        for i, (src, land) in enumerate(zip(src_refs, land_refs)):
            for k, (peer, pk) in enumerate(peers):
                sems = dict(send_sem=send_sems.at[self.fan * i + k], recv_sem=recv_sems.at[self.fan * i + k],
                            device_id=peer, device_id_type=MESH)
                sent = pltpu.make_async_remote_copy(src_ref=self.src_of(src, k, p, pk),
                                                    dst_ref=self.dst_of(land, k, p, pk), **sems)
                here = self.arrive_of(land, k, p, pk)
                out.append((sent, pltpu.make_async_remote_copy(src_ref=here, dst_ref=here, **sems)))
        return out

    def start(self):
        n = len(self.srcs)

        def body(*refs):
            for sent, _ in self._copies(refs[:n], refs[n:2 * n], refs[2 * n], refs[2 * n + 1]):
                sent.start()
            refs[-1][...] = jnp.zeros((8, LANES), F32)

        lands = [lax.empty(shape, src.dtype) for shape, src in zip(self.land_shapes, self.srcs)]
        operands = [pltpu.with_memory_space_constraint(a, pltpu.HBM) for a in self.srcs + lands]
        outs = pl.pallas_call(
            body, name=self.name + "_start",
            out_shape=(pltpu.SemaphoreType.DMA((self.fan * n,)), pltpu.SemaphoreType.DMA((self.fan * n,)),
                       *[pltpu.HBM(a.shape, a.dtype) for a in operands], jax.ShapeDtypeStruct((8, LANES), F32)),
            in_specs=[HBM] * (2 * n), out_specs=(SEM, SEM, *[HBM] * (2 * n), pl.BlockSpec(memory_space=pltpu.VMEM)),
            input_output_aliases={j: 2 + j for j in range(2 * n)},
            compiler_params=pltpu.CompilerParams(has_side_effects=pltpu.SideEffectType.DATAFLOW_SIDE_EFFECTING),
        )(*operands)
        self._sems, self._thru = outs[:2], list(outs[2:2 + 2 * n])
        return outs[-1]

    def wait(self, after):
        n = len(self.srcs)

        def body(*refs):
            for sent, arrived in self._copies(refs[:n], refs[n:2 * n], refs[2 * n], refs[2 * n + 1]):
                sent.wait_send()
                arrived.wait_recv()

        outs = pl.pallas_call(
            body, name=self.name + "_wait",
            out_shape=tuple(pltpu.HBM(a.shape, a.dtype) for a in self._thru),
            in_specs=[HBM] * (2 * n) + [SEM, SEM, ANY], out_specs=tuple([HBM] * (2 * n)),
            input_output_aliases={j: j for j in range(2 * n)},
            compiler_params=pltpu.CompilerParams(has_side_effects=pltpu.SideEffectType.DATAFLOW_SIDE_EFFECTING),
        )(*self._thru, *self._sems, after)
        return list(outs[:n]), list(outs[n:])


def _small_allreduce(parts):
    n = len(parts)

    def body(*refs):
        in_refs, out_refs, gather_refs = refs[:n], refs[n:2 * n], refs[2 * n:3 * n]
        send_sems, recv_sems = refs[3 * n:]
        x, y, c = _mesh_pos()
        me = 4 * x + 2 * y + c
        cps = []
        for i in range(n):
            gather_refs[i][me] = in_refs[i][...]
            for j in range(1, 8):
                peer = (x ^ (j >> 2), y ^ ((j >> 1) & 1), c ^ (j & 1))
                cp = pltpu.make_async_remote_copy(
                    src_ref=in_refs[i], dst_ref=gather_refs[i].at[me], send_sem=send_sems.at[7 * i + j - 1],
                    recv_sem=recv_sems.at[7 * i + j - 1], device_id=peer, device_id_type=MESH)
                cp.start()
                cps.append(cp)
        for i in range(n):
            for j in range(1, 8):
                peer_id = 4 * (x ^ (j >> 2)) + 2 * (y ^ ((j >> 1) & 1)) + (c ^ (j & 1))
                slot = gather_refs[i].at[peer_id]
                pltpu.make_async_remote_copy(src_ref=slot, dst_ref=slot, send_sem=send_sems.at[7 * i + j - 1],
                                             recv_sem=recv_sems.at[7 * i + j - 1], device_id=(x, y, c),
                                             device_id_type=MESH).wait_recv()
        for cp in cps:
            cp.wait_send()
        for i in range(n):
            tot = gather_refs[i][0]
            for d in range(1, 8):
                tot = tot + gather_refs[i][d]
            out_refs[i][...] = tot

    vmem = pl.BlockSpec(memory_space=pltpu.VMEM)
    return pl.pallas_call(
        body, name="small_allreduce", in_specs=[vmem] * n, out_specs=[vmem] * n,
        out_shape=[jax.ShapeDtypeStruct(p.shape, F32) for p in parts],
        scratch_shapes=[pltpu.VMEM((8,) + p.shape, F32) for p in parts]
        + [pltpu.SemaphoreType.DMA((7 * n,)), pltpu.SemaphoreType.DMA((7 * n,))],
    )(*parts)


def _adamw_small(ws, gs, ms, vs):
    n = len(ws)

    def body(*refs):
        for i in range(n):
            w_ref, g_ref, m_ref, v_ref = refs[i], refs[n + i], refs[2 * n + i], refs[3 * n + i]
            d_ref, nm_ref, nv_ref = refs[4 * n + i], refs[5 * n + i], refs[6 * n + i]
            d_ref[...], nm_ref[...], nv_ref[...] = _adamw_math(w_ref[...], g_ref[...], m_ref[...], v_ref[...])

    vmem = pl.BlockSpec(memory_space=pltpu.VMEM)
    shapes = [jax.ShapeDtypeStruct(w.shape, F32) for w in ws]
    outs = pl.pallas_call(body, name="adamw_small", in_specs=[vmem] * (4 * n), out_specs=[vmem] * (3 * n),
                          out_shape=shapes * 3)(*ws, *gs, *ms, *vs)
    return outs[:n], outs[n:2 * n], outs[2 * n:]


W_IN_PIECES = ((0, 256, P_QLAT), (256, 384, P_CKV), (384, 448, P_KR), (448, 1472, P_QB), (1472, 1728, P_KB),
               (1728, 1984, P_VB), (1984, 3008, P_GA), (3008, 4032, P_GB))
W_IN_SHARD = 1008


def _w_in_from_shards(shards):
    cols = []
    for lo, hi, _ in sorted(W_IN_PIECES, key=lambda piece: piece[2]):
        for q in range(4):
            a, b = max(lo, q * W_IN_SHARD), min(hi, (q + 1) * W_IN_SHARD)
            if a < b:
                cols.append(shards[q][:, a - q * W_IN_SHARD:b - q * W_IN_SHARD])
    cols.append(jnp.zeros((shards.shape[1], W_IN_PAD - 4 * W_IN_SHARD), shards.dtype))
    return jnp.concatenate(cols, axis=1)


def _w_in_to_shards(p):
    shards = []
    for q in range(4):
        cols = []
        for lo, hi, at in W_IN_PIECES:
            a, b = max(lo, q * W_IN_SHARD), min(hi, (q + 1) * W_IN_SHARD)
            if a < b:
                cols.append(p[:, at + a - lo:at + b - lo])
        shards.append(jnp.concatenate(cols, axis=1))
    return jnp.stack(shards)


def _col_shards(w):
    r, c4 = w.shape
    return w.reshape(r, 4, c4 // 4).transpose(1, 0, 2)


def _local_step(x, positions, target, norm1_g, w_in_p, q_a_norm_g, wq, kv_a_norm_g, wkv, rel_bias, sinks,
                late_weights, norm2_g, conv_w, conv_b, final_norm_g, early_grads=None, last_grads=None):
    s = x.shape[0]
    half = QK_ROPE // 2
    inv_freq = jnp.asarray(np.float32(ROPE_THETA) ** (-np.arange(half, dtype=np.float32) / np.float32(half)))
    ang = positions.astype(F32)[:, None] * inv_freq[None, :]
    cos, sin = jnp.cos(ang), jnp.sin(ang)
    z64 = jnp.zeros((s, 64), F32)
    cos_t = jnp.concatenate([cos, cos, z64], axis=1)
    sin_t = jnp.concatenate([-sin, sin, z64], axis=1)
    bucket = _t5_bucket_table()
    sinks1 = sinks.reshape(H_B)

    h1, rstd1 = _rmsnorm_fwd("norm1_fwd", x, norm1_g, D_MODEL, 0)
    proj, proj_b = _matmul("proj", h1, w_in_p, out_shape=(s, W_IN_PAD), out_dtype=F32, grid=(s // MM_ROWS, W_IN_PAD // 1024, 1),
                           a_spec=_bs((MM_ROWS, D_MODEL), lambda i, j, k: (i, 0)), b_spec=_bs((D_MODEL, 1024), lambda i, j, k: (0, j)),
                           o_spec=_bs((MM_ROWS, 1024), lambda i, j, k: (i, j)), contract=NN, bf16_copy=True)
    qn, cn, rstd_q, rstd_c = _lat_norms(proj, q_a_norm_g, kv_a_norm_g)
    q = _q_heads(qn, wq, cos_t, sin_t)
    k, v = _kv_heads(cn, wkv, proj, cos_t, sin_t)
    o_a, lse_a = _mla_fwd(q, k, v)

    bias = _win_bias(bucket, rel_bias)
    o_b, lse_b = _win_fwd(proj_b, bias, sinks1)

    mixed = _gate_fwd(proj, o_a, o_b)
    w_out, w_up, w_down = late_weights(mixed)
    row512 = lambda w: _bs((MM_ROWS, w), lambda i, j, k: (i, 0))
    whole = lambda r, c: _bs((r, c), lambda i, j, k: (0, 0))
    x1 = _matmul("attn_out", mixed, w_out, out_shape=(s, D_MODEL), out_dtype=F32, grid=(s // MM_ROWS, 1, 1),
                 a_spec=row512(D_MODEL), b_spec=whole(D_MODEL, D_MODEL), o_spec=row512(D_MODEL), contract=NN, add=x)
    h2, rstd2 = _rmsnorm_fwd("norm2_fwd", x1, norm2_g, D_MODEL, 0)
    u = _matmul("ffn_up", h2, w_up, out_shape=(s, 2 * D_FF), out_dtype=F32, grid=(s // MM_ROWS, 4, 1),
                a_spec=_bs((MM_ROWS, D_MODEL), lambda i, j, k: (i, 0)), b_spec=_bs((D_MODEL, D_FF // 2), lambda i, j, k: (0, j)),
                o_spec=_bs((MM_ROWS, D_FF // 2), lambda i, j, k: (i, j)), contract=NN)
    f = _convffn_fwd(u, conv_w, conv_b)
    x2 = _matmul("ffn_down", f, w_down, out_shape=(s, D_MODEL), out_dtype=F32, grid=(s // MM_ROWS, 1, 1),
                 a_spec=row512(D_FF), b_spec=whole(D_FF, D_MODEL), o_spec=row512(D_MODEL), contract=NN, add=x1)
    loss, dx2, d_final_g, dx2_b = _final_loss(x2, target, final_norm_g.reshape(1, D_MODEL))
    tk = min(s, DW_ROWS)

    df = _matmul("ffn_down_dx", dx2_b, w_down, out_shape=(s, D_FF), out_dtype=F32, grid=(s // MM_ROWS, 2, 1),
                 a_spec=row512(D_MODEL), b_spec=_bs((D_FF // 2, D_MODEL), lambda i, j, k: (j, 0)),
                 o_spec=_bs((MM_ROWS, D_FF // 2), lambda i, j, k: (i, j)), contract=NT)
    d_w_down = _matmul("ffn_down_dw", f, dx2_b, out_shape=(D_FF, D_MODEL), out_dtype=F32, grid=(2, 1, s // tk),
                       a_spec=_bs((tk, D_FF // 2), lambda i, j, k: (k, i)), b_spec=_bs((tk, D_MODEL), lambda i, j, k: (k, 0)),
                       o_spec=_bs((D_FF // 2, D_MODEL), lambda i, j, k: (i, 0)), contract=TN)
    du, d_conv_w2, d_conv_b2 = _convffn_bwd(u, conv_w, conv_b, df)
    kc = D_FF // 2
    dh2 = _matmul("ffn_up_dx", du, w_up, out_shape=(s, D_MODEL), out_dtype=F32, grid=(s // 1024, 1, 4),
                  a_spec=_bs((None, 1024, kc), lambda i, j, k: (k // 2, i, k % 2)),
                  b_spec=_bs((D_MODEL, kc), lambda i, j, k: (0, k)),
                  o_spec=_bs((1024, D_MODEL), lambda i, j, k: (i, 0)), contract=NT)
    d_w_up = _matmul("ffn_up_dw", h2, du, out_shape=(D_MODEL, 2 * D_FF), out_dtype=F32, grid=(1, 4, s // tk),
                     a_spec=_bs((tk, D_MODEL), lambda i, j, k: (k, 0)),
                     b_spec=_bs((None, tk, kc), lambda i, j, k: (j // 2, k, j % 2)),
                     o_spec=_bs((D_MODEL, kc), lambda i, j, k: (0, j)), contract=TN)
    dx1, d_norm2_g, dx1_b = _rmsnorm_bwd("norm2_bwd", dh2, x1, rstd2, norm2_g, D_MODEL, 0, F32, res=dx2, bf16_copy=True)

    d_w_out = _matmul("attn_out_dw", mixed, dx1_b, out_shape=(D_MODEL, D_MODEL), out_dtype=F32, grid=(1, 1, s // tk),
                      a_spec=_bs((tk, D_MODEL), lambda i, j, k: (k, 0)), b_spec=_bs((tk, D_MODEL), lambda i, j, k: (k, 0)),
                      o_spec=whole(D_MODEL, D_MODEL), contract=TN)
    token, early_grads_on = early_grads(d_w_out, d_w_up, d_w_down) if early_grads is not None else (None, None)
    dmixed = _matmul("attn_out_dx", dx1_b, w_out, out_shape=(s, D_MODEL), out_dtype=F32, grid=(s // MM_ROWS, 1, 1),
                     a_spec=row512(D_MODEL), b_spec=whole(D_MODEL, D_MODEL), o_spec=row512(D_MODEL), contract=NT,
                     after=token)
    do_a, do_b, d_ga, d_gb = _gate_bwd(dmixed, proj, o_a, o_b)
    if early_grads_on is not None:
        sinks1 = sinks1 + early_grads_on(d_ga)[0, :H_B]

    d_qb, dk_pad, dv_pad, dbias, dsink_rows = _win_bwd(proj_b, bias, sinks1, do_b, lse_b)
    wp = _win_param_grads(bucket, dbias, dsink_rows)[:, 0, :]
    d_rel_bias = wp[:, :NUM_BUCKETS].T
    d_sinks = wp[:, NUM_BUCKETS].reshape(1, H_B)
    d_kb = dk_pad[WINDOW:WINDOW + s].astype(BF16)
    d_vb = dv_pad[WINDOW:WINDOW + s].astype(BF16)

    dq, dk, dv = _mla_bwd(q, k, v, do_a, o_a, lse_a)
    dq_pre, dkv_pre, d_kr = _mla_bwd_prep(dq, dk, dv, cos_t, sin_t)
    th = min(s, HEAD_ROWS)
    hgrid = (s // th, 1, H_A)
    hblock = _bs((None, th, HEAD_PAD), lambda i, j, k: (k, i, 0))
    hrows = lambda w: _bs((th, w), lambda i, j, k: (i, 0))
    dqn = _matmul("q_up_dx", dq_pre, wq, out_shape=(s, Q_LORA), out_dtype=F32, grid=hgrid, a_spec=hblock,
                  b_spec=_bs((None, Q_LORA, HEAD_PAD), lambda i, j, k: (k, 0, 0)), o_spec=hrows(Q_LORA), contract=NT)
    dcn = _matmul("kv_up_dx", dkv_pre, wkv, out_shape=(s, KV_LORA), out_dtype=F32, grid=hgrid, a_spec=hblock,
                  b_spec=_bs((None, KV_LORA, HEAD_PAD), lambda i, j, k: (k, 0, 0)), o_spec=hrows(KV_LORA), contract=NT)
    wgrid = (H_A, 1, s // th)
    d_wq = _matmul("q_up_dw", qn, dq_pre, out_shape=(H_A, Q_LORA, HEAD_PAD), out_dtype=F32, grid=wgrid,
                   a_spec=_bs((th, Q_LORA), lambda i, j, k: (k, 0)), b_spec=_bs((None, th, HEAD_PAD), lambda i, j, k: (i, k, 0)),
                   o_spec=_bs((None, Q_LORA, HEAD_PAD), lambda i, j, k: (i, 0, 0)), contract=TN)
    d_wkv = _matmul("kv_up_dw", cn, dkv_pre, out_shape=(H_A, KV_LORA, HEAD_PAD), out_dtype=F32, grid=wgrid,
                    a_spec=_bs((th, KV_LORA), lambda i, j, k: (k, 0)), b_spec=_bs((None, th, HEAD_PAD), lambda i, j, k: (i, k, 0)),
                    o_spec=_bs((None, KV_LORA, HEAD_PAD), lambda i, j, k: (i, 0, 0)), contract=TN)
    d_qlat, d_gq = _rmsnorm_bwd("q_norm_bwd", dqn, proj, rstd_q, q_a_norm_g, Q_LORA, P_QLAT // Q_LORA, BF16)
    d_ckv, d_gkv = _rmsnorm_bwd("kv_norm_bwd", dcn, proj, rstd_c, kv_a_norm_g, KV_LORA, P_CKV // KV_LORA, BF16)

    dproj = jnp.concatenate([d_qb, d_ga, d_gb, d_qlat, d_kb, d_vb, d_ckv, d_kr], axis=1)
    d_w_in_p = _matmul("proj_dw", h1, dproj, out_shape=(D_MODEL, W_IN_PAD), out_dtype=F32, grid=(1, W_IN_PAD // 1024, s // tk),
                       a_spec=_bs((tk, D_MODEL), lambda i, j, k: (k, 0)), b_spec=_bs((tk, 1024), lambda i, j, k: (k, j)),
                       o_spec=_bs((D_MODEL, 1024), lambda i, j, k: (0, j)), contract=TN)
    token = last_grads(d_w_in_p, d_wq, d_wkv) if last_grads is not None else None
    dh1 = _matmul("proj_dx", dproj, w_in_p, out_shape=(s, D_MODEL), out_dtype=F32, grid=(s // 1024, 1, W_IN_PAD // 1024),
                  a_spec=_bs((1024, 1024), lambda i, j, k: (i, k)), b_spec=_bs((D_MODEL, 1024), lambda i, j, k: (0, k)),
                  o_spec=_bs((1024, D_MODEL), lambda i, j, k: (i, 0)), contract=NT, after=token)
    dx, d_norm1_g = _rmsnorm_bwd("norm1_bwd", dh1, x, rstd1, norm1_g, D_MODEL, 0, F32, res=dx1)

    grads = dict(
        norm1_g=d_norm1_g, w_in_p=d_w_in_p, q_a_norm_g=d_gq, wq=d_wq, kv_a_norm_g=d_gkv, wkv=d_wkv,
        rel_bias=d_rel_bias, sinks=d_sinks, w_out=d_w_out, norm2_g=d_norm2_g, w_up=d_w_up,
        conv_w=jnp.concatenate([d_conv_w2[0], d_conv_w2[1]], axis=1),
        conv_b=jnp.concatenate([d_conv_b2[0], d_conv_b2[1]], axis=1),
        w_down=d_w_down, final_norm_g=d_final_g.reshape(D_MODEL))
    return loss, dx, grads


def _wq_heads(w_q_b):
    w = w_q_b.reshape(Q_LORA, H_A, QK_NOPE + QK_ROPE).transpose(1, 0, 2)
    return jnp.pad(w, ((0, 0), (0, 0), (0, HEAD_PAD - QK_NOPE - QK_ROPE)))


def _wq_unheads(d_wq):
    return d_wq[:, :, :QK_NOPE + QK_ROPE].transpose(1, 0, 2).reshape(Q_LORA, H_A * (QK_NOPE + QK_ROPE))


def _wkv_heads(w_kv_b):
    return w_kv_b.reshape(KV_LORA, H_A, QK_NOPE + V_DIM).transpose(1, 0, 2)


def _wkv_unheads(d_wkv):
    return d_wkv.transpose(1, 0, 2).reshape(KV_LORA, H_A * (QK_NOPE + V_DIM))


SMALL = ("norm1_g", "q_a_norm_g", "kv_a_norm_g", "rel_bias", "sinks", "norm2_g", "conv_b", "final_norm_g")
FIRST = ("w_in", "w_q_b", "w_kv_b")
LATER = ("w_out", "w_up", "w_down")
BIG = FIRST + LATER


def kernel(x, positions, norm1_g, w_in, q_a_norm_g, w_q_b, kv_a_norm_g, w_kv_b, rel_bias, sinks, w_out, norm2_g, w_up, conv_w, conv_b, w_down, final_norm_g, loss_target, m_norm1_g, m_w_in, m_q_a_norm_g, m_w_q_b, m_kv_a_norm_g, m_w_kv_b, m_rel_bias, m_sinks, m_w_out, m_norm2_g, m_w_up, m_conv_w, m_conv_b, m_w_down, m_final_norm_g, v_norm1_g, v_w_in, v_q_a_norm_g, v_w_q_b, v_kv_a_norm_g, v_w_kv_b, v_rel_bias, v_sinks, v_w_out, v_norm2_g, v_w_up, v_conv_w, v_conv_b, v_w_down, v_final_norm_g):
    weights = dict(norm1_g=norm1_g, w_in=w_in, q_a_norm_g=q_a_norm_g, w_q_b=w_q_b, kv_a_norm_g=kv_a_norm_g,
                   w_kv_b=w_kv_b, rel_bias=rel_bias, sinks=sinks, w_out=w_out, norm2_g=norm2_g, w_up=w_up,
                   conv_w=conv_w, conv_b=conv_b, w_down=w_down, final_norm_g=final_norm_g)
    mom_m = dict(norm1_g=m_norm1_g, w_in=m_w_in, q_a_norm_g=m_q_a_norm_g, w_q_b=m_w_q_b, kv_a_norm_g=m_kv_a_norm_g,
                 w_kv_b=m_w_kv_b, rel_bias=m_rel_bias, sinks=m_sinks, w_out=m_w_out, norm2_g=m_norm2_g, w_up=m_w_up,
                 conv_w=m_conv_w, conv_b=m_conv_b, w_down=m_w_down, final_norm_g=m_final_norm_g)
    mom_v = dict(norm1_g=v_norm1_g, w_in=v_w_in, q_a_norm_g=v_q_a_norm_g, w_q_b=v_w_q_b, kv_a_norm_g=v_kv_a_norm_g,
                 w_kv_b=v_w_kv_b, rel_bias=v_rel_bias, sinks=v_sinks, w_out=v_w_out, norm2_g=v_norm2_g, w_up=v_w_up,
                 conv_w=v_conv_w, conv_b=v_conv_b, w_down=v_w_down, final_norm_g=v_final_norm_g)
    shard2d = {n: weights[n][0] for n in BIG}
    conv_w_shard = conv_w[0]
    xi, yi, ci = lax.axis_index("x"), lax.axis_index("y"), lax.axis_index("c")
    chip = (2 * xi + yi).astype(jnp.int32)

    core = ci.astype(jnp.int32).reshape(1)
    chip1 = chip.reshape(1)
    cat_cols = lambda a: jnp.concatenate([a[0], a[1], a[2], a[3]], axis=1)
    own_slot = lambda a, own: lax.dynamic_update_index_in_dim(a, own, chip, 0)
    halved = lambda a: a.reshape((2, a.shape[0] // 2) + a.shape[1:])
    quartered = lambda a: a.reshape(4, 2, a.shape[1] // 2, a.shape[2])

    send = [halved(shard2d[n].astype(BF16)) for n in FIRST] + [conv_w_shard]
    gathered = [own_slot(a, own) for a, own in zip(_allgather_weights(send, split=[True] * len(FIRST) + [False]), send)]
    g = {n: a.reshape((4,) + shard2d[n].shape) for n, a in zip(FIRST, gathered)}
    w_in_p = _w_in_from_shards(g["w_in"])
    wq = _wq_heads(cat_cols(g["w_q_b"]))
    wkv = _wkv_heads(cat_cols(g["w_kv_b"]))
    conv_w_f = cat_cols(gathered[-1])

    later = [shard2d[n].astype(BF16) for n in LATER]
    gather2 = _SplitExchange("gather_later", later, [(4,) + a.shape for a in later],
                             src_of=lambda ref, k, p, pk: ref, dst_of=lambda ref, k, p, pk: ref.at[p],
                             arrive_of=lambda ref, k, p, pk: ref.at[pk])
    norm1_g_in = norm1_g + gather2.start()[:1, :1]

    def late_weights(after):
        w_out_g, w_up_g, w_down_g = [own_slot(a, own) for own, a in zip(*gather2.wait(after))]
        return w_out_g.reshape(D_MODEL, D_MODEL), cat_cols(w_up_g), w_down_g.reshape(D_FF, D_MODEL)

    early = {}

    def early_grads(d_w_out, d_w_up, d_w_down):
        grads = [quartered(d_w_out.reshape(4, D_MODEL // 4, D_MODEL)), quartered(_col_shards(d_w_up)),
                 quartered(d_w_down.reshape(4, D_FF // 4, D_MODEL))]
        swap = _SplitExchange("rs_pair_exchange_early", grads, [(4,) + a.shape[2:] for a in grads], to_sibling=True,
                              src_of=lambda ref, k, p, pk: ref.at[:, pk], dst_of=lambda ref, k, p, pk: ref,
                              arrive_of=lambda ref, k, p, pk: ref)

        def on(after):
            kept, recv = swap.wait(after)
            early["pairs"] = [_rs_pair_add(f"rs_pair_add_{n}", core, gfull, r) for n, gfull, r in zip(LATER, kept, recv)]
            early["ici"] = _SplitExchange("rs_ici_early", early["pairs"], [(3,) + a.shape[1:] for a in early["pairs"]],
                                          src_of=lambda ref, k, p, pk: ref.at[pk], dst_of=lambda ref, k, p, pk: ref.at[k],
                                          arrive_of=lambda ref, k, p, pk: ref.at[k])
            return early["ici"].start()

        return swap.start(), on

    last = {}

    def last_grads(d_w_in_p, d_wq, d_wkv):
        grads = [quartered(_w_in_to_shards(d_w_in_p)), quartered(_col_shards(_wq_unheads(d_wq))),
                 quartered(_col_shards(_wkv_unheads(d_wkv)))]
        recv = _rs_pair_exchange("rs_pair_exchange_last", grads)
        last["pairs"] = [_rs_pair_add(f"rs_pair_add_{n}", core, gfull, r) for n, gfull, r in zip(FIRST, grads, recv)]
        last["ici"] = _SplitExchange("rs_ici_last", last["pairs"], [(3,) + a.shape[1:] for a in last["pairs"]],
                                     src_of=lambda ref, k, p, pk: ref.at[pk], dst_of=lambda ref, k, p, pk: ref.at[k],
                                     arrive_of=lambda ref, k, p, pk: ref.at[k])
        return last["ici"].start()

    loss, dx, gr = _local_step(x[0], positions, loss_target[0], norm1_g_in, w_in_p, q_a_norm_g, wq, kv_a_norm_g, wkv,
                               rel_bias, sinks, late_weights, norm2_g, conv_w_f, conv_b, final_norm_g, early_grads,
                               last_grads)

    last_pairs, last_recv = last["ici"].wait(dx)
    early_pairs, early_recv = early["ici"].wait(dx)
    pairs, recv2 = last_pairs + early_pairs, last_recv + early_recv
    halves = [_rs_final_add(f"rs_final_add_{n}", chip1, pr, r) for n, pr, r in zip(FIRST + LATER, pairs, recv2)]
    sibling_halves = _rs_pair_share(halves)

    as_rows = lambda a: a.reshape((-1, a.shape[-1]))
    summed = _small_allreduce([as_rows(gr[n]) for n in SMALL] + [gr["conv_w"], loss])
    small_g = dict(zip(SMALL, summed[:len(SMALL)]))
    conv_w_g = lax.dynamic_slice_in_dim(summed[len(SMALL)], chip * (2 * D_FF // 4), 2 * D_FF // 4, axis=1)
    loss_out = summed[-1].reshape(())

    out_g, out_d, out_m, out_v = {}, {}, {}, {}
    for n, mine, theirs in zip(FIRST + LATER, halves, sibling_halves):
        gsh, d, nm, nv = _adamw_halves(f"adamw_{n}", core, shard2d[n], mine, theirs, mom_m[n][0], mom_v[n][0])
        out_g[n], out_d[n], out_m[n], out_v[n] = gsh[None], d[None], nm[None], nv[None]
    names = SMALL + ("conv_w",)
    sg = [small_g[n] for n in SMALL] + [conv_w_g]
    ds, nms, nvs = _adamw_small([as_rows(weights[n]) for n in names], sg, [as_rows(mom_m[n]) for n in names],
                                [as_rows(mom_v[n]) for n in names])
    for n, gg, dd, mm, vv in zip(names, sg, ds, nms, nvs):
        shp = weights[n].shape
        out_g[n], out_d[n], out_m[n], out_v[n] = gg.reshape(shp), dd.reshape(shp), mm.reshape(shp), vv.reshape(shp)

    order = ("norm1_g", "w_in", "q_a_norm_g", "w_q_b", "kv_a_norm_g", "w_kv_b", "rel_bias", "sinks", "w_out",
             "norm2_g", "w_up", "conv_w", "conv_b", "w_down", "final_norm_g")
    return (loss_out, dx[None], *[out_g[n] for n in order], *[out_d[n] for n in order],
            *[out_m[n] for n in order], *[out_v[n] for n in order])
```

```python
import functools
import math

import jax
import jax.numpy as jnp
import numpy as np
from jax import lax
from jax.experimental import pallas as pl
from jax.experimental.pallas import tpu as pltpu

F32 = jnp.float32
BF16 = jnp.bfloat16
MESH = pl.DeviceIdType.MESH

D_MODEL = 1024
EPS = 1e-6
H_A = 8
QK_NOPE = 128
QK_ROPE = 64
V_DIM = 128
Q_LORA = 256
KV_LORA = 128
ROPE_THETA = 10000.0
H_B = 16
KV_B = 4
GROUP = 4
HD_B = 64
WINDOW = 128
Q_BLOCK = 128
NUM_BUCKETS = 32
MAX_DISTANCE = 128
D_FF = 2816
HEAD_PAD = 256

ADAM_LR = 0.001
ADAM_B1 = 0.9
ADAM_B2 = 0.999
ADAM_EPS = 1e-08
ADAM_WD = 0.01
ADAM_STEP = 10

LANES = 128
P_QB, P_GA, P_GB, P_QLAT, P_KB, P_VB, P_CKV, P_KR = 0, 1024, 2048, 3072, 3328, 3584, 3840, 3968
W_IN_PAD = 4096

NT = (((1,), (1,)), ((), ()))
NN = (((1,), (0,)), ((), ()))
TN = (((0,), (0,)), ((), ()))


def _arb(n):
    return pltpu.CompilerParams(dimension_semantics=("arbitrary",) * n)


def _matmul(name, a, b, *, out_shape, out_dtype, grid, a_spec, b_spec, o_spec, contract, add=None, bf16_copy=False,
            after=None):
    nk = grid[2]
    acc_shape = tuple(d for d in o_spec.block_shape if d is not None)
    n_in = 2 + (add is not None) + (after is not None)
    n_out = 2 if bf16_copy else 1

    def body(*refs):
        a_ref, b_ref = refs[:2]
        add_ref = refs[2] if add is not None else None
        o_refs = refs[n_in:n_in + n_out]
        scratch = refs[n_in + n_out:]
        prod = lax.dot_general(a_ref[...].astype(BF16), b_ref[...].astype(BF16), contract,
                               preferred_element_type=F32)

        def finish(val):
            if add_ref is not None:
                val = add_ref[...] + val
            o_refs[0][...] = val.astype(out_dtype)
            if bf16_copy:
                o_refs[1][...] = val.astype(BF16)

        if nk == 1:
            finish(prod)
        else:
            acc_ref = scratch[0]
            k = pl.program_id(2)

            @pl.when(k == 0)
            def _():
                acc_ref[...] = prod

            @pl.when((k > 0) & (k < nk - 1))
            def _():
                acc_ref[...] += prod

            @pl.when(k == nk - 1)
            def _():
                finish(acc_ref[...] + prod)

    in_specs = [a_spec, b_spec]
    args = [a, b]
    if add is not None:
        in_specs.append(o_spec)
        args.append(add)
    if after is not None:
        in_specs.append(pl.BlockSpec(memory_space=pl.ANY))
        args.append(after)
    out_shapes = [jax.ShapeDtypeStruct(out_shape, out_dtype)]
    if bf16_copy:
        out_shapes.append(jax.ShapeDtypeStruct(out_shape, BF16))
    res = pl.pallas_call(
        body, name=name, grid=grid, in_specs=in_specs, out_specs=[o_spec] * n_out, out_shape=out_shapes,
        scratch_shapes=[pltpu.VMEM(acc_shape, F32)] if nk > 1 else [],
        compiler_params=_arb(3),
    )(*args)
    return res if bf16_copy else res[0]


def _bs(block, fn):
    return pl.BlockSpec(block, fn)


def _rmsnorm_fwd(name, src, g, d, cb, ts=512):
    s = src.shape[0]

    def body(x_ref, g_ref, h_ref, r_ref):
        x = x_ref[...]
        r = lax.rsqrt(jnp.mean(x * x, axis=-1, keepdims=True) + EPS)
        h_ref[...] = (x * r * g_ref[...]).astype(BF16)
        r_ref[...] = r

    return pl.pallas_call(
        body, name=name, grid=(s // ts,),
        in_specs=[_bs((ts, d), lambda i: (i, cb)), _bs((1, d), lambda i: (0, 0))],
        out_specs=[_bs((ts, d), lambda i: (i, 0)), _bs((ts, 1), lambda i: (i, 0))],
        out_shape=[jax.ShapeDtypeStruct((s, d), BF16), jax.ShapeDtypeStruct((s, 1), F32)],
        compiler_params=_arb(1),
    )(src, g)


def _rmsnorm_bwd(name, dy, src, rstd, g, d, cb, out_dtype, res=None, bf16_copy=False, ts=512):
    s = src.shape[0]

    def body(*refs):
        dy_ref, x_ref, r_ref, g_ref = refs[:4]
        res_ref = refs[4] if res is not None else None
        dx_ref, dg_ref = refs[n_in:n_in + 2]
        dyv = dy_ref[...]
        r = r_ref[...]
        xhat = x_ref[...] * r
        dyh = dyv * g_ref[...]
        c = jnp.mean(dyh * xhat, axis=-1, keepdims=True)
        dx = r * (dyh - xhat * c)
        if res_ref is not None:
            dx = res_ref[...] + dx
        dx_ref[...] = dx.astype(out_dtype)
        if bf16_copy:
            refs[n_in + 2][...] = dx.astype(BF16)
        part = jnp.sum(dyv * xhat, axis=0, keepdims=True)

        @pl.when(pl.program_id(0) == 0)
        def _():
            dg_ref[...] = part

        @pl.when(pl.program_id(0) > 0)
        def _():
            dg_ref[...] += part

    in_specs = [_bs((ts, d), lambda i: (i, 0)), _bs((ts, d), lambda i: (i, cb)),
                _bs((ts, 1), lambda i: (i, 0)), _bs((1, d), lambda i: (0, 0))]
    args = [dy, src, rstd, g]
    if res is not None:
        in_specs.append(_bs((ts, d), lambda i: (i, 0)))
        args.append(res)
    n_in = len(args)
    out_specs = [_bs((ts, d), lambda i: (i, 0)), _bs((1, d), lambda i: (0, 0))]
    out_shape = [jax.ShapeDtypeStruct((s, d), out_dtype), jax.ShapeDtypeStruct((1, d), F32)]
    if bf16_copy:
        out_specs.append(_bs((ts, d), lambda i: (i, 0)))
        out_shape.append(jax.ShapeDtypeStruct((s, d), BF16))
    return pl.pallas_call(
        body, name=name, grid=(s // ts,), in_specs=in_specs, out_specs=out_specs, out_shape=out_shape,
        compiler_params=_arb(1),
    )(*args)


def _final_loss(x2, target, g, ts=512):
    s, d = x2.shape

    def body(x_ref, t_ref, g_ref, loss_ref, dx_ref, dg_ref, dxb_ref):
        x = x_ref[...]
        r = lax.rsqrt(jnp.mean(x * x, axis=-1, keepdims=True) + EPS)
        xhat = x * r
        gv = g_ref[...]
        err = xhat * gv - t_ref[...]
        lpart = 0.5 * jnp.sum(jnp.mean(err * err, axis=-1, keepdims=True), axis=0, keepdims=True)
        dyv = err * (1.0 / d)
        dyh = dyv * gv
        c = jnp.mean(dyh * xhat, axis=-1, keepdims=True)
        dx = r * (dyh - xhat * c)
        dx_ref[...] = dx
        dxb_ref[...] = dx.astype(BF16)
        gpart = jnp.sum(dyv * xhat, axis=0, keepdims=True)

        @pl.when(pl.program_id(0) == 0)
        def _():
            dg_ref[...] = gpart
            loss_ref[...] = lpart

        @pl.when(pl.program_id(0) > 0)
        def _():
            dg_ref[...] += gpart
            loss_ref[...] += lpart

    return pl.pallas_call(
        body, name="final_loss", grid=(s // ts,),
        in_specs=[_bs((ts, d), lambda i: (i, 0)), _bs((ts, d), lambda i: (i, 0)), _bs((1, d), lambda i: (0, 0))],
        out_specs=[_bs((1, 1), lambda i: (0, 0)), _bs((ts, d), lambda i: (i, 0)), _bs((1, d), lambda i: (0, 0)),
                   _bs((ts, d), lambda i: (i, 0))],
        out_shape=[jax.ShapeDtypeStruct((1, 1), F32), jax.ShapeDtypeStruct((s, d), F32),
                   jax.ShapeDtypeStruct((1, d), F32), jax.ShapeDtypeStruct((s, d), BF16)],
        compiler_params=_arb(1),
    )(x2, target, g)


def _swap_halves(t):
    lane = lax.broadcasted_iota(jnp.int32, t.shape, 1)
    return jnp.where(lane < 32, pltpu.roll(t, 96, 1), pltpu.roll(t, 32, 1))


def _rope_fwd(t, cos_t, sin_t):
    return t * cos_t + _swap_halves(t) * sin_t


def _rope_bwd(dt, cos_t, sin_t):
    return dt * cos_t - _swap_halves(dt) * sin_t


def _lat_norms(proj, gq, gkv, ts=512):
    s = proj.shape[0]

    def body(q_ref, c_ref, gq_ref, gkv_ref, qn_ref, cn_ref, rq_ref, rc_ref):
        q = q_ref[...]
        rq = lax.rsqrt(jnp.mean(q * q, axis=-1, keepdims=True) + EPS)
        qn_ref[...] = (q * rq * gq_ref[...]).astype(BF16)
        rq_ref[...] = rq
        cv = c_ref[...]
        rc = lax.rsqrt(jnp.mean(cv * cv, axis=-1, keepdims=True) + EPS)
        cn_ref[...] = (cv * rc * gkv_ref[...]).astype(BF16)
        rc_ref[...] = rc

    return pl.pallas_call(
        body, name="lat_norms", grid=(s // ts,),
        in_specs=[_bs((ts, Q_LORA), lambda i: (i, P_QLAT // Q_LORA)),
                  _bs((ts, KV_LORA), lambda i: (i, P_CKV // KV_LORA)),
                  _bs((1, Q_LORA), lambda i: (0, 0)), _bs((1, KV_LORA), lambda i: (0, 0))],
        out_specs=[_bs((ts, Q_LORA), lambda i: (i, 0)), _bs((ts, KV_LORA), lambda i: (i, 0)),
                   _bs((ts, 1), lambda i: (i, 0)), _bs((ts, 1), lambda i: (i, 0))],
        out_shape=[jax.ShapeDtypeStruct((s, Q_LORA), BF16), jax.ShapeDtypeStruct((s, KV_LORA), BF16),
                   jax.ShapeDtypeStruct((s, 1), F32), jax.ShapeDtypeStruct((s, 1), F32)],
        compiler_params=_arb(1),
    )(proj, proj, gq, gkv)


HEAD_ROWS = 2048
DW_ROWS = 2048
MM_ROWS = 1024


def _q_heads(qn, wq, cos_t, sin_t):
    s = qn.shape[0]
    ts = min(s, HEAD_ROWS)

    def body(qn_ref, w_ref, cos_ref, sin_ref, q_ref):
        o = jnp.dot(qn_ref[...], w_ref[...], preferred_element_type=F32)
        q_ref[:, :LANES] = o[:, :LANES].astype(BF16)
        q_ref[:, LANES:] = _rope_fwd(o[:, LANES:], cos_ref[...], sin_ref[...]).astype(BF16)

    return pl.pallas_call(
        body, name="q_heads", grid=(H_A, s // ts),
        in_specs=[_bs((ts, Q_LORA), lambda h, i: (i, 0)), _bs((None, Q_LORA, HEAD_PAD), lambda h, i: (h, 0, 0)),
                  _bs((ts, LANES), lambda h, i: (i, 0)), _bs((ts, LANES), lambda h, i: (i, 0))],
        out_specs=_bs((None, ts, HEAD_PAD), lambda h, i: (h, i, 0)),
        out_shape=jax.ShapeDtypeStruct((H_A, s, HEAD_PAD), BF16),
        compiler_params=_arb(2),
    )(qn, wq, cos_t, sin_t)


def _kv_heads(cn, wkv, proj, cos_t, sin_t):
    s = cn.shape[0]
    ts = min(s, HEAD_ROWS)

    def body(cn_ref, w_ref, kr_ref, cos_ref, sin_ref, k_ref, v_ref):
        o = jnp.dot(cn_ref[...], w_ref[...], preferred_element_type=F32)
        k_ref[:, :LANES] = o[:, :LANES].astype(BF16)
        k_ref[:, LANES:] = _rope_fwd(kr_ref[...], cos_ref[...], sin_ref[...]).astype(BF16)
        v_ref[...] = o[:, LANES:].astype(BF16)

    return pl.pallas_call(
        body, name="kv_heads", grid=(H_A, s // ts),
        in_specs=[_bs((ts, KV_LORA), lambda h, i: (i, 0)),
                  _bs((None, KV_LORA, QK_NOPE + V_DIM), lambda h, i: (h, 0, 0)),
                  _bs((ts, LANES), lambda h, i: (i, P_KR // LANES)),
                  _bs((ts, LANES), lambda h, i: (i, 0)), _bs((ts, LANES), lambda h, i: (i, 0))],
        out_specs=[_bs((None, ts, HEAD_PAD), lambda h, i: (h, i, 0)), _bs((None, ts, V_DIM), lambda h, i: (h, i, 0))],
        out_shape=[jax.ShapeDtypeStruct((H_A, s, HEAD_PAD), BF16), jax.ShapeDtypeStruct((H_A, s, V_DIM), BF16)],
        compiler_params=_arb(2),
    )(cn, wkv, proj, cos_t, sin_t)


MLA_SCALE = 1.0 / math.sqrt(QK_NOPE + QK_ROPE)
LOG2E = math.log2(math.e)
MLA_EXP2_SCALE = MLA_SCALE * LOG2E


def _lane_tiles(a):
    return [a[:, j * LANES:(j + 1) * LANES] for j in range(a.shape[1] // LANES)]


MLA_SUB = 512


def _mla_fwd(q, k, v, tq=512, tk=1024):
    s = q.shape[1]
    tq = min(tq, s)
    nk = s // tk

    def body(q_ref, k_ref, v_ref, o_ref, lse_ref, m_ref, l_ref, acc_ref):
        m_ref[...] = jnp.full(m_ref.shape, -jnp.inf, F32)
        l_ref[...] = jnp.zeros(l_ref.shape, F32)
        acc_ref[...] = jnp.zeros(acc_ref.shape, F32)

        def step(c, carry):
            rows = pl.ds(pl.multiple_of(c * tk, tk), tk)
            for sub in range(tq // MLA_SUB):
                qr = slice(sub * MLA_SUB, (sub + 1) * MLA_SUB)
                raw = lax.dot_general(q_ref[qr, :], k_ref[rows, :], NT, preferred_element_type=F32)
                m_prev = m_ref[qr, :]
                m_new = jnp.maximum(m_prev, jnp.max(raw, axis=-1, keepdims=True))
                alpha = jnp.exp2((m_prev - m_new) * MLA_EXP2_SCALE)
                ps = [jnp.exp2((t - m_new) * MLA_EXP2_SCALE) for t in _lane_tiles(raw)]
                l_ref[qr, :] = alpha * l_ref[qr, :] + functools.reduce(lambda a, b: a + b, ps)
                p = jnp.concatenate(ps, axis=1).astype(BF16)
                acc_ref[qr, :] = alpha * acc_ref[qr, :] + jnp.dot(p, v_ref[rows, :], preferred_element_type=F32)
                m_ref[qr, :] = m_new
            return carry

        lax.fori_loop(0, nk, step, 0, unroll=True)
        l = jnp.sum(l_ref[...], axis=-1, keepdims=True)
        o_ref[...] = acc_ref[...] / l
        lse_ref[...] = m_ref[...] * MLA_SCALE + jnp.log(l)

    return pl.pallas_call(
        body, name="mla_fwd", grid=(H_A, s // tq),
        in_specs=[_bs((None, tq, HEAD_PAD), lambda h, i: (h, i, 0)),
                  _bs((None, s, HEAD_PAD), lambda h, i: (h, 0, 0)),
                  _bs((None, s, V_DIM), lambda h, i: (h, 0, 0))],
        out_specs=[_bs((tq, V_DIM), lambda h, i: (i, h)), _bs((None, tq, LANES), lambda h, i: (h, i, 0))],
        out_shape=[jax.ShapeDtypeStruct((s, H_A * V_DIM), F32), jax.ShapeDtypeStruct((H_A, s, LANES), F32)],
        scratch_shapes=[pltpu.VMEM((tq, LANES), F32), pltpu.VMEM((tq, LANES), F32), pltpu.VMEM((tq, V_DIM), F32)],
        compiler_params=_arb(2),
    )(q, k, v)


def _mla_bwd(q, k, v, do, o, lse, tq=512, tk=512):
    s = q.shape[1]
    nq = s // tq

    def body(q_ref, k_ref, v_ref, do_ref, o_ref, lse_ref, dq_ref, dk_ref, dv_ref, delta_ref):
        @pl.when(pl.program_id(1) == 0)
        def _():
            def init(c, carry):
                rows = pl.ds(pl.multiple_of(c * tq, tq), tq)
                delta = jnp.sum(do_ref[rows, :] * o_ref[rows, :], axis=-1, keepdims=True)
                delta_ref[rows, :] = jnp.broadcast_to(delta, (tq, LANES))
                dq_ref[rows, :] = jnp.zeros((tq, HEAD_PAD), F32)
                return carry

            lax.fori_loop(0, nq, init, 0)

        dk_ref[...] = jnp.zeros(dk_ref.shape, F32)
        dv_ref[...] = jnp.zeros(dv_ref.shape, F32)
        kb = k_ref[...]
        vb = v_ref[...]

        def step(c, carry):
            rows = pl.ds(pl.multiple_of(c * tq, tq), tq)
            qc = q_ref[rows, :]
            doc = do_ref[rows, :].astype(BF16)
            raw = lax.dot_general(qc, kb, NT, preferred_element_type=F32)
            dp = lax.dot_general(doc, vb, NT, preferred_element_type=F32)
            lse2 = lse_ref[rows, :] * LOG2E
            delta = delta_ref[rows, :]
            ps = [jnp.exp2(t * MLA_EXP2_SCALE - lse2) for t in _lane_tiles(raw)]
            dss = [pj * (dj - delta) * MLA_SCALE for pj, dj in zip(ps, _lane_tiles(dp))]
            p = jnp.concatenate(ps, axis=1).astype(BF16)
            ds = jnp.concatenate(dss, axis=1).astype(BF16)
            dv_ref[...] += lax.dot_general(p, doc, TN, preferred_element_type=F32)
            dk_ref[...] += lax.dot_general(ds, qc, TN, preferred_element_type=F32)
            dq_ref[rows, :] += jnp.dot(ds, kb, preferred_element_type=F32)
            return carry

        lax.fori_loop(0, nq, step, 0, unroll=True)

    return pl.pallas_call(
        body, name="mla_bwd", grid=(H_A, s // tk),
        in_specs=[_bs((None, s, HEAD_PAD), lambda h, j: (h, 0, 0)),
                  _bs((None, tk, HEAD_PAD), lambda h, j: (h, j, 0)),
                  _bs((None, tk, V_DIM), lambda h, j: (h, j, 0)),
                  _bs((s, V_DIM), lambda h, j: (0, h)), _bs((s, V_DIM), lambda h, j: (0, h)),
                  _bs((None, s, LANES), lambda h, j: (h, 0, 0))],
        out_specs=[_bs((None, s, HEAD_PAD), lambda h, j: (h, 0, 0)),
                   _bs((None, tk, HEAD_PAD), lambda h, j: (h, j, 0)),
                   _bs((None, tk, V_DIM), lambda h, j: (h, j, 0))],
        out_shape=[jax.ShapeDtypeStruct((H_A, s, HEAD_PAD), F32), jax.ShapeDtypeStruct((H_A, s, HEAD_PAD), F32),
                   jax.ShapeDtypeStruct((H_A, s, V_DIM), F32)],
        scratch_shapes=[pltpu.VMEM((s, LANES), F32)],
        compiler_params=_arb(2),
    )(q, k, v, do, o, lse)


def _mla_bwd_prep(dq, dk, dv, cos_t, sin_t, ts=256):
    s = dq.shape[1]

    def body(dq_ref, dk_ref, dv_ref, cos_ref, sin_ref, dqp_ref, dkvp_ref, dkr_ref):
        cos_v = cos_ref[...]
        sin_v = sin_ref[...]
        kr = jnp.zeros((ts, LANES), F32)
        for h in range(H_A):
            dqp_ref[h, :, :LANES] = dq_ref[h, :, :LANES].astype(BF16)
            dqp_ref[h, :, LANES:] = _rope_bwd(dq_ref[h, :, LANES:], cos_v, sin_v).astype(BF16)
            dkvp_ref[h, :, :LANES] = dk_ref[h, :, :LANES].astype(BF16)
            dkvp_ref[h, :, LANES:] = dv_ref[h].astype(BF16)
            kr = kr + dk_ref[h, :, LANES:]
        dkr_ref[...] = _rope_bwd(kr, cos_v, sin_v).astype(BF16)

    blk3 = lambda w: _bs((H_A, ts, w), lambda i: (0, i, 0))
    return pl.pallas_call(
        body, name="mla_bwd_prep", grid=(s // ts,),
        in_specs=[blk3(HEAD_PAD), blk3(HEAD_PAD), blk3(V_DIM),
                  _bs((ts, LANES), lambda i: (i, 0)), _bs((ts, LANES), lambda i: (i, 0))],
        out_specs=[blk3(HEAD_PAD), blk3(HEAD_PAD), _bs((ts, LANES), lambda i: (i, 0))],
        out_shape=[jax.ShapeDtypeStruct((H_A, s, HEAD_PAD), BF16), jax.ShapeDtypeStruct((H_A, s, HEAD_PAD), BF16),
                   jax.ShapeDtypeStruct((s, LANES), BF16)],
        compiler_params=_arb(1),
    )(dq, dk, dv, cos_t, sin_t)


WIN_SCALE = 1.0 / math.sqrt(HD_B)
SPAN = Q_BLOCK + 2 * WINDOW


def _t5_bucket_table():
    a = jnp.arange(Q_BLOCK, dtype=jnp.int32)[:, None]
    c = jnp.arange(SPAN, dtype=jnp.int32)[None, :]
    rel = c - WINDOW - a
    nb = NUM_BUCKETS // 2
    max_exact = nb // 2
    base = (rel > 0).astype(jnp.int32) * nb
    n = jnp.abs(rel)
    nf = jnp.maximum(n, 1).astype(F32)
    large = max_exact + (jnp.log(nf / max_exact) / math.log(MAX_DISTANCE / max_exact)
                         * (nb - max_exact)).astype(jnp.int32)
    large = jnp.minimum(large, nb - 1)
    return base + jnp.where(n < max_exact, n, large)


def _win_bias(bucket, rel_bias):
    def body(rb_ref, bk_ref, o_ref):
        h = pl.program_id(0)
        bk = bk_ref[...]
        acc = jnp.zeros((Q_BLOCK, SPAN), F32)
        for b in range(NUM_BUCKETS):
            acc = jnp.where(bk == b, rb_ref[b, h], acc)
        o_ref[...] = acc

    return pl.pallas_call(
        body, name="win_bias", grid=(H_B,),
        in_specs=[pl.BlockSpec(memory_space=pltpu.SMEM), _bs((Q_BLOCK, SPAN), lambda h: (0, 0))],
        out_specs=_bs((None, Q_BLOCK, SPAN), lambda h: (h, 0, 0)),
        out_shape=jax.ShapeDtypeStruct((H_B, Q_BLOCK, SPAN), F32),
        compiler_params=_arb(1),
    )(rel_bias, bucket)


GROUP_W = GROUP * HD_B


def _win_kv_rows(n, j, nblk):
    blk = jnp.clip(n + j - 1, 0, nblk - 1)
    return pl.ds(pl.multiple_of(blk * Q_BLOCK, Q_BLOCK), Q_BLOCK)


def _win_head_cols(kv):
    return slice(kv * HD_B, (kv + 1) * HD_B)


def _win_stack(ref, kv):
    return jnp.concatenate([ref[:, kv * GROUP_W + g * HD_B:kv * GROUP_W + (g + 1) * HD_B] for g in range(GROUP)], axis=0)


def _win_unstack(ref, kv, val):
    for g in range(GROUP):
        ref[:, kv * GROUP_W + g * HD_B:kv * GROUP_W + (g + 1) * HD_B] = val[g * Q_BLOCK:(g + 1) * Q_BLOCK].astype(ref.dtype)


def _win_scores(q, k_ref, kv, bias_ref, n, nblk):
    a = lax.broadcasted_iota(jnp.int32, (GROUP, Q_BLOCK, Q_BLOCK), 1)
    cc = lax.broadcasted_iota(jnp.int32, (GROUP, Q_BLOCK, Q_BLOCK), 2)
    valid = [(cc >= a) & (n > 0), None, (cc <= a) & (n < nblk - 1)]
    out = []
    for j in range(3):
        sc = lax.dot_general(q, k_ref[_win_kv_rows(n, j, nblk), _win_head_cols(kv)], NT, preferred_element_type=F32)
        sc = (sc.reshape(GROUP, Q_BLOCK, Q_BLOCK) * WIN_SCALE
              + bias_ref[kv * GROUP:(kv + 1) * GROUP, :, j * Q_BLOCK:(j + 1) * Q_BLOCK])
        if valid[j] is not None:
            sc = jnp.where(valid[j], sc, -1e30)
        out.append(sc)
    return out


def _win_sink(sink_ref, kv):
    hs = lax.broadcasted_iota(jnp.int32, (GROUP, Q_BLOCK, 1), 0)
    sk = jnp.zeros((GROUP, Q_BLOCK, 1), F32)
    for g in range(GROUP):
        sk = jnp.where(hs == g, sink_ref[kv * GROUP + g], sk)
    return sk


def _win_fwd(proj_b, bias, sinks):
    s = proj_b.shape[0]
    nblk = s // Q_BLOCK
    rows = GROUP * Q_BLOCK

    def body(sink_ref, q_ref, k_ref, v_ref, bias_ref, o_ref, lse_ref):
        n = pl.program_id(0)
        for kv in range(KV_B):
            sk = _win_sink(sink_ref, kv)
            q = _win_stack(q_ref, kv)
            ss = _win_scores(q, k_ref, kv, bias_ref, n, nblk)
            m = jnp.maximum(jnp.max(jnp.maximum(jnp.maximum(ss[0], ss[1]), ss[2]), axis=2, keepdims=True), sk)
            es = [jnp.exp(sc - m) for sc in ss]
            l = jnp.sum(es[0] + es[1] + es[2], axis=2, keepdims=True) + jnp.exp(sk - m)
            acc = jnp.zeros((rows, HD_B), F32)
            for j, e in enumerate(es):
                p = (e / l).astype(BF16).reshape(rows, Q_BLOCK)
                acc = acc + jnp.dot(p, v_ref[_win_kv_rows(n, j, nblk), _win_head_cols(kv)],
                                    preferred_element_type=F32)
            _win_unstack(o_ref, kv, acc)
            lse_ref[kv * GROUP:(kv + 1) * GROUP] = m + jnp.log(l)

    kv_w = KV_B * HD_B
    return pl.pallas_call(
        body, name="win_fwd", grid=(nblk,),
        in_specs=[pl.BlockSpec(memory_space=pltpu.SMEM), _bs((Q_BLOCK, H_B * HD_B), lambda n: (n, P_QB // (H_B * HD_B))),
                  _bs((s, kv_w), lambda n: (0, P_KB // kv_w)), _bs((s, kv_w), lambda n: (0, P_VB // kv_w)),
                  _bs((H_B, Q_BLOCK, SPAN), lambda n: (0, 0, 0))],
        out_specs=[_bs((Q_BLOCK, H_B * HD_B), lambda n: (n, 0)), _bs((H_B, Q_BLOCK, 1), lambda n: (0, n, 0))],
        out_shape=[jax.ShapeDtypeStruct((s, H_B * HD_B), F32), jax.ShapeDtypeStruct((H_B, s, 1), F32)],
        compiler_params=_arb(1),
    )(sinks, proj_b, proj_b, proj_b, bias)


def _win_bwd(proj_b, bias, sinks, do_b, lse):
    s = proj_b.shape[0]
    nblk = s // Q_BLOCK
    rows = GROUP * Q_BLOCK
    spad = s + 2 * WINDOW

    def body(sink_ref, q_ref, k_ref, v_ref, bias_ref, do_ref, lse_ref, dq_ref, dk_ref, dv_ref, db_ref, dsk_ref):
        n = pl.program_id(0)

        @pl.when(n == 0)
        def _():
            dk_ref[...] = jnp.zeros(dk_ref.shape, F32)
            dv_ref[...] = jnp.zeros(dv_ref.shape, F32)
            db_ref[...] = jnp.zeros(db_ref.shape, F32)
            dsk_ref[...] = jnp.zeros(dsk_ref.shape, F32)

        for kv in range(KV_B):
            heads = slice(kv * GROUP, (kv + 1) * GROUP)
            sk = _win_sink(sink_ref, kv)
            q = _win_stack(q_ref, kv)
            dob = _win_stack(do_ref, kv)
            lse_v = lse_ref[heads]
            ss = _win_scores(q, k_ref, kv, bias_ref, n, nblk)
            ps = [jnp.exp(sc - lse_v) for sc in ss]
            dps = [lax.dot_general(dob, v_ref[_win_kv_rows(n, j, nblk), _win_head_cols(kv)], NT,
                                   preferred_element_type=F32).reshape(GROUP, Q_BLOCK, Q_BLOCK) for j in range(3)]
            delta = jnp.sum(ps[0] * dps[0] + ps[1] * dps[1] + ps[2] * dps[2], axis=2, keepdims=True)
            dq = jnp.zeros((rows, HD_B), F32)
            for j in range(3):
                ds = ps[j] * (dps[j] - delta)
                db_ref[heads, :, j * Q_BLOCK:(j + 1) * Q_BLOCK] += ds
                dsb = (ds * WIN_SCALE).astype(BF16).reshape(rows, Q_BLOCK)
                dq = dq + jnp.dot(dsb, k_ref[_win_kv_rows(n, j, nblk), _win_head_cols(kv)],
                                  preferred_element_type=F32)
                krows = pl.ds(pl.multiple_of((n + j) * Q_BLOCK, Q_BLOCK), Q_BLOCK)
                dk_ref[krows, _win_head_cols(kv)] += lax.dot_general(dsb, q, TN, preferred_element_type=F32)
                dv_ref[krows, _win_head_cols(kv)] += lax.dot_general(
                    ps[j].astype(BF16).reshape(rows, Q_BLOCK), dob, TN, preferred_element_type=F32)
            dsk_ref[heads] += -(jnp.exp(sk - lse_v) * delta)
            _win_unstack(dq_ref, kv, dq)

    kv_w = KV_B * HD_B
    qspec = _bs((Q_BLOCK, H_B * HD_B), lambda n: (n, 0))
    kacc = _bs((spad, kv_w), lambda n: (0, 0))
    return pl.pallas_call(
        body, name="win_bwd", grid=(nblk,),
        in_specs=[pl.BlockSpec(memory_space=pltpu.SMEM), _bs((Q_BLOCK, H_B * HD_B), lambda n: (n, P_QB // (H_B * HD_B))),
                  _bs((s, kv_w), lambda n: (0, P_KB // kv_w)), _bs((s, kv_w), lambda n: (0, P_VB // kv_w)),
                  _bs((H_B, Q_BLOCK, SPAN), lambda n: (0, 0, 0)), qspec, _bs((H_B, Q_BLOCK, 1), lambda n: (0, n, 0))],
        out_specs=[qspec, kacc, kacc, _bs((H_B, Q_BLOCK, SPAN), lambda n: (0, 0, 0)),
                   _bs((H_B, Q_BLOCK, 1), lambda n: (0, 0, 0))],
        out_shape=[jax.ShapeDtypeStruct((s, H_B * HD_B), BF16), jax.ShapeDtypeStruct((spad, kv_w), F32),
                   jax.ShapeDtypeStruct((spad, kv_w), F32), jax.ShapeDtypeStruct((H_B, Q_BLOCK, SPAN), F32),
                   jax.ShapeDtypeStruct((H_B, Q_BLOCK, 1), F32)],
        compiler_params=_arb(1),
    )(sinks, proj_b, proj_b, proj_b, bias, do_b, lse)


def _win_param_grads(bucket, dbias, dsink_rows):
    def body(bk_ref, db_ref, ds_ref, o_ref):
        bk = bk_ref[...]
        dbv = db_ref[...]
        lane = lax.broadcasted_iota(jnp.int32, (1, LANES), 1)
        res = jnp.zeros((1, LANES), F32)
        for b in range(NUM_BUCKETS):
            tot = jnp.sum(jnp.sum(jnp.where(bk == b, dbv, 0.0), axis=1, keepdims=True), axis=0, keepdims=True)
            res = jnp.where(lane == b, tot, res)
        stot = jnp.sum(ds_ref[...], axis=0, keepdims=True)
        o_ref[...] = jnp.where(lane == NUM_BUCKETS, stot, res)

    return pl.pallas_call(
        body, name="win_param_grads", grid=(H_B,),
        in_specs=[_bs((Q_BLOCK, SPAN), lambda h: (0, 0)), _bs((None, Q_BLOCK, SPAN), lambda h: (h, 0, 0)),
                  _bs((None, Q_BLOCK, 1), lambda h: (h, 0, 0))],
        out_specs=_bs((None, 1, LANES), lambda h: (h, 0, 0)),
        out_shape=jax.ShapeDtypeStruct((H_B, 1, LANES), F32),
        compiler_params=_arb(1),
    )(bucket, dbias, dsink_rows)


def _gate_fwd(proj, o_a, o_b, ts=256):
    s = o_a.shape[0]
    wide = lambda cb: _bs((ts, D_MODEL), lambda i: (i, cb))

    def body(ga_ref, gb_ref, oa_ref, ob_ref, m_ref):
        m_ref[...] = (jax.nn.sigmoid(ga_ref[...]) * oa_ref[...]
                      + jax.nn.sigmoid(gb_ref[...]) * ob_ref[...]).astype(BF16)

    return pl.pallas_call(
        body, name="gate_fwd", grid=(s // ts,),
        in_specs=[wide(P_GA // D_MODEL), wide(P_GB // D_MODEL), wide(0), wide(0)],
        out_specs=wide(0), out_shape=jax.ShapeDtypeStruct((s, D_MODEL), BF16),
        compiler_params=_arb(1),
    )(proj, proj, o_a, o_b)


def _gate_bwd(dmixed, proj, o_a, o_b, ts=256):
    s = o_a.shape[0]
    wide = lambda cb: _bs((ts, D_MODEL), lambda i: (i, cb))

    def body(dm_ref, ga_ref, gb_ref, oa_ref, ob_ref, doa_ref, dob_ref, dga_ref, dgb_ref):
        dm = dm_ref[...]
        sa = jax.nn.sigmoid(ga_ref[...])
        sb = jax.nn.sigmoid(gb_ref[...])
        doa_ref[...] = dm * sa
        dob_ref[...] = (dm * sb).astype(BF16)
        dga_ref[...] = (dm * oa_ref[...] * (sa * (1.0 - sa))).astype(BF16)
        dgb_ref[...] = (dm * ob_ref[...] * (sb * (1.0 - sb))).astype(BF16)

    return pl.pallas_call(
        body, name="gate_bwd", grid=(s // ts,),
        in_specs=[wide(0), wide(P_GA // D_MODEL), wide(P_GB // D_MODEL), wide(0), wide(0)],
        out_specs=[wide(0)] * 4,
        out_shape=[jax.ShapeDtypeStruct((s, D_MODEL), F32), jax.ShapeDtypeStruct((s, D_MODEL), BF16),
                   jax.ShapeDtypeStruct((s, D_MODEL), BF16), jax.ShapeDtypeStruct((s, D_MODEL), BF16)],
        compiler_params=_arb(1),
    )(dmixed, proj, proj, o_a, o_b)


CONV_CHUNK = 128
N_SLAB = D_FF // LANES


def _shifted(ref, c, nchunks):
    r0 = c * CONV_CHUNK
    cur = ref[r0:r0 + CONV_CHUNK, :]
    row = lax.broadcasted_iota(jnp.int32, (8, LANES), 0)
    if c > 0:
        prev = ref[r0 - 1:r0 - 1 + CONV_CHUNK, :]
    else:
        down = pltpu.roll(cur, 1, 0)
        prev = jnp.concatenate([jnp.where(row == 0, 0.0, down[:8]), down[8:]], axis=0)
    if c < nchunks - 1:
        nxt = ref[r0 + 1:r0 + 1 + CONV_CHUNK, :]
    else:
        up = pltpu.roll(cur, CONV_CHUNK - 1, 0)
        nxt = jnp.concatenate([up[:-8], jnp.where(row == 7, 0.0, up[-8:])], axis=0)
    return prev, cur, nxt


def _conv_taps(ref, w_ref, b_ref, c, nchunks):
    prev, cur, nxt = _shifted(ref, c, nchunks)
    conv = prev * w_ref[0:1, :] + cur * w_ref[1:2, :] + nxt * w_ref[2:3, :] + b_ref[...]
    return conv, prev, cur, nxt


def _convffn_fwd(u, conv_w, conv_b):
    s = u.shape[0]
    nchunks = s // CONV_CHUNK

    def body(ug_ref, uv_ref, wg_ref, wv_ref, bg_ref, bv_ref, f_ref):
        for c in range(nchunks):
            cg = _conv_taps(ug_ref, wg_ref, bg_ref, c, nchunks)[0]
            cv = _conv_taps(uv_ref, wv_ref, bv_ref, c, nchunks)[0]
            f_ref[c * CONV_CHUNK:(c + 1) * CONV_CHUNK, :] = (cg * jax.nn.sigmoid(cg) * cv).astype(BF16)

    slab = lambda off: _bs((s, LANES), lambda j: (0, off + j))
    wsl = lambda off: _bs((3, LANES), lambda j: (0, off + j))
    bsl = lambda off: _bs((1, LANES), lambda j: (0, off + j))
    return pl.pallas_call(
        body, name="convffn_fwd", grid=(N_SLAB,),
        in_specs=[slab(0), slab(N_SLAB), wsl(0), wsl(N_SLAB), bsl(0), bsl(N_SLAB)],
        out_specs=slab(0), out_shape=jax.ShapeDtypeStruct((s, D_FF), BF16),
        compiler_params=_arb(1),
    )(u, u, conv_w, conv_w, conv_b, conv_b)


def _convffn_bwd(u, conv_w, conv_b, df):
    s = u.shape[0]
    nchunks = s // CONV_CHUNK

    def body(ug_ref, uv_ref, wg_ref, wv_ref, bg_ref, bv_ref, df_ref, du_ref, dw_ref, db_ref, dcg_ref, dcv_ref):
        dwg = [jnp.zeros((1, LANES), F32) for _ in range(3)]
        dwv = [jnp.zeros((1, LANES), F32) for _ in range(3)]
        dbg = jnp.zeros((1, LANES), F32)
        dbv = jnp.zeros((1, LANES), F32)
        for c in range(nchunks):
            rows = slice(c * CONV_CHUNK, (c + 1) * CONV_CHUNK)
            cg, gp, gc, gn = _conv_taps(ug_ref, wg_ref, bg_ref, c, nchunks)
            cv, vp, vc, vn = _conv_taps(uv_ref, wv_ref, bv_ref, c, nchunks)
            dfv = df_ref[rows, :]
            sg = jax.nn.sigmoid(cg)
            dcg = dfv * cv * (sg * (1.0 + cg * (1.0 - sg)))
            dcv = dfv * (cg * sg)
            dcg_ref[rows, :] = dcg
            dcv_ref[rows, :] = dcv
            for t, (tg, tv) in enumerate(((gp, vp), (gc, vc), (gn, vn))):
                dwg[t] = dwg[t] + jnp.sum(tg * dcg, axis=0, keepdims=True)
                dwv[t] = dwv[t] + jnp.sum(tv * dcv, axis=0, keepdims=True)
            dbg = dbg + jnp.sum(dcg, axis=0, keepdims=True)
            dbv = dbv + jnp.sum(dcv, axis=0, keepdims=True)
        for t in range(3):
            dw_ref[0, t:t + 1, :] = dwg[t]
            dw_ref[1, t:t + 1, :] = dwv[t]
        db_ref[0] = dbg
        db_ref[1] = dbv
        for half, (dc_ref, w_ref) in enumerate(((dcg_ref, wg_ref), (dcv_ref, wv_ref))):
            for c in range(nchunks):
                prev, cur, nxt = _shifted(dc_ref, c, nchunks)
                du = nxt * w_ref[0:1, :] + cur * w_ref[1:2, :] + prev * w_ref[2:3, :]
                du_ref[half, c * CONV_CHUNK:(c + 1) * CONV_CHUNK, :] = du.astype(BF16)

    slab = lambda off: _bs((s, LANES), lambda j: (0, off + j))
    wsl = lambda off: _bs((3, LANES), lambda j: (0, off + j))
    bsl = lambda off: _bs((1, LANES), lambda j: (0, off + j))
    return pl.pallas_call(
        body, name="convffn_bwd", grid=(N_SLAB,),
        in_specs=[slab(0), slab(N_SLAB), wsl(0), wsl(N_SLAB), bsl(0), bsl(N_SLAB), slab(0)],
        out_specs=[_bs((2, s, LANES), lambda j: (0, 0, j)), _bs((2, 3, LANES), lambda j: (0, 0, j)),
                   _bs((2, 1, LANES), lambda j: (0, 0, j))],
        out_shape=[jax.ShapeDtypeStruct((2, s, D_FF), BF16), jax.ShapeDtypeStruct((2, 3, D_FF), F32),
                   jax.ShapeDtypeStruct((2, 1, D_FF), F32)],
        scratch_shapes=[pltpu.VMEM((s, LANES), F32), pltpu.VMEM((s, LANES), F32)],
        compiler_params=_arb(1),
    )(u, u, conv_w, conv_w, conv_b, conv_b, df)


def _row_tile(rows, limit=512):
    best = rows
    for t in range(8, min(rows, limit) + 1, 8):
        if rows % t == 0:
            best = t
    return best if rows % 8 == 0 else rows


ADAM_C1 = 1.0 - ADAM_B1 ** ADAM_STEP
ADAM_C2 = 1.0 - ADAM_B2 ** ADAM_STEP


def _adamw_math(w, gv, m, v):
    nm = ADAM_B1 * m + (1.0 - ADAM_B1) * gv
    nv = ADAM_B2 * v + (1.0 - ADAM_B2) * (gv * gv)
    m_hat = nm / ADAM_C1
    v_hat = nv / ADAM_C2
    return -ADAM_LR * (m_hat / (jnp.sqrt(v_hat) + ADAM_EPS) + ADAM_WD * w), nm, nv


def _adamw_halves(name, core, w, mine, theirs, m, v):
    half, cols = mine.shape
    tr = _row_tile(half)
    nr = half // tr

    def body(core_ref, w_ref, mine_ref, theirs_ref, m_ref, v_ref, g_ref, d_ref, nm_ref, nv_ref):
        gv = jnp.where(pl.program_id(0) == core_ref[0], mine_ref[...], theirs_ref[...])
        g_ref[...] = gv
        d_ref[...], nm_ref[...], nv_ref[...] = _adamw_math(w_ref[...], gv, m_ref[...], v_ref[...])

    full = pl.BlockSpec((tr, cols), lambda hf, r, cr: (hf * nr + r, 0))
    part = pl.BlockSpec((tr, cols), lambda hf, r, cr: (r, 0))
    return pl.pallas_call(
        body, name=name,
        grid_spec=pltpu.PrefetchScalarGridSpec(num_scalar_prefetch=1, grid=(2, nr),
                                               in_specs=[full, part, part, full, full], out_specs=[full] * 4),
        out_shape=[jax.ShapeDtypeStruct((2 * half, cols), F32)] * 4, compiler_params=_arb(2),
    )(core, w, mine, theirs, m, v)


ANY = pl.BlockSpec(memory_space=pl.ANY)


def _mesh_pos():
    return lax.axis_index("x"), lax.axis_index("y"), lax.axis_index("c")


def _other_chips(x, y):
    return [(1 - x, y), (x, 1 - y), (1 - x, 1 - y)]


def _forward_to_sibling(gathered):
    n = len(gathered)

    def body(*refs):
        in_refs, out_refs = refs[:n], refs[n:2 * n]
        send_sems, recv_sems = refs[2 * n:]
        x, y, c = _mesh_pos()
        cps = []
        for i in range(n):
            for k, chip in enumerate(_other_chips(x, y)):
                pk = 2 * chip[0] + chip[1]
                sems = dict(send_sem=send_sems.at[3 * i + k], recv_sem=recv_sems.at[3 * i + k],
                            device_id=(x, y, 1 - c), device_id_type=MESH)
                sent = pltpu.make_async_remote_copy(src_ref=in_refs[i].at[pk, c], dst_ref=out_refs[i].at[pk, c], **sems)
                sent.start()
                theirs = out_refs[i].at[pk, 1 - c]
                cps.append((sent, pltpu.make_async_remote_copy(src_ref=theirs, dst_ref=theirs, **sems)))
        for sent, arrived in cps:
            sent.wait_send()
            arrived.wait_recv()

    return pl.pallas_call(
        body, name="forward_to_sibling", in_specs=[ANY] * n, out_specs=[ANY] * n,
        out_shape=[jax.ShapeDtypeStruct(a.shape, a.dtype) for a in gathered],
        input_output_aliases={i: i for i in range(n)},
        scratch_shapes=[pltpu.SemaphoreType.DMA((3 * n,)), pltpu.SemaphoreType.DMA((3 * n,))],
    )(*gathered)


def _rs_pair_exchange(name, grads):
    n = len(grads)

    def body(*refs):
        g_refs, o_refs = refs[:n], refs[n:2 * n]
        send_sems, recv_sems = refs[2 * n:]
        x, y, c = _mesh_pos()
        cps = []
        for i in range(n):
            cp = pltpu.make_async_remote_copy(
                src_ref=g_refs[i].at[:, 1 - c], dst_ref=o_refs[i],
                send_sem=send_sems.at[i], recv_sem=recv_sems.at[i], device_id=(x, y, 1 - c), device_id_type=MESH)
            cp.start()
            cps.append(cp)
        for cp in cps:
            cp.wait()

    return pl.pallas_call(
        body, name=name, in_specs=[ANY] * n, out_specs=[ANY] * n,
        out_shape=[jax.ShapeDtypeStruct((4,) + g.shape[2:], F32) for g in grads],
        scratch_shapes=[pltpu.SemaphoreType.DMA((n,)), pltpu.SemaphoreType.DMA((n,))],
    )(*grads)


def _rs_pair_add(name, core, g, recv):
    _, half, cols = recv.shape
    tr = _row_tile(half)
    nr = half // tr

    def body(core_ref, g_ref, r_ref, o_ref):
        o_ref[...] = (g_ref[...] + r_ref[...]).astype(BF16)

    return pl.pallas_call(
        body, name=name,
        grid_spec=pltpu.PrefetchScalarGridSpec(
            num_scalar_prefetch=1, grid=(4, nr),
            in_specs=[pl.BlockSpec((None, None, tr, cols), lambda q, r, cr: (q, cr[0], r, 0)),
                      pl.BlockSpec((None, tr, cols), lambda q, r, cr: (q, r, 0))],
            out_specs=pl.BlockSpec((None, tr, cols), lambda q, r, cr: (q, r, 0))),
        out_shape=jax.ShapeDtypeStruct((4, half, cols), BF16),
        compiler_params=_arb(2),
    )(core, g, recv)


def _rs_final_add(name, chip, pair, recv):
    _, half, cols = pair.shape
    tr = _row_tile(half)

    def body(chip_ref, p_ref, r_ref, o_ref):
        o_ref[...] = ((p_ref[...].astype(F32) + r_ref[0].astype(F32)) + r_ref[1].astype(F32)) + r_ref[2].astype(F32)

    return pl.pallas_call(
        body, name=name,
        grid_spec=pltpu.PrefetchScalarGridSpec(
            num_scalar_prefetch=1, grid=(half // tr,),
            in_specs=[pl.BlockSpec((None, tr, cols), lambda r, ch: (ch[0], r, 0)),
                      pl.BlockSpec((3, tr, cols), lambda r, ch: (0, r, 0))],
            out_specs=pl.BlockSpec((tr, cols), lambda r, ch: (r, 0))),
        out_shape=jax.ShapeDtypeStruct((half, cols), F32),
        compiler_params=_arb(1),
    )(chip, pair, recv)


def _rs_pair_share(halves):
    n = len(halves)

    def body(*refs):
        h_refs, o_refs = refs[:n], refs[n:2 * n]
        send_sems, recv_sems = refs[2 * n:]
        x, y, c = _mesh_pos()
        cps = []
        for i in range(n):
            cp = pltpu.make_async_remote_copy(src_ref=h_refs[i], dst_ref=o_refs[i], send_sem=send_sems.at[i],
                                              recv_sem=recv_sems.at[i], device_id=(x, y, 1 - c), device_id_type=MESH)
            cp.start()
            cps.append(cp)
        for cp in cps:
            cp.wait()

    return pl.pallas_call(
        body, name="rs_pair_share", in_specs=[ANY] * n, out_specs=[ANY] * n,
        out_shape=[jax.ShapeDtypeStruct(h.shape, F32) for h in halves],
        scratch_shapes=[pltpu.SemaphoreType.DMA((n,)), pltpu.SemaphoreType.DMA((n,))],
    )(*halves)


HBM = pl.BlockSpec(memory_space=pltpu.HBM)
SEM = pl.BlockSpec(memory_space=pltpu.SEMAPHORE)


class _SplitExchange:
    def __init__(self, name, srcs, land_shapes, src_of, dst_of, arrive_of, to_sibling=False):
        self.name, self.srcs, self.land_shapes = name, list(srcs), list(land_shapes)
        self.src_of, self.dst_of, self.arrive_of = src_of, dst_of, arrive_of
        self.to_sibling = to_sibling
        self.fan = 1 if to_sibling else 3

    def _copies(self, src_refs, land_refs, send_sems, recv_sems):
        x, y, c = _mesh_pos()
        p = 2 * x + y
        if self.to_sibling:
            peers = [((x, y, 1 - c), 1 - c)]
        else:
            peers = [((*chip, c), 2 * chip[0] + chip[1]) for chip in _other_chips(x, y)]
        out = []
        for i, (src, land) in enumerate(zip(src_refs, land_refs)):
            for k, (peer, pk) in enumerate(peers):
                sems = dict(send_sem=send_sems.at[self.fan * i + k], recv_sem=recv_sems.at[self.fan * i + k],
                            device_id=peer, device_id_type=MESH)
                sent = pltpu.make_async_remote_copy(src_ref=self.src_of(src, k, p, pk, c),
                                                    dst_ref=self.dst_of(land, k, p, pk, c), **sems)
                here = self.arrive_of(land, k, p, pk, c)
                out.append((sent, pltpu.make_async_remote_copy(src_ref=here, dst_ref=here, **sems)))
        return out

    def start(self, after=None):
        n = len(self.srcs)
        n_in = 2 * n + (after is not None)

        def body(*refs):
            for sent, _ in self._copies(refs[:n], refs[n:2 * n], refs[n_in], refs[n_in + 1]):
                sent.start()
            refs[-1][...] = jnp.zeros((8, LANES), F32)

        lands = [lax.empty(shape, src.dtype) for shape, src in zip(self.land_shapes, self.srcs)]
        operands = [pltpu.with_memory_space_constraint(a, pltpu.HBM) for a in self.srcs + lands]
        outs = pl.pallas_call(
            body, name=self.name + "_start",
            out_shape=(pltpu.SemaphoreType.DMA((self.fan * n,)), pltpu.SemaphoreType.DMA((self.fan * n,)),
                       *[pltpu.HBM(a.shape, a.dtype) for a in operands], jax.ShapeDtypeStruct((8, LANES), F32)),
            in_specs=[HBM] * (2 * n) + [ANY] * (after is not None),
            out_specs=(SEM, SEM, *[HBM] * (2 * n), pl.BlockSpec(memory_space=pltpu.VMEM)),
            input_output_aliases={j: 2 + j for j in range(2 * n)},
            compiler_params=pltpu.CompilerParams(has_side_effects=pltpu.SideEffectType.DATAFLOW_SIDE_EFFECTING),
        )(*operands, *([after] if after is not None else []))
        self._sems, self._thru = outs[:2], list(outs[2:2 + 2 * n])
        return outs[-1]

    def wait(self, after):
        n = len(self.srcs)

        def body(*refs):
            for sent, arrived in self._copies(refs[:n], refs[n:2 * n], refs[2 * n], refs[2 * n + 1]):
                sent.wait_send()
                arrived.wait_recv()

        outs = pl.pallas_call(
            body, name=self.name + "_wait",
            out_shape=tuple(pltpu.HBM(a.shape, a.dtype) for a in self._thru),
            in_specs=[HBM] * (2 * n) + [SEM, SEM, ANY], out_specs=tuple([HBM] * (2 * n)),
            input_output_aliases={j: j for j in range(2 * n)},
            compiler_params=pltpu.CompilerParams(has_side_effects=pltpu.SideEffectType.DATAFLOW_SIDE_EFFECTING),
        )(*self._thru, *self._sems, after)
        return list(outs[:n]), list(outs[n:])


def _small_allreduce(parts):
    n = len(parts)

    def body(*refs):
        in_refs, out_refs, gather_refs = refs[:n], refs[n:2 * n], refs[2 * n:3 * n]
        send_sems, recv_sems = refs[3 * n:]
        x, y, c = _mesh_pos()
        me = 4 * x + 2 * y + c
        cps = []
        for i in range(n):
            gather_refs[i][me] = in_refs[i][...]
            for j in range(1, 8):
                peer = (x ^ (j >> 2), y ^ ((j >> 1) & 1), c ^ (j & 1))
                cp = pltpu.make_async_remote_copy(
                    src_ref=in_refs[i], dst_ref=gather_refs[i].at[me], send_sem=send_sems.at[7 * i + j - 1],
                    recv_sem=recv_sems.at[7 * i + j - 1], device_id=peer, device_id_type=MESH)
                cp.start()
                cps.append(cp)
        for i in range(n):
            for j in range(1, 8):
                peer_id = 4 * (x ^ (j >> 2)) + 2 * (y ^ ((j >> 1) & 1)) + (c ^ (j & 1))
                slot = gather_refs[i].at[peer_id]
                pltpu.make_async_remote_copy(src_ref=slot, dst_ref=slot, send_sem=send_sems.at[7 * i + j - 1],
                                             recv_sem=recv_sems.at[7 * i + j - 1], device_id=(x, y, c),
                                             device_id_type=MESH).wait_recv()
        for cp in cps:
            cp.wait_send()
        for i in range(n):
            tot = gather_refs[i][0]
            for d in range(1, 8):
                tot = tot + gather_refs[i][d]
            out_refs[i][...] = tot

    vmem = pl.BlockSpec(memory_space=pltpu.VMEM)
    return pl.pallas_call(
        body, name="small_allreduce", in_specs=[vmem] * n, out_specs=[vmem] * n,
        out_shape=[jax.ShapeDtypeStruct(p.shape, F32) for p in parts],
        scratch_shapes=[pltpu.VMEM((8,) + p.shape, F32) for p in parts]
        + [pltpu.SemaphoreType.DMA((7 * n,)), pltpu.SemaphoreType.DMA((7 * n,))],
    )(*parts)


def _adamw_small(ws, gs, ms, vs):
    n = len(ws)

    def body(*refs):
        for i in range(n):
            w_ref, g_ref, m_ref, v_ref = refs[i], refs[n + i], refs[2 * n + i], refs[3 * n + i]
            d_ref, nm_ref, nv_ref = refs[4 * n + i], refs[5 * n + i], refs[6 * n + i]
            d_ref[...], nm_ref[...], nv_ref[...] = _adamw_math(w_ref[...], g_ref[...], m_ref[...], v_ref[...])

    vmem = pl.BlockSpec(memory_space=pltpu.VMEM)
    shapes = [jax.ShapeDtypeStruct(w.shape, F32) for w in ws]
    outs = pl.pallas_call(body, name="adamw_small", in_specs=[vmem] * (4 * n), out_specs=[vmem] * (3 * n),
                          out_shape=shapes * 3)(*ws, *gs, *ms, *vs)
    return outs[:n], outs[n:2 * n], outs[2 * n:]


W_IN_PIECES = ((0, 256, P_QLAT), (256, 384, P_CKV), (384, 448, P_KR), (448, 1472, P_QB), (1472, 1728, P_KB),
               (1728, 1984, P_VB), (1984, 3008, P_GA), (3008, 4032, P_GB))
W_IN_SHARD = 1008


def _w_in_from_shards(shards):
    cols = []
    for lo, hi, _ in sorted(W_IN_PIECES, key=lambda piece: piece[2]):
        for q in range(4):
            a, b = max(lo, q * W_IN_SHARD), min(hi, (q + 1) * W_IN_SHARD)
            if a < b:
                cols.append(shards[q][:, a - q * W_IN_SHARD:b - q * W_IN_SHARD])
    cols.append(jnp.zeros((shards.shape[1], W_IN_PAD - 4 * W_IN_SHARD), shards.dtype))
    return jnp.concatenate(cols, axis=1)


def _w_in_to_shards(p):
    shards = []
    for q in range(4):
        cols = []
        for lo, hi, at in W_IN_PIECES:
            a, b = max(lo, q * W_IN_SHARD), min(hi, (q + 1) * W_IN_SHARD)
            if a < b:
                cols.append(p[:, at + a - lo:at + b - lo])
        shards.append(jnp.concatenate(cols, axis=1))
    return jnp.stack(shards)


def _col_shards(w):
    r, c4 = w.shape
    return w.reshape(r, 4, c4 // 4).transpose(1, 0, 2)


def _local_step(x, positions, target, norm1_g, first_weights, q_a_norm_g, kv_a_norm_g, rel_bias, sinks,
                late_weights, norm2_g, conv_b, final_norm_g, early_grads=None, last_grads=None):
    s = x.shape[0]
    half = QK_ROPE // 2
    inv_freq = jnp.asarray(np.float32(ROPE_THETA) ** (-np.arange(half, dtype=np.float32) / np.float32(half)))
    ang = positions.astype(F32)[:, None] * inv_freq[None, :]
    cos, sin = jnp.cos(ang), jnp.sin(ang)
    z64 = jnp.zeros((s, 64), F32)
    cos_t = jnp.concatenate([cos, cos, z64], axis=1)
    sin_t = jnp.concatenate([-sin, sin, z64], axis=1)
    bucket = _t5_bucket_table()
    sinks1 = sinks.reshape(H_B)

    h1, rstd1 = _rmsnorm_fwd("norm1_fwd", x, norm1_g, D_MODEL, 0)
    w_in_p, wq, wkv = first_weights(h1)
    proj, proj_b = _matmul("proj", h1, w_in_p, out_shape=(s, W_IN_PAD), out_dtype=F32, grid=(s // MM_ROWS, W_IN_PAD // 1024, 1),
                           a_spec=_bs((MM_ROWS, D_MODEL), lambda i, j, k: (i, 0)), b_spec=_bs((D_MODEL, 1024), lambda i, j, k: (0, j)),
                           o_spec=_bs((MM_ROWS, 1024), lambda i, j, k: (i, j)), contract=NN, bf16_copy=True)
    qn, cn, rstd_q, rstd_c = _lat_norms(proj, q_a_norm_g, kv_a_norm_g)
    q = _q_heads(qn, wq, cos_t, sin_t)
    k, v = _kv_heads(cn, wkv, proj, cos_t, sin_t)
    o_a, lse_a = _mla_fwd(q, k, v)

    bias = _win_bias(bucket, rel_bias)
    o_b, lse_b = _win_fwd(proj_b, bias, sinks1)

    mixed = _gate_fwd(proj, o_a, o_b)
    w_out, w_up, w_down, conv_w = late_weights(mixed)
    row512 = lambda w: _bs((MM_ROWS, w), lambda i, j, k: (i, 0))
    whole = lambda r, c: _bs((r, c), lambda i, j, k: (0, 0))
    x1 = _matmul("attn_out", mixed, w_out, out_shape=(s, D_MODEL), out_dtype=F32, grid=(s // MM_ROWS, 1, 1),
                 a_spec=row512(D_MODEL), b_spec=whole(D_MODEL, D_MODEL), o_spec=row512(D_MODEL), contract=NN, add=x)
    h2, rstd2 = _rmsnorm_fwd("norm2_fwd", x1, norm2_g, D_MODEL, 0)
    u = _matmul("ffn_up", h2, w_up, out_shape=(s, 2 * D_FF), out_dtype=F32, grid=(s // MM_ROWS, 4, 1),
                a_spec=_bs((MM_ROWS, D_MODEL), lambda i, j, k: (i, 0)), b_spec=_bs((D_MODEL, D_FF // 2), lambda i, j, k: (0, j)),
                o_spec=_bs((MM_ROWS, D_FF // 2), lambda i, j, k: (i, j)), contract=NN)
    f = _convffn_fwd(u, conv_w, conv_b)
    x2 = _matmul("ffn_down", f, w_down, out_shape=(s, D_MODEL), out_dtype=F32, grid=(s // MM_ROWS, 1, 1),
                 a_spec=row512(D_FF), b_spec=whole(D_FF, D_MODEL), o_spec=row512(D_MODEL), contract=NN, add=x1)
    loss, dx2, d_final_g, dx2_b = _final_loss(x2, target, final_norm_g.reshape(1, D_MODEL))
    tk = min(s, DW_ROWS)

    df = _matmul("ffn_down_dx", dx2_b, w_down, out_shape=(s, D_FF), out_dtype=F32, grid=(s // MM_ROWS, 2, 1),
                 a_spec=row512(D_MODEL), b_spec=_bs((D_FF // 2, D_MODEL), lambda i, j, k: (j, 0)),
                 o_spec=_bs((MM_ROWS, D_FF // 2), lambda i, j, k: (i, j)), contract=NT)
    d_w_down = _matmul("ffn_down_dw", f, dx2_b, out_shape=(D_FF, D_MODEL), out_dtype=F32, grid=(2, 1, s // tk),
                       a_spec=_bs((tk, D_FF // 2), lambda i, j, k: (k, i)), b_spec=_bs((tk, D_MODEL), lambda i, j, k: (k, 0)),
                       o_spec=_bs((D_FF // 2, D_MODEL), lambda i, j, k: (i, 0)), contract=TN)
    du, d_conv_w2, d_conv_b2 = _convffn_bwd(u, conv_w, conv_b, df)
    kc = D_FF // 2
    dh2 = _matmul("ffn_up_dx", du, w_up, out_shape=(s, D_MODEL), out_dtype=F32, grid=(s // 1024, 1, 4),
                  a_spec=_bs((None, 1024, kc), lambda i, j, k: (k // 2, i, k % 2)),
                  b_spec=_bs((D_MODEL, kc), lambda i, j, k: (0, k)),
                  o_spec=_bs((1024, D_MODEL), lambda i, j, k: (i, 0)), contract=NT)
    d_w_up = _matmul("ffn_up_dw", h2, du, out_shape=(D_MODEL, 2 * D_FF), out_dtype=F32, grid=(1, 4, s // tk),
                     a_spec=_bs((tk, D_MODEL), lambda i, j, k: (k, 0)),
                     b_spec=_bs((None, tk, kc), lambda i, j, k: (j // 2, k, j % 2)),
                     o_spec=_bs((D_MODEL, kc), lambda i, j, k: (0, j)), contract=TN)
    dx1, d_norm2_g, dx1_b = _rmsnorm_bwd("norm2_bwd", dh2, x1, rstd2, norm2_g, D_MODEL, 0, F32, res=dx2, bf16_copy=True)

    d_w_out = _matmul("attn_out_dw", mixed, dx1_b, out_shape=(D_MODEL, D_MODEL), out_dtype=F32, grid=(1, 1, s // tk),
                      a_spec=_bs((tk, D_MODEL), lambda i, j, k: (k, 0)), b_spec=_bs((tk, D_MODEL), lambda i, j, k: (k, 0)),
                      o_spec=whole(D_MODEL, D_MODEL), contract=TN)
    token, early_grads_on = early_grads(d_w_out, d_w_up, d_w_down) if early_grads is not None else (None, None)
    dmixed = _matmul("attn_out_dx", dx1_b, w_out, out_shape=(s, D_MODEL), out_dtype=F32, grid=(s // MM_ROWS, 1, 1),
                     a_spec=row512(D_MODEL), b_spec=whole(D_MODEL, D_MODEL), o_spec=row512(D_MODEL), contract=NT,
                     after=token)
    do_a, do_b, d_ga, d_gb = _gate_bwd(dmixed, proj, o_a, o_b)
    if early_grads_on is not None:
        sinks1 = sinks1 + early_grads_on(d_ga)[0, :H_B]

    d_qb, dk_pad, dv_pad, dbias, dsink_rows = _win_bwd(proj_b, bias, sinks1, do_b, lse_b)
    wp = _win_param_grads(bucket, dbias, dsink_rows)[:, 0, :]
    d_rel_bias = wp[:, :NUM_BUCKETS].T
    d_sinks = wp[:, NUM_BUCKETS].reshape(1, H_B)
    d_kb = dk_pad[WINDOW:WINDOW + s].astype(BF16)
    d_vb = dv_pad[WINDOW:WINDOW + s].astype(BF16)

    dq, dk, dv = _mla_bwd(q, k, v, do_a, o_a, lse_a)
    dq_pre, dkv_pre, d_kr = _mla_bwd_prep(dq, dk, dv, cos_t, sin_t)
    th = min(s, HEAD_ROWS)
    hgrid = (s // th, 1, H_A)
    hblock = _bs((None, th, HEAD_PAD), lambda i, j, k: (k, i, 0))
    hrows = lambda w: _bs((th, w), lambda i, j, k: (i, 0))
    dqn = _matmul("q_up_dx", dq_pre, wq, out_shape=(s, Q_LORA), out_dtype=F32, grid=hgrid, a_spec=hblock,
                  b_spec=_bs((None, Q_LORA, HEAD_PAD), lambda i, j, k: (k, 0, 0)), o_spec=hrows(Q_LORA), contract=NT)
    dcn = _matmul("kv_up_dx", dkv_pre, wkv, out_shape=(s, KV_LORA), out_dtype=F32, grid=hgrid, a_spec=hblock,
                  b_spec=_bs((None, KV_LORA, HEAD_PAD), lambda i, j, k: (k, 0, 0)), o_spec=hrows(KV_LORA), contract=NT)
    wgrid = (H_A, 1, s // th)
    d_wq = _matmul("q_up_dw", qn, dq_pre, out_shape=(H_A, Q_LORA, HEAD_PAD), out_dtype=F32, grid=wgrid,
                   a_spec=_bs((th, Q_LORA), lambda i, j, k: (k, 0)), b_spec=_bs((None, th, HEAD_PAD), lambda i, j, k: (i, k, 0)),
                   o_spec=_bs((None, Q_LORA, HEAD_PAD), lambda i, j, k: (i, 0, 0)), contract=TN)
    d_wkv = _matmul("kv_up_dw", cn, dkv_pre, out_shape=(H_A, KV_LORA, HEAD_PAD), out_dtype=F32, grid=wgrid,
                    a_spec=_bs((th, KV_LORA), lambda i, j, k: (k, 0)), b_spec=_bs((None, th, HEAD_PAD), lambda i, j, k: (i, k, 0)),
                    o_spec=_bs((None, KV_LORA, HEAD_PAD), lambda i, j, k: (i, 0, 0)), contract=TN)
    d_qlat, d_gq = _rmsnorm_bwd("q_norm_bwd", dqn, proj, rstd_q, q_a_norm_g, Q_LORA, P_QLAT // Q_LORA, BF16)
    d_ckv, d_gkv = _rmsnorm_bwd("kv_norm_bwd", dcn, proj, rstd_c, kv_a_norm_g, KV_LORA, P_CKV // KV_LORA, BF16)

    dproj = jnp.concatenate([d_qb, d_ga, d_gb, d_qlat, d_kb, d_vb, d_ckv, d_kr], axis=1)
    d_w_in_p = _matmul("proj_dw", h1, dproj, out_shape=(D_MODEL, W_IN_PAD), out_dtype=F32, grid=(1, W_IN_PAD // 1024, s // tk),
                       a_spec=_bs((tk, D_MODEL), lambda i, j, k: (k, 0)), b_spec=_bs((tk, 1024), lambda i, j, k: (k, j)),
                       o_spec=_bs((D_MODEL, 1024), lambda i, j, k: (0, j)), contract=TN)
    token = last_grads(d_w_in_p, d_wq, d_wkv) if last_grads is not None else None
    dh1 = _matmul("proj_dx", dproj, w_in_p, out_shape=(s, D_MODEL), out_dtype=F32, grid=(s // 1024, 1, W_IN_PAD // 1024),
                  a_spec=_bs((1024, 1024), lambda i, j, k: (i, k)), b_spec=_bs((D_MODEL, 1024), lambda i, j, k: (0, k)),
                  o_spec=_bs((1024, D_MODEL), lambda i, j, k: (i, 0)), contract=NT, after=token)
    dx, d_norm1_g = _rmsnorm_bwd("norm1_bwd", dh1, x, rstd1, norm1_g, D_MODEL, 0, F32, res=dx1)

    grads = dict(
        norm1_g=d_norm1_g, w_in_p=d_w_in_p, q_a_norm_g=d_gq, wq=d_wq, kv_a_norm_g=d_gkv, wkv=d_wkv,
        rel_bias=d_rel_bias, sinks=d_sinks, w_out=d_w_out, norm2_g=d_norm2_g, w_up=d_w_up,
        conv_w=jnp.concatenate([d_conv_w2[0], d_conv_w2[1]], axis=1),
        conv_b=jnp.concatenate([d_conv_b2[0], d_conv_b2[1]], axis=1),
        w_down=d_w_down, final_norm_g=d_final_g.reshape(D_MODEL))
    return loss, dx, grads


HEADS_PER_SHARD = H_A // 4


def _head_cols(h, width):
    return slice((h % HEADS_PER_SHARD) * width, (h % HEADS_PER_SHARD + 1) * width)


def _wq_heads(shards):
    per = QK_NOPE + QK_ROPE
    w = jnp.stack([shards[h // HEADS_PER_SHARD][:, _head_cols(h, per)] for h in range(H_A)])
    return jnp.pad(w, ((0, 0), (0, 0), (0, HEAD_PAD - per)))


def _wq_shards(d_wq):
    per = QK_NOPE + QK_ROPE
    return jnp.stack([jnp.concatenate([d_wq[h][:, :per] for h in range(q * HEADS_PER_SHARD, (q + 1) * HEADS_PER_SHARD)],
                                      axis=1) for q in range(4)])


def _wkv_heads(shards):
    return jnp.stack([shards[h // HEADS_PER_SHARD][:, _head_cols(h, QK_NOPE + V_DIM)] for h in range(H_A)])


def _wkv_shards(d_wkv):
    return jnp.stack([jnp.concatenate([d_wkv[h] for h in range(q * HEADS_PER_SHARD, (q + 1) * HEADS_PER_SHARD)], axis=1)
                      for q in range(4)])


SMALL = ("norm1_g", "q_a_norm_g", "kv_a_norm_g", "rel_bias", "sinks", "norm2_g", "conv_b", "final_norm_g")
FIRST = ("w_in", "w_q_b", "w_kv_b")
LATER = ("w_out", "w_up", "w_down")
BIG = FIRST + LATER


def kernel(x, positions, norm1_g, w_in, q_a_norm_g, w_q_b, kv_a_norm_g, w_kv_b, rel_bias, sinks, w_out, norm2_g, w_up, conv_w, conv_b, w_down, final_norm_g, loss_target, m_norm1_g, m_w_in, m_q_a_norm_g, m_w_q_b, m_kv_a_norm_g, m_w_kv_b, m_rel_bias, m_sinks, m_w_out, m_norm2_g, m_w_up, m_conv_w, m_conv_b, m_w_down, m_final_norm_g, v_norm1_g, v_w_in, v_q_a_norm_g, v_w_q_b, v_kv_a_norm_g, v_w_kv_b, v_rel_bias, v_sinks, v_w_out, v_norm2_g, v_w_up, v_conv_w, v_conv_b, v_w_down, v_final_norm_g):
    weights = dict(norm1_g=norm1_g, w_in=w_in, q_a_norm_g=q_a_norm_g, w_q_b=w_q_b, kv_a_norm_g=kv_a_norm_g,
                   w_kv_b=w_kv_b, rel_bias=rel_bias, sinks=sinks, w_out=w_out, norm2_g=norm2_g, w_up=w_up,
                   conv_w=conv_w, conv_b=conv_b, w_down=w_down, final_norm_g=final_norm_g)
    mom_m = dict(norm1_g=m_norm1_g, w_in=m_w_in, q_a_norm_g=m_q_a_norm_g, w_q_b=m_w_q_b, kv_a_norm_g=m_kv_a_norm_g,
                 w_kv_b=m_w_kv_b, rel_bias=m_rel_bias, sinks=m_sinks, w_out=m_w_out, norm2_g=m_norm2_g, w_up=m_w_up,
                 conv_w=m_conv_w, conv_b=m_conv_b, w_down=m_w_down, final_norm_g=m_final_norm_g)
    mom_v = dict(norm1_g=v_norm1_g, w_in=v_w_in, q_a_norm_g=v_q_a_norm_g, w_q_b=v_w_q_b, kv_a_norm_g=v_kv_a_norm_g,
                 w_kv_b=v_w_kv_b, rel_bias=v_rel_bias, sinks=v_sinks, w_out=v_w_out, norm2_g=v_norm2_g, w_up=v_w_up,
                 conv_w=v_conv_w, conv_b=v_conv_b, w_down=v_w_down, final_norm_g=v_final_norm_g)
    shard2d = {n: weights[n][0] for n in BIG}
    conv_w_shard = conv_w[0]
    xi, yi, ci = lax.axis_index("x"), lax.axis_index("y"), lax.axis_index("c")
    chip = (2 * xi + yi).astype(jnp.int32)

    core = ci.astype(jnp.int32).reshape(1)
    chip1 = chip.reshape(1)
    cat_cols = lambda a: jnp.concatenate([a[0], a[1], a[2], a[3]], axis=1)
    own_slot = lambda a, own: lax.dynamic_update_index_in_dim(a, own, chip, 0)
    halved = lambda a: a.reshape((2, a.shape[0] // 2) + a.shape[1:])
    quartered = lambda a: a.reshape(4, 2, a.shape[1] // 2, a.shape[2])

    first = [halved(shard2d[n].astype(BF16)) for n in FIRST]
    gather1 = _SplitExchange("gather_first", first, [(4,) + a.shape for a in first],
                             src_of=lambda ref, k, p, pk, c: ref.at[c], dst_of=lambda ref, k, p, pk, c: ref.at[p, c],
                             arrive_of=lambda ref, k, p, pk, c: ref.at[pk, c])
    token1 = gather1.start()

    def first_weights(after):
        own, landed = gather1.wait(after)
        gathered = [own_slot(a, mine) for a, mine in zip(_forward_to_sibling(landed), own)]
        g = {n: a.reshape((4,) + shard2d[n].shape) for n, a in zip(FIRST, gathered)}
        return _w_in_from_shards(g["w_in"]), _wq_heads(g["w_q_b"]), _wkv_heads(g["w_kv_b"])

    later = [shard2d[n].astype(BF16) for n in LATER] + [conv_w_shard]
    gather2 = _SplitExchange("gather_later", later, [(4,) + a.shape for a in later],
                             src_of=lambda ref, k, p, pk, c: ref, dst_of=lambda ref, k, p, pk, c: ref.at[p],
                             arrive_of=lambda ref, k, p, pk, c: ref.at[pk])
    norm1_g_in = norm1_g + gather2.start(after=token1)[:1, :1]

    def late_weights(after):
        w_out_g, w_up_g, w_down_g, conv_w_g = [own_slot(a, mine) for mine, a in zip(*gather2.wait(after))]
        return w_out_g.reshape(D_MODEL, D_MODEL), cat_cols(w_up_g), w_down_g.reshape(D_FF, D_MODEL), cat_cols(conv_w_g)

    early = {}

    def early_grads(d_w_out, d_w_up, d_w_down):
        grads = [quartered(d_w_out.reshape(4, D_MODEL // 4, D_MODEL)), quartered(_col_shards(d_w_up)),
                 quartered(d_w_down.reshape(4, D_FF // 4, D_MODEL))]
        swap = _SplitExchange("rs_pair_exchange_early", grads, [(4,) + a.shape[2:] for a in grads], to_sibling=True,
                              src_of=lambda ref, k, p, pk, c: ref.at[:, pk], dst_of=lambda ref, k, p, pk, c: ref,
                              arrive_of=lambda ref, k, p, pk, c: ref)

        def on(after):
            kept, recv = swap.wait(after)
            early["pairs"] = [_rs_pair_add(f"rs_pair_add_{n}", core, gfull, r) for n, gfull, r in zip(LATER, kept, recv)]
            early["ici"] = _SplitExchange("rs_ici_early", early["pairs"], [(3,) + a.shape[1:] for a in early["pairs"]],
                                          src_of=lambda ref, k, p, pk, c: ref.at[pk], dst_of=lambda ref, k, p, pk, c: ref.at[k],
                                          arrive_of=lambda ref, k, p, pk, c: ref.at[k])
            return early["ici"].start()

        return swap.start(), on

    last = {}

    def last_grads(d_w_in_p, d_wq, d_wkv):
        grads = [quartered(_w_in_to_shards(d_w_in_p)), quartered(_wq_shards(d_wq)), quartered(_wkv_shards(d_wkv))]
        recv = _rs_pair_exchange("rs_pair_exchange_last", grads)
        last["pairs"] = [_rs_pair_add(f"rs_pair_add_{n}", core, gfull, r) for n, gfull, r in zip(FIRST, grads, recv)]
        last["ici"] = _SplitExchange("rs_ici_last", last["pairs"], [(3,) + a.shape[1:] for a in last["pairs"]],
                                     src_of=lambda ref, k, p, pk, c: ref.at[pk], dst_of=lambda ref, k, p, pk, c: ref.at[k],
                                     arrive_of=lambda ref, k, p, pk, c: ref.at[k])
        return last["ici"].start()

    loss, dx, gr = _local_step(x[0], positions, loss_target[0], norm1_g_in, first_weights, q_a_norm_g, kv_a_norm_g,
                               rel_bias, sinks, late_weights, norm2_g, conv_b, final_norm_g, early_grads, last_grads)

    last_pairs, last_recv = last["ici"].wait(dx)
    early_pairs, early_recv = early["ici"].wait(dx)
    pairs, recv2 = last_pairs + early_pairs, last_recv + early_recv
    halves = [_rs_final_add(f"rs_final_add_{n}", chip1, pr, r) for n, pr, r in zip(FIRST + LATER, pairs, recv2)]
    sibling_halves = _rs_pair_share(halves)

    as_rows = lambda a: a.reshape((-1, a.shape[-1]))
    summed = _small_allreduce([as_rows(gr[n]) for n in SMALL] + [gr["conv_w"], loss])
    small_g = dict(zip(SMALL, summed[:len(SMALL)]))
    conv_w_g = lax.dynamic_slice_in_dim(summed[len(SMALL)], chip * (2 * D_FF // 4), 2 * D_FF // 4, axis=1)
    loss_out = summed[-1].reshape(())

    out_g, out_d, out_m, out_v = {}, {}, {}, {}
    for n, mine, theirs in zip(FIRST + LATER, halves, sibling_halves):
        gsh, d, nm, nv = _adamw_halves(f"adamw_{n}", core, shard2d[n], mine, theirs, mom_m[n][0], mom_v[n][0])
        out_g[n], out_d[n], out_m[n], out_v[n] = gsh[None], d[None], nm[None], nv[None]
    names = SMALL + ("conv_w",)
    sg = [small_g[n] for n in SMALL] + [conv_w_g]
    ds, nms, nvs = _adamw_small([as_rows(weights[n]) for n in names], sg, [as_rows(mom_m[n]) for n in names],
                                [as_rows(mom_v[n]) for n in names])
    for n, gg, dd, mm, vv in zip(names, sg, ds, nms, nvs):
        shp = weights[n].shape
        out_g[n], out_d[n], out_m[n], out_v[n] = gg.reshape(shp), dd.reshape(shp), mm.reshape(shp), vv.reshape(shp)

    order = ("norm1_g", "w_in", "q_a_norm_g", "w_q_b", "kv_a_norm_g", "w_kv_b", "rel_bias", "sinks", "w_out",
             "norm2_g", "w_up", "conv_w", "conv_b", "w_down", "final_norm_g")
    return (loss_out, dx[None], *[out_g[n] for n in order], *[out_d[n] for n in order],
            *[out_m[n] for n in order], *[out_v[n] for n in order])
```

```python
import functools
import math

import jax
import jax.numpy as jnp
import numpy as np
from jax import lax
from jax.experimental import pallas as pl
from jax.experimental.pallas import tpu as pltpu

F32 = jnp.float32
BF16 = jnp.bfloat16
MESH = pl.DeviceIdType.MESH

D_MODEL = 1024
EPS = 1e-6
H_A = 8
QK_NOPE = 128
QK_ROPE = 64
V_DIM = 128
Q_LORA = 256
KV_LORA = 128
ROPE_THETA = 10000.0
H_B = 16
KV_B = 4
GROUP = 4
HD_B = 64
WINDOW = 128
Q_BLOCK = 128
NUM_BUCKETS = 32
MAX_DISTANCE = 128
D_FF = 2816
HEAD_PAD = 256

ADAM_LR = 0.001
ADAM_B1 = 0.9
ADAM_B2 = 0.999
ADAM_EPS = 1e-08
ADAM_WD = 0.01
ADAM_STEP = 10

LANES = 128
P_QB, P_GA, P_GB, P_QLAT, P_KB, P_VB, P_CKV, P_KR = 0, 1024, 2048, 3072, 3328, 3584, 3840, 3968
W_IN_PAD = 4096

NT = (((1,), (1,)), ((), ()))
NN = (((1,), (0,)), ((), ()))
TN = (((0,), (0,)), ((), ()))


def _arb(n):
    return pltpu.CompilerParams(dimension_semantics=("arbitrary",) * n)


def _matmul(name, a, b, *, out_shape, out_dtype, grid, a_spec, b_spec, o_spec, contract, add=None, bf16_copy=False,
            after=None):
    nk = grid[2]
    acc_shape = tuple(d for d in o_spec.block_shape if d is not None)
    n_in = 2 + (add is not None) + (after is not None)
    n_out = 2 if bf16_copy else 1

    def body(*refs):
        a_ref, b_ref = refs[:2]
        add_ref = refs[2] if add is not None else None
        o_refs = refs[n_in:n_in + n_out]
        scratch = refs[n_in + n_out:]
        prod = lax.dot_general(a_ref[...].astype(BF16), b_ref[...].astype(BF16), contract,
                               preferred_element_type=F32)

        def finish(val):
            if add_ref is not None:
                val = add_ref[...] + val
            o_refs[0][...] = val.astype(out_dtype)
            if bf16_copy:
                o_refs[1][...] = val.astype(BF16)

        if nk == 1:
            finish(prod)
        else:
            acc_ref = scratch[0]
            k = pl.program_id(2)

            @pl.when(k == 0)
            def _():
                acc_ref[...] = prod

            @pl.when((k > 0) & (k < nk - 1))
            def _():
                acc_ref[...] += prod

            @pl.when(k == nk - 1)
            def _():
                finish(acc_ref[...] + prod)

    in_specs = [a_spec, b_spec]
    args = [a, b]
    if add is not None:
        in_specs.append(o_spec)
        args.append(add)
    if after is not None:
        in_specs.append(pl.BlockSpec(memory_space=pl.ANY))
        args.append(after)
    out_shapes = [jax.ShapeDtypeStruct(out_shape, out_dtype)]
    if bf16_copy:
        out_shapes.append(jax.ShapeDtypeStruct(out_shape, BF16))
    res = pl.pallas_call(
        body, name=name, grid=grid, in_specs=in_specs, out_specs=[o_spec] * n_out, out_shape=out_shapes,
        scratch_shapes=[pltpu.VMEM(acc_shape, F32)] if nk > 1 else [],
        compiler_params=_arb(3),
    )(*args)
    return res if bf16_copy else res[0]


def _bs(block, fn):
    return pl.BlockSpec(block, fn)


def _rmsnorm_fwd(name, src, g, d, cb, ts=512):
    s = src.shape[0]

    def body(x_ref, g_ref, h_ref, r_ref):
        x = x_ref[...]
        r = lax.rsqrt(jnp.mean(x * x, axis=-1, keepdims=True) + EPS)
        h_ref[...] = (x * r * g_ref[...]).astype(BF16)
        r_ref[...] = r

    return pl.pallas_call(
        body, name=name, grid=(s // ts,),
        in_specs=[_bs((ts, d), lambda i: (i, cb)), _bs((1, d), lambda i: (0, 0))],
        out_specs=[_bs((ts, d), lambda i: (i, 0)), _bs((ts, 1), lambda i: (i, 0))],
        out_shape=[jax.ShapeDtypeStruct((s, d), BF16), jax.ShapeDtypeStruct((s, 1), F32)],
        compiler_params=_arb(1),
    )(src, g)


def _rmsnorm_bwd(name, dy, src, rstd, g, d, cb, out_dtype, res=None, bf16_copy=False, ts=512):
    s = src.shape[0]

    def body(*refs):
        dy_ref, x_ref, r_ref, g_ref = refs[:4]
        res_ref = refs[4] if res is not None else None
        dx_ref, dg_ref = refs[n_in:n_in + 2]
        dyv = dy_ref[...]
        r = r_ref[...]
        xhat = x_ref[...] * r
        dyh = dyv * g_ref[...]
        c = jnp.mean(dyh * xhat, axis=-1, keepdims=True)
        dx = r * (dyh - xhat * c)
        if res_ref is not None:
            dx = res_ref[...] + dx
        dx_ref[...] = dx.astype(out_dtype)
        if bf16_copy:
            refs[n_in + 2][...] = dx.astype(BF16)
        part = jnp.sum(dyv * xhat, axis=0, keepdims=True)

        @pl.when(pl.program_id(0) == 0)
        def _():
            dg_ref[...] = part

        @pl.when(pl.program_id(0) > 0)
        def _():
            dg_ref[...] += part

    in_specs = [_bs((ts, d), lambda i: (i, 0)), _bs((ts, d), lambda i: (i, cb)),
                _bs((ts, 1), lambda i: (i, 0)), _bs((1, d), lambda i: (0, 0))]
    args = [dy, src, rstd, g]
    if res is not None:
        in_specs.append(_bs((ts, d), lambda i: (i, 0)))
        args.append(res)
    n_in = len(args)
    out_specs = [_bs((ts, d), lambda i: (i, 0)), _bs((1, d), lambda i: (0, 0))]
    out_shape = [jax.ShapeDtypeStruct((s, d), out_dtype), jax.ShapeDtypeStruct((1, d), F32)]
    if bf16_copy:
        out_specs.append(_bs((ts, d), lambda i: (i, 0)))
        out_shape.append(jax.ShapeDtypeStruct((s, d), BF16))
    return pl.pallas_call(
        body, name=name, grid=(s // ts,), in_specs=in_specs, out_specs=out_specs, out_shape=out_shape,
        compiler_params=_arb(1),
    )(*args)


def _final_loss(x2, target, g, ts=512):
    s, d = x2.shape

    def body(x_ref, t_ref, g_ref, loss_ref, dx_ref, dg_ref, dxb_ref):
        x = x_ref[...]
        r = lax.rsqrt(jnp.mean(x * x, axis=-1, keepdims=True) + EPS)
        xhat = x * r
        gv = g_ref[...]
        err = xhat * gv - t_ref[...]
        lpart = 0.5 * jnp.sum(jnp.mean(err * err, axis=-1, keepdims=True), axis=0, keepdims=True)
        dyv = err * (1.0 / d)
        dyh = dyv * gv
        c = jnp.mean(dyh * xhat, axis=-1, keepdims=True)
        dx = r * (dyh - xhat * c)
        dx_ref[...] = dx
        dxb_ref[...] = dx.astype(BF16)
        gpart = jnp.sum(dyv * xhat, axis=0, keepdims=True)

        @pl.when(pl.program_id(0) == 0)
        def _():
            dg_ref[...] = gpart
            loss_ref[...] = lpart

        @pl.when(pl.program_id(0) > 0)
        def _():
            dg_ref[...] += gpart
            loss_ref[...] += lpart

    return pl.pallas_call(
        body, name="final_loss", grid=(s // ts,),
        in_specs=[_bs((ts, d), lambda i: (i, 0)), _bs((ts, d), lambda i: (i, 0)), _bs((1, d), lambda i: (0, 0))],
        out_specs=[_bs((1, 1), lambda i: (0, 0)), _bs((ts, d), lambda i: (i, 0)), _bs((1, d), lambda i: (0, 0)),
                   _bs((ts, d), lambda i: (i, 0))],
        out_shape=[jax.ShapeDtypeStruct((1, 1), F32), jax.ShapeDtypeStruct((s, d), F32),
                   jax.ShapeDtypeStruct((1, d), F32), jax.ShapeDtypeStruct((s, d), BF16)],
        compiler_params=_arb(1),
    )(x2, target, g)


def _swap_halves(t):
    lane = lax.broadcasted_iota(jnp.int32, t.shape, 1)
    return jnp.where(lane < 32, pltpu.roll(t, 96, 1), pltpu.roll(t, 32, 1))


def _rope_fwd(t, cos_t, sin_t):
    return t * cos_t + _swap_halves(t) * sin_t


def _rope_bwd(dt, cos_t, sin_t):
    return dt * cos_t - _swap_halves(dt) * sin_t


def _lat_norms(proj, gq, gkv, ts=512):
    s = proj.shape[0]

    def body(q_ref, c_ref, gq_ref, gkv_ref, qn_ref, cn_ref, rq_ref, rc_ref):
        q = q_ref[...]
        rq = lax.rsqrt(jnp.mean(q * q, axis=-1, keepdims=True) + EPS)
        qn_ref[...] = (q * rq * gq_ref[...]).astype(BF16)
        rq_ref[...] = rq
        cv = c_ref[...]
        rc = lax.rsqrt(jnp.mean(cv * cv, axis=-1, keepdims=True) + EPS)
        cn_ref[...] = (cv * rc * gkv_ref[...]).astype(BF16)
        rc_ref[...] = rc

    return pl.pallas_call(
        body, name="lat_norms", grid=(s // ts,),
        in_specs=[_bs((ts, Q_LORA), lambda i: (i, P_QLAT // Q_LORA)),
                  _bs((ts, KV_LORA), lambda i: (i, P_CKV // KV_LORA)),
                  _bs((1, Q_LORA), lambda i: (0, 0)), _bs((1, KV_LORA), lambda i: (0, 0))],
        out_specs=[_bs((ts, Q_LORA), lambda i: (i, 0)), _bs((ts, KV_LORA), lambda i: (i, 0)),
                   _bs((ts, 1), lambda i: (i, 0)), _bs((ts, 1), lambda i: (i, 0))],
        out_shape=[jax.ShapeDtypeStruct((s, Q_LORA), BF16), jax.ShapeDtypeStruct((s, KV_LORA), BF16),
                   jax.ShapeDtypeStruct((s, 1), F32), jax.ShapeDtypeStruct((s, 1), F32)],
        compiler_params=_arb(1),
    )(proj, proj, gq, gkv)


HEAD_ROWS = 2048
DW_ROWS = 2048
MM_ROWS = 1024


def _q_heads(qn, wq, cos_t, sin_t):
    s = qn.shape[0]
    ts = min(s, HEAD_ROWS)

    def body(qn_ref, w_ref, cos_ref, sin_ref, q_ref):
        o = jnp.dot(qn_ref[...], w_ref[...], preferred_element_type=F32)
        q_ref[:, :LANES] = o[:, :LANES].astype(BF16)
        q_ref[:, LANES:] = _rope_fwd(o[:, LANES:], cos_ref[...], sin_ref[...]).astype(BF16)

    return pl.pallas_call(
        body, name="q_heads", grid=(s // ts, H_A),
        in_specs=[_bs((ts, Q_LORA), lambda i, h: (i, 0)), _bs((None, Q_LORA, HEAD_PAD), lambda i, h: (h, 0, 0)),
                  _bs((ts, LANES), lambda i, h: (i, 0)), _bs((ts, LANES), lambda i, h: (i, 0))],
        out_specs=_bs((None, ts, HEAD_PAD), lambda i, h: (h, i, 0)),
        out_shape=jax.ShapeDtypeStruct((H_A, s, HEAD_PAD), BF16),
        compiler_params=_arb(2),
    )(qn, wq, cos_t, sin_t)


def _kv_heads(cn, wkv, proj, cos_t, sin_t):
    s = cn.shape[0]
    ts = min(s, HEAD_ROWS)

    def body(cn_ref, w_ref, kr_ref, cos_ref, sin_ref, k_ref, v_ref):
        o = jnp.dot(cn_ref[...], w_ref[...], preferred_element_type=F32)
        k_ref[:, :LANES] = o[:, :LANES].astype(BF16)
        k_ref[:, LANES:] = _rope_fwd(kr_ref[...], cos_ref[...], sin_ref[...]).astype(BF16)
        v_ref[...] = o[:, LANES:].astype(BF16)

    return pl.pallas_call(
        body, name="kv_heads", grid=(s // ts, H_A),
        in_specs=[_bs((ts, KV_LORA), lambda i, h: (i, 0)),
                  _bs((None, KV_LORA, QK_NOPE + V_DIM), lambda i, h: (h, 0, 0)),
                  _bs((ts, LANES), lambda i, h: (i, P_KR // LANES)),
                  _bs((ts, LANES), lambda i, h: (i, 0)), _bs((ts, LANES), lambda i, h: (i, 0))],
        out_specs=[_bs((None, ts, HEAD_PAD), lambda i, h: (h, i, 0)), _bs((None, ts, V_DIM), lambda i, h: (h, i, 0))],
        out_shape=[jax.ShapeDtypeStruct((H_A, s, HEAD_PAD), BF16), jax.ShapeDtypeStruct((H_A, s, V_DIM), BF16)],
        compiler_params=_arb(2),
    )(cn, wkv, proj, cos_t, sin_t)


MLA_SCALE = 1.0 / math.sqrt(QK_NOPE + QK_ROPE)
LOG2E = math.log2(math.e)
MLA_EXP2_SCALE = MLA_SCALE * LOG2E


def _lane_tiles(a):
    return [a[:, j * LANES:(j + 1) * LANES] for j in range(a.shape[1] // LANES)]


MLA_SUB = 512


def _mla_fwd(q, k, v, tq=512, tk=1024):
    s = q.shape[1]
    tq = min(tq, s)
    nk = s // tk

    def body(q_ref, k_ref, v_ref, o_ref, lse_ref, m_ref, l_ref, acc_ref):
        m_ref[...] = jnp.full(m_ref.shape, -jnp.inf, F32)
        l_ref[...] = jnp.zeros(l_ref.shape, F32)
        acc_ref[...] = jnp.zeros(acc_ref.shape, F32)

        def step(c, carry):
            rows = pl.ds(pl.multiple_of(c * tk, tk), tk)
            for sub in range(tq // MLA_SUB):
                qr = slice(sub * MLA_SUB, (sub + 1) * MLA_SUB)
                raw = lax.dot_general(q_ref[qr, :], k_ref[rows, :], NT, preferred_element_type=F32)
                m_prev = m_ref[qr, :]
                m_new = jnp.maximum(m_prev, jnp.max(raw, axis=-1, keepdims=True))
                alpha = jnp.exp2((m_prev - m_new) * MLA_EXP2_SCALE)
                ps = [jnp.exp2((t - m_new) * MLA_EXP2_SCALE) for t in _lane_tiles(raw)]
                l_ref[qr, :] = alpha * l_ref[qr, :] + functools.reduce(lambda a, b: a + b, ps)
                p = jnp.concatenate(ps, axis=1).astype(BF16)
                acc_ref[qr, :] = alpha * acc_ref[qr, :] + jnp.dot(p, v_ref[rows, :], preferred_element_type=F32)
                m_ref[qr, :] = m_new
            return carry

        lax.fori_loop(0, nk, step, 0, unroll=True)
        l = jnp.sum(l_ref[...], axis=-1, keepdims=True)
        o_ref[...] = acc_ref[...] / l
        lse_ref[...] = m_ref[...] * MLA_SCALE + jnp.log(l)

    return pl.pallas_call(
        body, name="mla_fwd", grid=(H_A, s // tq),
        in_specs=[_bs((None, tq, HEAD_PAD), lambda h, i: (h, i, 0)),
                  _bs((None, s, HEAD_PAD), lambda h, i: (h, 0, 0)),
                  _bs((None, s, V_DIM), lambda h, i: (h, 0, 0))],
        out_specs=[_bs((tq, V_DIM), lambda h, i: (i, h)), _bs((None, tq, LANES), lambda h, i: (h, i, 0))],
        out_shape=[jax.ShapeDtypeStruct((s, H_A * V_DIM), F32), jax.ShapeDtypeStruct((H_A, s, LANES), F32)],
        scratch_shapes=[pltpu.VMEM((tq, LANES), F32), pltpu.VMEM((tq, LANES), F32), pltpu.VMEM((tq, V_DIM), F32)],
        compiler_params=_arb(2),
    )(q, k, v)


def _mla_bwd(q, k, v, do, o, lse, cos_t, sin_t, tq=512, tk=512):
    s = q.shape[1]
    nq = s // tq
    nkb = s // tk

    def body(q_ref, k_ref, v_ref, do_ref, o_ref, lse_ref, cos_ref, sin_ref, dqp_ref, dkvp_ref, dkr_ref,
             delta_ref, dq_ref, dk_ref, dv_ref):
        @pl.when(pl.program_id(1) == 0)
        def _():
            def init(c, carry):
                rows = pl.ds(pl.multiple_of(c * tq, tq), tq)
                delta = jnp.sum(do_ref[rows, :] * o_ref[rows, :], axis=-1, keepdims=True)
                delta_ref[rows, :] = jnp.broadcast_to(delta, (tq, LANES))
                dq_ref[rows, :] = jnp.zeros((tq, HEAD_PAD), F32)
                return carry

            lax.fori_loop(0, nq, init, 0)

        dk_ref[...] = jnp.zeros(dk_ref.shape, F32)
        dv_ref[...] = jnp.zeros(dv_ref.shape, F32)
        kb = k_ref[...]
        vb = v_ref[...]

        def step(c, carry):
            rows = pl.ds(pl.multiple_of(c * tq, tq), tq)
            qc = q_ref[rows, :]
            doc = do_ref[rows, :].astype(BF16)
            raw = lax.dot_general(qc, kb, NT, preferred_element_type=F32)
            dp = lax.dot_general(doc, vb, NT, preferred_element_type=F32)
            lse2 = lse_ref[rows, :] * LOG2E
            delta = delta_ref[rows, :]
            ps = [jnp.exp2(t * MLA_EXP2_SCALE - lse2) for t in _lane_tiles(raw)]
            dss = [pj * (dj - delta) * MLA_SCALE for pj, dj in zip(ps, _lane_tiles(dp))]
            p = jnp.concatenate(ps, axis=1).astype(BF16)
            ds = jnp.concatenate(dss, axis=1).astype(BF16)
            dv_ref[...] += lax.dot_general(p, doc, TN, preferred_element_type=F32)
            dk_ref[...] += lax.dot_general(ds, qc, TN, preferred_element_type=F32)
            dq_ref[rows, :] += jnp.dot(ds, kb, preferred_element_type=F32)
            return carry

        lax.fori_loop(0, nq, step, 0, unroll=True)
        dkvp_ref[:, :LANES] = dk_ref[:, :LANES].astype(BF16)
        dkvp_ref[:, LANES:] = dv_ref[...].astype(BF16)
        dkr_ref[...] = dk_ref[:, LANES:]

        @pl.when(pl.program_id(1) == nkb - 1)
        def _():
            def finish(c, carry):
                rows = pl.ds(pl.multiple_of(c * tq, tq), tq)
                dqp_ref[rows, :LANES] = dq_ref[rows, :LANES].astype(BF16)
                dqp_ref[rows, LANES:] = _rope_bwd(dq_ref[rows, LANES:], cos_ref[rows, :], sin_ref[rows, :]).astype(BF16)
                return carry

            lax.fori_loop(0, nq, finish, 0)

    whole = lambda w: _bs((s, w), lambda h, j: (0, 0))
    return pl.pallas_call(
        body, name="mla_bwd", grid=(H_A, nkb),
        in_specs=[_bs((None, s, HEAD_PAD), lambda h, j: (h, 0, 0)),
                  _bs((None, tk, HEAD_PAD), lambda h, j: (h, j, 0)),
                  _bs((None, tk, V_DIM), lambda h, j: (h, j, 0)),
                  _bs((s, V_DIM), lambda h, j: (0, h)), _bs((s, V_DIM), lambda h, j: (0, h)),
                  _bs((None, s, LANES), lambda h, j: (h, 0, 0)), whole(LANES), whole(LANES)],
        out_specs=[_bs((None, s, HEAD_PAD), lambda h, j: (h, 0, 0)),
                   _bs((None, tk, HEAD_PAD), lambda h, j: (h, j, 0)),
                   _bs((None, tk, LANES), lambda h, j: (h, j, 0))],
        out_shape=[jax.ShapeDtypeStruct((H_A, s, HEAD_PAD), BF16), jax.ShapeDtypeStruct((H_A, s, HEAD_PAD), BF16),
                   jax.ShapeDtypeStruct((H_A, s, LANES), F32)],
        scratch_shapes=[pltpu.VMEM((s, LANES), F32), pltpu.VMEM((s, HEAD_PAD), F32), pltpu.VMEM((tk, HEAD_PAD), F32),
                        pltpu.VMEM((tk, V_DIM), F32)],
        compiler_params=_arb(2),
    )(q, k, v, do, o, lse, cos_t, sin_t)


def _mla_key_rope_grad(dkr, cos_t, sin_t, ts=512):
    s = dkr.shape[1]

    def body(d_ref, cos_ref, sin_ref, o_ref):
        tot = d_ref[0]
        for h in range(1, H_A):
            tot = tot + d_ref[h]
        o_ref[...] = _rope_bwd(tot, cos_ref[...], sin_ref[...]).astype(BF16)

    rows = _bs((ts, LANES), lambda i: (i, 0))
    return pl.pallas_call(
        body, name="mla_key_rope_grad", grid=(s // ts,),
        in_specs=[_bs((H_A, ts, LANES), lambda i: (0, i, 0)), rows, rows], out_specs=rows,
        out_shape=jax.ShapeDtypeStruct((s, LANES), BF16), compiler_params=_arb(1),
    )(dkr, cos_t, sin_t)


WIN_SCALE = 1.0 / math.sqrt(HD_B)
SPAN = Q_BLOCK + 2 * WINDOW


def _t5_bucket_table():
    a = jnp.arange(Q_BLOCK, dtype=jnp.int32)[:, None]
    c = jnp.arange(SPAN, dtype=jnp.int32)[None, :]
    rel = c - WINDOW - a
    nb = NUM_BUCKETS // 2
    max_exact = nb // 2
    base = (rel > 0).astype(jnp.int32) * nb
    n = jnp.abs(rel)
    nf = jnp.maximum(n, 1).astype(F32)
    large = max_exact + (jnp.log(nf / max_exact) / math.log(MAX_DISTANCE / max_exact)
                         * (nb - max_exact)).astype(jnp.int32)
    large = jnp.minimum(large, nb - 1)
    return base + jnp.where(n < max_exact, n, large)


def _win_bias(bucket, rel_bias):
    def body(rb_ref, bk_ref, o_ref):
        h = pl.program_id(0)
        bk = bk_ref[...]
        acc = jnp.zeros((Q_BLOCK, SPAN), F32)
        for b in range(NUM_BUCKETS):
            acc = jnp.where(bk == b, rb_ref[b, h], acc)
        o_ref[...] = acc

    return pl.pallas_call(
        body, name="win_bias", grid=(H_B,),
        in_specs=[pl.BlockSpec(memory_space=pltpu.SMEM), _bs((Q_BLOCK, SPAN), lambda h: (0, 0))],
        out_specs=_bs((None, Q_BLOCK, SPAN), lambda h: (h, 0, 0)),
        out_shape=jax.ShapeDtypeStruct((H_B, Q_BLOCK, SPAN), F32),
        compiler_params=_arb(1),
    )(rel_bias, bucket)


GROUP_W = GROUP * HD_B


def _win_kv_rows(n, j, nblk):
    blk = jnp.clip(n + j - 1, 0, nblk - 1)
    return pl.ds(pl.multiple_of(blk * Q_BLOCK, Q_BLOCK), Q_BLOCK)


def _win_head_cols(kv):
    return slice(kv * HD_B, (kv + 1) * HD_B)


def _win_stack(ref, kv):
    return jnp.concatenate([ref[:, kv * GROUP_W + g * HD_B:kv * GROUP_W + (g + 1) * HD_B] for g in range(GROUP)], axis=0)


def _win_unstack(ref, kv, val):
    for g in range(GROUP):
        ref[:, kv * GROUP_W + g * HD_B:kv * GROUP_W + (g + 1) * HD_B] = val[g * Q_BLOCK:(g + 1) * Q_BLOCK].astype(ref.dtype)


def _win_scores(q, k_ref, kv, bias_ref, n, nblk):
    a = lax.broadcasted_iota(jnp.int32, (GROUP, Q_BLOCK, Q_BLOCK), 1)
    cc = lax.broadcasted_iota(jnp.int32, (GROUP, Q_BLOCK, Q_BLOCK), 2)
    valid = [(cc >= a) & (n > 0), None, (cc <= a) & (n < nblk - 1)]
    out = []
    for j in range(3):
        sc = lax.dot_general(q, k_ref[_win_kv_rows(n, j, nblk), _win_head_cols(kv)], NT, preferred_element_type=F32)
        sc = (sc.reshape(GROUP, Q_BLOCK, Q_BLOCK) * WIN_SCALE
              + bias_ref[kv * GROUP:(kv + 1) * GROUP, :, j * Q_BLOCK:(j + 1) * Q_BLOCK])
        if valid[j] is not None:
            sc = jnp.where(valid[j], sc, -1e30)
        out.append(sc)
    return out


def _win_sink(sink_ref, kv):
    hs = lax.broadcasted_iota(jnp.int32, (GROUP, Q_BLOCK, 1), 0)
    sk = jnp.zeros((GROUP, Q_BLOCK, 1), F32)
    for g in range(GROUP):
        sk = jnp.where(hs == g, sink_ref[kv * GROUP + g], sk)
    return sk


def _win_fwd(proj_b, bias, sinks):
    s = proj_b.shape[0]
    nblk = s // Q_BLOCK
    rows = GROUP * Q_BLOCK

    def body(sink_ref, q_ref, k_ref, v_ref, bias_ref, o_ref, lse_ref):
        n = pl.program_id(0)
        for kv in range(KV_B):
            sk = _win_sink(sink_ref, kv)
            q = _win_stack(q_ref, kv)
            ss = _win_scores(q, k_ref, kv, bias_ref, n, nblk)
            m = jnp.maximum(jnp.max(jnp.maximum(jnp.maximum(ss[0], ss[1]), ss[2]), axis=2, keepdims=True), sk)
            es = [jnp.exp(sc - m) for sc in ss]
            l = jnp.sum(es[0] + es[1] + es[2], axis=2, keepdims=True) + jnp.exp(sk - m)
            acc = jnp.zeros((rows, HD_B), F32)
            for j, e in enumerate(es):
                p = (e / l).astype(BF16).reshape(rows, Q_BLOCK)
                acc = acc + jnp.dot(p, v_ref[_win_kv_rows(n, j, nblk), _win_head_cols(kv)],
                                    preferred_element_type=F32)
            _win_unstack(o_ref, kv, acc)
            lse_ref[kv * GROUP:(kv + 1) * GROUP] = m + jnp.log(l)

    kv_w = KV_B * HD_B
    return pl.pallas_call(
        body, name="win_fwd", grid=(nblk,),
        in_specs=[pl.BlockSpec(memory_space=pltpu.SMEM), _bs((Q_BLOCK, H_B * HD_B), lambda n: (n, P_QB // (H_B * HD_B))),
                  _bs((s, kv_w), lambda n: (0, P_KB // kv_w)), _bs((s, kv_w), lambda n: (0, P_VB // kv_w)),
                  _bs((H_B, Q_BLOCK, SPAN), lambda n: (0, 0, 0))],
        out_specs=[_bs((Q_BLOCK, H_B * HD_B), lambda n: (n, 0)), _bs((H_B, Q_BLOCK, 1), lambda n: (0, n, 0))],
        out_shape=[jax.ShapeDtypeStruct((s, H_B * HD_B), F32), jax.ShapeDtypeStruct((H_B, s, 1), F32)],
        compiler_params=_arb(1),
    )(sinks, proj_b, proj_b, proj_b, bias)


def _win_bwd(proj_b, bias, sinks, do_b, lse):
    s = proj_b.shape[0]
    nblk = s // Q_BLOCK
    rows = GROUP * Q_BLOCK
    spad = s + 2 * WINDOW

    def body(sink_ref, q_ref, k_ref, v_ref, bias_ref, do_ref, lse_ref, dq_ref, dk_ref, dv_ref, db_ref, dsk_ref):
        n = pl.program_id(0)

        @pl.when(n == 0)
        def _():
            dk_ref[...] = jnp.zeros(dk_ref.shape, F32)
            dv_ref[...] = jnp.zeros(dv_ref.shape, F32)
            db_ref[...] = jnp.zeros(db_ref.shape, F32)
            dsk_ref[...] = jnp.zeros(dsk_ref.shape, F32)

        for kv in range(KV_B):
            heads = slice(kv * GROUP, (kv + 1) * GROUP)
            sk = _win_sink(sink_ref, kv)
            q = _win_stack(q_ref, kv)
            dob = _win_stack(do_ref, kv)
            lse_v = lse_ref[heads]
            ss = _win_scores(q, k_ref, kv, bias_ref, n, nblk)
            ps = [jnp.exp(sc - lse_v) for sc in ss]
            dps = [lax.dot_general(dob, v_ref[_win_kv_rows(n, j, nblk), _win_head_cols(kv)], NT,
                                   preferred_element_type=F32).reshape(GROUP, Q_BLOCK, Q_BLOCK) for j in range(3)]
            delta = jnp.sum(ps[0] * dps[0] + ps[1] * dps[1] + ps[2] * dps[2], axis=2, keepdims=True)
            dq = jnp.zeros((rows, HD_B), F32)
            for j in range(3):
                ds = ps[j] * (dps[j] - delta)
                db_ref[heads, :, j * Q_BLOCK:(j + 1) * Q_BLOCK] += ds
                dsb = (ds * WIN_SCALE).astype(BF16).reshape(rows, Q_BLOCK)
                dq = dq + jnp.dot(dsb, k_ref[_win_kv_rows(n, j, nblk), _win_head_cols(kv)],
                                  preferred_element_type=F32)
                krows = pl.ds(pl.multiple_of((n + j) * Q_BLOCK, Q_BLOCK), Q_BLOCK)
                dk_ref[krows, _win_head_cols(kv)] += lax.dot_general(dsb, q, TN, preferred_element_type=F32)
                dv_ref[krows, _win_head_cols(kv)] += lax.dot_general(
                    ps[j].astype(BF16).reshape(rows, Q_BLOCK), dob, TN, preferred_element_type=F32)
            dsk_ref[heads] += -(jnp.exp(sk - lse_v) * delta)
            _win_unstack(dq_ref, kv, dq)

    kv_w = KV_B * HD_B
    qspec = _bs((Q_BLOCK, H_B * HD_B), lambda n: (n, 0))
    kacc = _bs((spad, kv_w), lambda n: (0, 0))
    return pl.pallas_call(
        body, name="win_bwd", grid=(nblk,),
        in_specs=[pl.BlockSpec(memory_space=pltpu.SMEM), _bs((Q_BLOCK, H_B * HD_B), lambda n: (n, P_QB // (H_B * HD_B))),
                  _bs((s, kv_w), lambda n: (0, P_KB // kv_w)), _bs((s, kv_w), lambda n: (0, P_VB // kv_w)),
                  _bs((H_B, Q_BLOCK, SPAN), lambda n: (0, 0, 0)), qspec, _bs((H_B, Q_BLOCK, 1), lambda n: (0, n, 0))],
        out_specs=[qspec, kacc, kacc, _bs((H_B, Q_BLOCK, SPAN), lambda n: (0, 0, 0)),
                   _bs((H_B, Q_BLOCK, 1), lambda n: (0, 0, 0))],
        out_shape=[jax.ShapeDtypeStruct((s, H_B * HD_B), BF16), jax.ShapeDtypeStruct((spad, kv_w), F32),
                   jax.ShapeDtypeStruct((spad, kv_w), F32), jax.ShapeDtypeStruct((H_B, Q_BLOCK, SPAN), F32),
                   jax.ShapeDtypeStruct((H_B, Q_BLOCK, 1), F32)],
        compiler_params=_arb(1),
    )(sinks, proj_b, proj_b, proj_b, bias, do_b, lse)


def _win_param_grads(bucket, dbias, dsink_rows):
    def body(bk_ref, db_ref, ds_ref, o_ref):
        bk = bk_ref[...]
        dbv = db_ref[...]
        lane = lax.broadcasted_iota(jnp.int32, (1, LANES), 1)
        res = jnp.zeros((1, LANES), F32)
        for b in range(NUM_BUCKETS):
            tot = jnp.sum(jnp.sum(jnp.where(bk == b, dbv, 0.0), axis=1, keepdims=True), axis=0, keepdims=True)
            res = jnp.where(lane == b, tot, res)
        stot = jnp.sum(ds_ref[...], axis=0, keepdims=True)
        o_ref[...] = jnp.where(lane == NUM_BUCKETS, stot, res)

    return pl.pallas_call(
        body, name="win_param_grads", grid=(H_B,),
        in_specs=[_bs((Q_BLOCK, SPAN), lambda h: (0, 0)), _bs((None, Q_BLOCK, SPAN), lambda h: (h, 0, 0)),
                  _bs((None, Q_BLOCK, 1), lambda h: (h, 0, 0))],
        out_specs=_bs((None, 1, LANES), lambda h: (h, 0, 0)),
        out_shape=jax.ShapeDtypeStruct((H_B, 1, LANES), F32),
        compiler_params=_arb(1),
    )(bucket, dbias, dsink_rows)


def _gate_fwd(proj, o_a, o_b, ts=256):
    s = o_a.shape[0]
    wide = lambda cb: _bs((ts, D_MODEL), lambda i: (i, cb))

    def body(ga_ref, gb_ref, oa_ref, ob_ref, m_ref):
        m_ref[...] = (jax.nn.sigmoid(ga_ref[...]) * oa_ref[...]
                      + jax.nn.sigmoid(gb_ref[...]) * ob_ref[...]).astype(BF16)

    return pl.pallas_call(
        body, name="gate_fwd", grid=(s // ts,),
        in_specs=[wide(P_GA // D_MODEL), wide(P_GB // D_MODEL), wide(0), wide(0)],
        out_specs=wide(0), out_shape=jax.ShapeDtypeStruct((s, D_MODEL), BF16),
        compiler_params=_arb(1),
    )(proj, proj, o_a, o_b)


def _gate_bwd(dmixed, proj, o_a, o_b, ts=256):
    s = o_a.shape[0]
    wide = lambda cb: _bs((ts, D_MODEL), lambda i: (i, cb))

    def body(dm_ref, ga_ref, gb_ref, oa_ref, ob_ref, doa_ref, dob_ref, dga_ref, dgb_ref):
        dm = dm_ref[...]
        sa = jax.nn.sigmoid(ga_ref[...])
        sb = jax.nn.sigmoid(gb_ref[...])
        doa_ref[...] = dm * sa
        dob_ref[...] = (dm * sb).astype(BF16)
        dga_ref[...] = (dm * oa_ref[...] * (sa * (1.0 - sa))).astype(BF16)
        dgb_ref[...] = (dm * ob_ref[...] * (sb * (1.0 - sb))).astype(BF16)

    return pl.pallas_call(
        body, name="gate_bwd", grid=(s // ts,),
        in_specs=[wide(0), wide(P_GA // D_MODEL), wide(P_GB // D_MODEL), wide(0), wide(0)],
        out_specs=[wide(0)] * 4,
        out_shape=[jax.ShapeDtypeStruct((s, D_MODEL), F32), jax.ShapeDtypeStruct((s, D_MODEL), BF16),
                   jax.ShapeDtypeStruct((s, D_MODEL), BF16), jax.ShapeDtypeStruct((s, D_MODEL), BF16)],
        compiler_params=_arb(1),
    )(dmixed, proj, proj, o_a, o_b)


CONV_CHUNK = 128
N_SLAB = D_FF // LANES


def _shifted(ref, c, nchunks):
    r0 = c * CONV_CHUNK
    cur = ref[r0:r0 + CONV_CHUNK, :]
    row = lax.broadcasted_iota(jnp.int32, (8, LANES), 0)
    if c > 0:
        prev = ref[r0 - 1:r0 - 1 + CONV_CHUNK, :]
    else:
        down = pltpu.roll(cur, 1, 0)
        prev = jnp.concatenate([jnp.where(row == 0, 0.0, down[:8]), down[8:]], axis=0)
    if c < nchunks - 1:
        nxt = ref[r0 + 1:r0 + 1 + CONV_CHUNK, :]
    else:
        up = pltpu.roll(cur, CONV_CHUNK - 1, 0)
        nxt = jnp.concatenate([up[:-8], jnp.where(row == 7, 0.0, up[-8:])], axis=0)
    return prev, cur, nxt


def _conv_taps(ref, w_ref, b_ref, c, nchunks):
    prev, cur, nxt = _shifted(ref, c, nchunks)
    conv = prev * w_ref[0:1, :] + cur * w_ref[1:2, :] + nxt * w_ref[2:3, :] + b_ref[...]
    return conv, prev, cur, nxt


def _convffn_fwd(u, conv_w, conv_b):
    s = u.shape[0]
    nchunks = s // CONV_CHUNK

    def body(ug_ref, uv_ref, wg_ref, wv_ref, bg_ref, bv_ref, f_ref):
        for c in range(nchunks):
            cg = _conv_taps(ug_ref, wg_ref, bg_ref, c, nchunks)[0]
            cv = _conv_taps(uv_ref, wv_ref, bv_ref, c, nchunks)[0]
            f_ref[c * CONV_CHUNK:(c + 1) * CONV_CHUNK, :] = (cg * jax.nn.sigmoid(cg) * cv).astype(BF16)

    slab = lambda off: _bs((s, LANES), lambda j: (0, off + j))
    wsl = lambda off: _bs((3, LANES), lambda j: (0, off + j))
    bsl = lambda off: _bs((1, LANES), lambda j: (0, off + j))
    return pl.pallas_call(
        body, name="convffn_fwd", grid=(N_SLAB,),
        in_specs=[slab(0), slab(N_SLAB), wsl(0), wsl(N_SLAB), bsl(0), bsl(N_SLAB)],
        out_specs=slab(0), out_shape=jax.ShapeDtypeStruct((s, D_FF), BF16),
        compiler_params=_arb(1),
    )(u, u, conv_w, conv_w, conv_b, conv_b)


def _convffn_bwd(u, conv_w, conv_b, df):
    s = u.shape[0]
    nchunks = s // CONV_CHUNK

    def body(ug_ref, uv_ref, wg_ref, wv_ref, bg_ref, bv_ref, df_ref, du_ref, dw_ref, db_ref, dcg_ref, dcv_ref):
        dwg = [jnp.zeros((1, LANES), F32) for _ in range(3)]
        dwv = [jnp.zeros((1, LANES), F32) for _ in range(3)]
        dbg = jnp.zeros((1, LANES), F32)
        dbv = jnp.zeros((1, LANES), F32)
        for c in range(nchunks):
            rows = slice(c * CONV_CHUNK, (c + 1) * CONV_CHUNK)
            cg, gp, gc, gn = _conv_taps(ug_ref, wg_ref, bg_ref, c, nchunks)
            cv, vp, vc, vn = _conv_taps(uv_ref, wv_ref, bv_ref, c, nchunks)
            dfv = df_ref[rows, :]
            sg = jax.nn.sigmoid(cg)
            dcg = dfv * cv * (sg * (1.0 + cg * (1.0 - sg)))
            dcv = dfv * (cg * sg)
            dcg_ref[rows, :] = dcg
            dcv_ref[rows, :] = dcv
            for t, (tg, tv) in enumerate(((gp, vp), (gc, vc), (gn, vn))):
                dwg[t] = dwg[t] + jnp.sum(tg * dcg, axis=0, keepdims=True)
                dwv[t] = dwv[t] + jnp.sum(tv * dcv, axis=0, keepdims=True)
            dbg = dbg + jnp.sum(dcg, axis=0, keepdims=True)
            dbv = dbv + jnp.sum(dcv, axis=0, keepdims=True)
        for t in range(3):
            dw_ref[0, t:t + 1, :] = dwg[t]
            dw_ref[1, t:t + 1, :] = dwv[t]
        db_ref[0] = dbg
        db_ref[1] = dbv
        for half, (dc_ref, w_ref) in enumerate(((dcg_ref, wg_ref), (dcv_ref, wv_ref))):
            for c in range(nchunks):
                prev, cur, nxt = _shifted(dc_ref, c, nchunks)
                du = nxt * w_ref[0:1, :] + cur * w_ref[1:2, :] + prev * w_ref[2:3, :]
                du_ref[half, c * CONV_CHUNK:(c + 1) * CONV_CHUNK, :] = du.astype(BF16)

    slab = lambda off: _bs((s, LANES), lambda j: (0, off + j))
    wsl = lambda off: _bs((3, LANES), lambda j: (0, off + j))
    bsl = lambda off: _bs((1, LANES), lambda j: (0, off + j))
    return pl.pallas_call(
        body, name="convffn_bwd", grid=(N_SLAB,),
        in_specs=[slab(0), slab(N_SLAB), wsl(0), wsl(N_SLAB), bsl(0), bsl(N_SLAB), slab(0)],
        out_specs=[_bs((2, s, LANES), lambda j: (0, 0, j)), _bs((2, 3, LANES), lambda j: (0, 0, j)),
                   _bs((2, 1, LANES), lambda j: (0, 0, j))],
        out_shape=[jax.ShapeDtypeStruct((2, s, D_FF), BF16), jax.ShapeDtypeStruct((2, 3, D_FF), F32),
                   jax.ShapeDtypeStruct((2, 1, D_FF), F32)],
        scratch_shapes=[pltpu.VMEM((s, LANES), F32), pltpu.VMEM((s, LANES), F32)],
        compiler_params=_arb(1),
    )(u, u, conv_w, conv_w, conv_b, conv_b, df)


def _row_tile(rows, limit=512):
    best = rows
    for t in range(8, min(rows, limit) + 1, 8):
        if rows % t == 0:
            best = t
    return best if rows % 8 == 0 else rows


ADAM_C1 = 1.0 - ADAM_B1 ** ADAM_STEP
ADAM_C2 = 1.0 - ADAM_B2 ** ADAM_STEP


def _adamw_math(w, gv, m, v):
    nm = ADAM_B1 * m + (1.0 - ADAM_B1) * gv
    nv = ADAM_B2 * v + (1.0 - ADAM_B2) * (gv * gv)
    m_hat = nm / ADAM_C1
    v_hat = nv / ADAM_C2
    return -ADAM_LR * (m_hat / (jnp.sqrt(v_hat) + ADAM_EPS) + ADAM_WD * w), nm, nv


def _adamw_halves(name, core, w, mine, theirs, m, v):
    half, cols = mine.shape
    tr = _row_tile(half)
    nr = half // tr

    def body(core_ref, w_ref, mine_ref, theirs_ref, m_ref, v_ref, g_ref, d_ref, nm_ref, nv_ref):
        gv = jnp.where(pl.program_id(0) == core_ref[0], mine_ref[...], theirs_ref[...])
        g_ref[...] = gv
        d_ref[...], nm_ref[...], nv_ref[...] = _adamw_math(w_ref[...], gv, m_ref[...], v_ref[...])

    full = pl.BlockSpec((tr, cols), lambda hf, r, cr: (hf * nr + r, 0))
    part = pl.BlockSpec((tr, cols), lambda hf, r, cr: (r, 0))
    return pl.pallas_call(
        body, name=name,
        grid_spec=pltpu.PrefetchScalarGridSpec(num_scalar_prefetch=1, grid=(2, nr),
                                               in_specs=[full, part, part, full, full], out_specs=[full] * 4),
        out_shape=[jax.ShapeDtypeStruct((2 * half, cols), F32)] * 4, compiler_params=_arb(2),
    )(core, w, mine, theirs, m, v)


ANY = pl.BlockSpec(memory_space=pl.ANY)


def _mesh_pos():
    return lax.axis_index("x"), lax.axis_index("y"), lax.axis_index("c")


def _other_chips(x, y):
    return [(1 - x, y), (x, 1 - y), (1 - x, 1 - y)]


def _forward_to_sibling(gathered):
    n = len(gathered)

    def body(*refs):
        in_refs, out_refs = refs[:n], refs[n:2 * n]
        send_sems, recv_sems = refs[2 * n:]
        x, y, c = _mesh_pos()
        cps = []
        for i in range(n):
            for k, chip in enumerate(_other_chips(x, y)):
                pk = 2 * chip[0] + chip[1]
                sems = dict(send_sem=send_sems.at[3 * i + k], recv_sem=recv_sems.at[3 * i + k],
                            device_id=(x, y, 1 - c), device_id_type=MESH)
                sent = pltpu.make_async_remote_copy(src_ref=in_refs[i].at[pk, c], dst_ref=out_refs[i].at[pk, c], **sems)
                sent.start()
                theirs = out_refs[i].at[pk, 1 - c]
                cps.append((sent, pltpu.make_async_remote_copy(src_ref=theirs, dst_ref=theirs, **sems)))
        for sent, arrived in cps:
            sent.wait_send()
            arrived.wait_recv()

    return pl.pallas_call(
        body, name="forward_to_sibling", in_specs=[ANY] * n, out_specs=[ANY] * n,
        out_shape=[jax.ShapeDtypeStruct(a.shape, a.dtype) for a in gathered],
        input_output_aliases={i: i for i in range(n)},
        scratch_shapes=[pltpu.SemaphoreType.DMA((3 * n,)), pltpu.SemaphoreType.DMA((3 * n,))],
    )(*gathered)


def _rs_pair_exchange(name, grads):
    n = len(grads)

    def body(*refs):
        g_refs, o_refs = refs[:n], refs[n:2 * n]
        send_sems, recv_sems = refs[2 * n:]
        x, y, c = _mesh_pos()
        cps = []
        for i in range(n):
            cp = pltpu.make_async_remote_copy(
                src_ref=g_refs[i].at[:, 1 - c], dst_ref=o_refs[i],
                send_sem=send_sems.at[i], recv_sem=recv_sems.at[i], device_id=(x, y, 1 - c), device_id_type=MESH)
            cp.start()
            cps.append(cp)
        for cp in cps:
            cp.wait()

    return pl.pallas_call(
        body, name=name, in_specs=[ANY] * n, out_specs=[ANY] * n,
        out_shape=[jax.ShapeDtypeStruct((4,) + g.shape[2:], F32) for g in grads],
        scratch_shapes=[pltpu.SemaphoreType.DMA((n,)), pltpu.SemaphoreType.DMA((n,))],
    )(*grads)


def _rs_pair_add(name, core, g, recv):
    _, half, cols = recv.shape
    tr = _row_tile(half)
    nr = half // tr

    def body(core_ref, g_ref, r_ref, o_ref):
        o_ref[...] = (g_ref[...] + r_ref[...]).astype(BF16)

    return pl.pallas_call(
        body, name=name,
        grid_spec=pltpu.PrefetchScalarGridSpec(
            num_scalar_prefetch=1, grid=(4, nr),
            in_specs=[pl.BlockSpec((None, None, tr, cols), lambda q, r, cr: (q, cr[0], r, 0)),
                      pl.BlockSpec((None, tr, cols), lambda q, r, cr: (q, r, 0))],
            out_specs=pl.BlockSpec((None, tr, cols), lambda q, r, cr: (q, r, 0))),
        out_shape=jax.ShapeDtypeStruct((4, half, cols), BF16),
        compiler_params=_arb(2),
    )(core, g, recv)


def _rs_final_add(name, chip, pair, recv):
    _, half, cols = pair.shape
    tr = _row_tile(half)

    def body(chip_ref, p_ref, r_ref, o_ref):
        o_ref[...] = ((p_ref[...].astype(F32) + r_ref[0].astype(F32)) + r_ref[1].astype(F32)) + r_ref[2].astype(F32)

    return pl.pallas_call(
        body, name=name,
        grid_spec=pltpu.PrefetchScalarGridSpec(
            num_scalar_prefetch=1, grid=(half // tr,),
            in_specs=[pl.BlockSpec((None, tr, cols), lambda r, ch: (ch[0], r, 0)),
                      pl.BlockSpec((3, tr, cols), lambda r, ch: (0, r, 0))],
            out_specs=pl.BlockSpec((tr, cols), lambda r, ch: (r, 0))),
        out_shape=jax.ShapeDtypeStruct((half, cols), F32),
        compiler_params=_arb(1),
    )(chip, pair, recv)


def _rs_pair_share(halves):
    n = len(halves)

    def body(*refs):
        h_refs, o_refs = refs[:n], refs[n:2 * n]
        send_sems, recv_sems = refs[2 * n:]
        x, y, c = _mesh_pos()
        cps = []
        for i in range(n):
            cp = pltpu.make_async_remote_copy(src_ref=h_refs[i], dst_ref=o_refs[i], send_sem=send_sems.at[i],
                                              recv_sem=recv_sems.at[i], device_id=(x, y, 1 - c), device_id_type=MESH)
            cp.start()
            cps.append(cp)
        for cp in cps:
            cp.wait()

    return pl.pallas_call(
        body, name="rs_pair_share", in_specs=[ANY] * n, out_specs=[ANY] * n,
        out_shape=[jax.ShapeDtypeStruct(h.shape, F32) for h in halves],
        scratch_shapes=[pltpu.SemaphoreType.DMA((n,)), pltpu.SemaphoreType.DMA((n,))],
    )(*halves)


HBM = pl.BlockSpec(memory_space=pltpu.HBM)
SEM = pl.BlockSpec(memory_space=pltpu.SEMAPHORE)


class _SplitExchange:
    def __init__(self, name, srcs, land_shapes, src_of, dst_of, arrive_of, to_sibling=False):
        self.name, self.srcs, self.land_shapes = name, list(srcs), list(land_shapes)
        self.src_of, self.dst_of, self.arrive_of = src_of, dst_of, arrive_of
        self.to_sibling = to_sibling
        self.fan = 1 if to_sibling else 3

    def _copies(self, src_refs, land_refs, send_sems, recv_sems):
        x, y, c = _mesh_pos()
        p = 2 * x + y
        if self.to_sibling:
            peers = [((x, y, 1 - c), 1 - c)]
        else:
            peers = [((*chip, c), 2 * chip[0] + chip[1]) for chip in _other_chips(x, y)]
        out = []
        for i, (src, land) in enumerate(zip(src_refs, land_refs)):
            for k, (peer, pk) in enumerate(peers):
                sems = dict(send_sem=send_sems.at[self.fan * i + k], recv_sem=recv_sems.at[self.fan * i + k],
                            device_id=peer, device_id_type=MESH)
                sent = pltpu.make_async_remote_copy(src_ref=self.src_of(src, k, p, pk, c),
                                                    dst_ref=self.dst_of(land, k, p, pk, c), **sems)
                here = self.arrive_of(land, k, p, pk, c)
                out.append((sent, pltpu.make_async_remote_copy(src_ref=here, dst_ref=here, **sems)))
        return out

    def start(self, after=None):
        n = len(self.srcs)
        n_in = 2 * n + (after is not None)

        def body(*refs):
            for sent, _ in self._copies(refs[:n], refs[n:2 * n], refs[n_in], refs[n_in + 1]):
                sent.start()
            refs[-1][...] = jnp.zeros((8, LANES), F32)

        lands = [lax.empty(shape, src.dtype) for shape, src in zip(self.land_shapes, self.srcs)]
        operands = [pltpu.with_memory_space_constraint(a, pltpu.HBM) for a in self.srcs + lands]
        outs = pl.pallas_call(
            body, name=self.name + "_start",
            out_shape=(pltpu.SemaphoreType.DMA((self.fan * n,)), pltpu.SemaphoreType.DMA((self.fan * n,)),
                       *[pltpu.HBM(a.shape, a.dtype) for a in operands], jax.ShapeDtypeStruct((8, LANES), F32)),
            in_specs=[HBM] * (2 * n) + [ANY] * (after is not None),
            out_specs=(SEM, SEM, *[HBM] * (2 * n), pl.BlockSpec(memory_space=pltpu.VMEM)),
            input_output_aliases={j: 2 + j for j in range(2 * n)},
            compiler_params=pltpu.CompilerParams(has_side_effects=pltpu.SideEffectType.DATAFLOW_SIDE_EFFECTING),
        )(*operands, *([after] if after is not None else []))
        self._sems, self._thru = outs[:2], list(outs[2:2 + 2 * n])
        return outs[-1]

    def wait(self, after):
        n = len(self.srcs)

        def body(*refs):
            for sent, arrived in self._copies(refs[:n], refs[n:2 * n], refs[2 * n], refs[2 * n + 1]):
                sent.wait_send()
                arrived.wait_recv()

        after = list(after) if isinstance(after, (list, tuple)) else [after]
        outs = pl.pallas_call(
            body, name=self.name + "_wait",
            out_shape=tuple(pltpu.HBM(a.shape, a.dtype) for a in self._thru),
            in_specs=[HBM] * (2 * n) + [SEM, SEM] + [ANY] * len(after), out_specs=tuple([HBM] * (2 * n)),
            input_output_aliases={j: j for j in range(2 * n)},
            compiler_params=pltpu.CompilerParams(has_side_effects=pltpu.SideEffectType.DATAFLOW_SIDE_EFFECTING),
        )(*self._thru, *self._sems, *after)
        return list(outs[:n]), list(outs[n:])


def _small_allreduce(parts):
    n = len(parts)

    def body(*refs):
        in_refs, out_refs, gather_refs = refs[:n], refs[n:2 * n], refs[2 * n:3 * n]
        send_sems, recv_sems = refs[3 * n:]
        x, y, c = _mesh_pos()
        me = 4 * x + 2 * y + c
        cps = []
        for i in range(n):
            gather_refs[i][me] = in_refs[i][...]
            for j in range(1, 8):
                peer = (x ^ (j >> 2), y ^ ((j >> 1) & 1), c ^ (j & 1))
                cp = pltpu.make_async_remote_copy(
                    src_ref=in_refs[i], dst_ref=gather_refs[i].at[me], send_sem=send_sems.at[7 * i + j - 1],
                    recv_sem=recv_sems.at[7 * i + j - 1], device_id=peer, device_id_type=MESH)
                cp.start()
                cps.append(cp)
        for i in range(n):
            for j in range(1, 8):
                peer_id = 4 * (x ^ (j >> 2)) + 2 * (y ^ ((j >> 1) & 1)) + (c ^ (j & 1))
                slot = gather_refs[i].at[peer_id]
                pltpu.make_async_remote_copy(src_ref=slot, dst_ref=slot, send_sem=send_sems.at[7 * i + j - 1],
                                             recv_sem=recv_sems.at[7 * i + j - 1], device_id=(x, y, c),
                                             device_id_type=MESH).wait_recv()
        for cp in cps:
            cp.wait_send()
        for i in range(n):
            tot = gather_refs[i][0]
            for d in range(1, 8):
                tot = tot + gather_refs[i][d]
            out_refs[i][...] = tot

    vmem = pl.BlockSpec(memory_space=pltpu.VMEM)
    return pl.pallas_call(
        body, name="small_allreduce", in_specs=[vmem] * n, out_specs=[vmem] * n,
        out_shape=[jax.ShapeDtypeStruct(p.shape, F32) for p in parts],
        scratch_shapes=[pltpu.VMEM((8,) + p.shape, F32) for p in parts]
        + [pltpu.SemaphoreType.DMA((7 * n,)), pltpu.SemaphoreType.DMA((7 * n,))],
    )(*parts)


def _adamw_small(ws, gs, ms, vs):
    n = len(ws)

    def body(*refs):
        for i in range(n):
            w_ref, g_ref, m_ref, v_ref = refs[i], refs[n + i], refs[2 * n + i], refs[3 * n + i]
            d_ref, nm_ref, nv_ref = refs[4 * n + i], refs[5 * n + i], refs[6 * n + i]
            d_ref[...], nm_ref[...], nv_ref[...] = _adamw_math(w_ref[...], g_ref[...], m_ref[...], v_ref[...])

    vmem = pl.BlockSpec(memory_space=pltpu.VMEM)
    shapes = [jax.ShapeDtypeStruct(w.shape, F32) for w in ws]
    outs = pl.pallas_call(body, name="adamw_small", in_specs=[vmem] * (4 * n), out_specs=[vmem] * (3 * n),
                          out_shape=shapes * 3)(*ws, *gs, *ms, *vs)
    return outs[:n], outs[n:2 * n], outs[2 * n:]


W_IN_PIECES = ((0, 256, P_QLAT), (256, 384, P_CKV), (384, 448, P_KR), (448, 1472, P_QB), (1472, 1728, P_KB),
               (1728, 1984, P_VB), (1984, 3008, P_GA), (3008, 4032, P_GB))
W_IN_SHARD = 1008


def _w_in_from_shards(shards):
    cols = []
    for lo, hi, _ in sorted(W_IN_PIECES, key=lambda piece: piece[2]):
        for q in range(4):
            a, b = max(lo, q * W_IN_SHARD), min(hi, (q + 1) * W_IN_SHARD)
            if a < b:
                cols.append(shards[q][:, a - q * W_IN_SHARD:b - q * W_IN_SHARD])
    cols.append(jnp.zeros((shards.shape[1], W_IN_PAD - 4 * W_IN_SHARD), shards.dtype))
    return jnp.concatenate(cols, axis=1)


def _w_in_to_shards(p):
    shards = []
    for q in range(4):
        cols = []
        for lo, hi, at in W_IN_PIECES:
            a, b = max(lo, q * W_IN_SHARD), min(hi, (q + 1) * W_IN_SHARD)
            if a < b:
                cols.append(p[:, at + a - lo:at + b - lo])
        shards.append(jnp.concatenate(cols, axis=1))
    return jnp.stack(shards)


def _col_shards(w):
    r, c4 = w.shape
    return w.reshape(r, 4, c4 // 4).transpose(1, 0, 2)


def _local_step(x, positions, target, norm1_g, first_weights, q_a_norm_g, kv_a_norm_g, rel_bias, sinks,
                late_weights, norm2_g, conv_b, final_norm_g, early_grads=None, last_grads=None):
    s = x.shape[0]
    half = QK_ROPE // 2
    inv_freq = jnp.asarray(np.float32(ROPE_THETA) ** (-np.arange(half, dtype=np.float32) / np.float32(half)))
    ang = positions.astype(F32)[:, None] * inv_freq[None, :]
    cos, sin = jnp.cos(ang), jnp.sin(ang)
    z64 = jnp.zeros((s, 64), F32)
    cos_t = jnp.concatenate([cos, cos, z64], axis=1)
    sin_t = jnp.concatenate([-sin, sin, z64], axis=1)
    bucket = _t5_bucket_table()
    sinks1 = sinks.reshape(H_B)

    h1, rstd1 = _rmsnorm_fwd("norm1_fwd", x, norm1_g, D_MODEL, 0)
    bias = _win_bias(bucket, rel_bias)
    w_in_p, wq, wkv = first_weights([h1, bias, cos_t, sin_t])
    proj, proj_b = _matmul("proj", h1, w_in_p, out_shape=(s, W_IN_PAD), out_dtype=F32, grid=(s // MM_ROWS, W_IN_PAD // 1024, 1),
                           a_spec=_bs((MM_ROWS, D_MODEL), lambda i, j, k: (i, 0)), b_spec=_bs((D_MODEL, 1024), lambda i, j, k: (0, j)),
                           o_spec=_bs((MM_ROWS, 1024), lambda i, j, k: (i, j)), contract=NN, bf16_copy=True)
    qn, cn, rstd_q, rstd_c = _lat_norms(proj, q_a_norm_g, kv_a_norm_g)
    q = _q_heads(qn, wq, cos_t, sin_t)
    k, v = _kv_heads(cn, wkv, proj, cos_t, sin_t)
    o_a, lse_a = _mla_fwd(q, k, v)

    o_b, lse_b = _win_fwd(proj_b, bias, sinks1)

    mixed = _gate_fwd(proj, o_a, o_b)
    w_out, w_up, w_down, conv_w = late_weights(mixed)
    row512 = lambda w: _bs((MM_ROWS, w), lambda i, j, k: (i, 0))
    whole = lambda r, c: _bs((r, c), lambda i, j, k: (0, 0))
    x1 = _matmul("attn_out", mixed, w_out, out_shape=(s, D_MODEL), out_dtype=F32, grid=(s // MM_ROWS, 1, 1),
                 a_spec=row512(D_MODEL), b_spec=whole(D_MODEL, D_MODEL), o_spec=row512(D_MODEL), contract=NN, add=x)
    h2, rstd2 = _rmsnorm_fwd("norm2_fwd", x1, norm2_g, D_MODEL, 0)
    u = _matmul("ffn_up", h2, w_up, out_shape=(s, 2 * D_FF), out_dtype=F32, grid=(s // MM_ROWS, 4, 1),
                a_spec=_bs((MM_ROWS, D_MODEL), lambda i, j, k: (i, 0)), b_spec=_bs((D_MODEL, D_FF // 2), lambda i, j, k: (0, j)),
                o_spec=_bs((MM_ROWS, D_FF // 2), lambda i, j, k: (i, j)), contract=NN)
    f = _convffn_fwd(u, conv_w, conv_b)
    x2 = _matmul("ffn_down", f, w_down, out_shape=(s, D_MODEL), out_dtype=F32, grid=(s // MM_ROWS, 1, 1),
                 a_spec=row512(D_FF), b_spec=whole(D_FF, D_MODEL), o_spec=row512(D_MODEL), contract=NN, add=x1)
    loss, dx2, d_final_g, dx2_b = _final_loss(x2, target, final_norm_g.reshape(1, D_MODEL))
    tk = min(s, DW_ROWS)

    df = _matmul("ffn_down_dx", dx2_b, w_down, out_shape=(s, D_FF), out_dtype=F32, grid=(s // MM_ROWS, 2, 1),
                 a_spec=row512(D_MODEL), b_spec=_bs((D_FF // 2, D_MODEL), lambda i, j, k: (j, 0)),
                 o_spec=_bs((MM_ROWS, D_FF // 2), lambda i, j, k: (i, j)), contract=NT)
    d_w_down = _matmul("ffn_down_dw", f, dx2_b, out_shape=(D_FF, D_MODEL), out_dtype=F32, grid=(2, 1, s // tk),
                       a_spec=_bs((tk, D_FF // 2), lambda i, j, k: (k, i)), b_spec=_bs((tk, D_MODEL), lambda i, j, k: (k, 0)),
                       o_spec=_bs((D_FF // 2, D_MODEL), lambda i, j, k: (i, 0)), contract=TN)
    du, d_conv_w2, d_conv_b2 = _convffn_bwd(u, conv_w, conv_b, df)
    kc = D_FF // 2
    dh2 = _matmul("ffn_up_dx", du, w_up, out_shape=(s, D_MODEL), out_dtype=F32, grid=(s // 1024, 1, 4),
                  a_spec=_bs((None, 1024, kc), lambda i, j, k: (k // 2, i, k % 2)),
                  b_spec=_bs((D_MODEL, kc), lambda i, j, k: (0, k)),
                  o_spec=_bs((1024, D_MODEL), lambda i, j, k: (i, 0)), contract=NT)
    d_w_up = _matmul("ffn_up_dw", h2, du, out_shape=(D_MODEL, 2 * D_FF), out_dtype=F32, grid=(1, 4, s // tk),
                     a_spec=_bs((tk, D_MODEL), lambda i, j, k: (k, 0)),
                     b_spec=_bs((None, tk, kc), lambda i, j, k: (j // 2, k, j % 2)),
                     o_spec=_bs((D_MODEL, kc), lambda i, j, k: (0, j)), contract=TN)
    dx1, d_norm2_g, dx1_b = _rmsnorm_bwd("norm2_bwd", dh2, x1, rstd2, norm2_g, D_MODEL, 0, F32, res=dx2, bf16_copy=True)

    d_w_out = _matmul("attn_out_dw", mixed, dx1_b, out_shape=(D_MODEL, D_MODEL), out_dtype=F32, grid=(1, 1, s // tk),
                      a_spec=_bs((tk, D_MODEL), lambda i, j, k: (k, 0)), b_spec=_bs((tk, D_MODEL), lambda i, j, k: (k, 0)),
                      o_spec=whole(D_MODEL, D_MODEL), contract=TN)
    token, early_grads_on = early_grads(d_w_out, d_w_up, d_w_down) if early_grads is not None else (None, None)
    dmixed = _matmul("attn_out_dx", dx1_b, w_out, out_shape=(s, D_MODEL), out_dtype=F32, grid=(s // MM_ROWS, 1, 1),
                     a_spec=row512(D_MODEL), b_spec=whole(D_MODEL, D_MODEL), o_spec=row512(D_MODEL), contract=NT,
                     after=token)
    do_a, do_b, d_ga, d_gb = _gate_bwd(dmixed, proj, o_a, o_b)
    if early_grads_on is not None:
        sinks1 = sinks1 + early_grads_on(d_ga)[0, :H_B]

    d_qb, dk_pad, dv_pad, dbias, dsink_rows = _win_bwd(proj_b, bias, sinks1, do_b, lse_b)
    wp = _win_param_grads(bucket, dbias, dsink_rows)[:, 0, :]
    d_rel_bias = wp[:, :NUM_BUCKETS].T
    d_sinks = wp[:, NUM_BUCKETS].reshape(1, H_B)
    d_kb = dk_pad[WINDOW:WINDOW + s].astype(BF16)
    d_vb = dv_pad[WINDOW:WINDOW + s].astype(BF16)

    dq_pre, dkv_pre, dkr = _mla_bwd(q, k, v, do_a, o_a, lse_a, cos_t, sin_t)
    d_kr = _mla_key_rope_grad(dkr, cos_t, sin_t)
    th = min(s, HEAD_ROWS)
    hgrid = (s // th, 1, H_A)
    hblock = _bs((None, th, HEAD_PAD), lambda i, j, k: (k, i, 0))
    hrows = lambda w: _bs((th, w), lambda i, j, k: (i, 0))
    dqn = _matmul("q_up_dx", dq_pre, wq, out_shape=(s, Q_LORA), out_dtype=F32, grid=hgrid, a_spec=hblock,
                  b_spec=_bs((None, Q_LORA, HEAD_PAD), lambda i, j, k: (k, 0, 0)), o_spec=hrows(Q_LORA), contract=NT)
    dcn = _matmul("kv_up_dx", dkv_pre, wkv, out_shape=(s, KV_LORA), out_dtype=F32, grid=hgrid, a_spec=hblock,
                  b_spec=_bs((None, KV_LORA, HEAD_PAD), lambda i, j, k: (k, 0, 0)), o_spec=hrows(KV_LORA), contract=NT)
    wgrid = (H_A, 1, s // th)
    d_wq = _matmul("q_up_dw", qn, dq_pre, out_shape=(H_A, Q_LORA, HEAD_PAD), out_dtype=F32, grid=wgrid,
                   a_spec=_bs((th, Q_LORA), lambda i, j, k: (k, 0)), b_spec=_bs((None, th, HEAD_PAD), lambda i, j, k: (i, k, 0)),
                   o_spec=_bs((None, Q_LORA, HEAD_PAD), lambda i, j, k: (i, 0, 0)), contract=TN)
    d_wkv = _matmul("kv_up_dw", cn, dkv_pre, out_shape=(H_A, KV_LORA, HEAD_PAD), out_dtype=F32, grid=wgrid,
                    a_spec=_bs((th, KV_LORA), lambda i, j, k: (k, 0)), b_spec=_bs((None, th, HEAD_PAD), lambda i, j, k: (i, k, 0)),
                    o_spec=_bs((None, KV_LORA, HEAD_PAD), lambda i, j, k: (i, 0, 0)), contract=TN)
    d_qlat, d_gq = _rmsnorm_bwd("q_norm_bwd", dqn, proj, rstd_q, q_a_norm_g, Q_LORA, P_QLAT // Q_LORA, BF16)
    d_ckv, d_gkv = _rmsnorm_bwd("kv_norm_bwd", dcn, proj, rstd_c, kv_a_norm_g, KV_LORA, P_CKV // KV_LORA, BF16)

    dproj = jnp.concatenate([d_qb, d_ga, d_gb, d_qlat, d_kb, d_vb, d_ckv, d_kr], axis=1)
    d_w_in_p = _matmul("proj_dw", h1, dproj, out_shape=(D_MODEL, W_IN_PAD), out_dtype=F32, grid=(1, W_IN_PAD // 1024, s // tk),
                       a_spec=_bs((tk, D_MODEL), lambda i, j, k: (k, 0)), b_spec=_bs((tk, 1024), lambda i, j, k: (k, j)),
                       o_spec=_bs((D_MODEL, 1024), lambda i, j, k: (0, j)), contract=TN)
    token = last_grads(d_w_in_p, d_wq, d_wkv) if last_grads is not None else None
    dh1 = _matmul("proj_dx", dproj, w_in_p, out_shape=(s, D_MODEL), out_dtype=F32, grid=(s // 1024, 1, W_IN_PAD // 1024),
                  a_spec=_bs((1024, 1024), lambda i, j, k: (i, k)), b_spec=_bs((D_MODEL, 1024), lambda i, j, k: (0, k)),
                  o_spec=_bs((1024, D_MODEL), lambda i, j, k: (i, 0)), contract=NT, after=token)
    dx, d_norm1_g = _rmsnorm_bwd("norm1_bwd", dh1, x, rstd1, norm1_g, D_MODEL, 0, F32, res=dx1)

    grads = dict(
        norm1_g=d_norm1_g, w_in_p=d_w_in_p, q_a_norm_g=d_gq, wq=d_wq, kv_a_norm_g=d_gkv, wkv=d_wkv,
        rel_bias=d_rel_bias, sinks=d_sinks, w_out=d_w_out, norm2_g=d_norm2_g, w_up=d_w_up,
        conv_w=jnp.concatenate([d_conv_w2[0], d_conv_w2[1]], axis=1),
        conv_b=jnp.concatenate([d_conv_b2[0], d_conv_b2[1]], axis=1),
        w_down=d_w_down, final_norm_g=d_final_g.reshape(D_MODEL))
    return loss, dx, grads


HEADS_PER_SHARD = H_A // 4


def _head_cols(h, width):
    return slice((h % HEADS_PER_SHARD) * width, (h % HEADS_PER_SHARD + 1) * width)


def _wq_heads(shards):
    per = QK_NOPE + QK_ROPE
    w = jnp.stack([shards[h // HEADS_PER_SHARD][:, _head_cols(h, per)] for h in range(H_A)])
    return jnp.pad(w, ((0, 0), (0, 0), (0, HEAD_PAD - per)))


def _wq_shards(d_wq):
    per = QK_NOPE + QK_ROPE
    return jnp.stack([jnp.concatenate([d_wq[h][:, :per] for h in range(q * HEADS_PER_SHARD, (q + 1) * HEADS_PER_SHARD)],
                                      axis=1) for q in range(4)])


def _wkv_heads(shards):
    return jnp.stack([shards[h // HEADS_PER_SHARD][:, _head_cols(h, QK_NOPE + V_DIM)] for h in range(H_A)])


def _wkv_shards(d_wkv):
    return jnp.stack([jnp.concatenate([d_wkv[h] for h in range(q * HEADS_PER_SHARD, (q + 1) * HEADS_PER_SHARD)], axis=1)
                      for q in range(4)])


SMALL = ("norm1_g", "q_a_norm_g", "kv_a_norm_g", "rel_bias", "sinks", "norm2_g", "conv_b", "final_norm_g")
FIRST = ("w_in", "w_q_b", "w_kv_b")
LATER = ("w_out", "w_up", "w_down")
BIG = FIRST + LATER


def kernel(x, positions, norm1_g, w_in, q_a_norm_g, w_q_b, kv_a_norm_g, w_kv_b, rel_bias, sinks, w_out, norm2_g, w_up, conv_w, conv_b, w_down, final_norm_g, loss_target, m_norm1_g, m_w_in, m_q_a_norm_g, m_w_q_b, m_kv_a_norm_g, m_w_kv_b, m_rel_bias, m_sinks, m_w_out, m_norm2_g, m_w_up, m_conv_w, m_conv_b, m_w_down, m_final_norm_g, v_norm1_g, v_w_in, v_q_a_norm_g, v_w_q_b, v_kv_a_norm_g, v_w_kv_b, v_rel_bias, v_sinks, v_w_out, v_norm2_g, v_w_up, v_conv_w, v_conv_b, v_w_down, v_final_norm_g):
    weights = dict(norm1_g=norm1_g, w_in=w_in, q_a_norm_g=q_a_norm_g, w_q_b=w_q_b, kv_a_norm_g=kv_a_norm_g,
                   w_kv_b=w_kv_b, rel_bias=rel_bias, sinks=sinks, w_out=w_out, norm2_g=norm2_g, w_up=w_up,
                   conv_w=conv_w, conv_b=conv_b, w_down=w_down, final_norm_g=final_norm_g)
    mom_m = dict(norm1_g=m_norm1_g, w_in=m_w_in, q_a_norm_g=m_q_a_norm_g, w_q_b=m_w_q_b, kv_a_norm_g=m_kv_a_norm_g,
                 w_kv_b=m_w_kv_b, rel_bias=m_rel_bias, sinks=m_sinks, w_out=m_w_out, norm2_g=m_norm2_g, w_up=m_w_up,
                 conv_w=m_conv_w, conv_b=m_conv_b, w_down=m_w_down, final_norm_g=m_final_norm_g)
    mom_v = dict(norm1_g=v_norm1_g, w_in=v_w_in, q_a_norm_g=v_q_a_norm_g, w_q_b=v_w_q_b, kv_a_norm_g=v_kv_a_norm_g,
                 w_kv_b=v_w_kv_b, rel_bias=v_rel_bias, sinks=v_sinks, w_out=v_w_out, norm2_g=v_norm2_g, w_up=v_w_up,
                 conv_w=v_conv_w, conv_b=v_conv_b, w_down=v_w_down, final_norm_g=v_final_norm_g)
    shard2d = {n: weights[n][0] for n in BIG}
    conv_w_shard = conv_w[0]
    xi, yi, ci = lax.axis_index("x"), lax.axis_index("y"), lax.axis_index("c")
    chip = (2 * xi + yi).astype(jnp.int32)

    core = ci.astype(jnp.int32).reshape(1)
    chip1 = chip.reshape(1)
    cat_cols = lambda a: jnp.concatenate([a[0], a[1], a[2], a[3]], axis=1)
    own_slot = lambda a, own: lax.dynamic_update_index_in_dim(a, own, chip, 0)
    halved = lambda a: a.reshape((2, a.shape[0] // 2) + a.shape[1:])
    quartered = lambda a: a.reshape(4, 2, a.shape[1] // 2, a.shape[2])

    first = [halved(shard2d[n].astype(BF16)) for n in FIRST]
    gather1 = _SplitExchange("gather_first", first, [(4,) + a.shape for a in first],
                             src_of=lambda ref, k, p, pk, c: ref.at[c], dst_of=lambda ref, k, p, pk, c: ref.at[p, c],
                             arrive_of=lambda ref, k, p, pk, c: ref.at[pk, c])
    token1 = gather1.start()

    def first_weights(after):
        own, landed = gather1.wait(after)
        gathered = [own_slot(a, mine) for a, mine in zip(_forward_to_sibling(landed), own)]
        g = {n: a.reshape((4,) + shard2d[n].shape) for n, a in zip(FIRST, gathered)}
        return _w_in_from_shards(g["w_in"]), _wq_heads(g["w_q_b"]), _wkv_heads(g["w_kv_b"])

    later = [shard2d[n].astype(BF16) for n in LATER] + [conv_w_shard]
    gather2 = _SplitExchange("gather_later", later, [(4,) + a.shape for a in later],
                             src_of=lambda ref, k, p, pk, c: ref, dst_of=lambda ref, k, p, pk, c: ref.at[p],
                             arrive_of=lambda ref, k, p, pk, c: ref.at[pk])
    norm1_g_in = norm1_g + gather2.start(after=token1)[:1, :1]

    def late_weights(after):
        w_out_g, w_up_g, w_down_g, conv_w_g = [own_slot(a, mine) for mine, a in zip(*gather2.wait(after))]
        return w_out_g.reshape(D_MODEL, D_MODEL), cat_cols(w_up_g), w_down_g.reshape(D_FF, D_MODEL), cat_cols(conv_w_g)

    early = {}

    def early_grads(d_w_out, d_w_up, d_w_down):
        grads = [quartered(d_w_out.reshape(4, D_MODEL // 4, D_MODEL)), quartered(_col_shards(d_w_up)),
                 quartered(d_w_down.reshape(4, D_FF // 4, D_MODEL))]
        swap = _SplitExchange("rs_pair_exchange_early", grads, [(4,) + a.shape[2:] for a in grads], to_sibling=True,
                              src_of=lambda ref, k, p, pk, c: ref.at[:, pk], dst_of=lambda ref, k, p, pk, c: ref,
                              arrive_of=lambda ref, k, p, pk, c: ref)

        def on(after):
            kept, recv = swap.wait(after)
            early["pairs"] = [_rs_pair_add(f"rs_pair_add_{n}", core, gfull, r) for n, gfull, r in zip(LATER, kept, recv)]
            early["ici"] = _SplitExchange("rs_ici_early", early["pairs"], [(3,) + a.shape[1:] for a in early["pairs"]],
                                          src_of=lambda ref, k, p, pk, c: ref.at[pk], dst_of=lambda ref, k, p, pk, c: ref.at[k],
                                          arrive_of=lambda ref, k, p, pk, c: ref.at[k])
            return early["ici"].start()

        return swap.start(), on

    last = {}

    def last_grads(d_w_in_p, d_wq, d_wkv):
        grads = [quartered(_w_in_to_shards(d_w_in_p)), quartered(_wq_shards(d_wq)), quartered(_wkv_shards(d_wkv))]
        recv = _rs_pair_exchange("rs_pair_exchange_last", grads)
        last["pairs"] = [_rs_pair_add(f"rs_pair_add_{n}", core, gfull, r) for n, gfull, r in zip(FIRST, grads, recv)]
        last["ici"] = _SplitExchange("rs_ici_last", last["pairs"], [(3,) + a.shape[1:] for a in last["pairs"]],
                                     src_of=lambda ref, k, p, pk, c: ref.at[pk], dst_of=lambda ref, k, p, pk, c: ref.at[k],
                                     arrive_of=lambda ref, k, p, pk, c: ref.at[k])
        return last["ici"].start()

    loss, dx, gr = _local_step(x[0], positions, loss_target[0], norm1_g_in, first_weights, q_a_norm_g, kv_a_norm_g,
                               rel_bias, sinks, late_weights, norm2_g, conv_b, final_norm_g, early_grads, last_grads)

    last_pairs, last_recv = last["ici"].wait(dx)
    early_pairs, early_recv = early["ici"].wait(dx)
    pairs, recv2 = last_pairs + early_pairs, last_recv + early_recv
    halves = [_rs_final_add(f"rs_final_add_{n}", chip1, pr, r) for n, pr, r in zip(FIRST + LATER, pairs, recv2)]
    sibling_halves = _rs_pair_share(halves)

    as_rows = lambda a: a.reshape((-1, a.shape[-1]))
    summed = _small_allreduce([as_rows(gr[n]) for n in SMALL] + [gr["conv_w"], loss])
    small_g = dict(zip(SMALL, summed[:len(SMALL)]))
    conv_w_g = lax.dynamic_slice_in_dim(summed[len(SMALL)], chip * (2 * D_FF // 4), 2 * D_FF // 4, axis=1)
    loss_out = summed[-1].reshape(())

    out_g, out_d, out_m, out_v = {}, {}, {}, {}
    for n, mine, theirs in zip(FIRST + LATER, halves, sibling_halves):
        gsh, d, nm, nv = _adamw_halves(f"adamw_{n}", core, shard2d[n], mine, theirs, mom_m[n][0], mom_v[n][0])
        out_g[n], out_d[n], out_m[n], out_v[n] = gsh[None], d[None], nm[None], nv[None]
    names = SMALL + ("conv_w",)
    sg = [small_g[n] for n in SMALL] + [conv_w_g]
    ds, nms, nvs = _adamw_small([as_rows(weights[n]) for n in names], sg, [as_rows(mom_m[n]) for n in names],
                                [as_rows(mom_v[n]) for n in names])
    for n, gg, dd, mm, vv in zip(names, sg, ds, nms, nvs):
        shp = weights[n].shape
        out_g[n], out_d[n], out_m[n], out_v[n] = gg.reshape(shp), dd.reshape(shp), mm.reshape(shp), vv.reshape(shp)

    order = ("norm1_g", "w_in", "q_a_norm_g", "w_q_b", "kv_a_norm_g", "w_kv_b", "rel_bias", "sinks", "w_out",
             "norm2_g", "w_up", "conv_w", "conv_b", "w_down", "final_norm_g")
    return (loss_out, dx[None], *[out_g[n] for n in order], *[out_d[n] for n in order],
            *[out_m[n] for n in order], *[out_v[n] for n in order])
```

```python
import functools
import math

import jax
import jax.numpy as jnp
import numpy as np
from jax import lax
from jax.experimental import pallas as pl
from jax.experimental.pallas import tpu as pltpu

F32 = jnp.float32
BF16 = jnp.bfloat16
MESH = pl.DeviceIdType.MESH

D_MODEL = 1024
EPS = 1e-6
H_A = 8
QK_NOPE = 128
QK_ROPE = 64
V_DIM = 128
Q_LORA = 256
KV_LORA = 128
ROPE_THETA = 10000.0
H_B = 16
KV_B = 4
GROUP = 4
HD_B = 64
WINDOW = 128
Q_BLOCK = 128
NUM_BUCKETS = 32
MAX_DISTANCE = 128
D_FF = 2816
HEAD_PAD = 256

ADAM_LR = 0.001
ADAM_B1 = 0.9
ADAM_B2 = 0.999
ADAM_EPS = 1e-08
ADAM_WD = 0.01
ADAM_STEP = 10

LANES = 128
P_QB, P_GA, P_GB, P_QLAT, P_KB, P_VB, P_CKV, P_KR = 0, 1024, 2048, 3072, 3328, 3584, 3840, 3968
W_IN_PAD = 4096

NT = (((1,), (1,)), ((), ()))
NN = (((1,), (0,)), ((), ()))
TN = (((0,), (0,)), ((), ()))


def _arb(n):
    return pltpu.CompilerParams(dimension_semantics=("arbitrary",) * n)


def _matmul(name, a, b, *, out_shape, out_dtype, grid, a_spec, b_spec, o_spec, contract, add=None, bf16_copy=False,
            after=None):
    nk = grid[2]
    acc_shape = tuple(d for d in o_spec.block_shape if d is not None)
    n_in = 2 + (add is not None) + (after is not None)
    n_out = 2 if bf16_copy else 1

    def body(*refs):
        a_ref, b_ref = refs[:2]
        add_ref = refs[2] if add is not None else None
        o_refs = refs[n_in:n_in + n_out]
        scratch = refs[n_in + n_out:]
        prod = lax.dot_general(a_ref[...].astype(BF16), b_ref[...].astype(BF16), contract,
                               preferred_element_type=F32)

        def finish(val):
            if add_ref is not None:
                val = add_ref[...] + val
            o_refs[0][...] = val.astype(out_dtype)
            if bf16_copy:
                o_refs[1][...] = val.astype(BF16)

        if nk == 1:
            finish(prod)
        else:
            acc_ref = scratch[0]
            k = pl.program_id(2)

            @pl.when(k == 0)
            def _():
                acc_ref[...] = prod

            @pl.when((k > 0) & (k < nk - 1))
            def _():
                acc_ref[...] += prod

            @pl.when(k == nk - 1)
            def _():
                finish(acc_ref[...] + prod)

    in_specs = [a_spec, b_spec]
    args = [a, b]
    if add is not None:
        in_specs.append(o_spec)
        args.append(add)
    if after is not None:
        in_specs.append(pl.BlockSpec(memory_space=pl.ANY))
        args.append(after)
    out_shapes = [jax.ShapeDtypeStruct(out_shape, out_dtype)]
    if bf16_copy:
        out_shapes.append(jax.ShapeDtypeStruct(out_shape, BF16))
    res = pl.pallas_call(
        body, name=name, grid=grid, in_specs=in_specs, out_specs=[o_spec] * n_out, out_shape=out_shapes,
        scratch_shapes=[pltpu.VMEM(acc_shape, F32)] if nk > 1 else [],
        compiler_params=_arb(3),
    )(*args)
    return res if bf16_copy else res[0]


def _bs(block, fn):
    return pl.BlockSpec(block, fn)


def _rmsnorm_fwd(name, src, g, d, cb, ts=512):
    s = src.shape[0]

    def body(x_ref, g_ref, h_ref, r_ref):
        x = x_ref[...]
        r = lax.rsqrt(jnp.mean(x * x, axis=-1, keepdims=True) + EPS)
        h_ref[...] = (x * r * g_ref[...]).astype(BF16)
        r_ref[...] = r

    return pl.pallas_call(
        body, name=name, grid=(s // ts,),
        in_specs=[_bs((ts, d), lambda i: (i, cb)), _bs((1, d), lambda i: (0, 0))],
        out_specs=[_bs((ts, d), lambda i: (i, 0)), _bs((ts, 1), lambda i: (i, 0))],
        out_shape=[jax.ShapeDtypeStruct((s, d), BF16), jax.ShapeDtypeStruct((s, 1), F32)],
        compiler_params=_arb(1),
    )(src, g)


def _rmsnorm_bwd(name, dy, src, rstd, g, d, cb, out_dtype, res=None, bf16_copy=False, ts=512):
    s = src.shape[0]

    def body(*refs):
        dy_ref, x_ref, r_ref, g_ref = refs[:4]
        res_ref = refs[4] if res is not None else None
        dx_ref, dg_ref = refs[n_in:n_in + 2]
        dyv = dy_ref[...]
        r = r_ref[...]
        xhat = x_ref[...] * r
        dyh = dyv * g_ref[...]
        c = jnp.mean(dyh * xhat, axis=-1, keepdims=True)
        dx = r * (dyh - xhat * c)
        if res_ref is not None:
            dx = res_ref[...] + dx
        dx_ref[...] = dx.astype(out_dtype)
        if bf16_copy:
            refs[n_in + 2][...] = dx.astype(BF16)
        part = jnp.sum(dyv * xhat, axis=0, keepdims=True)

        @pl.when(pl.program_id(0) == 0)
        def _():
            dg_ref[...] = part

        @pl.when(pl.program_id(0) > 0)
        def _():
            dg_ref[...] += part

    in_specs = [_bs((ts, d), lambda i: (i, 0)), _bs((ts, d), lambda i: (i, cb)),
                _bs((ts, 1), lambda i: (i, 0)), _bs((1, d), lambda i: (0, 0))]
    args = [dy, src, rstd, g]
    if res is not None:
        in_specs.append(_bs((ts, d), lambda i: (i, 0)))
        args.append(res)
    n_in = len(args)
    out_specs = [_bs((ts, d), lambda i: (i, 0)), _bs((1, d), lambda i: (0, 0))]
    out_shape = [jax.ShapeDtypeStruct((s, d), out_dtype), jax.ShapeDtypeStruct((1, d), F32)]
    if bf16_copy:
        out_specs.append(_bs((ts, d), lambda i: (i, 0)))
        out_shape.append(jax.ShapeDtypeStruct((s, d), BF16))
    return pl.pallas_call(
        body, name=name, grid=(s // ts,), in_specs=in_specs, out_specs=out_specs, out_shape=out_shape,
        compiler_params=_arb(1),
    )(*args)


def _matmul_norm_bwd(name, a, b, *, grid, a_spec, b_spec, x, rstd, g, res, bf16_copy=False, after=None):
    s, d = x.shape
    ni, nk = grid
    assert nk >= 2, "the first and the last contraction step are distinct branches"
    tm = s // ni
    n_in = 6 + (after is not None)

    def body(*refs):
        a_ref, b_ref, x_ref, r_ref, g_ref, res_ref = refs[:6]
        dx_ref, dg_ref = refs[n_in:n_in + 2]
        acc_ref = refs[-1]
        k = pl.program_id(1)
        prod = lax.dot_general(a_ref[...], b_ref[...], NT, preferred_element_type=F32)

        @pl.when(k == 0)
        def _():
            acc_ref[...] = prod

        @pl.when((k > 0) & (k < nk - 1))
        def _():
            acc_ref[...] += prod

        @pl.when(k == nk - 1)
        def _():
            dyv = acc_ref[...] + prod
            r = r_ref[...]
            xhat = x_ref[...] * r
            dyh = dyv * g_ref[...]
            c = jnp.mean(dyh * xhat, axis=-1, keepdims=True)
            dx = res_ref[...] + r * (dyh - xhat * c)
            dx_ref[...] = dx
            if bf16_copy:
                refs[n_in + 2][...] = dx.astype(BF16)
            part = jnp.sum(dyv * xhat, axis=0, keepdims=True)

            @pl.when(pl.program_id(0) == 0)
            def _():
                dg_ref[...] = part

            @pl.when(pl.program_id(0) > 0)
            def _():
                dg_ref[...] += part

    rows = _bs((tm, d), lambda i, k: (i, 0))
    in_specs = [a_spec, b_spec, rows, _bs((tm, 1), lambda i, k: (i, 0)), _bs((1, d), lambda i, k: (0, 0)), rows]
    args = [a, b, x, rstd, g, res]
    if after is not None:
        in_specs.append(pl.BlockSpec(memory_space=pl.ANY))
        args.append(after)
    out_specs = [rows, _bs((1, d), lambda i, k: (0, 0))]
    out_shape = [jax.ShapeDtypeStruct((s, d), F32), jax.ShapeDtypeStruct((1, d), F32)]
    if bf16_copy:
        out_specs.append(rows)
        out_shape.append(jax.ShapeDtypeStruct((s, d), BF16))
    return pl.pallas_call(
        body, name=name, grid=grid, in_specs=in_specs, out_specs=out_specs, out_shape=out_shape,
        scratch_shapes=[pltpu.VMEM((tm, d), F32)], compiler_params=_arb(2),
    )(*args)


def _ffn_down_loss(f, w_down, x1, target, g, ts=512):
    s, d = x1.shape
    dff = f.shape[1]

    def body(f_ref, w_ref, x_ref, t_ref, g_ref, loss_ref, dx_ref, dg_ref, dxb_ref):
        x = x_ref[...] + jnp.dot(f_ref[...], w_ref[...], preferred_element_type=F32)
        r = lax.rsqrt(jnp.mean(x * x, axis=-1, keepdims=True) + EPS)
        xhat = x * r
        gv = g_ref[...]
        err = xhat * gv - t_ref[...]
        lpart = 0.5 * jnp.sum(jnp.mean(err * err, axis=-1, keepdims=True), axis=0, keepdims=True)
        dyv = err * (1.0 / d)
        dyh = dyv * gv
        c = jnp.mean(dyh * xhat, axis=-1, keepdims=True)
        dx = r * (dyh - xhat * c)
        dx_ref[...] = dx
        dxb_ref[...] = dx.astype(BF16)
        gpart = jnp.sum(dyv * xhat, axis=0, keepdims=True)

        @pl.when(pl.program_id(0) == 0)
        def _():
            dg_ref[...] = gpart
            loss_ref[...] = lpart

        @pl.when(pl.program_id(0) > 0)
        def _():
            dg_ref[...] += gpart
            loss_ref[...] += lpart

    rows = _bs((ts, d), lambda i: (i, 0))
    return pl.pallas_call(
        body, name="ffn_down_loss", grid=(s // ts,),
        in_specs=[_bs((ts, dff), lambda i: (i, 0)), _bs((dff, d), lambda i: (0, 0)), rows, rows,
                  _bs((1, d), lambda i: (0, 0))],
        out_specs=[_bs((1, 1), lambda i: (0, 0)), rows, _bs((1, d), lambda i: (0, 0)), rows],
        out_shape=[jax.ShapeDtypeStruct((1, 1), F32), jax.ShapeDtypeStruct((s, d), F32),
                   jax.ShapeDtypeStruct((1, d), F32), jax.ShapeDtypeStruct((s, d), BF16)],
        compiler_params=_arb(1),
    )(f, w_down, x1, target, g)


def _swap_halves(t):
    lane = lax.broadcasted_iota(jnp.int32, t.shape, 1)
    return jnp.where(lane < 32, pltpu.roll(t, 96, 1), pltpu.roll(t, 32, 1))


def _rope_fwd(t, cos_t, sin_t):
    return t * cos_t + _swap_halves(t) * sin_t


def _rope_bwd(dt, cos_t, sin_t):
    return dt * cos_t - _swap_halves(dt) * sin_t


def _lat_norms(proj, gq, gkv, ts=512):
    s = proj.shape[0]

    def body(q_ref, c_ref, gq_ref, gkv_ref, qn_ref, cn_ref, rq_ref, rc_ref):
        q = q_ref[...]
        rq = lax.rsqrt(jnp.mean(q * q, axis=-1, keepdims=True) + EPS)
        qn_ref[...] = (q * rq * gq_ref[...]).astype(BF16)
        rq_ref[...] = rq
        cv = c_ref[...]
        rc = lax.rsqrt(jnp.mean(cv * cv, axis=-1, keepdims=True) + EPS)
        cn_ref[...] = (cv * rc * gkv_ref[...]).astype(BF16)
        rc_ref[...] = rc

    return pl.pallas_call(
        body, name="lat_norms", grid=(s // ts,),
        in_specs=[_bs((ts, Q_LORA), lambda i: (i, P_QLAT // Q_LORA)),
                  _bs((ts, KV_LORA), lambda i: (i, P_CKV // KV_LORA)),
                  _bs((1, Q_LORA), lambda i: (0, 0)), _bs((1, KV_LORA), lambda i: (0, 0))],
        out_specs=[_bs((ts, Q_LORA), lambda i: (i, 0)), _bs((ts, KV_LORA), lambda i: (i, 0)),
                   _bs((ts, 1), lambda i: (i, 0)), _bs((ts, 1), lambda i: (i, 0))],
        out_shape=[jax.ShapeDtypeStruct((s, Q_LORA), BF16), jax.ShapeDtypeStruct((s, KV_LORA), BF16),
                   jax.ShapeDtypeStruct((s, 1), F32), jax.ShapeDtypeStruct((s, 1), F32)],
        compiler_params=_arb(1),
    )(proj, proj, gq, gkv)


HEAD_ROWS = 2048
DW_ROWS = 2048
MM_ROWS = 1024


def _q_heads(qn, wq, cos_t, sin_t):
    s = qn.shape[0]
    ts = min(s, HEAD_ROWS)

    def body(qn_ref, w_ref, cos_ref, sin_ref, q_ref):
        o = jnp.dot(qn_ref[...], w_ref[...], preferred_element_type=F32)
        q_ref[:, :LANES] = o[:, :LANES].astype(BF16)
        q_ref[:, LANES:] = _rope_fwd(o[:, LANES:], cos_ref[...], sin_ref[...]).astype(BF16)

    return pl.pallas_call(
        body, name="q_heads", grid=(s // ts, H_A),
        in_specs=[_bs((ts, Q_LORA), lambda i, h: (i, 0)), _bs((None, Q_LORA, HEAD_PAD), lambda i, h: (h, 0, 0)),
                  _bs((ts, LANES), lambda i, h: (i, 0)), _bs((ts, LANES), lambda i, h: (i, 0))],
        out_specs=_bs((None, ts, HEAD_PAD), lambda i, h: (h, i, 0)),
        out_shape=jax.ShapeDtypeStruct((H_A, s, HEAD_PAD), BF16),
        compiler_params=_arb(2),
    )(qn, wq, cos_t, sin_t)


def _kv_heads(cn, wkv, proj, cos_t, sin_t):
    s = cn.shape[0]
    ts = min(s, HEAD_ROWS)

    def body(cn_ref, w_ref, kr_ref, cos_ref, sin_ref, k_ref, v_ref):
        o = jnp.dot(cn_ref[...], w_ref[...], preferred_element_type=F32)
        k_ref[:, :LANES] = o[:, :LANES].astype(BF16)
        k_ref[:, LANES:] = _rope_fwd(kr_ref[...], cos_ref[...], sin_ref[...]).astype(BF16)
        v_ref[...] = o[:, LANES:].astype(BF16)

    return pl.pallas_call(
        body, name="kv_heads", grid=(s // ts, H_A),
        in_specs=[_bs((ts, KV_LORA), lambda i, h: (i, 0)),
                  _bs((None, KV_LORA, QK_NOPE + V_DIM), lambda i, h: (h, 0, 0)),
                  _bs((ts, LANES), lambda i, h: (i, P_KR // LANES)),
                  _bs((ts, LANES), lambda i, h: (i, 0)), _bs((ts, LANES), lambda i, h: (i, 0))],
        out_specs=[_bs((None, ts, HEAD_PAD), lambda i, h: (h, i, 0)), _bs((None, ts, V_DIM), lambda i, h: (h, i, 0))],
        out_shape=[jax.ShapeDtypeStruct((H_A, s, HEAD_PAD), BF16), jax.ShapeDtypeStruct((H_A, s, V_DIM), BF16)],
        compiler_params=_arb(2),
    )(cn, wkv, proj, cos_t, sin_t)


MLA_SCALE = 1.0 / math.sqrt(QK_NOPE + QK_ROPE)
LOG2E = math.log2(math.e)
MLA_EXP2_SCALE = MLA_SCALE * LOG2E


def _lane_tiles(a):
    return [a[:, j * LANES:(j + 1) * LANES] for j in range(a.shape[1] // LANES)]


MLA_SUB = 512


def _mla_fwd(q, k, v, tq=512, tk=1024):
    s = q.shape[1]
    tq = min(tq, s)
    nk = s // tk

    def body(q_ref, k_ref, v_ref, o_ref, lse_ref, m_ref, l_ref, acc_ref):
        m_ref[...] = jnp.full(m_ref.shape, -jnp.inf, F32)
        l_ref[...] = jnp.zeros(l_ref.shape, F32)
        acc_ref[...] = jnp.zeros(acc_ref.shape, F32)

        def step(c, carry):
            rows = pl.ds(pl.multiple_of(c * tk, tk), tk)
            for sub in range(tq // MLA_SUB):
                qr = slice(sub * MLA_SUB, (sub + 1) * MLA_SUB)
                raw = lax.dot_general(q_ref[qr, :], k_ref[rows, :], NT, preferred_element_type=F32)
                m_prev = m_ref[qr, :]
                m_new = jnp.maximum(m_prev, jnp.max(raw, axis=-1, keepdims=True))
                alpha = jnp.exp2((m_prev - m_new) * MLA_EXP2_SCALE)
                ps = [jnp.exp2((t - m_new) * MLA_EXP2_SCALE) for t in _lane_tiles(raw)]
                l_ref[qr, :] = alpha * l_ref[qr, :] + functools.reduce(lambda a, b: a + b, ps)
                p = jnp.concatenate(ps, axis=1).astype(BF16)
                acc_ref[qr, :] = alpha * acc_ref[qr, :] + jnp.dot(p, v_ref[rows, :], preferred_element_type=F32)
                m_ref[qr, :] = m_new
            return carry

        lax.fori_loop(0, nk, step, 0, unroll=True)
        l = jnp.sum(l_ref[...], axis=-1, keepdims=True)
        o_ref[...] = acc_ref[...] / l
        lse_ref[...] = m_ref[...] * MLA_SCALE + jnp.log(l)

    return pl.pallas_call(
        body, name="mla_fwd", grid=(H_A, s // tq),
        in_specs=[_bs((None, tq, HEAD_PAD), lambda h, i: (h, i, 0)),
                  _bs((None, s, HEAD_PAD), lambda h, i: (h, 0, 0)),
                  _bs((None, s, V_DIM), lambda h, i: (h, 0, 0))],
        out_specs=[_bs((tq, V_DIM), lambda h, i: (i, h)), _bs((None, tq, LANES), lambda h, i: (h, i, 0))],
        out_shape=[jax.ShapeDtypeStruct((s, H_A * V_DIM), F32), jax.ShapeDtypeStruct((H_A, s, LANES), F32)],
        scratch_shapes=[pltpu.VMEM((tq, LANES), F32), pltpu.VMEM((tq, LANES), F32), pltpu.VMEM((tq, V_DIM), F32)],
        compiler_params=_arb(2),
    )(q, k, v)


def _mla_bwd(q, k, v, do, o, lse, cos_t, sin_t, tq=512, tk=512):
    s = q.shape[1]
    nq = s // tq
    nkb = s // tk

    def body(q_ref, k_ref, v_ref, do_ref, o_ref, lse_ref, cos_ref, sin_ref, dqp_ref, dkvp_ref, dkr_ref,
             delta_ref, dq_ref, dk_ref, dv_ref):
        @pl.when(pl.program_id(1) == 0)
        def _():
            def init(c, carry):
                rows = pl.ds(pl.multiple_of(c * tq, tq), tq)
                delta = jnp.sum(do_ref[rows, :] * o_ref[rows, :], axis=-1, keepdims=True)
                delta_ref[rows, :] = jnp.broadcast_to(delta, (tq, LANES))
                dq_ref[rows, :] = jnp.zeros((tq, HEAD_PAD), F32)
                return carry

            lax.fori_loop(0, nq, init, 0)

        dk_ref[...] = jnp.zeros(dk_ref.shape, F32)
        dv_ref[...] = jnp.zeros(dv_ref.shape, F32)
        kb = k_ref[...]
        vb = v_ref[...]

        def step(c, carry):
            rows = pl.ds(pl.multiple_of(c * tq, tq), tq)
            qc = q_ref[rows, :]
            doc = do_ref[rows, :].astype(BF16)
            raw = lax.dot_general(qc, kb, NT, preferred_element_type=F32)
            dp = lax.dot_general(doc, vb, NT, preferred_element_type=F32)
            lse2 = lse_ref[rows, :] * LOG2E
            delta = delta_ref[rows, :]
            ps = [jnp.exp2(t * MLA_EXP2_SCALE - lse2) for t in _lane_tiles(raw)]
            dss = [pj * (dj - delta) * MLA_SCALE for pj, dj in zip(ps, _lane_tiles(dp))]
            p = jnp.concatenate(ps, axis=1).astype(BF16)
            ds = jnp.concatenate(dss, axis=1).astype(BF16)
            dv_ref[...] += lax.dot_general(p, doc, TN, preferred_element_type=F32)
            dk_ref[...] += lax.dot_general(ds, qc, TN, preferred_element_type=F32)
            dq_ref[rows, :] += jnp.dot(ds, kb, preferred_element_type=F32)
            return carry

        lax.fori_loop(0, nq, step, 0, unroll=True)
        dkvp_ref[:, :LANES] = dk_ref[:, :LANES].astype(BF16)
        dkvp_ref[:, LANES:] = dv_ref[...].astype(BF16)
        dkr_ref[...] = dk_ref[:, LANES:]

        @pl.when(pl.program_id(1) == nkb - 1)
        def _():
            def finish(c, carry):
                rows = pl.ds(pl.multiple_of(c * tq, tq), tq)
                dqp_ref[rows, :LANES] = dq_ref[rows, :LANES].astype(BF16)
                dqp_ref[rows, LANES:] = _rope_bwd(dq_ref[rows, LANES:], cos_ref[rows, :], sin_ref[rows, :]).astype(BF16)
                return carry

            lax.fori_loop(0, nq, finish, 0)

    whole = lambda w: _bs((s, w), lambda h, j: (0, 0))
    return pl.pallas_call(
        body, name="mla_bwd", grid=(H_A, nkb),
        in_specs=[_bs((None, s, HEAD_PAD), lambda h, j: (h, 0, 0)),
                  _bs((None, tk, HEAD_PAD), lambda h, j: (h, j, 0)),
                  _bs((None, tk, V_DIM), lambda h, j: (h, j, 0)),
                  _bs((s, V_DIM), lambda h, j: (0, h)), _bs((s, V_DIM), lambda h, j: (0, h)),
                  _bs((None, s, LANES), lambda h, j: (h, 0, 0)), whole(LANES), whole(LANES)],
        out_specs=[_bs((None, s, HEAD_PAD), lambda h, j: (h, 0, 0)),
                   _bs((None, tk, HEAD_PAD), lambda h, j: (h, j, 0)),
                   _bs((None, tk, LANES), lambda h, j: (h, j, 0))],
        out_shape=[jax.ShapeDtypeStruct((H_A, s, HEAD_PAD), BF16), jax.ShapeDtypeStruct((H_A, s, HEAD_PAD), BF16),
                   jax.ShapeDtypeStruct((H_A, s, LANES), F32)],
        scratch_shapes=[pltpu.VMEM((s, LANES), F32), pltpu.VMEM((s, HEAD_PAD), F32), pltpu.VMEM((tk, HEAD_PAD), F32),
                        pltpu.VMEM((tk, V_DIM), F32)],
        compiler_params=_arb(2),
    )(q, k, v, do, o, lse, cos_t, sin_t)


def _mla_key_rope_grad(dkr, cos_t, sin_t, ts=512):
    s = dkr.shape[1]

    def body(d_ref, cos_ref, sin_ref, o_ref):
        tot = d_ref[0]
        for h in range(1, H_A):
            tot = tot + d_ref[h]
        o_ref[...] = _rope_bwd(tot, cos_ref[...], sin_ref[...]).astype(BF16)

    rows = _bs((ts, LANES), lambda i: (i, 0))
    return pl.pallas_call(
        body, name="mla_key_rope_grad", grid=(s // ts,),
        in_specs=[_bs((H_A, ts, LANES), lambda i: (0, i, 0)), rows, rows], out_specs=rows,
        out_shape=jax.ShapeDtypeStruct((s, LANES), BF16), compiler_params=_arb(1),
    )(dkr, cos_t, sin_t)


WIN_SCALE = 1.0 / math.sqrt(HD_B)
SPAN = Q_BLOCK + 2 * WINDOW


def _t5_bucket_table():
    a = jnp.arange(Q_BLOCK, dtype=jnp.int32)[:, None]
    c = jnp.arange(SPAN, dtype=jnp.int32)[None, :]
    rel = c - WINDOW - a
    nb = NUM_BUCKETS // 2
    max_exact = nb // 2
    base = (rel > 0).astype(jnp.int32) * nb
    n = jnp.abs(rel)
    nf = jnp.maximum(n, 1).astype(F32)
    large = max_exact + (jnp.log(nf / max_exact) / math.log(MAX_DISTANCE / max_exact)
                         * (nb - max_exact)).astype(jnp.int32)
    large = jnp.minimum(large, nb - 1)
    return base + jnp.where(n < max_exact, n, large)


def _win_bias(bucket, rel_bias):
    def body(rb_ref, bk_ref, o_ref):
        h = pl.program_id(0)
        bk = bk_ref[...]
        acc = jnp.zeros((Q_BLOCK, SPAN), F32)
        for b in range(NUM_BUCKETS):
            acc = jnp.where(bk == b, rb_ref[b, h], acc)
        o_ref[...] = acc

    return pl.pallas_call(
        body, name="win_bias", grid=(H_B,),
        in_specs=[pl.BlockSpec(memory_space=pltpu.SMEM), _bs((Q_BLOCK, SPAN), lambda h: (0, 0))],
        out_specs=_bs((None, Q_BLOCK, SPAN), lambda h: (h, 0, 0)),
        out_shape=jax.ShapeDtypeStruct((H_B, Q_BLOCK, SPAN), F32),
        compiler_params=_arb(1),
    )(rel_bias, bucket)


WIN_HEADS = GROUP


def _win_kv_rows(n, j, nblk):
    blk = jnp.clip(n + j - 1, 0, nblk - 1)
    return pl.ds(pl.multiple_of(blk * Q_BLOCK, Q_BLOCK), Q_BLOCK)


def _win_kv_cols(h0):
    kv = h0 // GROUP
    return slice(kv * HD_B, (kv + 1) * HD_B)


def _win_stack(ref, h0):
    return jnp.concatenate([ref[:, (h0 + g) * HD_B:(h0 + g + 1) * HD_B] for g in range(WIN_HEADS)], axis=0)


def _win_unstack(ref, h0, val):
    for g in range(WIN_HEADS):
        ref[:, (h0 + g) * HD_B:(h0 + g + 1) * HD_B] = val[g * Q_BLOCK:(g + 1) * Q_BLOCK].astype(ref.dtype)


def _win_scores(q, k_ref, h0, bias_ref, n, nblk):
    a = lax.broadcasted_iota(jnp.int32, (WIN_HEADS, Q_BLOCK, Q_BLOCK), 1)
    cc = lax.broadcasted_iota(jnp.int32, (WIN_HEADS, Q_BLOCK, Q_BLOCK), 2)
    valid = [(cc >= a) & (n > 0), None, (cc <= a) & (n < nblk - 1)]
    out = []
    for j in range(3):
        sc = lax.dot_general(q, k_ref[_win_kv_rows(n, j, nblk), _win_kv_cols(h0)], NT, preferred_element_type=F32)
        sc = (sc.reshape(WIN_HEADS, Q_BLOCK, Q_BLOCK) * WIN_SCALE
              + bias_ref[h0:h0 + WIN_HEADS, :, j * Q_BLOCK:(j + 1) * Q_BLOCK])
        if valid[j] is not None:
            sc = jnp.where(valid[j], sc, -1e30)
        out.append(sc)
    return out


def _win_sink(sink_ref, h0):
    hs = lax.broadcasted_iota(jnp.int32, (WIN_HEADS, Q_BLOCK, 1), 0)
    sk = jnp.zeros((WIN_HEADS, Q_BLOCK, 1), F32)
    for g in range(WIN_HEADS):
        sk = jnp.where(hs == g, sink_ref[h0 + g], sk)
    return sk


def _win_fwd(proj_b, bias, sinks):
    s = proj_b.shape[0]
    nblk = s // Q_BLOCK
    rows = WIN_HEADS * Q_BLOCK

    def body(sink_ref, q_ref, k_ref, v_ref, bias_ref, o_ref, lse_ref):
        n = pl.program_id(0)
        for h0 in range(0, H_B, WIN_HEADS):
            sk = _win_sink(sink_ref, h0)
            q = _win_stack(q_ref, h0)
            ss = _win_scores(q, k_ref, h0, bias_ref, n, nblk)
            m = jnp.maximum(jnp.max(jnp.maximum(jnp.maximum(ss[0], ss[1]), ss[2]), axis=2, keepdims=True), sk)
            es = [jnp.exp(sc - m) for sc in ss]
            l = jnp.sum(es[0] + es[1] + es[2], axis=2, keepdims=True) + jnp.exp(sk - m)
            acc = jnp.zeros((rows, HD_B), F32)
            for j, e in enumerate(es):
                p = (e / l).astype(BF16).reshape(rows, Q_BLOCK)
                acc = acc + jnp.dot(p, v_ref[_win_kv_rows(n, j, nblk), _win_kv_cols(h0)],
                                    preferred_element_type=F32)
            _win_unstack(o_ref, h0, acc)
            lse_ref[h0:h0 + WIN_HEADS] = m + jnp.log(l)

    kv_w = KV_B * HD_B
    return pl.pallas_call(
        body, name="win_fwd", grid=(nblk,),
        in_specs=[pl.BlockSpec(memory_space=pltpu.SMEM), _bs((Q_BLOCK, H_B * HD_B), lambda n: (n, P_QB // (H_B * HD_B))),
                  _bs((s, kv_w), lambda n: (0, P_KB // kv_w)), _bs((s, kv_w), lambda n: (0, P_VB // kv_w)),
                  _bs((H_B, Q_BLOCK, SPAN), lambda n: (0, 0, 0))],
        out_specs=[_bs((Q_BLOCK, H_B * HD_B), lambda n: (n, 0)), _bs((H_B, Q_BLOCK, 1), lambda n: (0, n, 0))],
        out_shape=[jax.ShapeDtypeStruct((s, H_B * HD_B), F32), jax.ShapeDtypeStruct((H_B, s, 1), F32)],
        compiler_params=_arb(1),
    )(sinks, proj_b, proj_b, proj_b, bias)


def _win_bwd(proj_b, bias, sinks, do_b, lse):
    s = proj_b.shape[0]
    nblk = s // Q_BLOCK
    rows = WIN_HEADS * Q_BLOCK
    spad = s + 2 * WINDOW

    def body(sink_ref, q_ref, k_ref, v_ref, bias_ref, do_ref, lse_ref, dq_ref, dk_ref, dv_ref, db_ref, dsk_ref):
        n = pl.program_id(0)

        @pl.when(n == 0)
        def _():
            dk_ref[...] = jnp.zeros(dk_ref.shape, F32)
            dv_ref[...] = jnp.zeros(dv_ref.shape, F32)
            db_ref[...] = jnp.zeros(db_ref.shape, F32)
            dsk_ref[...] = jnp.zeros(dsk_ref.shape, F32)

        for h0 in range(0, H_B, WIN_HEADS):
            heads = slice(h0, h0 + WIN_HEADS)
            sk = _win_sink(sink_ref, h0)
            q = _win_stack(q_ref, h0)
            dob = _win_stack(do_ref, h0)
            lse_v = lse_ref[heads]
            ss = _win_scores(q, k_ref, h0, bias_ref, n, nblk)
            ps = [jnp.exp(sc - lse_v) for sc in ss]
            dps = [lax.dot_general(dob, v_ref[_win_kv_rows(n, j, nblk), _win_kv_cols(h0)], NT,
                                   preferred_element_type=F32).reshape(WIN_HEADS, Q_BLOCK, Q_BLOCK) for j in range(3)]
            delta = jnp.sum(ps[0] * dps[0] + ps[1] * dps[1] + ps[2] * dps[2], axis=2, keepdims=True)
            dq = jnp.zeros((rows, HD_B), F32)
            for j in range(3):
                ds = ps[j] * (dps[j] - delta)
                db_ref[heads, :, j * Q_BLOCK:(j + 1) * Q_BLOCK] += ds
                dsb = (ds * WIN_SCALE).astype(BF16).reshape(rows, Q_BLOCK)
                dq = dq + jnp.dot(dsb, k_ref[_win_kv_rows(n, j, nblk), _win_kv_cols(h0)],
                                  preferred_element_type=F32)
                krows = pl.ds(pl.multiple_of((n + j) * Q_BLOCK, Q_BLOCK), Q_BLOCK)
                dk_ref[krows, _win_kv_cols(h0)] += lax.dot_general(dsb, q, TN, preferred_element_type=F32)
                dv_ref[krows, _win_kv_cols(h0)] += lax.dot_general(
                    ps[j].astype(BF16).reshape(rows, Q_BLOCK), dob, TN, preferred_element_type=F32)
            dsk_ref[heads] += -(jnp.exp(sk - lse_v) * delta)
            _win_unstack(dq_ref, h0, dq)

    kv_w = KV_B * HD_B
    qspec = _bs((Q_BLOCK, H_B * HD_B), lambda n: (n, 0))
    kacc = _bs((spad, kv_w), lambda n: (0, 0))
    return pl.pallas_call(
        body, name="win_bwd", grid=(nblk,),
        in_specs=[pl.BlockSpec(memory_space=pltpu.SMEM), _bs((Q_BLOCK, H_B * HD_B), lambda n: (n, P_QB // (H_B * HD_B))),
                  _bs((s, kv_w), lambda n: (0, P_KB // kv_w)), _bs((s, kv_w), lambda n: (0, P_VB // kv_w)),
                  _bs((H_B, Q_BLOCK, SPAN), lambda n: (0, 0, 0)), qspec, _bs((H_B, Q_BLOCK, 1), lambda n: (0, n, 0))],
        out_specs=[qspec, kacc, kacc, _bs((H_B, Q_BLOCK, SPAN), lambda n: (0, 0, 0)),
                   _bs((H_B, Q_BLOCK, 1), lambda n: (0, 0, 0))],
        out_shape=[jax.ShapeDtypeStruct((s, H_B * HD_B), BF16), jax.ShapeDtypeStruct((spad, kv_w), F32),
                   jax.ShapeDtypeStruct((spad, kv_w), F32), jax.ShapeDtypeStruct((H_B, Q_BLOCK, SPAN), F32),
                   jax.ShapeDtypeStruct((H_B, Q_BLOCK, 1), F32)],
        compiler_params=_arb(1),
    )(sinks, proj_b, proj_b, proj_b, bias, do_b, lse)


def _win_param_grads(bucket, dbias, dsink_rows):
    def body(bk_ref, db_ref, ds_ref, o_ref):
        bk = bk_ref[...]
        dbv = db_ref[...]
        lane = lax.broadcasted_iota(jnp.int32, (1, LANES), 1)
        res = jnp.zeros((1, LANES), F32)
        for b in range(NUM_BUCKETS):
            tot = jnp.sum(jnp.sum(jnp.where(bk == b, dbv, 0.0), axis=1, keepdims=True), axis=0, keepdims=True)
            res = jnp.where(lane == b, tot, res)
        stot = jnp.sum(ds_ref[...], axis=0, keepdims=True)
        o_ref[...] = jnp.where(lane == NUM_BUCKETS, stot, res)

    return pl.pallas_call(
        body, name="win_param_grads", grid=(H_B,),
        in_specs=[_bs((Q_BLOCK, SPAN), lambda h: (0, 0)), _bs((None, Q_BLOCK, SPAN), lambda h: (h, 0, 0)),
                  _bs((None, Q_BLOCK, 1), lambda h: (h, 0, 0))],
        out_specs=_bs((None, 1, LANES), lambda h: (h, 0, 0)),
        out_shape=jax.ShapeDtypeStruct((H_B, 1, LANES), F32),
        compiler_params=_arb(1),
    )(bucket, dbias, dsink_rows)


def _mix_out_norm(proj, o_a, o_b, w_out, x, g2, ts=512):
    s = o_a.shape[0]
    wide = lambda cb: _bs((ts, D_MODEL), lambda i: (i, cb))

    def body(ga_ref, gb_ref, oa_ref, ob_ref, w_ref, x_ref, g_ref, m_ref, x1_ref, h_ref, r_ref):
        mixed = (jax.nn.sigmoid(ga_ref[...]) * oa_ref[...] + jax.nn.sigmoid(gb_ref[...]) * ob_ref[...]).astype(BF16)
        m_ref[...] = mixed
        x1 = x_ref[...] + jnp.dot(mixed, w_ref[...], preferred_element_type=F32)
        x1_ref[...] = x1
        r = lax.rsqrt(jnp.mean(x1 * x1, axis=-1, keepdims=True) + EPS)
        h_ref[...] = (x1 * r * g_ref[...]).astype(BF16)
        r_ref[...] = r

    return pl.pallas_call(
        body, name="mix_out_norm", grid=(s // ts,),
        in_specs=[wide(P_GA // D_MODEL), wide(P_GB // D_MODEL), wide(0), wide(0),
                  _bs((D_MODEL, D_MODEL), lambda i: (0, 0)), wide(0), _bs((1, D_MODEL), lambda i: (0, 0))],
        out_specs=[wide(0), wide(0), wide(0), _bs((ts, 1), lambda i: (i, 0))],
        out_shape=[jax.ShapeDtypeStruct((s, D_MODEL), BF16), jax.ShapeDtypeStruct((s, D_MODEL), F32),
                   jax.ShapeDtypeStruct((s, D_MODEL), BF16), jax.ShapeDtypeStruct((s, 1), F32)],
        compiler_params=_arb(1),
    )(proj, proj, o_a, o_b, w_out, x, g2)


def _mix_out_bwd(dx1_b, w_out, proj, o_a, o_b, after=None, ts=512):
    s = o_a.shape[0]
    wide = lambda cb: _bs((ts, D_MODEL), lambda i: (i, cb))

    def body(dx_ref, w_ref, ga_ref, gb_ref, oa_ref, ob_ref, *rest):
        doa_ref, dob_ref, dga_ref, dgb_ref = rest[-4:]
        dm = lax.dot_general(dx_ref[...], w_ref[...], NT, preferred_element_type=F32)
        sa = jax.nn.sigmoid(ga_ref[...])
        sb = jax.nn.sigmoid(gb_ref[...])
        doa_ref[...] = dm * sa
        dob_ref[...] = (dm * sb).astype(BF16)
        dga_ref[...] = (dm * oa_ref[...] * (sa * (1.0 - sa))).astype(BF16)
        dgb_ref[...] = (dm * ob_ref[...] * (sb * (1.0 - sb))).astype(BF16)

    extra = [after] if after is not None else []
    return pl.pallas_call(
        body, name="mix_out_bwd", grid=(s // ts,),
        in_specs=[wide(0), _bs((D_MODEL, D_MODEL), lambda i: (0, 0)), wide(P_GA // D_MODEL), wide(P_GB // D_MODEL),
                  wide(0), wide(0)] + [pl.BlockSpec(memory_space=pl.ANY)] * len(extra),
        out_specs=[wide(0)] * 4,
        out_shape=[jax.ShapeDtypeStruct((s, D_MODEL), F32), jax.ShapeDtypeStruct((s, D_MODEL), BF16),
                   jax.ShapeDtypeStruct((s, D_MODEL), BF16), jax.ShapeDtypeStruct((s, D_MODEL), BF16)],
        compiler_params=_arb(1),
    )(dx1_b, w_out, proj, proj, o_a, o_b, *extra)


CONV_CHUNK = 128
N_SLAB = D_FF // LANES


def _shifted(ref, c, nchunks):
    r0 = c * CONV_CHUNK
    cur = ref[r0:r0 + CONV_CHUNK, :]
    row = lax.broadcasted_iota(jnp.int32, (8, LANES), 0)
    if c > 0:
        prev = ref[r0 - 1:r0 - 1 + CONV_CHUNK, :]
    else:
        down = pltpu.roll(cur, 1, 0)
        prev = jnp.concatenate([jnp.where(row == 0, 0.0, down[:8]), down[8:]], axis=0)
    if c < nchunks - 1:
        nxt = ref[r0 + 1:r0 + 1 + CONV_CHUNK, :]
    else:
        up = pltpu.roll(cur, CONV_CHUNK - 1, 0)
        nxt = jnp.concatenate([up[:-8], jnp.where(row == 7, 0.0, up[-8:])], axis=0)
    return prev, cur, nxt


def _conv_taps(ref, w_ref, b_ref, c, nchunks):
    prev, cur, nxt = _shifted(ref, c, nchunks)
    conv = prev * w_ref[0:1, :] + cur * w_ref[1:2, :] + nxt * w_ref[2:3, :] + b_ref[...]
    return conv, prev, cur, nxt


def _convffn_fwd(u, conv_w, conv_b):
    s = u.shape[0]
    nchunks = s // CONV_CHUNK

    def body(ug_ref, uv_ref, wg_ref, wv_ref, bg_ref, bv_ref, f_ref):
        for c in range(nchunks):
            cg = _conv_taps(ug_ref, wg_ref, bg_ref, c, nchunks)[0]
            cv = _conv_taps(uv_ref, wv_ref, bv_ref, c, nchunks)[0]
            f_ref[c * CONV_CHUNK:(c + 1) * CONV_CHUNK, :] = (cg * jax.nn.sigmoid(cg) * cv).astype(BF16)

    slab = lambda off: _bs((s, LANES), lambda j: (0, off + j))
    wsl = lambda off: _bs((3, LANES), lambda j: (0, off + j))
    bsl = lambda off: _bs((1, LANES), lambda j: (0, off + j))
    return pl.pallas_call(
        body, name="convffn_fwd", grid=(N_SLAB,),
        in_specs=[slab(0), slab(N_SLAB), wsl(0), wsl(N_SLAB), bsl(0), bsl(N_SLAB)],
        out_specs=slab(0), out_shape=jax.ShapeDtypeStruct((s, D_FF), BF16),
        compiler_params=_arb(1),
    )(u, u, conv_w, conv_w, conv_b, conv_b)


def _convffn_bwd(u, conv_w, conv_b, df):
    s = u.shape[0]
    nchunks = s // CONV_CHUNK

    def body(ug_ref, uv_ref, wg_ref, wv_ref, bg_ref, bv_ref, df_ref, du_ref, dw_ref, db_ref, dcg_ref, dcv_ref):
        dwg = [jnp.zeros((1, LANES), F32) for _ in range(3)]
        dwv = [jnp.zeros((1, LANES), F32) for _ in range(3)]
        dbg = jnp.zeros((1, LANES), F32)
        dbv = jnp.zeros((1, LANES), F32)
        for c in range(nchunks):
            rows = slice(c * CONV_CHUNK, (c + 1) * CONV_CHUNK)
            cg, gp, gc, gn = _conv_taps(ug_ref, wg_ref, bg_ref, c, nchunks)
            cv, vp, vc, vn = _conv_taps(uv_ref, wv_ref, bv_ref, c, nchunks)
            dfv = df_ref[rows, :]
            sg = jax.nn.sigmoid(cg)
            dcg = dfv * cv * (sg * (1.0 + cg * (1.0 - sg)))
            dcv = dfv * (cg * sg)
            dcg_ref[rows, :] = dcg
            dcv_ref[rows, :] = dcv
            for t, (tg, tv) in enumerate(((gp, vp), (gc, vc), (gn, vn))):
                dwg[t] = dwg[t] + jnp.sum(tg * dcg, axis=0, keepdims=True)
                dwv[t] = dwv[t] + jnp.sum(tv * dcv, axis=0, keepdims=True)
            dbg = dbg + jnp.sum(dcg, axis=0, keepdims=True)
            dbv = dbv + jnp.sum(dcv, axis=0, keepdims=True)
        for t in range(3):
            dw_ref[0, t:t + 1, :] = dwg[t]
            dw_ref[1, t:t + 1, :] = dwv[t]
        db_ref[0] = dbg
        db_ref[1] = dbv
        for half, (dc_ref, w_ref) in enumerate(((dcg_ref, wg_ref), (dcv_ref, wv_ref))):
            for c in range(nchunks):
                prev, cur, nxt = _shifted(dc_ref, c, nchunks)
                du = nxt * w_ref[0:1, :] + cur * w_ref[1:2, :] + prev * w_ref[2:3, :]
                du_ref[half, c * CONV_CHUNK:(c + 1) * CONV_CHUNK, :] = du.astype(BF16)

    slab = lambda off: _bs((s, LANES), lambda j: (0, off + j))
    wsl = lambda off: _bs((3, LANES), lambda j: (0, off + j))
    bsl = lambda off: _bs((1, LANES), lambda j: (0, off + j))
    return pl.pallas_call(
        body, name="convffn_bwd", grid=(N_SLAB,),
        in_specs=[slab(0), slab(N_SLAB), wsl(0), wsl(N_SLAB), bsl(0), bsl(N_SLAB), slab(0)],
        out_specs=[_bs((2, s, LANES), lambda j: (0, 0, j)), _bs((2, 3, LANES), lambda j: (0, 0, j)),
                   _bs((2, 1, LANES), lambda j: (0, 0, j))],
        out_shape=[jax.ShapeDtypeStruct((2, s, D_FF), BF16), jax.ShapeDtypeStruct((2, 3, D_FF), F32),
                   jax.ShapeDtypeStruct((2, 1, D_FF), F32)],
        scratch_shapes=[pltpu.VMEM((s, LANES), F32), pltpu.VMEM((s, LANES), F32)],
        compiler_params=_arb(1),
    )(u, u, conv_w, conv_w, conv_b, conv_b, df)


def _row_tile(rows, limit=512):
    best = rows
    for t in range(8, min(rows, limit) + 1, 8):
        if rows % t == 0:
            best = t
    return best if rows % 8 == 0 else rows


ADAM_C1 = 1.0 - ADAM_B1 ** ADAM_STEP
ADAM_C2 = 1.0 - ADAM_B2 ** ADAM_STEP


def _adamw_math(w, gv, m, v):
    nm = ADAM_B1 * m + (1.0 - ADAM_B1) * gv
    nv = ADAM_B2 * v + (1.0 - ADAM_B2) * (gv * gv)
    m_hat = nm / ADAM_C1
    v_hat = nv / ADAM_C2
    return -ADAM_LR * (m_hat / (jnp.sqrt(v_hat) + ADAM_EPS) + ADAM_WD * w), nm, nv


def _adamw_halves(name, core, w, mine, theirs, m, v):
    half, cols = mine.shape
    tr = _row_tile(half)
    nr = half // tr

    def body(core_ref, w_ref, mine_ref, theirs_ref, m_ref, v_ref, g_ref, d_ref, nm_ref, nv_ref):
        gv = jnp.where(pl.program_id(0) == core_ref[0], mine_ref[...], theirs_ref[...])
        g_ref[...] = gv
        d_ref[...], nm_ref[...], nv_ref[...] = _adamw_math(w_ref[...], gv, m_ref[...], v_ref[...])

    full = pl.BlockSpec((tr, cols), lambda hf, r, cr: (hf * nr + r, 0))
    part = pl.BlockSpec((tr, cols), lambda hf, r, cr: (r, 0))
    return pl.pallas_call(
        body, name=name,
        grid_spec=pltpu.PrefetchScalarGridSpec(num_scalar_prefetch=1, grid=(2, nr),
                                               in_specs=[full, part, part, full, full], out_specs=[full] * 4),
        out_shape=[jax.ShapeDtypeStruct((2 * half, cols), F32)] * 4, compiler_params=_arb(2),
    )(core, w, mine, theirs, m, v)


ANY = pl.BlockSpec(memory_space=pl.ANY)


def _mesh_pos():
    return lax.axis_index("x"), lax.axis_index("y"), lax.axis_index("c")


def _other_chips(x, y):
    return [(1 - x, y), (x, 1 - y), (1 - x, 1 - y)]


def _forward_to_sibling(gathered):
    n = len(gathered)

    def body(*refs):
        in_refs, out_refs = refs[:n], refs[n:2 * n]
        send_sems, recv_sems = refs[2 * n:]
        x, y, c = _mesh_pos()
        cps = []
        for i in range(n):
            for k, chip in enumerate(_other_chips(x, y)):
                pk = 2 * chip[0] + chip[1]
                sems = dict(send_sem=send_sems.at[3 * i + k], recv_sem=recv_sems.at[3 * i + k],
                            device_id=(x, y, 1 - c), device_id_type=MESH)
                sent = pltpu.make_async_remote_copy(src_ref=in_refs[i].at[pk, c], dst_ref=out_refs[i].at[pk, c], **sems)
                sent.start()
                theirs = out_refs[i].at[pk, 1 - c]
                cps.append((sent, pltpu.make_async_remote_copy(src_ref=theirs, dst_ref=theirs, **sems)))
        for sent, arrived in cps:
            sent.wait_send()
            arrived.wait_recv()

    return pl.pallas_call(
        body, name="forward_to_sibling", in_specs=[ANY] * n, out_specs=[ANY] * n,
        out_shape=[jax.ShapeDtypeStruct(a.shape, a.dtype) for a in gathered],
        input_output_aliases={i: i for i in range(n)},
        scratch_shapes=[pltpu.SemaphoreType.DMA((3 * n,)), pltpu.SemaphoreType.DMA((3 * n,))],
    )(*gathered)


def _rs_pair_exchange(name, grads):
    n = len(grads)

    def body(*refs):
        g_refs, o_refs = refs[:n], refs[n:2 * n]
        send_sems, recv_sems = refs[2 * n:]
        x, y, c = _mesh_pos()
        cps = []
        for i in range(n):
            cp = pltpu.make_async_remote_copy(
                src_ref=g_refs[i].at[:, 1 - c], dst_ref=o_refs[i],
                send_sem=send_sems.at[i], recv_sem=recv_sems.at[i], device_id=(x, y, 1 - c), device_id_type=MESH)
            cp.start()
            cps.append(cp)
        for cp in cps:
            cp.wait()

    return pl.pallas_call(
        body, name=name, in_specs=[ANY] * n, out_specs=[ANY] * n,
        out_shape=[jax.ShapeDtypeStruct((4,) + g.shape[2:], F32) for g in grads],
        scratch_shapes=[pltpu.SemaphoreType.DMA((n,)), pltpu.SemaphoreType.DMA((n,))],
    )(*grads)


def _rs_pair_add(name, core, g, recv):
    _, half, cols = recv.shape
    tr = _row_tile(half)
    nr = half // tr

    def body(core_ref, g_ref, r_ref, o_ref):
        o_ref[...] = (g_ref[...] + r_ref[...]).astype(BF16)

    return pl.pallas_call(
        body, name=name,
        grid_spec=pltpu.PrefetchScalarGridSpec(
            num_scalar_prefetch=1, grid=(4, nr),
            in_specs=[pl.BlockSpec((None, None, tr, cols), lambda q, r, cr: (q, cr[0], r, 0)),
                      pl.BlockSpec((None, tr, cols), lambda q, r, cr: (q, r, 0))],
            out_specs=pl.BlockSpec((None, tr, cols), lambda q, r, cr: (q, r, 0))),
        out_shape=jax.ShapeDtypeStruct((4, half, cols), BF16),
        compiler_params=_arb(2),
    )(core, g, recv)


def _rs_final_add(name, chip, pair, recv):
    _, half, cols = pair.shape
    tr = _row_tile(half)

    def body(chip_ref, p_ref, r_ref, o_ref):
        o_ref[...] = ((p_ref[...].astype(F32) + r_ref[0].astype(F32)) + r_ref[1].astype(F32)) + r_ref[2].astype(F32)

    return pl.pallas_call(
        body, name=name,
        grid_spec=pltpu.PrefetchScalarGridSpec(
            num_scalar_prefetch=1, grid=(half // tr,),
            in_specs=[pl.BlockSpec((None, tr, cols), lambda r, ch: (ch[0], r, 0)),
                      pl.BlockSpec((3, tr, cols), lambda r, ch: (0, r, 0))],
            out_specs=pl.BlockSpec((tr, cols), lambda r, ch: (r, 0))),
        out_shape=jax.ShapeDtypeStruct((half, cols), F32),
        compiler_params=_arb(1),
    )(chip, pair, recv)


def _rs_pair_share(halves):
    n = len(halves)

    def body(*refs):
        h_refs, o_refs = refs[:n], refs[n:2 * n]
        send_sems, recv_sems = refs[2 * n:]
        x, y, c = _mesh_pos()
        cps = []
        for i in range(n):
            cp = pltpu.make_async_remote_copy(src_ref=h_refs[i], dst_ref=o_refs[i], send_sem=send_sems.at[i],
                                              recv_sem=recv_sems.at[i], device_id=(x, y, 1 - c), device_id_type=MESH)
            cp.start()
            cps.append(cp)
        for cp in cps:
            cp.wait()

    return pl.pallas_call(
        body, name="rs_pair_share", in_specs=[ANY] * n, out_specs=[ANY] * n,
        out_shape=[jax.ShapeDtypeStruct(h.shape, F32) for h in halves],
        scratch_shapes=[pltpu.SemaphoreType.DMA((n,)), pltpu.SemaphoreType.DMA((n,))],
    )(*halves)


HBM = pl.BlockSpec(memory_space=pltpu.HBM)
SEM = pl.BlockSpec(memory_space=pltpu.SEMAPHORE)


class _SplitExchange:
    def __init__(self, name, srcs, land_shapes, src_of, dst_of, arrive_of, to_sibling=False):
        self.name, self.srcs, self.land_shapes = name, list(srcs), list(land_shapes)
        self.src_of, self.dst_of, self.arrive_of = src_of, dst_of, arrive_of
        self.to_sibling = to_sibling
        self.fan = 1 if to_sibling else 3

    def _copies(self, src_refs, land_refs, send_sems, recv_sems):
        x, y, c = _mesh_pos()
        p = 2 * x + y
        if self.to_sibling:
            peers = [((x, y, 1 - c), 1 - c)]
        else:
            peers = [((*chip, c), 2 * chip[0] + chip[1]) for chip in _other_chips(x, y)]
        out = []
        for i, (src, land) in enumerate(zip(src_refs, land_refs)):
            for k, (peer, pk) in enumerate(peers):
                sems = dict(send_sem=send_sems.at[self.fan * i + k], recv_sem=recv_sems.at[self.fan * i + k],
                            device_id=peer, device_id_type=MESH)
                sent = pltpu.make_async_remote_copy(src_ref=self.src_of(src, k, p, pk, c),
                                                    dst_ref=self.dst_of(land, k, p, pk, c), **sems)
                here = self.arrive_of(land, k, p, pk, c)
                out.append((sent, pltpu.make_async_remote_copy(src_ref=here, dst_ref=here, **sems)))
        return out

    def start(self, after=None):
        n = len(self.srcs)
        n_in = 2 * n + (after is not None)

        def body(*refs):
            for sent, _ in self._copies(refs[:n], refs[n:2 * n], refs[n_in], refs[n_in + 1]):
                sent.start()
            refs[-1][...] = jnp.zeros((8, LANES), F32)

        lands = [lax.empty(shape, src.dtype) for shape, src in zip(self.land_shapes, self.srcs)]
        operands = [pltpu.with_memory_space_constraint(a, pltpu.HBM) for a in self.srcs + lands]
        outs = pl.pallas_call(
            body, name=self.name + "_start",
            out_shape=(pltpu.SemaphoreType.DMA((self.fan * n,)), pltpu.SemaphoreType.DMA((self.fan * n,)),
                       *[pltpu.HBM(a.shape, a.dtype) for a in operands], jax.ShapeDtypeStruct((8, LANES), F32)),
            in_specs=[HBM] * (2 * n) + [ANY] * (after is not None),
            out_specs=(SEM, SEM, *[HBM] * (2 * n), pl.BlockSpec(memory_space=pltpu.VMEM)),
            input_output_aliases={j: 2 + j for j in range(2 * n)},
            compiler_params=pltpu.CompilerParams(has_side_effects=pltpu.SideEffectType.DATAFLOW_SIDE_EFFECTING),
        )(*operands, *([after] if after is not None else []))
        self._sems, self._thru = outs[:2], list(outs[2:2 + 2 * n])
        return outs[-1]

    def wait(self, after):
        n = len(self.srcs)

        def body(*refs):
            for sent, arrived in self._copies(refs[:n], refs[n:2 * n], refs[2 * n], refs[2 * n + 1]):
                sent.wait_send()
                arrived.wait_recv()

        after = list(after) if isinstance(after, (list, tuple)) else [after]
        outs = pl.pallas_call(
            body, name=self.name + "_wait",
            out_shape=tuple(pltpu.HBM(a.shape, a.dtype) for a in self._thru),
            in_specs=[HBM] * (2 * n) + [SEM, SEM] + [ANY] * len(after), out_specs=tuple([HBM] * (2 * n)),
            input_output_aliases={j: j for j in range(2 * n)},
            compiler_params=pltpu.CompilerParams(has_side_effects=pltpu.SideEffectType.DATAFLOW_SIDE_EFFECTING),
        )(*self._thru, *self._sems, *after)
        return list(outs[:n]), list(outs[n:])


def _small_allreduce(parts):
    n = len(parts)

    def body(*refs):
        in_refs, out_refs, gather_refs = refs[:n], refs[n:2 * n], refs[2 * n:3 * n]
        send_sems, recv_sems = refs[3 * n:]
        x, y, c = _mesh_pos()
        me = 4 * x + 2 * y + c
        cps = []
        for i in range(n):
            gather_refs[i][me] = in_refs[i][...]
            for j in range(1, 8):
                peer = (x ^ (j >> 2), y ^ ((j >> 1) & 1), c ^ (j & 1))
                cp = pltpu.make_async_remote_copy(
                    src_ref=in_refs[i], dst_ref=gather_refs[i].at[me], send_sem=send_sems.at[7 * i + j - 1],
                    recv_sem=recv_sems.at[7 * i + j - 1], device_id=peer, device_id_type=MESH)
                cp.start()
                cps.append(cp)
        for i in range(n):
            for j in range(1, 8):
                peer_id = 4 * (x ^ (j >> 2)) + 2 * (y ^ ((j >> 1) & 1)) + (c ^ (j & 1))
                slot = gather_refs[i].at[peer_id]
                pltpu.make_async_remote_copy(src_ref=slot, dst_ref=slot, send_sem=send_sems.at[7 * i + j - 1],
                                             recv_sem=recv_sems.at[7 * i + j - 1], device_id=(x, y, c),
                                             device_id_type=MESH).wait_recv()
        for cp in cps:
            cp.wait_send()
        for i in range(n):
            tot = gather_refs[i][0]
            for d in range(1, 8):
                tot = tot + gather_refs[i][d]
            out_refs[i][...] = tot

    vmem = pl.BlockSpec(memory_space=pltpu.VMEM)
    return pl.pallas_call(
        body, name="small_allreduce", in_specs=[vmem] * n, out_specs=[vmem] * n,
        out_shape=[jax.ShapeDtypeStruct(p.shape, F32) for p in parts],
        scratch_shapes=[pltpu.VMEM((8,) + p.shape, F32) for p in parts]
        + [pltpu.SemaphoreType.DMA((7 * n,)), pltpu.SemaphoreType.DMA((7 * n,))],
    )(*parts)


def _adamw_small(ws, gs, ms, vs):
    n = len(ws)

    def body(*refs):
        for i in range(n):
            w_ref, g_ref, m_ref, v_ref = refs[i], refs[n + i], refs[2 * n + i], refs[3 * n + i]
            d_ref, nm_ref, nv_ref = refs[4 * n + i], refs[5 * n + i], refs[6 * n + i]
            d_ref[...], nm_ref[...], nv_ref[...] = _adamw_math(w_ref[...], g_ref[...], m_ref[...], v_ref[...])

    vmem = pl.BlockSpec(memory_space=pltpu.VMEM)
    shapes = [jax.ShapeDtypeStruct(w.shape, F32) for w in ws]
    outs = pl.pallas_call(body, name="adamw_small", in_specs=[vmem] * (4 * n), out_specs=[vmem] * (3 * n),
                          out_shape=shapes * 3)(*ws, *gs, *ms, *vs)
    return outs[:n], outs[n:2 * n], outs[2 * n:]


W_IN_PIECES = ((0, 256, P_QLAT), (256, 384, P_CKV), (384, 448, P_KR), (448, 1472, P_QB), (1472, 1728, P_KB),
               (1728, 1984, P_VB), (1984, 3008, P_GA), (3008, 4032, P_GB))
W_IN_SHARD = 1008


def _w_in_from_shards(shards):
    cols = []
    for lo, hi, _ in sorted(W_IN_PIECES, key=lambda piece: piece[2]):
        for q in range(4):
            a, b = max(lo, q * W_IN_SHARD), min(hi, (q + 1) * W_IN_SHARD)
            if a < b:
                cols.append(shards[q][:, a - q * W_IN_SHARD:b - q * W_IN_SHARD])
    cols.append(jnp.zeros((shards.shape[1], W_IN_PAD - 4 * W_IN_SHARD), shards.dtype))
    return jnp.concatenate(cols, axis=1)


def _w_in_to_shards(p):
    shards = []
    for q in range(4):
        cols = []
        for lo, hi, at in W_IN_PIECES:
            a, b = max(lo, q * W_IN_SHARD), min(hi, (q + 1) * W_IN_SHARD)
            if a < b:
                cols.append(p[:, at + a - lo:at + b - lo])
        shards.append(jnp.concatenate(cols, axis=1))
    return jnp.stack(shards)


def _col_shards(w):
    r, c4 = w.shape
    return w.reshape(r, 4, c4 // 4).transpose(1, 0, 2)


def _local_step(x, positions, target, norm1_g, first_weights, q_a_norm_g, kv_a_norm_g, rel_bias, sinks,
                late_weights, norm2_g, conv_b, final_norm_g, early_grads=None, last_grads=None):
    s = x.shape[0]
    half = QK_ROPE // 2
    inv_freq = jnp.asarray(np.float32(ROPE_THETA) ** (-np.arange(half, dtype=np.float32) / np.float32(half)))
    ang = positions.astype(F32)[:, None] * inv_freq[None, :]
    cos, sin = jnp.cos(ang), jnp.sin(ang)
    z64 = jnp.zeros((s, 64), F32)
    cos_t = jnp.concatenate([cos, cos, z64], axis=1)
    sin_t = jnp.concatenate([-sin, sin, z64], axis=1)
    bucket = _t5_bucket_table()
    sinks1 = sinks.reshape(H_B)

    h1, rstd1 = _rmsnorm_fwd("norm1_fwd", x, norm1_g, D_MODEL, 0)
    bias = _win_bias(bucket, rel_bias)
    w_in_p, wq, wkv = first_weights([h1, bias, cos_t, sin_t])
    proj, proj_b = _matmul("proj", h1, w_in_p, out_shape=(s, W_IN_PAD), out_dtype=F32, grid=(s // MM_ROWS, W_IN_PAD // 1024, 1),
                           a_spec=_bs((MM_ROWS, D_MODEL), lambda i, j, k: (i, 0)), b_spec=_bs((D_MODEL, 1024), lambda i, j, k: (0, j)),
                           o_spec=_bs((MM_ROWS, 1024), lambda i, j, k: (i, j)), contract=NN, bf16_copy=True)
    qn, cn, rstd_q, rstd_c = _lat_norms(proj, q_a_norm_g, kv_a_norm_g)
    q = _q_heads(qn, wq, cos_t, sin_t)
    k, v = _kv_heads(cn, wkv, proj, cos_t, sin_t)
    o_a, lse_a = _mla_fwd(q, k, v)

    o_b, lse_b = _win_fwd(proj_b, bias, sinks1)

    w_out, w_up, w_down, conv_w = late_weights(o_b)
    row512 = lambda w: _bs((MM_ROWS, w), lambda i, j, k: (i, 0))
    whole = lambda r, c: _bs((r, c), lambda i, j, k: (0, 0))
    mixed, x1, h2, rstd2 = _mix_out_norm(proj, o_a, o_b, w_out, x, norm2_g)
    u = _matmul("ffn_up", h2, w_up, out_shape=(s, 2 * D_FF), out_dtype=F32, grid=(s // MM_ROWS, 4, 1),
                a_spec=_bs((MM_ROWS, D_MODEL), lambda i, j, k: (i, 0)), b_spec=_bs((D_MODEL, D_FF // 2), lambda i, j, k: (0, j)),
                o_spec=_bs((MM_ROWS, D_FF // 2), lambda i, j, k: (i, j)), contract=NN)
    f = _convffn_fwd(u, conv_w, conv_b)
    loss, dx2, d_final_g, dx2_b = _ffn_down_loss(f, w_down, x1, target, final_norm_g.reshape(1, D_MODEL))
    tk = min(s, DW_ROWS)

    df = _matmul("ffn_down_dx", dx2_b, w_down, out_shape=(s, D_FF), out_dtype=F32, grid=(s // MM_ROWS, 2, 1),
                 a_spec=row512(D_MODEL), b_spec=_bs((D_FF // 2, D_MODEL), lambda i, j, k: (j, 0)),
                 o_spec=_bs((MM_ROWS, D_FF // 2), lambda i, j, k: (i, j)), contract=NT)
    d_w_down = _matmul("ffn_down_dw", f, dx2_b, out_shape=(D_FF, D_MODEL), out_dtype=F32, grid=(2, 1, s // tk),
                       a_spec=_bs((tk, D_FF // 2), lambda i, j, k: (k, i)), b_spec=_bs((tk, D_MODEL), lambda i, j, k: (k, 0)),
                       o_spec=_bs((D_FF // 2, D_MODEL), lambda i, j, k: (i, 0)), contract=TN)
    du, d_conv_w2, d_conv_b2 = _convffn_bwd(u, conv_w, conv_b, df)
    kc = D_FF // 2
    dx1, d_norm2_g, dx1_b = _matmul_norm_bwd(
        "ffn_up_dx_norm2_bwd", du, w_up, grid=(s // MM_ROWS, 4),
        a_spec=_bs((None, MM_ROWS, kc), lambda i, k: (k // 2, i, k % 2)), b_spec=_bs((D_MODEL, kc), lambda i, k: (0, k)),
        x=x1, rstd=rstd2, g=norm2_g, res=dx2, bf16_copy=True)
    d_w_up = _matmul("ffn_up_dw", h2, du, out_shape=(D_MODEL, 2 * D_FF), out_dtype=F32, grid=(1, 4, s // tk),
                     a_spec=_bs((tk, D_MODEL), lambda i, j, k: (k, 0)),
                     b_spec=_bs((None, tk, kc), lambda i, j, k: (j // 2, k, j % 2)),
                     o_spec=_bs((D_MODEL, kc), lambda i, j, k: (0, j)), contract=TN)

    d_w_out = _matmul("attn_out_dw", mixed, dx1_b, out_shape=(D_MODEL, D_MODEL), out_dtype=F32, grid=(1, 1, s // tk),
                      a_spec=_bs((tk, D_MODEL), lambda i, j, k: (k, 0)), b_spec=_bs((tk, D_MODEL), lambda i, j, k: (k, 0)),
                      o_spec=whole(D_MODEL, D_MODEL), contract=TN)
    token, early_grads_on = early_grads(d_w_out, d_w_up, d_w_down) if early_grads is not None else (None, None)
    do_a, do_b, d_ga, d_gb = _mix_out_bwd(dx1_b, w_out, proj, o_a, o_b, after=token)
    if early_grads_on is not None:
        sinks1 = sinks1 + early_grads_on(d_ga)[0, :H_B]

    d_qb, dk_pad, dv_pad, dbias, dsink_rows = _win_bwd(proj_b, bias, sinks1, do_b, lse_b)
    wp = _win_param_grads(bucket, dbias, dsink_rows)[:, 0, :]
    d_rel_bias = wp[:, :NUM_BUCKETS].T
    d_sinks = wp[:, NUM_BUCKETS].reshape(1, H_B)
    d_kb = dk_pad[WINDOW:WINDOW + s].astype(BF16)
    d_vb = dv_pad[WINDOW:WINDOW + s].astype(BF16)

    dq_pre, dkv_pre, dkr = _mla_bwd(q, k, v, do_a, o_a, lse_a, cos_t, sin_t)
    d_kr = _mla_key_rope_grad(dkr, cos_t, sin_t)
    th = min(s, HEAD_ROWS)
    hgrid = (s // th, 1, H_A)
    hblock = _bs((None, th, HEAD_PAD), lambda i, j, k: (k, i, 0))
    hrows = lambda w: _bs((th, w), lambda i, j, k: (i, 0))
    dqn = _matmul("q_up_dx", dq_pre, wq, out_shape=(s, Q_LORA), out_dtype=F32, grid=hgrid, a_spec=hblock,
                  b_spec=_bs((None, Q_LORA, HEAD_PAD), lambda i, j, k: (k, 0, 0)), o_spec=hrows(Q_LORA), contract=NT)
    dcn = _matmul("kv_up_dx", dkv_pre, wkv, out_shape=(s, KV_LORA), out_dtype=F32, grid=hgrid, a_spec=hblock,
                  b_spec=_bs((None, KV_LORA, HEAD_PAD), lambda i, j, k: (k, 0, 0)), o_spec=hrows(KV_LORA), contract=NT)
    wgrid = (H_A, 1, s // th)
    d_wq = _matmul("q_up_dw", qn, dq_pre, out_shape=(H_A, Q_LORA, HEAD_PAD), out_dtype=F32, grid=wgrid,
                   a_spec=_bs((th, Q_LORA), lambda i, j, k: (k, 0)), b_spec=_bs((None, th, HEAD_PAD), lambda i, j, k: (i, k, 0)),
                   o_spec=_bs((None, Q_LORA, HEAD_PAD), lambda i, j, k: (i, 0, 0)), contract=TN)
    d_wkv = _matmul("kv_up_dw", cn, dkv_pre, out_shape=(H_A, KV_LORA, HEAD_PAD), out_dtype=F32, grid=wgrid,
                    a_spec=_bs((th, KV_LORA), lambda i, j, k: (k, 0)), b_spec=_bs((None, th, HEAD_PAD), lambda i, j, k: (i, k, 0)),
                    o_spec=_bs((None, KV_LORA, HEAD_PAD), lambda i, j, k: (i, 0, 0)), contract=TN)
    d_qlat, d_gq = _rmsnorm_bwd("q_norm_bwd", dqn, proj, rstd_q, q_a_norm_g, Q_LORA, P_QLAT // Q_LORA, BF16)
    d_ckv, d_gkv = _rmsnorm_bwd("kv_norm_bwd", dcn, proj, rstd_c, kv_a_norm_g, KV_LORA, P_CKV // KV_LORA, BF16)

    dproj = jnp.concatenate([d_qb, d_ga, d_gb, d_qlat, d_kb, d_vb, d_ckv, d_kr], axis=1)
    d_w_in_p = _matmul("proj_dw", h1, dproj, out_shape=(D_MODEL, W_IN_PAD), out_dtype=F32, grid=(1, W_IN_PAD // 1024, s // tk),
                       a_spec=_bs((tk, D_MODEL), lambda i, j, k: (k, 0)), b_spec=_bs((tk, 1024), lambda i, j, k: (k, j)),
                       o_spec=_bs((D_MODEL, 1024), lambda i, j, k: (0, j)), contract=TN)
    token = last_grads(d_w_in_p, d_wq, d_wkv) if last_grads is not None else None
    dx, d_norm1_g = _matmul_norm_bwd(
        "proj_dx_norm1_bwd", dproj, w_in_p, grid=(s // MM_ROWS, W_IN_PAD // 1024),
        a_spec=_bs((MM_ROWS, 1024), lambda i, k: (i, k)), b_spec=_bs((D_MODEL, 1024), lambda i, k: (0, k)),
        x=x, rstd=rstd1, g=norm1_g, res=dx1, after=token)

    grads = dict(
        norm1_g=d_norm1_g, w_in_p=d_w_in_p, q_a_norm_g=d_gq, wq=d_wq, kv_a_norm_g=d_gkv, wkv=d_wkv,
        rel_bias=d_rel_bias, sinks=d_sinks, w_out=d_w_out, norm2_g=d_norm2_g, w_up=d_w_up,
        conv_w=jnp.concatenate([d_conv_w2[0], d_conv_w2[1]], axis=1),
        conv_b=jnp.concatenate([d_conv_b2[0], d_conv_b2[1]], axis=1),
        w_down=d_w_down, final_norm_g=d_final_g.reshape(D_MODEL))
    return loss, dx, grads


HEADS_PER_SHARD = H_A // 4


def _head_cols(h, width):
    return slice((h % HEADS_PER_SHARD) * width, (h % HEADS_PER_SHARD + 1) * width)


def _wq_heads(shards):
    per = QK_NOPE + QK_ROPE
    w = jnp.stack([shards[h // HEADS_PER_SHARD][:, _head_cols(h, per)] for h in range(H_A)])
    return jnp.pad(w, ((0, 0), (0, 0), (0, HEAD_PAD - per)))


def _wq_shards(d_wq):
    per = QK_NOPE + QK_ROPE
    return jnp.stack([jnp.concatenate([d_wq[h][:, :per] for h in range(q * HEADS_PER_SHARD, (q + 1) * HEADS_PER_SHARD)],
                                      axis=1) for q in range(4)])


def _wkv_heads(shards):
    return jnp.stack([shards[h // HEADS_PER_SHARD][:, _head_cols(h, QK_NOPE + V_DIM)] for h in range(H_A)])


def _wkv_shards(d_wkv):
    return jnp.stack([jnp.concatenate([d_wkv[h] for h in range(q * HEADS_PER_SHARD, (q + 1) * HEADS_PER_SHARD)], axis=1)
                      for q in range(4)])


SMALL = ("norm1_g", "q_a_norm_g", "kv_a_norm_g", "rel_bias", "sinks", "norm2_g", "conv_b", "final_norm_g")
FIRST = ("w_in", "w_q_b", "w_kv_b")
LATER = ("w_out", "w_up", "w_down")
BIG = FIRST + LATER


def kernel(x, positions, norm1_g, w_in, q_a_norm_g, w_q_b, kv_a_norm_g, w_kv_b, rel_bias, sinks, w_out, norm2_g, w_up, conv_w, conv_b, w_down, final_norm_g, loss_target, m_norm1_g, m_w_in, m_q_a_norm_g, m_w_q_b, m_kv_a_norm_g, m_w_kv_b, m_rel_bias, m_sinks, m_w_out, m_norm2_g, m_w_up, m_conv_w, m_conv_b, m_w_down, m_final_norm_g, v_norm1_g, v_w_in, v_q_a_norm_g, v_w_q_b, v_kv_a_norm_g, v_w_kv_b, v_rel_bias, v_sinks, v_w_out, v_norm2_g, v_w_up, v_conv_w, v_conv_b, v_w_down, v_final_norm_g):
    weights = dict(norm1_g=norm1_g, w_in=w_in, q_a_norm_g=q_a_norm_g, w_q_b=w_q_b, kv_a_norm_g=kv_a_norm_g,
                   w_kv_b=w_kv_b, rel_bias=rel_bias, sinks=sinks, w_out=w_out, norm2_g=norm2_g, w_up=w_up,
                   conv_w=conv_w, conv_b=conv_b, w_down=w_down, final_norm_g=final_norm_g)
    mom_m = dict(norm1_g=m_norm1_g, w_in=m_w_in, q_a_norm_g=m_q_a_norm_g, w_q_b=m_w_q_b, kv_a_norm_g=m_kv_a_norm_g,
                 w_kv_b=m_w_kv_b, rel_bias=m_rel_bias, sinks=m_sinks, w_out=m_w_out, norm2_g=m_norm2_g, w_up=m_w_up,
                 conv_w=m_conv_w, conv_b=m_conv_b, w_down=m_w_down, final_norm_g=m_final_norm_g)
    mom_v = dict(norm1_g=v_norm1_g, w_in=v_w_in, q_a_norm_g=v_q_a_norm_g, w_q_b=v_w_q_b, kv_a_norm_g=v_kv_a_norm_g,
                 w_kv_b=v_w_kv_b, rel_bias=v_rel_bias, sinks=v_sinks, w_out=v_w_out, norm2_g=v_norm2_g, w_up=v_w_up,
                 conv_w=v_conv_w, conv_b=v_conv_b, w_down=v_w_down, final_norm_g=v_final_norm_g)
    shard2d = {n: weights[n][0] for n in BIG}
    conv_w_shard = conv_w[0]
    xi, yi, ci = lax.axis_index("x"), lax.axis_index("y"), lax.axis_index("c")
    chip = (2 * xi + yi).astype(jnp.int32)

    core = ci.astype(jnp.int32).reshape(1)
    chip1 = chip.reshape(1)
    cat_cols = lambda a: jnp.concatenate([a[0], a[1], a[2], a[3]], axis=1)
    own_slot = lambda a, own: lax.dynamic_update_index_in_dim(a, own, chip, 0)
    halved = lambda a: a.reshape((2, a.shape[0] // 2) + a.shape[1:])
    quartered = lambda a: a.reshape(4, 2, a.shape[1] // 2, a.shape[2])

    first = [halved(shard2d[n].astype(BF16)) for n in FIRST]
    gather1 = _SplitExchange("gather_first", first, [(4,) + a.shape for a in first],
                             src_of=lambda ref, k, p, pk, c: ref.at[c], dst_of=lambda ref, k, p, pk, c: ref.at[p, c],
                             arrive_of=lambda ref, k, p, pk, c: ref.at[pk, c])
    token1 = gather1.start()

    def first_weights(after):
        own, landed = gather1.wait(after)
        gathered = [own_slot(a, mine) for a, mine in zip(_forward_to_sibling(landed), own)]
        g = {n: a.reshape((4,) + shard2d[n].shape) for n, a in zip(FIRST, gathered)}
        return _w_in_from_shards(g["w_in"]), _wq_heads(g["w_q_b"]), _wkv_heads(g["w_kv_b"])

    later = [shard2d[n].astype(BF16) for n in LATER] + [conv_w_shard]
    gather2 = _SplitExchange("gather_later", later, [(4,) + a.shape for a in later],
                             src_of=lambda ref, k, p, pk, c: ref, dst_of=lambda ref, k, p, pk, c: ref.at[p],
                             arrive_of=lambda ref, k, p, pk, c: ref.at[pk])
    norm1_g_in = norm1_g + gather2.start(after=token1)[:1, :1]

    def late_weights(after):
        w_out_g, w_up_g, w_down_g, conv_w_g = [own_slot(a, mine) for mine, a in zip(*gather2.wait(after))]
        return w_out_g.reshape(D_MODEL, D_MODEL), cat_cols(w_up_g), w_down_g.reshape(D_FF, D_MODEL), cat_cols(conv_w_g)

    early = {}

    def early_grads(d_w_out, d_w_up, d_w_down):
        grads = [quartered(d_w_out.reshape(4, D_MODEL // 4, D_MODEL)), quartered(_col_shards(d_w_up)),
                 quartered(d_w_down.reshape(4, D_FF // 4, D_MODEL))]
        swap = _SplitExchange("rs_pair_exchange_early", grads, [(4,) + a.shape[2:] for a in grads], to_sibling=True,
                              src_of=lambda ref, k, p, pk, c: ref.at[:, pk], dst_of=lambda ref, k, p, pk, c: ref,
                              arrive_of=lambda ref, k, p, pk, c: ref)

        def on(after):
            kept, recv = swap.wait(after)
            early["pairs"] = [_rs_pair_add(f"rs_pair_add_{n}", core, gfull, r) for n, gfull, r in zip(LATER, kept, recv)]
            early["ici"] = _SplitExchange("rs_ici_early", early["pairs"], [(3,) + a.shape[1:] for a in early["pairs"]],
                                          src_of=lambda ref, k, p, pk, c: ref.at[pk], dst_of=lambda ref, k, p, pk, c: ref.at[k],
                                          arrive_of=lambda ref, k, p, pk, c: ref.at[k])
            return early["ici"].start()

        return swap.start(), on

    last = {}

    def last_grads(d_w_in_p, d_wq, d_wkv):
        grads = [quartered(_w_in_to_shards(d_w_in_p)), quartered(_wq_shards(d_wq)), quartered(_wkv_shards(d_wkv))]
        recv = _rs_pair_exchange("rs_pair_exchange_last", grads)
        last["pairs"] = [_rs_pair_add(f"rs_pair_add_{n}", core, gfull, r) for n, gfull, r in zip(FIRST, grads, recv)]
        last["ici"] = _SplitExchange("rs_ici_last", last["pairs"], [(3,) + a.shape[1:] for a in last["pairs"]],
                                     src_of=lambda ref, k, p, pk, c: ref.at[pk], dst_of=lambda ref, k, p, pk, c: ref.at[k],
                                     arrive_of=lambda ref, k, p, pk, c: ref.at[k])
        return last["ici"].start()

    loss, dx, gr = _local_step(x[0], positions, loss_target[0], norm1_g_in, first_weights, q_a_norm_g, kv_a_norm_g,
                               rel_bias, sinks, late_weights, norm2_g, conv_b, final_norm_g, early_grads, last_grads)

    last_pairs, last_recv = last["ici"].wait(dx)
    early_pairs, early_recv = early["ici"].wait(dx)
    pairs, recv2 = last_pairs + early_pairs, last_recv + early_recv
    halves = [_rs_final_add(f"rs_final_add_{n}", chip1, pr, r) for n, pr, r in zip(FIRST + LATER, pairs, recv2)]
    sibling_halves = _rs_pair_share(halves)

    as_rows = lambda a: a.reshape((-1, a.shape[-1]))
    summed = _small_allreduce([as_rows(gr[n]) for n in SMALL] + [gr["conv_w"], loss])
    small_g = dict(zip(SMALL, summed[:len(SMALL)]))
    conv_w_g = lax.dynamic_slice_in_dim(summed[len(SMALL)], chip * (2 * D_FF // 4), 2 * D_FF // 4, axis=1)
    loss_out = summed[-1].reshape(())

    out_g, out_d, out_m, out_v = {}, {}, {}, {}
    for n, mine, theirs in zip(FIRST + LATER, halves, sibling_halves):
        gsh, d, nm, nv = _adamw_halves(f"adamw_{n}", core, shard2d[n], mine, theirs, mom_m[n][0], mom_v[n][0])
        out_g[n], out_d[n], out_m[n], out_v[n] = gsh[None], d[None], nm[None], nv[None]
    names = SMALL + ("conv_w",)
    sg = [small_g[n] for n in SMALL] + [conv_w_g]
    ds, nms, nvs = _adamw_small([as_rows(weights[n]) for n in names], sg, [as_rows(mom_m[n]) for n in names],
                                [as_rows(mom_v[n]) for n in names])
    for n, gg, dd, mm, vv in zip(names, sg, ds, nms, nvs):
        shp = weights[n].shape
        out_g[n], out_d[n], out_m[n], out_v[n] = gg.reshape(shp), dd.reshape(shp), mm.reshape(shp), vv.reshape(shp)

    order = ("norm1_g", "w_in", "q_a_norm_g", "w_q_b", "kv_a_norm_g", "w_kv_b", "rel_bias", "sinks", "w_out",
             "norm2_g", "w_up", "conv_w", "conv_b", "w_down", "final_norm_g")
    return (loss_out, dx[None], *[out_g[n] for n in order], *[out_d[n] for n in order],
            *[out_m[n] for n in order], *[out_v[n] for n in order])
```

```python
import functools
import math

import jax
import jax.numpy as jnp
import numpy as np
from jax import lax
from jax.experimental import pallas as pl
from jax.experimental.pallas import tpu as pltpu

F32 = jnp.float32
BF16 = jnp.bfloat16
MESH = pl.DeviceIdType.MESH

D_MODEL = 1024
EPS = 1e-6
H_A = 8
QK_NOPE = 128
QK_ROPE = 64
V_DIM = 128
Q_LORA = 256
KV_LORA = 128
ROPE_THETA = 10000.0
H_B = 16
KV_B = 4
GROUP = 4
HD_B = 64
WINDOW = 128
Q_BLOCK = 128
NUM_BUCKETS = 32
MAX_DISTANCE = 128
D_FF = 2816
HEAD_PAD = 256

ADAM_LR = 0.001
ADAM_B1 = 0.9
ADAM_B2 = 0.999
ADAM_EPS = 1e-08
ADAM_WD = 0.01
ADAM_STEP = 10

LANES = 128
P_QB, P_GA, P_GB, P_QLAT, P_KB, P_VB, P_CKV, P_KR = 0, 1024, 2048, 3072, 3328, 3584, 3840, 3968
W_IN_PAD = 4096

NT = (((1,), (1,)), ((), ()))
NN = (((1,), (0,)), ((), ()))
TN = (((0,), (0,)), ((), ()))


def _arb(n):
    return pltpu.CompilerParams(dimension_semantics=("arbitrary",) * n)


def _matmul(name, a, b, *, out_shape, out_dtype, grid, a_spec, b_spec, o_spec, contract, add=None, bf16_copy=False,
            after=None):
    nk = grid[2]
    acc_shape = tuple(d for d in o_spec.block_shape if d is not None)
    n_in = 2 + (add is not None) + (after is not None)
    n_out = 2 if bf16_copy else 1

    def body(*refs):
        a_ref, b_ref = refs[:2]
        add_ref = refs[2] if add is not None else None
        o_refs = refs[n_in:n_in + n_out]
        scratch = refs[n_in + n_out:]
        prod = lax.dot_general(a_ref[...].astype(BF16), b_ref[...].astype(BF16), contract,
                               preferred_element_type=F32)

        def finish(val):
            if add_ref is not None:
                val = add_ref[...] + val
            o_refs[0][...] = val.astype(out_dtype)
            if bf16_copy:
                o_refs[1][...] = val.astype(BF16)

        if nk == 1:
            finish(prod)
        else:
            acc_ref = scratch[0]
            k = pl.program_id(2)

            @pl.when(k == 0)
            def _():
                acc_ref[...] = prod

            @pl.when((k > 0) & (k < nk - 1))
            def _():
                acc_ref[...] += prod

            @pl.when(k == nk - 1)
            def _():
                finish(acc_ref[...] + prod)

    in_specs = [a_spec, b_spec]
    args = [a, b]
    if add is not None:
        in_specs.append(o_spec)
        args.append(add)
    if after is not None:
        in_specs.append(pl.BlockSpec(memory_space=pl.ANY))
        args.append(after)
    out_shapes = [jax.ShapeDtypeStruct(out_shape, out_dtype)]
    if bf16_copy:
        out_shapes.append(jax.ShapeDtypeStruct(out_shape, BF16))
    res = pl.pallas_call(
        body, name=name, grid=grid, in_specs=in_specs, out_specs=[o_spec] * n_out, out_shape=out_shapes,
        scratch_shapes=[pltpu.VMEM(acc_shape, F32)] if nk > 1 else [],
        compiler_params=_arb(3),
    )(*args)
    return res if bf16_copy else res[0]


def _bs(block, fn):
    return pl.BlockSpec(block, fn)


def _rmsnorm_fwd(name, src, g, d, cb, ts=512):
    s = src.shape[0]

    def body(x_ref, g_ref, h_ref, r_ref):
        x = x_ref[...]
        r = lax.rsqrt(jnp.mean(x * x, axis=-1, keepdims=True) + EPS)
        h_ref[...] = (x * r * g_ref[...]).astype(BF16)
        r_ref[...] = r

    return pl.pallas_call(
        body, name=name, grid=(s // ts,),
        in_specs=[_bs((ts, d), lambda i: (i, cb)), _bs((1, d), lambda i: (0, 0))],
        out_specs=[_bs((ts, d), lambda i: (i, 0)), _bs((ts, 1), lambda i: (i, 0))],
        out_shape=[jax.ShapeDtypeStruct((s, d), BF16), jax.ShapeDtypeStruct((s, 1), F32)],
        compiler_params=_arb(1),
    )(src, g)


def _rmsnorm_bwd(name, dy, src, rstd, g, d, cb, out_dtype, res=None, bf16_copy=False, ts=512):
    s = src.shape[0]

    def body(*refs):
        dy_ref, x_ref, r_ref, g_ref = refs[:4]
        res_ref = refs[4] if res is not None else None
        dx_ref, dg_ref = refs[n_in:n_in + 2]
        dyv = dy_ref[...]
        r = r_ref[...]
        xhat = x_ref[...] * r
        dyh = dyv * g_ref[...]
        c = jnp.mean(dyh * xhat, axis=-1, keepdims=True)
        dx = r * (dyh - xhat * c)
        if res_ref is not None:
            dx = res_ref[...] + dx
        dx_ref[...] = dx.astype(out_dtype)
        if bf16_copy:
            refs[n_in + 2][...] = dx.astype(BF16)
        part = jnp.sum(dyv * xhat, axis=0, keepdims=True)

        @pl.when(pl.program_id(0) == 0)
        def _():
            dg_ref[...] = part

        @pl.when(pl.program_id(0) > 0)
        def _():
            dg_ref[...] += part

    in_specs = [_bs((ts, d), lambda i: (i, 0)), _bs((ts, d), lambda i: (i, cb)),
                _bs((ts, 1), lambda i: (i, 0)), _bs((1, d), lambda i: (0, 0))]
    args = [dy, src, rstd, g]
    if res is not None:
        in_specs.append(_bs((ts, d), lambda i: (i, 0)))
        args.append(res)
    n_in = len(args)
    out_specs = [_bs((ts, d), lambda i: (i, 0)), _bs((1, d), lambda i: (0, 0))]
    out_shape = [jax.ShapeDtypeStruct((s, d), out_dtype), jax.ShapeDtypeStruct((1, d), F32)]
    if bf16_copy:
        out_specs.append(_bs((ts, d), lambda i: (i, 0)))
        out_shape.append(jax.ShapeDtypeStruct((s, d), BF16))
    return pl.pallas_call(
        body, name=name, grid=(s // ts,), in_specs=in_specs, out_specs=out_specs, out_shape=out_shape,
        compiler_params=_arb(1),
    )(*args)


def _matmul_norm_bwd(name, a, b, *, grid, a_spec, b_spec, x, rstd, g, res, bf16_copy=False, after=None):
    s, d = x.shape
    ni, nk = grid
    assert nk >= 2, "the first and the last contraction step are distinct branches"
    tm = s // ni
    n_in = 6 + (after is not None)

    def body(*refs):
        a_ref, b_ref, x_ref, r_ref, g_ref, res_ref = refs[:6]
        dx_ref, dg_ref = refs[n_in:n_in + 2]
        acc_ref = refs[-1]
        k = pl.program_id(1)
        prod = lax.dot_general(a_ref[...], b_ref[...], NT, preferred_element_type=F32)

        @pl.when(k == 0)
        def _():
            acc_ref[...] = prod

        @pl.when((k > 0) & (k < nk - 1))
        def _():
            acc_ref[...] += prod

        @pl.when(k == nk - 1)
        def _():
            dyv = acc_ref[...] + prod
            r = r_ref[...]
            xhat = x_ref[...] * r
            dyh = dyv * g_ref[...]
            c = jnp.mean(dyh * xhat, axis=-1, keepdims=True)
            dx = res_ref[...] + r * (dyh - xhat * c)
            dx_ref[...] = dx
            if bf16_copy:
                refs[n_in + 2][...] = dx.astype(BF16)
            part = jnp.sum(dyv * xhat, axis=0, keepdims=True)

            @pl.when(pl.program_id(0) == 0)
            def _():
                dg_ref[...] = part

            @pl.when(pl.program_id(0) > 0)
            def _():
                dg_ref[...] += part

    rows = _bs((tm, d), lambda i, k: (i, 0))
    in_specs = [a_spec, b_spec, rows, _bs((tm, 1), lambda i, k: (i, 0)), _bs((1, d), lambda i, k: (0, 0)), rows]
    args = [a, b, x, rstd, g, res]
    if after is not None:
        in_specs.append(pl.BlockSpec(memory_space=pl.ANY))
        args.append(after)
    out_specs = [rows, _bs((1, d), lambda i, k: (0, 0))]
    out_shape = [jax.ShapeDtypeStruct((s, d), F32), jax.ShapeDtypeStruct((1, d), F32)]
    if bf16_copy:
        out_specs.append(rows)
        out_shape.append(jax.ShapeDtypeStruct((s, d), BF16))
    return pl.pallas_call(
        body, name=name, grid=grid, in_specs=in_specs, out_specs=out_specs, out_shape=out_shape,
        scratch_shapes=[pltpu.VMEM((tm, d), F32)], compiler_params=_arb(2),
    )(*args)


def _ffn_down_loss(f, w_down, x1, target, g, ts=512):
    s, d = x1.shape
    dff = f.shape[1]

    def body(f_ref, w_ref, x_ref, t_ref, g_ref, loss_ref, dx_ref, dg_ref, dxb_ref):
        x = x_ref[...] + jnp.dot(f_ref[...], w_ref[...], preferred_element_type=F32)
        r = lax.rsqrt(jnp.mean(x * x, axis=-1, keepdims=True) + EPS)
        xhat = x * r
        gv = g_ref[...]
        err = xhat * gv - t_ref[...]
        lpart = 0.5 * jnp.sum(jnp.mean(err * err, axis=-1, keepdims=True), axis=0, keepdims=True)
        dyv = err * (1.0 / d)
        dyh = dyv * gv
        c = jnp.mean(dyh * xhat, axis=-1, keepdims=True)
        dx = r * (dyh - xhat * c)
        dx_ref[...] = dx
        dxb_ref[...] = dx.astype(BF16)
        gpart = jnp.sum(dyv * xhat, axis=0, keepdims=True)

        @pl.when(pl.program_id(0) == 0)
        def _():
            dg_ref[...] = gpart
            loss_ref[...] = lpart

        @pl.when(pl.program_id(0) > 0)
        def _():
            dg_ref[...] += gpart
            loss_ref[...] += lpart

    rows = _bs((ts, d), lambda i: (i, 0))
    return pl.pallas_call(
        body, name="ffn_down_loss", grid=(s // ts,),
        in_specs=[_bs((ts, dff), lambda i: (i, 0)), _bs((dff, d), lambda i: (0, 0)), rows, rows,
                  _bs((1, d), lambda i: (0, 0))],
        out_specs=[_bs((1, 1), lambda i: (0, 0)), rows, _bs((1, d), lambda i: (0, 0)), rows],
        out_shape=[jax.ShapeDtypeStruct((1, 1), F32), jax.ShapeDtypeStruct((s, d), F32),
                   jax.ShapeDtypeStruct((1, d), F32), jax.ShapeDtypeStruct((s, d), BF16)],
        compiler_params=_arb(1),
    )(f, w_down, x1, target, g)


def _swap_halves(t):
    lane = lax.broadcasted_iota(jnp.int32, t.shape, 1)
    return jnp.where(lane < 32, pltpu.roll(t, 96, 1), pltpu.roll(t, 32, 1))


def _rope_fwd(t, cos_t, sin_t):
    return t * cos_t + _swap_halves(t) * sin_t


def _rope_bwd(dt, cos_t, sin_t):
    return dt * cos_t - _swap_halves(dt) * sin_t


def _lat_norms(proj, gq, gkv, ts=512):
    s = proj.shape[0]

    def body(q_ref, c_ref, gq_ref, gkv_ref, qn_ref, cn_ref, rq_ref, rc_ref):
        q = q_ref[...]
        rq = lax.rsqrt(jnp.mean(q * q, axis=-1, keepdims=True) + EPS)
        qn_ref[...] = (q * rq * gq_ref[...]).astype(BF16)
        rq_ref[...] = rq
        cv = c_ref[...]
        rc = lax.rsqrt(jnp.mean(cv * cv, axis=-1, keepdims=True) + EPS)
        cn_ref[...] = (cv * rc * gkv_ref[...]).astype(BF16)
        rc_ref[...] = rc

    return pl.pallas_call(
        body, name="lat_norms", grid=(s // ts,),
        in_specs=[_bs((ts, Q_LORA), lambda i: (i, P_QLAT // Q_LORA)),
                  _bs((ts, KV_LORA), lambda i: (i, P_CKV // KV_LORA)),
                  _bs((1, Q_LORA), lambda i: (0, 0)), _bs((1, KV_LORA), lambda i: (0, 0))],
        out_specs=[_bs((ts, Q_LORA), lambda i: (i, 0)), _bs((ts, KV_LORA), lambda i: (i, 0)),
                   _bs((ts, 1), lambda i: (i, 0)), _bs((ts, 1), lambda i: (i, 0))],
        out_shape=[jax.ShapeDtypeStruct((s, Q_LORA), BF16), jax.ShapeDtypeStruct((s, KV_LORA), BF16),
                   jax.ShapeDtypeStruct((s, 1), F32), jax.ShapeDtypeStruct((s, 1), F32)],
        compiler_params=_arb(1),
    )(proj, proj, gq, gkv)


HEAD_ROWS = 2048
DW_ROWS = 2048
MM_ROWS = 1024


def _q_heads(qn, wq, cos_t, sin_t):
    s = qn.shape[0]
    ts = min(s, HEAD_ROWS)

    def body(qn_ref, w_ref, cos_ref, sin_ref, q_ref):
        o = jnp.dot(qn_ref[...], w_ref[...], preferred_element_type=F32)
        q_ref[:, :LANES] = o[:, :LANES].astype(BF16)
        q_ref[:, LANES:] = _rope_fwd(o[:, LANES:], cos_ref[...], sin_ref[...]).astype(BF16)

    return pl.pallas_call(
        body, name="q_heads", grid=(s // ts, H_A),
        in_specs=[_bs((ts, Q_LORA), lambda i, h: (i, 0)), _bs((None, Q_LORA, HEAD_PAD), lambda i, h: (h, 0, 0)),
                  _bs((ts, LANES), lambda i, h: (i, 0)), _bs((ts, LANES), lambda i, h: (i, 0))],
        out_specs=_bs((None, ts, HEAD_PAD), lambda i, h: (h, i, 0)),
        out_shape=jax.ShapeDtypeStruct((H_A, s, HEAD_PAD), BF16),
        compiler_params=_arb(2),
    )(qn, wq, cos_t, sin_t)


def _kv_heads(cn, wkv, proj, cos_t, sin_t):
    s = cn.shape[0]
    ts = min(s, HEAD_ROWS)

    def body(cn_ref, w_ref, kr_ref, cos_ref, sin_ref, k_ref, v_ref):
        o = jnp.dot(cn_ref[...], w_ref[...], preferred_element_type=F32)
        k_ref[:, :LANES] = o[:, :LANES].astype(BF16)
        k_ref[:, LANES:] = _rope_fwd(kr_ref[...], cos_ref[...], sin_ref[...]).astype(BF16)
        v_ref[...] = o[:, LANES:].astype(BF16)

    return pl.pallas_call(
        body, name="kv_heads", grid=(s // ts, H_A),
        in_specs=[_bs((ts, KV_LORA), lambda i, h: (i, 0)),
                  _bs((None, KV_LORA, QK_NOPE + V_DIM), lambda i, h: (h, 0, 0)),
                  _bs((ts, LANES), lambda i, h: (i, P_KR // LANES)),
                  _bs((ts, LANES), lambda i, h: (i, 0)), _bs((ts, LANES), lambda i, h: (i, 0))],
        out_specs=[_bs((None, ts, HEAD_PAD), lambda i, h: (h, i, 0)), _bs((None, ts, V_DIM), lambda i, h: (h, i, 0))],
        out_shape=[jax.ShapeDtypeStruct((H_A, s, HEAD_PAD), BF16), jax.ShapeDtypeStruct((H_A, s, V_DIM), BF16)],
        compiler_params=_arb(2),
    )(cn, wkv, proj, cos_t, sin_t)


MLA_SCALE = 1.0 / math.sqrt(QK_NOPE + QK_ROPE)
LOG2E = math.log2(math.e)
MLA_EXP2_SCALE = MLA_SCALE * LOG2E


def _lane_tiles(a):
    return [a[:, j * LANES:(j + 1) * LANES] for j in range(a.shape[1] // LANES)]


MLA_SUB = 512


def _mla_fwd(q, k, v, tq=512, tk=1024):
    s = q.shape[1]
    tq = min(tq, s)
    nk = s // tk

    def body(q_ref, k_ref, v_ref, o_ref, lse_ref, m_ref, l_ref, acc_ref):
        m_ref[...] = jnp.full(m_ref.shape, -jnp.inf, F32)
        l_ref[...] = jnp.zeros(l_ref.shape, F32)
        acc_ref[...] = jnp.zeros(acc_ref.shape, F32)

        def step(c, carry):
            rows = pl.ds(pl.multiple_of(c * tk, tk), tk)
            for sub in range(tq // MLA_SUB):
                qr = slice(sub * MLA_SUB, (sub + 1) * MLA_SUB)
                raw = lax.dot_general(q_ref[qr, :], k_ref[rows, :], NT, preferred_element_type=F32)
                m_prev = m_ref[qr, :]
                m_new = jnp.maximum(m_prev, jnp.max(raw, axis=-1, keepdims=True))
                alpha = jnp.exp2((m_prev - m_new) * MLA_EXP2_SCALE)
                ps = [jnp.exp2((t - m_new) * MLA_EXP2_SCALE) for t in _lane_tiles(raw)]
                l_ref[qr, :] = alpha * l_ref[qr, :] + functools.reduce(lambda a, b: a + b, ps)
                p = jnp.concatenate(ps, axis=1).astype(BF16)
                acc_ref[qr, :] = alpha * acc_ref[qr, :] + jnp.dot(p, v_ref[rows, :], preferred_element_type=F32)
                m_ref[qr, :] = m_new
            return carry

        lax.fori_loop(0, nk, step, 0, unroll=True)
        l = jnp.sum(l_ref[...], axis=-1, keepdims=True)
        o_ref[...] = acc_ref[...] / l
        lse_ref[...] = m_ref[...] * MLA_SCALE + jnp.log(l)

    return pl.pallas_call(
        body, name="mla_fwd", grid=(H_A, s // tq),
        in_specs=[_bs((None, tq, HEAD_PAD), lambda h, i: (h, i, 0)),
                  _bs((None, s, HEAD_PAD), lambda h, i: (h, 0, 0)),
                  _bs((None, s, V_DIM), lambda h, i: (h, 0, 0))],
        out_specs=[_bs((tq, V_DIM), lambda h, i: (i, h)), _bs((None, tq, LANES), lambda h, i: (h, i, 0))],
        out_shape=[jax.ShapeDtypeStruct((s, H_A * V_DIM), F32), jax.ShapeDtypeStruct((H_A, s, LANES), F32)],
        scratch_shapes=[pltpu.VMEM((tq, LANES), F32), pltpu.VMEM((tq, LANES), F32), pltpu.VMEM((tq, V_DIM), F32)],
        compiler_params=_arb(2),
    )(q, k, v)


def _mla_bwd(q, k, v, do, o, lse, cos_t, sin_t, tq=512, tk=512):
    s = q.shape[1]
    nq = s // tq
    nkb = s // tk

    def body(q_ref, k_ref, v_ref, do_ref, o_ref, lse_ref, cos_ref, sin_ref, dqp_ref, dkvp_ref, dkr_ref,
             delta_ref, dq_ref, dk_ref, dv_ref):
        @pl.when(pl.program_id(1) == 0)
        def _():
            def init(c, carry):
                rows = pl.ds(pl.multiple_of(c * tq, tq), tq)
                delta = jnp.sum(do_ref[rows, :] * o_ref[rows, :], axis=-1, keepdims=True)
                delta_ref[rows, :] = jnp.broadcast_to(delta, (tq, LANES))
                dq_ref[rows, :] = jnp.zeros((tq, HEAD_PAD), F32)
                return carry

            lax.fori_loop(0, nq, init, 0)

        dk_ref[...] = jnp.zeros(dk_ref.shape, F32)
        dv_ref[...] = jnp.zeros(dv_ref.shape, F32)
        kb = k_ref[...]
        vb = v_ref[...]

        def step(c, carry):
            rows = pl.ds(pl.multiple_of(c * tq, tq), tq)
            qc = q_ref[rows, :]
            doc = do_ref[rows, :].astype(BF16)
            raw = lax.dot_general(qc, kb, NT, preferred_element_type=F32)
            dp = lax.dot_general(doc, vb, NT, preferred_element_type=F32)
            lse2 = lse_ref[rows, :] * LOG2E
            delta = delta_ref[rows, :]
            ps = [jnp.exp2(t * MLA_EXP2_SCALE - lse2) for t in _lane_tiles(raw)]
            dss = [pj * (dj - delta) * MLA_SCALE for pj, dj in zip(ps, _lane_tiles(dp))]
            p = jnp.concatenate(ps, axis=1).astype(BF16)
            ds = jnp.concatenate(dss, axis=1).astype(BF16)
            dv_ref[...] += lax.dot_general(p, doc, TN, preferred_element_type=F32)
            dk_ref[...] += lax.dot_general(ds, qc, TN, preferred_element_type=F32)
            dq_ref[rows, :] += jnp.dot(ds, kb, preferred_element_type=F32)
            return carry

        lax.fori_loop(0, nq, step, 0, unroll=True)
        dkvp_ref[:, :LANES] = dk_ref[:, :LANES].astype(BF16)
        dkvp_ref[:, LANES:] = dv_ref[...].astype(BF16)
        dkr_ref[...] = dk_ref[:, LANES:]

        @pl.when(pl.program_id(1) == nkb - 1)
        def _():
            def finish(c, carry):
                rows = pl.ds(pl.multiple_of(c * tq, tq), tq)
                dqp_ref[rows, :LANES] = dq_ref[rows, :LANES].astype(BF16)
                dqp_ref[rows, LANES:] = _rope_bwd(dq_ref[rows, LANES:], cos_ref[rows, :], sin_ref[rows, :]).astype(BF16)
                return carry

            lax.fori_loop(0, nq, finish, 0)

    whole = lambda w: _bs((s, w), lambda h, j: (0, 0))
    return pl.pallas_call(
        body, name="mla_bwd", grid=(H_A, nkb),
        in_specs=[_bs((None, s, HEAD_PAD), lambda h, j: (h, 0, 0)),
                  _bs((None, tk, HEAD_PAD), lambda h, j: (h, j, 0)),
                  _bs((None, tk, V_DIM), lambda h, j: (h, j, 0)),
                  _bs((s, V_DIM), lambda h, j: (0, h)), _bs((s, V_DIM), lambda h, j: (0, h)),
                  _bs((None, s, LANES), lambda h, j: (h, 0, 0)), whole(LANES), whole(LANES)],
        out_specs=[_bs((None, s, HEAD_PAD), lambda h, j: (h, 0, 0)),
                   _bs((None, tk, HEAD_PAD), lambda h, j: (h, j, 0)),
                   _bs((None, tk, LANES), lambda h, j: (h, j, 0))],
        out_shape=[jax.ShapeDtypeStruct((H_A, s, HEAD_PAD), BF16), jax.ShapeDtypeStruct((H_A, s, HEAD_PAD), BF16),
                   jax.ShapeDtypeStruct((H_A, s, LANES), F32)],
        scratch_shapes=[pltpu.VMEM((s, LANES), F32), pltpu.VMEM((s, HEAD_PAD), F32), pltpu.VMEM((tk, HEAD_PAD), F32),
                        pltpu.VMEM((tk, V_DIM), F32)],
        compiler_params=_arb(2),
    )(q, k, v, do, o, lse, cos_t, sin_t)


def _mla_key_rope_grad(dkr, cos_t, sin_t, ts=512):
    s = dkr.shape[1]

    def body(d_ref, cos_ref, sin_ref, o_ref):
        tot = d_ref[0]
        for h in range(1, H_A):
            tot = tot + d_ref[h]
        o_ref[...] = _rope_bwd(tot, cos_ref[...], sin_ref[...]).astype(BF16)

    rows = _bs((ts, LANES), lambda i: (i, 0))
    return pl.pallas_call(
        body, name="mla_key_rope_grad", grid=(s // ts,),
        in_specs=[_bs((H_A, ts, LANES), lambda i: (0, i, 0)), rows, rows], out_specs=rows,
        out_shape=jax.ShapeDtypeStruct((s, LANES), BF16), compiler_params=_arb(1),
    )(dkr, cos_t, sin_t)


WIN_SCALE = 1.0 / math.sqrt(HD_B)
SPAN = Q_BLOCK + 2 * WINDOW


def _t5_bucket_table():
    a = jnp.arange(Q_BLOCK, dtype=jnp.int32)[:, None]
    c = jnp.arange(SPAN, dtype=jnp.int32)[None, :]
    rel = c - WINDOW - a
    nb = NUM_BUCKETS // 2
    max_exact = nb // 2
    base = (rel > 0).astype(jnp.int32) * nb
    n = jnp.abs(rel)
    nf = jnp.maximum(n, 1).astype(F32)
    large = max_exact + (jnp.log(nf / max_exact) / math.log(MAX_DISTANCE / max_exact)
                         * (nb - max_exact)).astype(jnp.int32)
    large = jnp.minimum(large, nb - 1)
    return base + jnp.where(n < max_exact, n, large)


def _win_bias(bucket, rel_bias):
    def body(rb_ref, bk_ref, o_ref):
        h = pl.program_id(0)
        bk = bk_ref[...]
        acc = jnp.zeros((Q_BLOCK, SPAN), F32)
        for b in range(NUM_BUCKETS):
            acc = jnp.where(bk == b, rb_ref[b, h], acc)
        o_ref[...] = acc

    return pl.pallas_call(
        body, name="win_bias", grid=(H_B,),
        in_specs=[pl.BlockSpec(memory_space=pltpu.SMEM), _bs((Q_BLOCK, SPAN), lambda h: (0, 0))],
        out_specs=_bs((None, Q_BLOCK, SPAN), lambda h: (h, 0, 0)),
        out_shape=jax.ShapeDtypeStruct((H_B, Q_BLOCK, SPAN), F32),
        compiler_params=_arb(1),
    )(rel_bias, bucket)


WIN_HEADS = GROUP


def _win_kv_rows(n, j, nblk):
    blk = jnp.clip(n + j - 1, 0, nblk - 1)
    return pl.ds(pl.multiple_of(blk * Q_BLOCK, Q_BLOCK), Q_BLOCK)


def _win_kv_cols(h0):
    kv = h0 // GROUP
    return slice(kv * HD_B, (kv + 1) * HD_B)


def _win_stack(ref, h0):
    return jnp.concatenate([ref[:, (h0 + g) * HD_B:(h0 + g + 1) * HD_B] for g in range(WIN_HEADS)], axis=0)


def _win_unstack(ref, h0, val):
    for g in range(WIN_HEADS):
        ref[:, (h0 + g) * HD_B:(h0 + g + 1) * HD_B] = val[g * Q_BLOCK:(g + 1) * Q_BLOCK].astype(ref.dtype)


def _win_scores(q, k_ref, h0, bias_ref, n, nblk):
    a = lax.broadcasted_iota(jnp.int32, (WIN_HEADS, Q_BLOCK, Q_BLOCK), 1)
    cc = lax.broadcasted_iota(jnp.int32, (WIN_HEADS, Q_BLOCK, Q_BLOCK), 2)
    valid = [(cc >= a) & (n > 0), None, (cc <= a) & (n < nblk - 1)]
    out = []
    for j in range(3):
        sc = lax.dot_general(q, k_ref[_win_kv_rows(n, j, nblk), _win_kv_cols(h0)], NT, preferred_element_type=F32)
        sc = (sc.reshape(WIN_HEADS, Q_BLOCK, Q_BLOCK) * WIN_SCALE
              + bias_ref[h0:h0 + WIN_HEADS, :, j * Q_BLOCK:(j + 1) * Q_BLOCK])
        if valid[j] is not None:
            sc = jnp.where(valid[j], sc, -1e30)
        out.append(sc)
    return out


def _win_sink(sink_ref, h0):
    hs = lax.broadcasted_iota(jnp.int32, (WIN_HEADS, Q_BLOCK, 1), 0)
    sk = jnp.zeros((WIN_HEADS, Q_BLOCK, 1), F32)
    for g in range(WIN_HEADS):
        sk = jnp.where(hs == g, sink_ref[h0 + g], sk)
    return sk


def _win_fwd(proj_b, bias, sinks):
    s = proj_b.shape[0]
    nblk = s // Q_BLOCK
    rows = WIN_HEADS * Q_BLOCK

    def body(sink_ref, q_ref, k_ref, v_ref, bias_ref, o_ref, lse_ref):
        n = pl.program_id(0)
        for h0 in range(0, H_B, WIN_HEADS):
            sk = _win_sink(sink_ref, h0)
            q = _win_stack(q_ref, h0)
            ss = _win_scores(q, k_ref, h0, bias_ref, n, nblk)
            m = jnp.maximum(jnp.max(jnp.maximum(jnp.maximum(ss[0], ss[1]), ss[2]), axis=2, keepdims=True), sk)
            es = [jnp.exp(sc - m) for sc in ss]
            l = jnp.sum(es[0] + es[1] + es[2], axis=2, keepdims=True) + jnp.exp(sk - m)
            acc = jnp.zeros((rows, HD_B), F32)
            for j, e in enumerate(es):
                p = (e / l).astype(BF16).reshape(rows, Q_BLOCK)
                acc = acc + jnp.dot(p, v_ref[_win_kv_rows(n, j, nblk), _win_kv_cols(h0)],
                                    preferred_element_type=F32)
            _win_unstack(o_ref, h0, acc)
            lse_ref[h0:h0 + WIN_HEADS] = m + jnp.log(l)

    kv_w = KV_B * HD_B
    return pl.pallas_call(
        body, name="win_fwd", grid=(nblk,),
        in_specs=[pl.BlockSpec(memory_space=pltpu.SMEM), _bs((Q_BLOCK, H_B * HD_B), lambda n: (n, P_QB // (H_B * HD_B))),
                  _bs((s, kv_w), lambda n: (0, P_KB // kv_w)), _bs((s, kv_w), lambda n: (0, P_VB // kv_w)),
                  _bs((H_B, Q_BLOCK, SPAN), lambda n: (0, 0, 0))],
        out_specs=[_bs((Q_BLOCK, H_B * HD_B), lambda n: (n, 0)), _bs((H_B, Q_BLOCK, 1), lambda n: (0, n, 0))],
        out_shape=[jax.ShapeDtypeStruct((s, H_B * HD_B), F32), jax.ShapeDtypeStruct((H_B, s, 1), F32)],
        compiler_params=_arb(1),
    )(sinks, proj_b, proj_b, proj_b, bias)


def _win_bwd(proj_b, bias, sinks, do_b, lse):
    s = proj_b.shape[0]
    nblk = s // Q_BLOCK
    rows = WIN_HEADS * Q_BLOCK
    spad = s + 2 * WINDOW

    def body(sink_ref, q_ref, k_ref, v_ref, bias_ref, do_ref, lse_ref, dq_ref, dk_ref, dv_ref, db_ref, dsk_ref):
        n = pl.program_id(0)

        @pl.when(n == 0)
        def _():
            dk_ref[...] = jnp.zeros(dk_ref.shape, F32)
            dv_ref[...] = jnp.zeros(dv_ref.shape, F32)
            db_ref[...] = jnp.zeros(db_ref.shape, F32)
            dsk_ref[...] = jnp.zeros(dsk_ref.shape, F32)

        for h0 in range(0, H_B, WIN_HEADS):
            heads = slice(h0, h0 + WIN_HEADS)
            sk = _win_sink(sink_ref, h0)
            q = _win_stack(q_ref, h0)
            dob = _win_stack(do_ref, h0)
            lse_v = lse_ref[heads]
            ss = _win_scores(q, k_ref, h0, bias_ref, n, nblk)
            ps = [jnp.exp(sc - lse_v) for sc in ss]
            dps = [lax.dot_general(dob, v_ref[_win_kv_rows(n, j, nblk), _win_kv_cols(h0)], NT,
                                   preferred_element_type=F32).reshape(WIN_HEADS, Q_BLOCK, Q_BLOCK) for j in range(3)]
            delta = jnp.sum(ps[0] * dps[0] + ps[1] * dps[1] + ps[2] * dps[2], axis=2, keepdims=True)
            dq = jnp.zeros((rows, HD_B), F32)
            for j in range(3):
                ds = ps[j] * (dps[j] - delta)
                db_ref[heads, :, j * Q_BLOCK:(j + 1) * Q_BLOCK] += ds
                dsb = (ds * WIN_SCALE).astype(BF16).reshape(rows, Q_BLOCK)
                dq = dq + jnp.dot(dsb, k_ref[_win_kv_rows(n, j, nblk), _win_kv_cols(h0)],
                                  preferred_element_type=F32)
                krows = pl.ds(pl.multiple_of((n + j) * Q_BLOCK, Q_BLOCK), Q_BLOCK)
                dk_ref[krows, _win_kv_cols(h0)] += lax.dot_general(dsb, q, TN, preferred_element_type=F32)
                dv_ref[krows, _win_kv_cols(h0)] += lax.dot_general(
                    ps[j].astype(BF16).reshape(rows, Q_BLOCK), dob, TN, preferred_element_type=F32)
            dsk_ref[heads] += -(jnp.exp(sk - lse_v) * delta)
            _win_unstack(dq_ref, h0, dq)

    kv_w = KV_B * HD_B
    qspec = _bs((Q_BLOCK, H_B * HD_B), lambda n: (n, 0))
    kacc = _bs((spad, kv_w), lambda n: (0, 0))
    return pl.pallas_call(
        body, name="win_bwd", grid=(nblk,),
        in_specs=[pl.BlockSpec(memory_space=pltpu.SMEM), _bs((Q_BLOCK, H_B * HD_B), lambda n: (n, P_QB // (H_B * HD_B))),
                  _bs((s, kv_w), lambda n: (0, P_KB // kv_w)), _bs((s, kv_w), lambda n: (0, P_VB // kv_w)),
                  _bs((H_B, Q_BLOCK, SPAN), lambda n: (0, 0, 0)), qspec, _bs((H_B, Q_BLOCK, 1), lambda n: (0, n, 0))],
        out_specs=[qspec, kacc, kacc, _bs((H_B, Q_BLOCK, SPAN), lambda n: (0, 0, 0)),
                   _bs((H_B, Q_BLOCK, 1), lambda n: (0, 0, 0))],
        out_shape=[jax.ShapeDtypeStruct((s, H_B * HD_B), BF16), jax.ShapeDtypeStruct((spad, kv_w), F32),
                   jax.ShapeDtypeStruct((spad, kv_w), F32), jax.ShapeDtypeStruct((H_B, Q_BLOCK, SPAN), F32),
                   jax.ShapeDtypeStruct((H_B, Q_BLOCK, 1), F32)],
        compiler_params=_arb(1),
    )(sinks, proj_b, proj_b, proj_b, bias, do_b, lse)


def _win_param_grads(bucket, dbias, dsink_rows):
    def body(bk_ref, db_ref, ds_ref, o_ref):
        bk = bk_ref[...]
        dbv = db_ref[...]
        lane = lax.broadcasted_iota(jnp.int32, (1, LANES), 1)
        res = jnp.zeros((1, LANES), F32)
        for b in range(NUM_BUCKETS):
            tot = jnp.sum(jnp.sum(jnp.where(bk == b, dbv, 0.0), axis=1, keepdims=True), axis=0, keepdims=True)
            res = jnp.where(lane == b, tot, res)
        stot = jnp.sum(ds_ref[...], axis=0, keepdims=True)
        o_ref[...] = jnp.where(lane == NUM_BUCKETS, stot, res)

    return pl.pallas_call(
        body, name="win_param_grads", grid=(H_B,),
        in_specs=[_bs((Q_BLOCK, SPAN), lambda h: (0, 0)), _bs((None, Q_BLOCK, SPAN), lambda h: (h, 0, 0)),
                  _bs((None, Q_BLOCK, 1), lambda h: (h, 0, 0))],
        out_specs=_bs((None, 1, LANES), lambda h: (h, 0, 0)),
        out_shape=jax.ShapeDtypeStruct((H_B, 1, LANES), F32),
        compiler_params=_arb(1),
    )(bucket, dbias, dsink_rows)


def _mix_out_norm(proj, o_a, o_b, w_out, x, g2, ts=512):
    s = o_a.shape[0]
    wide = lambda cb: _bs((ts, D_MODEL), lambda i: (i, cb))

    def body(ga_ref, gb_ref, oa_ref, ob_ref, w_ref, x_ref, g_ref, m_ref, x1_ref, h_ref, r_ref):
        mixed = (jax.nn.sigmoid(ga_ref[...]) * oa_ref[...] + jax.nn.sigmoid(gb_ref[...]) * ob_ref[...]).astype(BF16)
        m_ref[...] = mixed
        x1 = x_ref[...] + jnp.dot(mixed, w_ref[...], preferred_element_type=F32)
        x1_ref[...] = x1
        r = lax.rsqrt(jnp.mean(x1 * x1, axis=-1, keepdims=True) + EPS)
        h_ref[...] = (x1 * r * g_ref[...]).astype(BF16)
        r_ref[...] = r

    return pl.pallas_call(
        body, name="mix_out_norm", grid=(s // ts,),
        in_specs=[wide(P_GA // D_MODEL), wide(P_GB // D_MODEL), wide(0), wide(0),
                  _bs((D_MODEL, D_MODEL), lambda i: (0, 0)), wide(0), _bs((1, D_MODEL), lambda i: (0, 0))],
        out_specs=[wide(0), wide(0), wide(0), _bs((ts, 1), lambda i: (i, 0))],
        out_shape=[jax.ShapeDtypeStruct((s, D_MODEL), BF16), jax.ShapeDtypeStruct((s, D_MODEL), F32),
                   jax.ShapeDtypeStruct((s, D_MODEL), BF16), jax.ShapeDtypeStruct((s, 1), F32)],
        compiler_params=_arb(1),
    )(proj, proj, o_a, o_b, w_out, x, g2)


def _mix_out_bwd(dx1_b, w_out, proj, o_a, o_b, after=None, ts=512):
    s = o_a.shape[0]
    wide = lambda cb: _bs((ts, D_MODEL), lambda i: (i, cb))

    def body(dx_ref, w_ref, ga_ref, gb_ref, oa_ref, ob_ref, *rest):
        doa_ref, dob_ref, dga_ref, dgb_ref = rest[-4:]
        dm = lax.dot_general(dx_ref[...], w_ref[...], NT, preferred_element_type=F32)
        sa = jax.nn.sigmoid(ga_ref[...])
        sb = jax.nn.sigmoid(gb_ref[...])
        doa_ref[...] = dm * sa
        dob_ref[...] = (dm * sb).astype(BF16)
        dga_ref[...] = (dm * oa_ref[...] * (sa * (1.0 - sa))).astype(BF16)
        dgb_ref[...] = (dm * ob_ref[...] * (sb * (1.0 - sb))).astype(BF16)

    extra = [after] if after is not None else []
    return pl.pallas_call(
        body, name="mix_out_bwd", grid=(s // ts,),
        in_specs=[wide(0), _bs((D_MODEL, D_MODEL), lambda i: (0, 0)), wide(P_GA // D_MODEL), wide(P_GB // D_MODEL),
                  wide(0), wide(0)] + [pl.BlockSpec(memory_space=pl.ANY)] * len(extra),
        out_specs=[wide(0)] * 4,
        out_shape=[jax.ShapeDtypeStruct((s, D_MODEL), F32), jax.ShapeDtypeStruct((s, D_MODEL), BF16),
                   jax.ShapeDtypeStruct((s, D_MODEL), BF16), jax.ShapeDtypeStruct((s, D_MODEL), BF16)],
        compiler_params=_arb(1),
    )(dx1_b, w_out, proj, proj, o_a, o_b, *extra)


CONV_CHUNK = 128
N_SLAB = D_FF // LANES


def _shifted(ref, c, nchunks):
    r0 = c * CONV_CHUNK
    cur = ref[r0:r0 + CONV_CHUNK, :]
    row = lax.broadcasted_iota(jnp.int32, (8, LANES), 0)
    if c > 0:
        prev = ref[r0 - 1:r0 - 1 + CONV_CHUNK, :]
    else:
        down = pltpu.roll(cur, 1, 0)
        prev = jnp.concatenate([jnp.where(row == 0, 0.0, down[:8]), down[8:]], axis=0)
    if c < nchunks - 1:
        nxt = ref[r0 + 1:r0 + 1 + CONV_CHUNK, :]
    else:
        up = pltpu.roll(cur, CONV_CHUNK - 1, 0)
        nxt = jnp.concatenate([up[:-8], jnp.where(row == 7, 0.0, up[-8:])], axis=0)
    return prev, cur, nxt


def _conv_taps(ref, w_ref, b_ref, c, nchunks):
    prev, cur, nxt = _shifted(ref, c, nchunks)
    conv = prev * w_ref[0:1, :] + cur * w_ref[1:2, :] + nxt * w_ref[2:3, :] + b_ref[...]
    return conv, prev, cur, nxt


def _convffn_fwd(u, conv_w, conv_b):
    s = u.shape[0]
    nchunks = s // CONV_CHUNK

    def body(ug_ref, uv_ref, wg_ref, wv_ref, bg_ref, bv_ref, f_ref):
        for c in range(nchunks):
            cg = _conv_taps(ug_ref, wg_ref, bg_ref, c, nchunks)[0]
            cv = _conv_taps(uv_ref, wv_ref, bv_ref, c, nchunks)[0]
            f_ref[c * CONV_CHUNK:(c + 1) * CONV_CHUNK, :] = (cg * jax.nn.sigmoid(cg) * cv).astype(BF16)

    slab = lambda off: _bs((s, LANES), lambda j: (0, off + j))
    wsl = lambda off: _bs((3, LANES), lambda j: (0, off + j))
    bsl = lambda off: _bs((1, LANES), lambda j: (0, off + j))
    return pl.pallas_call(
        body, name="convffn_fwd", grid=(N_SLAB,),
        in_specs=[slab(0), slab(N_SLAB), wsl(0), wsl(N_SLAB), bsl(0), bsl(N_SLAB)],
        out_specs=slab(0), out_shape=jax.ShapeDtypeStruct((s, D_FF), BF16),
        compiler_params=_arb(1),
    )(u, u, conv_w, conv_w, conv_b, conv_b)


def _convffn_bwd(u, conv_w, conv_b, df):
    s = u.shape[0]
    nchunks = s // CONV_CHUNK

    def body(ug_ref, uv_ref, wg_ref, wv_ref, bg_ref, bv_ref, df_ref, du_ref, dw_ref, db_ref, dcg_ref, dcv_ref):
        dwg = [jnp.zeros((1, LANES), F32) for _ in range(3)]
        dwv = [jnp.zeros((1, LANES), F32) for _ in range(3)]
        dbg = jnp.zeros((1, LANES), F32)
        dbv = jnp.zeros((1, LANES), F32)
        for c in range(nchunks):
            rows = slice(c * CONV_CHUNK, (c + 1) * CONV_CHUNK)
            cg, gp, gc, gn = _conv_taps(ug_ref, wg_ref, bg_ref, c, nchunks)
            cv, vp, vc, vn = _conv_taps(uv_ref, wv_ref, bv_ref, c, nchunks)
            dfv = df_ref[rows, :]
            sg = jax.nn.sigmoid(cg)
            dcg = dfv * cv * (sg * (1.0 + cg * (1.0 - sg)))
            dcv = dfv * (cg * sg)
            dcg_ref[rows, :] = dcg
            dcv_ref[rows, :] = dcv
            for t, (tg, tv) in enumerate(((gp, vp), (gc, vc), (gn, vn))):
                dwg[t] = dwg[t] + jnp.sum(tg * dcg, axis=0, keepdims=True)
                dwv[t] = dwv[t] + jnp.sum(tv * dcv, axis=0, keepdims=True)
            dbg = dbg + jnp.sum(dcg, axis=0, keepdims=True)
            dbv = dbv + jnp.sum(dcv, axis=0, keepdims=True)
        for t in range(3):
            dw_ref[0, t:t + 1, :] = dwg[t]
            dw_ref[1, t:t + 1, :] = dwv[t]
        db_ref[0] = dbg
        db_ref[1] = dbv
        for half, (dc_ref, w_ref) in enumerate(((dcg_ref, wg_ref), (dcv_ref, wv_ref))):
            for c in range(nchunks):
                prev, cur, nxt = _shifted(dc_ref, c, nchunks)
                du = nxt * w_ref[0:1, :] + cur * w_ref[1:2, :] + prev * w_ref[2:3, :]
                du_ref[half, c * CONV_CHUNK:(c + 1) * CONV_CHUNK, :] = du.astype(BF16)

    slab = lambda off: _bs((s, LANES), lambda j: (0, off + j))
    wsl = lambda off: _bs((3, LANES), lambda j: (0, off + j))
    bsl = lambda off: _bs((1, LANES), lambda j: (0, off + j))
    return pl.pallas_call(
        body, name="convffn_bwd", grid=(N_SLAB,),
        in_specs=[slab(0), slab(N_SLAB), wsl(0), wsl(N_SLAB), bsl(0), bsl(N_SLAB), slab(0)],
        out_specs=[_bs((2, s, LANES), lambda j: (0, 0, j)), _bs((2, 3, LANES), lambda j: (0, 0, j)),
                   _bs((2, 1, LANES), lambda j: (0, 0, j))],
        out_shape=[jax.ShapeDtypeStruct((2, s, D_FF), BF16), jax.ShapeDtypeStruct((2, 3, D_FF), F32),
                   jax.ShapeDtypeStruct((2, 1, D_FF), F32)],
        scratch_shapes=[pltpu.VMEM((s, LANES), F32), pltpu.VMEM((s, LANES), F32)],
        compiler_params=_arb(1),
    )(u, u, conv_w, conv_w, conv_b, conv_b, df)


def _row_tile(rows, limit=512):
    best = rows
    for t in range(8, min(rows, limit) + 1, 8):
        if rows % t == 0:
            best = t
    return best if rows % 8 == 0 else rows


ADAM_C1 = 1.0 - ADAM_B1 ** ADAM_STEP
ADAM_C2 = 1.0 - ADAM_B2 ** ADAM_STEP


def _adamw_math(w, gv, m, v):
    nm = ADAM_B1 * m + (1.0 - ADAM_B1) * gv
    nv = ADAM_B2 * v + (1.0 - ADAM_B2) * (gv * gv)
    m_hat = nm / ADAM_C1
    v_hat = nv / ADAM_C2
    return -ADAM_LR * (m_hat / (jnp.sqrt(v_hat) + ADAM_EPS) + ADAM_WD * w), nm, nv


def _adamw_halves(name, core, w, mine, theirs, m, v):
    half, cols = mine.shape
    tr = _row_tile(half)
    nr = half // tr

    def body(core_ref, w_ref, mine_ref, theirs_ref, m_ref, v_ref, g_ref, d_ref, nm_ref, nv_ref):
        gv = jnp.where(pl.program_id(0) == core_ref[0], mine_ref[...], theirs_ref[...])
        g_ref[...] = gv
        d_ref[...], nm_ref[...], nv_ref[...] = _adamw_math(w_ref[...], gv, m_ref[...], v_ref[...])

    full = pl.BlockSpec((tr, cols), lambda hf, r, cr: (hf * nr + r, 0))
    part = pl.BlockSpec((tr, cols), lambda hf, r, cr: (r, 0))
    return pl.pallas_call(
        body, name=name,
        grid_spec=pltpu.PrefetchScalarGridSpec(num_scalar_prefetch=1, grid=(2, nr),
                                               in_specs=[full, part, part, full, full], out_specs=[full] * 4),
        out_shape=[jax.ShapeDtypeStruct((2 * half, cols), F32)] * 4, compiler_params=_arb(2),
    )(core, w, mine, theirs, m, v)


ANY = pl.BlockSpec(memory_space=pl.ANY)


def _mesh_pos():
    return lax.axis_index("x"), lax.axis_index("y"), lax.axis_index("c")


def _other_chips(x, y):
    return [(1 - x, y), (x, 1 - y), (1 - x, 1 - y)]


def _forward_to_sibling(gathered):
    n = len(gathered)

    def body(*refs):
        in_refs, out_refs = refs[:n], refs[n:2 * n]
        send_sems, recv_sems = refs[2 * n:]
        x, y, c = _mesh_pos()
        cps = []
        for i in range(n):
            for k, chip in enumerate(_other_chips(x, y)):
                pk = 2 * chip[0] + chip[1]
                sems = dict(send_sem=send_sems.at[3 * i + k], recv_sem=recv_sems.at[3 * i + k],
                            device_id=(x, y, 1 - c), device_id_type=MESH)
                sent = pltpu.make_async_remote_copy(src_ref=in_refs[i].at[pk, c], dst_ref=out_refs[i].at[pk, c], **sems)
                sent.start()
                theirs = out_refs[i].at[pk, 1 - c]
                cps.append((sent, pltpu.make_async_remote_copy(src_ref=theirs, dst_ref=theirs, **sems)))
        for sent, arrived in cps:
            sent.wait_send()
            arrived.wait_recv()

    return pl.pallas_call(
        body, name="forward_to_sibling", in_specs=[ANY] * n, out_specs=[ANY] * n,
        out_shape=[jax.ShapeDtypeStruct(a.shape, a.dtype) for a in gathered],
        input_output_aliases={i: i for i in range(n)},
        scratch_shapes=[pltpu.SemaphoreType.DMA((3 * n,)), pltpu.SemaphoreType.DMA((3 * n,))],
    )(*gathered)


def _rs_pair_exchange(name, grads):
    n = len(grads)

    def body(*refs):
        g_refs, o_refs = refs[:n], refs[n:2 * n]
        send_sems, recv_sems = refs[2 * n:]
        x, y, c = _mesh_pos()
        cps = []
        for i in range(n):
            cp = pltpu.make_async_remote_copy(
                src_ref=g_refs[i].at[:, 1 - c], dst_ref=o_refs[i],
                send_sem=send_sems.at[i], recv_sem=recv_sems.at[i], device_id=(x, y, 1 - c), device_id_type=MESH)
            cp.start()
            cps.append(cp)
        for cp in cps:
            cp.wait()

    return pl.pallas_call(
        body, name=name, in_specs=[ANY] * n, out_specs=[ANY] * n,
        out_shape=[jax.ShapeDtypeStruct((4,) + g.shape[2:], F32) for g in grads],
        scratch_shapes=[pltpu.SemaphoreType.DMA((n,)), pltpu.SemaphoreType.DMA((n,))],
    )(*grads)


def _rs_pair_add(name, core, g, recv):
    _, half, cols = recv.shape
    tr = _row_tile(half)
    nr = half // tr

    def body(core_ref, g_ref, r_ref, o_ref):
        o_ref[...] = (g_ref[...] + r_ref[...]).astype(BF16)

    return pl.pallas_call(
        body, name=name,
        grid_spec=pltpu.PrefetchScalarGridSpec(
            num_scalar_prefetch=1, grid=(4, nr),
            in_specs=[pl.BlockSpec((None, None, tr, cols), lambda q, r, cr: (q, cr[0], r, 0)),
                      pl.BlockSpec((None, tr, cols), lambda q, r, cr: (q, r, 0))],
            out_specs=pl.BlockSpec((None, tr, cols), lambda q, r, cr: (q, r, 0))),
        out_shape=jax.ShapeDtypeStruct((4, half, cols), BF16),
        compiler_params=_arb(2),
    )(core, g, recv)


def _rs_final_add(name, chip, pair, recv):
    _, half, cols = pair.shape
    tr = _row_tile(half)

    def body(chip_ref, p_ref, r_ref, o_ref):
        o_ref[...] = ((p_ref[...].astype(F32) + r_ref[0].astype(F32)) + r_ref[1].astype(F32)) + r_ref[2].astype(F32)

    return pl.pallas_call(
        body, name=name,
        grid_spec=pltpu.PrefetchScalarGridSpec(
            num_scalar_prefetch=1, grid=(half // tr,),
            in_specs=[pl.BlockSpec((None, tr, cols), lambda r, ch: (ch[0], r, 0)),
                      pl.BlockSpec((3, tr, cols), lambda r, ch: (0, r, 0))],
            out_specs=pl.BlockSpec((tr, cols), lambda r, ch: (r, 0))),
        out_shape=jax.ShapeDtypeStruct((half, cols), F32),
        compiler_params=_arb(1),
    )(chip, pair, recv)


def _rs_pair_share(halves):
    n = len(halves)

    def body(*refs):
        h_refs, o_refs = refs[:n], refs[n:2 * n]
        send_sems, recv_sems = refs[2 * n:]
        x, y, c = _mesh_pos()
        cps = []
        for i in range(n):
            cp = pltpu.make_async_remote_copy(src_ref=h_refs[i], dst_ref=o_refs[i], send_sem=send_sems.at[i],
                                              recv_sem=recv_sems.at[i], device_id=(x, y, 1 - c), device_id_type=MESH)
            cp.start()
            cps.append(cp)
        for cp in cps:
            cp.wait()

    return pl.pallas_call(
        body, name="rs_pair_share", in_specs=[ANY] * n, out_specs=[ANY] * n,
        out_shape=[jax.ShapeDtypeStruct(h.shape, F32) for h in halves],
        scratch_shapes=[pltpu.SemaphoreType.DMA((n,)), pltpu.SemaphoreType.DMA((n,))],
    )(*halves)


HBM = pl.BlockSpec(memory_space=pltpu.HBM)
SEM = pl.BlockSpec(memory_space=pltpu.SEMAPHORE)


class _SplitExchange:
    def __init__(self, name, srcs, land_shapes, src_of, dst_of, arrive_of, to_sibling=False):
        self.name, self.srcs, self.land_shapes = name, list(srcs), list(land_shapes)
        self.src_of, self.dst_of, self.arrive_of = src_of, dst_of, arrive_of
        self.to_sibling = to_sibling
        self.fan = 1 if to_sibling else 3

    def _copies(self, src_refs, land_refs, send_sems, recv_sems):
        x, y, c = _mesh_pos()
        p = 2 * x + y
        if self.to_sibling:
            peers = [((x, y, 1 - c), 1 - c)]
        else:
            peers = [((*chip, c), 2 * chip[0] + chip[1]) for chip in _other_chips(x, y)]
        out = []
        for i, (src, land) in enumerate(zip(src_refs, land_refs)):
            for k, (peer, pk) in enumerate(peers):
                sems = dict(send_sem=send_sems.at[self.fan * i + k], recv_sem=recv_sems.at[self.fan * i + k],
                            device_id=peer, device_id_type=MESH)
                sent = pltpu.make_async_remote_copy(src_ref=self.src_of(src, k, p, pk, c),
                                                    dst_ref=self.dst_of(land, k, p, pk, c), **sems)
                here = self.arrive_of(land, k, p, pk, c)
                out.append((sent, pltpu.make_async_remote_copy(src_ref=here, dst_ref=here, **sems)))
        return out

    def start(self, after=None):
        n = len(self.srcs)
        n_in = 2 * n + (after is not None)

        def body(*refs):
            for sent, _ in self._copies(refs[:n], refs[n:2 * n], refs[n_in], refs[n_in + 1]):
                sent.start()
            refs[-1][...] = jnp.zeros((8, LANES), F32)

        lands = [lax.empty(shape, src.dtype) for shape, src in zip(self.land_shapes, self.srcs)]
        operands = [pltpu.with_memory_space_constraint(a, pltpu.HBM) for a in self.srcs + lands]
        outs = pl.pallas_call(
            body, name=self.name + "_start",
            out_shape=(pltpu.SemaphoreType.DMA((self.fan * n,)), pltpu.SemaphoreType.DMA((self.fan * n,)),
                       *[pltpu.HBM(a.shape, a.dtype) for a in operands], jax.ShapeDtypeStruct((8, LANES), F32)),
            in_specs=[HBM] * (2 * n) + [ANY] * (after is not None),
            out_specs=(SEM, SEM, *[HBM] * (2 * n), pl.BlockSpec(memory_space=pltpu.VMEM)),
            input_output_aliases={j: 2 + j for j in range(2 * n)},
            compiler_params=pltpu.CompilerParams(has_side_effects=pltpu.SideEffectType.DATAFLOW_SIDE_EFFECTING),
        )(*operands, *([after] if after is not None else []))
        self._sems, self._thru = outs[:2], list(outs[2:2 + 2 * n])
        return outs[-1]

    def wait(self, after):
        n = len(self.srcs)

        def body(*refs):
            for sent, arrived in self._copies(refs[:n], refs[n:2 * n], refs[2 * n], refs[2 * n + 1]):
                sent.wait_send()
                arrived.wait_recv()

        after = list(after) if isinstance(after, (list, tuple)) else [after]
        outs = pl.pallas_call(
            body, name=self.name + "_wait",
            out_shape=tuple(pltpu.HBM(a.shape, a.dtype) for a in self._thru),
            in_specs=[HBM] * (2 * n) + [SEM, SEM] + [ANY] * len(after), out_specs=tuple([HBM] * (2 * n)),
            input_output_aliases={j: j for j in range(2 * n)},
            compiler_params=pltpu.CompilerParams(has_side_effects=pltpu.SideEffectType.DATAFLOW_SIDE_EFFECTING),
        )(*self._thru, *self._sems, *after)
        return list(outs[:n]), list(outs[n:])


def _small_allreduce(parts):
    n = len(parts)

    def body(*refs):
        in_refs, out_refs, gather_refs = refs[:n], refs[n:2 * n], refs[2 * n:3 * n]
        send_sems, recv_sems = refs[3 * n:]
        x, y, c = _mesh_pos()
        me = 4 * x + 2 * y + c
        cps = []
        for i in range(n):
            gather_refs[i][me] = in_refs[i][...]
            for j in range(1, 8):
                peer = (x ^ (j >> 2), y ^ ((j >> 1) & 1), c ^ (j & 1))
                cp = pltpu.make_async_remote_copy(
                    src_ref=in_refs[i], dst_ref=gather_refs[i].at[me], send_sem=send_sems.at[7 * i + j - 1],
                    recv_sem=recv_sems.at[7 * i + j - 1], device_id=peer, device_id_type=MESH)
                cp.start()
                cps.append(cp)
        for i in range(n):
            for j in range(1, 8):
                peer_id = 4 * (x ^ (j >> 2)) + 2 * (y ^ ((j >> 1) & 1)) + (c ^ (j & 1))
                slot = gather_refs[i].at[peer_id]
                pltpu.make_async_remote_copy(src_ref=slot, dst_ref=slot, send_sem=send_sems.at[7 * i + j - 1],
                                             recv_sem=recv_sems.at[7 * i + j - 1], device_id=(x, y, c),
                                             device_id_type=MESH).wait_recv()
        for cp in cps:
            cp.wait_send()
        for i in range(n):
            tot = gather_refs[i][0]
            for d in range(1, 8):
                tot = tot + gather_refs[i][d]
            out_refs[i][...] = tot

    vmem = pl.BlockSpec(memory_space=pltpu.VMEM)
    return pl.pallas_call(
        body, name="small_allreduce", in_specs=[vmem] * n, out_specs=[vmem] * n,
        out_shape=[jax.ShapeDtypeStruct(p.shape, F32) for p in parts],
        scratch_shapes=[pltpu.VMEM((8,) + p.shape, F32) for p in parts]
        + [pltpu.SemaphoreType.DMA((7 * n,)), pltpu.SemaphoreType.DMA((7 * n,))],
    )(*parts)


def _adamw_small(ws, gs, ms, vs):
    n = len(ws)

    def body(*refs):
        for i in range(n):
            w_ref, g_ref, m_ref, v_ref = refs[i], refs[n + i], refs[2 * n + i], refs[3 * n + i]
            d_ref, nm_ref, nv_ref = refs[4 * n + i], refs[5 * n + i], refs[6 * n + i]
            d_ref[...], nm_ref[...], nv_ref[...] = _adamw_math(w_ref[...], g_ref[...], m_ref[...], v_ref[...])

    vmem = pl.BlockSpec(memory_space=pltpu.VMEM)
    shapes = [jax.ShapeDtypeStruct(w.shape, F32) for w in ws]
    outs = pl.pallas_call(body, name="adamw_small", in_specs=[vmem] * (4 * n), out_specs=[vmem] * (3 * n),
                          out_shape=shapes * 3)(*ws, *gs, *ms, *vs)
    return outs[:n], outs[n:2 * n], outs[2 * n:]


W_IN_PIECES = ((0, 256, P_QLAT), (256, 384, P_CKV), (384, 448, P_KR), (448, 1472, P_QB), (1472, 1728, P_KB),
               (1728, 1984, P_VB), (1984, 3008, P_GA), (3008, 4032, P_GB))
W_IN_SHARD = 1008


def _w_in_from_shards(shards):
    cols = []
    for lo, hi, _ in sorted(W_IN_PIECES, key=lambda piece: piece[2]):
        for q in range(4):
            a, b = max(lo, q * W_IN_SHARD), min(hi, (q + 1) * W_IN_SHARD)
            if a < b:
                cols.append(shards[q][:, a - q * W_IN_SHARD:b - q * W_IN_SHARD])
    cols.append(jnp.zeros((shards.shape[1], W_IN_PAD - 4 * W_IN_SHARD), shards.dtype))
    return jnp.concatenate(cols, axis=1)


def _w_in_to_shards(p):
    shards = []
    for q in range(4):
        cols = []
        for lo, hi, at in W_IN_PIECES:
            a, b = max(lo, q * W_IN_SHARD), min(hi, (q + 1) * W_IN_SHARD)
            if a < b:
                cols.append(p[:, at + a - lo:at + b - lo])
        shards.append(jnp.concatenate(cols, axis=1))
    return jnp.stack(shards)


def _col_shards(w):
    r, c4 = w.shape
    return w.reshape(r, 4, c4 // 4).transpose(1, 0, 2)


def _local_step(x, positions, target, norm1_g, first_weights, q_a_norm_g, kv_a_norm_g, rel_bias, sinks,
                late_weights, norm2_g, conv_b, final_norm_g, early_grads=None, last_grads=None):
    s = x.shape[0]
    half = QK_ROPE // 2
    inv_freq = jnp.asarray(np.float32(ROPE_THETA) ** (-np.arange(half, dtype=np.float32) / np.float32(half)))
    ang = positions.astype(F32)[:, None] * inv_freq[None, :]
    cos, sin = jnp.cos(ang), jnp.sin(ang)
    z64 = jnp.zeros((s, 64), F32)
    cos_t = jnp.concatenate([cos, cos, z64], axis=1)
    sin_t = jnp.concatenate([-sin, sin, z64], axis=1)
    bucket = _t5_bucket_table()
    sinks1 = sinks.reshape(H_B)

    h1, rstd1 = _rmsnorm_fwd("norm1_fwd", x, norm1_g, D_MODEL, 0)
    bias = _win_bias(bucket, rel_bias)
    w_in_p, wq, wkv = first_weights([h1, bias, cos_t, sin_t])
    proj, proj_b = _matmul("proj", h1, w_in_p, out_shape=(s, W_IN_PAD), out_dtype=F32, grid=(s // MM_ROWS, W_IN_PAD // 1024, 1),
                           a_spec=_bs((MM_ROWS, D_MODEL), lambda i, j, k: (i, 0)), b_spec=_bs((D_MODEL, 1024), lambda i, j, k: (0, j)),
                           o_spec=_bs((MM_ROWS, 1024), lambda i, j, k: (i, j)), contract=NN, bf16_copy=True)
    qn, cn, rstd_q, rstd_c = _lat_norms(proj, q_a_norm_g, kv_a_norm_g)
    q = _q_heads(qn, wq, cos_t, sin_t)
    k, v = _kv_heads(cn, wkv, proj, cos_t, sin_t)
    o_a, lse_a = _mla_fwd(q, k, v)

    o_b, lse_b = _win_fwd(proj_b, bias, sinks1)

    w_out, w_up, w_down, conv_w = late_weights([o_a, o_b])
    row512 = lambda w: _bs((MM_ROWS, w), lambda i, j, k: (i, 0))
    whole = lambda r, c: _bs((r, c), lambda i, j, k: (0, 0))
    mixed, x1, h2, rstd2 = _mix_out_norm(proj, o_a, o_b, w_out, x, norm2_g)
    u = _matmul("ffn_up", h2, w_up, out_shape=(s, 2 * D_FF), out_dtype=F32, grid=(s // MM_ROWS, 4, 1),
                a_spec=_bs((MM_ROWS, D_MODEL), lambda i, j, k: (i, 0)), b_spec=_bs((D_MODEL, D_FF // 2), lambda i, j, k: (0, j)),
                o_spec=_bs((MM_ROWS, D_FF // 2), lambda i, j, k: (i, j)), contract=NN)
    f = _convffn_fwd(u, conv_w, conv_b)
    loss, dx2, d_final_g, dx2_b = _ffn_down_loss(f, w_down, x1, target, final_norm_g.reshape(1, D_MODEL))
    tk = min(s, DW_ROWS)

    df = _matmul("ffn_down_dx", dx2_b, w_down, out_shape=(s, D_FF), out_dtype=F32, grid=(s // MM_ROWS, 2, 1),
                 a_spec=row512(D_MODEL), b_spec=_bs((D_FF // 2, D_MODEL), lambda i, j, k: (j, 0)),
                 o_spec=_bs((MM_ROWS, D_FF // 2), lambda i, j, k: (i, j)), contract=NT)
    d_w_down = _matmul("ffn_down_dw", f, dx2_b, out_shape=(D_FF, D_MODEL), out_dtype=F32, grid=(2, 1, s // tk),
                       a_spec=_bs((tk, D_FF // 2), lambda i, j, k: (k, i)), b_spec=_bs((tk, D_MODEL), lambda i, j, k: (k, 0)),
                       o_spec=_bs((D_FF // 2, D_MODEL), lambda i, j, k: (i, 0)), contract=TN)
    du, d_conv_w2, d_conv_b2 = _convffn_bwd(u, conv_w, conv_b, df)
    kc = D_FF // 2
    dx1, d_norm2_g, dx1_b = _matmul_norm_bwd(
        "ffn_up_dx_norm2_bwd", du, w_up, grid=(s // MM_ROWS, 4),
        a_spec=_bs((None, MM_ROWS, kc), lambda i, k: (k // 2, i, k % 2)), b_spec=_bs((D_MODEL, kc), lambda i, k: (0, k)),
        x=x1, rstd=rstd2, g=norm2_g, res=dx2, bf16_copy=True)
    d_w_up = _matmul("ffn_up_dw", h2, du, out_shape=(D_MODEL, 2 * D_FF), out_dtype=F32, grid=(1, 4, s // tk),
                     a_spec=_bs((tk, D_MODEL), lambda i, j, k: (k, 0)),
                     b_spec=_bs((None, tk, kc), lambda i, j, k: (j // 2, k, j % 2)),
                     o_spec=_bs((D_MODEL, kc), lambda i, j, k: (0, j)), contract=TN)

    d_w_out = _matmul("attn_out_dw", mixed, dx1_b, out_shape=(D_MODEL, D_MODEL), out_dtype=F32, grid=(1, 1, s // tk),
                      a_spec=_bs((tk, D_MODEL), lambda i, j, k: (k, 0)), b_spec=_bs((tk, D_MODEL), lambda i, j, k: (k, 0)),
                      o_spec=whole(D_MODEL, D_MODEL), contract=TN)
    token, early_grads_on = early_grads(d_w_out, d_w_up, d_w_down) if early_grads is not None else (None, None)
    do_a, do_b, d_ga, d_gb = _mix_out_bwd(dx1_b, w_out, proj, o_a, o_b, after=token)
    if early_grads_on is not None:
        sinks1 = sinks1 + early_grads_on(d_ga)[0, :H_B]

    d_qb, dk_pad, dv_pad, dbias, dsink_rows = _win_bwd(proj_b, bias, sinks1, do_b, lse_b)
    wp = _win_param_grads(bucket, dbias, dsink_rows)[:, 0, :]
    d_rel_bias = wp[:, :NUM_BUCKETS].T
    d_sinks = wp[:, NUM_BUCKETS].reshape(1, H_B)
    d_kb = dk_pad[WINDOW:WINDOW + s].astype(BF16)
    d_vb = dv_pad[WINDOW:WINDOW + s].astype(BF16)

    dq_pre, dkv_pre, dkr = _mla_bwd(q, k, v, do_a, o_a, lse_a, cos_t, sin_t)
    d_kr = _mla_key_rope_grad(dkr, cos_t, sin_t)
    th = min(s, HEAD_ROWS)
    hgrid = (s // th, 1, H_A)
    hblock = _bs((None, th, HEAD_PAD), lambda i, j, k: (k, i, 0))
    hrows = lambda w: _bs((th, w), lambda i, j, k: (i, 0))
    dqn = _matmul("q_up_dx", dq_pre, wq, out_shape=(s, Q_LORA), out_dtype=F32, grid=hgrid, a_spec=hblock,
                  b_spec=_bs((None, Q_LORA, HEAD_PAD), lambda i, j, k: (k, 0, 0)), o_spec=hrows(Q_LORA), contract=NT)
    dcn = _matmul("kv_up_dx", dkv_pre, wkv, out_shape=(s, KV_LORA), out_dtype=F32, grid=hgrid, a_spec=hblock,
                  b_spec=_bs((None, KV_LORA, HEAD_PAD), lambda i, j, k: (k, 0, 0)), o_spec=hrows(KV_LORA), contract=NT)
    wgrid = (H_A, 1, s // th)
    d_wq = _matmul("q_up_dw", qn, dq_pre, out_shape=(H_A, Q_LORA, HEAD_PAD), out_dtype=F32, grid=wgrid,
                   a_spec=_bs((th, Q_LORA), lambda i, j, k: (k, 0)), b_spec=_bs((None, th, HEAD_PAD), lambda i, j, k: (i, k, 0)),
                   o_spec=_bs((None, Q_LORA, HEAD_PAD), lambda i, j, k: (i, 0, 0)), contract=TN)
    d_wkv = _matmul("kv_up_dw", cn, dkv_pre, out_shape=(H_A, KV_LORA, HEAD_PAD), out_dtype=F32, grid=wgrid,
                    a_spec=_bs((th, KV_LORA), lambda i, j, k: (k, 0)), b_spec=_bs((None, th, HEAD_PAD), lambda i, j, k: (i, k, 0)),
                    o_spec=_bs((None, KV_LORA, HEAD_PAD), lambda i, j, k: (i, 0, 0)), contract=TN)
    d_qlat, d_gq = _rmsnorm_bwd("q_norm_bwd", dqn, proj, rstd_q, q_a_norm_g, Q_LORA, P_QLAT // Q_LORA, BF16)
    d_ckv, d_gkv = _rmsnorm_bwd("kv_norm_bwd", dcn, proj, rstd_c, kv_a_norm_g, KV_LORA, P_CKV // KV_LORA, BF16)

    dproj = jnp.concatenate([d_qb, d_ga, d_gb, d_qlat, d_kb, d_vb, d_ckv, d_kr], axis=1)
    d_w_in_p = _matmul("proj_dw", h1, dproj, out_shape=(D_MODEL, W_IN_PAD), out_dtype=F32, grid=(1, W_IN_PAD // 1024, s // tk),
                       a_spec=_bs((tk, D_MODEL), lambda i, j, k: (k, 0)), b_spec=_bs((tk, 1024), lambda i, j, k: (k, j)),
                       o_spec=_bs((D_MODEL, 1024), lambda i, j, k: (0, j)), contract=TN)
    token = last_grads(d_w_in_p, d_wq, d_wkv) if last_grads is not None else None
    dx, d_norm1_g = _matmul_norm_bwd(
        "proj_dx_norm1_bwd", dproj, w_in_p, grid=(s // MM_ROWS, W_IN_PAD // 1024),
        a_spec=_bs((MM_ROWS, 1024), lambda i, k: (i, k)), b_spec=_bs((D_MODEL, 1024), lambda i, k: (0, k)),
        x=x, rstd=rstd1, g=norm1_g, res=dx1, after=token)

    grads = dict(
        norm1_g=d_norm1_g, w_in_p=d_w_in_p, q_a_norm_g=d_gq, wq=d_wq, kv_a_norm_g=d_gkv, wkv=d_wkv,
        rel_bias=d_rel_bias, sinks=d_sinks, w_out=d_w_out, norm2_g=d_norm2_g, w_up=d_w_up,
        conv_w=jnp.concatenate([d_conv_w2[0], d_conv_w2[1]], axis=1),
        conv_b=jnp.concatenate([d_conv_b2[0], d_conv_b2[1]], axis=1),
        w_down=d_w_down, final_norm_g=d_final_g.reshape(D_MODEL))
    return loss, dx, grads


HEADS_PER_SHARD = H_A // 4


def _head_cols(h, width):
    return slice((h % HEADS_PER_SHARD) * width, (h % HEADS_PER_SHARD + 1) * width)


def _wq_heads(shards):
    per = QK_NOPE + QK_ROPE
    w = jnp.stack([shards[h // HEADS_PER_SHARD][:, _head_cols(h, per)] for h in range(H_A)])
    return jnp.pad(w, ((0, 0), (0, 0), (0, HEAD_PAD - per)))


def _wq_shards(d_wq):
    per = QK_NOPE + QK_ROPE
    return jnp.stack([jnp.concatenate([d_wq[h][:, :per] for h in range(q * HEADS_PER_SHARD, (q + 1) * HEADS_PER_SHARD)],
                                      axis=1) for q in range(4)])


def _wkv_heads(shards):
    return jnp.stack([shards[h // HEADS_PER_SHARD][:, _head_cols(h, QK_NOPE + V_DIM)] for h in range(H_A)])


def _wkv_shards(d_wkv):
    return jnp.stack([jnp.concatenate([d_wkv[h] for h in range(q * HEADS_PER_SHARD, (q + 1) * HEADS_PER_SHARD)], axis=1)
                      for q in range(4)])


SMALL = ("norm1_g", "q_a_norm_g", "kv_a_norm_g", "rel_bias", "sinks", "norm2_g", "conv_b", "final_norm_g")
FIRST = ("w_in", "w_q_b", "w_kv_b")
LATER = ("w_out", "w_up", "w_down")
BIG = FIRST + LATER


def kernel(x, positions, norm1_g, w_in, q_a_norm_g, w_q_b, kv_a_norm_g, w_kv_b, rel_bias, sinks, w_out, norm2_g, w_up, conv_w, conv_b, w_down, final_norm_g, loss_target, m_norm1_g, m_w_in, m_q_a_norm_g, m_w_q_b, m_kv_a_norm_g, m_w_kv_b, m_rel_bias, m_sinks, m_w_out, m_norm2_g, m_w_up, m_conv_w, m_conv_b, m_w_down, m_final_norm_g, v_norm1_g, v_w_in, v_q_a_norm_g, v_w_q_b, v_kv_a_norm_g, v_w_kv_b, v_rel_bias, v_sinks, v_w_out, v_norm2_g, v_w_up, v_conv_w, v_conv_b, v_w_down, v_final_norm_g):
    weights = dict(norm1_g=norm1_g, w_in=w_in, q_a_norm_g=q_a_norm_g, w_q_b=w_q_b, kv_a_norm_g=kv_a_norm_g,
                   w_kv_b=w_kv_b, rel_bias=rel_bias, sinks=sinks, w_out=w_out, norm2_g=norm2_g, w_up=w_up,
                   conv_w=conv_w, conv_b=conv_b, w_down=w_down, final_norm_g=final_norm_g)
    mom_m = dict(norm1_g=m_norm1_g, w_in=m_w_in, q_a_norm_g=m_q_a_norm_g, w_q_b=m_w_q_b, kv_a_norm_g=m_kv_a_norm_g,
                 w_kv_b=m_w_kv_b, rel_bias=m_rel_bias, sinks=m_sinks, w_out=m_w_out, norm2_g=m_norm2_g, w_up=m_w_up,
                 conv_w=m_conv_w, conv_b=m_conv_b, w_down=m_w_down, final_norm_g=m_final_norm_g)
    mom_v = dict(norm1_g=v_norm1_g, w_in=v_w_in, q_a_norm_g=v_q_a_norm_g, w_q_b=v_w_q_b, kv_a_norm_g=v_kv_a_norm_g,
                 w_kv_b=v_w_kv_b, rel_bias=v_rel_bias, sinks=v_sinks, w_out=v_w_out, norm2_g=v_norm2_g, w_up=v_w_up,
                 conv_w=v_conv_w, conv_b=v_conv_b, w_down=v_w_down, final_norm_g=v_final_norm_g)
    shard2d = {n: weights[n][0] for n in BIG}
    conv_w_shard = conv_w[0]
    xi, yi, ci = lax.axis_index("x"), lax.axis_index("y"), lax.axis_index("c")
    chip = (2 * xi + yi).astype(jnp.int32)

    core = ci.astype(jnp.int32).reshape(1)
    chip1 = chip.reshape(1)
    cat_cols = lambda a: jnp.concatenate([a[0], a[1], a[2], a[3]], axis=1)
    own_slot = lambda a, own: lax.dynamic_update_index_in_dim(a, own, chip, 0)
    halved = lambda a: a.reshape((2, a.shape[0] // 2) + a.shape[1:])
    quartered = lambda a: a.reshape(4, 2, a.shape[1] // 2, a.shape[2])

    first = [halved(shard2d[n].astype(BF16)) for n in FIRST]
    gather1 = _SplitExchange("gather_first", first, [(4,) + a.shape for a in first],
                             src_of=lambda ref, k, p, pk, c: ref.at[c], dst_of=lambda ref, k, p, pk, c: ref.at[p, c],
                             arrive_of=lambda ref, k, p, pk, c: ref.at[pk, c])
    token1 = gather1.start()

    def first_weights(after):
        own, landed = gather1.wait(after)
        gathered = [own_slot(a, mine) for a, mine in zip(_forward_to_sibling(landed), own)]
        g = {n: a.reshape((4,) + shard2d[n].shape) for n, a in zip(FIRST, gathered)}
        return _w_in_from_shards(g["w_in"]), _wq_heads(g["w_q_b"]), _wkv_heads(g["w_kv_b"])

    later = [shard2d[n].astype(BF16) for n in LATER] + [conv_w_shard]
    gather2 = _SplitExchange("gather_later", later, [(4,) + a.shape for a in later],
                             src_of=lambda ref, k, p, pk, c: ref, dst_of=lambda ref, k, p, pk, c: ref.at[p],
                             arrive_of=lambda ref, k, p, pk, c: ref.at[pk])
    norm1_g_in = norm1_g + gather2.start(after=token1)[:1, :1]

    def late_weights(after):
        w_out_g, w_up_g, w_down_g, conv_w_g = [own_slot(a, mine) for mine, a in zip(*gather2.wait(after))]
        return w_out_g.reshape(D_MODEL, D_MODEL), cat_cols(w_up_g), w_down_g.reshape(D_FF, D_MODEL), cat_cols(conv_w_g)

    early = {}

    def early_grads(d_w_out, d_w_up, d_w_down):
        grads = [quartered(d_w_out.reshape(4, D_MODEL // 4, D_MODEL)), quartered(_col_shards(d_w_up)),
                 quartered(d_w_down.reshape(4, D_FF // 4, D_MODEL))]
        swap = _SplitExchange("rs_pair_exchange_early", grads, [(4,) + a.shape[2:] for a in grads], to_sibling=True,
                              src_of=lambda ref, k, p, pk, c: ref.at[:, pk], dst_of=lambda ref, k, p, pk, c: ref,
                              arrive_of=lambda ref, k, p, pk, c: ref)

        def on(after):
            kept, recv = swap.wait(after)
            early["pairs"] = [_rs_pair_add(f"rs_pair_add_{n}", core, gfull, r) for n, gfull, r in zip(LATER, kept, recv)]
            early["ici"] = _SplitExchange("rs_ici_early", early["pairs"], [(3,) + a.shape[1:] for a in early["pairs"]],
                                          src_of=lambda ref, k, p, pk, c: ref.at[pk], dst_of=lambda ref, k, p, pk, c: ref.at[k],
                                          arrive_of=lambda ref, k, p, pk, c: ref.at[k])
            return early["ici"].start()

        return swap.start(), on

    last = {}

    def last_grads(d_w_in_p, d_wq, d_wkv):
        grads = [quartered(_w_in_to_shards(d_w_in_p)), quartered(_wq_shards(d_wq)), quartered(_wkv_shards(d_wkv))]
        recv = _rs_pair_exchange("rs_pair_exchange_last", grads)
        last["pairs"] = [_rs_pair_add(f"rs_pair_add_{n}", core, gfull, r) for n, gfull, r in zip(FIRST, grads, recv)]
        last["ici"] = _SplitExchange("rs_ici_last", last["pairs"], [(3,) + a.shape[1:] for a in last["pairs"]],
                                     src_of=lambda ref, k, p, pk, c: ref.at[pk], dst_of=lambda ref, k, p, pk, c: ref.at[k],
                                     arrive_of=lambda ref, k, p, pk, c: ref.at[k])
        return last["ici"].start()

    loss, dx, gr = _local_step(x[0], positions, loss_target[0], norm1_g_in, first_weights, q_a_norm_g, kv_a_norm_g,
                               rel_bias, sinks, late_weights, norm2_g, conv_b, final_norm_g, early_grads, last_grads)

    last_pairs, last_recv = last["ici"].wait(dx)
    early_pairs, early_recv = early["ici"].wait(dx)
    pairs, recv2 = last_pairs + early_pairs, last_recv + early_recv
    halves = [_rs_final_add(f"rs_final_add_{n}", chip1, pr, r) for n, pr, r in zip(FIRST + LATER, pairs, recv2)]
    sibling_halves = _rs_pair_share(halves)

    as_rows = lambda a: a.reshape((-1, a.shape[-1]))
    summed = _small_allreduce([as_rows(gr[n]) for n in SMALL] + [gr["conv_w"], loss])
    small_g = dict(zip(SMALL, summed[:len(SMALL)]))
    conv_w_g = lax.dynamic_slice_in_dim(summed[len(SMALL)], chip * (2 * D_FF // 4), 2 * D_FF // 4, axis=1)
    loss_out = summed[-1].reshape(())

    out_g, out_d, out_m, out_v = {}, {}, {}, {}
    for n, mine, theirs in zip(FIRST + LATER, halves, sibling_halves):
        gsh, d, nm, nv = _adamw_halves(f"adamw_{n}", core, shard2d[n], mine, theirs, mom_m[n][0], mom_v[n][0])
        out_g[n], out_d[n], out_m[n], out_v[n] = gsh[None], d[None], nm[None], nv[None]
    names = SMALL + ("conv_w",)
    sg = [small_g[n] for n in SMALL] + [conv_w_g]
    ds, nms, nvs = _adamw_small([as_rows(weights[n]) for n in names], sg, [as_rows(mom_m[n]) for n in names],
                                [as_rows(mom_v[n]) for n in names])
    for n, gg, dd, mm, vv in zip(names, sg, ds, nms, nvs):
        shp = weights[n].shape
        out_g[n], out_d[n], out_m[n], out_v[n] = gg.reshape(shp), dd.reshape(shp), mm.reshape(shp), vv.reshape(shp)

    order = ("norm1_g", "w_in", "q_a_norm_g", "w_q_b", "kv_a_norm_g", "w_kv_b", "rel_bias", "sinks", "w_out",
             "norm2_g", "w_up", "conv_w", "conv_b", "w_down", "final_norm_g")
    return (loss_out, dx[None], *[out_g[n] for n in order], *[out_d[n] for n in order],
            *[out_m[n] for n in order], *[out_v[n] for n in order])
```

```python
import functools
import math

import jax
import jax.numpy as jnp
import numpy as np
from jax import lax
from jax.experimental import pallas as pl
from jax.experimental.pallas import tpu as pltpu

F32 = jnp.float32
BF16 = jnp.bfloat16
MESH = pl.DeviceIdType.MESH

D_MODEL = 1024
EPS = 1e-6
H_A = 8
QK_NOPE = 128
QK_ROPE = 64
V_DIM = 128
Q_LORA = 256
KV_LORA = 128
ROPE_THETA = 10000.0
H_B = 16
KV_B = 4
GROUP = 4
HD_B = 64
WINDOW = 128
Q_BLOCK = 128
NUM_BUCKETS = 32
MAX_DISTANCE = 128
D_FF = 2816
HEAD_PAD = 256

ADAM_LR = 0.001
ADAM_B1 = 0.9
ADAM_B2 = 0.999
ADAM_EPS = 1e-08
ADAM_WD = 0.01
ADAM_STEP = 10

LANES = 128
P_QB, P_GA, P_GB, P_QLAT, P_KB, P_VB, P_CKV, P_KR = 0, 1024, 2048, 3072, 3328, 3584, 3840, 3968
W_IN_PAD = 4096

NT = (((1,), (1,)), ((), ()))
NN = (((1,), (0,)), ((), ()))
TN = (((0,), (0,)), ((), ()))


def _arb(n):
    return pltpu.CompilerParams(dimension_semantics=("arbitrary",) * n)


def _matmul(name, a, b, *, out_shape, out_dtype, grid, a_spec, b_spec, o_spec, contract, add=None, bf16_copy=False,
            after=None):
    nk = grid[2]
    acc_shape = tuple(d for d in o_spec.block_shape if d is not None)
    n_in = 2 + (add is not None) + (after is not None)
    n_out = 2 if bf16_copy else 1

    def body(*refs):
        a_ref, b_ref = refs[:2]
        add_ref = refs[2] if add is not None else None
        o_refs = refs[n_in:n_in + n_out]
        scratch = refs[n_in + n_out:]
        prod = lax.dot_general(a_ref[...].astype(BF16), b_ref[...].astype(BF16), contract,
                               preferred_element_type=F32)

        def finish(val):
            if add_ref is not None:
                val = add_ref[...] + val
            o_refs[0][...] = val.astype(out_dtype)
            if bf16_copy:
                o_refs[1][...] = val.astype(BF16)

        if nk == 1:
            finish(prod)
        else:
            acc_ref = scratch[0]
            k = pl.program_id(2)

            @pl.when(k == 0)
            def _():
                acc_ref[...] = prod

            @pl.when((k > 0) & (k < nk - 1))
            def _():
                acc_ref[...] += prod

            @pl.when(k == nk - 1)
            def _():
                finish(acc_ref[...] + prod)

    in_specs = [a_spec, b_spec]
    args = [a, b]
    if add is not None:
        in_specs.append(o_spec)
        args.append(add)
    if after is not None:
        in_specs.append(pl.BlockSpec(memory_space=pl.ANY))
        args.append(after)
    out_shapes = [jax.ShapeDtypeStruct(out_shape, out_dtype)]
    if bf16_copy:
        out_shapes.append(jax.ShapeDtypeStruct(out_shape, BF16))
    res = pl.pallas_call(
        body, name=name, grid=grid, in_specs=in_specs, out_specs=[o_spec] * n_out, out_shape=out_shapes,
        scratch_shapes=[pltpu.VMEM(acc_shape, F32)] if nk > 1 else [],
        compiler_params=_arb(3),
    )(*args)
    return res if bf16_copy else res[0]


def _bs(block, fn):
    return pl.BlockSpec(block, fn)


def _rmsnorm_fwd(name, src, g, d, cb, ts=512):
    s = src.shape[0]

    def body(x_ref, g_ref, h_ref, r_ref):
        x = x_ref[...]
        r = lax.rsqrt(jnp.mean(x * x, axis=-1, keepdims=True) + EPS)
        h_ref[...] = (x * r * g_ref[...]).astype(BF16)
        r_ref[...] = r

    return pl.pallas_call(
        body, name=name, grid=(s // ts,),
        in_specs=[_bs((ts, d), lambda i: (i, cb)), _bs((1, d), lambda i: (0, 0))],
        out_specs=[_bs((ts, d), lambda i: (i, 0)), _bs((ts, 1), lambda i: (i, 0))],
        out_shape=[jax.ShapeDtypeStruct((s, d), BF16), jax.ShapeDtypeStruct((s, 1), F32)],
        compiler_params=_arb(1),
    )(src, g)


def _rmsnorm_bwd(name, dy, src, rstd, g, d, cb, out_dtype, res=None, bf16_copy=False, ts=512):
    s = src.shape[0]

    def body(*refs):
        dy_ref, x_ref, r_ref, g_ref = refs[:4]
        res_ref = refs[4] if res is not None else None
        dx_ref, dg_ref = refs[n_in:n_in + 2]
        dyv = dy_ref[...]
        r = r_ref[...]
        xhat = x_ref[...] * r
        dyh = dyv * g_ref[...]
        c = jnp.mean(dyh * xhat, axis=-1, keepdims=True)
        dx = r * (dyh - xhat * c)
        if res_ref is not None:
            dx = res_ref[...] + dx
        dx_ref[...] = dx.astype(out_dtype)
        if bf16_copy:
            refs[n_in + 2][...] = dx.astype(BF16)
        part = jnp.sum(dyv * xhat, axis=0, keepdims=True)

        @pl.when(pl.program_id(0) == 0)
        def _():
            dg_ref[...] = part

        @pl.when(pl.program_id(0) > 0)
        def _():
            dg_ref[...] += part

    in_specs = [_bs((ts, d), lambda i: (i, 0)), _bs((ts, d), lambda i: (i, cb)),
                _bs((ts, 1), lambda i: (i, 0)), _bs((1, d), lambda i: (0, 0))]
    args = [dy, src, rstd, g]
    if res is not None:
        in_specs.append(_bs((ts, d), lambda i: (i, 0)))
        args.append(res)
    n_in = len(args)
    out_specs = [_bs((ts, d), lambda i: (i, 0)), _bs((1, d), lambda i: (0, 0))]
    out_shape = [jax.ShapeDtypeStruct((s, d), out_dtype), jax.ShapeDtypeStruct((1, d), F32)]
    if bf16_copy:
        out_specs.append(_bs((ts, d), lambda i: (i, 0)))
        out_shape.append(jax.ShapeDtypeStruct((s, d), BF16))
    return pl.pallas_call(
        body, name=name, grid=(s // ts,), in_specs=in_specs, out_specs=out_specs, out_shape=out_shape,
        compiler_params=_arb(1),
    )(*args)


def _matmul_norm_bwd(name, a, b, *, grid, a_spec, b_spec, x, rstd, g, res, bf16_copy=False, after=None):
    s, d = x.shape
    ni, nk = grid
    assert nk >= 2, "the first and the last contraction step are distinct branches"
    tm = s // ni
    n_in = 6 + (after is not None)

    def body(*refs):
        a_ref, b_ref, x_ref, r_ref, g_ref, res_ref = refs[:6]
        dx_ref, dg_ref = refs[n_in:n_in + 2]
        acc_ref = refs[-1]
        k = pl.program_id(1)
        prod = lax.dot_general(a_ref[...], b_ref[...], NT, preferred_element_type=F32)

        @pl.when(k == 0)
        def _():
            acc_ref[...] = prod

        @pl.when((k > 0) & (k < nk - 1))
        def _():
            acc_ref[...] += prod

        @pl.when(k == nk - 1)
        def _():
            dyv = acc_ref[...] + prod
            r = r_ref[...]
            xhat = x_ref[...] * r
            dyh = dyv * g_ref[...]
            c = jnp.mean(dyh * xhat, axis=-1, keepdims=True)
            dx = res_ref[...] + r * (dyh - xhat * c)
            dx_ref[...] = dx
            if bf16_copy:
                refs[n_in + 2][...] = dx.astype(BF16)
            part = jnp.sum(dyv * xhat, axis=0, keepdims=True)

            @pl.when(pl.program_id(0) == 0)
            def _():
                dg_ref[...] = part

            @pl.when(pl.program_id(0) > 0)
            def _():
                dg_ref[...] += part

    rows = _bs((tm, d), lambda i, k: (i, 0))
    in_specs = [a_spec, b_spec, rows, _bs((tm, 1), lambda i, k: (i, 0)), _bs((1, d), lambda i, k: (0, 0)), rows]
    args = [a, b, x, rstd, g, res]
    if after is not None:
        in_specs.append(pl.BlockSpec(memory_space=pl.ANY))
        args.append(after)
    out_specs = [rows, _bs((1, d), lambda i, k: (0, 0))]
    out_shape = [jax.ShapeDtypeStruct((s, d), F32), jax.ShapeDtypeStruct((1, d), F32)]
    if bf16_copy:
        out_specs.append(rows)
        out_shape.append(jax.ShapeDtypeStruct((s, d), BF16))
    return pl.pallas_call(
        body, name=name, grid=grid, in_specs=in_specs, out_specs=out_specs, out_shape=out_shape,
        scratch_shapes=[pltpu.VMEM((tm, d), F32)], compiler_params=_arb(2),
    )(*args)


def _ffn_down_loss(f, w_down, x1, target, g, ts=512):
    s, d = x1.shape
    dff = f.shape[1]

    def body(f_ref, w_ref, x_ref, t_ref, g_ref, loss_ref, dx_ref, dg_ref, dxb_ref):
        x = x_ref[...] + jnp.dot(f_ref[...], w_ref[...], preferred_element_type=F32)
        r = lax.rsqrt(jnp.mean(x * x, axis=-1, keepdims=True) + EPS)
        xhat = x * r
        gv = g_ref[...]
        err = xhat * gv - t_ref[...]
        lpart = 0.5 * jnp.sum(jnp.mean(err * err, axis=-1, keepdims=True), axis=0, keepdims=True)
        dyv = err * (1.0 / d)
        dyh = dyv * gv
        c = jnp.mean(dyh * xhat, axis=-1, keepdims=True)
        dx = r * (dyh - xhat * c)
        dx_ref[...] = dx
        dxb_ref[...] = dx.astype(BF16)
        gpart = jnp.sum(dyv * xhat, axis=0, keepdims=True)

        @pl.when(pl.program_id(0) == 0)
        def _():
            dg_ref[...] = gpart
            loss_ref[...] = lpart

        @pl.when(pl.program_id(0) > 0)
        def _():
            dg_ref[...] += gpart
            loss_ref[...] += lpart

    rows = _bs((ts, d), lambda i: (i, 0))
    return pl.pallas_call(
        body, name="ffn_down_loss", grid=(s // ts,),
        in_specs=[_bs((ts, dff), lambda i: (i, 0)), _bs((dff, d), lambda i: (0, 0)), rows, rows,
                  _bs((1, d), lambda i: (0, 0))],
        out_specs=[_bs((1, 1), lambda i: (0, 0)), rows, _bs((1, d), lambda i: (0, 0)), rows],
        out_shape=[jax.ShapeDtypeStruct((1, 1), F32), jax.ShapeDtypeStruct((s, d), F32),
                   jax.ShapeDtypeStruct((1, d), F32), jax.ShapeDtypeStruct((s, d), BF16)],
        compiler_params=_arb(1),
    )(f, w_down, x1, target, g)


def _swap_halves(t):
    lane = lax.broadcasted_iota(jnp.int32, t.shape, 1)
    return jnp.where(lane < 32, pltpu.roll(t, 96, 1), pltpu.roll(t, 32, 1))


def _rope_fwd(t, cos_t, sin_t):
    return t * cos_t + _swap_halves(t) * sin_t


def _rope_bwd(dt, cos_t, sin_t):
    return dt * cos_t - _swap_halves(dt) * sin_t


def _lat_norms(proj, gq, gkv, ts=512):
    s = proj.shape[0]

    def body(q_ref, c_ref, gq_ref, gkv_ref, qn_ref, cn_ref, rq_ref, rc_ref):
        q = q_ref[...]
        rq = lax.rsqrt(jnp.mean(q * q, axis=-1, keepdims=True) + EPS)
        qn_ref[...] = (q * rq * gq_ref[...]).astype(BF16)
        rq_ref[...] = rq
        cv = c_ref[...]
        rc = lax.rsqrt(jnp.mean(cv * cv, axis=-1, keepdims=True) + EPS)
        cn_ref[...] = (cv * rc * gkv_ref[...]).astype(BF16)
        rc_ref[...] = rc

    return pl.pallas_call(
        body, name="lat_norms", grid=(s // ts,),
        in_specs=[_bs((ts, Q_LORA), lambda i: (i, P_QLAT // Q_LORA)),
                  _bs((ts, KV_LORA), lambda i: (i, P_CKV // KV_LORA)),
                  _bs((1, Q_LORA), lambda i: (0, 0)), _bs((1, KV_LORA), lambda i: (0, 0))],
        out_specs=[_bs((ts, Q_LORA), lambda i: (i, 0)), _bs((ts, KV_LORA), lambda i: (i, 0)),
                   _bs((ts, 1), lambda i: (i, 0)), _bs((ts, 1), lambda i: (i, 0))],
        out_shape=[jax.ShapeDtypeStruct((s, Q_LORA), BF16), jax.ShapeDtypeStruct((s, KV_LORA), BF16),
                   jax.ShapeDtypeStruct((s, 1), F32), jax.ShapeDtypeStruct((s, 1), F32)],
        compiler_params=_arb(1),
    )(proj, proj, gq, gkv)


HEAD_ROWS = 2048
DW_ROWS = 2048
MM_ROWS = 1024


def _q_heads(qn, wq, cos_t, sin_t):
    s = qn.shape[0]
    ts = min(s, HEAD_ROWS)

    def body(qn_ref, w_ref, cos_ref, sin_ref, q_ref):
        o = jnp.dot(qn_ref[...], w_ref[...], preferred_element_type=F32)
        q_ref[:, :LANES] = o[:, :LANES].astype(BF16)
        q_ref[:, LANES:] = _rope_fwd(o[:, LANES:], cos_ref[...], sin_ref[...]).astype(BF16)

    return pl.pallas_call(
        body, name="q_heads", grid=(s // ts, H_A),
        in_specs=[_bs((ts, Q_LORA), lambda i, h: (i, 0)), _bs((None, Q_LORA, HEAD_PAD), lambda i, h: (h, 0, 0)),
                  _bs((ts, LANES), lambda i, h: (i, 0)), _bs((ts, LANES), lambda i, h: (i, 0))],
        out_specs=_bs((None, ts, HEAD_PAD), lambda i, h: (h, i, 0)),
        out_shape=jax.ShapeDtypeStruct((H_A, s, HEAD_PAD), BF16),
        compiler_params=_arb(2),
    )(qn, wq, cos_t, sin_t)


def _kv_heads(cn, wkv, proj, cos_t, sin_t):
    s = cn.shape[0]
    ts = min(s, HEAD_ROWS)

    def body(cn_ref, w_ref, kr_ref, cos_ref, sin_ref, k_ref, v_ref):
        o = jnp.dot(cn_ref[...], w_ref[...], preferred_element_type=F32)
        k_ref[:, :LANES] = o[:, :LANES].astype(BF16)
        k_ref[:, LANES:] = _rope_fwd(kr_ref[...], cos_ref[...], sin_ref[...]).astype(BF16)
        v_ref[...] = o[:, LANES:].astype(BF16)

    return pl.pallas_call(
        body, name="kv_heads", grid=(s // ts, H_A),
        in_specs=[_bs((ts, KV_LORA), lambda i, h: (i, 0)),
                  _bs((None, KV_LORA, QK_NOPE + V_DIM), lambda i, h: (h, 0, 0)),
                  _bs((ts, LANES), lambda i, h: (i, P_KR // LANES)),
                  _bs((ts, LANES), lambda i, h: (i, 0)), _bs((ts, LANES), lambda i, h: (i, 0))],
        out_specs=[_bs((None, ts, HEAD_PAD), lambda i, h: (h, i, 0)), _bs((None, ts, V_DIM), lambda i, h: (h, i, 0))],
        out_shape=[jax.ShapeDtypeStruct((H_A, s, HEAD_PAD), BF16), jax.ShapeDtypeStruct((H_A, s, V_DIM), BF16)],
        compiler_params=_arb(2),
    )(cn, wkv, proj, cos_t, sin_t)


MLA_SCALE = 1.0 / math.sqrt(QK_NOPE + QK_ROPE)
LOG2E = math.log2(math.e)
MLA_EXP2_SCALE = MLA_SCALE * LOG2E


def _lane_tiles(a):
    return [a[:, j * LANES:(j + 1) * LANES] for j in range(a.shape[1] // LANES)]


MLA_SUB = 512


def _mla_fwd(q, k, v, tq=512, tk=1024):
    s = q.shape[1]
    tq = min(tq, s)
    nk = s // tk

    def body(q_ref, k_ref, v_ref, o_ref, lse_ref, m_ref, l_ref, acc_ref):
        m_ref[...] = jnp.full(m_ref.shape, -jnp.inf, F32)
        l_ref[...] = jnp.zeros(l_ref.shape, F32)
        acc_ref[...] = jnp.zeros(acc_ref.shape, F32)

        def step(c, carry):
            rows = pl.ds(pl.multiple_of(c * tk, tk), tk)
            for sub in range(tq // MLA_SUB):
                qr = slice(sub * MLA_SUB, (sub + 1) * MLA_SUB)
                raw = lax.dot_general(q_ref[qr, :], k_ref[rows, :], NT, preferred_element_type=F32)
                m_prev = m_ref[qr, :]
                m_new = jnp.maximum(m_prev, jnp.max(raw, axis=-1, keepdims=True))
                alpha = jnp.exp2((m_prev - m_new) * MLA_EXP2_SCALE)
                ps = [jnp.exp2((t - m_new) * MLA_EXP2_SCALE) for t in _lane_tiles(raw)]
                l_ref[qr, :] = alpha * l_ref[qr, :] + functools.reduce(lambda a, b: a + b, ps)
                p = jnp.concatenate(ps, axis=1).astype(BF16)
                acc_ref[qr, :] = alpha * acc_ref[qr, :] + jnp.dot(p, v_ref[rows, :], preferred_element_type=F32)
                m_ref[qr, :] = m_new
            return carry

        lax.fori_loop(0, nk, step, 0, unroll=True)
        l = jnp.sum(l_ref[...], axis=-1, keepdims=True)
        o_ref[...] = acc_ref[...] / l
        lse_ref[...] = m_ref[...] * MLA_SCALE + jnp.log(l)

    return pl.pallas_call(
        body, name="mla_fwd", grid=(H_A, s // tq),
        in_specs=[_bs((None, tq, HEAD_PAD), lambda h, i: (h, i, 0)),
                  _bs((None, s, HEAD_PAD), lambda h, i: (h, 0, 0)),
                  _bs((None, s, V_DIM), lambda h, i: (h, 0, 0))],
        out_specs=[_bs((tq, V_DIM), lambda h, i: (i, h)), _bs((None, tq, LANES), lambda h, i: (h, i, 0))],
        out_shape=[jax.ShapeDtypeStruct((s, H_A * V_DIM), F32), jax.ShapeDtypeStruct((H_A, s, LANES), F32)],
        scratch_shapes=[pltpu.VMEM((tq, LANES), F32), pltpu.VMEM((tq, LANES), F32), pltpu.VMEM((tq, V_DIM), F32)],
        compiler_params=_arb(2),
    )(q, k, v)


def _mla_bwd(q, k, v, do, o, lse, cos_t, sin_t, tq=512, tk=512):
    s = q.shape[1]
    nq = s // tq
    nkb = s // tk

    def body(q_ref, k_ref, v_ref, do_ref, o_ref, lse_ref, cos_ref, sin_ref, dqp_ref, dkvp_ref, dkr_ref,
             delta_ref, dq_ref, dk_ref, dv_ref):
        @pl.when(pl.program_id(1) == 0)
        def _():
            def init(c, carry):
                rows = pl.ds(pl.multiple_of(c * tq, tq), tq)
                delta = jnp.sum(do_ref[rows, :] * o_ref[rows, :], axis=-1, keepdims=True)
                delta_ref[rows, :] = jnp.broadcast_to(delta, (tq, LANES))
                dq_ref[rows, :] = jnp.zeros((tq, HEAD_PAD), F32)
                return carry

            lax.fori_loop(0, nq, init, 0)

        dk_ref[...] = jnp.zeros(dk_ref.shape, F32)
        dv_ref[...] = jnp.zeros(dv_ref.shape, F32)
        kb = k_ref[...]
        vb = v_ref[...]

        def step(c, carry):
            rows = pl.ds(pl.multiple_of(c * tq, tq), tq)
            qc = q_ref[rows, :]
            doc = do_ref[rows, :].astype(BF16)
            raw = lax.dot_general(qc, kb, NT, preferred_element_type=F32)
            dp = lax.dot_general(doc, vb, NT, preferred_element_type=F32)
            lse2 = lse_ref[rows, :] * LOG2E
            delta = delta_ref[rows, :]
            ps = [jnp.exp2(t * MLA_EXP2_SCALE - lse2) for t in _lane_tiles(raw)]
            dss = [pj * (dj - delta) * MLA_SCALE for pj, dj in zip(ps, _lane_tiles(dp))]
            p = jnp.concatenate(ps, axis=1).astype(BF16)
            ds = jnp.concatenate(dss, axis=1).astype(BF16)
            dv_ref[...] += lax.dot_general(p, doc, TN, preferred_element_type=F32)
            dk_ref[...] += lax.dot_general(ds, qc, TN, preferred_element_type=F32)
            dq_ref[rows, :] += jnp.dot(ds, kb, preferred_element_type=F32)
            return carry

        lax.fori_loop(0, nq, step, 0, unroll=True)
        dkvp_ref[:, :LANES] = dk_ref[:, :LANES].astype(BF16)
        dkvp_ref[:, LANES:] = dv_ref[...].astype(BF16)
        dkr_ref[...] = dk_ref[:, LANES:]

        @pl.when(pl.program_id(1) == nkb - 1)
        def _():
            def finish(c, carry):
                rows = pl.ds(pl.multiple_of(c * tq, tq), tq)
                dqp_ref[rows, :LANES] = dq_ref[rows, :LANES].astype(BF16)
                dqp_ref[rows, LANES:] = _rope_bwd(dq_ref[rows, LANES:], cos_ref[rows, :], sin_ref[rows, :]).astype(BF16)
                return carry

            lax.fori_loop(0, nq, finish, 0)

    whole = lambda w: _bs((s, w), lambda h, j: (0, 0))
    return pl.pallas_call(
        body, name="mla_bwd", grid=(H_A, nkb),
        in_specs=[_bs((None, s, HEAD_PAD), lambda h, j: (h, 0, 0)),
                  _bs((None, tk, HEAD_PAD), lambda h, j: (h, j, 0)),
                  _bs((None, tk, V_DIM), lambda h, j: (h, j, 0)),
                  _bs((s, V_DIM), lambda h, j: (0, h)), _bs((s, V_DIM), lambda h, j: (0, h)),
                  _bs((None, s, LANES), lambda h, j: (h, 0, 0)), whole(LANES), whole(LANES)],
        out_specs=[_bs((None, s, HEAD_PAD), lambda h, j: (h, 0, 0)),
                   _bs((None, tk, HEAD_PAD), lambda h, j: (h, j, 0)),
                   _bs((None, tk, LANES), lambda h, j: (h, j, 0))],
        out_shape=[jax.ShapeDtypeStruct((H_A, s, HEAD_PAD), BF16), jax.ShapeDtypeStruct((H_A, s, HEAD_PAD), BF16),
                   jax.ShapeDtypeStruct((H_A, s, LANES), F32)],
        scratch_shapes=[pltpu.VMEM((s, LANES), F32), pltpu.VMEM((s, HEAD_PAD), F32), pltpu.VMEM((tk, HEAD_PAD), F32),
                        pltpu.VMEM((tk, V_DIM), F32)],
        compiler_params=_arb(2),
    )(q, k, v, do, o, lse, cos_t, sin_t)


def _mla_key_rope_grad(dkr, cos_t, sin_t, ts=512):
    s = dkr.shape[1]

    def body(d_ref, cos_ref, sin_ref, o_ref):
        tot = d_ref[0]
        for h in range(1, H_A):
            tot = tot + d_ref[h]
        o_ref[...] = _rope_bwd(tot, cos_ref[...], sin_ref[...]).astype(BF16)

    rows = _bs((ts, LANES), lambda i: (i, 0))
    return pl.pallas_call(
        body, name="mla_key_rope_grad", grid=(s // ts,),
        in_specs=[_bs((H_A, ts, LANES), lambda i: (0, i, 0)), rows, rows], out_specs=rows,
        out_shape=jax.ShapeDtypeStruct((s, LANES), BF16), compiler_params=_arb(1),
    )(dkr, cos_t, sin_t)


WIN_SCALE = 1.0 / math.sqrt(HD_B)
SPAN = Q_BLOCK + 2 * WINDOW


def _t5_bucket_table():
    a = jnp.arange(Q_BLOCK, dtype=jnp.int32)[:, None]
    c = jnp.arange(SPAN, dtype=jnp.int32)[None, :]
    rel = c - WINDOW - a
    nb = NUM_BUCKETS // 2
    max_exact = nb // 2
    base = (rel > 0).astype(jnp.int32) * nb
    n = jnp.abs(rel)
    nf = jnp.maximum(n, 1).astype(F32)
    large = max_exact + (jnp.log(nf / max_exact) / math.log(MAX_DISTANCE / max_exact)
                         * (nb - max_exact)).astype(jnp.int32)
    large = jnp.minimum(large, nb - 1)
    return base + jnp.where(n < max_exact, n, large)


def _win_bias(bucket, rel_bias):
    def body(rb_ref, bk_ref, o_ref):
        h = pl.program_id(0)
        bk = bk_ref[...]
        acc = jnp.zeros((Q_BLOCK, SPAN), F32)
        for b in range(NUM_BUCKETS):
            acc = jnp.where(bk == b, rb_ref[b, h], acc)
        o_ref[...] = acc

    return pl.pallas_call(
        body, name="win_bias", grid=(H_B,),
        in_specs=[pl.BlockSpec(memory_space=pltpu.SMEM), _bs((Q_BLOCK, SPAN), lambda h: (0, 0))],
        out_specs=_bs((None, Q_BLOCK, SPAN), lambda h: (h, 0, 0)),
        out_shape=jax.ShapeDtypeStruct((H_B, Q_BLOCK, SPAN), F32),
        compiler_params=_arb(1),
    )(rel_bias, bucket)


WIN_HEADS = GROUP


def _win_kv_rows(n, j, nblk):
    blk = jnp.clip(n + j - 1, 0, nblk - 1)
    return pl.ds(pl.multiple_of(blk * Q_BLOCK, Q_BLOCK), Q_BLOCK)


def _win_kv_cols(h0):
    kv = h0 // GROUP
    return slice(kv * HD_B, (kv + 1) * HD_B)


def _win_stack(ref, h0):
    return jnp.concatenate([ref[:, (h0 + g) * HD_B:(h0 + g + 1) * HD_B] for g in range(WIN_HEADS)], axis=0)


def _win_unstack(ref, h0, val):
    for g in range(WIN_HEADS):
        ref[:, (h0 + g) * HD_B:(h0 + g + 1) * HD_B] = val[g * Q_BLOCK:(g + 1) * Q_BLOCK].astype(ref.dtype)


def _win_scores(q, k_ref, h0, bias_ref, n, nblk):
    a = lax.broadcasted_iota(jnp.int32, (WIN_HEADS, Q_BLOCK, Q_BLOCK), 1)
    cc = lax.broadcasted_iota(jnp.int32, (WIN_HEADS, Q_BLOCK, Q_BLOCK), 2)
    valid = [(cc >= a) & (n > 0), None, (cc <= a) & (n < nblk - 1)]
    out = []
    for j in range(3):
        sc = lax.dot_general(q, k_ref[_win_kv_rows(n, j, nblk), _win_kv_cols(h0)], NT, preferred_element_type=F32)
        sc = (sc.reshape(WIN_HEADS, Q_BLOCK, Q_BLOCK) * WIN_SCALE
              + bias_ref[h0:h0 + WIN_HEADS, :, j * Q_BLOCK:(j + 1) * Q_BLOCK])
        if valid[j] is not None:
            sc = jnp.where(valid[j], sc, -1e30)
        out.append(sc)
    return out


def _win_sink(sink_ref, h0):
    hs = lax.broadcasted_iota(jnp.int32, (WIN_HEADS, Q_BLOCK, 1), 0)
    sk = jnp.zeros((WIN_HEADS, Q_BLOCK, 1), F32)
    for g in range(WIN_HEADS):
        sk = jnp.where(hs == g, sink_ref[h0 + g], sk)
    return sk


def _win_fwd(proj_b, bias, sinks):
    s = proj_b.shape[0]
    nblk = s // Q_BLOCK
    rows = WIN_HEADS * Q_BLOCK

    def body(sink_ref, q_ref, k_ref, v_ref, bias_ref, o_ref, lse_ref):
        n = pl.program_id(0)
        for h0 in range(0, H_B, WIN_HEADS):
            sk = _win_sink(sink_ref, h0)
            q = _win_stack(q_ref, h0)
            ss = _win_scores(q, k_ref, h0, bias_ref, n, nblk)
            m = jnp.maximum(jnp.max(jnp.maximum(jnp.maximum(ss[0], ss[1]), ss[2]), axis=2, keepdims=True), sk)
            es = [jnp.exp(sc - m) for sc in ss]
            l = jnp.sum(es[0] + es[1] + es[2], axis=2, keepdims=True) + jnp.exp(sk - m)
            acc = jnp.zeros((rows, HD_B), F32)
            for j, e in enumerate(es):
                p = (e / l).astype(BF16).reshape(rows, Q_BLOCK)
                acc = acc + jnp.dot(p, v_ref[_win_kv_rows(n, j, nblk), _win_kv_cols(h0)],
                                    preferred_element_type=F32)
            _win_unstack(o_ref, h0, acc)
            lse_ref[h0:h0 + WIN_HEADS] = m + jnp.log(l)

    kv_w = KV_B * HD_B
    return pl.pallas_call(
        body, name="win_fwd", grid=(nblk,),
        in_specs=[pl.BlockSpec(memory_space=pltpu.SMEM), _bs((Q_BLOCK, H_B * HD_B), lambda n: (n, P_QB // (H_B * HD_B))),
                  _bs((s, kv_w), lambda n: (0, P_KB // kv_w)), _bs((s, kv_w), lambda n: (0, P_VB // kv_w)),
                  _bs((H_B, Q_BLOCK, SPAN), lambda n: (0, 0, 0))],
        out_specs=[_bs((Q_BLOCK, H_B * HD_B), lambda n: (n, 0)), _bs((H_B, Q_BLOCK, 1), lambda n: (0, n, 0))],
        out_shape=[jax.ShapeDtypeStruct((s, H_B * HD_B), F32), jax.ShapeDtypeStruct((H_B, s, 1), F32)],
        compiler_params=_arb(1),
    )(sinks, proj_b, proj_b, proj_b, bias)


def _win_bwd(proj_b, bias, sinks, do_b, lse):
    s = proj_b.shape[0]
    nblk = s // Q_BLOCK
    rows = WIN_HEADS * Q_BLOCK
    spad = s + 2 * WINDOW

    def body(sink_ref, q_ref, k_ref, v_ref, bias_ref, do_ref, lse_ref, dq_ref, dk_ref, dv_ref, db_ref, dsk_ref):
        n = pl.program_id(0)

        @pl.when(n == 0)
        def _():
            dk_ref[...] = jnp.zeros(dk_ref.shape, F32)
            dv_ref[...] = jnp.zeros(dv_ref.shape, F32)
            db_ref[...] = jnp.zeros(db_ref.shape, F32)
            dsk_ref[...] = jnp.zeros(dsk_ref.shape, F32)

        for h0 in range(0, H_B, WIN_HEADS):
            heads = slice(h0, h0 + WIN_HEADS)
            sk = _win_sink(sink_ref, h0)
            q = _win_stack(q_ref, h0)
            dob = _win_stack(do_ref, h0)
            lse_v = lse_ref[heads]
            ss = _win_scores(q, k_ref, h0, bias_ref, n, nblk)
            ps = [jnp.exp(sc - lse_v) for sc in ss]
            dps = [lax.dot_general(dob, v_ref[_win_kv_rows(n, j, nblk), _win_kv_cols(h0)], NT,
                                   preferred_element_type=F32).reshape(WIN_HEADS, Q_BLOCK, Q_BLOCK) for j in range(3)]
            delta = jnp.sum(ps[0] * dps[0] + ps[1] * dps[1] + ps[2] * dps[2], axis=2, keepdims=True)
            dq = jnp.zeros((rows, HD_B), F32)
            for j in range(3):
                ds = ps[j] * (dps[j] - delta)
                db_ref[heads, :, j * Q_BLOCK:(j + 1) * Q_BLOCK] += ds
                dsb = (ds * WIN_SCALE).astype(BF16).reshape(rows, Q_BLOCK)
                dq = dq + jnp.dot(dsb, k_ref[_win_kv_rows(n, j, nblk), _win_kv_cols(h0)],
                                  preferred_element_type=F32)
                krows = pl.ds(pl.multiple_of((n + j) * Q_BLOCK, Q_BLOCK), Q_BLOCK)
                dk_ref[krows, _win_kv_cols(h0)] += lax.dot_general(dsb, q, TN, preferred_element_type=F32)
                dv_ref[krows, _win_kv_cols(h0)] += lax.dot_general(
                    ps[j].astype(BF16).reshape(rows, Q_BLOCK), dob, TN, preferred_element_type=F32)
            dsk_ref[heads] += -(jnp.exp(sk - lse_v) * delta)
            _win_unstack(dq_ref, h0, dq)

    kv_w = KV_B * HD_B
    qspec = _bs((Q_BLOCK, H_B * HD_B), lambda n: (n, 0))
    kacc = _bs((spad, kv_w), lambda n: (0, 0))
    return pl.pallas_call(
        body, name="win_bwd", grid=(nblk,),
        in_specs=[pl.BlockSpec(memory_space=pltpu.SMEM), _bs((Q_BLOCK, H_B * HD_B), lambda n: (n, P_QB // (H_B * HD_B))),
                  _bs((s, kv_w), lambda n: (0, P_KB // kv_w)), _bs((s, kv_w), lambda n: (0, P_VB // kv_w)),
                  _bs((H_B, Q_BLOCK, SPAN), lambda n: (0, 0, 0)), qspec, _bs((H_B, Q_BLOCK, 1), lambda n: (0, n, 0))],
        out_specs=[qspec, kacc, kacc, _bs((H_B, Q_BLOCK, SPAN), lambda n: (0, 0, 0)),
                   _bs((H_B, Q_BLOCK, 1), lambda n: (0, 0, 0))],
        out_shape=[jax.ShapeDtypeStruct((s, H_B * HD_B), BF16), jax.ShapeDtypeStruct((spad, kv_w), F32),
                   jax.ShapeDtypeStruct((spad, kv_w), F32), jax.ShapeDtypeStruct((H_B, Q_BLOCK, SPAN), F32),
                   jax.ShapeDtypeStruct((H_B, Q_BLOCK, 1), F32)],
        compiler_params=_arb(1),
    )(sinks, proj_b, proj_b, proj_b, bias, do_b, lse)


def _win_param_grads(bucket, dbias, dsink_rows):
    def body(bk_ref, db_ref, ds_ref, o_ref):
        bk = bk_ref[...]
        dbv = db_ref[...]
        lane = lax.broadcasted_iota(jnp.int32, (1, LANES), 1)
        res = jnp.zeros((1, LANES), F32)
        for b in range(NUM_BUCKETS):
            tot = jnp.sum(jnp.sum(jnp.where(bk == b, dbv, 0.0), axis=1, keepdims=True), axis=0, keepdims=True)
            res = jnp.where(lane == b, tot, res)
        stot = jnp.sum(ds_ref[...], axis=0, keepdims=True)
        o_ref[...] = jnp.where(lane == NUM_BUCKETS, stot, res)

    return pl.pallas_call(
        body, name="win_param_grads", grid=(H_B,),
        in_specs=[_bs((Q_BLOCK, SPAN), lambda h: (0, 0)), _bs((None, Q_BLOCK, SPAN), lambda h: (h, 0, 0)),
                  _bs((None, Q_BLOCK, 1), lambda h: (h, 0, 0))],
        out_specs=_bs((None, 1, LANES), lambda h: (h, 0, 0)),
        out_shape=jax.ShapeDtypeStruct((H_B, 1, LANES), F32),
        compiler_params=_arb(1),
    )(bucket, dbias, dsink_rows)


def _mix_out_norm(proj, o_a, o_b, w_out, x, g2, ts=512):
    s = o_a.shape[0]
    wide = lambda cb: _bs((ts, D_MODEL), lambda i: (i, cb))

    def body(ga_ref, gb_ref, oa_ref, ob_ref, w_ref, x_ref, g_ref, m_ref, x1_ref, h_ref, r_ref):
        mixed = (jax.nn.sigmoid(ga_ref[...]) * oa_ref[...] + jax.nn.sigmoid(gb_ref[...]) * ob_ref[...]).astype(BF16)
        m_ref[...] = mixed
        x1 = x_ref[...] + jnp.dot(mixed, w_ref[...], preferred_element_type=F32)
        x1_ref[...] = x1
        r = lax.rsqrt(jnp.mean(x1 * x1, axis=-1, keepdims=True) + EPS)
        h_ref[...] = (x1 * r * g_ref[...]).astype(BF16)
        r_ref[...] = r

    return pl.pallas_call(
        body, name="mix_out_norm", grid=(s // ts,),
        in_specs=[wide(P_GA // D_MODEL), wide(P_GB // D_MODEL), wide(0), wide(0),
                  _bs((D_MODEL, D_MODEL), lambda i: (0, 0)), wide(0), _bs((1, D_MODEL), lambda i: (0, 0))],
        out_specs=[wide(0), wide(0), wide(0), _bs((ts, 1), lambda i: (i, 0))],
        out_shape=[jax.ShapeDtypeStruct((s, D_MODEL), BF16), jax.ShapeDtypeStruct((s, D_MODEL), F32),
                   jax.ShapeDtypeStruct((s, D_MODEL), BF16), jax.ShapeDtypeStruct((s, 1), F32)],
        compiler_params=_arb(1),
    )(proj, proj, o_a, o_b, w_out, x, g2)


def _mix_out_bwd(dx1_b, w_out, proj, o_a, o_b, after=None, ts=512):
    s = o_a.shape[0]
    wide = lambda cb: _bs((ts, D_MODEL), lambda i: (i, cb))

    def body(dx_ref, w_ref, ga_ref, gb_ref, oa_ref, ob_ref, *rest):
        doa_ref, dob_ref, dga_ref, dgb_ref = rest[-4:]
        dm = lax.dot_general(dx_ref[...], w_ref[...], NT, preferred_element_type=F32)
        sa = jax.nn.sigmoid(ga_ref[...])
        sb = jax.nn.sigmoid(gb_ref[...])
        doa_ref[...] = dm * sa
        dob_ref[...] = (dm * sb).astype(BF16)
        dga_ref[...] = (dm * oa_ref[...] * (sa * (1.0 - sa))).astype(BF16)
        dgb_ref[...] = (dm * ob_ref[...] * (sb * (1.0 - sb))).astype(BF16)

    extra = [after] if after is not None else []
    return pl.pallas_call(
        body, name="mix_out_bwd", grid=(s // ts,),
        in_specs=[wide(0), _bs((D_MODEL, D_MODEL), lambda i: (0, 0)), wide(P_GA // D_MODEL), wide(P_GB // D_MODEL),
                  wide(0), wide(0)] + [pl.BlockSpec(memory_space=pl.ANY)] * len(extra),
        out_specs=[wide(0)] * 4,
        out_shape=[jax.ShapeDtypeStruct((s, D_MODEL), F32), jax.ShapeDtypeStruct((s, D_MODEL), BF16),
                   jax.ShapeDtypeStruct((s, D_MODEL), BF16), jax.ShapeDtypeStruct((s, D_MODEL), BF16)],
        compiler_params=_arb(1),
    )(dx1_b, w_out, proj, proj, o_a, o_b, *extra)


CONV_CHUNK = 128
N_SLAB = D_FF // LANES


def _shifted(ref, c, nchunks):
    r0 = c * CONV_CHUNK
    cur = ref[r0:r0 + CONV_CHUNK, :]
    row = lax.broadcasted_iota(jnp.int32, (8, LANES), 0)
    if c > 0:
        prev = ref[r0 - 1:r0 - 1 + CONV_CHUNK, :]
    else:
        down = pltpu.roll(cur, 1, 0)
        prev = jnp.concatenate([jnp.where(row == 0, 0.0, down[:8]), down[8:]], axis=0)
    if c < nchunks - 1:
        nxt = ref[r0 + 1:r0 + 1 + CONV_CHUNK, :]
    else:
        up = pltpu.roll(cur, CONV_CHUNK - 1, 0)
        nxt = jnp.concatenate([up[:-8], jnp.where(row == 7, 0.0, up[-8:])], axis=0)
    return prev, cur, nxt


def _conv_taps(ref, w_ref, b_ref, c, nchunks):
    prev, cur, nxt = _shifted(ref, c, nchunks)
    conv = prev * w_ref[0:1, :] + cur * w_ref[1:2, :] + nxt * w_ref[2:3, :] + b_ref[...]
    return conv, prev, cur, nxt


def _convffn_fwd(u, conv_w, conv_b):
    s = u.shape[0]
    nchunks = s // CONV_CHUNK

    def body(ug_ref, uv_ref, wg_ref, wv_ref, bg_ref, bv_ref, f_ref):
        for c in range(nchunks):
            cg = _conv_taps(ug_ref, wg_ref, bg_ref, c, nchunks)[0]
            cv = _conv_taps(uv_ref, wv_ref, bv_ref, c, nchunks)[0]
            f_ref[c * CONV_CHUNK:(c + 1) * CONV_CHUNK, :] = (cg * jax.nn.sigmoid(cg) * cv).astype(BF16)

    slab = lambda off: _bs((s, LANES), lambda j: (0, off + j))
    wsl = lambda off: _bs((3, LANES), lambda j: (0, off + j))
    bsl = lambda off: _bs((1, LANES), lambda j: (0, off + j))
    return pl.pallas_call(
        body, name="convffn_fwd", grid=(N_SLAB,),
        in_specs=[slab(0), slab(N_SLAB), wsl(0), wsl(N_SLAB), bsl(0), bsl(N_SLAB)],
        out_specs=slab(0), out_shape=jax.ShapeDtypeStruct((s, D_FF), BF16),
        compiler_params=_arb(1),
    )(u, u, conv_w, conv_w, conv_b, conv_b)


def _convffn_bwd(u, conv_w, conv_b, df):
    s = u.shape[0]
    nchunks = s // CONV_CHUNK

    def body(ug_ref, uv_ref, wg_ref, wv_ref, bg_ref, bv_ref, df_ref, du_ref, dw_ref, db_ref, dcg_ref, dcv_ref):
        dwg = [jnp.zeros((1, LANES), F32) for _ in range(3)]
        dwv = [jnp.zeros((1, LANES), F32) for _ in range(3)]
        dbg = jnp.zeros((1, LANES), F32)
        dbv = jnp.zeros((1, LANES), F32)
        for c in range(nchunks):
            rows = slice(c * CONV_CHUNK, (c + 1) * CONV_CHUNK)
            cg, gp, gc, gn = _conv_taps(ug_ref, wg_ref, bg_ref, c, nchunks)
            cv, vp, vc, vn = _conv_taps(uv_ref, wv_ref, bv_ref, c, nchunks)
            dfv = df_ref[rows, :]
            sg = jax.nn.sigmoid(cg)
            dcg = dfv * cv * (sg * (1.0 + cg * (1.0 - sg)))
            dcv = dfv * (cg * sg)
            dcg_ref[rows, :] = dcg
            dcv_ref[rows, :] = dcv
            for t, (tg, tv) in enumerate(((gp, vp), (gc, vc), (gn, vn))):
                dwg[t] = dwg[t] + jnp.sum(tg * dcg, axis=0, keepdims=True)
                dwv[t] = dwv[t] + jnp.sum(tv * dcv, axis=0, keepdims=True)
            dbg = dbg + jnp.sum(dcg, axis=0, keepdims=True)
            dbv = dbv + jnp.sum(dcv, axis=0, keepdims=True)
        for t in range(3):
            dw_ref[0, t:t + 1, :] = dwg[t]
            dw_ref[1, t:t + 1, :] = dwv[t]
        db_ref[0] = dbg
        db_ref[1] = dbv
        for half, (dc_ref, w_ref) in enumerate(((dcg_ref, wg_ref), (dcv_ref, wv_ref))):
            for c in range(nchunks):
                prev, cur, nxt = _shifted(dc_ref, c, nchunks)
                du = nxt * w_ref[0:1, :] + cur * w_ref[1:2, :] + prev * w_ref[2:3, :]
                du_ref[half, c * CONV_CHUNK:(c + 1) * CONV_CHUNK, :] = du.astype(BF16)

    slab = lambda off: _bs((s, LANES), lambda j: (0, off + j))
    wsl = lambda off: _bs((3, LANES), lambda j: (0, off + j))
    bsl = lambda off: _bs((1, LANES), lambda j: (0, off + j))
    return pl.pallas_call(
        body, name="convffn_bwd", grid=(N_SLAB,),
        in_specs=[slab(0), slab(N_SLAB), wsl(0), wsl(N_SLAB), bsl(0), bsl(N_SLAB), slab(0)],
        out_specs=[_bs((2, s, LANES), lambda j: (0, 0, j)), _bs((2, 3, LANES), lambda j: (0, 0, j)),
                   _bs((2, 1, LANES), lambda j: (0, 0, j))],
        out_shape=[jax.ShapeDtypeStruct((2, s, D_FF), BF16), jax.ShapeDtypeStruct((2, 3, D_FF), F32),
                   jax.ShapeDtypeStruct((2, 1, D_FF), F32)],
        scratch_shapes=[pltpu.VMEM((s, LANES), F32), pltpu.VMEM((s, LANES), F32)],
        compiler_params=_arb(1),
    )(u, u, conv_w, conv_w, conv_b, conv_b, df)


def _row_tile(rows, limit=512):
    best = rows
    for t in range(8, min(rows, limit) + 1, 8):
        if rows % t == 0:
            best = t
    return best if rows % 8 == 0 else rows


ADAM_C1 = 1.0 - ADAM_B1 ** ADAM_STEP
ADAM_C2 = 1.0 - ADAM_B2 ** ADAM_STEP


def _adamw_math(w, gv, m, v):
    nm = ADAM_B1 * m + (1.0 - ADAM_B1) * gv
    nv = ADAM_B2 * v + (1.0 - ADAM_B2) * (gv * gv)
    m_hat = nm / ADAM_C1
    v_hat = nv / ADAM_C2
    return -ADAM_LR * (m_hat / (jnp.sqrt(v_hat) + ADAM_EPS) + ADAM_WD * w), nm, nv


def _adamw_halves(name, core, w, mine, theirs, m, v, transposed=False):
    half, cols = (mine.shape[1], mine.shape[0]) if transposed else mine.shape
    tr = _row_tile(cols if transposed else half)
    nr = (cols if transposed else half) // tr

    def body(core_ref, w_ref, mine_ref, theirs_ref, m_ref, v_ref, g_ref, d_ref, nm_ref, nv_ref):
        gv = jnp.where(pl.program_id(0) == core_ref[0], mine_ref[...], theirs_ref[...])
        g_ref[...] = gv
        d_ref[...], nm_ref[...], nv_ref[...] = _adamw_math(w_ref[...], gv, m_ref[...], v_ref[...])

    if transposed:
        full = pl.BlockSpec((tr, half), lambda hf, r, cr: (r, hf))
        part = pl.BlockSpec((tr, half), lambda hf, r, cr: (r, 0))
        shape = (cols, 2 * half)
    else:
        full = pl.BlockSpec((tr, cols), lambda hf, r, cr: (hf * nr + r, 0))
        part = pl.BlockSpec((tr, cols), lambda hf, r, cr: (r, 0))
        shape = (2 * half, cols)
    return pl.pallas_call(
        body, name=name,
        grid_spec=pltpu.PrefetchScalarGridSpec(num_scalar_prefetch=1, grid=(2, nr),
                                               in_specs=[full, part, part, full, full], out_specs=[full] * 4),
        out_shape=[jax.ShapeDtypeStruct(shape, F32)] * 4, compiler_params=_arb(2),
    )(core, w, mine, theirs, m, v)


ANY = pl.BlockSpec(memory_space=pl.ANY)


def _mesh_pos():
    return lax.axis_index("x"), lax.axis_index("y"), lax.axis_index("c")


def _other_chips(x, y):
    return [(1 - x, y), (x, 1 - y), (1 - x, 1 - y)]


def _forward_to_sibling(gathered):
    n = len(gathered)

    def body(*refs):
        in_refs, out_refs = refs[:n], refs[n:2 * n]
        send_sems, recv_sems = refs[2 * n:]
        x, y, c = _mesh_pos()
        cps = []
        for i in range(n):
            for k, chip in enumerate(_other_chips(x, y)):
                pk = 2 * chip[0] + chip[1]
                sems = dict(send_sem=send_sems.at[3 * i + k], recv_sem=recv_sems.at[3 * i + k],
                            device_id=(x, y, 1 - c), device_id_type=MESH)
                sent = pltpu.make_async_remote_copy(src_ref=in_refs[i].at[pk, c], dst_ref=out_refs[i].at[pk, c], **sems)
                sent.start()
                theirs = out_refs[i].at[pk, 1 - c]
                cps.append((sent, pltpu.make_async_remote_copy(src_ref=theirs, dst_ref=theirs, **sems)))
        for sent, arrived in cps:
            sent.wait_send()
            arrived.wait_recv()

    return pl.pallas_call(
        body, name="forward_to_sibling", in_specs=[ANY] * n, out_specs=[ANY] * n,
        out_shape=[jax.ShapeDtypeStruct(a.shape, a.dtype) for a in gathered],
        input_output_aliases={i: i for i in range(n)},
        scratch_shapes=[pltpu.SemaphoreType.DMA((3 * n,)), pltpu.SemaphoreType.DMA((3 * n,))],
    )(*gathered)


def _rs_pair_exchange(name, grads):
    n = len(grads)

    def body(*refs):
        g_refs, o_refs = refs[:n], refs[n:2 * n]
        send_sems, recv_sems = refs[2 * n:]
        x, y, c = _mesh_pos()
        cps = []
        for i in range(n):
            cp = pltpu.make_async_remote_copy(
                src_ref=g_refs[i].at[:, 1 - c], dst_ref=o_refs[i],
                send_sem=send_sems.at[i], recv_sem=recv_sems.at[i], device_id=(x, y, 1 - c), device_id_type=MESH)
            cp.start()
            cps.append(cp)
        for cp in cps:
            cp.wait()

    return pl.pallas_call(
        body, name=name, in_specs=[ANY] * n, out_specs=[ANY] * n,
        out_shape=[jax.ShapeDtypeStruct((4,) + g.shape[2:], F32) for g in grads],
        scratch_shapes=[pltpu.SemaphoreType.DMA((n,)), pltpu.SemaphoreType.DMA((n,))],
    )(*grads)


def _rs_pair_add(name, core, g, recv):
    _, half, cols = recv.shape
    tr = _row_tile(half)
    nr = half // tr

    def body(core_ref, g_ref, r_ref, o_ref):
        o_ref[...] = (g_ref[...] + r_ref[...]).astype(BF16)

    return pl.pallas_call(
        body, name=name,
        grid_spec=pltpu.PrefetchScalarGridSpec(
            num_scalar_prefetch=1, grid=(4, nr),
            in_specs=[pl.BlockSpec((None, None, tr, cols), lambda q, r, cr: (q, cr[0], r, 0)),
                      pl.BlockSpec((None, tr, cols), lambda q, r, cr: (q, r, 0))],
            out_specs=pl.BlockSpec((None, tr, cols), lambda q, r, cr: (q, r, 0))),
        out_shape=jax.ShapeDtypeStruct((4, half, cols), BF16),
        compiler_params=_arb(2),
    )(core, g, recv)


def _rs_final_add(name, chip, pair, recv):
    _, half, cols = pair.shape
    tr = _row_tile(half)

    def body(chip_ref, p_ref, r_ref, o_ref):
        o_ref[...] = ((p_ref[...].astype(F32) + r_ref[0].astype(F32)) + r_ref[1].astype(F32)) + r_ref[2].astype(F32)

    return pl.pallas_call(
        body, name=name,
        grid_spec=pltpu.PrefetchScalarGridSpec(
            num_scalar_prefetch=1, grid=(half // tr,),
            in_specs=[pl.BlockSpec((None, tr, cols), lambda r, ch: (ch[0], r, 0)),
                      pl.BlockSpec((3, tr, cols), lambda r, ch: (0, r, 0))],
            out_specs=pl.BlockSpec((tr, cols), lambda r, ch: (r, 0))),
        out_shape=jax.ShapeDtypeStruct((half, cols), F32),
        compiler_params=_arb(1),
    )(chip, pair, recv)


def _rs_pair_share(halves):
    n = len(halves)

    def body(*refs):
        h_refs, o_refs = refs[:n], refs[n:2 * n]
        send_sems, recv_sems = refs[2 * n:]
        x, y, c = _mesh_pos()
        cps = []
        for i in range(n):
            cp = pltpu.make_async_remote_copy(src_ref=h_refs[i], dst_ref=o_refs[i], send_sem=send_sems.at[i],
                                              recv_sem=recv_sems.at[i], device_id=(x, y, 1 - c), device_id_type=MESH)
            cp.start()
            cps.append(cp)
        for cp in cps:
            cp.wait()

    return pl.pallas_call(
        body, name="rs_pair_share", in_specs=[ANY] * n, out_specs=[ANY] * n,
        out_shape=[jax.ShapeDtypeStruct(h.shape, F32) for h in halves],
        scratch_shapes=[pltpu.SemaphoreType.DMA((n,)), pltpu.SemaphoreType.DMA((n,))],
    )(*halves)


HBM = pl.BlockSpec(memory_space=pltpu.HBM)
SEM = pl.BlockSpec(memory_space=pltpu.SEMAPHORE)


class _SplitExchange:
    def __init__(self, name, srcs, land_shapes, src_of, dst_of, arrive_of, to_sibling=False):
        self.name, self.srcs, self.land_shapes = name, list(srcs), list(land_shapes)
        self.src_of, self.dst_of, self.arrive_of = src_of, dst_of, arrive_of
        self.to_sibling = to_sibling
        self.fan = 1 if to_sibling else 3

    def _copies(self, src_refs, land_refs, send_sems, recv_sems):
        x, y, c = _mesh_pos()
        p = 2 * x + y
        if self.to_sibling:
            peers = [((x, y, 1 - c), 1 - c)]
        else:
            peers = [((*chip, c), 2 * chip[0] + chip[1]) for chip in _other_chips(x, y)]
        out = []
        for i, (src, land) in enumerate(zip(src_refs, land_refs)):
            for k, (peer, pk) in enumerate(peers):
                sems = dict(send_sem=send_sems.at[self.fan * i + k], recv_sem=recv_sems.at[self.fan * i + k],
                            device_id=peer, device_id_type=MESH)
                sent = pltpu.make_async_remote_copy(src_ref=self.src_of(src, k, p, pk, c),
                                                    dst_ref=self.dst_of(land, k, p, pk, c), **sems)
                here = self.arrive_of(land, k, p, pk, c)
                out.append((sent, pltpu.make_async_remote_copy(src_ref=here, dst_ref=here, **sems)))
        return out

    def start(self, after=None):
        n = len(self.srcs)
        n_in = 2 * n + (after is not None)

        def body(*refs):
            for sent, _ in self._copies(refs[:n], refs[n:2 * n], refs[n_in], refs[n_in + 1]):
                sent.start()
            refs[-1][...] = jnp.zeros((8, LANES), F32)

        lands = [lax.empty(shape, src.dtype) for shape, src in zip(self.land_shapes, self.srcs)]
        operands = [pltpu.with_memory_space_constraint(a, pltpu.HBM) for a in self.srcs + lands]
        outs = pl.pallas_call(
            body, name=self.name + "_start",
            out_shape=(pltpu.SemaphoreType.DMA((self.fan * n,)), pltpu.SemaphoreType.DMA((self.fan * n,)),
                       *[pltpu.HBM(a.shape, a.dtype) for a in operands], jax.ShapeDtypeStruct((8, LANES), F32)),
            in_specs=[HBM] * (2 * n) + [ANY] * (after is not None),
            out_specs=(SEM, SEM, *[HBM] * (2 * n), pl.BlockSpec(memory_space=pltpu.VMEM)),
            input_output_aliases={j: 2 + j for j in range(2 * n)},
            compiler_params=pltpu.CompilerParams(has_side_effects=pltpu.SideEffectType.DATAFLOW_SIDE_EFFECTING),
        )(*operands, *([after] if after is not None else []))
        self._sems, self._thru = outs[:2], list(outs[2:2 + 2 * n])
        return outs[-1]

    def wait(self, after):
        n = len(self.srcs)

        def body(*refs):
            for sent, arrived in self._copies(refs[:n], refs[n:2 * n], refs[2 * n], refs[2 * n + 1]):
                sent.wait_send()
                arrived.wait_recv()

        after = list(after) if isinstance(after, (list, tuple)) else [after]
        outs = pl.pallas_call(
            body, name=self.name + "_wait",
            out_shape=tuple(pltpu.HBM(a.shape, a.dtype) for a in self._thru),
            in_specs=[HBM] * (2 * n) + [SEM, SEM] + [ANY] * len(after), out_specs=tuple([HBM] * (2 * n)),
            input_output_aliases={j: j for j in range(2 * n)},
            compiler_params=pltpu.CompilerParams(has_side_effects=pltpu.SideEffectType.DATAFLOW_SIDE_EFFECTING),
        )(*self._thru, *self._sems, *after)
        return list(outs[:n]), list(outs[n:])


def _small_allreduce(parts):
    n = len(parts)

    def body(*refs):
        in_refs, out_refs, gather_refs = refs[:n], refs[n:2 * n], refs[2 * n:3 * n]
        send_sems, recv_sems = refs[3 * n:]
        x, y, c = _mesh_pos()
        me = 4 * x + 2 * y + c
        cps = []
        for i in range(n):
            gather_refs[i][me] = in_refs[i][...]
            for j in range(1, 8):
                peer = (x ^ (j >> 2), y ^ ((j >> 1) & 1), c ^ (j & 1))
                cp = pltpu.make_async_remote_copy(
                    src_ref=in_refs[i], dst_ref=gather_refs[i].at[me], send_sem=send_sems.at[7 * i + j - 1],
                    recv_sem=recv_sems.at[7 * i + j - 1], device_id=peer, device_id_type=MESH)
                cp.start()
                cps.append(cp)
        for i in range(n):
            for j in range(1, 8):
                peer_id = 4 * (x ^ (j >> 2)) + 2 * (y ^ ((j >> 1) & 1)) + (c ^ (j & 1))
                slot = gather_refs[i].at[peer_id]
                pltpu.make_async_remote_copy(src_ref=slot, dst_ref=slot, send_sem=send_sems.at[7 * i + j - 1],
                                             recv_sem=recv_sems.at[7 * i + j - 1], device_id=(x, y, c),
                                             device_id_type=MESH).wait_recv()
        for cp in cps:
            cp.wait_send()
        for i in range(n):
            tot = gather_refs[i][0]
            for d in range(1, 8):
                tot = tot + gather_refs[i][d]
            out_refs[i][...] = tot

    vmem = pl.BlockSpec(memory_space=pltpu.VMEM)
    return pl.pallas_call(
        body, name="small_allreduce", in_specs=[vmem] * n, out_specs=[vmem] * n,
        out_shape=[jax.ShapeDtypeStruct(p.shape, F32) for p in parts],
        scratch_shapes=[pltpu.VMEM((8,) + p.shape, F32) for p in parts]
        + [pltpu.SemaphoreType.DMA((7 * n,)), pltpu.SemaphoreType.DMA((7 * n,))],
    )(*parts)


def _adamw_small(ws, gs, ms, vs):
    n = len(ws)

    def body(*refs):
        for i in range(n):
            w_ref, g_ref, m_ref, v_ref = refs[i], refs[n + i], refs[2 * n + i], refs[3 * n + i]
            d_ref, nm_ref, nv_ref = refs[4 * n + i], refs[5 * n + i], refs[6 * n + i]
            d_ref[...], nm_ref[...], nv_ref[...] = _adamw_math(w_ref[...], g_ref[...], m_ref[...], v_ref[...])

    vmem = pl.BlockSpec(memory_space=pltpu.VMEM)
    shapes = [jax.ShapeDtypeStruct(w.shape, F32) for w in ws]
    outs = pl.pallas_call(body, name="adamw_small", in_specs=[vmem] * (4 * n), out_specs=[vmem] * (3 * n),
                          out_shape=shapes * 3)(*ws, *gs, *ms, *vs)
    return outs[:n], outs[n:2 * n], outs[2 * n:]


W_IN_PIECES = ((0, 256, P_QLAT), (256, 384, P_CKV), (384, 448, P_KR), (448, 1472, P_QB), (1472, 1728, P_KB),
               (1728, 1984, P_VB), (1984, 3008, P_GA), (3008, 4032, P_GB))
W_IN_SHARD = 1008


def _w_in_from_shards(shards):
    cols = []
    for lo, hi, _ in sorted(W_IN_PIECES, key=lambda piece: piece[2]):
        for q in range(4):
            a, b = max(lo, q * W_IN_SHARD), min(hi, (q + 1) * W_IN_SHARD)
            if a < b:
                cols.append(shards[q][:, a - q * W_IN_SHARD:b - q * W_IN_SHARD])
    cols.append(jnp.zeros((shards.shape[1], W_IN_PAD - 4 * W_IN_SHARD), shards.dtype))
    return jnp.concatenate(cols, axis=1)


def _w_in_to_shards(p):
    shards = []
    for q in range(4):
        cols = []
        for lo, hi, at in W_IN_PIECES:
            a, b = max(lo, q * W_IN_SHARD), min(hi, (q + 1) * W_IN_SHARD)
            if a < b:
                cols.append(p[:, at + a - lo:at + b - lo])
        shards.append(jnp.concatenate(cols, axis=1))
    return jnp.stack(shards)


def _col_shards(w):
    r, c4 = w.shape
    return w.reshape(r, 4, c4 // 4).transpose(1, 0, 2)


def _local_step(x, positions, target, norm1_g, first_weights, q_a_norm_g, kv_a_norm_g, rel_bias, sinks,
                late_weights, norm2_g, conv_b, final_norm_g, early_grads=None, last_grads=None):
    s = x.shape[0]
    half = QK_ROPE // 2
    inv_freq = jnp.asarray(np.float32(ROPE_THETA) ** (-np.arange(half, dtype=np.float32) / np.float32(half)))
    ang = positions.astype(F32)[:, None] * inv_freq[None, :]
    cos, sin = jnp.cos(ang), jnp.sin(ang)
    z64 = jnp.zeros((s, 64), F32)
    cos_t = jnp.concatenate([cos, cos, z64], axis=1)
    sin_t = jnp.concatenate([-sin, sin, z64], axis=1)
    bucket = _t5_bucket_table()
    sinks1 = sinks.reshape(H_B)

    h1, rstd1 = _rmsnorm_fwd("norm1_fwd", x, norm1_g, D_MODEL, 0)
    bias = _win_bias(bucket, rel_bias)
    w_in_p, wq, wkv = first_weights([h1, bias, cos_t, sin_t])
    proj, proj_b = _matmul("proj", h1, w_in_p, out_shape=(s, W_IN_PAD), out_dtype=F32, grid=(s // MM_ROWS, W_IN_PAD // 1024, 1),
                           a_spec=_bs((MM_ROWS, D_MODEL), lambda i, j, k: (i, 0)), b_spec=_bs((D_MODEL, 1024), lambda i, j, k: (0, j)),
                           o_spec=_bs((MM_ROWS, 1024), lambda i, j, k: (i, j)), contract=NN, bf16_copy=True)
    qn, cn, rstd_q, rstd_c = _lat_norms(proj, q_a_norm_g, kv_a_norm_g)
    q = _q_heads(qn, wq, cos_t, sin_t)
    k, v = _kv_heads(cn, wkv, proj, cos_t, sin_t)
    o_a, lse_a = _mla_fwd(q, k, v)

    o_b, lse_b = _win_fwd(proj_b, bias, sinks1)

    w_out, w_up, w_down, conv_w = late_weights([o_a, o_b])
    row512 = lambda w: _bs((MM_ROWS, w), lambda i, j, k: (i, 0))
    whole = lambda r, c: _bs((r, c), lambda i, j, k: (0, 0))
    mixed, x1, h2, rstd2 = _mix_out_norm(proj, o_a, o_b, w_out, x, norm2_g)
    u = _matmul("ffn_up", h2, w_up, out_shape=(s, 2 * D_FF), out_dtype=F32, grid=(s // MM_ROWS, 4, 1),
                a_spec=_bs((MM_ROWS, D_MODEL), lambda i, j, k: (i, 0)), b_spec=_bs((D_MODEL, D_FF // 2), lambda i, j, k: (0, j)),
                o_spec=_bs((MM_ROWS, D_FF // 2), lambda i, j, k: (i, j)), contract=NN)
    f = _convffn_fwd(u, conv_w, conv_b)
    loss, dx2, d_final_g, dx2_b = _ffn_down_loss(f, w_down, x1, target, final_norm_g.reshape(1, D_MODEL))
    tk = min(s, DW_ROWS)

    df = _matmul("ffn_down_dx", dx2_b, w_down, out_shape=(s, D_FF), out_dtype=F32, grid=(s // MM_ROWS, 2, 1),
                 a_spec=row512(D_MODEL), b_spec=_bs((D_FF // 2, D_MODEL), lambda i, j, k: (j, 0)),
                 o_spec=_bs((MM_ROWS, D_FF // 2), lambda i, j, k: (i, j)), contract=NT)
    d_w_down = _matmul("ffn_down_dw", f, dx2_b, out_shape=(D_FF, D_MODEL), out_dtype=F32, grid=(2, 1, s // tk),
                       a_spec=_bs((tk, D_FF // 2), lambda i, j, k: (k, i)), b_spec=_bs((tk, D_MODEL), lambda i, j, k: (k, 0)),
                       o_spec=_bs((D_FF // 2, D_MODEL), lambda i, j, k: (i, 0)), contract=TN)
    du, d_conv_w2, d_conv_b2 = _convffn_bwd(u, conv_w, conv_b, df)
    kc = D_FF // 2
    dx1, d_norm2_g, dx1_b = _matmul_norm_bwd(
        "ffn_up_dx_norm2_bwd", du, w_up, grid=(s // MM_ROWS, 4),
        a_spec=_bs((None, MM_ROWS, kc), lambda i, k: (k // 2, i, k % 2)), b_spec=_bs((D_MODEL, kc), lambda i, k: (0, k)),
        x=x1, rstd=rstd2, g=norm2_g, res=dx2, bf16_copy=True)
    d_w_up = _matmul("ffn_up_dw", h2, du, out_shape=(D_MODEL, 2 * D_FF), out_dtype=F32, grid=(1, 4, s // tk),
                     a_spec=_bs((tk, D_MODEL), lambda i, j, k: (k, 0)),
                     b_spec=_bs((None, tk, kc), lambda i, j, k: (j // 2, k, j % 2)),
                     o_spec=_bs((D_MODEL, kc), lambda i, j, k: (0, j)), contract=TN)

    d_w_out = _matmul("attn_out_dw", mixed, dx1_b, out_shape=(D_MODEL, D_MODEL), out_dtype=F32, grid=(1, 1, s // tk),
                      a_spec=_bs((tk, D_MODEL), lambda i, j, k: (k, 0)), b_spec=_bs((tk, D_MODEL), lambda i, j, k: (k, 0)),
                      o_spec=whole(D_MODEL, D_MODEL), contract=TN)
    token, early_grads_on = early_grads(d_w_out, d_w_up, d_w_down) if early_grads is not None else (None, None)
    do_a, do_b, d_ga, d_gb = _mix_out_bwd(dx1_b, w_out, proj, o_a, o_b, after=token)
    if early_grads_on is not None:
        sinks1 = sinks1 + early_grads_on(d_ga)[0, :H_B]

    d_qb, dk_pad, dv_pad, dbias, dsink_rows = _win_bwd(proj_b, bias, sinks1, do_b, lse_b)
    wp = _win_param_grads(bucket, dbias, dsink_rows)[:, 0, :]
    d_rel_bias = wp[:, :NUM_BUCKETS].T
    d_sinks = wp[:, NUM_BUCKETS].reshape(1, H_B)
    d_kb = dk_pad[WINDOW:WINDOW + s].astype(BF16)
    d_vb = dv_pad[WINDOW:WINDOW + s].astype(BF16)

    dq_pre, dkv_pre, dkr = _mla_bwd(q, k, v, do_a, o_a, lse_a, cos_t, sin_t)
    d_kr = _mla_key_rope_grad(dkr, cos_t, sin_t)
    th = min(s, HEAD_ROWS)
    hgrid = (s // th, 1, H_A)
    hblock = _bs((None, th, HEAD_PAD), lambda i, j, k: (k, i, 0))
    hrows = lambda w: _bs((th, w), lambda i, j, k: (i, 0))
    dqn = _matmul("q_up_dx", dq_pre, wq, out_shape=(s, Q_LORA), out_dtype=F32, grid=hgrid, a_spec=hblock,
                  b_spec=_bs((None, Q_LORA, HEAD_PAD), lambda i, j, k: (k, 0, 0)), o_spec=hrows(Q_LORA), contract=NT)
    dcn = _matmul("kv_up_dx", dkv_pre, wkv, out_shape=(s, KV_LORA), out_dtype=F32, grid=hgrid, a_spec=hblock,
                  b_spec=_bs((None, KV_LORA, HEAD_PAD), lambda i, j, k: (k, 0, 0)), o_spec=hrows(KV_LORA), contract=NT)
    wgrid = (H_A, 1, s // th)
    d_wq = _matmul("q_up_dw", qn, dq_pre, out_shape=(H_A, Q_LORA, HEAD_PAD), out_dtype=F32, grid=wgrid,
                   a_spec=_bs((th, Q_LORA), lambda i, j, k: (k, 0)), b_spec=_bs((None, th, HEAD_PAD), lambda i, j, k: (i, k, 0)),
                   o_spec=_bs((None, Q_LORA, HEAD_PAD), lambda i, j, k: (i, 0, 0)), contract=TN)
    d_wkv = _matmul("kv_up_dw", cn, dkv_pre, out_shape=(H_A, KV_LORA, HEAD_PAD), out_dtype=F32, grid=wgrid,
                    a_spec=_bs((th, KV_LORA), lambda i, j, k: (k, 0)), b_spec=_bs((None, th, HEAD_PAD), lambda i, j, k: (i, k, 0)),
                    o_spec=_bs((None, KV_LORA, HEAD_PAD), lambda i, j, k: (i, 0, 0)), contract=TN)
    d_qlat, d_gq = _rmsnorm_bwd("q_norm_bwd", dqn, proj, rstd_q, q_a_norm_g, Q_LORA, P_QLAT // Q_LORA, BF16)
    d_ckv, d_gkv = _rmsnorm_bwd("kv_norm_bwd", dcn, proj, rstd_c, kv_a_norm_g, KV_LORA, P_CKV // KV_LORA, BF16)

    dproj = jnp.concatenate([d_qb, d_ga, d_gb, d_qlat, d_kb, d_vb, d_ckv, d_kr], axis=1)
    d_w_in_p = _matmul("proj_dw", h1, dproj, out_shape=(D_MODEL, W_IN_PAD), out_dtype=F32, grid=(1, W_IN_PAD // 1024, s // tk),
                       a_spec=_bs((tk, D_MODEL), lambda i, j, k: (k, 0)), b_spec=_bs((tk, 1024), lambda i, j, k: (k, j)),
                       o_spec=_bs((D_MODEL, 1024), lambda i, j, k: (0, j)), contract=TN)
    token = last_grads(d_w_in_p, d_wq, d_wkv) if last_grads is not None else None
    dx, d_norm1_g = _matmul_norm_bwd(
        "proj_dx_norm1_bwd", dproj, w_in_p, grid=(s // MM_ROWS, W_IN_PAD // 1024),
        a_spec=_bs((MM_ROWS, 1024), lambda i, k: (i, k)), b_spec=_bs((D_MODEL, 1024), lambda i, k: (0, k)),
        x=x, rstd=rstd1, g=norm1_g, res=dx1, after=token)

    grads = dict(
        norm1_g=d_norm1_g, w_in_p=d_w_in_p, q_a_norm_g=d_gq, wq=d_wq, kv_a_norm_g=d_gkv, wkv=d_wkv,
        rel_bias=d_rel_bias, sinks=d_sinks, w_out=d_w_out, norm2_g=d_norm2_g, w_up=d_w_up,
        conv_w=jnp.concatenate([d_conv_w2[0], d_conv_w2[1]], axis=1),
        conv_b=jnp.concatenate([d_conv_b2[0], d_conv_b2[1]], axis=1),
        w_down=d_w_down, final_norm_g=d_final_g.reshape(D_MODEL))
    return loss, dx, grads


HEADS_PER_SHARD = H_A // 4


def _head_cols(h, width):
    return slice((h % HEADS_PER_SHARD) * width, (h % HEADS_PER_SHARD + 1) * width)


def _wq_heads(shards):
    per = QK_NOPE + QK_ROPE
    w = jnp.stack([shards[h // HEADS_PER_SHARD][:, _head_cols(h, per)] for h in range(H_A)])
    return jnp.pad(w, ((0, 0), (0, 0), (0, HEAD_PAD - per)))


def _wq_shards(d_wq):
    per = QK_NOPE + QK_ROPE
    return jnp.stack([jnp.concatenate([d_wq[h][:, :per] for h in range(q * HEADS_PER_SHARD, (q + 1) * HEADS_PER_SHARD)],
                                      axis=1) for q in range(4)])


def _wkv_heads(shards):
    return jnp.stack([shards[h // HEADS_PER_SHARD][:, _head_cols(h, QK_NOPE + V_DIM)] for h in range(H_A)])


def _wkv_shards(d_wkv):
    return jnp.stack([jnp.concatenate([d_wkv[h] for h in range(q * HEADS_PER_SHARD, (q + 1) * HEADS_PER_SHARD)], axis=1)
                      for q in range(4)])


SMALL = ("norm1_g", "q_a_norm_g", "kv_a_norm_g", "rel_bias", "sinks", "norm2_g", "conv_b", "final_norm_g")
FIRST = ("w_in", "w_q_b", "w_kv_b")
LATER = ("w_out", "w_up", "w_down")
BIG = FIRST + LATER


def kernel(x, positions, norm1_g, w_in, q_a_norm_g, w_q_b, kv_a_norm_g, w_kv_b, rel_bias, sinks, w_out, norm2_g, w_up, conv_w, conv_b, w_down, final_norm_g, loss_target, m_norm1_g, m_w_in, m_q_a_norm_g, m_w_q_b, m_kv_a_norm_g, m_w_kv_b, m_rel_bias, m_sinks, m_w_out, m_norm2_g, m_w_up, m_conv_w, m_conv_b, m_w_down, m_final_norm_g, v_norm1_g, v_w_in, v_q_a_norm_g, v_w_q_b, v_kv_a_norm_g, v_w_kv_b, v_rel_bias, v_sinks, v_w_out, v_norm2_g, v_w_up, v_conv_w, v_conv_b, v_w_down, v_final_norm_g):
    weights = dict(norm1_g=norm1_g, w_in=w_in, q_a_norm_g=q_a_norm_g, w_q_b=w_q_b, kv_a_norm_g=kv_a_norm_g,
                   w_kv_b=w_kv_b, rel_bias=rel_bias, sinks=sinks, w_out=w_out, norm2_g=norm2_g, w_up=w_up,
                   conv_w=conv_w, conv_b=conv_b, w_down=w_down, final_norm_g=final_norm_g)
    mom_m = dict(norm1_g=m_norm1_g, w_in=m_w_in, q_a_norm_g=m_q_a_norm_g, w_q_b=m_w_q_b, kv_a_norm_g=m_kv_a_norm_g,
                 w_kv_b=m_w_kv_b, rel_bias=m_rel_bias, sinks=m_sinks, w_out=m_w_out, norm2_g=m_norm2_g, w_up=m_w_up,
                 conv_w=m_conv_w, conv_b=m_conv_b, w_down=m_w_down, final_norm_g=m_final_norm_g)
    mom_v = dict(norm1_g=v_norm1_g, w_in=v_w_in, q_a_norm_g=v_q_a_norm_g, w_q_b=v_w_q_b, kv_a_norm_g=v_kv_a_norm_g,
                 w_kv_b=v_w_kv_b, rel_bias=v_rel_bias, sinks=v_sinks, w_out=v_w_out, norm2_g=v_norm2_g, w_up=v_w_up,
                 conv_w=v_conv_w, conv_b=v_conv_b, w_down=v_w_down, final_norm_g=v_final_norm_g)
    shard2d = {n: weights[n][0] for n in BIG}
    conv_w_shard = conv_w[0]
    xi, yi, ci = lax.axis_index("x"), lax.axis_index("y"), lax.axis_index("c")
    chip = (2 * xi + yi).astype(jnp.int32)

    core = ci.astype(jnp.int32).reshape(1)
    chip1 = chip.reshape(1)
    cat_cols = lambda a: jnp.concatenate([a[0], a[1], a[2], a[3]], axis=1)
    own_slot = lambda a, own: lax.dynamic_update_index_in_dim(a, own, chip, 0)
    halved = lambda a: a.reshape((2, a.shape[0] // 2) + a.shape[1:])
    quartered = lambda a: a.reshape(4, 2, a.shape[1] // 2, a.shape[2])

    first = [halved(shard2d[n].astype(BF16)) for n in FIRST]
    gather1 = _SplitExchange("gather_first", first, [(4,) + a.shape for a in first],
                             src_of=lambda ref, k, p, pk, c: ref.at[c], dst_of=lambda ref, k, p, pk, c: ref.at[p, c],
                             arrive_of=lambda ref, k, p, pk, c: ref.at[pk, c])
    token1 = gather1.start()

    def first_weights(after):
        own, landed = gather1.wait(after)
        gathered = [own_slot(a, mine) for a, mine in zip(_forward_to_sibling(landed), own)]
        g = {n: a.reshape((4,) + shard2d[n].shape) for n, a in zip(FIRST, gathered)}
        return _w_in_from_shards(g["w_in"]), _wq_heads(g["w_q_b"]), _wkv_heads(g["w_kv_b"])

    later = [shard2d[n].astype(BF16) for n in LATER] + [conv_w_shard]
    gather2 = _SplitExchange("gather_later", later, [(4,) + a.shape for a in later],
                             src_of=lambda ref, k, p, pk, c: ref, dst_of=lambda ref, k, p, pk, c: ref.at[p],
                             arrive_of=lambda ref, k, p, pk, c: ref.at[pk])
    norm1_g_in = norm1_g + gather2.start(after=token1)[:1, :1]

    def late_weights(after):
        w_out_g, w_up_g, w_down_g, conv_w_g = [own_slot(a, mine) for mine, a in zip(*gather2.wait(after))]
        return w_out_g.reshape(D_MODEL, D_MODEL), cat_cols(w_up_g), w_down_g.reshape(D_FF, D_MODEL), cat_cols(conv_w_g)

    early = {}

    def early_grads(d_w_out, d_w_up, d_w_down):
        grads = [quartered(d_w_out.reshape(4, D_MODEL // 4, D_MODEL)), quartered(_col_shards(d_w_up)),
                 quartered(d_w_down.reshape(4, D_FF // 4, D_MODEL))]
        swap = _SplitExchange("rs_pair_exchange_early", grads, [(4,) + a.shape[2:] for a in grads], to_sibling=True,
                              src_of=lambda ref, k, p, pk, c: ref.at[:, pk], dst_of=lambda ref, k, p, pk, c: ref,
                              arrive_of=lambda ref, k, p, pk, c: ref)

        def on(after):
            kept, recv = swap.wait(after)
            early["pairs"] = [_rs_pair_add(f"rs_pair_add_{n}", core, gfull, r) for n, gfull, r in zip(LATER, kept, recv)]
            early["ici"] = _SplitExchange("rs_ici_early", early["pairs"], [(3,) + a.shape[1:] for a in early["pairs"]],
                                          src_of=lambda ref, k, p, pk, c: ref.at[pk], dst_of=lambda ref, k, p, pk, c: ref.at[k],
                                          arrive_of=lambda ref, k, p, pk, c: ref.at[k])
            return early["ici"].start()

        return swap.start(), on

    last = {}

    def last_grads(d_w_in_p, d_wq, d_wkv):
        grads = [quartered(_w_in_to_shards(d_w_in_p)), quartered(_wq_shards(d_wq)), quartered(_wkv_shards(d_wkv))]
        recv = _rs_pair_exchange("rs_pair_exchange_last", grads)
        last["pairs"] = [_rs_pair_add(f"rs_pair_add_{n}", core, gfull, r) for n, gfull, r in zip(FIRST, grads, recv)]
        last["ici"] = _SplitExchange("rs_ici_last", last["pairs"], [(3,) + a.shape[1:] for a in last["pairs"]],
                                     src_of=lambda ref, k, p, pk, c: ref.at[pk], dst_of=lambda ref, k, p, pk, c: ref.at[k],
                                     arrive_of=lambda ref, k, p, pk, c: ref.at[k])
        return last["ici"].start()

    loss, dx, gr = _local_step(x[0], positions, loss_target[0], norm1_g_in, first_weights, q_a_norm_g, kv_a_norm_g,
                               rel_bias, sinks, late_weights, norm2_g, conv_b, final_norm_g, early_grads, last_grads)

    last_pairs, last_recv = last["ici"].wait(dx)
    early_pairs, early_recv = early["ici"].wait(dx)
    pairs, recv2 = last_pairs + early_pairs, last_recv + early_recv
    halves = [_rs_final_add(f"rs_final_add_{n}", chip1, pr, r) for n, pr, r in zip(FIRST + LATER, pairs, recv2)]
    sibling_halves = _rs_pair_share(halves)

    as_rows = lambda a: a.reshape((-1, a.shape[-1]))
    summed = _small_allreduce([as_rows(gr[n]) for n in SMALL] + [gr["conv_w"], loss])
    small_g = dict(zip(SMALL, summed[:len(SMALL)]))
    conv_w_g = lax.dynamic_slice_in_dim(summed[len(SMALL)], chip * (2 * D_FF // 4), 2 * D_FF // 4, axis=1)
    loss_out = summed[-1].reshape(())

    out_g, out_d, out_m, out_v = {}, {}, {}, {}
    for n, mine, theirs in zip(FIRST + LATER, halves, sibling_halves):
        if shard2d[n].shape[1] % LANES:
            res = _adamw_halves(f"adamw_{n}", core, shard2d[n].T, mine.T, theirs.T, mom_m[n][0].T, mom_v[n][0].T,
                                transposed=True)
            gsh, d, nm, nv = [a.T for a in res]
        else:
            gsh, d, nm, nv = _adamw_halves(f"adamw_{n}", core, shard2d[n], mine, theirs, mom_m[n][0], mom_v[n][0])
        out_g[n], out_d[n], out_m[n], out_v[n] = gsh[None], d[None], nm[None], nv[None]
    names = SMALL + ("conv_w",)
    sg = [small_g[n] for n in SMALL] + [conv_w_g]
    ds, nms, nvs = _adamw_small([as_rows(weights[n]) for n in names], sg, [as_rows(mom_m[n]) for n in names],
                                [as_rows(mom_v[n]) for n in names])
    for n, gg, dd, mm, vv in zip(names, sg, ds, nms, nvs):
        shp = weights[n].shape
        out_g[n], out_d[n], out_m[n], out_v[n] = gg.reshape(shp), dd.reshape(shp), mm.reshape(shp), vv.reshape(shp)

    order = ("norm1_g", "w_in", "q_a_norm_g", "w_q_b", "kv_a_norm_g", "w_kv_b", "rel_bias", "sinks", "w_out",
             "norm2_g", "w_up", "conv_w", "conv_b", "w_down", "final_norm_g")
    return (loss_out, dx[None], *[out_g[n] for n in order], *[out_d[n] for n in order],
            *[out_m[n] for n in order], *[out_v[n] for n in order])
```

```python
import functools
import math

import jax
import jax.numpy as jnp
import numpy as np
from jax import lax
from jax.experimental import pallas as pl
from jax.experimental.pallas import tpu as pltpu

F32 = jnp.float32
BF16 = jnp.bfloat16
MESH = pl.DeviceIdType.MESH

D_MODEL = 1024
EPS = 1e-6
H_A = 8
QK_NOPE = 128
QK_ROPE = 64
V_DIM = 128
Q_LORA = 256
KV_LORA = 128
ROPE_THETA = 10000.0
H_B = 16
KV_B = 4
GROUP = 4
HD_B = 64
WINDOW = 128
Q_BLOCK = 128
NUM_BUCKETS = 32
MAX_DISTANCE = 128
D_FF = 2816
HEAD_PAD = 256

ADAM_LR = 0.001
ADAM_B1 = 0.9
ADAM_B2 = 0.999
ADAM_EPS = 1e-08
ADAM_WD = 0.01
ADAM_STEP = 10

LANES = 128
P_QB, P_GA, P_GB, P_QLAT, P_KB, P_VB, P_CKV, P_KR = 0, 1024, 2048, 3072, 3328, 3584, 3840, 3968
W_IN_PAD = 4096

NT = (((1,), (1,)), ((), ()))
NN = (((1,), (0,)), ((), ()))
TN = (((0,), (0,)), ((), ()))


def _arb(n):
    return pltpu.CompilerParams(dimension_semantics=("arbitrary",) * n)


def _matmul(name, a, b, *, out_shape, out_dtype, grid, a_spec, b_spec, o_spec, contract, add=None, bf16_copy=False,
            after=None):
    nk = grid[2]
    acc_shape = tuple(d for d in o_spec.block_shape if d is not None)
    n_in = 2 + (add is not None) + (after is not None)
    n_out = 2 if bf16_copy else 1

    def body(*refs):
        a_ref, b_ref = refs[:2]
        add_ref = refs[2] if add is not None else None
        o_refs = refs[n_in:n_in + n_out]
        scratch = refs[n_in + n_out:]
        prod = lax.dot_general(a_ref[...].astype(BF16), b_ref[...].astype(BF16), contract,
                               preferred_element_type=F32)

        def finish(val):
            if add_ref is not None:
                val = add_ref[...] + val
            o_refs[0][...] = val.astype(out_dtype)
            if bf16_copy:
                o_refs[1][...] = val.astype(BF16)

        if nk == 1:
            finish(prod)
        else:
            acc_ref = scratch[0]
            k = pl.program_id(2)

            @pl.when(k == 0)
            def _():
                acc_ref[...] = prod

            @pl.when((k > 0) & (k < nk - 1))
            def _():
                acc_ref[...] += prod

            @pl.when(k == nk - 1)
            def _():
                finish(acc_ref[...] + prod)

    in_specs = [a_spec, b_spec]
    args = [a, b]
    if add is not None:
        in_specs.append(o_spec)
        args.append(add)
    if after is not None:
        in_specs.append(pl.BlockSpec(memory_space=pl.ANY))
        args.append(after)
    out_shapes = [jax.ShapeDtypeStruct(out_shape, out_dtype)]
    if bf16_copy:
        out_shapes.append(jax.ShapeDtypeStruct(out_shape, BF16))
    res = pl.pallas_call(
        body, name=name, grid=grid, in_specs=in_specs, out_specs=[o_spec] * n_out, out_shape=out_shapes,
        scratch_shapes=[pltpu.VMEM(acc_shape, F32)] if nk > 1 else [],
        compiler_params=_arb(3),
    )(*args)
    return res if bf16_copy else res[0]


def _bs(block, fn):
    return pl.BlockSpec(block, fn)


def _rmsnorm_fwd(name, src, g, d, cb, ts=512):
    s = src.shape[0]

    def body(x_ref, g_ref, h_ref, r_ref):
        x = x_ref[...]
        r = lax.rsqrt(jnp.mean(x * x, axis=-1, keepdims=True) + EPS)
        h_ref[...] = (x * r * g_ref[...]).astype(BF16)
        r_ref[...] = r

    return pl.pallas_call(
        body, name=name, grid=(s // ts,),
        in_specs=[_bs((ts, d), lambda i: (i, cb)), _bs((1, d), lambda i: (0, 0))],
        out_specs=[_bs((ts, d), lambda i: (i, 0)), _bs((ts, 1), lambda i: (i, 0))],
        out_shape=[jax.ShapeDtypeStruct((s, d), BF16), jax.ShapeDtypeStruct((s, 1), F32)],
        compiler_params=_arb(1),
    )(src, g)


def _rmsnorm_bwd(name, dy, src, rstd, g, d, cb, out_dtype, res=None, bf16_copy=False, ts=512):
    s = src.shape[0]

    def body(*refs):
        dy_ref, x_ref, r_ref, g_ref = refs[:4]
        res_ref = refs[4] if res is not None else None
        dx_ref, dg_ref = refs[n_in:n_in + 2]
        dyv = dy_ref[...]
        r = r_ref[...]
        xhat = x_ref[...] * r
        dyh = dyv * g_ref[...]
        c = jnp.mean(dyh * xhat, axis=-1, keepdims=True)
        dx = r * (dyh - xhat * c)
        if res_ref is not None:
            dx = res_ref[...] + dx
        dx_ref[...] = dx.astype(out_dtype)
        if bf16_copy:
            refs[n_in + 2][...] = dx.astype(BF16)
        part = jnp.sum(dyv * xhat, axis=0, keepdims=True)

        @pl.when(pl.program_id(0) == 0)
        def _():
            dg_ref[...] = part

        @pl.when(pl.program_id(0) > 0)
        def _():
            dg_ref[...] += part

    in_specs = [_bs((ts, d), lambda i: (i, 0)), _bs((ts, d), lambda i: (i, cb)),
                _bs((ts, 1), lambda i: (i, 0)), _bs((1, d), lambda i: (0, 0))]
    args = [dy, src, rstd, g]
    if res is not None:
        in_specs.append(_bs((ts, d), lambda i: (i, 0)))
        args.append(res)
    n_in = len(args)
    out_specs = [_bs((ts, d), lambda i: (i, 0)), _bs((1, d), lambda i: (0, 0))]
    out_shape = [jax.ShapeDtypeStruct((s, d), out_dtype), jax.ShapeDtypeStruct((1, d), F32)]
    if bf16_copy:
        out_specs.append(_bs((ts, d), lambda i: (i, 0)))
        out_shape.append(jax.ShapeDtypeStruct((s, d), BF16))
    return pl.pallas_call(
        body, name=name, grid=(s // ts,), in_specs=in_specs, out_specs=out_specs, out_shape=out_shape,
        compiler_params=_arb(1),
    )(*args)


def _matmul_norm_bwd(name, a, b, *, grid, a_spec, b_spec, x, rstd, g, res, bf16_copy=False, after=None):
    s, d = x.shape
    ni, nk = grid
    assert nk >= 2, "the first and the last contraction step are distinct branches"
    tm = s // ni
    n_in = 6 + (after is not None)

    def body(*refs):
        a_ref, b_ref, x_ref, r_ref, g_ref, res_ref = refs[:6]
        dx_ref, dg_ref = refs[n_in:n_in + 2]
        acc_ref = refs[-1]
        k = pl.program_id(1)
        prod = lax.dot_general(a_ref[...], b_ref[...], NT, preferred_element_type=F32)

        @pl.when(k == 0)
        def _():
            acc_ref[...] = prod

        @pl.when((k > 0) & (k < nk - 1))
        def _():
            acc_ref[...] += prod

        @pl.when(k == nk - 1)
        def _():
            dyv = acc_ref[...] + prod
            r = r_ref[...]
            xhat = x_ref[...] * r
            dyh = dyv * g_ref[...]
            c = jnp.mean(dyh * xhat, axis=-1, keepdims=True)
            dx = res_ref[...] + r * (dyh - xhat * c)
            dx_ref[...] = dx
            if bf16_copy:
                refs[n_in + 2][...] = dx.astype(BF16)
            part = jnp.sum(dyv * xhat, axis=0, keepdims=True)

            @pl.when(pl.program_id(0) == 0)
            def _():
                dg_ref[...] = part

            @pl.when(pl.program_id(0) > 0)
            def _():
                dg_ref[...] += part

    rows = _bs((tm, d), lambda i, k: (i, 0))
    in_specs = [a_spec, b_spec, rows, _bs((tm, 1), lambda i, k: (i, 0)), _bs((1, d), lambda i, k: (0, 0)), rows]
    args = [a, b, x, rstd, g, res]
    if after is not None:
        in_specs.append(pl.BlockSpec(memory_space=pl.ANY))
        args.append(after)
    out_specs = [rows, _bs((1, d), lambda i, k: (0, 0))]
    out_shape = [jax.ShapeDtypeStruct((s, d), F32), jax.ShapeDtypeStruct((1, d), F32)]
    if bf16_copy:
        out_specs.append(rows)
        out_shape.append(jax.ShapeDtypeStruct((s, d), BF16))
    return pl.pallas_call(
        body, name=name, grid=grid, in_specs=in_specs, out_specs=out_specs, out_shape=out_shape,
        scratch_shapes=[pltpu.VMEM((tm, d), F32)], compiler_params=_arb(2),
    )(*args)


def _ffn_down_loss(f, w_down, x1, target, g, ts=512):
    s, d = x1.shape
    dff = f.shape[1]

    def body(f_ref, w_ref, x_ref, t_ref, g_ref, loss_ref, dx_ref, dg_ref, dxb_ref):
        x = x_ref[...] + jnp.dot(f_ref[...], w_ref[...], preferred_element_type=F32)
        r = lax.rsqrt(jnp.mean(x * x, axis=-1, keepdims=True) + EPS)
        xhat = x * r
        gv = g_ref[...]
        err = xhat * gv - t_ref[...]
        lpart = 0.5 * jnp.sum(jnp.mean(err * err, axis=-1, keepdims=True), axis=0, keepdims=True)
        dyv = err * (1.0 / d)
        dyh = dyv * gv
        c = jnp.mean(dyh * xhat, axis=-1, keepdims=True)
        dx = r * (dyh - xhat * c)
        dx_ref[...] = dx
        dxb_ref[...] = dx.astype(BF16)
        gpart = jnp.sum(dyv * xhat, axis=0, keepdims=True)

        @pl.when(pl.program_id(0) == 0)
        def _():
            dg_ref[...] = gpart
            loss_ref[...] = lpart

        @pl.when(pl.program_id(0) > 0)
        def _():
            dg_ref[...] += gpart
            loss_ref[...] += lpart

    rows = _bs((ts, d), lambda i: (i, 0))
    return pl.pallas_call(
        body, name="ffn_down_loss", grid=(s // ts,),
        in_specs=[_bs((ts, dff), lambda i: (i, 0)), _bs((dff, d), lambda i: (0, 0)), rows, rows,
                  _bs((1, d), lambda i: (0, 0))],
        out_specs=[_bs((1, 1), lambda i: (0, 0)), rows, _bs((1, d), lambda i: (0, 0)), rows],
        out_shape=[jax.ShapeDtypeStruct((1, 1), F32), jax.ShapeDtypeStruct((s, d), F32),
                   jax.ShapeDtypeStruct((1, d), F32), jax.ShapeDtypeStruct((s, d), BF16)],
        compiler_params=_arb(1),
    )(f, w_down, x1, target, g)


def _swap_halves(t):
    lane = lax.broadcasted_iota(jnp.int32, t.shape, 1)
    return jnp.where(lane < 32, pltpu.roll(t, 96, 1), pltpu.roll(t, 32, 1))


def _rope_fwd(t, cos_t, sin_t):
    return t * cos_t + _swap_halves(t) * sin_t


def _rope_bwd(dt, cos_t, sin_t):
    return dt * cos_t - _swap_halves(dt) * sin_t


def _lat_norms(proj, gq, gkv, ts=512):
    s = proj.shape[0]

    def body(q_ref, c_ref, gq_ref, gkv_ref, qn_ref, cn_ref, rq_ref, rc_ref):
        q = q_ref[...]
        rq = lax.rsqrt(jnp.mean(q * q, axis=-1, keepdims=True) + EPS)
        qn_ref[...] = (q * rq * gq_ref[...]).astype(BF16)
        rq_ref[...] = rq
        cv = c_ref[...]
        rc = lax.rsqrt(jnp.mean(cv * cv, axis=-1, keepdims=True) + EPS)
        cn_ref[...] = (cv * rc * gkv_ref[...]).astype(BF16)
        rc_ref[...] = rc

    return pl.pallas_call(
        body, name="lat_norms", grid=(s // ts,),
        in_specs=[_bs((ts, Q_LORA), lambda i: (i, P_QLAT // Q_LORA)),
                  _bs((ts, KV_LORA), lambda i: (i, P_CKV // KV_LORA)),
                  _bs((1, Q_LORA), lambda i: (0, 0)), _bs((1, KV_LORA), lambda i: (0, 0))],
        out_specs=[_bs((ts, Q_LORA), lambda i: (i, 0)), _bs((ts, KV_LORA), lambda i: (i, 0)),
                   _bs((ts, 1), lambda i: (i, 0)), _bs((ts, 1), lambda i: (i, 0))],
        out_shape=[jax.ShapeDtypeStruct((s, Q_LORA), BF16), jax.ShapeDtypeStruct((s, KV_LORA), BF16),
                   jax.ShapeDtypeStruct((s, 1), F32), jax.ShapeDtypeStruct((s, 1), F32)],
        compiler_params=_arb(1),
    )(proj, proj, gq, gkv)


HEAD_ROWS = 2048
DW_ROWS = 2048
MM_ROWS = 1024


def _q_heads(qn, wq, cos_t, sin_t):
    s = qn.shape[0]
    ts = min(s, HEAD_ROWS)

    def body(qn_ref, w_ref, cos_ref, sin_ref, q_ref):
        o = jnp.dot(qn_ref[...], w_ref[...], preferred_element_type=F32)
        q_ref[:, :LANES] = o[:, :LANES].astype(BF16)
        q_ref[:, LANES:] = _rope_fwd(o[:, LANES:], cos_ref[...], sin_ref[...]).astype(BF16)

    return pl.pallas_call(
        body, name="q_heads", grid=(s // ts, H_A),
        in_specs=[_bs((ts, Q_LORA), lambda i, h: (i, 0)), _bs((None, Q_LORA, HEAD_PAD), lambda i, h: (h, 0, 0)),
                  _bs((ts, LANES), lambda i, h: (i, 0)), _bs((ts, LANES), lambda i, h: (i, 0))],
        out_specs=_bs((None, ts, HEAD_PAD), lambda i, h: (h, i, 0)),
        out_shape=jax.ShapeDtypeStruct((H_A, s, HEAD_PAD), BF16),
        compiler_params=_arb(2),
    )(qn, wq, cos_t, sin_t)


def _kv_heads(cn, wkv, proj, cos_t, sin_t):
    s = cn.shape[0]
    ts = min(s, HEAD_ROWS)

    def body(cn_ref, w_ref, kr_ref, cos_ref, sin_ref, k_ref, v_ref):
        o = jnp.dot(cn_ref[...], w_ref[...], preferred_element_type=F32)
        k_ref[:, :LANES] = o[:, :LANES].astype(BF16)
        k_ref[:, LANES:] = _rope_fwd(kr_ref[...], cos_ref[...], sin_ref[...]).astype(BF16)
        v_ref[...] = o[:, LANES:].astype(BF16)

    return pl.pallas_call(
        body, name="kv_heads", grid=(s // ts, H_A),
        in_specs=[_bs((ts, KV_LORA), lambda i, h: (i, 0)),
                  _bs((None, KV_LORA, QK_NOPE + V_DIM), lambda i, h: (h, 0, 0)),
                  _bs((ts, LANES), lambda i, h: (i, P_KR // LANES)),
                  _bs((ts, LANES), lambda i, h: (i, 0)), _bs((ts, LANES), lambda i, h: (i, 0))],
        out_specs=[_bs((None, ts, HEAD_PAD), lambda i, h: (h, i, 0)), _bs((None, ts, V_DIM), lambda i, h: (h, i, 0))],
        out_shape=[jax.ShapeDtypeStruct((H_A, s, HEAD_PAD), BF16), jax.ShapeDtypeStruct((H_A, s, V_DIM), BF16)],
        compiler_params=_arb(2),
    )(cn, wkv, proj, cos_t, sin_t)


MLA_SCALE = 1.0 / math.sqrt(QK_NOPE + QK_ROPE)
LOG2E = math.log2(math.e)
MLA_EXP2_SCALE = MLA_SCALE * LOG2E


def _lane_tiles(a):
    return [a[:, j * LANES:(j + 1) * LANES] for j in range(a.shape[1] // LANES)]


MLA_SUB = 512


def _mla_fwd(q, k, v, tq=512, tk=1024):
    s = q.shape[1]
    tq = min(tq, s)
    nk = s // tk

    def body(q_ref, k_ref, v_ref, o_ref, lse_ref, m_ref, l_ref, acc_ref):
        m_ref[...] = jnp.full(m_ref.shape, -jnp.inf, F32)
        l_ref[...] = jnp.zeros(l_ref.shape, F32)
        acc_ref[...] = jnp.zeros(acc_ref.shape, F32)

        def step(c, carry):
            rows = pl.ds(pl.multiple_of(c * tk, tk), tk)
            for sub in range(tq // MLA_SUB):
                qr = slice(sub * MLA_SUB, (sub + 1) * MLA_SUB)
                raw = lax.dot_general(q_ref[qr, :], k_ref[rows, :], NT, preferred_element_type=F32)
                m_prev = m_ref[qr, :]
                m_new = jnp.maximum(m_prev, jnp.max(raw, axis=-1, keepdims=True))
                alpha = jnp.exp2((m_prev - m_new) * MLA_EXP2_SCALE)
                ps = [jnp.exp2((t - m_new) * MLA_EXP2_SCALE) for t in _lane_tiles(raw)]
                l_ref[qr, :] = alpha * l_ref[qr, :] + functools.reduce(lambda a, b: a + b, ps)
                p = jnp.concatenate(ps, axis=1).astype(BF16)
                acc_ref[qr, :] = alpha * acc_ref[qr, :] + jnp.dot(p, v_ref[rows, :], preferred_element_type=F32)
                m_ref[qr, :] = m_new
            return carry

        lax.fori_loop(0, nk, step, 0, unroll=True)
        l = jnp.sum(l_ref[...], axis=-1, keepdims=True)
        o_ref[...] = acc_ref[...] / l
        lse_ref[...] = m_ref[...] * MLA_SCALE + jnp.log(l)

    return pl.pallas_call(
        body, name="mla_fwd", grid=(H_A, s // tq),
        in_specs=[_bs((None, tq, HEAD_PAD), lambda h, i: (h, i, 0)),
                  _bs((None, s, HEAD_PAD), lambda h, i: (h, 0, 0)),
                  _bs((None, s, V_DIM), lambda h, i: (h, 0, 0))],
        out_specs=[_bs((tq, V_DIM), lambda h, i: (i, h)), _bs((None, tq, LANES), lambda h, i: (h, i, 0))],
        out_shape=[jax.ShapeDtypeStruct((s, H_A * V_DIM), F32), jax.ShapeDtypeStruct((H_A, s, LANES), F32)],
        scratch_shapes=[pltpu.VMEM((tq, LANES), F32), pltpu.VMEM((tq, LANES), F32), pltpu.VMEM((tq, V_DIM), F32)],
        compiler_params=_arb(2),
    )(q, k, v)


def _mla_bwd(q, k, v, do, o, lse, cos_t, sin_t, tq=512, tk=512):
    s = q.shape[1]
    nq = s // tq
    nkb = s // tk

    def body(q_ref, k_ref, v_ref, do_ref, o_ref, lse_ref, cos_ref, sin_ref, dqp_ref, dkvp_ref, dkr_ref,
             delta_ref, dq_ref, dk_ref, dv_ref):
        @pl.when(pl.program_id(1) == 0)
        def _():
            def init(c, carry):
                rows = pl.ds(pl.multiple_of(c * tq, tq), tq)
                delta = jnp.sum(do_ref[rows, :] * o_ref[rows, :], axis=-1, keepdims=True)
                delta_ref[rows, :] = jnp.broadcast_to(delta, (tq, LANES))
                dq_ref[rows, :] = jnp.zeros((tq, HEAD_PAD), F32)
                return carry

            lax.fori_loop(0, nq, init, 0)

        dk_ref[...] = jnp.zeros(dk_ref.shape, F32)
        dv_ref[...] = jnp.zeros(dv_ref.shape, F32)
        kb = k_ref[...]
        vb = v_ref[...]

        def step(c, carry):
            rows = pl.ds(pl.multiple_of(c * tq, tq), tq)
            qc = q_ref[rows, :]
            doc = do_ref[rows, :].astype(BF16)
            raw = lax.dot_general(qc, kb, NT, preferred_element_type=F32)
            dp = lax.dot_general(doc, vb, NT, preferred_element_type=F32)
            lse2 = lse_ref[rows, :] * LOG2E
            delta = delta_ref[rows, :]
            ps = [jnp.exp2(t * MLA_EXP2_SCALE - lse2) for t in _lane_tiles(raw)]
            dss = [pj * (dj - delta) * MLA_SCALE for pj, dj in zip(ps, _lane_tiles(dp))]
            p = jnp.concatenate(ps, axis=1).astype(BF16)
            ds = jnp.concatenate(dss, axis=1).astype(BF16)
            dv_ref[...] += lax.dot_general(p, doc, TN, preferred_element_type=F32)
            dk_ref[...] += lax.dot_general(ds, qc, TN, preferred_element_type=F32)
            dq_ref[rows, :] += jnp.dot(ds, kb, preferred_element_type=F32)
            return carry

        lax.fori_loop(0, nq, step, 0, unroll=True)
        dkvp_ref[:, :LANES] = dk_ref[:, :LANES].astype(BF16)
        dkvp_ref[:, LANES:] = dv_ref[...].astype(BF16)
        dkr_ref[...] = dk_ref[:, LANES:]

        @pl.when(pl.program_id(1) == nkb - 1)
        def _():
            def finish(c, carry):
                rows = pl.ds(pl.multiple_of(c * tq, tq), tq)
                dqp_ref[rows, :LANES] = dq_ref[rows, :LANES].astype(BF16)
                dqp_ref[rows, LANES:] = _rope_bwd(dq_ref[rows, LANES:], cos_ref[rows, :], sin_ref[rows, :]).astype(BF16)
                return carry

            lax.fori_loop(0, nq, finish, 0)

    whole = lambda w: _bs((s, w), lambda h, j: (0, 0))
    return pl.pallas_call(
        body, name="mla_bwd", grid=(H_A, nkb),
        in_specs=[_bs((None, s, HEAD_PAD), lambda h, j: (h, 0, 0)),
                  _bs((None, tk, HEAD_PAD), lambda h, j: (h, j, 0)),
                  _bs((None, tk, V_DIM), lambda h, j: (h, j, 0)),
                  _bs((s, V_DIM), lambda h, j: (0, h)), _bs((s, V_DIM), lambda h, j: (0, h)),
                  _bs((None, s, LANES), lambda h, j: (h, 0, 0)), whole(LANES), whole(LANES)],
        out_specs=[_bs((None, s, HEAD_PAD), lambda h, j: (h, 0, 0)),
                   _bs((None, tk, HEAD_PAD), lambda h, j: (h, j, 0)),
                   _bs((None, tk, LANES), lambda h, j: (h, j, 0))],
        out_shape=[jax.ShapeDtypeStruct((H_A, s, HEAD_PAD), BF16), jax.ShapeDtypeStruct((H_A, s, HEAD_PAD), BF16),
                   jax.ShapeDtypeStruct((H_A, s, LANES), F32)],
        scratch_shapes=[pltpu.VMEM((s, LANES), F32), pltpu.VMEM((s, HEAD_PAD), F32), pltpu.VMEM((tk, HEAD_PAD), F32),
                        pltpu.VMEM((tk, V_DIM), F32)],
        compiler_params=_arb(2),
    )(q, k, v, do, o, lse, cos_t, sin_t)


def _mla_key_rope_grad(dkr, cos_t, sin_t, ts=512):
    s = dkr.shape[1]

    def body(d_ref, cos_ref, sin_ref, o_ref):
        tot = d_ref[0]
        for h in range(1, H_A):
            tot = tot + d_ref[h]
        o_ref[...] = _rope_bwd(tot, cos_ref[...], sin_ref[...]).astype(BF16)

    rows = _bs((ts, LANES), lambda i: (i, 0))
    return pl.pallas_call(
        body, name="mla_key_rope_grad", grid=(s // ts,),
        in_specs=[_bs((H_A, ts, LANES), lambda i: (0, i, 0)), rows, rows], out_specs=rows,
        out_shape=jax.ShapeDtypeStruct((s, LANES), BF16), compiler_params=_arb(1),
    )(dkr, cos_t, sin_t)


WIN_SCALE = 1.0 / math.sqrt(HD_B)
SPAN = Q_BLOCK + 2 * WINDOW


def _t5_bucket_table():
    a = jnp.arange(Q_BLOCK, dtype=jnp.int32)[:, None]
    c = jnp.arange(SPAN, dtype=jnp.int32)[None, :]
    rel = c - WINDOW - a
    nb = NUM_BUCKETS // 2
    max_exact = nb // 2
    base = (rel > 0).astype(jnp.int32) * nb
    n = jnp.abs(rel)
    nf = jnp.maximum(n, 1).astype(F32)
    large = max_exact + (jnp.log(nf / max_exact) / math.log(MAX_DISTANCE / max_exact)
                         * (nb - max_exact)).astype(jnp.int32)
    large = jnp.minimum(large, nb - 1)
    return base + jnp.where(n < max_exact, n, large)


def _win_bias(bucket, rel_bias):
    def body(rb_ref, bk_ref, o_ref):
        h = pl.program_id(0)
        bk = bk_ref[...]
        acc = jnp.zeros((Q_BLOCK, SPAN), F32)
        for b in range(NUM_BUCKETS):
            acc = jnp.where(bk == b, rb_ref[b, h], acc)
        o_ref[...] = acc

    return pl.pallas_call(
        body, name="win_bias", grid=(H_B,),
        in_specs=[pl.BlockSpec(memory_space=pltpu.SMEM), _bs((Q_BLOCK, SPAN), lambda h: (0, 0))],
        out_specs=_bs((None, Q_BLOCK, SPAN), lambda h: (h, 0, 0)),
        out_shape=jax.ShapeDtypeStruct((H_B, Q_BLOCK, SPAN), F32),
        compiler_params=_arb(1),
    )(rel_bias, bucket)


WIN_HEADS = GROUP


def _win_kv_rows(n, j, nblk):
    blk = jnp.clip(n + j - 1, 0, nblk - 1)
    return pl.ds(pl.multiple_of(blk * Q_BLOCK, Q_BLOCK), Q_BLOCK)


def _win_kv_cols(h0):
    kv = h0 // GROUP
    return slice(kv * HD_B, (kv + 1) * HD_B)


def _win_stack(ref, h0):
    return jnp.concatenate([ref[:, (h0 + g) * HD_B:(h0 + g + 1) * HD_B] for g in range(WIN_HEADS)], axis=0)


def _win_unstack(ref, h0, val):
    for g in range(WIN_HEADS):
        ref[:, (h0 + g) * HD_B:(h0 + g + 1) * HD_B] = val[g * Q_BLOCK:(g + 1) * Q_BLOCK].astype(ref.dtype)


def _win_scores(q, k_ref, h0, bias_ref, n, nblk):
    a = lax.broadcasted_iota(jnp.int32, (WIN_HEADS, Q_BLOCK, Q_BLOCK), 1)
    cc = lax.broadcasted_iota(jnp.int32, (WIN_HEADS, Q_BLOCK, Q_BLOCK), 2)
    valid = [(cc >= a) & (n > 0), None, (cc <= a) & (n < nblk - 1)]
    out = []
    for j in range(3):
        sc = lax.dot_general(q, k_ref[_win_kv_rows(n, j, nblk), _win_kv_cols(h0)], NT, preferred_element_type=F32)
        sc = (sc.reshape(WIN_HEADS, Q_BLOCK, Q_BLOCK) * WIN_SCALE
              + bias_ref[h0:h0 + WIN_HEADS, :, j * Q_BLOCK:(j + 1) * Q_BLOCK])
        if valid[j] is not None:
            sc = jnp.where(valid[j], sc, -1e30)
        out.append(sc)
    return out


def _win_sink(sink_ref, h0):
    hs = lax.broadcasted_iota(jnp.int32, (WIN_HEADS, Q_BLOCK, 1), 0)
    sk = jnp.zeros((WIN_HEADS, Q_BLOCK, 1), F32)
    for g in range(WIN_HEADS):
        sk = jnp.where(hs == g, sink_ref[h0 + g], sk)
    return sk


def _win_fwd(proj_b, bias, sinks):
    s = proj_b.shape[0]
    nblk = s // Q_BLOCK
    rows = WIN_HEADS * Q_BLOCK

    def body(sink_ref, q_ref, k_ref, v_ref, bias_ref, o_ref, lse_ref):
        n = pl.program_id(0)
        for h0 in range(0, H_B, WIN_HEADS):
            sk = _win_sink(sink_ref, h0)
            q = _win_stack(q_ref, h0)
            ss = _win_scores(q, k_ref, h0, bias_ref, n, nblk)
            m = jnp.maximum(jnp.max(jnp.maximum(jnp.maximum(ss[0], ss[1]), ss[2]), axis=2, keepdims=True), sk)
            es = [jnp.exp(sc - m) for sc in ss]
            l = jnp.sum(es[0] + es[1] + es[2], axis=2, keepdims=True) + jnp.exp(sk - m)
            acc = jnp.zeros((rows, HD_B), F32)
            for j, e in enumerate(es):
                p = (e / l).astype(BF16).reshape(rows, Q_BLOCK)
                acc = acc + jnp.dot(p, v_ref[_win_kv_rows(n, j, nblk), _win_kv_cols(h0)],
                                    preferred_element_type=F32)
            _win_unstack(o_ref, h0, acc)
            lse_ref[h0:h0 + WIN_HEADS] = m + jnp.log(l)

    kv_w = KV_B * HD_B
    return pl.pallas_call(
        body, name="win_fwd", grid=(nblk,),
        in_specs=[pl.BlockSpec(memory_space=pltpu.SMEM), _bs((Q_BLOCK, H_B * HD_B), lambda n: (n, P_QB // (H_B * HD_B))),
                  _bs((s, kv_w), lambda n: (0, P_KB // kv_w)), _bs((s, kv_w), lambda n: (0, P_VB // kv_w)),
                  _bs((H_B, Q_BLOCK, SPAN), lambda n: (0, 0, 0))],
        out_specs=[_bs((Q_BLOCK, H_B * HD_B), lambda n: (n, 0)), _bs((H_B, Q_BLOCK, 1), lambda n: (0, n, 0))],
        out_shape=[jax.ShapeDtypeStruct((s, H_B * HD_B), F32), jax.ShapeDtypeStruct((H_B, s, 1), F32)],
        compiler_params=_arb(1),
    )(sinks, proj_b, proj_b, proj_b, bias)


def _win_bwd(proj_b, bias, sinks, do_b, lse):
    s = proj_b.shape[0]
    nblk = s // Q_BLOCK
    rows = WIN_HEADS * Q_BLOCK
    spad = s + 2 * WINDOW

    def body(sink_ref, q_ref, k_ref, v_ref, bias_ref, do_ref, lse_ref, dq_ref, dk_ref, dv_ref, db_ref, dsk_ref):
        n = pl.program_id(0)

        @pl.when(n == 0)
        def _():
            dk_ref[...] = jnp.zeros(dk_ref.shape, F32)
            dv_ref[...] = jnp.zeros(dv_ref.shape, F32)
            db_ref[...] = jnp.zeros(db_ref.shape, F32)
            dsk_ref[...] = jnp.zeros(dsk_ref.shape, F32)

        for h0 in range(0, H_B, WIN_HEADS):
            heads = slice(h0, h0 + WIN_HEADS)
            sk = _win_sink(sink_ref, h0)
            q = _win_stack(q_ref, h0)
            dob = _win_stack(do_ref, h0)
            lse_v = lse_ref[heads]
            ss = _win_scores(q, k_ref, h0, bias_ref, n, nblk)
            ps = [jnp.exp(sc - lse_v) for sc in ss]
            dps = [lax.dot_general(dob, v_ref[_win_kv_rows(n, j, nblk), _win_kv_cols(h0)], NT,
                                   preferred_element_type=F32).reshape(WIN_HEADS, Q_BLOCK, Q_BLOCK) for j in range(3)]
            delta = jnp.sum(ps[0] * dps[0] + ps[1] * dps[1] + ps[2] * dps[2], axis=2, keepdims=True)
            dq = jnp.zeros((rows, HD_B), F32)
            for j in range(3):
                ds = ps[j] * (dps[j] - delta)
                db_ref[heads, :, j * Q_BLOCK:(j + 1) * Q_BLOCK] += ds
                dsb = (ds * WIN_SCALE).astype(BF16).reshape(rows, Q_BLOCK)
                dq = dq + jnp.dot(dsb, k_ref[_win_kv_rows(n, j, nblk), _win_kv_cols(h0)],
                                  preferred_element_type=F32)
                krows = pl.ds(pl.multiple_of((n + j) * Q_BLOCK, Q_BLOCK), Q_BLOCK)
                dk_ref[krows, _win_kv_cols(h0)] += lax.dot_general(dsb, q, TN, preferred_element_type=F32)
                dv_ref[krows, _win_kv_cols(h0)] += lax.dot_general(
                    ps[j].astype(BF16).reshape(rows, Q_BLOCK), dob, TN, preferred_element_type=F32)
            dsk_ref[heads] += -(jnp.exp(sk - lse_v) * delta)
            _win_unstack(dq_ref, h0, dq)

    kv_w = KV_B * HD_B
    qspec = _bs((Q_BLOCK, H_B * HD_B), lambda n: (n, 0))
    kacc = _bs((spad, kv_w), lambda n: (0, 0))
    return pl.pallas_call(
        body, name="win_bwd", grid=(nblk,),
        in_specs=[pl.BlockSpec(memory_space=pltpu.SMEM), _bs((Q_BLOCK, H_B * HD_B), lambda n: (n, P_QB // (H_B * HD_B))),
                  _bs((s, kv_w), lambda n: (0, P_KB // kv_w)), _bs((s, kv_w), lambda n: (0, P_VB // kv_w)),
                  _bs((H_B, Q_BLOCK, SPAN), lambda n: (0, 0, 0)), qspec, _bs((H_B, Q_BLOCK, 1), lambda n: (0, n, 0))],
        out_specs=[qspec, kacc, kacc, _bs((H_B, Q_BLOCK, SPAN), lambda n: (0, 0, 0)),
                   _bs((H_B, Q_BLOCK, 1), lambda n: (0, 0, 0))],
        out_shape=[jax.ShapeDtypeStruct((s, H_B * HD_B), BF16), jax.ShapeDtypeStruct((spad, kv_w), F32),
                   jax.ShapeDtypeStruct((spad, kv_w), F32), jax.ShapeDtypeStruct((H_B, Q_BLOCK, SPAN), F32),
                   jax.ShapeDtypeStruct((H_B, Q_BLOCK, 1), F32)],
        compiler_params=_arb(1),
    )(sinks, proj_b, proj_b, proj_b, bias, do_b, lse)


def _win_param_grads(bucket, dbias, dsink_rows):
    def body(bk_ref, db_ref, ds_ref, o_ref):
        bk = bk_ref[...]
        dbv = db_ref[...]
        lane = lax.broadcasted_iota(jnp.int32, (1, LANES), 1)
        res = jnp.zeros((1, LANES), F32)
        for b in range(NUM_BUCKETS):
            tot = jnp.sum(jnp.sum(jnp.where(bk == b, dbv, 0.0), axis=1, keepdims=True), axis=0, keepdims=True)
            res = jnp.where(lane == b, tot, res)
        stot = jnp.sum(ds_ref[...], axis=0, keepdims=True)
        o_ref[...] = jnp.where(lane == NUM_BUCKETS, stot, res)

    return pl.pallas_call(
        body, name="win_param_grads", grid=(H_B,),
        in_specs=[_bs((Q_BLOCK, SPAN), lambda h: (0, 0)), _bs((None, Q_BLOCK, SPAN), lambda h: (h, 0, 0)),
                  _bs((None, Q_BLOCK, 1), lambda h: (h, 0, 0))],
        out_specs=_bs((None, 1, LANES), lambda h: (h, 0, 0)),
        out_shape=jax.ShapeDtypeStruct((H_B, 1, LANES), F32),
        compiler_params=_arb(1),
    )(bucket, dbias, dsink_rows)


def _mix_out_norm(proj, o_a, o_b, w_out, x, g2, ts=512):
    s = o_a.shape[0]
    wide = lambda cb: _bs((ts, D_MODEL), lambda i: (i, cb))

    def body(ga_ref, gb_ref, oa_ref, ob_ref, w_ref, x_ref, g_ref, m_ref, x1_ref, h_ref, r_ref):
        mixed = (jax.nn.sigmoid(ga_ref[...]) * oa_ref[...] + jax.nn.sigmoid(gb_ref[...]) * ob_ref[...]).astype(BF16)
        m_ref[...] = mixed
        x1 = x_ref[...] + jnp.dot(mixed, w_ref[...], preferred_element_type=F32)
        x1_ref[...] = x1
        r = lax.rsqrt(jnp.mean(x1 * x1, axis=-1, keepdims=True) + EPS)
        h_ref[...] = (x1 * r * g_ref[...]).astype(BF16)
        r_ref[...] = r

    return pl.pallas_call(
        body, name="mix_out_norm", grid=(s // ts,),
        in_specs=[wide(P_GA // D_MODEL), wide(P_GB // D_MODEL), wide(0), wide(0),
                  _bs((D_MODEL, D_MODEL), lambda i: (0, 0)), wide(0), _bs((1, D_MODEL), lambda i: (0, 0))],
        out_specs=[wide(0), wide(0), wide(0), _bs((ts, 1), lambda i: (i, 0))],
        out_shape=[jax.ShapeDtypeStruct((s, D_MODEL), BF16), jax.ShapeDtypeStruct((s, D_MODEL), F32),
                   jax.ShapeDtypeStruct((s, D_MODEL), BF16), jax.ShapeDtypeStruct((s, 1), F32)],
        compiler_params=_arb(1),
    )(proj, proj, o_a, o_b, w_out, x, g2)


def _mix_out_bwd(dx1_b, w_out, proj, o_a, o_b, after=None, ts=512):
    s = o_a.shape[0]
    wide = lambda cb: _bs((ts, D_MODEL), lambda i: (i, cb))

    def body(dx_ref, w_ref, ga_ref, gb_ref, oa_ref, ob_ref, *rest):
        doa_ref, dob_ref, dga_ref, dgb_ref = rest[-4:]
        dm = lax.dot_general(dx_ref[...], w_ref[...], NT, preferred_element_type=F32)
        sa = jax.nn.sigmoid(ga_ref[...])
        sb = jax.nn.sigmoid(gb_ref[...])
        doa_ref[...] = dm * sa
        dob_ref[...] = (dm * sb).astype(BF16)
        dga_ref[...] = (dm * oa_ref[...] * (sa * (1.0 - sa))).astype(BF16)
        dgb_ref[...] = (dm * ob_ref[...] * (sb * (1.0 - sb))).astype(BF16)

    extra = [after] if after is not None else []
    return pl.pallas_call(
        body, name="mix_out_bwd", grid=(s // ts,),
        in_specs=[wide(0), _bs((D_MODEL, D_MODEL), lambda i: (0, 0)), wide(P_GA // D_MODEL), wide(P_GB // D_MODEL),
                  wide(0), wide(0)] + [pl.BlockSpec(memory_space=pl.ANY)] * len(extra),
        out_specs=[wide(0)] * 4,
        out_shape=[jax.ShapeDtypeStruct((s, D_MODEL), F32), jax.ShapeDtypeStruct((s, D_MODEL), BF16),
                   jax.ShapeDtypeStruct((s, D_MODEL), BF16), jax.ShapeDtypeStruct((s, D_MODEL), BF16)],
        compiler_params=_arb(1),
    )(dx1_b, w_out, proj, proj, o_a, o_b, *extra)


CONV_CHUNK = 128
N_SLAB = D_FF // LANES


def _shifted(ref, c, nchunks):
    r0 = c * CONV_CHUNK
    cur = ref[r0:r0 + CONV_CHUNK, :]
    row = lax.broadcasted_iota(jnp.int32, (8, LANES), 0)
    if c > 0:
        prev = ref[r0 - 1:r0 - 1 + CONV_CHUNK, :]
    else:
        down = pltpu.roll(cur, 1, 0)
        prev = jnp.concatenate([jnp.where(row == 0, 0.0, down[:8]), down[8:]], axis=0)
    if c < nchunks - 1:
        nxt = ref[r0 + 1:r0 + 1 + CONV_CHUNK, :]
    else:
        up = pltpu.roll(cur, CONV_CHUNK - 1, 0)
        nxt = jnp.concatenate([up[:-8], jnp.where(row == 7, 0.0, up[-8:])], axis=0)
    return prev, cur, nxt


def _conv_taps(ref, w_ref, b_ref, c, nchunks):
    prev, cur, nxt = _shifted(ref, c, nchunks)
    conv = prev * w_ref[0:1, :] + cur * w_ref[1:2, :] + nxt * w_ref[2:3, :] + b_ref[...]
    return conv, prev, cur, nxt


def _convffn_fwd(u, conv_w, conv_b):
    s = u.shape[0]
    nchunks = s // CONV_CHUNK

    def body(ug_ref, uv_ref, wg_ref, wv_ref, bg_ref, bv_ref, f_ref):
        for c in range(nchunks):
            cg = _conv_taps(ug_ref, wg_ref, bg_ref, c, nchunks)[0]
            cv = _conv_taps(uv_ref, wv_ref, bv_ref, c, nchunks)[0]
            f_ref[c * CONV_CHUNK:(c + 1) * CONV_CHUNK, :] = (cg * jax.nn.sigmoid(cg) * cv).astype(BF16)

    slab = lambda off: _bs((s, LANES), lambda j: (0, off + j))
    wsl = lambda off: _bs((3, LANES), lambda j: (0, off + j))
    bsl = lambda off: _bs((1, LANES), lambda j: (0, off + j))
    return pl.pallas_call(
        body, name="convffn_fwd", grid=(N_SLAB,),
        in_specs=[slab(0), slab(N_SLAB), wsl(0), wsl(N_SLAB), bsl(0), bsl(N_SLAB)],
        out_specs=slab(0), out_shape=jax.ShapeDtypeStruct((s, D_FF), BF16),
        compiler_params=_arb(1),
    )(u, u, conv_w, conv_w, conv_b, conv_b)


def _convffn_bwd(u, conv_w, conv_b, df):
    s = u.shape[0]
    nchunks = s // CONV_CHUNK

    def body(ug_ref, uv_ref, wg_ref, wv_ref, bg_ref, bv_ref, df_ref, du_ref, dw_ref, db_ref, dcg_ref, dcv_ref):
        dwg = [jnp.zeros((1, LANES), F32) for _ in range(3)]
        dwv = [jnp.zeros((1, LANES), F32) for _ in range(3)]
        dbg = jnp.zeros((1, LANES), F32)
        dbv = jnp.zeros((1, LANES), F32)
        for c in range(nchunks):
            rows = slice(c * CONV_CHUNK, (c + 1) * CONV_CHUNK)
            cg, gp, gc, gn = _conv_taps(ug_ref, wg_ref, bg_ref, c, nchunks)
            cv, vp, vc, vn = _conv_taps(uv_ref, wv_ref, bv_ref, c, nchunks)
            dfv = df_ref[rows, :]
            sg = jax.nn.sigmoid(cg)
            dcg = dfv * cv * (sg * (1.0 + cg * (1.0 - sg)))
            dcv = dfv * (cg * sg)
            dcg_ref[rows, :] = dcg
            dcv_ref[rows, :] = dcv
            for t, (tg, tv) in enumerate(((gp, vp), (gc, vc), (gn, vn))):
                dwg[t] = dwg[t] + jnp.sum(tg * dcg, axis=0, keepdims=True)
                dwv[t] = dwv[t] + jnp.sum(tv * dcv, axis=0, keepdims=True)
            dbg = dbg + jnp.sum(dcg, axis=0, keepdims=True)
            dbv = dbv + jnp.sum(dcv, axis=0, keepdims=True)
        for t in range(3):
            dw_ref[0, t:t + 1, :] = dwg[t]
            dw_ref[1, t:t + 1, :] = dwv[t]
        db_ref[0] = dbg
        db_ref[1] = dbv
        for half, (dc_ref, w_ref) in enumerate(((dcg_ref, wg_ref), (dcv_ref, wv_ref))):
            for c in range(nchunks):
                prev, cur, nxt = _shifted(dc_ref, c, nchunks)
                du = nxt * w_ref[0:1, :] + cur * w_ref[1:2, :] + prev * w_ref[2:3, :]
                du_ref[half, c * CONV_CHUNK:(c + 1) * CONV_CHUNK, :] = du.astype(BF16)

    slab = lambda off: _bs((s, LANES), lambda j: (0, off + j))
    wsl = lambda off: _bs((3, LANES), lambda j: (0, off + j))
    bsl = lambda off: _bs((1, LANES), lambda j: (0, off + j))
    return pl.pallas_call(
        body, name="convffn_bwd", grid=(N_SLAB,),
        in_specs=[slab(0), slab(N_SLAB), wsl(0), wsl(N_SLAB), bsl(0), bsl(N_SLAB), slab(0)],
        out_specs=[_bs((2, s, LANES), lambda j: (0, 0, j)), _bs((2, 3, LANES), lambda j: (0, 0, j)),
                   _bs((2, 1, LANES), lambda j: (0, 0, j))],
        out_shape=[jax.ShapeDtypeStruct((2, s, D_FF), BF16), jax.ShapeDtypeStruct((2, 3, D_FF), F32),
                   jax.ShapeDtypeStruct((2, 1, D_FF), F32)],
        scratch_shapes=[pltpu.VMEM((s, LANES), F32), pltpu.VMEM((s, LANES), F32)],
        compiler_params=_arb(1),
    )(u, u, conv_w, conv_w, conv_b, conv_b, df)


def _row_tile(rows, limit=512):
    best = rows
    for t in range(8, min(rows, limit) + 1, 8):
        if rows % t == 0:
            best = t
    return best if rows % 8 == 0 else rows


ADAM_C1 = 1.0 - ADAM_B1 ** ADAM_STEP
ADAM_C2 = 1.0 - ADAM_B2 ** ADAM_STEP


def _adamw_math(w, gv, m, v):
    nm = ADAM_B1 * m + (1.0 - ADAM_B1) * gv
    nv = ADAM_B2 * v + (1.0 - ADAM_B2) * (gv * gv)
    m_hat = nm / ADAM_C1
    v_hat = nv / ADAM_C2
    return -ADAM_LR * (m_hat / (jnp.sqrt(v_hat) + ADAM_EPS) + ADAM_WD * w), nm, nv


def _adamw_halves(name, core, w, mine, theirs, m, v, transposed=False):
    half, cols = (mine.shape[1], mine.shape[0]) if transposed else mine.shape
    tr = _row_tile(cols if transposed else half)
    nr = (cols if transposed else half) // tr

    def body(core_ref, w_ref, mine_ref, theirs_ref, m_ref, v_ref, g_ref, d_ref, nm_ref, nv_ref):
        gv = jnp.where(pl.program_id(0) == core_ref[0], mine_ref[...], theirs_ref[...])
        g_ref[...] = gv
        d_ref[...], nm_ref[...], nv_ref[...] = _adamw_math(w_ref[...], gv, m_ref[...], v_ref[...])

    if transposed:
        full = pl.BlockSpec((tr, half), lambda hf, r, cr: (r, hf))
        part = pl.BlockSpec((tr, half), lambda hf, r, cr: (r, 0))
        shape = (cols, 2 * half)
    else:
        full = pl.BlockSpec((tr, cols), lambda hf, r, cr: (hf * nr + r, 0))
        part = pl.BlockSpec((tr, cols), lambda hf, r, cr: (r, 0))
        shape = (2 * half, cols)
    return pl.pallas_call(
        body, name=name,
        grid_spec=pltpu.PrefetchScalarGridSpec(num_scalar_prefetch=1, grid=(2, nr),
                                               in_specs=[full, part, part, full, full], out_specs=[full] * 4),
        out_shape=[jax.ShapeDtypeStruct(shape, F32)] * 4, compiler_params=_arb(2),
    )(core, w, mine, theirs, m, v)


ANY = pl.BlockSpec(memory_space=pl.ANY)


def _mesh_pos():
    return lax.axis_index("x"), lax.axis_index("y"), lax.axis_index("c")


def _other_chips(x, y):
    return [(1 - x, y), (x, 1 - y), (1 - x, 1 - y)]


def _forward_to_sibling(gathered):
    n = len(gathered)

    def body(*refs):
        in_refs, out_refs = refs[:n], refs[n:2 * n]
        send_sems, recv_sems = refs[2 * n:]
        x, y, c = _mesh_pos()
        cps = []
        for i in range(n):
            for k, chip in enumerate(_other_chips(x, y)):
                pk = 2 * chip[0] + chip[1]
                sems = dict(send_sem=send_sems.at[3 * i + k], recv_sem=recv_sems.at[3 * i + k],
                            device_id=(x, y, 1 - c), device_id_type=MESH)
                sent = pltpu.make_async_remote_copy(src_ref=in_refs[i].at[pk, c], dst_ref=out_refs[i].at[pk, c], **sems)
                sent.start()
                theirs = out_refs[i].at[pk, 1 - c]
                cps.append((sent, pltpu.make_async_remote_copy(src_ref=theirs, dst_ref=theirs, **sems)))
        for sent, arrived in cps:
            sent.wait_send()
            arrived.wait_recv()

    return pl.pallas_call(
        body, name="forward_to_sibling", in_specs=[ANY] * n, out_specs=[ANY] * n,
        out_shape=[jax.ShapeDtypeStruct(a.shape, a.dtype) for a in gathered],
        input_output_aliases={i: i for i in range(n)},
        scratch_shapes=[pltpu.SemaphoreType.DMA((3 * n,)), pltpu.SemaphoreType.DMA((3 * n,))],
    )(*gathered)


def _rs_pair_exchange(name, grads):
    n = len(grads)

    def body(*refs):
        g_refs, o_refs = refs[:n], refs[n:2 * n]
        send_sems, recv_sems = refs[2 * n:]
        x, y, c = _mesh_pos()
        cps = []
        for i in range(n):
            cp = pltpu.make_async_remote_copy(
                src_ref=g_refs[i].at[:, 1 - c], dst_ref=o_refs[i],
                send_sem=send_sems.at[i], recv_sem=recv_sems.at[i], device_id=(x, y, 1 - c), device_id_type=MESH)
            cp.start()
            cps.append(cp)
        for cp in cps:
            cp.wait()

    return pl.pallas_call(
        body, name=name, in_specs=[ANY] * n, out_specs=[ANY] * n,
        out_shape=[jax.ShapeDtypeStruct((4,) + g.shape[2:], F32) for g in grads],
        scratch_shapes=[pltpu.SemaphoreType.DMA((n,)), pltpu.SemaphoreType.DMA((n,))],
    )(*grads)


def _rs_pair_add(name, core, g, recv):
    _, half, cols = recv.shape
    tr = _row_tile(half)
    nr = half // tr

    def body(core_ref, g_ref, r_ref, o_ref):
        o_ref[...] = (g_ref[...] + r_ref[...]).astype(BF16)

    return pl.pallas_call(
        body, name=name,
        grid_spec=pltpu.PrefetchScalarGridSpec(
            num_scalar_prefetch=1, grid=(4, nr),
            in_specs=[pl.BlockSpec((None, None, tr, cols), lambda q, r, cr: (q, cr[0], r, 0)),
                      pl.BlockSpec((None, tr, cols), lambda q, r, cr: (q, r, 0))],
            out_specs=pl.BlockSpec((None, tr, cols), lambda q, r, cr: (q, r, 0))),
        out_shape=jax.ShapeDtypeStruct((4, half, cols), BF16),
        compiler_params=_arb(2),
    )(core, g, recv)


def _rs_final_add(name, chip, pair, recv):
    _, half, cols = pair.shape
    tr = _row_tile(half)

    def body(chip_ref, p_ref, r_ref, o_ref):
        o_ref[...] = ((p_ref[...].astype(F32) + r_ref[0].astype(F32)) + r_ref[1].astype(F32)) + r_ref[2].astype(F32)

    return pl.pallas_call(
        body, name=name,
        grid_spec=pltpu.PrefetchScalarGridSpec(
            num_scalar_prefetch=1, grid=(half // tr,),
            in_specs=[pl.BlockSpec((None, tr, cols), lambda r, ch: (ch[0], r, 0)),
                      pl.BlockSpec((3, tr, cols), lambda r, ch: (0, r, 0))],
            out_specs=pl.BlockSpec((tr, cols), lambda r, ch: (r, 0))),
        out_shape=jax.ShapeDtypeStruct((half, cols), F32),
        compiler_params=_arb(1),
    )(chip, pair, recv)


def _rs_pair_share(halves):
    n = len(halves)

    def body(*refs):
        h_refs, o_refs = refs[:n], refs[n:2 * n]
        send_sems, recv_sems = refs[2 * n:]
        x, y, c = _mesh_pos()
        cps = []
        for i in range(n):
            cp = pltpu.make_async_remote_copy(src_ref=h_refs[i], dst_ref=o_refs[i], send_sem=send_sems.at[i],
                                              recv_sem=recv_sems.at[i], device_id=(x, y, 1 - c), device_id_type=MESH)
            cp.start()
            cps.append(cp)
        for cp in cps:
            cp.wait()

    return pl.pallas_call(
        body, name="rs_pair_share", in_specs=[ANY] * n, out_specs=[ANY] * n,
        out_shape=[jax.ShapeDtypeStruct(h.shape, F32) for h in halves],
        scratch_shapes=[pltpu.SemaphoreType.DMA((n,)), pltpu.SemaphoreType.DMA((n,))],
    )(*halves)


HBM = pl.BlockSpec(memory_space=pltpu.HBM)
SEM = pl.BlockSpec(memory_space=pltpu.SEMAPHORE)


class _SplitExchange:
    def __init__(self, name, srcs, land_shapes, src_of, dst_of, arrive_of, to_sibling=False):
        self.name, self.srcs, self.land_shapes = name, list(srcs), list(land_shapes)
        self.src_of, self.dst_of, self.arrive_of = src_of, dst_of, arrive_of
        self.to_sibling = to_sibling
        self.fan = 1 if to_sibling else 3

    def _copies(self, src_refs, land_refs, send_sems, recv_sems):
        x, y, c = _mesh_pos()
        p = 2 * x + y
        if self.to_sibling:
            peers = [((x, y, 1 - c), 1 - c)]
        else:
            peers = [((*chip, c), 2 * chip[0] + chip[1]) for chip in _other_chips(x, y)]
        out = []
        for i, (src, land) in enumerate(zip(src_refs, land_refs)):
            for k, (peer, pk) in enumerate(peers):
                sems = dict(send_sem=send_sems.at[self.fan * i + k], recv_sem=recv_sems.at[self.fan * i + k],
                            device_id=peer, device_id_type=MESH)
                sent = pltpu.make_async_remote_copy(src_ref=self.src_of(src, k, p, pk, c),
                                                    dst_ref=self.dst_of(land, k, p, pk, c), **sems)
                here = self.arrive_of(land, k, p, pk, c)
                out.append((sent, pltpu.make_async_remote_copy(src_ref=here, dst_ref=here, **sems)))
        return out

    def start(self, after=None):
        n = len(self.srcs)
        n_in = 2 * n + (after is not None)

        def body(*refs):
            for sent, _ in self._copies(refs[:n], refs[n:2 * n], refs[n_in], refs[n_in + 1]):
                sent.start()
            refs[-1][...] = jnp.zeros((8, LANES), F32)

        lands = [lax.empty(shape, src.dtype) for shape, src in zip(self.land_shapes, self.srcs)]
        operands = [pltpu.with_memory_space_constraint(a, pltpu.HBM) for a in self.srcs + lands]
        outs = pl.pallas_call(
            body, name=self.name + "_start",
            out_shape=(pltpu.SemaphoreType.DMA((self.fan * n,)), pltpu.SemaphoreType.DMA((self.fan * n,)),
                       *[pltpu.HBM(a.shape, a.dtype) for a in operands], jax.ShapeDtypeStruct((8, LANES), F32)),
            in_specs=[HBM] * (2 * n) + [ANY] * (after is not None),
            out_specs=(SEM, SEM, *[HBM] * (2 * n), pl.BlockSpec(memory_space=pltpu.VMEM)),
            input_output_aliases={j: 2 + j for j in range(2 * n)},
            compiler_params=pltpu.CompilerParams(has_side_effects=pltpu.SideEffectType.DATAFLOW_SIDE_EFFECTING),
        )(*operands, *([after] if after is not None else []))
        self._sems, self._thru = outs[:2], list(outs[2:2 + 2 * n])
        return outs[-1]

    def wait(self, after):
        n = len(self.srcs)

        def body(*refs):
            for sent, arrived in self._copies(refs[:n], refs[n:2 * n], refs[2 * n], refs[2 * n + 1]):
                sent.wait_send()
                arrived.wait_recv()

        after = list(after) if isinstance(after, (list, tuple)) else [after]
        outs = pl.pallas_call(
            body, name=self.name + "_wait",
            out_shape=tuple(pltpu.HBM(a.shape, a.dtype) for a in self._thru),
            in_specs=[HBM] * (2 * n) + [SEM, SEM] + [ANY] * len(after), out_specs=tuple([HBM] * (2 * n)),
            input_output_aliases={j: j for j in range(2 * n)},
            compiler_params=pltpu.CompilerParams(has_side_effects=pltpu.SideEffectType.DATAFLOW_SIDE_EFFECTING),
        )(*self._thru, *self._sems, *after)
        return list(outs[:n]), list(outs[n:])


def _small_allreduce(parts):
    n = len(parts)

    def body(*refs):
        in_refs, out_refs, gather_refs = refs[:n], refs[n:2 * n], refs[2 * n:3 * n]
        send_sems, recv_sems = refs[3 * n:]
        x, y, c = _mesh_pos()
        me = 4 * x + 2 * y + c
        cps = []
        for i in range(n):
            gather_refs[i][me] = in_refs[i][...]
            for j in range(1, 8):
                peer = (x ^ (j >> 2), y ^ ((j >> 1) & 1), c ^ (j & 1))
                cp = pltpu.make_async_remote_copy(
                    src_ref=in_refs[i], dst_ref=gather_refs[i].at[me], send_sem=send_sems.at[7 * i + j - 1],
                    recv_sem=recv_sems.at[7 * i + j - 1], device_id=peer, device_id_type=MESH)
                cp.start()
                cps.append(cp)
        for i in range(n):
            for j in range(1, 8):
                peer_id = 4 * (x ^ (j >> 2)) + 2 * (y ^ ((j >> 1) & 1)) + (c ^ (j & 1))
                slot = gather_refs[i].at[peer_id]
                pltpu.make_async_remote_copy(src_ref=slot, dst_ref=slot, send_sem=send_sems.at[7 * i + j - 1],
                                             recv_sem=recv_sems.at[7 * i + j - 1], device_id=(x, y, c),
                                             device_id_type=MESH).wait_recv()
        for cp in cps:
            cp.wait_send()
        for i in range(n):
            tot = gather_refs[i][0]
            for d in range(1, 8):
                tot = tot + gather_refs[i][d]
            out_refs[i][...] = tot

    vmem = pl.BlockSpec(memory_space=pltpu.VMEM)
    return pl.pallas_call(
        body, name="small_allreduce", in_specs=[vmem] * n, out_specs=[vmem] * n,
        out_shape=[jax.ShapeDtypeStruct(p.shape, F32) for p in parts],
        scratch_shapes=[pltpu.VMEM((8,) + p.shape, F32) for p in parts]
        + [pltpu.SemaphoreType.DMA((7 * n,)), pltpu.SemaphoreType.DMA((7 * n,))],
    )(*parts)


def _adamw_small(ws, gs, ms, vs):
    n = len(ws)

    def body(*refs):
        for i in range(n):
            w_ref, g_ref, m_ref, v_ref = refs[i], refs[n + i], refs[2 * n + i], refs[3 * n + i]
            d_ref, nm_ref, nv_ref = refs[4 * n + i], refs[5 * n + i], refs[6 * n + i]
            d_ref[...], nm_ref[...], nv_ref[...] = _adamw_math(w_ref[...], g_ref[...], m_ref[...], v_ref[...])

    vmem = pl.BlockSpec(memory_space=pltpu.VMEM)
    shapes = [jax.ShapeDtypeStruct(w.shape, F32) for w in ws]
    outs = pl.pallas_call(body, name="adamw_small", in_specs=[vmem] * (4 * n), out_specs=[vmem] * (3 * n),
                          out_shape=shapes * 3)(*ws, *gs, *ms, *vs)
    return outs[:n], outs[n:2 * n], outs[2 * n:]


W_IN_PIECES = ((0, 256, P_QLAT), (256, 384, P_CKV), (384, 448, P_KR), (448, 1472, P_QB), (1472, 1728, P_KB),
               (1728, 1984, P_VB), (1984, 3008, P_GA), (3008, 4032, P_GB))
W_IN_SHARD = 1008


def _w_in_from_shards(shards):
    cols = []
    for lo, hi, _ in sorted(W_IN_PIECES, key=lambda piece: piece[2]):
        for q in range(4):
            a, b = max(lo, q * W_IN_SHARD), min(hi, (q + 1) * W_IN_SHARD)
            if a < b:
                cols.append(shards[q][:, a - q * W_IN_SHARD:b - q * W_IN_SHARD])
    cols.append(jnp.zeros((shards.shape[1], W_IN_PAD - 4 * W_IN_SHARD), shards.dtype))
    return jnp.concatenate(cols, axis=1)


def _w_in_to_shards(p):
    shards = []
    for q in range(4):
        cols = []
        for lo, hi, at in W_IN_PIECES:
            a, b = max(lo, q * W_IN_SHARD), min(hi, (q + 1) * W_IN_SHARD)
            if a < b:
                cols.append(p[:, at + a - lo:at + b - lo])
        shards.append(jnp.concatenate(cols, axis=1))
    return jnp.stack(shards)


def _col_shards(w):
    r, c4 = w.shape
    return w.reshape(r, 4, c4 // 4).transpose(1, 0, 2)


def _local_step(x, positions, target, norm1_g, first_weights, q_a_norm_g, kv_a_norm_g, rel_bias, sinks,
                late_weights, norm2_g, conv_b, final_norm_g, early_grads=None, last_grads=None):
    s = x.shape[0]
    half = QK_ROPE // 2
    inv_freq = jnp.asarray(np.float32(ROPE_THETA) ** (-np.arange(half, dtype=np.float32) / np.float32(half)))
    ang = positions.astype(F32)[:, None] * inv_freq[None, :]
    cos, sin = jnp.cos(ang), jnp.sin(ang)
    z64 = jnp.zeros((s, 64), F32)
    cos_t = jnp.concatenate([cos, cos, z64], axis=1)
    sin_t = jnp.concatenate([-sin, sin, z64], axis=1)
    bucket = _t5_bucket_table()
    sinks1 = sinks.reshape(H_B)

    h1, rstd1 = _rmsnorm_fwd("norm1_fwd", x, norm1_g, D_MODEL, 0)
    bias = _win_bias(bucket, rel_bias)
    w_in_p, wq, wkv = first_weights([h1, bias, cos_t, sin_t])
    proj, proj_b = _matmul("proj", h1, w_in_p, out_shape=(s, W_IN_PAD), out_dtype=F32, grid=(s // MM_ROWS, W_IN_PAD // 1024, 1),
                           a_spec=_bs((MM_ROWS, D_MODEL), lambda i, j, k: (i, 0)), b_spec=_bs((D_MODEL, 1024), lambda i, j, k: (0, j)),
                           o_spec=_bs((MM_ROWS, 1024), lambda i, j, k: (i, j)), contract=NN, bf16_copy=True)
    qn, cn, rstd_q, rstd_c = _lat_norms(proj, q_a_norm_g, kv_a_norm_g)
    q = _q_heads(qn, wq, cos_t, sin_t)
    k, v = _kv_heads(cn, wkv, proj, cos_t, sin_t)
    o_a, lse_a = _mla_fwd(q, k, v)

    o_b, lse_b = _win_fwd(proj_b, bias, sinks1)

    w_out, w_up, w_down, conv_w = late_weights([o_a, o_b])
    row512 = lambda w: _bs((MM_ROWS, w), lambda i, j, k: (i, 0))
    whole = lambda r, c: _bs((r, c), lambda i, j, k: (0, 0))
    mixed, x1, h2, rstd2 = _mix_out_norm(proj, o_a, o_b, w_out, x, norm2_g)
    u = _matmul("ffn_up", h2, w_up, out_shape=(s, 2 * D_FF), out_dtype=F32, grid=(s // MM_ROWS, 4, 1),
                a_spec=_bs((MM_ROWS, D_MODEL), lambda i, j, k: (i, 0)), b_spec=_bs((D_MODEL, D_FF // 2), lambda i, j, k: (0, j)),
                o_spec=_bs((MM_ROWS, D_FF // 2), lambda i, j, k: (i, j)), contract=NN)
    f = _convffn_fwd(u, conv_w, conv_b)
    loss, dx2, d_final_g, dx2_b = _ffn_down_loss(f, w_down, x1, target, final_norm_g.reshape(1, D_MODEL))
    tk = min(s, DW_ROWS)

    df = _matmul("ffn_down_dx", dx2_b, w_down, out_shape=(s, D_FF), out_dtype=F32, grid=(s // MM_ROWS, 2, 1),
                 a_spec=row512(D_MODEL), b_spec=_bs((D_FF // 2, D_MODEL), lambda i, j, k: (j, 0)),
                 o_spec=_bs((MM_ROWS, D_FF // 2), lambda i, j, k: (i, j)), contract=NT)
    d_w_down = _matmul("ffn_down_dw", f, dx2_b, out_shape=(D_FF, D_MODEL), out_dtype=F32, grid=(2, 1, s // tk),
                       a_spec=_bs((tk, D_FF // 2), lambda i, j, k: (k, i)), b_spec=_bs((tk, D_MODEL), lambda i, j, k: (k, 0)),
                       o_spec=_bs((D_FF // 2, D_MODEL), lambda i, j, k: (i, 0)), contract=TN)
    du, d_conv_w2, d_conv_b2 = _convffn_bwd(u, conv_w, conv_b, df)
    kc = D_FF // 2
    dx1, d_norm2_g, dx1_b = _matmul_norm_bwd(
        "ffn_up_dx_norm2_bwd", du, w_up, grid=(s // MM_ROWS, 4),
        a_spec=_bs((None, MM_ROWS, kc), lambda i, k: (k // 2, i, k % 2)), b_spec=_bs((D_MODEL, kc), lambda i, k: (0, k)),
        x=x1, rstd=rstd2, g=norm2_g, res=dx2, bf16_copy=True)
    d_w_up = _matmul("ffn_up_dw", h2, du, out_shape=(D_MODEL, 2 * D_FF), out_dtype=F32, grid=(1, 4, s // tk),
                     a_spec=_bs((tk, D_MODEL), lambda i, j, k: (k, 0)),
                     b_spec=_bs((None, tk, kc), lambda i, j, k: (j // 2, k, j % 2)),
                     o_spec=_bs((D_MODEL, kc), lambda i, j, k: (0, j)), contract=TN)

    d_w_out = _matmul("attn_out_dw", mixed, dx1_b, out_shape=(D_MODEL, D_MODEL), out_dtype=F32, grid=(1, 1, s // tk),
                      a_spec=_bs((tk, D_MODEL), lambda i, j, k: (k, 0)), b_spec=_bs((tk, D_MODEL), lambda i, j, k: (k, 0)),
                      o_spec=whole(D_MODEL, D_MODEL), contract=TN)
    token, early_grads_on = early_grads(d_w_out, d_w_up, d_w_down) if early_grads is not None else (None, None)
    do_a, do_b, d_ga, d_gb = _mix_out_bwd(dx1_b, w_out, proj, o_a, o_b, after=token)
    if early_grads_on is not None:
        sinks1 = sinks1 + early_grads_on(d_ga)[0, :H_B]

    d_qb, dk_pad, dv_pad, dbias, dsink_rows = _win_bwd(proj_b, bias, sinks1, do_b, lse_b)
    wp = _win_param_grads(bucket, dbias, dsink_rows)[:, 0, :]
    d_rel_bias = wp[:, :NUM_BUCKETS].T
    d_sinks = wp[:, NUM_BUCKETS].reshape(1, H_B)
    d_kb = dk_pad[WINDOW:WINDOW + s].astype(BF16)
    d_vb = dv_pad[WINDOW:WINDOW + s].astype(BF16)

    dq_pre, dkv_pre, dkr = _mla_bwd(q, k, v, do_a, o_a, lse_a, cos_t, sin_t)
    d_kr = _mla_key_rope_grad(dkr, cos_t, sin_t)
    th = min(s, HEAD_ROWS)
    hgrid = (s // th, 1, H_A)
    hblock = _bs((None, th, HEAD_PAD), lambda i, j, k: (k, i, 0))
    hrows = lambda w: _bs((th, w), lambda i, j, k: (i, 0))
    dqn = _matmul("q_up_dx", dq_pre, wq, out_shape=(s, Q_LORA), out_dtype=F32, grid=hgrid, a_spec=hblock,
                  b_spec=_bs((None, Q_LORA, HEAD_PAD), lambda i, j, k: (k, 0, 0)), o_spec=hrows(Q_LORA), contract=NT)
    dcn = _matmul("kv_up_dx", dkv_pre, wkv, out_shape=(s, KV_LORA), out_dtype=F32, grid=hgrid, a_spec=hblock,
                  b_spec=_bs((None, KV_LORA, HEAD_PAD), lambda i, j, k: (k, 0, 0)), o_spec=hrows(KV_LORA), contract=NT)
    d_qlat, d_gq = _rmsnorm_bwd("q_norm_bwd", dqn, proj, rstd_q, q_a_norm_g, Q_LORA, P_QLAT // Q_LORA, BF16)
    d_ckv, d_gkv = _rmsnorm_bwd("kv_norm_bwd", dcn, proj, rstd_c, kv_a_norm_g, KV_LORA, P_CKV // KV_LORA, BF16)

    dproj = jnp.concatenate([d_qb, d_ga, d_gb, d_qlat, d_kb, d_vb, d_ckv, d_kr], axis=1)
    d_w_in_p = _matmul("proj_dw", h1, dproj, out_shape=(D_MODEL, W_IN_PAD), out_dtype=F32, grid=(1, W_IN_PAD // 1024, s // tk),
                       a_spec=_bs((tk, D_MODEL), lambda i, j, k: (k, 0)), b_spec=_bs((tk, 1024), lambda i, j, k: (k, j)),
                       o_spec=_bs((D_MODEL, 1024), lambda i, j, k: (0, j)), contract=TN)
    token, last_grads_on = last_grads(d_w_in_p) if last_grads is not None else (None, None)
    wgrid = (H_A, 1, s // th)
    d_wq = _matmul("q_up_dw", qn, dq_pre, out_shape=(H_A, Q_LORA, HEAD_PAD), out_dtype=F32, grid=wgrid,
                   a_spec=_bs((th, Q_LORA), lambda i, j, k: (k, 0)), b_spec=_bs((None, th, HEAD_PAD), lambda i, j, k: (i, k, 0)),
                   o_spec=_bs((None, Q_LORA, HEAD_PAD), lambda i, j, k: (i, 0, 0)), contract=TN, after=token)
    d_wkv = _matmul("kv_up_dw", cn, dkv_pre, out_shape=(H_A, KV_LORA, HEAD_PAD), out_dtype=F32, grid=wgrid,
                    a_spec=_bs((th, KV_LORA), lambda i, j, k: (k, 0)), b_spec=_bs((None, th, HEAD_PAD), lambda i, j, k: (i, k, 0)),
                    o_spec=_bs((None, KV_LORA, HEAD_PAD), lambda i, j, k: (i, 0, 0)), contract=TN, after=token)
    if last_grads_on is not None:
        token = last_grads_on(d_wq, d_wkv)
    dx, d_norm1_g = _matmul_norm_bwd(
        "proj_dx_norm1_bwd", dproj, w_in_p, grid=(s // MM_ROWS, W_IN_PAD // 1024),
        a_spec=_bs((MM_ROWS, 1024), lambda i, k: (i, k)), b_spec=_bs((D_MODEL, 1024), lambda i, k: (0, k)),
        x=x, rstd=rstd1, g=norm1_g, res=dx1, after=token)

    grads = dict(
        norm1_g=d_norm1_g, w_in_p=d_w_in_p, q_a_norm_g=d_gq, wq=d_wq, kv_a_norm_g=d_gkv, wkv=d_wkv,
        rel_bias=d_rel_bias, sinks=d_sinks, w_out=d_w_out, norm2_g=d_norm2_g, w_up=d_w_up,
        conv_w=jnp.concatenate([d_conv_w2[0], d_conv_w2[1]], axis=1),
        conv_b=jnp.concatenate([d_conv_b2[0], d_conv_b2[1]], axis=1),
        w_down=d_w_down, final_norm_g=d_final_g.reshape(D_MODEL))
    return loss, dx, grads


HEADS_PER_SHARD = H_A // 4


def _head_cols(h, width):
    return slice((h % HEADS_PER_SHARD) * width, (h % HEADS_PER_SHARD + 1) * width)


def _wq_heads(shards):
    per = QK_NOPE + QK_ROPE
    w = jnp.stack([shards[h // HEADS_PER_SHARD][:, _head_cols(h, per)] for h in range(H_A)])
    return jnp.pad(w, ((0, 0), (0, 0), (0, HEAD_PAD - per)))


def _wq_shards(d_wq):
    per = QK_NOPE + QK_ROPE
    return jnp.stack([jnp.concatenate([d_wq[h][:, :per] for h in range(q * HEADS_PER_SHARD, (q + 1) * HEADS_PER_SHARD)],
                                      axis=1) for q in range(4)])


def _wkv_heads(shards):
    return jnp.stack([shards[h // HEADS_PER_SHARD][:, _head_cols(h, QK_NOPE + V_DIM)] for h in range(H_A)])


def _wkv_shards(d_wkv):
    return jnp.stack([jnp.concatenate([d_wkv[h] for h in range(q * HEADS_PER_SHARD, (q + 1) * HEADS_PER_SHARD)], axis=1)
                      for q in range(4)])


SMALL = ("norm1_g", "q_a_norm_g", "kv_a_norm_g", "rel_bias", "sinks", "norm2_g", "conv_b", "final_norm_g")
FIRST = ("w_in", "w_q_b", "w_kv_b")
LATER = ("w_out", "w_up", "w_down")
BIG = FIRST + LATER


def kernel(x, positions, norm1_g, w_in, q_a_norm_g, w_q_b, kv_a_norm_g, w_kv_b, rel_bias, sinks, w_out, norm2_g, w_up, conv_w, conv_b, w_down, final_norm_g, loss_target, m_norm1_g, m_w_in, m_q_a_norm_g, m_w_q_b, m_kv_a_norm_g, m_w_kv_b, m_rel_bias, m_sinks, m_w_out, m_norm2_g, m_w_up, m_conv_w, m_conv_b, m_w_down, m_final_norm_g, v_norm1_g, v_w_in, v_q_a_norm_g, v_w_q_b, v_kv_a_norm_g, v_w_kv_b, v_rel_bias, v_sinks, v_w_out, v_norm2_g, v_w_up, v_conv_w, v_conv_b, v_w_down, v_final_norm_g):
    weights = dict(norm1_g=norm1_g, w_in=w_in, q_a_norm_g=q_a_norm_g, w_q_b=w_q_b, kv_a_norm_g=kv_a_norm_g,
                   w_kv_b=w_kv_b, rel_bias=rel_bias, sinks=sinks, w_out=w_out, norm2_g=norm2_g, w_up=w_up,
                   conv_w=conv_w, conv_b=conv_b, w_down=w_down, final_norm_g=final_norm_g)
    mom_m = dict(norm1_g=m_norm1_g, w_in=m_w_in, q_a_norm_g=m_q_a_norm_g, w_q_b=m_w_q_b, kv_a_norm_g=m_kv_a_norm_g,
                 w_kv_b=m_w_kv_b, rel_bias=m_rel_bias, sinks=m_sinks, w_out=m_w_out, norm2_g=m_norm2_g, w_up=m_w_up,
                 conv_w=m_conv_w, conv_b=m_conv_b, w_down=m_w_down, final_norm_g=m_final_norm_g)
    mom_v = dict(norm1_g=v_norm1_g, w_in=v_w_in, q_a_norm_g=v_q_a_norm_g, w_q_b=v_w_q_b, kv_a_norm_g=v_kv_a_norm_g,
                 w_kv_b=v_w_kv_b, rel_bias=v_rel_bias, sinks=v_sinks, w_out=v_w_out, norm2_g=v_norm2_g, w_up=v_w_up,
                 conv_w=v_conv_w, conv_b=v_conv_b, w_down=v_w_down, final_norm_g=v_final_norm_g)
    shard2d = {n: weights[n][0] for n in BIG}
    conv_w_shard = conv_w[0]
    xi, yi, ci = lax.axis_index("x"), lax.axis_index("y"), lax.axis_index("c")
    chip = (2 * xi + yi).astype(jnp.int32)

    core = ci.astype(jnp.int32).reshape(1)
    chip1 = chip.reshape(1)
    cat_cols = lambda a: jnp.concatenate([a[0], a[1], a[2], a[3]], axis=1)
    own_slot = lambda a, own: lax.dynamic_update_index_in_dim(a, own, chip, 0)
    halved = lambda a: a.reshape((2, a.shape[0] // 2) + a.shape[1:])
    quartered = lambda a: a.reshape(4, 2, a.shape[1] // 2, a.shape[2])

    first = [halved(shard2d[n].astype(BF16)) for n in FIRST]
    gather1 = _SplitExchange("gather_first", first, [(4,) + a.shape for a in first],
                             src_of=lambda ref, k, p, pk, c: ref.at[c], dst_of=lambda ref, k, p, pk, c: ref.at[p, c],
                             arrive_of=lambda ref, k, p, pk, c: ref.at[pk, c])
    token1 = gather1.start()

    def first_weights(after):
        own, landed = gather1.wait(after)
        gathered = [own_slot(a, mine) for a, mine in zip(_forward_to_sibling(landed), own)]
        g = {n: a.reshape((4,) + shard2d[n].shape) for n, a in zip(FIRST, gathered)}
        return _w_in_from_shards(g["w_in"]), _wq_heads(g["w_q_b"]), _wkv_heads(g["w_kv_b"])

    later = [shard2d[n].astype(BF16) for n in LATER] + [conv_w_shard]
    gather2 = _SplitExchange("gather_later", later, [(4,) + a.shape for a in later],
                             src_of=lambda ref, k, p, pk, c: ref, dst_of=lambda ref, k, p, pk, c: ref.at[p],
                             arrive_of=lambda ref, k, p, pk, c: ref.at[pk])
    norm1_g_in = norm1_g + gather2.start(after=token1)[:1, :1]

    def late_weights(after):
        w_out_g, w_up_g, w_down_g, conv_w_g = [own_slot(a, mine) for mine, a in zip(*gather2.wait(after))]
        return w_out_g.reshape(D_MODEL, D_MODEL), cat_cols(w_up_g), w_down_g.reshape(D_FF, D_MODEL), cat_cols(conv_w_g)

    early = {}

    def early_grads(d_w_out, d_w_up, d_w_down):
        grads = [quartered(d_w_out.reshape(4, D_MODEL // 4, D_MODEL)), quartered(_col_shards(d_w_up)),
                 quartered(d_w_down.reshape(4, D_FF // 4, D_MODEL))]
        swap = _SplitExchange("rs_pair_exchange_early", grads, [(4,) + a.shape[2:] for a in grads], to_sibling=True,
                              src_of=lambda ref, k, p, pk, c: ref.at[:, pk], dst_of=lambda ref, k, p, pk, c: ref,
                              arrive_of=lambda ref, k, p, pk, c: ref)

        def on(after):
            kept, recv = swap.wait(after)
            early["pairs"] = [_rs_pair_add(f"rs_pair_add_{n}", core, gfull, r) for n, gfull, r in zip(LATER, kept, recv)]
            early["ici"] = _SplitExchange("rs_ici_early", early["pairs"], [(3,) + a.shape[1:] for a in early["pairs"]],
                                          src_of=lambda ref, k, p, pk, c: ref.at[pk], dst_of=lambda ref, k, p, pk, c: ref.at[k],
                                          arrive_of=lambda ref, k, p, pk, c: ref.at[k])
            return early["ici"].start()

        return swap.start(), on

    last = {}

    def last_grads(d_w_in_p):
        big = [quartered(_w_in_to_shards(d_w_in_p))]
        swap = _SplitExchange("rs_pair_exchange_last_w_in", big, [(4,) + a.shape[2:] for a in big], to_sibling=True,
                              src_of=lambda ref, k, p, pk, c: ref.at[:, pk], dst_of=lambda ref, k, p, pk, c: ref,
                              arrive_of=lambda ref, k, p, pk, c: ref)

        def on(d_wq, d_wkv):
            small = [quartered(_wq_shards(d_wq)), quartered(_wkv_shards(d_wkv))]
            kept, recv = swap.wait(d_wkv)
            grads, recv = kept + small, recv + list(_rs_pair_exchange("rs_pair_exchange_last", small))
            last["pairs"] = [_rs_pair_add(f"rs_pair_add_{n}", core, gfull, r) for n, gfull, r in zip(FIRST, grads, recv)]
            last["ici"] = _SplitExchange("rs_ici_last", last["pairs"], [(3,) + a.shape[1:] for a in last["pairs"]],
                                         src_of=lambda ref, k, p, pk, c: ref.at[pk], dst_of=lambda ref, k, p, pk, c: ref.at[k],
                                         arrive_of=lambda ref, k, p, pk, c: ref.at[k])
            return last["ici"].start()

        return swap.start(), on

    loss, dx, gr = _local_step(x[0], positions, loss_target[0], norm1_g_in, first_weights, q_a_norm_g, kv_a_norm_g,
                               rel_bias, sinks, late_weights, norm2_g, conv_b, final_norm_g, early_grads, last_grads)

    last_pairs, last_recv = last["ici"].wait(dx)
    early_pairs, early_recv = early["ici"].wait(dx)
    pairs, recv2 = last_pairs + early_pairs, last_recv + early_recv
    halves = [_rs_final_add(f"rs_final_add_{n}", chip1, pr, r) for n, pr, r in zip(FIRST + LATER, pairs, recv2)]
    sibling_halves = _rs_pair_share(halves)

    as_rows = lambda a: a.reshape((-1, a.shape[-1]))
    summed = _small_allreduce([as_rows(gr[n]) for n in SMALL] + [gr["conv_w"], loss])
    small_g = dict(zip(SMALL, summed[:len(SMALL)]))
    conv_w_g = lax.dynamic_slice_in_dim(summed[len(SMALL)], chip * (2 * D_FF // 4), 2 * D_FF // 4, axis=1)
    loss_out = summed[-1].reshape(())

    out_g, out_d, out_m, out_v = {}, {}, {}, {}
    for n, mine, theirs in zip(FIRST + LATER, halves, sibling_halves):
        if shard2d[n].shape[1] % LANES:
            res = _adamw_halves(f"adamw_{n}", core, shard2d[n].T, mine.T, theirs.T, mom_m[n][0].T, mom_v[n][0].T,
                                transposed=True)
            gsh, d, nm, nv = [a.T for a in res]
        else:
            gsh, d, nm, nv = _adamw_halves(f"adamw_{n}", core, shard2d[n], mine, theirs, mom_m[n][0], mom_v[n][0])
        out_g[n], out_d[n], out_m[n], out_v[n] = gsh[None], d[None], nm[None], nv[None]
    names = SMALL + ("conv_w",)
    sg = [small_g[n] for n in SMALL] + [conv_w_g]
    ds, nms, nvs = _adamw_small([as_rows(weights[n]) for n in names], sg, [as_rows(mom_m[n]) for n in names],
                                [as_rows(mom_v[n]) for n in names])
    for n, gg, dd, mm, vv in zip(names, sg, ds, nms, nvs):
        shp = weights[n].shape
        out_g[n], out_d[n], out_m[n], out_v[n] = gg.reshape(shp), dd.reshape(shp), mm.reshape(shp), vv.reshape(shp)

    order = ("norm1_g", "w_in", "q_a_norm_g", "w_q_b", "kv_a_norm_g", "w_kv_b", "rel_bias", "sinks", "w_out",
             "norm2_g", "w_up", "conv_w", "conv_b", "w_down", "final_norm_g")
    return (loss_out, dx[None], *[out_g[n] for n in order], *[out_d[n] for n in order],
            *[out_m[n] for n in order], *[out_v[n] for n in order])
```

```python
import functools
import math

import jax
import jax.numpy as jnp
import numpy as np
from jax import lax
from jax.experimental import pallas as pl
from jax.experimental.pallas import tpu as pltpu

F32 = jnp.float32
BF16 = jnp.bfloat16
MESH = pl.DeviceIdType.MESH

D_MODEL = 1024
EPS = 1e-6
H_A = 8
QK_NOPE = 128
QK_ROPE = 64
V_DIM = 128
Q_LORA = 256
KV_LORA = 128
ROPE_THETA = 10000.0
H_B = 16
KV_B = 4
GROUP = 4
HD_B = 64
WINDOW = 128
Q_BLOCK = 128
NUM_BUCKETS = 32
MAX_DISTANCE = 128
D_FF = 2816
HEAD_PAD = 256

ADAM_LR = 0.001
ADAM_B1 = 0.9
ADAM_B2 = 0.999
ADAM_EPS = 1e-08
ADAM_WD = 0.01
ADAM_STEP = 10

LANES = 128
P_QB, P_GA, P_GB, P_QLAT, P_KB, P_VB, P_CKV, P_KR = 0, 1024, 2048, 3072, 3328, 3584, 3840, 3968
W_IN_PAD = 4096

NT = (((1,), (1,)), ((), ()))
NN = (((1,), (0,)), ((), ()))
TN = (((0,), (0,)), ((), ()))


def _arb(n):
    return pltpu.CompilerParams(dimension_semantics=("arbitrary",) * n)


def _matmul(name, a, b, *, out_shape, out_dtype, grid, a_spec, b_spec, o_spec, contract, add=None, bf16_copy=False,
            after=None):
    nk = grid[2]
    acc_shape = tuple(d for d in o_spec.block_shape if d is not None)
    n_in = 2 + (add is not None) + (after is not None)
    n_out = 2 if bf16_copy else 1

    def body(*refs):
        a_ref, b_ref = refs[:2]
        add_ref = refs[2] if add is not None else None
        o_refs = refs[n_in:n_in + n_out]
        scratch = refs[n_in + n_out:]
        prod = lax.dot_general(a_ref[...].astype(BF16), b_ref[...].astype(BF16), contract,
                               preferred_element_type=F32)

        def finish(val):
            if add_ref is not None:
                val = add_ref[...] + val
            o_refs[0][...] = val.astype(out_dtype)
            if bf16_copy:
                o_refs[1][...] = val.astype(BF16)

        if nk == 1:
            finish(prod)
        else:
            acc_ref = scratch[0]
            k = pl.program_id(2)

            @pl.when(k == 0)
            def _():
                acc_ref[...] = prod

            @pl.when((k > 0) & (k < nk - 1))
            def _():
                acc_ref[...] += prod

            @pl.when(k == nk - 1)
            def _():
                finish(acc_ref[...] + prod)

    in_specs = [a_spec, b_spec]
    args = [a, b]
    if add is not None:
        in_specs.append(o_spec)
        args.append(add)
    if after is not None:
        in_specs.append(pl.BlockSpec(memory_space=pl.ANY))
        args.append(after)
    out_shapes = [jax.ShapeDtypeStruct(out_shape, out_dtype)]
    if bf16_copy:
        out_shapes.append(jax.ShapeDtypeStruct(out_shape, BF16))
    res = pl.pallas_call(
        body, name=name, grid=grid, in_specs=in_specs, out_specs=[o_spec] * n_out, out_shape=out_shapes,
        scratch_shapes=[pltpu.VMEM(acc_shape, F32)] if nk > 1 else [],
        compiler_params=_arb(3),
    )(*args)
    return res if bf16_copy else res[0]


def _bs(block, fn):
    return pl.BlockSpec(block, fn)


PROJ_COLS = 1024


def _input_projection(h1, w_in_p):
    s, d = h1.shape
    last = W_IN_PAD // PROJ_COLS - 1

    def body(a_ref, b_ref, f_ref, h_ref):
        j = pl.program_id(1)
        val = jnp.dot(a_ref[...], b_ref[...], preferred_element_type=F32)

        @pl.when(j >= 1)
        def _():
            f_ref[...] = val

        @pl.when((j == 0) | (j == last))
        def _():
            h_ref[...] = val.astype(BF16)

    return pl.pallas_call(
        body, name="proj", grid=(s // MM_ROWS, W_IN_PAD // PROJ_COLS),
        in_specs=[_bs((MM_ROWS, d), lambda i, j: (i, 0)), _bs((d, PROJ_COLS), lambda i, j: (0, j))],
        out_specs=[_bs((MM_ROWS, PROJ_COLS), lambda i, j: (i, jnp.maximum(j, 1))),
                   _bs((MM_ROWS, PROJ_COLS), lambda i, j: (i, jnp.where(j == last, last, 0)))],
        out_shape=[jax.ShapeDtypeStruct((s, W_IN_PAD), F32), jax.ShapeDtypeStruct((s, W_IN_PAD), BF16)],
        compiler_params=_arb(2),
    )(h1, w_in_p)


def _rmsnorm_fwd(name, src, g, d, cb, ts=512):
    s = src.shape[0]

    def body(x_ref, g_ref, h_ref, r_ref):
        x = x_ref[...]
        r = lax.rsqrt(jnp.mean(x * x, axis=-1, keepdims=True) + EPS)
        h_ref[...] = (x * r * g_ref[...]).astype(BF16)
        r_ref[...] = r

    return pl.pallas_call(
        body, name=name, grid=(s // ts,),
        in_specs=[_bs((ts, d), lambda i: (i, cb)), _bs((1, d), lambda i: (0, 0))],
        out_specs=[_bs((ts, d), lambda i: (i, 0)), _bs((ts, 1), lambda i: (i, 0))],
        out_shape=[jax.ShapeDtypeStruct((s, d), BF16), jax.ShapeDtypeStruct((s, 1), F32)],
        compiler_params=_arb(1),
    )(src, g)


def _rmsnorm_bwd(name, dy, src, rstd, g, d, cb, out_dtype, res=None, bf16_copy=False, ts=512):
    s = src.shape[0]

    def body(*refs):
        dy_ref, x_ref, r_ref, g_ref = refs[:4]
        res_ref = refs[4] if res is not None else None
        dx_ref, dg_ref = refs[n_in:n_in + 2]
        dyv = dy_ref[...]
        r = r_ref[...]
        xhat = x_ref[...] * r
        dyh = dyv * g_ref[...]
        c = jnp.mean(dyh * xhat, axis=-1, keepdims=True)
        dx = r * (dyh - xhat * c)
        if res_ref is not None:
            dx = res_ref[...] + dx
        dx_ref[...] = dx.astype(out_dtype)
        if bf16_copy:
            refs[n_in + 2][...] = dx.astype(BF16)
        part = jnp.sum(dyv * xhat, axis=0, keepdims=True)

        @pl.when(pl.program_id(0) == 0)
        def _():
            dg_ref[...] = part

        @pl.when(pl.program_id(0) > 0)
        def _():
            dg_ref[...] += part

    in_specs = [_bs((ts, d), lambda i: (i, 0)), _bs((ts, d), lambda i: (i, cb)),
                _bs((ts, 1), lambda i: (i, 0)), _bs((1, d), lambda i: (0, 0))]
    args = [dy, src, rstd, g]
    if res is not None:
        in_specs.append(_bs((ts, d), lambda i: (i, 0)))
        args.append(res)
    n_in = len(args)
    out_specs = [_bs((ts, d), lambda i: (i, 0)), _bs((1, d), lambda i: (0, 0))]
    out_shape = [jax.ShapeDtypeStruct((s, d), out_dtype), jax.ShapeDtypeStruct((1, d), F32)]
    if bf16_copy:
        out_specs.append(_bs((ts, d), lambda i: (i, 0)))
        out_shape.append(jax.ShapeDtypeStruct((s, d), BF16))
    return pl.pallas_call(
        body, name=name, grid=(s // ts,), in_specs=in_specs, out_specs=out_specs, out_shape=out_shape,
        compiler_params=_arb(1),
    )(*args)


def _matmul_norm_bwd(name, a, b, *, grid, a_spec, b_spec, x, rstd, g, res, bf16_copy=False, after=None):
    s, d = x.shape
    ni, nk = grid
    assert nk >= 2, "the first and the last contraction step are distinct branches"
    tm = s // ni
    n_in = 6 + (after is not None)

    def body(*refs):
        a_ref, b_ref, x_ref, r_ref, g_ref, res_ref = refs[:6]
        dx_ref, dg_ref = refs[n_in:n_in + 2]
        acc_ref = refs[-1]
        k = pl.program_id(1)
        prod = lax.dot_general(a_ref[...], b_ref[...], NT, preferred_element_type=F32)

        @pl.when(k == 0)
        def _():
            acc_ref[...] = prod

        @pl.when((k > 0) & (k < nk - 1))
        def _():
            acc_ref[...] += prod

        @pl.when(k == nk - 1)
        def _():
            dyv = acc_ref[...] + prod
            r = r_ref[...]
            xhat = x_ref[...] * r
            dyh = dyv * g_ref[...]
            c = jnp.mean(dyh * xhat, axis=-1, keepdims=True)
            dx = res_ref[...] + r * (dyh - xhat * c)
            dx_ref[...] = dx
            if bf16_copy:
                refs[n_in + 2][...] = dx.astype(BF16)
            part = jnp.sum(dyv * xhat, axis=0, keepdims=True)

            @pl.when(pl.program_id(0) == 0)
            def _():
                dg_ref[...] = part

            @pl.when(pl.program_id(0) > 0)
            def _():
                dg_ref[...] += part

    rows = _bs((tm, d), lambda i, k: (i, 0))
    in_specs = [a_spec, b_spec, rows, _bs((tm, 1), lambda i, k: (i, 0)), _bs((1, d), lambda i, k: (0, 0)), rows]
    args = [a, b, x, rstd, g, res]
    if after is not None:
        in_specs.append(pl.BlockSpec(memory_space=pl.ANY))
        args.append(after)
    out_specs = [rows, _bs((1, d), lambda i, k: (0, 0))]
    out_shape = [jax.ShapeDtypeStruct((s, d), F32), jax.ShapeDtypeStruct((1, d), F32)]
    if bf16_copy:
        out_specs.append(rows)
        out_shape.append(jax.ShapeDtypeStruct((s, d), BF16))
    return pl.pallas_call(
        body, name=name, grid=grid, in_specs=in_specs, out_specs=out_specs, out_shape=out_shape,
        scratch_shapes=[pltpu.VMEM((tm, d), F32)], compiler_params=_arb(2),
    )(*args)


def _ffn_down_loss(f, w_down, x1, target, g, ts=512):
    s, d = x1.shape
    dff = f.shape[1]

    def body(f_ref, w_ref, x_ref, t_ref, g_ref, loss_ref, dx_ref, dg_ref, dxb_ref):
        x = x_ref[...] + jnp.dot(f_ref[...], w_ref[...], preferred_element_type=F32)
        r = lax.rsqrt(jnp.mean(x * x, axis=-1, keepdims=True) + EPS)
        xhat = x * r
        gv = g_ref[...]
        err = xhat * gv - t_ref[...]
        lpart = 0.5 * jnp.sum(jnp.mean(err * err, axis=-1, keepdims=True), axis=0, keepdims=True)
        dyv = err * (1.0 / d)
        dyh = dyv * gv
        c = jnp.mean(dyh * xhat, axis=-1, keepdims=True)
        dx = r * (dyh - xhat * c)
        dx_ref[...] = dx
        dxb_ref[...] = dx.astype(BF16)
        gpart = jnp.sum(dyv * xhat, axis=0, keepdims=True)

        @pl.when(pl.program_id(0) == 0)
        def _():
            dg_ref[...] = gpart
            loss_ref[...] = lpart

        @pl.when(pl.program_id(0) > 0)
        def _():
            dg_ref[...] += gpart
            loss_ref[...] += lpart

    rows = _bs((ts, d), lambda i: (i, 0))
    return pl.pallas_call(
        body, name="ffn_down_loss", grid=(s // ts,),
        in_specs=[_bs((ts, dff), lambda i: (i, 0)), _bs((dff, d), lambda i: (0, 0)), rows, rows,
                  _bs((1, d), lambda i: (0, 0))],
        out_specs=[_bs((1, 1), lambda i: (0, 0)), rows, _bs((1, d), lambda i: (0, 0)), rows],
        out_shape=[jax.ShapeDtypeStruct((1, 1), F32), jax.ShapeDtypeStruct((s, d), F32),
                   jax.ShapeDtypeStruct((1, d), F32), jax.ShapeDtypeStruct((s, d), BF16)],
        compiler_params=_arb(1),
    )(f, w_down, x1, target, g)


def _swap_halves(t):
    lane = lax.broadcasted_iota(jnp.int32, t.shape, 1)
    return jnp.where(lane < 32, pltpu.roll(t, 96, 1), pltpu.roll(t, 32, 1))


def _rope_fwd(t, cos_t, sin_t):
    return t * cos_t + _swap_halves(t) * sin_t


def _rope_bwd(dt, cos_t, sin_t):
    return dt * cos_t - _swap_halves(dt) * sin_t


def _lat_norms(proj, gq, gkv, ts=512):
    s = proj.shape[0]

    def body(q_ref, c_ref, gq_ref, gkv_ref, qn_ref, cn_ref, rq_ref, rc_ref):
        q = q_ref[...]
        rq = lax.rsqrt(jnp.mean(q * q, axis=-1, keepdims=True) + EPS)
        qn_ref[...] = (q * rq * gq_ref[...]).astype(BF16)
        rq_ref[...] = rq
        cv = c_ref[...]
        rc = lax.rsqrt(jnp.mean(cv * cv, axis=-1, keepdims=True) + EPS)
        cn_ref[...] = (cv * rc * gkv_ref[...]).astype(BF16)
        rc_ref[...] = rc

    return pl.pallas_call(
        body, name="lat_norms", grid=(s // ts,),
        in_specs=[_bs((ts, Q_LORA), lambda i: (i, P_QLAT // Q_LORA)),
                  _bs((ts, KV_LORA), lambda i: (i, P_CKV // KV_LORA)),
                  _bs((1, Q_LORA), lambda i: (0, 0)), _bs((1, KV_LORA), lambda i: (0, 0))],
        out_specs=[_bs((ts, Q_LORA), lambda i: (i, 0)), _bs((ts, KV_LORA), lambda i: (i, 0)),
                   _bs((ts, 1), lambda i: (i, 0)), _bs((ts, 1), lambda i: (i, 0))],
        out_shape=[jax.ShapeDtypeStruct((s, Q_LORA), BF16), jax.ShapeDtypeStruct((s, KV_LORA), BF16),
                   jax.ShapeDtypeStruct((s, 1), F32), jax.ShapeDtypeStruct((s, 1), F32)],
        compiler_params=_arb(1),
    )(proj, proj, gq, gkv)


HEAD_ROWS = 2048
DW_ROWS = 2048
MM_ROWS = 1024


def _q_heads(qn, wq, cos_t, sin_t):
    s = qn.shape[0]
    ts = min(s, HEAD_ROWS)

    def body(qn_ref, w_ref, cos_ref, sin_ref, q_ref):
        o = jnp.dot(qn_ref[...], w_ref[...], preferred_element_type=F32)
        q_ref[:, :LANES] = o[:, :LANES].astype(BF16)
        q_ref[:, LANES:] = _rope_fwd(o[:, LANES:], cos_ref[...], sin_ref[...]).astype(BF16)

    return pl.pallas_call(
        body, name="q_heads", grid=(s // ts, H_A),
        in_specs=[_bs((ts, Q_LORA), lambda i, h: (i, 0)), _bs((None, Q_LORA, HEAD_PAD), lambda i, h: (h, 0, 0)),
                  _bs((ts, LANES), lambda i, h: (i, 0)), _bs((ts, LANES), lambda i, h: (i, 0))],
        out_specs=_bs((None, ts, HEAD_PAD), lambda i, h: (h, i, 0)),
        out_shape=jax.ShapeDtypeStruct((H_A, s, HEAD_PAD), BF16),
        compiler_params=_arb(2),
    )(qn, wq, cos_t, sin_t)


def _kv_heads(cn, wkv, proj, cos_t, sin_t):
    s = cn.shape[0]
    ts = min(s, HEAD_ROWS)

    def body(cn_ref, w_ref, kr_ref, cos_ref, sin_ref, k_ref, v_ref):
        o = jnp.dot(cn_ref[...], w_ref[...], preferred_element_type=F32)
        k_ref[:, :LANES] = o[:, :LANES].astype(BF16)
        k_ref[:, LANES:] = _rope_fwd(kr_ref[...], cos_ref[...], sin_ref[...]).astype(BF16)
        v_ref[...] = o[:, LANES:].astype(BF16)

    return pl.pallas_call(
        body, name="kv_heads", grid=(s // ts, H_A),
        in_specs=[_bs((ts, KV_LORA), lambda i, h: (i, 0)),
                  _bs((None, KV_LORA, QK_NOPE + V_DIM), lambda i, h: (h, 0, 0)),
                  _bs((ts, LANES), lambda i, h: (i, P_KR // LANES)),
                  _bs((ts, LANES), lambda i, h: (i, 0)), _bs((ts, LANES), lambda i, h: (i, 0))],
        out_specs=[_bs((None, ts, HEAD_PAD), lambda i, h: (h, i, 0)), _bs((None, ts, V_DIM), lambda i, h: (h, i, 0))],
        out_shape=[jax.ShapeDtypeStruct((H_A, s, HEAD_PAD), BF16), jax.ShapeDtypeStruct((H_A, s, V_DIM), BF16)],
        compiler_params=_arb(2),
    )(cn, wkv, proj, cos_t, sin_t)


MLA_SCALE = 1.0 / math.sqrt(QK_NOPE + QK_ROPE)
LOG2E = math.log2(math.e)
MLA_EXP2_SCALE = MLA_SCALE * LOG2E


def _lane_tiles(a):
    return [a[:, j * LANES:(j + 1) * LANES] for j in range(a.shape[1] // LANES)]


MLA_SUB = 512


def _mla_fwd(q, k, v, tq=512, tk=1024):
    s = q.shape[1]
    tq = min(tq, s)
    nk = s // tk

    def body(q_ref, k_ref, v_ref, o_ref, lse_ref, m_ref, l_ref, acc_ref):
        m_ref[...] = jnp.full(m_ref.shape, -jnp.inf, F32)
        l_ref[...] = jnp.zeros(l_ref.shape, F32)
        acc_ref[...] = jnp.zeros(acc_ref.shape, F32)

        def step(c, carry):
            rows = pl.ds(pl.multiple_of(c * tk, tk), tk)
            for sub in range(tq // MLA_SUB):
                qr = slice(sub * MLA_SUB, (sub + 1) * MLA_SUB)
                raw = lax.dot_general(q_ref[qr, :], k_ref[rows, :], NT, preferred_element_type=F32)
                m_prev = m_ref[qr, :]
                m_new = jnp.maximum(m_prev, jnp.max(raw, axis=-1, keepdims=True))
                alpha = jnp.exp2((m_prev - m_new) * MLA_EXP2_SCALE)
                ps = [jnp.exp2((t - m_new) * MLA_EXP2_SCALE) for t in _lane_tiles(raw)]
                l_ref[qr, :] = alpha * l_ref[qr, :] + functools.reduce(lambda a, b: a + b, ps)
                p = jnp.concatenate(ps, axis=1).astype(BF16)
                acc_ref[qr, :] = alpha * acc_ref[qr, :] + jnp.dot(p, v_ref[rows, :], preferred_element_type=F32)
                m_ref[qr, :] = m_new
            return carry

        lax.fori_loop(0, nk, step, 0, unroll=True)
        l = jnp.sum(l_ref[...], axis=-1, keepdims=True)
        o_ref[...] = acc_ref[...] / l
        lse_ref[...] = m_ref[...] * MLA_SCALE + jnp.log(l)

    return pl.pallas_call(
        body, name="mla_fwd", grid=(H_A, s // tq),
        in_specs=[_bs((None, tq, HEAD_PAD), lambda h, i: (h, i, 0)),
                  _bs((None, s, HEAD_PAD), lambda h, i: (h, 0, 0)),
                  _bs((None, s, V_DIM), lambda h, i: (h, 0, 0))],
        out_specs=[_bs((tq, V_DIM), lambda h, i: (i, h)), _bs((None, tq, LANES), lambda h, i: (h, i, 0))],
        out_shape=[jax.ShapeDtypeStruct((s, H_A * V_DIM), F32), jax.ShapeDtypeStruct((H_A, s, LANES), F32)],
        scratch_shapes=[pltpu.VMEM((tq, LANES), F32), pltpu.VMEM((tq, LANES), F32), pltpu.VMEM((tq, V_DIM), F32)],
        compiler_params=_arb(2),
    )(q, k, v)


def _mla_bwd(q, k, v, do, o, lse, cos_t, sin_t, tq=512, tk=512):
    s = q.shape[1]
    nq = s // tq
    nkb = s // tk

    def body(q_ref, k_ref, v_ref, do_ref, o_ref, lse_ref, cos_ref, sin_ref, dqp_ref, dkvp_ref, dkr_ref,
             delta_ref, dq_ref, dk_ref, dv_ref):
        @pl.when(pl.program_id(1) == 0)
        def _():
            def init(c, carry):
                rows = pl.ds(pl.multiple_of(c * tq, tq), tq)
                delta = jnp.sum(do_ref[rows, :] * o_ref[rows, :], axis=-1, keepdims=True)
                delta_ref[rows, :] = jnp.broadcast_to(delta, (tq, LANES))
                dq_ref[rows, :] = jnp.zeros((tq, HEAD_PAD), F32)
                return carry

            lax.fori_loop(0, nq, init, 0)

        dk_ref[...] = jnp.zeros(dk_ref.shape, F32)
        dv_ref[...] = jnp.zeros(dv_ref.shape, F32)
        kb = k_ref[...]
        vb = v_ref[...]

        def step(c, carry):
            rows = pl.ds(pl.multiple_of(c * tq, tq), tq)
            qc = q_ref[rows, :]
            doc = do_ref[rows, :].astype(BF16)
            raw = lax.dot_general(qc, kb, NT, preferred_element_type=F32)
            dp = lax.dot_general(doc, vb, NT, preferred_element_type=F32)
            lse2 = lse_ref[rows, :] * LOG2E
            delta = delta_ref[rows, :]
            ps = [jnp.exp2(t * MLA_EXP2_SCALE - lse2) for t in _lane_tiles(raw)]
            dss = [pj * (dj - delta) * MLA_SCALE for pj, dj in zip(ps, _lane_tiles(dp))]
            p = jnp.concatenate(ps, axis=1).astype(BF16)
            ds = jnp.concatenate(dss, axis=1).astype(BF16)
            dv_ref[...] += lax.dot_general(p, doc, TN, preferred_element_type=F32)
            dk_ref[...] += lax.dot_general(ds, qc, TN, preferred_element_type=F32)
            dq_ref[rows, :] += jnp.dot(ds, kb, preferred_element_type=F32)
            return carry

        lax.fori_loop(0, nq, step, 0, unroll=True)
        dkvp_ref[:, :LANES] = dk_ref[:, :LANES].astype(BF16)
        dkvp_ref[:, LANES:] = dv_ref[...].astype(BF16)
        dkr_ref[...] = dk_ref[:, LANES:]

        @pl.when(pl.program_id(1) == nkb - 1)
        def _():
            def finish(c, carry):
                rows = pl.ds(pl.multiple_of(c * tq, tq), tq)
                dqp_ref[rows, :LANES] = dq_ref[rows, :LANES].astype(BF16)
                dqp_ref[rows, LANES:] = _rope_bwd(dq_ref[rows, LANES:], cos_ref[rows, :], sin_ref[rows, :]).astype(BF16)
                return carry

            lax.fori_loop(0, nq, finish, 0)

    whole = lambda w: _bs((s, w), lambda h, j: (0, 0))
    return pl.pallas_call(
        body, name="mla_bwd", grid=(H_A, nkb),
        in_specs=[_bs((None, s, HEAD_PAD), lambda h, j: (h, 0, 0)),
                  _bs((None, tk, HEAD_PAD), lambda h, j: (h, j, 0)),
                  _bs((None, tk, V_DIM), lambda h, j: (h, j, 0)),
                  _bs((s, V_DIM), lambda h, j: (0, h)), _bs((s, V_DIM), lambda h, j: (0, h)),
                  _bs((None, s, LANES), lambda h, j: (h, 0, 0)), whole(LANES), whole(LANES)],
        out_specs=[_bs((None, s, HEAD_PAD), lambda h, j: (h, 0, 0)),
                   _bs((None, tk, HEAD_PAD), lambda h, j: (h, j, 0)),
                   _bs((None, tk, LANES), lambda h, j: (h, j, 0))],
        out_shape=[jax.ShapeDtypeStruct((H_A, s, HEAD_PAD), BF16), jax.ShapeDtypeStruct((H_A, s, HEAD_PAD), BF16),
                   jax.ShapeDtypeStruct((H_A, s, LANES), F32)],
        scratch_shapes=[pltpu.VMEM((s, LANES), F32), pltpu.VMEM((s, HEAD_PAD), F32), pltpu.VMEM((tk, HEAD_PAD), F32),
                        pltpu.VMEM((tk, V_DIM), F32)],
        compiler_params=_arb(2),
    )(q, k, v, do, o, lse, cos_t, sin_t)


def _mla_key_rope_grad(dkr, cos_t, sin_t, ts=512):
    s = dkr.shape[1]

    def body(d_ref, cos_ref, sin_ref, o_ref):
        tot = d_ref[0]
        for h in range(1, H_A):
            tot = tot + d_ref[h]
        o_ref[...] = _rope_bwd(tot, cos_ref[...], sin_ref[...]).astype(BF16)

    rows = _bs((ts, LANES), lambda i: (i, 0))
    return pl.pallas_call(
        body, name="mla_key_rope_grad", grid=(s // ts,),
        in_specs=[_bs((H_A, ts, LANES), lambda i: (0, i, 0)), rows, rows], out_specs=rows,
        out_shape=jax.ShapeDtypeStruct((s, LANES), BF16), compiler_params=_arb(1),
    )(dkr, cos_t, sin_t)


WIN_SCALE = 1.0 / math.sqrt(HD_B)
SPAN = Q_BLOCK + 2 * WINDOW


def _t5_bucket_table():
    a = jnp.arange(Q_BLOCK, dtype=jnp.int32)[:, None]
    c = jnp.arange(SPAN, dtype=jnp.int32)[None, :]
    rel = c - WINDOW - a
    nb = NUM_BUCKETS // 2
    max_exact = nb // 2
    base = (rel > 0).astype(jnp.int32) * nb
    n = jnp.abs(rel)
    nf = jnp.maximum(n, 1).astype(F32)
    large = max_exact + (jnp.log(nf / max_exact) / math.log(MAX_DISTANCE / max_exact)
                         * (nb - max_exact)).astype(jnp.int32)
    large = jnp.minimum(large, nb - 1)
    return base + jnp.where(n < max_exact, n, large)


def _win_bias(bucket, rel_bias):
    def body(rb_ref, bk_ref, o_ref):
        h = pl.program_id(0)
        bk = bk_ref[...]
        acc = jnp.zeros((Q_BLOCK, SPAN), F32)
        for b in range(NUM_BUCKETS):
            acc = jnp.where(bk == b, rb_ref[b, h], acc)
        o_ref[...] = acc

    return pl.pallas_call(
        body, name="win_bias", grid=(H_B,),
        in_specs=[pl.BlockSpec(memory_space=pltpu.SMEM), _bs((Q_BLOCK, SPAN), lambda h: (0, 0))],
        out_specs=_bs((None, Q_BLOCK, SPAN), lambda h: (h, 0, 0)),
        out_shape=jax.ShapeDtypeStruct((H_B, Q_BLOCK, SPAN), F32),
        compiler_params=_arb(1),
    )(rel_bias, bucket)


WIN_HEADS = GROUP


def _win_kv_rows(n, j, nblk):
    blk = jnp.clip(n + j - 1, 0, nblk - 1)
    return pl.ds(pl.multiple_of(blk * Q_BLOCK, Q_BLOCK), Q_BLOCK)


def _win_kv_cols(h0):
    kv = h0 // GROUP
    return slice(kv * HD_B, (kv + 1) * HD_B)


def _win_stack(ref, h0):
    return jnp.concatenate([ref[:, (h0 + g) * HD_B:(h0 + g + 1) * HD_B] for g in range(WIN_HEADS)], axis=0)


def _win_unstack(ref, h0, val):
    for g in range(WIN_HEADS):
        ref[:, (h0 + g) * HD_B:(h0 + g + 1) * HD_B] = val[g * Q_BLOCK:(g + 1) * Q_BLOCK].astype(ref.dtype)


def _win_scores(q, k_ref, h0, bias_ref, n, nblk):
    a = lax.broadcasted_iota(jnp.int32, (WIN_HEADS, Q_BLOCK, Q_BLOCK), 1)
    cc = lax.broadcasted_iota(jnp.int32, (WIN_HEADS, Q_BLOCK, Q_BLOCK), 2)
    valid = [(cc >= a) & (n > 0), None, (cc <= a) & (n < nblk - 1)]
    out = []
    for j in range(3):
        sc = lax.dot_general(q, k_ref[_win_kv_rows(n, j, nblk), _win_kv_cols(h0)], NT, preferred_element_type=F32)
        sc = (sc.reshape(WIN_HEADS, Q_BLOCK, Q_BLOCK) * WIN_SCALE
              + bias_ref[h0:h0 + WIN_HEADS, :, j * Q_BLOCK:(j + 1) * Q_BLOCK])
        if valid[j] is not None:
            sc = jnp.where(valid[j], sc, -1e30)
        out.append(sc)
    return out


def _win_sink(sink_ref, h0):
    hs = lax.broadcasted_iota(jnp.int32, (WIN_HEADS, Q_BLOCK, 1), 0)
    sk = jnp.zeros((WIN_HEADS, Q_BLOCK, 1), F32)
    for g in range(WIN_HEADS):
        sk = jnp.where(hs == g, sink_ref[h0 + g], sk)
    return sk


def _win_fwd(proj_b, bias, sinks):
    s = proj_b.shape[0]
    nblk = s // Q_BLOCK
    rows = WIN_HEADS * Q_BLOCK

    def body(sink_ref, q_ref, k_ref, v_ref, bias_ref, o_ref, lse_ref):
        n = pl.program_id(0)
        for h0 in range(0, H_B, WIN_HEADS):
            sk = _win_sink(sink_ref, h0)
            q = _win_stack(q_ref, h0)
            ss = _win_scores(q, k_ref, h0, bias_ref, n, nblk)
            m = jnp.maximum(jnp.max(jnp.maximum(jnp.maximum(ss[0], ss[1]), ss[2]), axis=2, keepdims=True), sk)
            es = [jnp.exp(sc - m) for sc in ss]
            l = jnp.sum(es[0] + es[1] + es[2], axis=2, keepdims=True) + jnp.exp(sk - m)
            acc = jnp.zeros((rows, HD_B), F32)
            for j, e in enumerate(es):
                p = (e / l).astype(BF16).reshape(rows, Q_BLOCK)
                acc = acc + jnp.dot(p, v_ref[_win_kv_rows(n, j, nblk), _win_kv_cols(h0)],
                                    preferred_element_type=F32)
            _win_unstack(o_ref, h0, acc)
            lse_ref[h0:h0 + WIN_HEADS] = m + jnp.log(l)

    kv_w = KV_B * HD_B
    return pl.pallas_call(
        body, name="win_fwd", grid=(nblk,),
        in_specs=[pl.BlockSpec(memory_space=pltpu.SMEM), _bs((Q_BLOCK, H_B * HD_B), lambda n: (n, P_QB // (H_B * HD_B))),
                  _bs((s, kv_w), lambda n: (0, P_KB // kv_w)), _bs((s, kv_w), lambda n: (0, P_VB // kv_w)),
                  _bs((H_B, Q_BLOCK, SPAN), lambda n: (0, 0, 0))],
        out_specs=[_bs((Q_BLOCK, H_B * HD_B), lambda n: (n, 0)), _bs((H_B, Q_BLOCK, 1), lambda n: (0, n, 0))],
        out_shape=[jax.ShapeDtypeStruct((s, H_B * HD_B), F32), jax.ShapeDtypeStruct((H_B, s, 1), F32)],
        compiler_params=_arb(1),
    )(sinks, proj_b, proj_b, proj_b, bias)


def _win_bwd(proj_b, bias, sinks, do_b, lse):
    s = proj_b.shape[0]
    nblk = s // Q_BLOCK
    rows = WIN_HEADS * Q_BLOCK
    spad = s + 2 * WINDOW

    def body(sink_ref, q_ref, k_ref, v_ref, bias_ref, do_ref, lse_ref, dq_ref, dk_ref, dv_ref, db_ref, dsk_ref):
        n = pl.program_id(0)

        @pl.when(n == 0)
        def _():
            dk_ref[...] = jnp.zeros(dk_ref.shape, F32)
            dv_ref[...] = jnp.zeros(dv_ref.shape, F32)
            db_ref[...] = jnp.zeros(db_ref.shape, F32)
            dsk_ref[...] = jnp.zeros(dsk_ref.shape, F32)

        for h0 in range(0, H_B, WIN_HEADS):
            heads = slice(h0, h0 + WIN_HEADS)
            sk = _win_sink(sink_ref, h0)
            q = _win_stack(q_ref, h0)
            dob = _win_stack(do_ref, h0)
            lse_v = lse_ref[heads]
            ss = _win_scores(q, k_ref, h0, bias_ref, n, nblk)
            ps = [jnp.exp(sc - lse_v) for sc in ss]
            dps = [lax.dot_general(dob, v_ref[_win_kv_rows(n, j, nblk), _win_kv_cols(h0)], NT,
                                   preferred_element_type=F32).reshape(WIN_HEADS, Q_BLOCK, Q_BLOCK) for j in range(3)]
            delta = jnp.sum(ps[0] * dps[0] + ps[1] * dps[1] + ps[2] * dps[2], axis=2, keepdims=True)
            dq = jnp.zeros((rows, HD_B), F32)
            for j in range(3):
                ds = ps[j] * (dps[j] - delta)
                db_ref[heads, :, j * Q_BLOCK:(j + 1) * Q_BLOCK] += ds
                dsb = (ds * WIN_SCALE).astype(BF16).reshape(rows, Q_BLOCK)
                dq = dq + jnp.dot(dsb, k_ref[_win_kv_rows(n, j, nblk), _win_kv_cols(h0)],
                                  preferred_element_type=F32)
                krows = pl.ds(pl.multiple_of((n + j) * Q_BLOCK, Q_BLOCK), Q_BLOCK)
                dk_ref[krows, _win_kv_cols(h0)] += lax.dot_general(dsb, q, TN, preferred_element_type=F32)
                dv_ref[krows, _win_kv_cols(h0)] += lax.dot_general(
                    ps[j].astype(BF16).reshape(rows, Q_BLOCK), dob, TN, preferred_element_type=F32)
            dsk_ref[heads] += -(jnp.exp(sk - lse_v) * delta)
            _win_unstack(dq_ref, h0, dq)

    kv_w = KV_B * HD_B
    qspec = _bs((Q_BLOCK, H_B * HD_B), lambda n: (n, 0))
    kacc = _bs((spad, kv_w), lambda n: (0, 0))
    return pl.pallas_call(
        body, name="win_bwd", grid=(nblk,),
        in_specs=[pl.BlockSpec(memory_space=pltpu.SMEM), _bs((Q_BLOCK, H_B * HD_B), lambda n: (n, P_QB // (H_B * HD_B))),
                  _bs((s, kv_w), lambda n: (0, P_KB // kv_w)), _bs((s, kv_w), lambda n: (0, P_VB // kv_w)),
                  _bs((H_B, Q_BLOCK, SPAN), lambda n: (0, 0, 0)), qspec, _bs((H_B, Q_BLOCK, 1), lambda n: (0, n, 0))],
        out_specs=[qspec, kacc, kacc, _bs((H_B, Q_BLOCK, SPAN), lambda n: (0, 0, 0)),
                   _bs((H_B, Q_BLOCK, 1), lambda n: (0, 0, 0))],
        out_shape=[jax.ShapeDtypeStruct((s, H_B * HD_B), BF16), jax.ShapeDtypeStruct((spad, kv_w), F32),
                   jax.ShapeDtypeStruct((spad, kv_w), F32), jax.ShapeDtypeStruct((H_B, Q_BLOCK, SPAN), F32),
                   jax.ShapeDtypeStruct((H_B, Q_BLOCK, 1), F32)],
        compiler_params=_arb(1),
    )(sinks, proj_b, proj_b, proj_b, bias, do_b, lse)


def _win_param_grads(bucket, dbias, dsink_rows):
    def body(bk_ref, db_ref, ds_ref, o_ref):
        bk = bk_ref[...]
        dbv = db_ref[...]
        lane = lax.broadcasted_iota(jnp.int32, (1, LANES), 1)
        res = jnp.zeros((1, LANES), F32)
        for b in range(NUM_BUCKETS):
            tot = jnp.sum(jnp.sum(jnp.where(bk == b, dbv, 0.0), axis=1, keepdims=True), axis=0, keepdims=True)
            res = jnp.where(lane == b, tot, res)
        stot = jnp.sum(ds_ref[...], axis=0, keepdims=True)
        o_ref[...] = jnp.where(lane == NUM_BUCKETS, stot, res)

    return pl.pallas_call(
        body, name="win_param_grads", grid=(H_B,),
        in_specs=[_bs((Q_BLOCK, SPAN), lambda h: (0, 0)), _bs((None, Q_BLOCK, SPAN), lambda h: (h, 0, 0)),
                  _bs((None, Q_BLOCK, 1), lambda h: (h, 0, 0))],
        out_specs=_bs((None, 1, LANES), lambda h: (h, 0, 0)),
        out_shape=jax.ShapeDtypeStruct((H_B, 1, LANES), F32),
        compiler_params=_arb(1),
    )(bucket, dbias, dsink_rows)


def _mix_out_norm(proj, o_a, o_b, w_out, x, g2, ts=512):
    s = o_a.shape[0]
    wide = lambda cb: _bs((ts, D_MODEL), lambda i: (i, cb))

    def body(ga_ref, gb_ref, oa_ref, ob_ref, w_ref, x_ref, g_ref, m_ref, x1_ref, h_ref, r_ref):
        mixed = (jax.nn.sigmoid(ga_ref[...]) * oa_ref[...] + jax.nn.sigmoid(gb_ref[...]) * ob_ref[...]).astype(BF16)
        m_ref[...] = mixed
        x1 = x_ref[...] + jnp.dot(mixed, w_ref[...], preferred_element_type=F32)
        x1_ref[...] = x1
        r = lax.rsqrt(jnp.mean(x1 * x1, axis=-1, keepdims=True) + EPS)
        h_ref[...] = (x1 * r * g_ref[...]).astype(BF16)
        r_ref[...] = r

    return pl.pallas_call(
        body, name="mix_out_norm", grid=(s // ts,),
        in_specs=[wide(P_GA // D_MODEL), wide(P_GB // D_MODEL), wide(0), wide(0),
                  _bs((D_MODEL, D_MODEL), lambda i: (0, 0)), wide(0), _bs((1, D_MODEL), lambda i: (0, 0))],
        out_specs=[wide(0), wide(0), wide(0), _bs((ts, 1), lambda i: (i, 0))],
        out_shape=[jax.ShapeDtypeStruct((s, D_MODEL), BF16), jax.ShapeDtypeStruct((s, D_MODEL), F32),
                   jax.ShapeDtypeStruct((s, D_MODEL), BF16), jax.ShapeDtypeStruct((s, 1), F32)],
        compiler_params=_arb(1),
    )(proj, proj, o_a, o_b, w_out, x, g2)


def _mix_out_bwd(dx1_b, w_out, proj, o_a, o_b, after=None, ts=512):
    s = o_a.shape[0]
    wide = lambda cb: _bs((ts, D_MODEL), lambda i: (i, cb))

    def body(dx_ref, w_ref, ga_ref, gb_ref, oa_ref, ob_ref, *rest):
        doa_ref, dob_ref, dga_ref, dgb_ref = rest[-4:]
        dm = lax.dot_general(dx_ref[...], w_ref[...], NT, preferred_element_type=F32)
        sa = jax.nn.sigmoid(ga_ref[...])
        sb = jax.nn.sigmoid(gb_ref[...])
        doa_ref[...] = dm * sa
        dob_ref[...] = (dm * sb).astype(BF16)
        dga_ref[...] = (dm * oa_ref[...] * (sa * (1.0 - sa))).astype(BF16)
        dgb_ref[...] = (dm * ob_ref[...] * (sb * (1.0 - sb))).astype(BF16)

    extra = [after] if after is not None else []
    return pl.pallas_call(
        body, name="mix_out_bwd", grid=(s // ts,),
        in_specs=[wide(0), _bs((D_MODEL, D_MODEL), lambda i: (0, 0)), wide(P_GA // D_MODEL), wide(P_GB // D_MODEL),
                  wide(0), wide(0)] + [pl.BlockSpec(memory_space=pl.ANY)] * len(extra),
        out_specs=[wide(0)] * 4,
        out_shape=[jax.ShapeDtypeStruct((s, D_MODEL), F32), jax.ShapeDtypeStruct((s, D_MODEL), BF16),
                   jax.ShapeDtypeStruct((s, D_MODEL), BF16), jax.ShapeDtypeStruct((s, D_MODEL), BF16)],
        compiler_params=_arb(1),
    )(dx1_b, w_out, proj, proj, o_a, o_b, *extra)


CONV_CHUNK = 128
N_SLAB = D_FF // LANES


def _shifted(ref, c, nchunks):
    r0 = c * CONV_CHUNK
    cur = ref[r0:r0 + CONV_CHUNK, :]
    row = lax.broadcasted_iota(jnp.int32, (8, LANES), 0)
    if c > 0:
        prev = ref[r0 - 1:r0 - 1 + CONV_CHUNK, :]
    else:
        down = pltpu.roll(cur, 1, 0)
        prev = jnp.concatenate([jnp.where(row == 0, 0.0, down[:8]), down[8:]], axis=0)
    if c < nchunks - 1:
        nxt = ref[r0 + 1:r0 + 1 + CONV_CHUNK, :]
    else:
        up = pltpu.roll(cur, CONV_CHUNK - 1, 0)
        nxt = jnp.concatenate([up[:-8], jnp.where(row == 7, 0.0, up[-8:])], axis=0)
    return prev, cur, nxt


def _conv_taps(ref, w_ref, b_ref, c, nchunks):
    prev, cur, nxt = _shifted(ref, c, nchunks)
    conv = prev * w_ref[0:1, :] + cur * w_ref[1:2, :] + nxt * w_ref[2:3, :] + b_ref[...]
    return conv, prev, cur, nxt


def _convffn_fwd(u, conv_w, conv_b):
    s = u.shape[0]
    nchunks = s // CONV_CHUNK

    def body(ug_ref, uv_ref, wg_ref, wv_ref, bg_ref, bv_ref, f_ref):
        for c in range(nchunks):
            cg = _conv_taps(ug_ref, wg_ref, bg_ref, c, nchunks)[0]
            cv = _conv_taps(uv_ref, wv_ref, bv_ref, c, nchunks)[0]
            f_ref[c * CONV_CHUNK:(c + 1) * CONV_CHUNK, :] = (cg * jax.nn.sigmoid(cg) * cv).astype(BF16)

    slab = lambda off: _bs((s, LANES), lambda j: (0, off + j))
    wsl = lambda off: _bs((3, LANES), lambda j: (0, off + j))
    bsl = lambda off: _bs((1, LANES), lambda j: (0, off + j))
    return pl.pallas_call(
        body, name="convffn_fwd", grid=(N_SLAB,),
        in_specs=[slab(0), slab(N_SLAB), wsl(0), wsl(N_SLAB), bsl(0), bsl(N_SLAB)],
        out_specs=slab(0), out_shape=jax.ShapeDtypeStruct((s, D_FF), BF16),
        compiler_params=_arb(1),
    )(u, u, conv_w, conv_w, conv_b, conv_b)


def _convffn_bwd(u, conv_w, conv_b, df):
    s = u.shape[0]
    nchunks = s // CONV_CHUNK

    def body(ug_ref, uv_ref, wg_ref, wv_ref, bg_ref, bv_ref, df_ref, du_ref, dw_ref, db_ref, dcg_ref, dcv_ref):
        dwg = [jnp.zeros((1, LANES), F32) for _ in range(3)]
        dwv = [jnp.zeros((1, LANES), F32) for _ in range(3)]
        dbg = jnp.zeros((1, LANES), F32)
        dbv = jnp.zeros((1, LANES), F32)
        for c in range(nchunks):
            rows = slice(c * CONV_CHUNK, (c + 1) * CONV_CHUNK)
            cg, gp, gc, gn = _conv_taps(ug_ref, wg_ref, bg_ref, c, nchunks)
            cv, vp, vc, vn = _conv_taps(uv_ref, wv_ref, bv_ref, c, nchunks)
            dfv = df_ref[rows, :]
            sg = jax.nn.sigmoid(cg)
            dcg = dfv * cv * (sg * (1.0 + cg * (1.0 - sg)))
            dcv = dfv * (cg * sg)
            dcg_ref[rows, :] = dcg
            dcv_ref[rows, :] = dcv
            for t, (tg, tv) in enumerate(((gp, vp), (gc, vc), (gn, vn))):
                dwg[t] = dwg[t] + jnp.sum(tg * dcg, axis=0, keepdims=True)
                dwv[t] = dwv[t] + jnp.sum(tv * dcv, axis=0, keepdims=True)
            dbg = dbg + jnp.sum(dcg, axis=0, keepdims=True)
            dbv = dbv + jnp.sum(dcv, axis=0, keepdims=True)
        for t in range(3):
            dw_ref[0, t:t + 1, :] = dwg[t]
            dw_ref[1, t:t + 1, :] = dwv[t]
        db_ref[0] = dbg
        db_ref[1] = dbv
        for half, (dc_ref, w_ref) in enumerate(((dcg_ref, wg_ref), (dcv_ref, wv_ref))):
            for c in range(nchunks):
                prev, cur, nxt = _shifted(dc_ref, c, nchunks)
                du = nxt * w_ref[0:1, :] + cur * w_ref[1:2, :] + prev * w_ref[2:3, :]
                du_ref[half, c * CONV_CHUNK:(c + 1) * CONV_CHUNK, :] = du.astype(BF16)

    slab = lambda off: _bs((s, LANES), lambda j: (0, off + j))
    wsl = lambda off: _bs((3, LANES), lambda j: (0, off + j))
    bsl = lambda off: _bs((1, LANES), lambda j: (0, off + j))
    return pl.pallas_call(
        body, name="convffn_bwd", grid=(N_SLAB,),
        in_specs=[slab(0), slab(N_SLAB), wsl(0), wsl(N_SLAB), bsl(0), bsl(N_SLAB), slab(0)],
        out_specs=[_bs((2, s, LANES), lambda j: (0, 0, j)), _bs((2, 3, LANES), lambda j: (0, 0, j)),
                   _bs((2, 1, LANES), lambda j: (0, 0, j))],
        out_shape=[jax.ShapeDtypeStruct((2, s, D_FF), BF16), jax.ShapeDtypeStruct((2, 3, D_FF), F32),
                   jax.ShapeDtypeStruct((2, 1, D_FF), F32)],
        scratch_shapes=[pltpu.VMEM((s, LANES), F32), pltpu.VMEM((s, LANES), F32)],
        compiler_params=_arb(1),
    )(u, u, conv_w, conv_w, conv_b, conv_b, df)


def _row_tile(rows, limit=512):
    best = rows
    for t in range(8, min(rows, limit) + 1, 8):
        if rows % t == 0:
            best = t
    return best if rows % 8 == 0 else rows


ADAM_C1 = 1.0 - ADAM_B1 ** ADAM_STEP
ADAM_C2 = 1.0 - ADAM_B2 ** ADAM_STEP


def _adamw_math(w, gv, m, v):
    nm = ADAM_B1 * m + (1.0 - ADAM_B1) * gv
    nv = ADAM_B2 * v + (1.0 - ADAM_B2) * (gv * gv)
    m_hat = nm / ADAM_C1
    v_hat = nv / ADAM_C2
    return -ADAM_LR * (m_hat / (jnp.sqrt(v_hat) + ADAM_EPS) + ADAM_WD * w), nm, nv


def _adamw_halves(name, core, w, mine, theirs, m, v, transposed=False):
    half, cols = (mine.shape[1], mine.shape[0]) if transposed else mine.shape
    tr = _row_tile(cols if transposed else half)
    nr = (cols if transposed else half) // tr

    def body(core_ref, w_ref, mine_ref, theirs_ref, m_ref, v_ref, g_ref, d_ref, nm_ref, nv_ref):
        gv = jnp.where(pl.program_id(0) == core_ref[0], mine_ref[...], theirs_ref[...])
        g_ref[...] = gv
        d_ref[...], nm_ref[...], nv_ref[...] = _adamw_math(w_ref[...], gv, m_ref[...], v_ref[...])

    if transposed:
        full = pl.BlockSpec((tr, half), lambda hf, r, cr: (r, hf))
        part = pl.BlockSpec((tr, half), lambda hf, r, cr: (r, 0))
        shape = (cols, 2 * half)
    else:
        full = pl.BlockSpec((tr, cols), lambda hf, r, cr: (hf * nr + r, 0))
        part = pl.BlockSpec((tr, cols), lambda hf, r, cr: (r, 0))
        shape = (2 * half, cols)
    return pl.pallas_call(
        body, name=name,
        grid_spec=pltpu.PrefetchScalarGridSpec(num_scalar_prefetch=1, grid=(2, nr),
                                               in_specs=[full, part, part, full, full], out_specs=[full] * 4),
        out_shape=[jax.ShapeDtypeStruct(shape, F32)] * 4, compiler_params=_arb(2),
    )(core, w, mine, theirs, m, v)


ANY = pl.BlockSpec(memory_space=pl.ANY)


def _mesh_pos():
    return lax.axis_index("x"), lax.axis_index("y"), lax.axis_index("c")


def _other_chips(x, y):
    return [(1 - x, y), (x, 1 - y), (1 - x, 1 - y)]


def _forward_to_sibling(gathered):
    n = len(gathered)

    def body(*refs):
        in_refs, out_refs = refs[:n], refs[n:2 * n]
        send_sems, recv_sems = refs[2 * n:]
        x, y, c = _mesh_pos()
        cps = []
        for i in range(n):
            for k, chip in enumerate(_other_chips(x, y)):
                pk = 2 * chip[0] + chip[1]
                sems = dict(send_sem=send_sems.at[3 * i + k], recv_sem=recv_sems.at[3 * i + k],
                            device_id=(x, y, 1 - c), device_id_type=MESH)
                sent = pltpu.make_async_remote_copy(src_ref=in_refs[i].at[pk, c], dst_ref=out_refs[i].at[pk, c], **sems)
                sent.start()
                theirs = out_refs[i].at[pk, 1 - c]
                cps.append((sent, pltpu.make_async_remote_copy(src_ref=theirs, dst_ref=theirs, **sems)))
        for sent, arrived in cps:
            sent.wait_send()
            arrived.wait_recv()

    return pl.pallas_call(
        body, name="forward_to_sibling", in_specs=[ANY] * n, out_specs=[ANY] * n,
        out_shape=[jax.ShapeDtypeStruct(a.shape, a.dtype) for a in gathered],
        input_output_aliases={i: i for i in range(n)},
        scratch_shapes=[pltpu.SemaphoreType.DMA((3 * n,)), pltpu.SemaphoreType.DMA((3 * n,))],
    )(*gathered)


def _rs_pair_exchange(name, grads):
    n = len(grads)

    def body(*refs):
        g_refs, o_refs = refs[:n], refs[n:2 * n]
        send_sems, recv_sems = refs[2 * n:]
        x, y, c = _mesh_pos()
        cps = []
        for i in range(n):
            cp = pltpu.make_async_remote_copy(
                src_ref=g_refs[i].at[:, 1 - c], dst_ref=o_refs[i],
                send_sem=send_sems.at[i], recv_sem=recv_sems.at[i], device_id=(x, y, 1 - c), device_id_type=MESH)
            cp.start()
            cps.append(cp)
        for cp in cps:
            cp.wait()

    return pl.pallas_call(
        body, name=name, in_specs=[ANY] * n, out_specs=[ANY] * n,
        out_shape=[jax.ShapeDtypeStruct((4,) + g.shape[2:], F32) for g in grads],
        scratch_shapes=[pltpu.SemaphoreType.DMA((n,)), pltpu.SemaphoreType.DMA((n,))],
    )(*grads)


def _rs_pair_add(name, core, g, recv):
    _, half, cols = recv.shape
    tr = _row_tile(half)
    nr = half // tr

    def body(core_ref, g_ref, r_ref, o_ref):
        o_ref[...] = (g_ref[...] + r_ref[...]).astype(BF16)

    return pl.pallas_call(
        body, name=name,
        grid_spec=pltpu.PrefetchScalarGridSpec(
            num_scalar_prefetch=1, grid=(4, nr),
            in_specs=[pl.BlockSpec((None, None, tr, cols), lambda q, r, cr: (q, cr[0], r, 0)),
                      pl.BlockSpec((None, tr, cols), lambda q, r, cr: (q, r, 0))],
            out_specs=pl.BlockSpec((None, tr, cols), lambda q, r, cr: (q, r, 0))),
        out_shape=jax.ShapeDtypeStruct((4, half, cols), BF16),
        compiler_params=_arb(2),
    )(core, g, recv)


def _rs_final_add(name, chip, pair, recv):
    _, half, cols = pair.shape
    tr = _row_tile(half)

    def body(chip_ref, p_ref, r_ref, o_ref):
        o_ref[...] = ((p_ref[...].astype(F32) + r_ref[0].astype(F32)) + r_ref[1].astype(F32)) + r_ref[2].astype(F32)

    return pl.pallas_call(
        body, name=name,
        grid_spec=pltpu.PrefetchScalarGridSpec(
            num_scalar_prefetch=1, grid=(half // tr,),
            in_specs=[pl.BlockSpec((None, tr, cols), lambda r, ch: (ch[0], r, 0)),
                      pl.BlockSpec((3, tr, cols), lambda r, ch: (0, r, 0))],
            out_specs=pl.BlockSpec((tr, cols), lambda r, ch: (r, 0))),
        out_shape=jax.ShapeDtypeStruct((half, cols), F32),
        compiler_params=_arb(1),
    )(chip, pair, recv)


def _rs_pair_share(halves):
    n = len(halves)

    def body(*refs):
        h_refs, o_refs = refs[:n], refs[n:2 * n]
        send_sems, recv_sems = refs[2 * n:]
        x, y, c = _mesh_pos()
        cps = []
        for i in range(n):
            cp = pltpu.make_async_remote_copy(src_ref=h_refs[i], dst_ref=o_refs[i], send_sem=send_sems.at[i],
                                              recv_sem=recv_sems.at[i], device_id=(x, y, 1 - c), device_id_type=MESH)
            cp.start()
            cps.append(cp)
        for cp in cps:
            cp.wait()

    return pl.pallas_call(
        body, name="rs_pair_share", in_specs=[ANY] * n, out_specs=[ANY] * n,
        out_shape=[jax.ShapeDtypeStruct(h.shape, F32) for h in halves],
        scratch_shapes=[pltpu.SemaphoreType.DMA((n,)), pltpu.SemaphoreType.DMA((n,))],
    )(*halves)


HBM = pl.BlockSpec(memory_space=pltpu.HBM)
SEM = pl.BlockSpec(memory_space=pltpu.SEMAPHORE)


class _SplitExchange:
    def __init__(self, name, srcs, land_shapes, src_of, dst_of, arrive_of, to_sibling=False):
        self.name, self.srcs, self.land_shapes = name, list(srcs), list(land_shapes)
        self.src_of, self.dst_of, self.arrive_of = src_of, dst_of, arrive_of
        self.to_sibling = to_sibling
        self.fan = 1 if to_sibling else 3

    def _copies(self, src_refs, land_refs, send_sems, recv_sems):
        x, y, c = _mesh_pos()
        p = 2 * x + y
        if self.to_sibling:
            peers = [((x, y, 1 - c), 1 - c)]
        else:
            peers = [((*chip, c), 2 * chip[0] + chip[1]) for chip in _other_chips(x, y)]
        out = []
        for i, (src, land) in enumerate(zip(src_refs, land_refs)):
            for k, (peer, pk) in enumerate(peers):
                sems = dict(send_sem=send_sems.at[self.fan * i + k], recv_sem=recv_sems.at[self.fan * i + k],
                            device_id=peer, device_id_type=MESH)
                sent = pltpu.make_async_remote_copy(src_ref=self.src_of(src, k, p, pk, c),
                                                    dst_ref=self.dst_of(land, k, p, pk, c), **sems)
                here = self.arrive_of(land, k, p, pk, c)
                out.append((sent, pltpu.make_async_remote_copy(src_ref=here, dst_ref=here, **sems)))
        return out

    def start(self, after=None):
        n = len(self.srcs)
        n_in = 2 * n + (after is not None)

        def body(*refs):
            for sent, _ in self._copies(refs[:n], refs[n:2 * n], refs[n_in], refs[n_in + 1]):
                sent.start()
            refs[-1][...] = jnp.zeros((8, LANES), F32)

        lands = [lax.empty(shape, src.dtype) for shape, src in zip(self.land_shapes, self.srcs)]
        operands = [pltpu.with_memory_space_constraint(a, pltpu.HBM) for a in self.srcs + lands]
        outs = pl.pallas_call(
            body, name=self.name + "_start",
            out_shape=(pltpu.SemaphoreType.DMA((self.fan * n,)), pltpu.SemaphoreType.DMA((self.fan * n,)),
                       *[pltpu.HBM(a.shape, a.dtype) for a in operands], jax.ShapeDtypeStruct((8, LANES), F32)),
            in_specs=[HBM] * (2 * n) + [ANY] * (after is not None),
            out_specs=(SEM, SEM, *[HBM] * (2 * n), pl.BlockSpec(memory_space=pltpu.VMEM)),
            input_output_aliases={j: 2 + j for j in range(2 * n)},
            compiler_params=pltpu.CompilerParams(has_side_effects=pltpu.SideEffectType.DATAFLOW_SIDE_EFFECTING),
        )(*operands, *([after] if after is not None else []))
        self._sems, self._thru = outs[:2], list(outs[2:2 + 2 * n])
        return outs[-1]

    def wait(self, after):
        n = len(self.srcs)

        def body(*refs):
            for sent, arrived in self._copies(refs[:n], refs[n:2 * n], refs[2 * n], refs[2 * n + 1]):
                sent.wait_send()
                arrived.wait_recv()

        after = list(after) if isinstance(after, (list, tuple)) else [after]
        outs = pl.pallas_call(
            body, name=self.name + "_wait",
            out_shape=tuple(pltpu.HBM(a.shape, a.dtype) for a in self._thru),
            in_specs=[HBM] * (2 * n) + [SEM, SEM] + [ANY] * len(after), out_specs=tuple([HBM] * (2 * n)),
            input_output_aliases={j: j for j in range(2 * n)},
            compiler_params=pltpu.CompilerParams(has_side_effects=pltpu.SideEffectType.DATAFLOW_SIDE_EFFECTING),
        )(*self._thru, *self._sems, *after)
        return list(outs[:n]), list(outs[n:])


def _small_allreduce(parts):
    n = len(parts)

    def body(*refs):
        in_refs, out_refs, gather_refs = refs[:n], refs[n:2 * n], refs[2 * n:3 * n]
        send_sems, recv_sems = refs[3 * n:]
        x, y, c = _mesh_pos()
        me = 4 * x + 2 * y + c
        cps = []
        for i in range(n):
            gather_refs[i][me] = in_refs[i][...]
            for j in range(1, 8):
                peer = (x ^ (j >> 2), y ^ ((j >> 1) & 1), c ^ (j & 1))
                cp = pltpu.make_async_remote_copy(
                    src_ref=in_refs[i], dst_ref=gather_refs[i].at[me], send_sem=send_sems.at[7 * i + j - 1],
                    recv_sem=recv_sems.at[7 * i + j - 1], device_id=peer, device_id_type=MESH)
                cp.start()
                cps.append(cp)
        for i in range(n):
            for j in range(1, 8):
                peer_id = 4 * (x ^ (j >> 2)) + 2 * (y ^ ((j >> 1) & 1)) + (c ^ (j & 1))
                slot = gather_refs[i].at[peer_id]
                pltpu.make_async_remote_copy(src_ref=slot, dst_ref=slot, send_sem=send_sems.at[7 * i + j - 1],
                                             recv_sem=recv_sems.at[7 * i + j - 1], device_id=(x, y, c),
                                             device_id_type=MESH).wait_recv()
        for cp in cps:
            cp.wait_send()
        for i in range(n):
            tot = gather_refs[i][0]
            for d in range(1, 8):
                tot = tot + gather_refs[i][d]
            out_refs[i][...] = tot

    vmem = pl.BlockSpec(memory_space=pltpu.VMEM)
    return pl.pallas_call(
        body, name="small_allreduce", in_specs=[vmem] * n, out_specs=[vmem] * n,
        out_shape=[jax.ShapeDtypeStruct(p.shape, F32) for p in parts],
        scratch_shapes=[pltpu.VMEM((8,) + p.shape, F32) for p in parts]
        + [pltpu.SemaphoreType.DMA((7 * n,)), pltpu.SemaphoreType.DMA((7 * n,))],
    )(*parts)


def _adamw_small(ws, gs, ms, vs):
    n = len(ws)

    def body(*refs):
        for i in range(n):
            w_ref, g_ref, m_ref, v_ref = refs[i], refs[n + i], refs[2 * n + i], refs[3 * n + i]
            d_ref, nm_ref, nv_ref = refs[4 * n + i], refs[5 * n + i], refs[6 * n + i]
            d_ref[...], nm_ref[...], nv_ref[...] = _adamw_math(w_ref[...], g_ref[...], m_ref[...], v_ref[...])

    vmem = pl.BlockSpec(memory_space=pltpu.VMEM)
    shapes = [jax.ShapeDtypeStruct(w.shape, F32) for w in ws]
    outs = pl.pallas_call(body, name="adamw_small", in_specs=[vmem] * (4 * n), out_specs=[vmem] * (3 * n),
                          out_shape=shapes * 3)(*ws, *gs, *ms, *vs)
    return outs[:n], outs[n:2 * n], outs[2 * n:]


W_IN_PIECES = ((0, 256, P_QLAT), (256, 384, P_CKV), (384, 448, P_KR), (448, 1472, P_QB), (1472, 1728, P_KB),
               (1728, 1984, P_VB), (1984, 3008, P_GA), (3008, 4032, P_GB))
W_IN_SHARD = 1008


def _w_in_from_shards(shards):
    cols = []
    for lo, hi, _ in sorted(W_IN_PIECES, key=lambda piece: piece[2]):
        for q in range(4):
            a, b = max(lo, q * W_IN_SHARD), min(hi, (q + 1) * W_IN_SHARD)
            if a < b:
                cols.append(shards[q][:, a - q * W_IN_SHARD:b - q * W_IN_SHARD])
    cols.append(jnp.zeros((shards.shape[1], W_IN_PAD - 4 * W_IN_SHARD), shards.dtype))
    return jnp.concatenate(cols, axis=1)


def _w_in_to_shards(p):
    shards = []
    for q in range(4):
        cols = []
        for lo, hi, at in W_IN_PIECES:
            a, b = max(lo, q * W_IN_SHARD), min(hi, (q + 1) * W_IN_SHARD)
            if a < b:
                cols.append(p[:, at + a - lo:at + b - lo])
        shards.append(jnp.concatenate(cols, axis=1))
    return jnp.stack(shards)


def _col_shards(w):
    r, c4 = w.shape
    return w.reshape(r, 4, c4 // 4).transpose(1, 0, 2)


def _local_step(x, positions, target, norm1_g, first_weights, q_a_norm_g, kv_a_norm_g, rel_bias, sinks,
                late_weights, norm2_g, conv_b, final_norm_g, early_grads=None, last_grads=None):
    s = x.shape[0]
    half = QK_ROPE // 2
    inv_freq = jnp.asarray(np.float32(ROPE_THETA) ** (-np.arange(half, dtype=np.float32) / np.float32(half)))
    ang = positions.astype(F32)[:, None] * inv_freq[None, :]
    cos, sin = jnp.cos(ang), jnp.sin(ang)
    z64 = jnp.zeros((s, 64), F32)
    cos_t = jnp.concatenate([cos, cos, z64], axis=1)
    sin_t = jnp.concatenate([-sin, sin, z64], axis=1)
    bucket = _t5_bucket_table()
    sinks1 = sinks.reshape(H_B)

    h1, rstd1 = _rmsnorm_fwd("norm1_fwd", x, norm1_g, D_MODEL, 0)
    bias = _win_bias(bucket, rel_bias)
    w_in_p, wq, wkv = first_weights([h1, bias, cos_t, sin_t])
    proj, proj_b = _input_projection(h1, w_in_p)
    qn, cn, rstd_q, rstd_c = _lat_norms(proj, q_a_norm_g, kv_a_norm_g)
    q = _q_heads(qn, wq, cos_t, sin_t)
    k, v = _kv_heads(cn, wkv, proj, cos_t, sin_t)
    o_a, lse_a = _mla_fwd(q, k, v)

    o_b, lse_b = _win_fwd(proj_b, bias, sinks1)

    w_out, w_up, w_down, conv_w = late_weights([o_a, o_b])
    row512 = lambda w: _bs((MM_ROWS, w), lambda i, j, k: (i, 0))
    whole = lambda r, c: _bs((r, c), lambda i, j, k: (0, 0))
    mixed, x1, h2, rstd2 = _mix_out_norm(proj, o_a, o_b, w_out, x, norm2_g)
    u = _matmul("ffn_up", h2, w_up, out_shape=(s, 2 * D_FF), out_dtype=F32, grid=(s // MM_ROWS, 4, 1),
                a_spec=_bs((MM_ROWS, D_MODEL), lambda i, j, k: (i, 0)), b_spec=_bs((D_MODEL, D_FF // 2), lambda i, j, k: (0, j)),
                o_spec=_bs((MM_ROWS, D_FF // 2), lambda i, j, k: (i, j)), contract=NN)
    f = _convffn_fwd(u, conv_w, conv_b)
    loss, dx2, d_final_g, dx2_b = _ffn_down_loss(f, w_down, x1, target, final_norm_g.reshape(1, D_MODEL))
    tk = min(s, DW_ROWS)

    df = _matmul("ffn_down_dx", dx2_b, w_down, out_shape=(s, D_FF), out_dtype=F32, grid=(s // MM_ROWS, 2, 1),
                 a_spec=row512(D_MODEL), b_spec=_bs((D_FF // 2, D_MODEL), lambda i, j, k: (j, 0)),
                 o_spec=_bs((MM_ROWS, D_FF // 2), lambda i, j, k: (i, j)), contract=NT)
    d_w_down = _matmul("ffn_down_dw", f, dx2_b, out_shape=(D_FF, D_MODEL), out_dtype=F32, grid=(2, 1, s // tk),
                       a_spec=_bs((tk, D_FF // 2), lambda i, j, k: (k, i)), b_spec=_bs((tk, D_MODEL), lambda i, j, k: (k, 0)),
                       o_spec=_bs((D_FF // 2, D_MODEL), lambda i, j, k: (i, 0)), contract=TN)
    du, d_conv_w2, d_conv_b2 = _convffn_bwd(u, conv_w, conv_b, df)
    kc = D_FF // 2
    dx1, d_norm2_g, dx1_b = _matmul_norm_bwd(
        "ffn_up_dx_norm2_bwd", du, w_up, grid=(s // MM_ROWS, 4),
        a_spec=_bs((None, MM_ROWS, kc), lambda i, k: (k // 2, i, k % 2)), b_spec=_bs((D_MODEL, kc), lambda i, k: (0, k)),
        x=x1, rstd=rstd2, g=norm2_g, res=dx2, bf16_copy=True)
    d_w_up = _matmul("ffn_up_dw", h2, du, out_shape=(D_MODEL, 2 * D_FF), out_dtype=F32, grid=(1, 4, s // tk),
                     a_spec=_bs((tk, D_MODEL), lambda i, j, k: (k, 0)),
                     b_spec=_bs((None, tk, kc), lambda i, j, k: (j // 2, k, j % 2)),
                     o_spec=_bs((D_MODEL, kc), lambda i, j, k: (0, j)), contract=TN)

    d_w_out = _matmul("attn_out_dw", mixed, dx1_b, out_shape=(D_MODEL, D_MODEL), out_dtype=F32, grid=(1, 1, s // tk),
                      a_spec=_bs((tk, D_MODEL), lambda i, j, k: (k, 0)), b_spec=_bs((tk, D_MODEL), lambda i, j, k: (k, 0)),
                      o_spec=whole(D_MODEL, D_MODEL), contract=TN)
    token, early_grads_on = early_grads(d_w_out, d_w_up, d_w_down) if early_grads is not None else (None, None)
    do_a, do_b, d_ga, d_gb = _mix_out_bwd(dx1_b, w_out, proj, o_a, o_b, after=token)
    if early_grads_on is not None:
        sinks1 = sinks1 + early_grads_on(d_ga)[0, :H_B]

    d_qb, dk_pad, dv_pad, dbias, dsink_rows = _win_bwd(proj_b, bias, sinks1, do_b, lse_b)
    wp = _win_param_grads(bucket, dbias, dsink_rows)[:, 0, :]
    d_rel_bias = wp[:, :NUM_BUCKETS].T
    d_sinks = wp[:, NUM_BUCKETS].reshape(1, H_B)
    d_kb = dk_pad[WINDOW:WINDOW + s].astype(BF16)
    d_vb = dv_pad[WINDOW:WINDOW + s].astype(BF16)

    dq_pre, dkv_pre, dkr = _mla_bwd(q, k, v, do_a, o_a, lse_a, cos_t, sin_t)
    d_kr = _mla_key_rope_grad(dkr, cos_t, sin_t)
    th = min(s, HEAD_ROWS)
    hgrid = (s // th, 1, H_A)
    hblock = _bs((None, th, HEAD_PAD), lambda i, j, k: (k, i, 0))
    hrows = lambda w: _bs((th, w), lambda i, j, k: (i, 0))
    dqn = _matmul("q_up_dx", dq_pre, wq, out_shape=(s, Q_LORA), out_dtype=F32, grid=hgrid, a_spec=hblock,
                  b_spec=_bs((None, Q_LORA, HEAD_PAD), lambda i, j, k: (k, 0, 0)), o_spec=hrows(Q_LORA), contract=NT)
    dcn = _matmul("kv_up_dx", dkv_pre, wkv, out_shape=(s, KV_LORA), out_dtype=F32, grid=hgrid, a_spec=hblock,
                  b_spec=_bs((None, KV_LORA, HEAD_PAD), lambda i, j, k: (k, 0, 0)), o_spec=hrows(KV_LORA), contract=NT)
    wgrid = (H_A, 1, s // th)
    d_wq = _matmul("q_up_dw", qn, dq_pre, out_shape=(H_A, Q_LORA, HEAD_PAD), out_dtype=F32, grid=wgrid,
                   a_spec=_bs((th, Q_LORA), lambda i, j, k: (k, 0)), b_spec=_bs((None, th, HEAD_PAD), lambda i, j, k: (i, k, 0)),
                   o_spec=_bs((None, Q_LORA, HEAD_PAD), lambda i, j, k: (i, 0, 0)), contract=TN)
    d_wkv = _matmul("kv_up_dw", cn, dkv_pre, out_shape=(H_A, KV_LORA, HEAD_PAD), out_dtype=F32, grid=wgrid,
                    a_spec=_bs((th, KV_LORA), lambda i, j, k: (k, 0)), b_spec=_bs((None, th, HEAD_PAD), lambda i, j, k: (i, k, 0)),
                    o_spec=_bs((None, KV_LORA, HEAD_PAD), lambda i, j, k: (i, 0, 0)), contract=TN)
    d_qlat, d_gq = _rmsnorm_bwd("q_norm_bwd", dqn, proj, rstd_q, q_a_norm_g, Q_LORA, P_QLAT // Q_LORA, BF16)
    d_ckv, d_gkv = _rmsnorm_bwd("kv_norm_bwd", dcn, proj, rstd_c, kv_a_norm_g, KV_LORA, P_CKV // KV_LORA, BF16)

    dproj = jnp.concatenate([d_qb, d_ga, d_gb, d_qlat, d_kb, d_vb, d_ckv, d_kr], axis=1)
    d_w_in_p = _matmul("proj_dw", h1, dproj, out_shape=(D_MODEL, W_IN_PAD), out_dtype=F32, grid=(1, W_IN_PAD // 1024, s // tk),
                       a_spec=_bs((tk, D_MODEL), lambda i, j, k: (k, 0)), b_spec=_bs((tk, 1024), lambda i, j, k: (k, j)),
                       o_spec=_bs((D_MODEL, 1024), lambda i, j, k: (0, j)), contract=TN)
    token = last_grads(d_w_in_p, d_wq, d_wkv) if last_grads is not None else None
    dx, d_norm1_g = _matmul_norm_bwd(
        "proj_dx_norm1_bwd", dproj, w_in_p, grid=(s // MM_ROWS, W_IN_PAD // 1024),
        a_spec=_bs((MM_ROWS, 1024), lambda i, k: (i, k)), b_spec=_bs((D_MODEL, 1024), lambda i, k: (0, k)),
        x=x, rstd=rstd1, g=norm1_g, res=dx1, after=token)

    grads = dict(
        norm1_g=d_norm1_g, w_in_p=d_w_in_p, q_a_norm_g=d_gq, wq=d_wq, kv_a_norm_g=d_gkv, wkv=d_wkv,
        rel_bias=d_rel_bias, sinks=d_sinks, w_out=d_w_out, norm2_g=d_norm2_g, w_up=d_w_up,
        conv_w=jnp.concatenate([d_conv_w2[0], d_conv_w2[1]], axis=1),
        conv_b=jnp.concatenate([d_conv_b2[0], d_conv_b2[1]], axis=1),
        w_down=d_w_down, final_norm_g=d_final_g.reshape(D_MODEL))
    return loss, dx, grads


HEADS_PER_SHARD = H_A // 4


def _head_cols(h, width):
    return slice((h % HEADS_PER_SHARD) * width, (h % HEADS_PER_SHARD + 1) * width)


def _wq_heads(shards):
    per = QK_NOPE + QK_ROPE
    w = jnp.stack([shards[h // HEADS_PER_SHARD][:, _head_cols(h, per)] for h in range(H_A)])
    return jnp.pad(w, ((0, 0), (0, 0), (0, HEAD_PAD - per)))


def _wq_shards(d_wq):
    per = QK_NOPE + QK_ROPE
    return jnp.stack([jnp.concatenate([d_wq[h][:, :per] for h in range(q * HEADS_PER_SHARD, (q + 1) * HEADS_PER_SHARD)],
                                      axis=1) for q in range(4)])


def _wkv_heads(shards):
    return jnp.stack([shards[h // HEADS_PER_SHARD][:, _head_cols(h, QK_NOPE + V_DIM)] for h in range(H_A)])


def _wkv_shards(d_wkv):
    return jnp.stack([jnp.concatenate([d_wkv[h] for h in range(q * HEADS_PER_SHARD, (q + 1) * HEADS_PER_SHARD)], axis=1)
                      for q in range(4)])


SMALL = ("norm1_g", "q_a_norm_g", "kv_a_norm_g", "rel_bias", "sinks", "norm2_g", "conv_b", "final_norm_g")
FIRST = ("w_in", "w_q_b", "w_kv_b")
LATER = ("w_out", "w_up", "w_down")
BIG = FIRST + LATER


def kernel(x, positions, norm1_g, w_in, q_a_norm_g, w_q_b, kv_a_norm_g, w_kv_b, rel_bias, sinks, w_out, norm2_g, w_up, conv_w, conv_b, w_down, final_norm_g, loss_target, m_norm1_g, m_w_in, m_q_a_norm_g, m_w_q_b, m_kv_a_norm_g, m_w_kv_b, m_rel_bias, m_sinks, m_w_out, m_norm2_g, m_w_up, m_conv_w, m_conv_b, m_w_down, m_final_norm_g, v_norm1_g, v_w_in, v_q_a_norm_g, v_w_q_b, v_kv_a_norm_g, v_w_kv_b, v_rel_bias, v_sinks, v_w_out, v_norm2_g, v_w_up, v_conv_w, v_conv_b, v_w_down, v_final_norm_g):
    weights = dict(norm1_g=norm1_g, w_in=w_in, q_a_norm_g=q_a_norm_g, w_q_b=w_q_b, kv_a_norm_g=kv_a_norm_g,
                   w_kv_b=w_kv_b, rel_bias=rel_bias, sinks=sinks, w_out=w_out, norm2_g=norm2_g, w_up=w_up,
                   conv_w=conv_w, conv_b=conv_b, w_down=w_down, final_norm_g=final_norm_g)
    mom_m = dict(norm1_g=m_norm1_g, w_in=m_w_in, q_a_norm_g=m_q_a_norm_g, w_q_b=m_w_q_b, kv_a_norm_g=m_kv_a_norm_g,
                 w_kv_b=m_w_kv_b, rel_bias=m_rel_bias, sinks=m_sinks, w_out=m_w_out, norm2_g=m_norm2_g, w_up=m_w_up,
                 conv_w=m_conv_w, conv_b=m_conv_b, w_down=m_w_down, final_norm_g=m_final_norm_g)
    mom_v = dict(norm1_g=v_norm1_g, w_in=v_w_in, q_a_norm_g=v_q_a_norm_g, w_q_b=v_w_q_b, kv_a_norm_g=v_kv_a_norm_g,
                 w_kv_b=v_w_kv_b, rel_bias=v_rel_bias, sinks=v_sinks, w_out=v_w_out, norm2_g=v_norm2_g, w_up=v_w_up,
                 conv_w=v_conv_w, conv_b=v_conv_b, w_down=v_w_down, final_norm_g=v_final_norm_g)
    shard2d = {n: weights[n][0] for n in BIG}
    conv_w_shard = conv_w[0]
    xi, yi, ci = lax.axis_index("x"), lax.axis_index("y"), lax.axis_index("c")
    chip = (2 * xi + yi).astype(jnp.int32)

    core = ci.astype(jnp.int32).reshape(1)
    chip1 = chip.reshape(1)
    cat_cols = lambda a: jnp.concatenate([a[0], a[1], a[2], a[3]], axis=1)
    own_slot = lambda a, own: lax.dynamic_update_index_in_dim(a, own, chip, 0)
    halved = lambda a: a.reshape((2, a.shape[0] // 2) + a.shape[1:])
    quartered = lambda a: a.reshape(4, 2, a.shape[1] // 2, a.shape[2])

    first = [halved(shard2d[n].astype(BF16)) for n in FIRST]
    gather1 = _SplitExchange("gather_first", first, [(4,) + a.shape for a in first],
                             src_of=lambda ref, k, p, pk, c: ref.at[c], dst_of=lambda ref, k, p, pk, c: ref.at[p, c],
                             arrive_of=lambda ref, k, p, pk, c: ref.at[pk, c])
    token1 = gather1.start()

    def first_weights(after):
        own, landed = gather1.wait(after)
        gathered = [own_slot(a, mine) for a, mine in zip(_forward_to_sibling(landed), own)]
        g = {n: a.reshape((4,) + shard2d[n].shape) for n, a in zip(FIRST, gathered)}
        return _w_in_from_shards(g["w_in"]), _wq_heads(g["w_q_b"]), _wkv_heads(g["w_kv_b"])

    later = [shard2d[n].astype(BF16) for n in LATER] + [conv_w_shard]
    gather2 = _SplitExchange("gather_later", later, [(4,) + a.shape for a in later],
                             src_of=lambda ref, k, p, pk, c: ref, dst_of=lambda ref, k, p, pk, c: ref.at[p],
                             arrive_of=lambda ref, k, p, pk, c: ref.at[pk])
    norm1_g_in = norm1_g + gather2.start(after=token1)[:1, :1]

    def late_weights(after):
        w_out_g, w_up_g, w_down_g, conv_w_g = [own_slot(a, mine) for mine, a in zip(*gather2.wait(after))]
        return w_out_g.reshape(D_MODEL, D_MODEL), cat_cols(w_up_g), w_down_g.reshape(D_FF, D_MODEL), cat_cols(conv_w_g)

    early = {}

    def early_grads(d_w_out, d_w_up, d_w_down):
        grads = [quartered(d_w_out.reshape(4, D_MODEL // 4, D_MODEL)), quartered(_col_shards(d_w_up)),
                 quartered(d_w_down.reshape(4, D_FF // 4, D_MODEL))]
        swap = _SplitExchange("rs_pair_exchange_early", grads, [(4,) + a.shape[2:] for a in grads], to_sibling=True,
                              src_of=lambda ref, k, p, pk, c: ref.at[:, pk], dst_of=lambda ref, k, p, pk, c: ref,
                              arrive_of=lambda ref, k, p, pk, c: ref)

        def on(after):
            kept, recv = swap.wait(after)
            early["pairs"] = [_rs_pair_add(f"rs_pair_add_{n}", core, gfull, r) for n, gfull, r in zip(LATER, kept, recv)]
            early["ici"] = _SplitExchange("rs_ici_early", early["pairs"], [(3,) + a.shape[1:] for a in early["pairs"]],
                                          src_of=lambda ref, k, p, pk, c: ref.at[pk], dst_of=lambda ref, k, p, pk, c: ref.at[k],
                                          arrive_of=lambda ref, k, p, pk, c: ref.at[k])
            return early["ici"].start()

        return swap.start(), on

    last = {}

    def last_grads(d_w_in_p, d_wq, d_wkv):
        grads = [quartered(_w_in_to_shards(d_w_in_p)), quartered(_wq_shards(d_wq)), quartered(_wkv_shards(d_wkv))]
        recv = _rs_pair_exchange("rs_pair_exchange_last", grads)
        last["pairs"] = [_rs_pair_add(f"rs_pair_add_{n}", core, gfull, r) for n, gfull, r in zip(FIRST, grads, recv)]
        last["ici"] = _SplitExchange("rs_ici_last", last["pairs"], [(3,) + a.shape[1:] for a in last["pairs"]],
                                     src_of=lambda ref, k, p, pk, c: ref.at[pk], dst_of=lambda ref, k, p, pk, c: ref.at[k],
                                     arrive_of=lambda ref, k, p, pk, c: ref.at[k])
        return last["ici"].start()

    loss, dx, gr = _local_step(x[0], positions, loss_target[0], norm1_g_in, first_weights, q_a_norm_g, kv_a_norm_g,
                               rel_bias, sinks, late_weights, norm2_g, conv_b, final_norm_g, early_grads, last_grads)

    last_pairs, last_recv = last["ici"].wait(dx)
    early_pairs, early_recv = early["ici"].wait(dx)
    pairs, recv2 = last_pairs + early_pairs, last_recv + early_recv
    halves = [_rs_final_add(f"rs_final_add_{n}", chip1, pr, r) for n, pr, r in zip(FIRST + LATER, pairs, recv2)]
    sibling_halves = _rs_pair_share(halves)

    as_rows = lambda a: a.reshape((-1, a.shape[-1]))
    summed = _small_allreduce([as_rows(gr[n]) for n in SMALL] + [gr["conv_w"], loss])
    small_g = dict(zip(SMALL, summed[:len(SMALL)]))
    conv_w_g = lax.dynamic_slice_in_dim(summed[len(SMALL)], chip * (2 * D_FF // 4), 2 * D_FF // 4, axis=1)
    loss_out = summed[-1].reshape(())

    out_g, out_d, out_m, out_v = {}, {}, {}, {}
    for n, mine, theirs in zip(FIRST + LATER, halves, sibling_halves):
        if shard2d[n].shape[1] % LANES:
            res = _adamw_halves(f"adamw_{n}", core, shard2d[n].T, mine.T, theirs.T, mom_m[n][0].T, mom_v[n][0].T,
                                transposed=True)
            gsh, d, nm, nv = [a.T for a in res]
        else:
            gsh, d, nm, nv = _adamw_halves(f"adamw_{n}", core, shard2d[n], mine, theirs, mom_m[n][0], mom_v[n][0])
        out_g[n], out_d[n], out_m[n], out_v[n] = gsh[None], d[None], nm[None], nv[None]
    names = SMALL + ("conv_w",)
    sg = [small_g[n] for n in SMALL] + [conv_w_g]
    ds, nms, nvs = _adamw_small([as_rows(weights[n]) for n in names], sg, [as_rows(mom_m[n]) for n in names],
                                [as_rows(mom_v[n]) for n in names])
    for n, gg, dd, mm, vv in zip(names, sg, ds, nms, nvs):
        shp = weights[n].shape
        out_g[n], out_d[n], out_m[n], out_v[n] = gg.reshape(shp), dd.reshape(shp), mm.reshape(shp), vv.reshape(shp)

    order = ("norm1_g", "w_in", "q_a_norm_g", "w_q_b", "kv_a_norm_g", "w_kv_b", "rel_bias", "sinks", "w_out",
             "norm2_g", "w_up", "conv_w", "conv_b", "w_down", "final_norm_g")
    return (loss_out, dx[None], *[out_g[n] for n in order], *[out_d[n] for n in order],
            *[out_m[n] for n in order], *[out_v[n] for n in order])
```

```python
import functools
import math

import jax
import jax.numpy as jnp
import numpy as np
from jax import lax
from jax.experimental import pallas as pl
from jax.experimental.pallas import tpu as pltpu

F32 = jnp.float32
BF16 = jnp.bfloat16
MESH = pl.DeviceIdType.MESH

D_MODEL = 1024
EPS = 1e-6
H_A = 8
QK_NOPE = 128
QK_ROPE = 64
V_DIM = 128
Q_LORA = 256
KV_LORA = 128
ROPE_THETA = 10000.0
H_B = 16
KV_B = 4
GROUP = 4
HD_B = 64
WINDOW = 128
Q_BLOCK = 128
NUM_BUCKETS = 32
MAX_DISTANCE = 128
D_FF = 2816
HEAD_PAD = 256

ADAM_LR = 0.001
ADAM_B1 = 0.9
ADAM_B2 = 0.999
ADAM_EPS = 1e-08
ADAM_WD = 0.01
ADAM_STEP = 10

LANES = 128
P_QB, P_GA, P_GB, P_QLAT, P_KB, P_VB, P_CKV, P_KR = 0, 1024, 2048, 3072, 3328, 3584, 3840, 3968
W_IN_PAD = 4096

NT = (((1,), (1,)), ((), ()))
NN = (((1,), (0,)), ((), ()))
TN = (((0,), (0,)), ((), ()))


def _arb(n):
    return pltpu.CompilerParams(dimension_semantics=("arbitrary",) * n)


def _matmul(name, a, b, *, out_shape, out_dtype, grid, a_spec, b_spec, o_spec, contract, add=None, bf16_copy=False,
            after=None):
    nk = grid[2]
    acc_shape = tuple(d for d in o_spec.block_shape if d is not None)
    n_in = 2 + (add is not None) + (after is not None)
    n_out = 2 if bf16_copy else 1

    def body(*refs):
        a_ref, b_ref = refs[:2]
        add_ref = refs[2] if add is not None else None
        o_refs = refs[n_in:n_in + n_out]
        scratch = refs[n_in + n_out:]
        prod = lax.dot_general(a_ref[...].astype(BF16), b_ref[...].astype(BF16), contract,
                               preferred_element_type=F32)

        def finish(val):
            if add_ref is not None:
                val = add_ref[...] + val
            o_refs[0][...] = val.astype(out_dtype)
            if bf16_copy:
                o_refs[1][...] = val.astype(BF16)

        if nk == 1:
            finish(prod)
        else:
            acc_ref = scratch[0]
            k = pl.program_id(2)

            @pl.when(k == 0)
            def _():
                acc_ref[...] = prod

            @pl.when((k > 0) & (k < nk - 1))
            def _():
                acc_ref[...] += prod

            @pl.when(k == nk - 1)
            def _():
                finish(acc_ref[...] + prod)

    in_specs = [a_spec, b_spec]
    args = [a, b]
    if add is not None:
        in_specs.append(o_spec)
        args.append(add)
    if after is not None:
        in_specs.append(pl.BlockSpec(memory_space=pl.ANY))
        args.append(after)
    out_shapes = [jax.ShapeDtypeStruct(out_shape, out_dtype)]
    if bf16_copy:
        out_shapes.append(jax.ShapeDtypeStruct(out_shape, BF16))
    res = pl.pallas_call(
        body, name=name, grid=grid, in_specs=in_specs, out_specs=[o_spec] * n_out, out_shape=out_shapes,
        scratch_shapes=[pltpu.VMEM(acc_shape, F32)] if nk > 1 else [],
        compiler_params=_arb(3),
    )(*args)
    return res if bf16_copy else res[0]


def _bs(block, fn):
    return pl.BlockSpec(block, fn)


def _rmsnorm_fwd(name, src, g, d, cb, ts=512):
    s = src.shape[0]

    def body(x_ref, g_ref, h_ref, r_ref):
        x = x_ref[...]
        r = lax.rsqrt(jnp.mean(x * x, axis=-1, keepdims=True) + EPS)
        h_ref[...] = (x * r * g_ref[...]).astype(BF16)
        r_ref[...] = r

    return pl.pallas_call(
        body, name=name, grid=(s // ts,),
        in_specs=[_bs((ts, d), lambda i: (i, cb)), _bs((1, d), lambda i: (0, 0))],
        out_specs=[_bs((ts, d), lambda i: (i, 0)), _bs((ts, 1), lambda i: (i, 0))],
        out_shape=[jax.ShapeDtypeStruct((s, d), BF16), jax.ShapeDtypeStruct((s, 1), F32)],
        compiler_params=_arb(1),
    )(src, g)


def _rmsnorm_bwd(name, dy, src, rstd, g, d, cb, out_dtype, res=None, bf16_copy=False, ts=512):
    s = src.shape[0]

    def body(*refs):
        dy_ref, x_ref, r_ref, g_ref = refs[:4]
        res_ref = refs[4] if res is not None else None
        dx_ref, dg_ref = refs[n_in:n_in + 2]
        dyv = dy_ref[...]
        r = r_ref[...]
        xhat = x_ref[...] * r
        dyh = dyv * g_ref[...]
        c = jnp.mean(dyh * xhat, axis=-1, keepdims=True)
        dx = r * (dyh - xhat * c)
        if res_ref is not None:
            dx = res_ref[...] + dx
        dx_ref[...] = dx.astype(out_dtype)
        if bf16_copy:
            refs[n_in + 2][...] = dx.astype(BF16)
        part = jnp.sum(dyv * xhat, axis=0, keepdims=True)

        @pl.when(pl.program_id(0) == 0)
        def _():
            dg_ref[...] = part

        @pl.when(pl.program_id(0) > 0)
        def _():
            dg_ref[...] += part

    in_specs = [_bs((ts, d), lambda i: (i, 0)), _bs((ts, d), lambda i: (i, cb)),
                _bs((ts, 1), lambda i: (i, 0)), _bs((1, d), lambda i: (0, 0))]
    args = [dy, src, rstd, g]
    if res is not None:
        in_specs.append(_bs((ts, d), lambda i: (i, 0)))
        args.append(res)
    n_in = len(args)
    out_specs = [_bs((ts, d), lambda i: (i, 0)), _bs((1, d), lambda i: (0, 0))]
    out_shape = [jax.ShapeDtypeStruct((s, d), out_dtype), jax.ShapeDtypeStruct((1, d), F32)]
    if bf16_copy:
        out_specs.append(_bs((ts, d), lambda i: (i, 0)))
        out_shape.append(jax.ShapeDtypeStruct((s, d), BF16))
    return pl.pallas_call(
        body, name=name, grid=(s // ts,), in_specs=in_specs, out_specs=out_specs, out_shape=out_shape,
        compiler_params=_arb(1),
    )(*args)


def _matmul_norm_bwd(name, a, b, *, grid, a_spec, b_spec, x, rstd, g, res, bf16_copy=False, after=None):
    s, d = x.shape
    ni, nk = grid
    assert nk >= 2, "the first and the last contraction step are distinct branches"
    tm = s // ni
    n_in = 6 + (after is not None)

    def body(*refs):
        a_ref, b_ref, x_ref, r_ref, g_ref, res_ref = refs[:6]
        dx_ref, dg_ref = refs[n_in:n_in + 2]
        acc_ref = refs[-1]
        k = pl.program_id(1)
        prod = lax.dot_general(a_ref[...], b_ref[...], NT, preferred_element_type=F32)

        @pl.when(k == 0)
        def _():
            acc_ref[...] = prod

        @pl.when((k > 0) & (k < nk - 1))
        def _():
            acc_ref[...] += prod

        @pl.when(k == nk - 1)
        def _():
            dyv = acc_ref[...] + prod
            r = r_ref[...]
            xhat = x_ref[...] * r
            dyh = dyv * g_ref[...]
            c = jnp.mean(dyh * xhat, axis=-1, keepdims=True)
            dx = res_ref[...] + r * (dyh - xhat * c)
            dx_ref[...] = dx
            if bf16_copy:
                refs[n_in + 2][...] = dx.astype(BF16)
            part = jnp.sum(dyv * xhat, axis=0, keepdims=True)

            @pl.when(pl.program_id(0) == 0)
            def _():
                dg_ref[...] = part

            @pl.when(pl.program_id(0) > 0)
            def _():
                dg_ref[...] += part

    rows = _bs((tm, d), lambda i, k: (i, 0))
    in_specs = [a_spec, b_spec, rows, _bs((tm, 1), lambda i, k: (i, 0)), _bs((1, d), lambda i, k: (0, 0)), rows]
    args = [a, b, x, rstd, g, res]
    if after is not None:
        in_specs.append(pl.BlockSpec(memory_space=pl.ANY))
        args.append(after)
    out_specs = [rows, _bs((1, d), lambda i, k: (0, 0))]
    out_shape = [jax.ShapeDtypeStruct((s, d), F32), jax.ShapeDtypeStruct((1, d), F32)]
    if bf16_copy:
        out_specs.append(rows)
        out_shape.append(jax.ShapeDtypeStruct((s, d), BF16))
    return pl.pallas_call(
        body, name=name, grid=grid, in_specs=in_specs, out_specs=out_specs, out_shape=out_shape,
        scratch_shapes=[pltpu.VMEM((tm, d), F32)], compiler_params=_arb(2),
    )(*args)


def _ffn_down_loss(f, w_down, x1, target, g, ts=512):
    s, d = x1.shape
    dff = f.shape[1]

    def body(f_ref, w_ref, x_ref, t_ref, g_ref, loss_ref, dx_ref, dg_ref, dxb_ref):
        x = x_ref[...] + jnp.dot(f_ref[...], w_ref[...], preferred_element_type=F32)
        r = lax.rsqrt(jnp.mean(x * x, axis=-1, keepdims=True) + EPS)
        xhat = x * r
        gv = g_ref[...]
        err = xhat * gv - t_ref[...]
        lpart = 0.5 * jnp.sum(jnp.mean(err * err, axis=-1, keepdims=True), axis=0, keepdims=True)
        dyv = err * (1.0 / d)
        dyh = dyv * gv
        c = jnp.mean(dyh * xhat, axis=-1, keepdims=True)
        dx = r * (dyh - xhat * c)
        dx_ref[...] = dx
        dxb_ref[...] = dx.astype(BF16)
        gpart = jnp.sum(dyv * xhat, axis=0, keepdims=True)

        @pl.when(pl.program_id(0) == 0)
        def _():
            dg_ref[...] = gpart
            loss_ref[...] = lpart

        @pl.when(pl.program_id(0) > 0)
        def _():
            dg_ref[...] += gpart
            loss_ref[...] += lpart

    rows = _bs((ts, d), lambda i: (i, 0))
    return pl.pallas_call(
        body, name="ffn_down_loss", grid=(s // ts,),
        in_specs=[_bs((ts, dff), lambda i: (i, 0)), _bs((dff, d), lambda i: (0, 0)), rows, rows,
                  _bs((1, d), lambda i: (0, 0))],
        out_specs=[_bs((1, 1), lambda i: (0, 0)), rows, _bs((1, d), lambda i: (0, 0)), rows],
        out_shape=[jax.ShapeDtypeStruct((1, 1), F32), jax.ShapeDtypeStruct((s, d), F32),
                   jax.ShapeDtypeStruct((1, d), F32), jax.ShapeDtypeStruct((s, d), BF16)],
        compiler_params=_arb(1),
    )(f, w_down, x1, target, g)


def _swap_halves(t):
    lane = lax.broadcasted_iota(jnp.int32, t.shape, 1)
    return jnp.where(lane < 32, pltpu.roll(t, 96, 1), pltpu.roll(t, 32, 1))


def _rope_fwd(t, cos_t, sin_t):
    return t * cos_t + _swap_halves(t) * sin_t


def _rope_bwd(dt, cos_t, sin_t):
    return dt * cos_t - _swap_halves(dt) * sin_t


def _lat_norms(proj, gq, gkv, ts=512):
    s = proj.shape[0]

    def body(q_ref, c_ref, gq_ref, gkv_ref, qn_ref, cn_ref, rq_ref, rc_ref):
        q = q_ref[...]
        rq = lax.rsqrt(jnp.mean(q * q, axis=-1, keepdims=True) + EPS)
        qn_ref[...] = (q * rq * gq_ref[...]).astype(BF16)
        rq_ref[...] = rq
        cv = c_ref[...]
        rc = lax.rsqrt(jnp.mean(cv * cv, axis=-1, keepdims=True) + EPS)
        cn_ref[...] = (cv * rc * gkv_ref[...]).astype(BF16)
        rc_ref[...] = rc

    return pl.pallas_call(
        body, name="lat_norms", grid=(s // ts,),
        in_specs=[_bs((ts, Q_LORA), lambda i: (i, P_QLAT // Q_LORA)),
                  _bs((ts, KV_LORA), lambda i: (i, P_CKV // KV_LORA)),
                  _bs((1, Q_LORA), lambda i: (0, 0)), _bs((1, KV_LORA), lambda i: (0, 0))],
        out_specs=[_bs((ts, Q_LORA), lambda i: (i, 0)), _bs((ts, KV_LORA), lambda i: (i, 0)),
                   _bs((ts, 1), lambda i: (i, 0)), _bs((ts, 1), lambda i: (i, 0))],
        out_shape=[jax.ShapeDtypeStruct((s, Q_LORA), BF16), jax.ShapeDtypeStruct((s, KV_LORA), BF16),
                   jax.ShapeDtypeStruct((s, 1), F32), jax.ShapeDtypeStruct((s, 1), F32)],
        compiler_params=_arb(1),
    )(proj, proj, gq, gkv)


HEAD_ROWS = 2048
DW_ROWS = 2048
MM_ROWS = 1024


def _q_heads(qn, wq, cos_t, sin_t):
    s = qn.shape[0]
    ts = min(s, HEAD_ROWS)

    def body(qn_ref, w_ref, cos_ref, sin_ref, q_ref):
        o = jnp.dot(qn_ref[...], w_ref[...], preferred_element_type=F32)
        q_ref[:, :LANES] = o[:, :LANES].astype(BF16)
        q_ref[:, LANES:] = _rope_fwd(o[:, LANES:], cos_ref[...], sin_ref[...]).astype(BF16)

    return pl.pallas_call(
        body, name="q_heads", grid=(s // ts, H_A),
        in_specs=[_bs((ts, Q_LORA), lambda i, h: (i, 0)), _bs((None, Q_LORA, HEAD_PAD), lambda i, h: (h, 0, 0)),
                  _bs((ts, LANES), lambda i, h: (i, 0)), _bs((ts, LANES), lambda i, h: (i, 0))],
        out_specs=_bs((None, ts, HEAD_PAD), lambda i, h: (h, i, 0)),
        out_shape=jax.ShapeDtypeStruct((H_A, s, HEAD_PAD), BF16),
        compiler_params=_arb(2),
    )(qn, wq, cos_t, sin_t)


def _kv_heads(cn, wkv, proj, cos_t, sin_t):
    s = cn.shape[0]
    ts = min(s, HEAD_ROWS)

    def body(cn_ref, w_ref, kr_ref, cos_ref, sin_ref, k_ref, v_ref):
        o = jnp.dot(cn_ref[...], w_ref[...], preferred_element_type=F32)
        k_ref[:, :LANES] = o[:, :LANES].astype(BF16)
        k_ref[:, LANES:] = _rope_fwd(kr_ref[...], cos_ref[...], sin_ref[...]).astype(BF16)
        v_ref[...] = o[:, LANES:].astype(BF16)

    return pl.pallas_call(
        body, name="kv_heads", grid=(s // ts, H_A),
        in_specs=[_bs((ts, KV_LORA), lambda i, h: (i, 0)),
                  _bs((None, KV_LORA, QK_NOPE + V_DIM), lambda i, h: (h, 0, 0)),
                  _bs((ts, LANES), lambda i, h: (i, P_KR // LANES)),
                  _bs((ts, LANES), lambda i, h: (i, 0)), _bs((ts, LANES), lambda i, h: (i, 0))],
        out_specs=[_bs((None, ts, HEAD_PAD), lambda i, h: (h, i, 0)), _bs((None, ts, V_DIM), lambda i, h: (h, i, 0))],
        out_shape=[jax.ShapeDtypeStruct((H_A, s, HEAD_PAD), BF16), jax.ShapeDtypeStruct((H_A, s, V_DIM), BF16)],
        compiler_params=_arb(2),
    )(cn, wkv, proj, cos_t, sin_t)


MLA_SCALE = 1.0 / math.sqrt(QK_NOPE + QK_ROPE)
LOG2E = math.log2(math.e)
MLA_EXP2_SCALE = MLA_SCALE * LOG2E


def _lane_tiles(a):
    return [a[:, j * LANES:(j + 1) * LANES] for j in range(a.shape[1] // LANES)]


MLA_SUB = 512


def _mla_fwd(q, k, v, tq=1024, tk=1024):
    s = q.shape[1]
    tq = min(tq, s)
    nk = s // tk

    def body(q_ref, k_ref, v_ref, o_ref, lse_ref, m_ref, l_ref, acc_ref):
        m_ref[...] = jnp.full(m_ref.shape, -jnp.inf, F32)
        l_ref[...] = jnp.zeros(l_ref.shape, F32)
        acc_ref[...] = jnp.zeros(acc_ref.shape, F32)

        def step(c, carry):
            rows = pl.ds(pl.multiple_of(c * tk, tk), tk)
            for sub in range(tq // MLA_SUB):
                qr = slice(sub * MLA_SUB, (sub + 1) * MLA_SUB)
                raw = lax.dot_general(q_ref[qr, :], k_ref[rows, :], NT, preferred_element_type=F32)
                m_prev = m_ref[qr, :]
                m_new = jnp.maximum(m_prev, jnp.max(raw, axis=-1, keepdims=True))
                alpha = jnp.exp2((m_prev - m_new) * MLA_EXP2_SCALE)
                ps = [jnp.exp2((t - m_new) * MLA_EXP2_SCALE) for t in _lane_tiles(raw)]
                l_ref[qr, :] = alpha * l_ref[qr, :] + functools.reduce(lambda a, b: a + b, ps)
                p = jnp.concatenate(ps, axis=1).astype(BF16)
                acc_ref[qr, :] = alpha * acc_ref[qr, :] + jnp.dot(p, v_ref[rows, :], preferred_element_type=F32)
                m_ref[qr, :] = m_new
            return carry

        lax.fori_loop(0, nk, step, 0, unroll=True)
        l = jnp.sum(l_ref[...], axis=-1, keepdims=True)
        o_ref[...] = acc_ref[...] / l
        lse_ref[...] = m_ref[...] * MLA_SCALE + jnp.log(l)

    return pl.pallas_call(
        body, name="mla_fwd", grid=(H_A, s // tq),
        in_specs=[_bs((None, tq, HEAD_PAD), lambda h, i: (h, i, 0)),
                  _bs((None, s, HEAD_PAD), lambda h, i: (h, 0, 0)),
                  _bs((None, s, V_DIM), lambda h, i: (h, 0, 0))],
        out_specs=[_bs((tq, V_DIM), lambda h, i: (i, h)), _bs((None, tq, LANES), lambda h, i: (h, i, 0))],
        out_shape=[jax.ShapeDtypeStruct((s, H_A * V_DIM), F32), jax.ShapeDtypeStruct((H_A, s, LANES), F32)],
        scratch_shapes=[pltpu.VMEM((tq, LANES), F32), pltpu.VMEM((tq, LANES), F32), pltpu.VMEM((tq, V_DIM), F32)],
        compiler_params=_arb(2),
    )(q, k, v)


def _mla_bwd(q, k, v, do, o, lse, cos_t, sin_t, tq=512, tk=512):
    s = q.shape[1]
    nq = s // tq
    nkb = s // tk

    def body(q_ref, k_ref, v_ref, do_ref, o_ref, lse_ref, cos_ref, sin_ref, dqp_ref, dkvp_ref, dkr_ref,
             delta_ref, dq_ref, dk_ref, dv_ref):
        @pl.when(pl.program_id(1) == 0)
        def _():
            def init(c, carry):
                rows = pl.ds(pl.multiple_of(c * tq, tq), tq)
                delta = jnp.sum(do_ref[rows, :] * o_ref[rows, :], axis=-1, keepdims=True)
                delta_ref[rows, :] = jnp.broadcast_to(delta, (tq, LANES))
                dq_ref[rows, :] = jnp.zeros((tq, HEAD_PAD), F32)
                return carry

            lax.fori_loop(0, nq, init, 0)

        dk_ref[...] = jnp.zeros(dk_ref.shape, F32)
        dv_ref[...] = jnp.zeros(dv_ref.shape, F32)
        kb = k_ref[...]
        vb = v_ref[...]

        def step(c, carry):
            rows = pl.ds(pl.multiple_of(c * tq, tq), tq)
            qc = q_ref[rows, :]
            doc = do_ref[rows, :].astype(BF16)
            raw = lax.dot_general(qc, kb, NT, preferred_element_type=F32)
            dp = lax.dot_general(doc, vb, NT, preferred_element_type=F32)
            lse2 = lse_ref[rows, :] * LOG2E
            delta = delta_ref[rows, :]
            ps = [jnp.exp2(t * MLA_EXP2_SCALE - lse2) for t in _lane_tiles(raw)]
            dss = [pj * (dj - delta) * MLA_SCALE for pj, dj in zip(ps, _lane_tiles(dp))]
            p = jnp.concatenate(ps, axis=1).astype(BF16)
            ds = jnp.concatenate(dss, axis=1).astype(BF16)
            dv_ref[...] += lax.dot_general(p, doc, TN, preferred_element_type=F32)
            dk_ref[...] += lax.dot_general(ds, qc, TN, preferred_element_type=F32)
            dq_ref[rows, :] += jnp.dot(ds, kb, preferred_element_type=F32)
            return carry

        lax.fori_loop(0, nq, step, 0, unroll=True)
        dkvp_ref[:, :LANES] = dk_ref[:, :LANES].astype(BF16)
        dkvp_ref[:, LANES:] = dv_ref[...].astype(BF16)
        dkr_ref[...] = dk_ref[:, LANES:]

        @pl.when(pl.program_id(1) == nkb - 1)
        def _():
            def finish(c, carry):
                rows = pl.ds(pl.multiple_of(c * tq, tq), tq)
                dqp_ref[rows, :LANES] = dq_ref[rows, :LANES].astype(BF16)
                dqp_ref[rows, LANES:] = _rope_bwd(dq_ref[rows, LANES:], cos_ref[rows, :], sin_ref[rows, :]).astype(BF16)
                return carry

            lax.fori_loop(0, nq, finish, 0)

    whole = lambda w: _bs((s, w), lambda h, j: (0, 0))
    return pl.pallas_call(
        body, name="mla_bwd", grid=(H_A, nkb),
        in_specs=[_bs((None, s, HEAD_PAD), lambda h, j: (h, 0, 0)),
                  _bs((None, tk, HEAD_PAD), lambda h, j: (h, j, 0)),
                  _bs((None, tk, V_DIM), lambda h, j: (h, j, 0)),
                  _bs((s, V_DIM), lambda h, j: (0, h)), _bs((s, V_DIM), lambda h, j: (0, h)),
                  _bs((None, s, LANES), lambda h, j: (h, 0, 0)), whole(LANES), whole(LANES)],
        out_specs=[_bs((None, s, HEAD_PAD), lambda h, j: (h, 0, 0)),
                   _bs((None, tk, HEAD_PAD), lambda h, j: (h, j, 0)),
                   _bs((None, tk, LANES), lambda h, j: (h, j, 0))],
        out_shape=[jax.ShapeDtypeStruct((H_A, s, HEAD_PAD), BF16), jax.ShapeDtypeStruct((H_A, s, HEAD_PAD), BF16),
                   jax.ShapeDtypeStruct((H_A, s, LANES), F32)],
        scratch_shapes=[pltpu.VMEM((s, LANES), F32), pltpu.VMEM((s, HEAD_PAD), F32), pltpu.VMEM((tk, HEAD_PAD), F32),
                        pltpu.VMEM((tk, V_DIM), F32)],
        compiler_params=_arb(2),
    )(q, k, v, do, o, lse, cos_t, sin_t)


def _mla_key_rope_grad(dkr, cos_t, sin_t, ts=512):
    s = dkr.shape[1]

    def body(d_ref, cos_ref, sin_ref, o_ref):
        tot = d_ref[0]
        for h in range(1, H_A):
            tot = tot + d_ref[h]
        o_ref[...] = _rope_bwd(tot, cos_ref[...], sin_ref[...]).astype(BF16)

    rows = _bs((ts, LANES), lambda i: (i, 0))
    return pl.pallas_call(
        body, name="mla_key_rope_grad", grid=(s // ts,),
        in_specs=[_bs((H_A, ts, LANES), lambda i: (0, i, 0)), rows, rows], out_specs=rows,
        out_shape=jax.ShapeDtypeStruct((s, LANES), BF16), compiler_params=_arb(1),
    )(dkr, cos_t, sin_t)


WIN_SCALE = 1.0 / math.sqrt(HD_B)
SPAN = Q_BLOCK + 2 * WINDOW


def _t5_bucket_table():
    a = jnp.arange(Q_BLOCK, dtype=jnp.int32)[:, None]
    c = jnp.arange(SPAN, dtype=jnp.int32)[None, :]
    rel = c - WINDOW - a
    nb = NUM_BUCKETS // 2
    max_exact = nb // 2
    base = (rel > 0).astype(jnp.int32) * nb
    n = jnp.abs(rel)
    nf = jnp.maximum(n, 1).astype(F32)
    large = max_exact + (jnp.log(nf / max_exact) / math.log(MAX_DISTANCE / max_exact)
                         * (nb - max_exact)).astype(jnp.int32)
    large = jnp.minimum(large, nb - 1)
    return base + jnp.where(n < max_exact, n, large)


def _win_bias(bucket, rel_bias):
    def body(rb_ref, bk_ref, o_ref):
        h = pl.program_id(0)
        bk = bk_ref[...]
        acc = jnp.zeros((Q_BLOCK, SPAN), F32)
        for b in range(NUM_BUCKETS):
            acc = jnp.where(bk == b, rb_ref[b, h], acc)
        o_ref[...] = acc

    return pl.pallas_call(
        body, name="win_bias", grid=(H_B,),
        in_specs=[pl.BlockSpec(memory_space=pltpu.SMEM), _bs((Q_BLOCK, SPAN), lambda h: (0, 0))],
        out_specs=_bs((None, Q_BLOCK, SPAN), lambda h: (h, 0, 0)),
        out_shape=jax.ShapeDtypeStruct((H_B, Q_BLOCK, SPAN), F32),
        compiler_params=_arb(1),
    )(rel_bias, bucket)


WIN_HEADS = GROUP


def _win_kv_rows(n, j, nblk):
    blk = jnp.clip(n + j - 1, 0, nblk - 1)
    return pl.ds(pl.multiple_of(blk * Q_BLOCK, Q_BLOCK), Q_BLOCK)


def _win_kv_cols(h0):
    kv = h0 // GROUP
    return slice(kv * HD_B, (kv + 1) * HD_B)


def _win_stack(ref, h0):
    return jnp.concatenate([ref[:, (h0 + g) * HD_B:(h0 + g + 1) * HD_B] for g in range(WIN_HEADS)], axis=0)


def _win_unstack(ref, h0, val):
    for g in range(WIN_HEADS):
        ref[:, (h0 + g) * HD_B:(h0 + g + 1) * HD_B] = val[g * Q_BLOCK:(g + 1) * Q_BLOCK].astype(ref.dtype)


def _win_scores(q, k_ref, h0, bias_ref, n, nblk):
    a = lax.broadcasted_iota(jnp.int32, (WIN_HEADS, Q_BLOCK, Q_BLOCK), 1)
    cc = lax.broadcasted_iota(jnp.int32, (WIN_HEADS, Q_BLOCK, Q_BLOCK), 2)
    valid = [(cc >= a) & (n > 0), None, (cc <= a) & (n < nblk - 1)]
    out = []
    for j in range(3):
        sc = lax.dot_general(q, k_ref[_win_kv_rows(n, j, nblk), _win_kv_cols(h0)], NT, preferred_element_type=F32)
        sc = (sc.reshape(WIN_HEADS, Q_BLOCK, Q_BLOCK) * WIN_SCALE
              + bias_ref[h0:h0 + WIN_HEADS, :, j * Q_BLOCK:(j + 1) * Q_BLOCK])
        if valid[j] is not None:
            sc = jnp.where(valid[j], sc, -1e30)
        out.append(sc)
    return out


def _win_sink(sink_ref, h0):
    hs = lax.broadcasted_iota(jnp.int32, (WIN_HEADS, Q_BLOCK, 1), 0)
    sk = jnp.zeros((WIN_HEADS, Q_BLOCK, 1), F32)
    for g in range(WIN_HEADS):
        sk = jnp.where(hs == g, sink_ref[h0 + g], sk)
    return sk


def _win_fwd(proj_b, bias, sinks):
    s = proj_b.shape[0]
    nblk = s // Q_BLOCK
    rows = WIN_HEADS * Q_BLOCK

    def body(sink_ref, q_ref, k_ref, v_ref, bias_ref, o_ref, lse_ref):
        n = pl.program_id(0)
        for h0 in range(0, H_B, WIN_HEADS):
            sk = _win_sink(sink_ref, h0)
            q = _win_stack(q_ref, h0)
            ss = _win_scores(q, k_ref, h0, bias_ref, n, nblk)
            m = jnp.maximum(jnp.max(jnp.maximum(jnp.maximum(ss[0], ss[1]), ss[2]), axis=2, keepdims=True), sk)
            es = [jnp.exp(sc - m) for sc in ss]
            l = jnp.sum(es[0] + es[1] + es[2], axis=2, keepdims=True) + jnp.exp(sk - m)
            acc = jnp.zeros((rows, HD_B), F32)
            for j, e in enumerate(es):
                p = (e / l).astype(BF16).reshape(rows, Q_BLOCK)
                acc = acc + jnp.dot(p, v_ref[_win_kv_rows(n, j, nblk), _win_kv_cols(h0)],
                                    preferred_element_type=F32)
            _win_unstack(o_ref, h0, acc)
            lse_ref[h0:h0 + WIN_HEADS] = m + jnp.log(l)

    kv_w = KV_B * HD_B
    return pl.pallas_call(
        body, name="win_fwd", grid=(nblk,),
        in_specs=[pl.BlockSpec(memory_space=pltpu.SMEM), _bs((Q_BLOCK, H_B * HD_B), lambda n: (n, P_QB // (H_B * HD_B))),
                  _bs((s, kv_w), lambda n: (0, P_KB // kv_w)), _bs((s, kv_w), lambda n: (0, P_VB // kv_w)),
                  _bs((H_B, Q_BLOCK, SPAN), lambda n: (0, 0, 0))],
        out_specs=[_bs((Q_BLOCK, H_B * HD_B), lambda n: (n, 0)), _bs((H_B, Q_BLOCK, 1), lambda n: (0, n, 0))],
        out_shape=[jax.ShapeDtypeStruct((s, H_B * HD_B), F32), jax.ShapeDtypeStruct((H_B, s, 1), F32)],
        compiler_params=_arb(1),
    )(sinks, proj_b, proj_b, proj_b, bias)


def _win_bwd(proj_b, bias, sinks, do_b, lse):
    s = proj_b.shape[0]
    nblk = s // Q_BLOCK
    rows = WIN_HEADS * Q_BLOCK
    spad = s + 2 * WINDOW

    def body(sink_ref, q_ref, k_ref, v_ref, bias_ref, do_ref, lse_ref, dq_ref, dk_ref, dv_ref, db_ref, dsk_ref):
        n = pl.program_id(0)

        @pl.when(n == 0)
        def _():
            dk_ref[...] = jnp.zeros(dk_ref.shape, F32)
            dv_ref[...] = jnp.zeros(dv_ref.shape, F32)
            db_ref[...] = jnp.zeros(db_ref.shape, F32)
            dsk_ref[...] = jnp.zeros(dsk_ref.shape, F32)

        for h0 in range(0, H_B, WIN_HEADS):
            heads = slice(h0, h0 + WIN_HEADS)
            sk = _win_sink(sink_ref, h0)
            q = _win_stack(q_ref, h0)
            dob = _win_stack(do_ref, h0)
            lse_v = lse_ref[heads]
            ss = _win_scores(q, k_ref, h0, bias_ref, n, nblk)
            ps = [jnp.exp(sc - lse_v) for sc in ss]
            dps = [lax.dot_general(dob, v_ref[_win_kv_rows(n, j, nblk), _win_kv_cols(h0)], NT,
                                   preferred_element_type=F32).reshape(WIN_HEADS, Q_BLOCK, Q_BLOCK) for j in range(3)]
            delta = jnp.sum(ps[0] * dps[0] + ps[1] * dps[1] + ps[2] * dps[2], axis=2, keepdims=True)
            dq = jnp.zeros((rows, HD_B), F32)
            for j in range(3):
                ds = ps[j] * (dps[j] - delta)
                db_ref[heads, :, j * Q_BLOCK:(j + 1) * Q_BLOCK] += ds
                dsb = (ds * WIN_SCALE).astype(BF16).reshape(rows, Q_BLOCK)
                dq = dq + jnp.dot(dsb, k_ref[_win_kv_rows(n, j, nblk), _win_kv_cols(h0)],
                                  preferred_element_type=F32)
                krows = pl.ds(pl.multiple_of((n + j) * Q_BLOCK, Q_BLOCK), Q_BLOCK)
                dk_ref[krows, _win_kv_cols(h0)] += lax.dot_general(dsb, q, TN, preferred_element_type=F32)
                dv_ref[krows, _win_kv_cols(h0)] += lax.dot_general(
                    ps[j].astype(BF16).reshape(rows, Q_BLOCK), dob, TN, preferred_element_type=F32)
            dsk_ref[heads] += -(jnp.exp(sk - lse_v) * delta)
            _win_unstack(dq_ref, h0, dq)

    kv_w = KV_B * HD_B
    qspec = _bs((Q_BLOCK, H_B * HD_B), lambda n: (n, 0))
    kacc = _bs((spad, kv_w), lambda n: (0, 0))
    return pl.pallas_call(
        body, name="win_bwd", grid=(nblk,),
        in_specs=[pl.BlockSpec(memory_space=pltpu.SMEM), _bs((Q_BLOCK, H_B * HD_B), lambda n: (n, P_QB // (H_B * HD_B))),
                  _bs((s, kv_w), lambda n: (0, P_KB // kv_w)), _bs((s, kv_w), lambda n: (0, P_VB // kv_w)),
                  _bs((H_B, Q_BLOCK, SPAN), lambda n: (0, 0, 0)), qspec, _bs((H_B, Q_BLOCK, 1), lambda n: (0, n, 0))],
        out_specs=[qspec, kacc, kacc, _bs((H_B, Q_BLOCK, SPAN), lambda n: (0, 0, 0)),
                   _bs((H_B, Q_BLOCK, 1), lambda n: (0, 0, 0))],
        out_shape=[jax.ShapeDtypeStruct((s, H_B * HD_B), BF16), jax.ShapeDtypeStruct((spad, kv_w), F32),
                   jax.ShapeDtypeStruct((spad, kv_w), F32), jax.ShapeDtypeStruct((H_B, Q_BLOCK, SPAN), F32),
                   jax.ShapeDtypeStruct((H_B, Q_BLOCK, 1), F32)],
        compiler_params=_arb(1),
    )(sinks, proj_b, proj_b, proj_b, bias, do_b, lse)


def _win_param_grads(bucket, dbias, dsink_rows):
    def body(bk_ref, db_ref, ds_ref, o_ref):
        bk = bk_ref[...]
        dbv = db_ref[...]
        lane = lax.broadcasted_iota(jnp.int32, (1, LANES), 1)
        res = jnp.zeros((1, LANES), F32)
        for b in range(NUM_BUCKETS):
            tot = jnp.sum(jnp.sum(jnp.where(bk == b, dbv, 0.0), axis=1, keepdims=True), axis=0, keepdims=True)
            res = jnp.where(lane == b, tot, res)
        stot = jnp.sum(ds_ref[...], axis=0, keepdims=True)
        o_ref[...] = jnp.where(lane == NUM_BUCKETS, stot, res)

    return pl.pallas_call(
        body, name="win_param_grads", grid=(H_B,),
        in_specs=[_bs((Q_BLOCK, SPAN), lambda h: (0, 0)), _bs((None, Q_BLOCK, SPAN), lambda h: (h, 0, 0)),
                  _bs((None, Q_BLOCK, 1), lambda h: (h, 0, 0))],
        out_specs=_bs((None, 1, LANES), lambda h: (h, 0, 0)),
        out_shape=jax.ShapeDtypeStruct((H_B, 1, LANES), F32),
        compiler_params=_arb(1),
    )(bucket, dbias, dsink_rows)


def _mix_out_norm(proj, o_a, o_b, w_out, x, g2, ts=512):
    s = o_a.shape[0]
    wide = lambda cb: _bs((ts, D_MODEL), lambda i: (i, cb))

    def body(ga_ref, gb_ref, oa_ref, ob_ref, w_ref, x_ref, g_ref, m_ref, x1_ref, h_ref, r_ref):
        mixed = (jax.nn.sigmoid(ga_ref[...]) * oa_ref[...] + jax.nn.sigmoid(gb_ref[...]) * ob_ref[...]).astype(BF16)
        m_ref[...] = mixed
        x1 = x_ref[...] + jnp.dot(mixed, w_ref[...], preferred_element_type=F32)
        x1_ref[...] = x1
        r = lax.rsqrt(jnp.mean(x1 * x1, axis=-1, keepdims=True) + EPS)
        h_ref[...] = (x1 * r * g_ref[...]).astype(BF16)
        r_ref[...] = r

    return pl.pallas_call(
        body, name="mix_out_norm", grid=(s // ts,),
        in_specs=[wide(P_GA // D_MODEL), wide(P_GB // D_MODEL), wide(0), wide(0),
                  _bs((D_MODEL, D_MODEL), lambda i: (0, 0)), wide(0), _bs((1, D_MODEL), lambda i: (0, 0))],
        out_specs=[wide(0), wide(0), wide(0), _bs((ts, 1), lambda i: (i, 0))],
        out_shape=[jax.ShapeDtypeStruct((s, D_MODEL), BF16), jax.ShapeDtypeStruct((s, D_MODEL), F32),
                   jax.ShapeDtypeStruct((s, D_MODEL), BF16), jax.ShapeDtypeStruct((s, 1), F32)],
        compiler_params=_arb(1),
    )(proj, proj, o_a, o_b, w_out, x, g2)


def _mix_out_bwd(dx1_b, w_out, proj, o_a, o_b, after=None, ts=512):
    s = o_a.shape[0]
    wide = lambda cb: _bs((ts, D_MODEL), lambda i: (i, cb))

    def body(dx_ref, w_ref, ga_ref, gb_ref, oa_ref, ob_ref, *rest):
        doa_ref, dob_ref, dga_ref, dgb_ref = rest[-4:]
        dm = lax.dot_general(dx_ref[...], w_ref[...], NT, preferred_element_type=F32)
        sa = jax.nn.sigmoid(ga_ref[...])
        sb = jax.nn.sigmoid(gb_ref[...])
        doa_ref[...] = dm * sa
        dob_ref[...] = (dm * sb).astype(BF16)
        dga_ref[...] = (dm * oa_ref[...] * (sa * (1.0 - sa))).astype(BF16)
        dgb_ref[...] = (dm * ob_ref[...] * (sb * (1.0 - sb))).astype(BF16)

    extra = [after] if after is not None else []
    return pl.pallas_call(
        body, name="mix_out_bwd", grid=(s // ts,),
        in_specs=[wide(0), _bs((D_MODEL, D_MODEL), lambda i: (0, 0)), wide(P_GA // D_MODEL), wide(P_GB // D_MODEL),
                  wide(0), wide(0)] + [pl.BlockSpec(memory_space=pl.ANY)] * len(extra),
        out_specs=[wide(0)] * 4,
        out_shape=[jax.ShapeDtypeStruct((s, D_MODEL), F32), jax.ShapeDtypeStruct((s, D_MODEL), BF16),
                   jax.ShapeDtypeStruct((s, D_MODEL), BF16), jax.ShapeDtypeStruct((s, D_MODEL), BF16)],
        compiler_params=_arb(1),
    )(dx1_b, w_out, proj, proj, o_a, o_b, *extra)


CONV_CHUNK = 128
N_SLAB = D_FF // LANES


def _shifted(ref, c, nchunks):
    r0 = c * CONV_CHUNK
    cur = ref[r0:r0 + CONV_CHUNK, :]
    row = lax.broadcasted_iota(jnp.int32, (8, LANES), 0)
    if c > 0:
        prev = ref[r0 - 1:r0 - 1 + CONV_CHUNK, :]
    else:
        down = pltpu.roll(cur, 1, 0)
        prev = jnp.concatenate([jnp.where(row == 0, 0.0, down[:8]), down[8:]], axis=0)
    if c < nchunks - 1:
        nxt = ref[r0 + 1:r0 + 1 + CONV_CHUNK, :]
    else:
        up = pltpu.roll(cur, CONV_CHUNK - 1, 0)
        nxt = jnp.concatenate([up[:-8], jnp.where(row == 7, 0.0, up[-8:])], axis=0)
    return prev, cur, nxt


def _conv_taps(ref, w_ref, b_ref, c, nchunks):
    prev, cur, nxt = _shifted(ref, c, nchunks)
    conv = prev * w_ref[0:1, :] + cur * w_ref[1:2, :] + nxt * w_ref[2:3, :] + b_ref[...]
    return conv, prev, cur, nxt


def _convffn_fwd(u, conv_w, conv_b):
    s = u.shape[0]
    nchunks = s // CONV_CHUNK

    def body(ug_ref, uv_ref, wg_ref, wv_ref, bg_ref, bv_ref, f_ref):
        for c in range(nchunks):
            cg = _conv_taps(ug_ref, wg_ref, bg_ref, c, nchunks)[0]
            cv = _conv_taps(uv_ref, wv_ref, bv_ref, c, nchunks)[0]
            f_ref[c * CONV_CHUNK:(c + 1) * CONV_CHUNK, :] = (cg * jax.nn.sigmoid(cg) * cv).astype(BF16)

    slab = lambda off: _bs((s, LANES), lambda j: (0, off + j))
    wsl = lambda off: _bs((3, LANES), lambda j: (0, off + j))
    bsl = lambda off: _bs((1, LANES), lambda j: (0, off + j))
    return pl.pallas_call(
        body, name="convffn_fwd", grid=(N_SLAB,),
        in_specs=[slab(0), slab(N_SLAB), wsl(0), wsl(N_SLAB), bsl(0), bsl(N_SLAB)],
        out_specs=slab(0), out_shape=jax.ShapeDtypeStruct((s, D_FF), BF16),
        compiler_params=_arb(1),
    )(u, u, conv_w, conv_w, conv_b, conv_b)


def _convffn_bwd(u, conv_w, conv_b, df):
    s = u.shape[0]
    nchunks = s // CONV_CHUNK

    def body(ug_ref, uv_ref, wg_ref, wv_ref, bg_ref, bv_ref, df_ref, du_ref, dw_ref, db_ref, dcg_ref, dcv_ref):
        dwg = [jnp.zeros((1, LANES), F32) for _ in range(3)]
        dwv = [jnp.zeros((1, LANES), F32) for _ in range(3)]
        dbg = jnp.zeros((1, LANES), F32)
        dbv = jnp.zeros((1, LANES), F32)
        for c in range(nchunks):
            rows = slice(c * CONV_CHUNK, (c + 1) * CONV_CHUNK)
            cg, gp, gc, gn = _conv_taps(ug_ref, wg_ref, bg_ref, c, nchunks)
            cv, vp, vc, vn = _conv_taps(uv_ref, wv_ref, bv_ref, c, nchunks)
            dfv = df_ref[rows, :]
            sg = jax.nn.sigmoid(cg)
            dcg = dfv * cv * (sg * (1.0 + cg * (1.0 - sg)))
            dcv = dfv * (cg * sg)
            dcg_ref[rows, :] = dcg
            dcv_ref[rows, :] = dcv
            for t, (tg, tv) in enumerate(((gp, vp), (gc, vc), (gn, vn))):
                dwg[t] = dwg[t] + jnp.sum(tg * dcg, axis=0, keepdims=True)
                dwv[t] = dwv[t] + jnp.sum(tv * dcv, axis=0, keepdims=True)
            dbg = dbg + jnp.sum(dcg, axis=0, keepdims=True)
            dbv = dbv + jnp.sum(dcv, axis=0, keepdims=True)
        for t in range(3):
            dw_ref[0, t:t + 1, :] = dwg[t]
            dw_ref[1, t:t + 1, :] = dwv[t]
        db_ref[0] = dbg
        db_ref[1] = dbv
        for half, (dc_ref, w_ref) in enumerate(((dcg_ref, wg_ref), (dcv_ref, wv_ref))):
            for c in range(nchunks):
                prev, cur, nxt = _shifted(dc_ref, c, nchunks)
                du = nxt * w_ref[0:1, :] + cur * w_ref[1:2, :] + prev * w_ref[2:3, :]
                du_ref[half, c * CONV_CHUNK:(c + 1) * CONV_CHUNK, :] = du.astype(BF16)

    slab = lambda off: _bs((s, LANES), lambda j: (0, off + j))
    wsl = lambda off: _bs((3, LANES), lambda j: (0, off + j))
    bsl = lambda off: _bs((1, LANES), lambda j: (0, off + j))
    return pl.pallas_call(
        body, name="convffn_bwd", grid=(N_SLAB,),
        in_specs=[slab(0), slab(N_SLAB), wsl(0), wsl(N_SLAB), bsl(0), bsl(N_SLAB), slab(0)],
        out_specs=[_bs((2, s, LANES), lambda j: (0, 0, j)), _bs((2, 3, LANES), lambda j: (0, 0, j)),
                   _bs((2, 1, LANES), lambda j: (0, 0, j))],
        out_shape=[jax.ShapeDtypeStruct((2, s, D_FF), BF16), jax.ShapeDtypeStruct((2, 3, D_FF), F32),
                   jax.ShapeDtypeStruct((2, 1, D_FF), F32)],
        scratch_shapes=[pltpu.VMEM((s, LANES), F32), pltpu.VMEM((s, LANES), F32)],
        compiler_params=_arb(1),
    )(u, u, conv_w, conv_w, conv_b, conv_b, df)


def _row_tile(rows, limit=512):
    best = rows
    for t in range(8, min(rows, limit) + 1, 8):
        if rows % t == 0:
            best = t
    return best if rows % 8 == 0 else rows


ADAM_C1 = 1.0 - ADAM_B1 ** ADAM_STEP
ADAM_C2 = 1.0 - ADAM_B2 ** ADAM_STEP


def _adamw_math(w, gv, m, v):
    nm = ADAM_B1 * m + (1.0 - ADAM_B1) * gv
    nv = ADAM_B2 * v + (1.0 - ADAM_B2) * (gv * gv)
    m_hat = nm / ADAM_C1
    v_hat = nv / ADAM_C2
    return -ADAM_LR * (m_hat / (jnp.sqrt(v_hat) + ADAM_EPS) + ADAM_WD * w), nm, nv


def _adamw_halves(name, core, w, mine, theirs, m, v, transposed=False):
    half, cols = (mine.shape[1], mine.shape[0]) if transposed else mine.shape
    tr = _row_tile(cols if transposed else half)
    nr = (cols if transposed else half) // tr

    def body(core_ref, w_ref, mine_ref, theirs_ref, m_ref, v_ref, g_ref, d_ref, nm_ref, nv_ref):
        gv = jnp.where(pl.program_id(0) == core_ref[0], mine_ref[...], theirs_ref[...])
        g_ref[...] = gv
        d_ref[...], nm_ref[...], nv_ref[...] = _adamw_math(w_ref[...], gv, m_ref[...], v_ref[...])

    if transposed:
        full = pl.BlockSpec((tr, half), lambda hf, r, cr: (r, hf))
        part = pl.BlockSpec((tr, half), lambda hf, r, cr: (r, 0))
        shape = (cols, 2 * half)
    else:
        full = pl.BlockSpec((tr, cols), lambda hf, r, cr: (hf * nr + r, 0))
        part = pl.BlockSpec((tr, cols), lambda hf, r, cr: (r, 0))
        shape = (2 * half, cols)
    return pl.pallas_call(
        body, name=name,
        grid_spec=pltpu.PrefetchScalarGridSpec(num_scalar_prefetch=1, grid=(2, nr),
                                               in_specs=[full, part, part, full, full], out_specs=[full] * 4),
        out_shape=[jax.ShapeDtypeStruct(shape, F32)] * 4, compiler_params=_arb(2),
    )(core, w, mine, theirs, m, v)


ANY = pl.BlockSpec(memory_space=pl.ANY)


def _mesh_pos():
    return lax.axis_index("x"), lax.axis_index("y"), lax.axis_index("c")


def _other_chips(x, y):
    return [(1 - x, y), (x, 1 - y), (1 - x, 1 - y)]


def _forward_to_sibling(gathered):
    n = len(gathered)

    def body(*refs):
        in_refs, out_refs = refs[:n], refs[n:2 * n]
        send_sems, recv_sems = refs[2 * n:]
        x, y, c = _mesh_pos()
        cps = []
        for i in range(n):
            for k, chip in enumerate(_other_chips(x, y)):
                pk = 2 * chip[0] + chip[1]
                sems = dict(send_sem=send_sems.at[3 * i + k], recv_sem=recv_sems.at[3 * i + k],
                            device_id=(x, y, 1 - c), device_id_type=MESH)
                sent = pltpu.make_async_remote_copy(src_ref=in_refs[i].at[pk, c], dst_ref=out_refs[i].at[pk, c], **sems)
                sent.start()
                theirs = out_refs[i].at[pk, 1 - c]
                cps.append((sent, pltpu.make_async_remote_copy(src_ref=theirs, dst_ref=theirs, **sems)))
        for sent, arrived in cps:
            sent.wait_send()
            arrived.wait_recv()

    return pl.pallas_call(
        body, name="forward_to_sibling", in_specs=[ANY] * n, out_specs=[ANY] * n,
        out_shape=[jax.ShapeDtypeStruct(a.shape, a.dtype) for a in gathered],
        input_output_aliases={i: i for i in range(n)},
        scratch_shapes=[pltpu.SemaphoreType.DMA((3 * n,)), pltpu.SemaphoreType.DMA((3 * n,))],
    )(*gathered)


def _rs_pair_exchange(name, grads):
    n = len(grads)

    def body(*refs):
        g_refs, o_refs = refs[:n], refs[n:2 * n]
        send_sems, recv_sems = refs[2 * n:]
        x, y, c = _mesh_pos()
        cps = []
        for i in range(n):
            cp = pltpu.make_async_remote_copy(
                src_ref=g_refs[i].at[:, 1 - c], dst_ref=o_refs[i],
                send_sem=send_sems.at[i], recv_sem=recv_sems.at[i], device_id=(x, y, 1 - c), device_id_type=MESH)
            cp.start()
            cps.append(cp)
        for cp in cps:
            cp.wait()

    return pl.pallas_call(
        body, name=name, in_specs=[ANY] * n, out_specs=[ANY] * n,
        out_shape=[jax.ShapeDtypeStruct((4,) + g.shape[2:], F32) for g in grads],
        scratch_shapes=[pltpu.SemaphoreType.DMA((n,)), pltpu.SemaphoreType.DMA((n,))],
    )(*grads)


def _rs_pair_add(name, core, g, recv):
    _, half, cols = recv.shape
    tr = _row_tile(half)
    nr = half // tr

    def body(core_ref, g_ref, r_ref, o_ref):
        o_ref[...] = (g_ref[...] + r_ref[...]).astype(BF16)

    return pl.pallas_call(
        body, name=name,
        grid_spec=pltpu.PrefetchScalarGridSpec(
            num_scalar_prefetch=1, grid=(4, nr),
            in_specs=[pl.BlockSpec((None, None, tr, cols), lambda q, r, cr: (q, cr[0], r, 0)),
                      pl.BlockSpec((None, tr, cols), lambda q, r, cr: (q, r, 0))],
            out_specs=pl.BlockSpec((None, tr, cols), lambda q, r, cr: (q, r, 0))),
        out_shape=jax.ShapeDtypeStruct((4, half, cols), BF16),
        compiler_params=_arb(2),
    )(core, g, recv)


def _rs_final_add(name, chip, pair, recv):
    _, half, cols = pair.shape
    tr = _row_tile(half)

    def body(chip_ref, p_ref, r_ref, o_ref):
        o_ref[...] = ((p_ref[...].astype(F32) + r_ref[0].astype(F32)) + r_ref[1].astype(F32)) + r_ref[2].astype(F32)

    return pl.pallas_call(
        body, name=name,
        grid_spec=pltpu.PrefetchScalarGridSpec(
            num_scalar_prefetch=1, grid=(half // tr,),
            in_specs=[pl.BlockSpec((None, tr, cols), lambda r, ch: (ch[0], r, 0)),
                      pl.BlockSpec((3, tr, cols), lambda r, ch: (0, r, 0))],
            out_specs=pl.BlockSpec((tr, cols), lambda r, ch: (r, 0))),
        out_shape=jax.ShapeDtypeStruct((half, cols), F32),
        compiler_params=_arb(1),
    )(chip, pair, recv)


def _rs_pair_share(halves):
    n = len(halves)

    def body(*refs):
        h_refs, o_refs = refs[:n], refs[n:2 * n]
        send_sems, recv_sems = refs[2 * n:]
        x, y, c = _mesh_pos()
        cps = []
        for i in range(n):
            cp = pltpu.make_async_remote_copy(src_ref=h_refs[i], dst_ref=o_refs[i], send_sem=send_sems.at[i],
                                              recv_sem=recv_sems.at[i], device_id=(x, y, 1 - c), device_id_type=MESH)
            cp.start()
            cps.append(cp)
        for cp in cps:
            cp.wait()

    return pl.pallas_call(
        body, name="rs_pair_share", in_specs=[ANY] * n, out_specs=[ANY] * n,
        out_shape=[jax.ShapeDtypeStruct(h.shape, F32) for h in halves],
        scratch_shapes=[pltpu.SemaphoreType.DMA((n,)), pltpu.SemaphoreType.DMA((n,))],
    )(*halves)


HBM = pl.BlockSpec(memory_space=pltpu.HBM)
SEM = pl.BlockSpec(memory_space=pltpu.SEMAPHORE)


class _SplitExchange:
    def __init__(self, name, srcs, land_shapes, src_of, dst_of, arrive_of, to_sibling=False):
        self.name, self.srcs, self.land_shapes = name, list(srcs), list(land_shapes)
        self.src_of, self.dst_of, self.arrive_of = src_of, dst_of, arrive_of
        self.to_sibling = to_sibling
        self.fan = 1 if to_sibling else 3

    def _copies(self, src_refs, land_refs, send_sems, recv_sems):
        x, y, c = _mesh_pos()
        p = 2 * x + y
        if self.to_sibling:
            peers = [((x, y, 1 - c), 1 - c)]
        else:
            peers = [((*chip, c), 2 * chip[0] + chip[1]) for chip in _other_chips(x, y)]
        out = []
        for i, (src, land) in enumerate(zip(src_refs, land_refs)):
            for k, (peer, pk) in enumerate(peers):
                sems = dict(send_sem=send_sems.at[self.fan * i + k], recv_sem=recv_sems.at[self.fan * i + k],
                            device_id=peer, device_id_type=MESH)
                sent = pltpu.make_async_remote_copy(src_ref=self.src_of(src, k, p, pk, c),
                                                    dst_ref=self.dst_of(land, k, p, pk, c), **sems)
                here = self.arrive_of(land, k, p, pk, c)
                out.append((sent, pltpu.make_async_remote_copy(src_ref=here, dst_ref=here, **sems)))
        return out

    def start(self, after=None):
        n = len(self.srcs)
        n_in = 2 * n + (after is not None)

        def body(*refs):
            for sent, _ in self._copies(refs[:n], refs[n:2 * n], refs[n_in], refs[n_in + 1]):
                sent.start()
            refs[-1][...] = jnp.zeros((8, LANES), F32)

        lands = [lax.empty(shape, src.dtype) for shape, src in zip(self.land_shapes, self.srcs)]
        operands = [pltpu.with_memory_space_constraint(a, pltpu.HBM) for a in self.srcs + lands]
        outs = pl.pallas_call(
            body, name=self.name + "_start",
            out_shape=(pltpu.SemaphoreType.DMA((self.fan * n,)), pltpu.SemaphoreType.DMA((self.fan * n,)),
                       *[pltpu.HBM(a.shape, a.dtype) for a in operands], jax.ShapeDtypeStruct((8, LANES), F32)),
            in_specs=[HBM] * (2 * n) + [ANY] * (after is not None),
            out_specs=(SEM, SEM, *[HBM] * (2 * n), pl.BlockSpec(memory_space=pltpu.VMEM)),
            input_output_aliases={j: 2 + j for j in range(2 * n)},
            compiler_params=pltpu.CompilerParams(has_side_effects=pltpu.SideEffectType.DATAFLOW_SIDE_EFFECTING),
        )(*operands, *([after] if after is not None else []))
        self._sems, self._thru = outs[:2], list(outs[2:2 + 2 * n])
        return outs[-1]

    def wait(self, after):
        n = len(self.srcs)

        def body(*refs):
            for sent, arrived in self._copies(refs[:n], refs[n:2 * n], refs[2 * n], refs[2 * n + 1]):
                sent.wait_send()
                arrived.wait_recv()

        after = list(after) if isinstance(after, (list, tuple)) else [after]
        outs = pl.pallas_call(
            body, name=self.name + "_wait",
            out_shape=tuple(pltpu.HBM(a.shape, a.dtype) for a in self._thru),
            in_specs=[HBM] * (2 * n) + [SEM, SEM] + [ANY] * len(after), out_specs=tuple([HBM] * (2 * n)),
            input_output_aliases={j: j for j in range(2 * n)},
            compiler_params=pltpu.CompilerParams(has_side_effects=pltpu.SideEffectType.DATAFLOW_SIDE_EFFECTING),
        )(*self._thru, *self._sems, *after)
        return list(outs[:n]), list(outs[n:])


def _small_allreduce(parts):
    n = len(parts)

    def body(*refs):
        in_refs, out_refs, gather_refs = refs[:n], refs[n:2 * n], refs[2 * n:3 * n]
        send_sems, recv_sems = refs[3 * n:]
        x, y, c = _mesh_pos()
        me = 4 * x + 2 * y + c
        cps = []
        for i in range(n):
            gather_refs[i][me] = in_refs[i][...]
            for j in range(1, 8):
                peer = (x ^ (j >> 2), y ^ ((j >> 1) & 1), c ^ (j & 1))
                cp = pltpu.make_async_remote_copy(
                    src_ref=in_refs[i], dst_ref=gather_refs[i].at[me], send_sem=send_sems.at[7 * i + j - 1],
                    recv_sem=recv_sems.at[7 * i + j - 1], device_id=peer, device_id_type=MESH)
                cp.start()
                cps.append(cp)
        for i in range(n):
            for j in range(1, 8):
                peer_id = 4 * (x ^ (j >> 2)) + 2 * (y ^ ((j >> 1) & 1)) + (c ^ (j & 1))
                slot = gather_refs[i].at[peer_id]
                pltpu.make_async_remote_copy(src_ref=slot, dst_ref=slot, send_sem=send_sems.at[7 * i + j - 1],
                                             recv_sem=recv_sems.at[7 * i + j - 1], device_id=(x, y, c),
                                             device_id_type=MESH).wait_recv()
        for cp in cps:
            cp.wait_send()
        for i in range(n):
            tot = gather_refs[i][0]
            for d in range(1, 8):
                tot = tot + gather_refs[i][d]
            out_refs[i][...] = tot

    vmem = pl.BlockSpec(memory_space=pltpu.VMEM)
    return pl.pallas_call(
        body, name="small_allreduce", in_specs=[vmem] * n, out_specs=[vmem] * n,
        out_shape=[jax.ShapeDtypeStruct(p.shape, F32) for p in parts],
        scratch_shapes=[pltpu.VMEM((8,) + p.shape, F32) for p in parts]
        + [pltpu.SemaphoreType.DMA((7 * n,)), pltpu.SemaphoreType.DMA((7 * n,))],
    )(*parts)


def _adamw_small(ws, gs, ms, vs):
    n = len(ws)

    def body(*refs):
        for i in range(n):
            w_ref, g_ref, m_ref, v_ref = refs[i], refs[n + i], refs[2 * n + i], refs[3 * n + i]
            d_ref, nm_ref, nv_ref = refs[4 * n + i], refs[5 * n + i], refs[6 * n + i]
            d_ref[...], nm_ref[...], nv_ref[...] = _adamw_math(w_ref[...], g_ref[...], m_ref[...], v_ref[...])

    vmem = pl.BlockSpec(memory_space=pltpu.VMEM)
    shapes = [jax.ShapeDtypeStruct(w.shape, F32) for w in ws]
    outs = pl.pallas_call(body, name="adamw_small", in_specs=[vmem] * (4 * n), out_specs=[vmem] * (3 * n),
                          out_shape=shapes * 3)(*ws, *gs, *ms, *vs)
    return outs[:n], outs[n:2 * n], outs[2 * n:]


W_IN_PIECES = ((0, 256, P_QLAT), (256, 384, P_CKV), (384, 448, P_KR), (448, 1472, P_QB), (1472, 1728, P_KB),
               (1728, 1984, P_VB), (1984, 3008, P_GA), (3008, 4032, P_GB))
W_IN_SHARD = 1008


def _w_in_from_shards(shards):
    cols = []
    for lo, hi, _ in sorted(W_IN_PIECES, key=lambda piece: piece[2]):
        for q in range(4):
            a, b = max(lo, q * W_IN_SHARD), min(hi, (q + 1) * W_IN_SHARD)
            if a < b:
                cols.append(shards[q][:, a - q * W_IN_SHARD:b - q * W_IN_SHARD])
    cols.append(jnp.zeros((shards.shape[1], W_IN_PAD - 4 * W_IN_SHARD), shards.dtype))
    return jnp.concatenate(cols, axis=1)


def _w_in_to_shards(p):
    shards = []
    for q in range(4):
        cols = []
        for lo, hi, at in W_IN_PIECES:
            a, b = max(lo, q * W_IN_SHARD), min(hi, (q + 1) * W_IN_SHARD)
            if a < b:
                cols.append(p[:, at + a - lo:at + b - lo])
        shards.append(jnp.concatenate(cols, axis=1))
    return jnp.stack(shards)


def _col_shards(w):
    r, c4 = w.shape
    return w.reshape(r, 4, c4 // 4).transpose(1, 0, 2)


def _local_step(x, positions, target, norm1_g, first_weights, q_a_norm_g, kv_a_norm_g, rel_bias, sinks,
                late_weights, norm2_g, conv_b, final_norm_g, early_grads=None, last_grads=None):
    s = x.shape[0]
    half = QK_ROPE // 2
    inv_freq = jnp.asarray(np.float32(ROPE_THETA) ** (-np.arange(half, dtype=np.float32) / np.float32(half)))
    ang = positions.astype(F32)[:, None] * inv_freq[None, :]
    cos, sin = jnp.cos(ang), jnp.sin(ang)
    z64 = jnp.zeros((s, 64), F32)
    cos_t = jnp.concatenate([cos, cos, z64], axis=1)
    sin_t = jnp.concatenate([-sin, sin, z64], axis=1)
    bucket = _t5_bucket_table()
    sinks1 = sinks.reshape(H_B)

    h1, rstd1 = _rmsnorm_fwd("norm1_fwd", x, norm1_g, D_MODEL, 0)
    bias = _win_bias(bucket, rel_bias)
    w_in_p, wq, wkv = first_weights([h1, bias, cos_t, sin_t])
    proj, proj_b = _matmul("proj", h1, w_in_p, out_shape=(s, W_IN_PAD), out_dtype=F32, grid=(s // MM_ROWS, W_IN_PAD // 1024, 1),
                           a_spec=_bs((MM_ROWS, D_MODEL), lambda i, j, k: (i, 0)), b_spec=_bs((D_MODEL, 1024), lambda i, j, k: (0, j)),
                           o_spec=_bs((MM_ROWS, 1024), lambda i, j, k: (i, j)), contract=NN, bf16_copy=True)
    qn, cn, rstd_q, rstd_c = _lat_norms(proj, q_a_norm_g, kv_a_norm_g)
    q = _q_heads(qn, wq, cos_t, sin_t)
    k, v = _kv_heads(cn, wkv, proj, cos_t, sin_t)
    o_a, lse_a = _mla_fwd(q, k, v)

    o_b, lse_b = _win_fwd(proj_b, bias, sinks1)

    w_out, w_up, w_down, conv_w = late_weights([o_a, o_b])
    row512 = lambda w: _bs((MM_ROWS, w), lambda i, j, k: (i, 0))
    whole = lambda r, c: _bs((r, c), lambda i, j, k: (0, 0))
    mixed, x1, h2, rstd2 = _mix_out_norm(proj, o_a, o_b, w_out, x, norm2_g)
    u = _matmul("ffn_up", h2, w_up, out_shape=(s, 2 * D_FF), out_dtype=F32, grid=(s // MM_ROWS, 4, 1),
                a_spec=_bs((MM_ROWS, D_MODEL), lambda i, j, k: (i, 0)), b_spec=_bs((D_MODEL, D_FF // 2), lambda i, j, k: (0, j)),
                o_spec=_bs((MM_ROWS, D_FF // 2), lambda i, j, k: (i, j)), contract=NN)
    f = _convffn_fwd(u, conv_w, conv_b)
    loss, dx2, d_final_g, dx2_b = _ffn_down_loss(f, w_down, x1, target, final_norm_g.reshape(1, D_MODEL))
    tk = min(s, DW_ROWS)

    df = _matmul("ffn_down_dx", dx2_b, w_down, out_shape=(s, D_FF), out_dtype=F32, grid=(s // MM_ROWS, 2, 1),
                 a_spec=row512(D_MODEL), b_spec=_bs((D_FF // 2, D_MODEL), lambda i, j, k: (j, 0)),
                 o_spec=_bs((MM_ROWS, D_FF // 2), lambda i, j, k: (i, j)), contract=NT)
    d_w_down = _matmul("ffn_down_dw", f, dx2_b, out_shape=(D_FF, D_MODEL), out_dtype=F32, grid=(2, 1, s // tk),
                       a_spec=_bs((tk, D_FF // 2), lambda i, j, k: (k, i)), b_spec=_bs((tk, D_MODEL), lambda i, j, k: (k, 0)),
                       o_spec=_bs((D_FF // 2, D_MODEL), lambda i, j, k: (i, 0)), contract=TN)
    du, d_conv_w2, d_conv_b2 = _convffn_bwd(u, conv_w, conv_b, df)
    kc = D_FF // 2
    dx1, d_norm2_g, dx1_b = _matmul_norm_bwd(
        "ffn_up_dx_norm2_bwd", du, w_up, grid=(s // MM_ROWS, 4),
        a_spec=_bs((None, MM_ROWS, kc), lambda i, k: (k // 2, i, k % 2)), b_spec=_bs((D_MODEL, kc), lambda i, k: (0, k)),
        x=x1, rstd=rstd2, g=norm2_g, res=dx2, bf16_copy=True)
    d_w_up = _matmul("ffn_up_dw", h2, du, out_shape=(D_MODEL, 2 * D_FF), out_dtype=F32, grid=(1, 4, s // tk),
                     a_spec=_bs((tk, D_MODEL), lambda i, j, k: (k, 0)),
                     b_spec=_bs((None, tk, kc), lambda i, j, k: (j // 2, k, j % 2)),
                     o_spec=_bs((D_MODEL, kc), lambda i, j, k: (0, j)), contract=TN)

    d_w_out = _matmul("attn_out_dw", mixed, dx1_b, out_shape=(D_MODEL, D_MODEL), out_dtype=F32, grid=(1, 1, s // tk),
                      a_spec=_bs((tk, D_MODEL), lambda i, j, k: (k, 0)), b_spec=_bs((tk, D_MODEL), lambda i, j, k: (k, 0)),
                      o_spec=whole(D_MODEL, D_MODEL), contract=TN)
    token, early_grads_on = early_grads(d_w_out, d_w_up, d_w_down) if early_grads is not None else (None, None)
    do_a, do_b, d_ga, d_gb = _mix_out_bwd(dx1_b, w_out, proj, o_a, o_b, after=token)
    if early_grads_on is not None:
        sinks1 = sinks1 + early_grads_on(d_ga)[0, :H_B]

    d_qb, dk_pad, dv_pad, dbias, dsink_rows = _win_bwd(proj_b, bias, sinks1, do_b, lse_b)
    wp = _win_param_grads(bucket, dbias, dsink_rows)[:, 0, :]
    d_rel_bias = wp[:, :NUM_BUCKETS].T
    d_sinks = wp[:, NUM_BUCKETS].reshape(1, H_B)
    d_kb = dk_pad[WINDOW:WINDOW + s].astype(BF16)
    d_vb = dv_pad[WINDOW:WINDOW + s].astype(BF16)

    dq_pre, dkv_pre, dkr = _mla_bwd(q, k, v, do_a, o_a, lse_a, cos_t, sin_t)
    d_kr = _mla_key_rope_grad(dkr, cos_t, sin_t)
    th = min(s, HEAD_ROWS)
    hgrid = (s // th, 1, H_A)
    hblock = _bs((None, th, HEAD_PAD), lambda i, j, k: (k, i, 0))
    hrows = lambda w: _bs((th, w), lambda i, j, k: (i, 0))
    dqn = _matmul("q_up_dx", dq_pre, wq, out_shape=(s, Q_LORA), out_dtype=F32, grid=hgrid, a_spec=hblock,
                  b_spec=_bs((None, Q_LORA, HEAD_PAD), lambda i, j, k: (k, 0, 0)), o_spec=hrows(Q_LORA), contract=NT)
    dcn = _matmul("kv_up_dx", dkv_pre, wkv, out_shape=(s, KV_LORA), out_dtype=F32, grid=hgrid, a_spec=hblock,
                  b_spec=_bs((None, KV_LORA, HEAD_PAD), lambda i, j, k: (k, 0, 0)), o_spec=hrows(KV_LORA), contract=NT)
    wgrid = (H_A, 1, s // th)
    d_wq = _matmul("q_up_dw", qn, dq_pre, out_shape=(H_A, Q_LORA, HEAD_PAD), out_dtype=F32, grid=wgrid,
                   a_spec=_bs((th, Q_LORA), lambda i, j, k: (k, 0)), b_spec=_bs((None, th, HEAD_PAD), lambda i, j, k: (i, k, 0)),
                   o_spec=_bs((None, Q_LORA, HEAD_PAD), lambda i, j, k: (i, 0, 0)), contract=TN)
    d_wkv = _matmul("kv_up_dw", cn, dkv_pre, out_shape=(H_A, KV_LORA, HEAD_PAD), out_dtype=F32, grid=wgrid,
                    a_spec=_bs((th, KV_LORA), lambda i, j, k: (k, 0)), b_spec=_bs((None, th, HEAD_PAD), lambda i, j, k: (i, k, 0)),
                    o_spec=_bs((None, KV_LORA, HEAD_PAD), lambda i, j, k: (i, 0, 0)), contract=TN)
    d_qlat, d_gq = _rmsnorm_bwd("q_norm_bwd", dqn, proj, rstd_q, q_a_norm_g, Q_LORA, P_QLAT // Q_LORA, BF16)
    d_ckv, d_gkv = _rmsnorm_bwd("kv_norm_bwd", dcn, proj, rstd_c, kv_a_norm_g, KV_LORA, P_CKV // KV_LORA, BF16)

    dproj = jnp.concatenate([d_qb, d_ga, d_gb, d_qlat, d_kb, d_vb, d_ckv, d_kr], axis=1)
    d_w_in_p = _matmul("proj_dw", h1, dproj, out_shape=(D_MODEL, W_IN_PAD), out_dtype=F32, grid=(1, W_IN_PAD // 1024, s // tk),
                       a_spec=_bs((tk, D_MODEL), lambda i, j, k: (k, 0)), b_spec=_bs((tk, 1024), lambda i, j, k: (k, j)),
                       o_spec=_bs((D_MODEL, 1024), lambda i, j, k: (0, j)), contract=TN)
    token = last_grads(d_w_in_p, d_wq, d_wkv) if last_grads is not None else None
    dx, d_norm1_g = _matmul_norm_bwd(
        "proj_dx_norm1_bwd", dproj, w_in_p, grid=(s // MM_ROWS, W_IN_PAD // 1024),
        a_spec=_bs((MM_ROWS, 1024), lambda i, k: (i, k)), b_spec=_bs((D_MODEL, 1024), lambda i, k: (0, k)),
        x=x, rstd=rstd1, g=norm1_g, res=dx1, after=token)

    grads = dict(
        norm1_g=d_norm1_g, w_in_p=d_w_in_p, q_a_norm_g=d_gq, wq=d_wq, kv_a_norm_g=d_gkv, wkv=d_wkv,
        rel_bias=d_rel_bias, sinks=d_sinks, w_out=d_w_out, norm2_g=d_norm2_g, w_up=d_w_up,
        conv_w=jnp.concatenate([d_conv_w2[0], d_conv_w2[1]], axis=1),
        conv_b=jnp.concatenate([d_conv_b2[0], d_conv_b2[1]], axis=1),
        w_down=d_w_down, final_norm_g=d_final_g.reshape(D_MODEL))
    return loss, dx, grads


HEADS_PER_SHARD = H_A // 4


def _head_cols(h, width):
    return slice((h % HEADS_PER_SHARD) * width, (h % HEADS_PER_SHARD + 1) * width)


def _wq_heads(shards):
    per = QK_NOPE + QK_ROPE
    w = jnp.stack([shards[h // HEADS_PER_SHARD][:, _head_cols(h, per)] for h in range(H_A)])
    return jnp.pad(w, ((0, 0), (0, 0), (0, HEAD_PAD - per)))


def _wq_shards(d_wq):
    per = QK_NOPE + QK_ROPE
    return jnp.stack([jnp.concatenate([d_wq[h][:, :per] for h in range(q * HEADS_PER_SHARD, (q + 1) * HEADS_PER_SHARD)],
                                      axis=1) for q in range(4)])


def _wkv_heads(shards):
    return jnp.stack([shards[h // HEADS_PER_SHARD][:, _head_cols(h, QK_NOPE + V_DIM)] for h in range(H_A)])


def _wkv_shards(d_wkv):
    return jnp.stack([jnp.concatenate([d_wkv[h] for h in range(q * HEADS_PER_SHARD, (q + 1) * HEADS_PER_SHARD)], axis=1)
                      for q in range(4)])


SMALL = ("norm1_g", "q_a_norm_g", "kv_a_norm_g", "rel_bias", "sinks", "norm2_g", "conv_b", "final_norm_g")
FIRST = ("w_in", "w_q_b", "w_kv_b")
LATER = ("w_out", "w_up", "w_down")
BIG = FIRST + LATER


def kernel(x, positions, norm1_g, w_in, q_a_norm_g, w_q_b, kv_a_norm_g, w_kv_b, rel_bias, sinks, w_out, norm2_g, w_up, conv_w, conv_b, w_down, final_norm_g, loss_target, m_norm1_g, m_w_in, m_q_a_norm_g, m_w_q_b, m_kv_a_norm_g, m_w_kv_b, m_rel_bias, m_sinks, m_w_out, m_norm2_g, m_w_up, m_conv_w, m_conv_b, m_w_down, m_final_norm_g, v_norm1_g, v_w_in, v_q_a_norm_g, v_w_q_b, v_kv_a_norm_g, v_w_kv_b, v_rel_bias, v_sinks, v_w_out, v_norm2_g, v_w_up, v_conv_w, v_conv_b, v_w_down, v_final_norm_g):
    weights = dict(norm1_g=norm1_g, w_in=w_in, q_a_norm_g=q_a_norm_g, w_q_b=w_q_b, kv_a_norm_g=kv_a_norm_g,
                   w_kv_b=w_kv_b, rel_bias=rel_bias, sinks=sinks, w_out=w_out, norm2_g=norm2_g, w_up=w_up,
                   conv_w=conv_w, conv_b=conv_b, w_down=w_down, final_norm_g=final_norm_g)
    mom_m = dict(norm1_g=m_norm1_g, w_in=m_w_in, q_a_norm_g=m_q_a_norm_g, w_q_b=m_w_q_b, kv_a_norm_g=m_kv_a_norm_g,
                 w_kv_b=m_w_kv_b, rel_bias=m_rel_bias, sinks=m_sinks, w_out=m_w_out, norm2_g=m_norm2_g, w_up=m_w_up,
                 conv_w=m_conv_w, conv_b=m_conv_b, w_down=m_w_down, final_norm_g=m_final_norm_g)
    mom_v = dict(norm1_g=v_norm1_g, w_in=v_w_in, q_a_norm_g=v_q_a_norm_g, w_q_b=v_w_q_b, kv_a_norm_g=v_kv_a_norm_g,
                 w_kv_b=v_w_kv_b, rel_bias=v_rel_bias, sinks=v_sinks, w_out=v_w_out, norm2_g=v_norm2_g, w_up=v_w_up,
                 conv_w=v_conv_w, conv_b=v_conv_b, w_down=v_w_down, final_norm_g=v_final_norm_g)
    shard2d = {n: weights[n][0] for n in BIG}
    conv_w_shard = conv_w[0]
    xi, yi, ci = lax.axis_index("x"), lax.axis_index("y"), lax.axis_index("c")
    chip = (2 * xi + yi).astype(jnp.int32)

    core = ci.astype(jnp.int32).reshape(1)
    chip1 = chip.reshape(1)
    cat_cols = lambda a: jnp.concatenate([a[0], a[1], a[2], a[3]], axis=1)
    own_slot = lambda a, own: lax.dynamic_update_index_in_dim(a, own, chip, 0)
    halved = lambda a: a.reshape((2, a.shape[0] // 2) + a.shape[1:])
    quartered = lambda a: a.reshape(4, 2, a.shape[1] // 2, a.shape[2])

    first = [halved(shard2d[n].astype(BF16)) for n in FIRST]
    gather1 = _SplitExchange("gather_first", first, [(4,) + a.shape for a in first],
                             src_of=lambda ref, k, p, pk, c: ref.at[c], dst_of=lambda ref, k, p, pk, c: ref.at[p, c],
                             arrive_of=lambda ref, k, p, pk, c: ref.at[pk, c])
    token1 = gather1.start()

    def first_weights(after):
        own, landed = gather1.wait(after)
        gathered = [own_slot(a, mine) for a, mine in zip(_forward_to_sibling(landed), own)]
        g = {n: a.reshape((4,) + shard2d[n].shape) for n, a in zip(FIRST, gathered)}
        return _w_in_from_shards(g["w_in"]), _wq_heads(g["w_q_b"]), _wkv_heads(g["w_kv_b"])

    later = [shard2d[n].astype(BF16) for n in LATER] + [conv_w_shard]
    gather2 = _SplitExchange("gather_later", later, [(4,) + a.shape for a in later],
                             src_of=lambda ref, k, p, pk, c: ref, dst_of=lambda ref, k, p, pk, c: ref.at[p],
                             arrive_of=lambda ref, k, p, pk, c: ref.at[pk])
    norm1_g_in = norm1_g + gather2.start(after=token1)[:1, :1]

    def late_weights(after):
        w_out_g, w_up_g, w_down_g, conv_w_g = [own_slot(a, mine) for mine, a in zip(*gather2.wait(after))]
        return w_out_g.reshape(D_MODEL, D_MODEL), cat_cols(w_up_g), w_down_g.reshape(D_FF, D_MODEL), cat_cols(conv_w_g)

    early = {}

    def early_grads(d_w_out, d_w_up, d_w_down):
        grads = [quartered(d_w_out.reshape(4, D_MODEL // 4, D_MODEL)), quartered(_col_shards(d_w_up)),
                 quartered(d_w_down.reshape(4, D_FF // 4, D_MODEL))]
        swap = _SplitExchange("rs_pair_exchange_early", grads, [(4,) + a.shape[2:] for a in grads], to_sibling=True,
                              src_of=lambda ref, k, p, pk, c: ref.at[:, pk], dst_of=lambda ref, k, p, pk, c: ref,
                              arrive_of=lambda ref, k, p, pk, c: ref)

        def on(after):
            kept, recv = swap.wait(after)
            early["pairs"] = [_rs_pair_add(f"rs_pair_add_{n}", core, gfull, r) for n, gfull, r in zip(LATER, kept, recv)]
            early["ici"] = _SplitExchange("rs_ici_early", early["pairs"], [(3,) + a.shape[1:] for a in early["pairs"]],
                                          src_of=lambda ref, k, p, pk, c: ref.at[pk], dst_of=lambda ref, k, p, pk, c: ref.at[k],
                                          arrive_of=lambda ref, k, p, pk, c: ref.at[k])
            return early["ici"].start()

        return swap.start(), on

    last = {}

    def last_grads(d_w_in_p, d_wq, d_wkv):
        grads = [quartered(_w_in_to_shards(d_w_in_p)), quartered(_wq_shards(d_wq)), quartered(_wkv_shards(d_wkv))]
        recv = _rs_pair_exchange("rs_pair_exchange_last", grads)
        last["pairs"] = [_rs_pair_add(f"rs_pair_add_{n}", core, gfull, r) for n, gfull, r in zip(FIRST, grads, recv)]
        last["ici"] = _SplitExchange("rs_ici_last", last["pairs"], [(3,) + a.shape[1:] for a in last["pairs"]],
                                     src_of=lambda ref, k, p, pk, c: ref.at[pk], dst_of=lambda ref, k, p, pk, c: ref.at[k],
                                     arrive_of=lambda ref, k, p, pk, c: ref.at[k])
        return last["ici"].start()

    loss, dx, gr = _local_step(x[0], positions, loss_target[0], norm1_g_in, first_weights, q_a_norm_g, kv_a_norm_g,
                               rel_bias, sinks, late_weights, norm2_g, conv_b, final_norm_g, early_grads, last_grads)

    last_pairs, last_recv = last["ici"].wait(dx)
    early_pairs, early_recv = early["ici"].wait(dx)
    pairs, recv2 = last_pairs + early_pairs, last_recv + early_recv
    halves = [_rs_final_add(f"rs_final_add_{n}", chip1, pr, r) for n, pr, r in zip(FIRST + LATER, pairs, recv2)]
    sibling_halves = _rs_pair_share(halves)

    as_rows = lambda a: a.reshape((-1, a.shape[-1]))
    summed = _small_allreduce([as_rows(gr[n]) for n in SMALL] + [gr["conv_w"], loss])
    small_g = dict(zip(SMALL, summed[:len(SMALL)]))
    conv_w_g = lax.dynamic_slice_in_dim(summed[len(SMALL)], chip * (2 * D_FF // 4), 2 * D_FF // 4, axis=1)
    loss_out = summed[-1].reshape(())

    out_g, out_d, out_m, out_v = {}, {}, {}, {}
    for n, mine, theirs in zip(FIRST + LATER, halves, sibling_halves):
        if shard2d[n].shape[1] % LANES:
            res = _adamw_halves(f"adamw_{n}", core, shard2d[n].T, mine.T, theirs.T, mom_m[n][0].T, mom_v[n][0].T,
                                transposed=True)
            gsh, d, nm, nv = [a.T for a in res]
        else:
            gsh, d, nm, nv = _adamw_halves(f"adamw_{n}", core, shard2d[n], mine, theirs, mom_m[n][0], mom_v[n][0])
        out_g[n], out_d[n], out_m[n], out_v[n] = gsh[None], d[None], nm[None], nv[None]
    names = SMALL + ("conv_w",)
    sg = [small_g[n] for n in SMALL] + [conv_w_g]
    ds, nms, nvs = _adamw_small([as_rows(weights[n]) for n in names], sg, [as_rows(mom_m[n]) for n in names],
                                [as_rows(mom_v[n]) for n in names])
    for n, gg, dd, mm, vv in zip(names, sg, ds, nms, nvs):
        shp = weights[n].shape
        out_g[n], out_d[n], out_m[n], out_v[n] = gg.reshape(shp), dd.reshape(shp), mm.reshape(shp), vv.reshape(shp)

    order = ("norm1_g", "w_in", "q_a_norm_g", "w_q_b", "kv_a_norm_g", "w_kv_b", "rel_bias", "sinks", "w_out",
             "norm2_g", "w_up", "conv_w", "conv_b", "w_down", "final_norm_g")
    return (loss_out, dx[None], *[out_g[n] for n in order], *[out_d[n] for n in order],
            *[out_m[n] for n in order], *[out_v[n] for n in order])
```
